```python
import math
import jax, jax.numpy as jnp
from jax import lax
import numpy as np

D_MODEL = 1024
BATCH = 8
SEQ = 16384
DEPTH = 1

MLA_HEADS = 8
QK_NOPE_DIM = 128
QK_ROPE_DIM = 64
V_HEAD_DIM = 128
Q_LORA_RANK = 384
KV_LORA_RANK = 256
MLA_WIDTH = MLA_HEADS * V_HEAD_DIM
QK_HEAD_DIM = QK_NOPE_DIM + QK_ROPE_DIM
ROPE_THETA = 10000.0
ATTN_BLOCK = 128

SSM_HEAD_DIM = 64
SSM_HEADS = 16
SSM_WIDTH = SSM_HEADS * SSM_HEAD_DIM
SSM_GROUPS = 2
SSM_STATE = 128
CONV_WIDTH = 4
CHUNK = 128
CONV_CH = SSM_WIDTH + 2 * SSM_GROUPS * SSM_STATE

MIX_WIDTH = MLA_WIDTH + SSM_WIDTH

IN_SPLITS = (Q_LORA_RANK, KV_LORA_RANK + QK_ROPE_DIM, MLA_WIDTH, CONV_CH, SSM_HEADS, SSM_WIDTH)
IN_WIDTH = sum(IN_SPLITS)

DEEPNORM_ALPHA = (2.0 * DEPTH) ** 0.25
DEEPNORM_BETA = (8.0 * DEPTH) ** -0.25
RMS_EPS = 1e-6
LN_EPS = 1e-5

kernel_name = "hybrid_mla_ssd_parallel_heads_deepnorm_adaln"


def split_last(x, sizes):
    out, start = [], 0
    for s in sizes:
        out.append(x[..., start:start + s])
        start += s
    return out


def rms_norm(x, g):
    xf = x.astype(jnp.float32)
    y = xf * lax.rsqrt(jnp.mean(xf * xf, axis=-1, keepdims=True) + RMS_EPS)
    return (y * g.astype(jnp.float32)).astype(x.dtype)


def layer_norm(x, g, b):
    xf = x.astype(jnp.float32)
    mu = jnp.mean(xf, axis=-1, keepdims=True)
    var = jnp.mean(jnp.square(xf - mu), axis=-1, keepdims=True)
    y = (xf - mu) * lax.rsqrt(var + LN_EPS)
    return (y * g.astype(jnp.float32) + b.astype(jnp.float32)).astype(x.dtype)


def apply_rope(x, positions):
    half = QK_ROPE_DIM // 2
    inv_freq = 1.0 / (ROPE_THETA ** (jnp.arange(half, dtype=jnp.float32) / half))
    ang = positions.astype(jnp.float32)[..., None] * inv_freq
    cos = jnp.cos(ang)[:, :, None, :]
    sin = jnp.sin(ang)[:, :, None, :]
    xf = x.astype(jnp.float32)
    x1, x2 = xf[..., :half], xf[..., half:]
    out = jnp.concatenate([x1 * cos - x2 * sin, x2 * cos + x1 * sin], axis=-1)
    return out.astype(x.dtype)


def causal_block_attention(q, k, v, scale):
    b, s, h, dq = q.shape
    nb = s // ATTN_BLOCK
    qb = q.reshape(b, nb, ATTN_BLOCK, h, dq).transpose(1, 0, 2, 3, 4)
    key_pos = jnp.arange(s)

    def one_block(args):
        q_blk, i = args
        sc = jnp.einsum('bqhd,bkhd->bhqk', q_blk, k,
                        preferred_element_type=jnp.float32) * scale
        q_pos = i * ATTN_BLOCK + jnp.arange(ATTN_BLOCK)
        mask = key_pos[None, :] <= q_pos[:, None]
        sc = jnp.where(mask[None, None], sc, -jnp.inf)
        p = jax.nn.softmax(sc, axis=-1).astype(v.dtype)
        return jnp.einsum('bhqk,bkhd->bqhd', p, v)

    out = lax.map(one_block, (qb, jnp.arange(nb)))
    return out.transpose(1, 0, 2, 3, 4).reshape(b, s, h, v.shape[-1])


def mla_branch(q_lat, kv_lat, positions, q_norm_g, w_qb, kv_norm_g, w_kvb):
    b, s, _ = q_lat.shape
    q = (rms_norm(q_lat, q_norm_g) @ w_qb).reshape(b, s, MLA_HEADS, QK_HEAD_DIM)
    q_nope, q_rope = q[..., :QK_NOPE_DIM], q[..., QK_NOPE_DIM:]
    c_kv, k_rope = kv_lat[..., :KV_LORA_RANK], kv_lat[..., KV_LORA_RANK:]
    kv = (rms_norm(c_kv, kv_norm_g) @ w_kvb).reshape(b, s, MLA_HEADS, QK_NOPE_DIM + V_HEAD_DIM)
    k_nope, v = kv[..., :QK_NOPE_DIM], kv[..., QK_NOPE_DIM:]
    q_rope = apply_rope(q_rope, positions)
    k_rope = apply_rope(k_rope[:, :, None, :], positions)
    q = jnp.concatenate([q_nope, q_rope], axis=-1)
    k = jnp.concatenate([k_nope, jnp.broadcast_to(k_rope, (b, s, MLA_HEADS, QK_ROPE_DIM))], axis=-1)
    o = causal_block_attention(q, k, v, QK_HEAD_DIM ** -0.5)
    return o.reshape(b, s, MLA_WIDTH)


def causal_depthwise_conv(x, w, bias):
    out = lax.conv_general_dilated(
        x, w[:, None, :], window_strides=(1,), padding=[(CONV_WIDTH - 1, 0)],
        dimension_numbers=('NWC', 'WIO', 'NWC'), feature_group_count=x.shape[-1])
    return out + bias


def ssd_chunked(x, da, bm, cm):
    out_dtype = x.dtype
    b, s, h, p = x.shape
    nc, r = s // CHUNK, h // SSM_GROUPS
    f32 = jnp.float32
    X = x.astype(f32).reshape(b, nc, CHUNK, SSM_GROUPS, r, p)
    A = da.astype(f32).reshape(b, nc, CHUNK, SSM_GROUPS, r)
    Bc = bm.astype(f32).reshape(b, nc, CHUNK, SSM_GROUPS, SSM_STATE)
    Cc = cm.astype(f32).reshape(b, nc, CHUNK, SSM_GROUPS, SSM_STATE)
    a_cum = jnp.cumsum(A, axis=2)
    seg = a_cum[:, :, :, None] - a_cum[:, :, None, :]
    tri = jnp.tril(jnp.ones((CHUNK, CHUNK), dtype=bool))[:, :, None, None]
    decay = jnp.exp(jnp.where(tri, seg, -jnp.inf))
    cb = jnp.einsum('bclgn,bcsgn->bclsg', Cc, Bc)
    y_diag = jnp.einsum('bclsgr,bcsgrp->bclgrp', cb[..., None] * decay, X)
    decay_to_end = jnp.exp(a_cum[:, :, -1:] - a_cum)
    states = jnp.einsum('bclgn,bclgr,bclgrp->bcgrpn', Bc, decay_to_end, X)
    chunk_decay = jnp.exp(a_cum[:, :, -1])

    def step(hc, inp):
        st, dc = inp
        return hc * dc[..., None, None] + st, hc

    h0 = jnp.zeros((b, SSM_GROUPS, r, p, SSM_STATE), f32)
    _, prev = lax.scan(step, h0, (states.transpose(1, 0, 2, 3, 4, 5),
                                   chunk_decay.transpose(1, 0, 2, 3)))
    prev = prev.transpose(1, 0, 2, 3, 4, 5)
    y_off = jnp.einsum('bclgn,bcgrpn,bclgr->bclgrp', Cc, prev, jnp.exp(a_cum))
    return (y_diag + y_off).reshape(b, s, h, p).astype(out_dtype)


def ssd_branch(xbc_raw, dt_raw, z, conv_w, conv_b, dt_bias, a_log, d_skip, ssm_norm_g):
    b, s, _ = xbc_raw.shape
    xbc = jax.nn.silu(causal_depthwise_conv(xbc_raw, conv_w, conv_b))
    xs, bm, cm = split_last(xbc, (SSM_WIDTH, SSM_GROUPS * SSM_STATE, SSM_GROUPS * SSM_STATE))
    xs = xs.reshape(b, s, SSM_HEADS, SSM_HEAD_DIM)
    bm = bm.reshape(b, s, SSM_GROUPS, SSM_STATE)
    cm = cm.reshape(b, s, SSM_GROUPS, SSM_STATE)
    dt = jax.nn.softplus(dt_raw.astype(jnp.float32) + dt_bias.astype(jnp.float32))
    a = -jnp.exp(a_log.astype(jnp.float32))
    y = ssd_chunked(xs * dt[..., None].astype(xs.dtype), dt * a, bm, cm)
    y = y + xs * d_skip[:, None]
    hf = (y.reshape(b, s, SSM_WIDTH).astype(jnp.float32)
          * jax.nn.silu(z.astype(jnp.float32))).reshape(b, s, SSM_GROUPS, -1)
    hf = hf * lax.rsqrt(jnp.mean(hf * hf, axis=-1, keepdims=True) + RMS_EPS)
    return (hf.reshape(b, s, SSM_WIDTH) * ssm_norm_g.astype(jnp.float32)).astype(xbc_raw.dtype)


def hybrid_layer(x, c, positions, w_ada, b_ada, w_in, q_norm_g, w_qb, kv_norm_g, w_kvb,
                 conv_w, conv_b, dt_bias, a_log, d_skip, ssm_norm_g, w_out, ln_g, ln_b):
    mod = c @ w_ada + b_ada
    shift, scale, gate = mod[:, :D_MODEL], mod[:, D_MODEL:2 * D_MODEL], mod[:, 2 * D_MODEL:]
    u = x * (1.0 + scale[:, None, :]) + shift[:, None, :]
    proj = u @ w_in
    q_lat, kv_lat, z_attn, xbc, dt_raw, z_ssm = split_last(proj, IN_SPLITS)
    o_attn = mla_branch(q_lat, kv_lat, positions, q_norm_g, w_qb, kv_norm_g, w_kvb) * jax.nn.silu(z_attn)
    o_ssm = ssd_branch(xbc, dt_raw, z_ssm, conv_w, conv_b, dt_bias, a_log, d_skip, ssm_norm_g)
    mixed = jnp.concatenate([o_attn, o_ssm], axis=-1) @ w_out
    return layer_norm(DEEPNORM_ALPHA * x + gate[:, None, :] * mixed, ln_g, ln_b)


def _fwd_setup_inputs(seed: int = 0) -> dict:
    key = jax.random.key(seed)
    ks = jax.random.split(key, 20)
    f32 = jnp.float32
    n = lambda k, shape, s: jax.random.normal(k, shape, f32) * s
    L = DEPTH
    dt0 = jnp.exp(jax.random.uniform(ks[12], (L, SSM_HEADS), f32, math.log(1e-3), math.log(1e-1)))
    return {
        "x": jax.random.normal(ks[0], (BATCH, SEQ, D_MODEL), f32),
        "c": jax.random.normal(ks[1], (BATCH, D_MODEL), f32),
        "positions": jnp.broadcast_to(jnp.arange(SEQ, dtype=jnp.int32), (BATCH, SEQ)),
        "w_ada": n(ks[2], (L, D_MODEL, 3 * D_MODEL), 0.5 * D_MODEL ** -0.5),
        "b_ada": n(ks[3], (L, 3 * D_MODEL), 0.02),
        "w_in": n(ks[4], (L, D_MODEL, IN_WIDTH), D_MODEL ** -0.5),
        "q_norm_g": 1.0 + n(ks[5], (L, Q_LORA_RANK), 0.05),
        "w_qb": n(ks[6], (L, Q_LORA_RANK, MLA_HEADS * QK_HEAD_DIM), Q_LORA_RANK ** -0.5),
        "kv_norm_g": 1.0 + n(ks[7], (L, KV_LORA_RANK), 0.05),
        "w_kvb": n(ks[8], (L, KV_LORA_RANK, MLA_HEADS * (QK_NOPE_DIM + V_HEAD_DIM)), KV_LORA_RANK ** -0.5),
        "conv_w": n(ks[9], (L, CONV_WIDTH, CONV_CH), CONV_WIDTH ** -0.5),
        "conv_b": n(ks[10], (L, CONV_CH), 0.02),
        "dt_bias": dt0 + jnp.log(-jnp.expm1(-dt0)),
        "a_log": jnp.log(jax.random.uniform(ks[13], (L, SSM_HEADS), f32, 1.0, 16.0)),
        "d_skip": 1.0 + n(ks[14], (L, SSM_HEADS), 0.1),
        "ssm_norm_g": 1.0 + n(ks[15], (L, SSM_WIDTH), 0.05),
        "w_out": n(ks[16], (L, MIX_WIDTH, D_MODEL), DEEPNORM_BETA * MIX_WIDTH ** -0.5),
        "ln_g": 1.0 + n(ks[17], (L, D_MODEL), 0.05),
        "ln_b": n(ks[18], (L, D_MODEL), 0.02),
    }


def _fwd_reference(x, c, positions, w_ada, b_ada, w_in, q_norm_g, w_qb, kv_norm_g, w_kvb,
              conv_w, conv_b, dt_bias, a_log, d_skip, ssm_norm_g, w_out, ln_g, ln_b):
    h = x
    for l in range(DEPTH):
        h = hybrid_layer(h, c, positions, w_ada[l], b_ada[l], w_in[l], q_norm_g[l], w_qb[l],
                         kv_norm_g[l], w_kvb[l], conv_w[l], conv_b[l], dt_bias[l], a_log[l],
                         d_skip[l], ssm_norm_g[l], w_out[l], ln_g[l], ln_b[l])
    return h


import jax as _jax
import jax.numpy as _jnp

TWIN_FORMAT = 'train_step'
FWD_PARAMS = ['x', 'c', 'positions', 'w_ada', 'b_ada', 'w_in', 'q_norm_g', 'w_qb', 'kv_norm_g', 'w_kvb', 'conv_w', 'conv_b', 'dt_bias', 'a_log', 'd_skip', 'ssm_norm_g', 'w_out', 'ln_g', 'ln_b']
TWIN_WEIGHTS = ['w_ada', 'b_ada', 'w_in', 'q_norm_g', 'w_qb', 'kv_norm_g', 'w_kvb', 'conv_w', 'conv_b', 'dt_bias', 'a_log', 'd_skip', 'ssm_norm_g', 'w_out', 'ln_g', 'ln_b']
TWIN_DIFF_INPUT = 'x'
TWIN_INPUTS = ['x', 'c', 'positions', 'w_ada', 'b_ada', 'w_in', 'q_norm_g', 'w_qb', 'kv_norm_g', 'w_kvb', 'conv_w', 'conv_b', 'dt_bias', 'a_log', 'd_skip', 'ssm_norm_g', 'w_out', 'ln_g', 'ln_b', 'loss_target', 'm_w_ada', 'm_b_ada', 'm_w_in', 'm_q_norm_g', 'm_w_qb', 'm_kv_norm_g', 'm_w_kvb', 'm_conv_w', 'm_conv_b', 'm_dt_bias', 'm_a_log', 'm_d_skip', 'm_ssm_norm_g', 'm_w_out', 'm_ln_g', 'm_ln_b', 'v_w_ada', 'v_b_ada', 'v_w_in', 'v_q_norm_g', 'v_w_qb', 'v_kv_norm_g', 'v_w_kvb', 'v_conv_w', 'v_conv_b', 'v_dt_bias', 'v_a_log', 'v_d_skip', 'v_ssm_norm_g', 'v_w_out', 'v_ln_g', 'v_ln_b']
TWIN_OUTPUTS = ['loss', 'grad_x', 'grad_w_ada', 'grad_b_ada', 'grad_w_in', 'grad_q_norm_g', 'grad_w_qb', 'grad_kv_norm_g', 'grad_w_kvb', 'grad_conv_w', 'grad_conv_b', 'grad_dt_bias', 'grad_a_log', 'grad_d_skip', 'grad_ssm_norm_g', 'grad_w_out', 'grad_ln_g', 'grad_ln_b', 'delta_w_ada', 'delta_b_ada', 'delta_w_in', 'delta_q_norm_g', 'delta_w_qb', 'delta_kv_norm_g', 'delta_w_kvb', 'delta_conv_w', 'delta_conv_b', 'delta_dt_bias', 'delta_a_log', 'delta_d_skip', 'delta_ssm_norm_g', 'delta_w_out', 'delta_ln_g', 'delta_ln_b', 'new_m_w_ada', 'new_m_b_ada', 'new_m_w_in', 'new_m_q_norm_g', 'new_m_w_qb', 'new_m_kv_norm_g', 'new_m_w_kvb', 'new_m_conv_w', 'new_m_conv_b', 'new_m_dt_bias', 'new_m_a_log', 'new_m_d_skip', 'new_m_ssm_norm_g', 'new_m_w_out', 'new_m_ln_g', 'new_m_ln_b', 'new_v_w_ada', 'new_v_b_ada', 'new_v_w_in', 'new_v_q_norm_g', 'new_v_w_qb', 'new_v_kv_norm_g', 'new_v_w_kvb', 'new_v_conv_w', 'new_v_conv_b', 'new_v_dt_bias', 'new_v_a_log', 'new_v_d_skip', 'new_v_ssm_norm_g', 'new_v_w_out', 'new_v_ln_g', 'new_v_ln_b']
TWIN_LEAF_KINDS = {'loss': 'loss', 'grad_x': 'grad_x', 'grad_w_ada': 'grad_w', 'grad_b_ada': 'grad_w', 'grad_w_in': 'grad_w', 'grad_q_norm_g': 'grad_w', 'grad_w_qb': 'grad_w', 'grad_kv_norm_g': 'grad_w', 'grad_w_kvb': 'grad_w', 'grad_conv_w': 'grad_w', 'grad_conv_b': 'grad_w', 'grad_dt_bias': 'grad_w', 'grad_a_log': 'grad_w', 'grad_d_skip': 'grad_w', 'grad_ssm_norm_g': 'grad_w', 'grad_w_out': 'grad_w', 'grad_ln_g': 'grad_w', 'grad_ln_b': 'grad_w', 'delta_w_ada': 'delta_w', 'delta_b_ada': 'delta_w', 'delta_w_in': 'delta_w', 'delta_q_norm_g': 'delta_w', 'delta_w_qb': 'delta_w', 'delta_kv_norm_g': 'delta_w', 'delta_w_kvb': 'delta_w', 'delta_conv_w': 'delta_w', 'delta_conv_b': 'delta_w', 'delta_dt_bias': 'delta_w', 'delta_a_log': 'delta_w', 'delta_d_skip': 'delta_w', 'delta_ssm_norm_g': 'delta_w', 'delta_w_out': 'delta_w', 'delta_ln_g': 'delta_w', 'delta_ln_b': 'delta_w', 'new_m_w_ada': 'new_m', 'new_m_b_ada': 'new_m', 'new_m_w_in': 'new_m', 'new_m_q_norm_g': 'new_m', 'new_m_w_qb': 'new_m', 'new_m_kv_norm_g': 'new_m', 'new_m_w_kvb': 'new_m', 'new_m_conv_w': 'new_m', 'new_m_conv_b': 'new_m', 'new_m_dt_bias': 'new_m', 'new_m_a_log': 'new_m', 'new_m_d_skip': 'new_m', 'new_m_ssm_norm_g': 'new_m', 'new_m_w_out': 'new_m', 'new_m_ln_g': 'new_m', 'new_m_ln_b': 'new_m', 'new_v_w_ada': 'new_v', 'new_v_b_ada': 'new_v', 'new_v_w_in': 'new_v', 'new_v_q_norm_g': 'new_v', 'new_v_w_qb': 'new_v', 'new_v_kv_norm_g': 'new_v', 'new_v_w_kvb': 'new_v', 'new_v_conv_w': 'new_v', 'new_v_conv_b': 'new_v', 'new_v_dt_bias': 'new_v', 'new_v_a_log': 'new_v', 'new_v_d_skip': 'new_v', 'new_v_ssm_norm_g': 'new_v', 'new_v_w_out': 'new_v', 'new_v_ln_g': 'new_v', 'new_v_ln_b': 'new_v'}


def _forward(args):
    return _fwd_reference(*[args[k] for k in FWD_PARAMS])


def _output_shape():
    def fwd():
        inp = _fwd_setup_inputs(0)
        return _fwd_reference(*[inp[k] for k in FWD_PARAMS])
    out = _jax.eval_shape(fwd)
    return out.shape, out.dtype

N_MICROBATCH = 1
ADAM_LR = 0.001
ADAM_B1 = 0.9
ADAM_B2 = 0.999
ADAM_EPS = 1e-08
ADAM_WD = 0.01
ADAM_STEP = 10
PER_EXAMPLE_BATCH_AXIS = {'x': 0, 'c': 0, 'positions': 0, 'loss_target': 0}
SHARED_INPUTS = []
_WEIGHT_DTYPES = {'w_ada': _jnp.float32, 'b_ada': _jnp.float32, 'w_in': _jnp.float32, 'q_norm_g': _jnp.float32, 'w_qb': _jnp.float32, 'kv_norm_g': _jnp.float32, 'w_kvb': _jnp.float32, 'conv_w': _jnp.float32, 'conv_b': _jnp.float32, 'dt_bias': _jnp.float32, 'a_log': _jnp.float32, 'd_skip': _jnp.float32, 'ssm_norm_g': _jnp.float32, 'w_out': _jnp.float32, 'ln_g': _jnp.float32, 'ln_b': _jnp.float32}
MOMENT_SCALE = {'w_ada': 1.411301e-01, 'b_ada': 1.489333e-01, 'w_in': 5.011078e-02, 'q_norm_g': 1.263755e-02, 'w_qb': 6.494573e-03, 'kv_norm_g': 4.610695e-02, 'w_kvb': 1.586027e-02, 'conv_w': 5.390070e-02, 'conv_b': 6.860438e-02, 'dt_bias': 1.722262e-01, 'a_log': 6.570112e-01, 'd_skip': 2.695073e-01, 'ssm_norm_g': 8.371071e-02, 'w_out': 1.220237e-01, 'ln_g': 1.283237e+02, 'ln_b': 2.614840e+00}


def _to_microbatches(a, axis):
    t = _jnp.moveaxis(a, axis, 0)
    t = t.reshape((N_MICROBATCH, t.shape[0] // N_MICROBATCH) + t.shape[1:])
    return _jnp.moveaxis(t, 1, axis + 1)


def setup_inputs(seed: int = 0) -> dict:
    inp = _fwd_setup_inputs(seed)
    key = _jax.random.fold_in(_jax.random.key(seed), 7919)
    shape, _ = _output_shape()
    out = dict(inp)
    out["loss_target"] = _jax.random.normal(_jax.random.fold_in(key, 0), shape, _jnp.float32)
    for i, name in enumerate(TWIN_WEIGHTS):
        w = inp[name].astype(_jnp.float32)
        if MOMENT_SCALE is None:
            s = _jnp.sqrt(_jnp.mean(_jnp.square(w)) + 1e-30)
        else:
            s = MOMENT_SCALE[name]
        km, kv = _jax.random.split(_jax.random.fold_in(key, i + 1))
        out[name] = w
        out["m_" + name] = s * _jax.random.normal(km, w.shape, _jnp.float32)
        out["v_" + name] = (s * s) * _jax.random.uniform(kv, w.shape, _jnp.float32, 0.5, 1.5)
    if N_MICROBATCH > 1:
        for name, axis in PER_EXAMPLE_BATCH_AXIS.items():
            out[name] = _to_microbatches(out[name], axis)
    return {'x': out['x'], 'c': out['c'], 'positions': out['positions'], 'w_ada': out['w_ada'], 'b_ada': out['b_ada'], 'w_in': out['w_in'], 'q_norm_g': out['q_norm_g'], 'w_qb': out['w_qb'], 'kv_norm_g': out['kv_norm_g'], 'w_kvb': out['w_kvb'], 'conv_w': out['conv_w'], 'conv_b': out['conv_b'], 'dt_bias': out['dt_bias'], 'a_log': out['a_log'], 'd_skip': out['d_skip'], 'ssm_norm_g': out['ssm_norm_g'], 'w_out': out['w_out'], 'ln_g': out['ln_g'], 'ln_b': out['ln_b'], 'loss_target': out['loss_target'], 'm_w_ada': out['m_w_ada'], 'm_b_ada': out['m_b_ada'], 'm_w_in': out['m_w_in'], 'm_q_norm_g': out['m_q_norm_g'], 'm_w_qb': out['m_w_qb'], 'm_kv_norm_g': out['m_kv_norm_g'], 'm_w_kvb': out['m_w_kvb'], 'm_conv_w': out['m_conv_w'], 'm_conv_b': out['m_conv_b'], 'm_dt_bias': out['m_dt_bias'], 'm_a_log': out['m_a_log'], 'm_d_skip': out['m_d_skip'], 'm_ssm_norm_g': out['m_ssm_norm_g'], 'm_w_out': out['m_w_out'], 'm_ln_g': out['m_ln_g'], 'm_ln_b': out['m_ln_b'], 'v_w_ada': out['v_w_ada'], 'v_b_ada': out['v_b_ada'], 'v_w_in': out['v_w_in'], 'v_q_norm_g': out['v_q_norm_g'], 'v_w_qb': out['v_w_qb'], 'v_kv_norm_g': out['v_kv_norm_g'], 'v_w_kvb': out['v_w_kvb'], 'v_conv_w': out['v_conv_w'], 'v_conv_b': out['v_conv_b'], 'v_dt_bias': out['v_dt_bias'], 'v_a_log': out['v_a_log'], 'v_d_skip': out['v_d_skip'], 'v_ssm_norm_g': out['v_ssm_norm_g'], 'v_w_out': out['v_w_out'], 'v_ln_g': out['v_ln_g'], 'v_ln_b': out['v_ln_b']}


def _loss(weights, diff, rest, loss_target):
    with _jax.named_scope("forward"):
        args = {**rest, TWIN_DIFF_INPUT: diff, **{k: w.astype(_WEIGHT_DTYPES[k]) for k, w in weights.items()}}
        y = _forward(args)
    with _jax.named_scope("loss_head"):
        err = _jnp.square(y.astype(_jnp.float32) - loss_target)
        return 0.5 * _jnp.sum(_jnp.mean(err, axis=-1)) if err.ndim else 0.5 * err


def _adamw(w, g, m, v):
    m = ADAM_B1 * m + (1.0 - ADAM_B1) * g
    v = ADAM_B2 * v + (1.0 - ADAM_B2) * _jnp.square(g)
    m_hat = m / (1.0 - ADAM_B1 ** ADAM_STEP)
    v_hat = v / (1.0 - ADAM_B2 ** ADAM_STEP)
    delta = -ADAM_LR * (m_hat / (_jnp.sqrt(v_hat) + ADAM_EPS) + ADAM_WD * w)
    return delta, m, v


def reference(x, c, positions, w_ada, b_ada, w_in, q_norm_g, w_qb, kv_norm_g, w_kvb, conv_w, conv_b, dt_bias, a_log, d_skip, ssm_norm_g, w_out, ln_g, ln_b, loss_target, m_w_ada, m_b_ada, m_w_in, m_q_norm_g, m_w_qb, m_kv_norm_g, m_w_kvb, m_conv_w, m_conv_b, m_dt_bias, m_a_log, m_d_skip, m_ssm_norm_g, m_w_out, m_ln_g, m_ln_b, v_w_ada, v_b_ada, v_w_in, v_q_norm_g, v_w_qb, v_kv_norm_g, v_w_kvb, v_conv_w, v_conv_b, v_dt_bias, v_a_log, v_d_skip, v_ssm_norm_g, v_w_out, v_ln_g, v_ln_b):
    given = dict(x=x, c=c, positions=positions, w_ada=w_ada, b_ada=b_ada, w_in=w_in, q_norm_g=q_norm_g, w_qb=w_qb, kv_norm_g=kv_norm_g, w_kvb=w_kvb, conv_w=conv_w, conv_b=conv_b, dt_bias=dt_bias, a_log=a_log, d_skip=d_skip, ssm_norm_g=ssm_norm_g, w_out=w_out, ln_g=ln_g, ln_b=ln_b, loss_target=loss_target, m_w_ada=m_w_ada, m_b_ada=m_b_ada, m_w_in=m_w_in, m_q_norm_g=m_q_norm_g, m_w_qb=m_w_qb, m_kv_norm_g=m_kv_norm_g, m_w_kvb=m_w_kvb, m_conv_w=m_conv_w, m_conv_b=m_conv_b, m_dt_bias=m_dt_bias, m_a_log=m_a_log, m_d_skip=m_d_skip, m_ssm_norm_g=m_ssm_norm_g, m_w_out=m_w_out, m_ln_g=m_ln_g, m_ln_b=m_ln_b, v_w_ada=v_w_ada, v_b_ada=v_b_ada, v_w_in=v_w_in, v_q_norm_g=v_q_norm_g, v_w_qb=v_w_qb, v_kv_norm_g=v_kv_norm_g, v_w_kvb=v_w_kvb, v_conv_w=v_conv_w, v_conv_b=v_conv_b, v_dt_bias=v_dt_bias, v_a_log=v_a_log, v_d_skip=v_d_skip, v_ssm_norm_g=v_ssm_norm_g, v_w_out=v_w_out, v_ln_g=v_ln_g, v_ln_b=v_ln_b)
    weights = {n: given[n] for n in TWIN_WEIGHTS}
    shared = {n: given[n] for n in SHARED_INPUTS}
    per_example = {n: given[n] for n in ['x', 'c', 'positions']}
    grad_fn = _jax.value_and_grad(_loss, argnums=(0, 1))

    def one_microbatch(ex, loss_target):
        ex = dict(ex)
        diff = ex.pop(TWIN_DIFF_INPUT)
        return grad_fn(weights, diff, {**shared, **ex}, loss_target)

    if N_MICROBATCH == 1:
        loss, (grad_w, grad_x) = one_microbatch(per_example, given["loss_target"])
    else:
        def body(carry, xs):
            loss_sum, grad_sum = carry
            l_k, (gw_k, gx_k) = one_microbatch(xs[0], xs[1])
            with _jax.named_scope("update"):
                return (loss_sum + l_k, _jax.tree.map(_jnp.add, grad_sum, gw_k)), gx_k

        init = (_jnp.zeros((), _jnp.float32), _jax.tree.map(_jnp.zeros_like, weights))
        (loss, grad_w), grad_x = _jax.lax.scan(body, init, (per_example, given["loss_target"]))
    with _jax.named_scope("update"):
        delta_w, new_m, new_v = {}, {}, {}
        for n in TWIN_WEIGHTS:
            delta_w[n], new_m[n], new_v[n] = _adamw(weights[n], grad_w[n], given["m_" + n], given["v_" + n])
    return (loss, grad_x, *[grad_w[n] for n in TWIN_WEIGHTS], *[delta_w[n] for n in TWIN_WEIGHTS],
            *[new_m[n] for n in TWIN_WEIGHTS], *[new_v[n] for n in TWIN_WEIGHTS])
```

```python
import functools
import math

import jax
import jax.numpy as jnp
from jax import lax
from jax.experimental import pallas as pl
from jax.experimental.pallas import tpu as pltpu

F32 = jnp.float32
BF16 = jnp.bfloat16

N_DEV = 8
D_MODEL = 1024
MLA_HEADS = 8
QK_NOPE = 128
QK_ROPE = 64
V_DIM = 128
Q_RANK = 384
KV_RANK = 256
QK_HEAD = QK_NOPE + QK_ROPE
HEAD_PAD = 256
ROPE_HALF = QK_ROPE // 2
ROPE_THETA = 10000.0
MLA_WIDTH = MLA_HEADS * V_DIM
SSM_HEADS = 16
SSM_P = 64
SSM_WIDTH = SSM_HEADS * SSM_P
SSM_GROUPS = 2
SSM_N = 128
CONV_K = 4
CHUNK = 128
CONV_CH = SSM_WIDTH + 2 * SSM_GROUPS * SSM_N
MIX_WIDTH = MLA_WIDTH + SSM_WIDTH
IN_SPLITS = (Q_RANK, KV_RANK + QK_ROPE, MLA_WIDTH, CONV_CH, SSM_HEADS, SSM_WIDTH)
IN_WIDTH = sum(IN_SPLITS)
LANE = 128
KV_LAT_PAD = KV_RANK + LANE
IN_PAD = (Q_RANK, KV_LAT_PAD, MLA_WIDTH, CONV_CH, LANE, SSM_WIDTH)
IN_PAD_WIDTH = sum(IN_PAD)
DEEPNORM_ALPHA = 2.0 ** 0.25
RMS_EPS = 1e-6
LN_EPS = 1e-5
ATTN_SCALE = QK_HEAD ** -0.5
ADAM_LR, ADAM_B1, ADAM_B2, ADAM_EPS, ADAM_WD, ADAM_STEP = 0.001, 0.9, 0.999, 1e-08, 0.01, 10

ROW_TILE = 256
ATTN_TILE = 512
SSD_ROWS = 512
VMEM_LIMIT = 56 * 1024 * 1024


def _nn(a, b):
    return jnp.dot(a, b, preferred_element_type=F32)


def _nt(a, b):
    return lax.dot_general(a, b, (((1,), (1,)), ((), ())), preferred_element_type=F32)


def _tn(a, b):
    return lax.dot_general(a, b, (((0,), (0,)), ((), ())), preferred_element_type=F32)


def _cparams(*sem):
    return pltpu.CompilerParams(dimension_semantics=sem, vmem_limit_bytes=VMEM_LIMIT)


def _rows(tm, w):
    return pl.BlockSpec((tm, w), lambda i: (i, 0))


def _whole(shape):
    return pl.BlockSpec(shape, lambda i: (0,) * len(shape))


def _sigmoid(z):
    return 1.0 / (1.0 + jnp.exp(-z))


def _lane_iota(shape):
    return lax.broadcasted_iota(jnp.int32, shape, len(shape) - 1)


def _swap_halves(r):
    lane = _lane_iota(r.shape)
    return jnp.where(lane < ROPE_HALF, pltpu.roll(r, LANE - ROPE_HALF, 1),
                     jnp.where(lane < QK_ROPE, pltpu.roll(r, ROPE_HALF, 1), 0.0))


def _rope(r, cos, sin):
    return r * cos + _swap_halves(r) * sin


def _rope_transposed(d, cos, sin):
    return d * cos + _swap_halves(d * sin)


def _rms(x):
    rstd = lax.rsqrt(jnp.mean(x * x, axis=-1, keepdims=True) + RMS_EPS)
    return x * rstd, rstd


def _rms_bwd(dxhat, xhat, rstd):
    return rstd * (dxhat - xhat * jnp.mean(dxhat * xhat, axis=-1, keepdims=True))


def _acc_rows(ref, val):
    @pl.when(pl.program_id(0) == 0)
    def _():
        ref[...] = jnp.zeros_like(ref)
    ref[...] += val


def _colsum(v):
    return jnp.sum(v, axis=0, keepdims=True)


def _inproj(x, scale1p, shift, w_in_p):
    s = x.shape[0]
    tm = ROW_TILE

    def body(x_ref, sc_ref, sh_ref, w_ref, u_ref, *outs):
        u = (x_ref[...] * sc_ref[...] + sh_ref[...]).astype(BF16)
        u_ref[...] = u
        proj = _nn(u, w_ref[...])
        off = 0
        for ref, w in zip(outs, IN_PAD):
            ref[...] = proj[:, off:off + w]
            off += w

    return pl.pallas_call(
        body, name="inproj", grid=(s // tm,),
        in_specs=[_rows(tm, D_MODEL), _whole((1, D_MODEL)), _whole((1, D_MODEL)), _whole((D_MODEL, IN_PAD_WIDTH))],
        out_specs=[_rows(tm, D_MODEL)] + [_rows(tm, w) for w in IN_PAD],
        out_shape=[jax.ShapeDtypeStruct((s, D_MODEL), BF16)] + [jax.ShapeDtypeStruct((s, w), F32) for w in IN_PAD],
        compiler_params=_cparams("parallel"),
    )(x, scale1p, shift, w_in_p)


def _qpath(q_lat, g_q, w_qb_p, cos, sin):
    s = q_lat.shape[0]
    tm = ROW_TILE

    def body(ql_ref, g_ref, w_ref, cos_ref, sin_ref, nq_ref, q_ref):
        xhat, _ = _rms(ql_ref[...])
        nq = (xhat * g_ref[...]).astype(BF16)
        nq_ref[...] = nq
        raw = _nn(nq, w_ref[...])
        c, sn = cos_ref[...], sin_ref[...]
        for h in range(MLA_HEADS):
            o = h * HEAD_PAD
            q_ref[:, o:o + QK_NOPE] = raw[:, o:o + QK_NOPE].astype(BF16)
            q_ref[:, o + QK_NOPE:o + HEAD_PAD] = _rope(raw[:, o + QK_NOPE:o + HEAD_PAD], c, sn).astype(BF16)

    return pl.pallas_call(
        body, name="qpath", grid=(s // tm,),
        in_specs=[_rows(tm, Q_RANK), _whole((1, Q_RANK)), _whole((Q_RANK, MLA_HEADS * HEAD_PAD)),
                  _rows(tm, LANE), _rows(tm, LANE)],
        out_specs=[_rows(tm, Q_RANK), _rows(tm, MLA_HEADS * HEAD_PAD)],
        out_shape=[jax.ShapeDtypeStruct((s, Q_RANK), BF16), jax.ShapeDtypeStruct((s, MLA_HEADS * HEAD_PAD), BF16)],
        compiler_params=_cparams("parallel"),
    )(q_lat, g_q, w_qb_p, cos, sin)


def _kvpath(kv_lat, g_kv, w_kvb_p, cos, sin):
    s = kv_lat.shape[0]
    tm = ROW_TILE

    def body(kl_ref, g_ref, w_ref, cos_ref, sin_ref, nkv_ref, k_ref, v_ref):
        kl = kl_ref[...]
        xhat, _ = _rms(kl[:, :KV_RANK])
        nkv = (xhat * g_ref[...]).astype(BF16)
        nkv_ref[...] = nkv
        raw = _nn(nkv, w_ref[...])
        kr = _rope(kl[:, KV_RANK:], cos_ref[...], sin_ref[...]).astype(BF16)
        for h in range(MLA_HEADS):
            o = h * HEAD_PAD
            k_ref[:, o:o + QK_NOPE] = raw[:, h * QK_NOPE:(h + 1) * QK_NOPE].astype(BF16)
            k_ref[:, o + QK_NOPE:o + HEAD_PAD] = kr
        v_ref[...] = raw[:, MLA_HEADS * QK_NOPE:].astype(BF16)

    return pl.pallas_call(
        body, name="kvpath", grid=(s // tm,),
        in_specs=[_rows(tm, KV_LAT_PAD), _whole((1, KV_RANK)), _whole((KV_RANK, MLA_HEADS * (QK_NOPE + V_DIM))),
                  _rows(tm, LANE), _rows(tm, LANE)],
        out_specs=[_rows(tm, KV_RANK), _rows(tm, MLA_HEADS * HEAD_PAD), _rows(tm, MLA_WIDTH)],
        out_shape=[jax.ShapeDtypeStruct((s, KV_RANK), BF16), jax.ShapeDtypeStruct((s, MLA_HEADS * HEAD_PAD), BF16),
                   jax.ShapeDtypeStruct((s, MLA_WIDTH), BF16)],
        compiler_params=_cparams("parallel"),
    )(kv_lat, g_kv, w_kvb_p, cos, sin)


def _causal_mask(t):
    row = lax.broadcasted_iota(jnp.int32, (t, t), 0)
    col = lax.broadcasted_iota(jnp.int32, (t, t), 1)
    return row, col


def _attn_fwd(q, k, v):
    s = q.shape[0]
    t = min(ATTN_TILE, s)
    nq = s // t

    def body(q_ref, k_ref, v_ref, o_ref, lse_ref, m_sc, l_sc, acc_sc):
        i = pl.program_id(1)
        qv = q_ref[...]
        m_sc[...] = jnp.full(m_sc.shape, -jnp.inf, F32)
        l_sc[...] = jnp.zeros(l_sc.shape, F32)
        acc_sc[...] = jnp.zeros(acc_sc.shape, F32)

        def step(j, masked):
            off = pl.multiple_of(j * t, t)
            kv_ = k_ref[pl.ds(off, t), :]
            vv = v_ref[pl.ds(off, t), :]
            sc = _nt(qv, kv_) * ATTN_SCALE
            if masked:
                row, col = _causal_mask(t)
                sc = jnp.where(col <= row, sc, -jnp.inf)
            m_prev = m_sc[...]
            m_new = jnp.maximum(m_prev, jnp.max(sc, axis=1, keepdims=True))
            alpha = jnp.exp(m_prev - m_new)
            p = jnp.exp(sc - jnp.tile(m_new, (1, t // LANE)))
            l_sc[...] = alpha * l_sc[...] + jnp.sum(p, axis=1, keepdims=True)
            acc_sc[...] = alpha * acc_sc[...] + _nn(p.astype(BF16), vv)
            m_sc[...] = m_new

        def loop_body(j, carry):
            step(j, False)
            return carry

        lax.fori_loop(0, i, loop_body, 0)
        step(i, True)
        l = l_sc[...]
        o_ref[...] = acc_sc[...] / l
        lse_ref[...] = m_sc[...] + jnp.log(l)

    return pl.pallas_call(
        body, name="attn_fwd", grid=(MLA_HEADS, nq),
        in_specs=[pl.BlockSpec((t, HEAD_PAD), lambda h, i: (i, h)),
                  pl.BlockSpec((s, HEAD_PAD), lambda h, i: (0, h)),
                  pl.BlockSpec((s, V_DIM), lambda h, i: (0, h))],
        out_specs=[pl.BlockSpec((t, V_DIM), lambda h, i: (i, h)), pl.BlockSpec((t, LANE), lambda h, i: (i, h))],
        out_shape=[jax.ShapeDtypeStruct((s, MLA_WIDTH), F32), jax.ShapeDtypeStruct((s, MLA_HEADS * LANE), F32)],
        scratch_shapes=[pltpu.VMEM((t, LANE), F32), pltpu.VMEM((t, LANE), F32), pltpu.VMEM((t, V_DIM), F32)],
        compiler_params=_cparams("parallel", "arbitrary"),
    )(q, k, v)


def _attn_dq(q, k, v, do, lse, delta):
    s = q.shape[0]
    t = min(ATTN_TILE, s)
    nq = s // t

    def body(q_ref, k_ref, v_ref, do_ref, lse_ref, dl_ref, dq_ref, acc_sc):
        i = pl.program_id(1)
        qv = q_ref[...]
        dov = do_ref[...]
        lse_t = jnp.tile(lse_ref[...], (1, t // LANE))
        dl_t = jnp.tile(dl_ref[...], (1, t // LANE))
        acc_sc[...] = jnp.zeros(acc_sc.shape, F32)

        def step(j, masked):
            off = pl.multiple_of(j * t, t)
            kv_ = k_ref[pl.ds(off, t), :]
            vv = v_ref[pl.ds(off, t), :]
            sc = _nt(qv, kv_) * ATTN_SCALE
            if masked:
                row, col = _causal_mask(t)
                sc = jnp.where(col <= row, sc, -jnp.inf)
            p = jnp.exp(sc - lse_t)
            dp = _nt(dov, vv)
            ds = (p * (dp - dl_t) * ATTN_SCALE).astype(BF16)
            acc_sc[...] += _nn(ds, kv_)

        def loop_body(j, carry):
            step(j, False)
            return carry

        lax.fori_loop(0, i, loop_body, 0)
        step(i, True)
        dq_ref[...] = acc_sc[...]

    return pl.pallas_call(
        body, name="attn_dq", grid=(MLA_HEADS, nq),
        in_specs=[pl.BlockSpec((t, HEAD_PAD), lambda h, i: (i, h)),
                  pl.BlockSpec((s, HEAD_PAD), lambda h, i: (0, h)),
                  pl.BlockSpec((s, V_DIM), lambda h, i: (0, h)),
                  pl.BlockSpec((t, V_DIM), lambda h, i: (i, h)),
                  pl.BlockSpec((t, LANE), lambda h, i: (i, h)),
                  pl.BlockSpec((t, LANE), lambda h, i: (i, h))],
        out_specs=pl.BlockSpec((t, HEAD_PAD), lambda h, i: (i, h)),
        out_shape=jax.ShapeDtypeStruct((s, MLA_HEADS * HEAD_PAD), F32),
        scratch_shapes=[pltpu.VMEM((t, HEAD_PAD), F32)],
        compiler_params=_cparams("parallel", "arbitrary"),
    )(q, k, v, do, lse, delta)


def _attn_dkv(q, k, v, do, lse_row, delta_row):
    s = q.shape[0]
    t = min(ATTN_TILE, s)
    nq = s // t

    def body(q_ref, k_ref, v_ref, do_ref, lse_ref, dl_ref, dk_ref, dv_ref, dk_sc, dv_sc):
        j = pl.program_id(1)
        kv_ = k_ref[...]
        vv = v_ref[...]
        dk_sc[...] = jnp.zeros(dk_sc.shape, F32)
        dv_sc[...] = jnp.zeros(dv_sc.shape, F32)

        def step(i, masked):
            off = pl.multiple_of(i * t, t)
            qv = q_ref[pl.ds(off, t), :]
            dov = do_ref[pl.ds(off, t), :]
            lse_r = lse_ref[0, :, pl.ds(off, t)]
            dl_r = dl_ref[0, :, pl.ds(off, t)]
            sct = _nt(kv_, qv) * ATTN_SCALE
            if masked:
                row, col = _causal_mask(t)
                sct = jnp.where(row <= col, sct, -jnp.inf)
            pt = jnp.exp(sct - lse_r)
            dv_sc[...] += _nn(pt.astype(BF16), dov)
            dpt = _nt(vv, dov)
            dst = (pt * (dpt - dl_r) * ATTN_SCALE).astype(BF16)
            dk_sc[...] += _nn(dst, qv)

        def loop_body(i, carry):
            step(i, False)
            return carry

        step(j, True)
        lax.fori_loop(j + 1, nq, loop_body, 0)
        dk_ref[...] = dk_sc[...]
        dv_ref[...] = dv_sc[...]

    return pl.pallas_call(
        body, name="attn_dkv", grid=(MLA_HEADS, nq),
        in_specs=[pl.BlockSpec((s, HEAD_PAD), lambda h, j: (0, h)),
                  pl.BlockSpec((t, HEAD_PAD), lambda h, j: (j, h)),
                  pl.BlockSpec((t, V_DIM), lambda h, j: (j, h)),
                  pl.BlockSpec((s, V_DIM), lambda h, j: (0, h)),
                  pl.BlockSpec((1, 1, s), lambda h, j: (h, 0, 0)),
                  pl.BlockSpec((1, 1, s), lambda h, j: (h, 0, 0))],
        out_specs=[pl.BlockSpec((t, HEAD_PAD), lambda h, j: (j, h)), pl.BlockSpec((t, V_DIM), lambda h, j: (j, h))],
        out_shape=[jax.ShapeDtypeStruct((s, MLA_HEADS * HEAD_PAD), F32), jax.ShapeDtypeStruct((s, MLA_WIDTH), F32)],
        scratch_shapes=[pltpu.VMEM((t, HEAD_PAD), F32), pltpu.VMEM((t, V_DIM), F32)],
        compiler_params=_cparams("parallel", "arbitrary"),
    )(q, k, v, do, lse_row, delta_row)


HALO = 8


def _silu(z):
    return z * _sigmoid(z)


def _silu_grad(z):
    sg = _sigmoid(z)
    return sg * (1.0 + z * (1.0 - sg))


def _softplus(x):
    e = jnp.exp(-jnp.abs(x))
    small = e * (1.0 - e * (0.5 - e * (1.0 / 3.0)))
    return jnp.maximum(x, 0.0) + jnp.where(e < 1e-3, small, jnp.log(1.0 + e))


def _conv_taps(xe_ref, w, tm, first):
    acc = None
    for k in range(CONV_K):
        term = xe_ref[pl.ds(HALO + first - (CONV_K - 1) + k, tm), :] * w[k:k + 1, :]
        acc = term if acc is None else acc + term
    return acc


def _ssd_pre(xbc_raw, dt_raw, conv_w, conv_b, dt_bias_p):
    s = xbc_raw.shape[0]
    tm = ROW_TILE
    hb = tm // HALO

    def body(x_ref, prev_ref, dtr_ref, w_ref, b_ref, db_ref, act_ref, dt_ref, xe_sc):
        i = pl.program_id(0)
        xe_sc[pl.ds(0, HALO), :] = jnp.where(i > 0, prev_ref[...], 0.0)
        xe_sc[pl.ds(HALO, tm), :] = x_ref[...]
        pre = _conv_taps(xe_sc, w_ref[...], tm, 0) + b_ref[...]
        act_ref[...] = _silu(pre)
        dt_ref[...] = _softplus(dtr_ref[...] + db_ref[...])

    return pl.pallas_call(
        body, name="ssd_pre", grid=(s // tm,),
        in_specs=[_rows(tm, CONV_CH), pl.BlockSpec((HALO, CONV_CH), lambda i: (jnp.maximum(i * hb - 1, 0), 0)),
                  _rows(tm, LANE), _whole((CONV_K, CONV_CH)), _whole((1, CONV_CH)), _whole((1, LANE))],
        out_specs=[_rows(tm, CONV_CH), _rows(tm, LANE)],
        out_shape=[jax.ShapeDtypeStruct((s, CONV_CH), F32), jax.ShapeDtypeStruct((s, LANE), F32)],
        scratch_shapes=[pltpu.VMEM((tm + HALO, CONV_CH), F32)],
        compiler_params=_cparams("parallel"),
    )(xbc_raw, xbc_raw, dt_raw, conv_w, conv_b, dt_bias_p)


def _split3(a):
    a1 = a.astype(BF16)
    r1 = a - a1.astype(F32)
    a2 = r1.astype(BF16)
    a3 = (r1 - a2.astype(F32)).astype(BF16)
    return a1, a2, a3


def _tri_left(tri, a):
    a1, a2, a3 = _split3(a)
    return _nn(tri, a1) + _nn(tri, a2) + _nn(tri, a3)


def _tri_right(a, tri):
    a1, a2, a3 = _split3(a)
    return _nn(a1, tri) + _nn(a2, tri) + _nn(a3, tri)


def _pair_sel(lane_lo, col_a, col_b):
    return jnp.where(lane_lo, col_a, col_b)


def _chunk_common(dt, a_neg, tril, triu):
    a = dt * a_neg
    lam_c = _tri_left(tril, a)
    lam_r = _tri_right(a.T, triu)
    lam_last = lam_c[CHUNK - 1:CHUNK, :]
    return lam_c, lam_r, lam_last


def _gated_norm_fwd(y, z, g):
    hf = y * _silu(z)
    outs = []
    for grp in range(SSM_GROUPS):
        w = SSM_WIDTH // SSM_GROUPS
        n, _ = _rms(hf[:, grp * w:(grp + 1) * w])
        outs.append(n)
    return jnp.concatenate(outs, axis=1) * g


def _ssd_fwd(xbc, dt, z, a_neg, dskip_x, g_x, tril, triu):
    s = xbc.shape[0]
    tm = min(SSD_ROWS, s)
    cpb = tm // CHUNK
    nc = s // CHUNK

    def body(xbc_ref, dt_ref, z_ref, a_ref, dsk_ref, g_ref, tril_ref, triu_ref, y_ref, o_ref, hin_ref, h_sc):
        @pl.when(pl.program_id(0) == 0)
        def _():
            h_sc[...] = jnp.zeros(h_sc.shape, F32)

        tril, triu = tril_ref[...], triu_ref[...]
        ltri = tril > 0
        lane_lo = _lane_iota((CHUNK, LANE)) < SSM_P
        lane_lo_n = lane_lo

        def chunk(c, carry):
            r0 = pl.multiple_of(c * CHUNK, CHUNK)
            dtc = dt_ref[pl.ds(r0, CHUNK), :]
            lam_c, lam_r, lam_last = _chunk_common(dtc, a_ref[...], tril, triu)
            e_c = jnp.exp(lam_c)
            f_r = jnp.exp(lam_r[:, CHUNK - 1:CHUNK] - lam_r)
            cd = jnp.exp(lam_last)
            for grp in range(SSM_GROUPS):
                bo = SSM_WIDTH + grp * SSM_N
                co = SSM_WIDTH + SSM_GROUPS * SSM_N + grp * SSM_N
                bm = xbc_ref[pl.ds(r0, CHUNK), bo:bo + SSM_N]
                cm = xbc_ref[pl.ds(r0, CHUNK), co:co + SSM_N]
                cm_b = cm.astype(BF16)
                gmat = _nt(cm_b, bm.astype(BF16))
                bt = bm.T
                for pj in range(SSM_HEADS // SSM_GROUPS // 2):
                    ha = grp * (SSM_HEADS // SSM_GROUPS) + 2 * pj
                    hb_ = ha + 1
                    lo = ha * SSM_P
                    xs = xbc_ref[pl.ds(r0, CHUNK), lo:lo + LANE]
                    x2 = xs * _pair_sel(lane_lo, dtc[:, ha:ha + 1], dtc[:, hb_:hb_ + 1])
                    x2b = x2.astype(BF16)
                    ys, sts = [], []
                    for hh in (ha, hb_):
                        seg = lam_c[:, hh:hh + 1] - lam_r[hh:hh + 1, :]
                        dec = jnp.exp(jnp.where(ltri, seg, -jnp.inf))
                        ys.append(_nn((gmat * dec).astype(BF16), x2b))
                        sts.append(_nn((bt * f_r[hh:hh + 1, :]).astype(BF16), x2b))
                    hp = h_sc[:, lo:lo + LANE]
                    hin_ref[c, :, lo:lo + LANE] = hp
                    zz = _nn(cm_b, hp.astype(BF16))
                    e2 = _pair_sel(lane_lo, e_c[:, ha:ha + 1], e_c[:, hb_:hb_ + 1])
                    yv = jnp.where(lane_lo, ys[0], ys[1]) + e2 * zz
                    y_ref[pl.ds(r0, CHUNK), lo:lo + LANE] = yv + xs * dsk_ref[:, lo:lo + LANE]
                    cd2 = _pair_sel(lane_lo_n, cd[:, ha:ha + 1], cd[:, hb_:hb_ + 1])
                    h_sc[:, lo:lo + LANE] = hp * cd2 + jnp.where(lane_lo_n, sts[0], sts[1])
            return carry

        lax.fori_loop(0, cpb, chunk, 0)
        o_ref[...] = _gated_norm_fwd(y_ref[...], z_ref[...], g_ref[...])

    return pl.pallas_call(
        body, name="ssd_fwd", grid=(s // tm,),
        in_specs=[_rows(tm, CONV_CH), _rows(tm, LANE), _rows(tm, SSM_WIDTH), _whole((1, LANE)),
                  _whole((1, SSM_WIDTH)), _whole((1, SSM_WIDTH)), _whole((CHUNK, CHUNK)), _whole((CHUNK, CHUNK))],
        out_specs=[_rows(tm, SSM_WIDTH), _rows(tm, SSM_WIDTH),
                   pl.BlockSpec((cpb, SSM_N, SSM_WIDTH), lambda i: (i, 0, 0))],
        out_shape=[jax.ShapeDtypeStruct((s, SSM_WIDTH), F32), jax.ShapeDtypeStruct((s, SSM_WIDTH), F32),
                   jax.ShapeDtypeStruct((nc, SSM_N, SSM_WIDTH), F32)],
        scratch_shapes=[pltpu.VMEM((SSM_N, SSM_WIDTH), F32)],
        compiler_params=_cparams("arbitrary"),
    )(xbc, dt, z, a_neg, dskip_x, g_x, tril, triu)


def _outln(o, z_attn, o_ssm, w_out, x, gate, ln_g, ln_b, tgt):
    s = x.shape[0]
    tm = ROW_TILE

    def body(o_ref, z_ref, os_ref, w_ref, x_ref, gate_ref, g_ref, b_ref, t_ref,
             cat_ref, dmix_ref, gx_ref, do_ref, dz_ref, dl_ref, dos_ref, loss_ref, dg_ref, db_ref, dgate_ref):
        ov, zv = o_ref[...], z_ref[...]
        sz = _silu(zv)
        cat_ref[:, :MLA_WIDTH] = (ov * sz).astype(BF16)
        cat_ref[:, MLA_WIDTH:] = os_ref[...].astype(BF16)
        w = w_ref[...]
        mixed = _nn(cat_ref[...], w)
        gate_v = gate_ref[...]
        hv = DEEPNORM_ALPHA * x_ref[...] + gate_v * mixed
        mu = jnp.mean(hv, axis=-1, keepdims=True)
        hc = hv - mu
        rstd = lax.rsqrt(jnp.mean(hc * hc, axis=-1, keepdims=True) + LN_EPS)
        xhat = hc * rstd
        g = g_ref[...]
        err = xhat * g + b_ref[...] - t_ref[...]
        _acc_rows(loss_ref, jnp.full((1, LANE), (0.5 / D_MODEL) * jnp.sum(err * err), F32))
        dy = err * (1.0 / D_MODEL)
        _acc_rows(dg_ref, _colsum(dy * xhat))
        _acc_rows(db_ref, _colsum(dy))
        dxhat = dy * g
        dh = rstd * (dxhat - jnp.mean(dxhat, axis=-1, keepdims=True)
                     - xhat * jnp.mean(dxhat * xhat, axis=-1, keepdims=True))
        gx_ref[...] = DEEPNORM_ALPHA * dh
        _acc_rows(dgate_ref, _colsum(dh * mixed))
        dmix = (gate_v * dh).astype(BF16)
        dmix_ref[...] = dmix
        dcat = _nt(dmix, w)
        da = dcat[:, :MLA_WIDTH]
        dos_ref[...] = dcat[:, MLA_WIDTH:]
        dov = da * sz
        do_ref[...] = dov.astype(BF16)
        dz_ref[...] = da * ov * _silu_grad(zv)
        prod = dov * ov
        for h in range(MLA_HEADS):
            dsum = jnp.sum(prod[:, h * V_DIM:(h + 1) * V_DIM], axis=1, keepdims=True)
            dl_ref[:, h * LANE:(h + 1) * LANE] = jnp.broadcast_to(dsum, (tm, LANE))

    vec = _whole((1, D_MODEL))
    return pl.pallas_call(
        body, name="outln", grid=(s // tm,),
        in_specs=[_rows(tm, MLA_WIDTH), _rows(tm, MLA_WIDTH), _rows(tm, SSM_WIDTH), _whole((MIX_WIDTH, D_MODEL)),
                  _rows(tm, D_MODEL), vec, vec, vec, _rows(tm, D_MODEL)],
        out_specs=[_rows(tm, MIX_WIDTH), _rows(tm, D_MODEL), _rows(tm, D_MODEL), _rows(tm, MLA_WIDTH),
                   _rows(tm, MLA_WIDTH), _rows(tm, MLA_HEADS * LANE), _rows(tm, SSM_WIDTH),
                   _whole((1, LANE)), vec, vec, vec],
        out_shape=[jax.ShapeDtypeStruct((s, MIX_WIDTH), BF16), jax.ShapeDtypeStruct((s, D_MODEL), BF16),
                   jax.ShapeDtypeStruct((s, D_MODEL), F32), jax.ShapeDtypeStruct((s, MLA_WIDTH), BF16),
                   jax.ShapeDtypeStruct((s, MLA_WIDTH), F32), jax.ShapeDtypeStruct((s, MLA_HEADS * LANE), F32),
                   jax.ShapeDtypeStruct((s, SSM_WIDTH), F32), jax.ShapeDtypeStruct((1, LANE), F32),
                   jax.ShapeDtypeStruct((1, D_MODEL), F32), jax.ShapeDtypeStruct((1, D_MODEL), F32),
                   jax.ShapeDtypeStruct((1, D_MODEL), F32)],
        compiler_params=_cparams("arbitrary"),
    )(o, z_attn, o_ssm, w_out, x, gate, ln_g, ln_b, tgt)


def _ssd_bwd(dos, y, z, xbc, dt, hin, a_neg, dskip_x, g_x, tril, triu):
    s = xbc.shape[0]
    tm = min(SSD_ROWS, s)
    cpb = tm // CHUNK
    nb = s // tm
    gw = SSM_WIDTH // SSM_GROUPS
    hpg = SSM_HEADS // SSM_GROUPS

    def body(dos_ref, y_ref, z_ref, xbc_ref, dt_ref, hin_ref, a_ref, dsk_ref, g_ref, tril_ref, triu_ref,
             dxbc_ref, ddt_ref, dz_ref, dg_ref, ddsk_ref, da_ref, dh_sc, dy_sc):
        @pl.when(pl.program_id(0) == 0)
        def _():
            dh_sc[...] = jnp.zeros(dh_sc.shape, F32)

        yv, zv, dov = y_ref[...], z_ref[...], dos_ref[...]
        sz = _silu(zv)
        hf = yv * sz
        gv = g_ref[...]
        dgs, dhfs = [], []
        for grp in range(SSM_GROUPS):
            sl = slice(grp * gw, (grp + 1) * gw)
            n, rstd = _rms(hf[:, sl])
            dgs.append(_colsum(dov[:, sl] * n))
            dhfs.append(_rms_bwd(dov[:, sl] * gv[:, sl], n, rstd))
        dhf = jnp.concatenate(dhfs, axis=1)
        _acc_rows(dg_ref, jnp.concatenate(dgs, axis=1))
        dy_sc[...] = dhf * sz
        dz_ref[...] = dhf * yv * _silu_grad(zv)

        tril, triu = tril_ref[...], triu_ref[...]
        ltri = tril > 0
        utri = triu > 0
        lane = _lane_iota((CHUNK, LANE))
        lane1 = _lane_iota((1, LANE))
        lane_lo = lane < SSM_P
        row_last = lax.broadcasted_iota(jnp.int32, (CHUNK, LANE), 0) == CHUNK - 1
        a_neg_v = a_ref[...]

        def chunk(ci, carry):
            dsk_acc, da_acc = carry
            cl = cpb - 1 - ci
            r0 = pl.multiple_of(cl * CHUNK, CHUNK)
            rows = pl.ds(r0, CHUNK)
            dtc = dt_ref[rows, :]
            lam_c, lam_r, lam_last = _chunk_common(dtc, a_neg_v, tril, triu)
            e_c = jnp.exp(lam_c)
            f_c = jnp.exp(lam_last - lam_c)
            cd = jnp.exp(lam_last)
            dlam = jnp.zeros((CHUNK, LANE), F32)
            dlast = jnp.zeros((1, LANE), F32)
            ddt_x = jnp.zeros((CHUNK, LANE), F32)
            dsk_parts = []
            for grp in range(SSM_GROUPS):
                bo = SSM_WIDTH + grp * SSM_N
                co = SSM_WIDTH + SSM_GROUPS * SSM_N + grp * SSM_N
                bm = xbc_ref[rows, bo:bo + SSM_N]
                cm = xbc_ref[rows, co:co + SSM_N]
                bm_b, cm_b = bm.astype(BF16), cm.astype(BF16)
                gmat = _nt(cm_b, bm_b)
                gmat_t = _nt(bm_b, cm_b)
                ct_b = cm.T.astype(BF16)
                acc_dg = jnp.zeros((CHUNK, CHUNK), F32)
                acc_dgt = jnp.zeros((CHUNK, CHUNK), F32)
                d_b = jnp.zeros((CHUNK, SSM_N), F32)
                d_c = jnp.zeros((CHUNK, SSM_N), F32)
                for pj in range(hpg // 2):
                    ha = grp * hpg + 2 * pj
                    hb_ = ha + 1
                    lo = ha * SSM_P
                    xs = xbc_ref[rows, lo:lo + LANE]
                    dt2 = _pair_sel(lane_lo, dtc[:, ha:ha + 1], dtc[:, hb_:hb_ + 1])
                    x2 = xs * dt2
                    x2b = x2.astype(BF16)
                    dy2 = dy_sc[rows, lo:lo + LANE]
                    dy2b = dy2.astype(BF16)
                    hp = hin_ref[cl, :, lo:lo + LANE]
                    hp_b = hp.astype(BF16)
                    dhn = dh_sc[:, lo:lo + LANE]
                    dhn_b = dhn.astype(BF16)
                    e2 = _pair_sel(lane_lo, e_c[:, ha:ha + 1], e_c[:, hb_:hb_ + 1])
                    yo = e2 * _nn(cm_b, hp_b)
                    dzz_b = (e2 * dy2).astype(BF16)
                    d_c = d_c + _nt(dzz_b, hp_b)
                    cd2 = _pair_sel(lane_lo, cd[:, ha:ha + 1], cd[:, hb_:hb_ + 1])
                    dh_sc[:, lo:lo + LANE] = _nn(ct_b, dzz_b) + cd2 * dhn
                    t_yo = dy2 * yo
                    t_hh = dhn * hp
                    dx2 = jnp.zeros((CHUNK, LANE), F32)
                    for hh, msk in ((ha, lane_lo), (hb_, jnp.logical_not(lane_lo))):
                        x2h_b = jnp.where(msk, x2, 0.0).astype(BF16)
                        dy2h_b = jnp.where(msk, dy2, 0.0).astype(BF16)
                        lc = lam_c[:, hh:hh + 1]
                        lr = lam_r[hh:hh + 1, :]
                        dec = jnp.exp(jnp.where(ltri, lc - lr, -jnp.inf))
                        dect = jnp.exp(jnp.where(utri, lr - lc, -jnp.inf))
                        dmd = _nt(dy2h_b, x2b) * dec
                        dmtd = _nt(x2h_b, dy2b) * dect
                        acc_dg = acc_dg + dmd
                        acc_dgt = acc_dgt + dmtd
                        w_row = jnp.sum(dmd * gmat, axis=1, keepdims=True)
                        wt_row = jnp.sum(dmtd * gmat_t, axis=1, keepdims=True)
                        fcol = f_c[:, hh:hh + 1]
                        dx_h = _nn((gmat_t * dect).astype(BF16), dy2b) + _nn((bm * fcol).astype(BF16), dhn_b)
                        qh = _nt(x2h_b, dhn_b)
                        d_b = d_b + fcol * qh
                        dff = jnp.sum(bm * qh, axis=1, keepdims=True) * fcol
                        yo_row = jnp.sum(jnp.where(msk, t_yo, 0.0), axis=1, keepdims=True)
                        dlam_h = w_row - wt_row + yo_row - dff
                        hh_sum = jnp.sum(jnp.sum(jnp.where(msk, t_hh, 0.0), axis=1, keepdims=True), axis=0, keepdims=True)
                        last_h = cd[:, hh:hh + 1] * hh_sum + jnp.sum(dff, axis=0, keepdims=True)
                        dlam = jnp.where(lane == hh, dlam_h, dlam)
                        dlast = jnp.where(lane1 == hh, last_h, dlast)
                        dx2 = jnp.where(msk, dx_h, dx2)
                    dxbc_ref[rows, lo:lo + LANE] = dx2 * dt2 + dy2 * dsk_ref[:, lo:lo + LANE]
                    prod = dx2 * xs
                    for hh, msk in ((ha, lane_lo), (hb_, jnp.logical_not(lane_lo))):
                        col = jnp.sum(jnp.where(msk, prod, 0.0), axis=1, keepdims=True)
                        ddt_x = jnp.where(lane == hh, col, ddt_x)
                    dsk_parts.append(_colsum(dy2 * xs))
                d_c = d_c + _nn(acc_dg.astype(BF16), bm_b)
                d_b = d_b + _nn(acc_dgt.astype(BF16), cm_b)
                dxbc_ref[rows, bo:bo + SSM_N] = d_b
                dxbc_ref[rows, co:co + SSM_N] = d_c
            dlam = dlam + jnp.where(row_last, dlast, 0.0)
            da = _tri_left(triu, dlam)
            ddt_ref[rows, :] = da * a_neg_v + ddt_x
            return dsk_acc + jnp.concatenate(dsk_parts, axis=1), da_acc + _colsum(da * dtc)

        dsk_tot, da_tot = lax.fori_loop(
            0, cpb, chunk, (jnp.zeros((1, SSM_WIDTH), F32), jnp.zeros((1, LANE), F32)))
        _acc_rows(ddsk_ref, dsk_tot)
        _acc_rows(da_ref, da_tot)

    rev = lambda i: (nb - 1 - i, 0)
    rrows = lambda w: pl.BlockSpec((tm, w), rev)
    return pl.pallas_call(
        body, name="ssd_bwd", grid=(nb,),
        in_specs=[rrows(SSM_WIDTH), rrows(SSM_WIDTH), rrows(SSM_WIDTH), rrows(CONV_CH), rrows(LANE),
                  pl.BlockSpec((cpb, SSM_N, SSM_WIDTH), lambda i: (nb - 1 - i, 0, 0)),
                  _whole((1, LANE)), _whole((1, SSM_WIDTH)), _whole((1, SSM_WIDTH)),
                  _whole((CHUNK, CHUNK)), _whole((CHUNK, CHUNK))],
        out_specs=[rrows(CONV_CH), rrows(LANE), rrows(SSM_WIDTH),
                   _whole((1, SSM_WIDTH)), _whole((1, SSM_WIDTH)), _whole((1, LANE))],
        out_shape=[jax.ShapeDtypeStruct((s, CONV_CH), F32), jax.ShapeDtypeStruct((s, LANE), F32),
                   jax.ShapeDtypeStruct((s, SSM_WIDTH), F32), jax.ShapeDtypeStruct((1, SSM_WIDTH), F32),
                   jax.ShapeDtypeStruct((1, SSM_WIDTH), F32), jax.ShapeDtypeStruct((1, LANE), F32)],
        scratch_shapes=[pltpu.VMEM((SSM_N, SSM_WIDTH), F32), pltpu.VMEM((tm, SSM_WIDTH), F32)],
        compiler_params=_cparams("arbitrary"),
    )(dos, y, z, xbc, dt, hin, a_neg, dskip_x, g_x, tril, triu)


def _ssd_post_bwd(xbc_raw, dxa, ddt, dt_raw, conv_w, conv_b, dt_bias_p):
    s = xbc_raw.shape[0]
    tm = ROW_TILE
    hb = tm // HALO
    nt = s // tm
    ext = tm + HALO

    def body(x_ref, prev_ref, next_ref, d_ref, dnext_ref, ddt_ref, dtr_ref, w_ref, b_ref, db_ref,
             dx_ref, ddtr_ref, dw_ref, dcb_ref, ddb_ref, xe_sc, de_sc):
        i = pl.program_id(0)
        w = w_ref[...]
        xe_sc[pl.ds(0, HALO), :] = jnp.where(i > 0, prev_ref[...], 0.0)
        xe_sc[pl.ds(HALO, tm), :] = x_ref[...]
        xe_sc[pl.ds(HALO + tm, HALO), :] = next_ref[...]
        pre = _conv_taps(xe_sc, w, ext, 0) + b_ref[...]
        sg = _silu_grad(pre)
        de_sc[pl.ds(0, tm), :] = d_ref[...] * sg[:tm]
        de_sc[pl.ds(tm, HALO), :] = jnp.where(i < nt - 1, dnext_ref[...] * sg[tm:], 0.0)
        dconv = de_sc[pl.ds(0, tm), :]
        acc = None
        dws = []
        for k in range(CONV_K):
            term = de_sc[pl.ds(CONV_K - 1 - k, tm), :] * w[k:k + 1, :]
            acc = term if acc is None else acc + term
            dws.append(_colsum(dconv * xe_sc[pl.ds(HALO - (CONV_K - 1) + k, tm), :]))
        dx_ref[...] = acc
        _acc_rows(dw_ref, jnp.concatenate(dws, axis=0))
        _acc_rows(dcb_ref, _colsum(dconv))
        ddtr = ddt_ref[...] * _sigmoid(dtr_ref[...] + db_ref[...])
        ddtr_ref[...] = ddtr
        _acc_rows(ddb_ref, _colsum(ddtr))

    halo_prev = pl.BlockSpec((HALO, CONV_CH), lambda i: (jnp.maximum(i * hb - 1, 0), 0))
    halo_next = pl.BlockSpec((HALO, CONV_CH), lambda i: (jnp.minimum((i + 1) * hb, s // HALO - 1), 0))
    return pl.pallas_call(
        body, name="ssd_post_bwd", grid=(nt,),
        in_specs=[_rows(tm, CONV_CH), halo_prev, halo_next, _rows(tm, CONV_CH), halo_next, _rows(tm, LANE),
                  _rows(tm, LANE), _whole((CONV_K, CONV_CH)), _whole((1, CONV_CH)), _whole((1, LANE))],
        out_specs=[_rows(tm, CONV_CH), _rows(tm, LANE), _whole((CONV_K, CONV_CH)), _whole((1, CONV_CH)),
                   _whole((1, LANE))],
        out_shape=[jax.ShapeDtypeStruct((s, CONV_CH), F32), jax.ShapeDtypeStruct((s, LANE), F32),
                   jax.ShapeDtypeStruct((CONV_K, CONV_CH), F32), jax.ShapeDtypeStruct((1, CONV_CH), F32),
                   jax.ShapeDtypeStruct((1, LANE), F32)],
        scratch_shapes=[pltpu.VMEM((tm + 2 * HALO, CONV_CH), F32), pltpu.VMEM((ext, CONV_CH), F32)],
        compiler_params=_cparams("arbitrary"),
    )(xbc_raw, xbc_raw, xbc_raw, dxa, dxa, ddt, dt_raw, conv_w, conv_b, dt_bias_p)


def _qbwd(dq_att, q_lat, g_q, w_qb_p, cos, sin):
    s = q_lat.shape[0]
    tm = ROW_TILE
    wq = MLA_HEADS * HEAD_PAD

    def body(dq_ref, ql_ref, g_ref, w_ref, cos_ref, sin_ref, dql_ref, draw_ref, dg_ref):
        c, sn = cos_ref[...], sin_ref[...]
        for h in range(MLA_HEADS):
            o = h * HEAD_PAD
            draw_ref[:, o:o + QK_NOPE] = dq_ref[:, o:o + QK_NOPE].astype(BF16)
            draw_ref[:, o + QK_NOPE:o + HEAD_PAD] = _rope_transposed(dq_ref[:, o + QK_NOPE:o + HEAD_PAD], c, sn).astype(BF16)
        dn = _nt(draw_ref[...], w_ref[...])
        xhat, rstd = _rms(ql_ref[...])
        _acc_rows(dg_ref, _colsum(dn * xhat))
        dql_ref[...] = _rms_bwd(dn * g_ref[...], xhat, rstd)

    return pl.pallas_call(
        body, name="qbwd", grid=(s // tm,),
        in_specs=[_rows(tm, wq), _rows(tm, Q_RANK), _whole((1, Q_RANK)), _whole((Q_RANK, wq)),
                  _rows(tm, LANE), _rows(tm, LANE)],
        out_specs=[_rows(tm, Q_RANK), _rows(tm, wq), _whole((1, Q_RANK))],
        out_shape=[jax.ShapeDtypeStruct((s, Q_RANK), F32), jax.ShapeDtypeStruct((s, wq), BF16),
                   jax.ShapeDtypeStruct((1, Q_RANK), F32)],
        compiler_params=_cparams("arbitrary"),
    )(dq_att, q_lat, g_q, w_qb_p, cos, sin)


def _kvbwd(dk_att, dv, kv_lat, g_kv, w_kvb_p, cos, sin):
    s = kv_lat.shape[0]
    tm = ROW_TILE
    wk = MLA_HEADS * HEAD_PAD
    wr = MLA_HEADS * (QK_NOPE + V_DIM)

    def body(dk_ref, dv_ref, kl_ref, g_ref, w_ref, cos_ref, sin_ref, dkl_ref, draw_ref, dg_ref):
        dkr = None
        for h in range(MLA_HEADS):
            o = h * HEAD_PAD
            draw_ref[:, h * QK_NOPE:(h + 1) * QK_NOPE] = dk_ref[:, o:o + QK_NOPE].astype(BF16)
            part = dk_ref[:, o + QK_NOPE:o + HEAD_PAD]
            dkr = part if dkr is None else dkr + part
        draw_ref[:, MLA_HEADS * QK_NOPE:] = dv_ref[...].astype(BF16)
        dn = _nt(draw_ref[...], w_ref[...])
        xhat, rstd = _rms(kl_ref[:, :KV_RANK])
        _acc_rows(dg_ref, _colsum(dn * xhat))
        dkl_ref[:, :KV_RANK] = _rms_bwd(dn * g_ref[...], xhat, rstd)
        dkl_ref[:, KV_RANK:] = _rope_transposed(dkr, cos_ref[...], sin_ref[...])

    return pl.pallas_call(
        body, name="kvbwd", grid=(s // tm,),
        in_specs=[_rows(tm, wk), _rows(tm, MLA_WIDTH), _rows(tm, KV_LAT_PAD), _whole((1, KV_RANK)),
                  _whole((KV_RANK, wr)), _rows(tm, LANE), _rows(tm, LANE)],
        out_specs=[_rows(tm, KV_LAT_PAD), _rows(tm, wr), _whole((1, KV_RANK))],
        out_shape=[jax.ShapeDtypeStruct((s, KV_LAT_PAD), F32), jax.ShapeDtypeStruct((s, wr), BF16),
                   jax.ShapeDtypeStruct((1, KV_RANK), F32)],
        compiler_params=_cparams("arbitrary"),
    )(dk_att, dv, kv_lat, g_kv, w_kvb_p, cos, sin)


def _inproj_bwd(pieces, w_in_p, x, scale1p, gx1):
    s = x.shape[0]
    tm = ROW_TILE

    def body(*refs):
        p_refs = refs[:len(IN_PAD)]
        w_ref, x_ref, sc_ref, gx1_ref, gx_ref, dp_ref, dsc_ref, dsh_ref = refs[len(IN_PAD):]
        off = 0
        for ref, w in zip(p_refs, IN_PAD):
            dp_ref[:, off:off + w] = ref[...].astype(BF16)
            off += w
        du = _nt(dp_ref[...], w_ref[...])
        gx_ref[...] = gx1_ref[...] + du * sc_ref[...]
        _acc_rows(dsc_ref, _colsum(du * x_ref[...]))
        _acc_rows(dsh_ref, _colsum(du))

    vec = _whole((1, D_MODEL))
    return pl.pallas_call(
        body, name="inproj_bwd", grid=(s // tm,),
        in_specs=[_rows(tm, w) for w in IN_PAD] + [_whole((D_MODEL, IN_PAD_WIDTH)), _rows(tm, D_MODEL), vec,
                                                    _rows(tm, D_MODEL)],
        out_specs=[_rows(tm, D_MODEL), _rows(tm, IN_PAD_WIDTH), vec, vec],
        out_shape=[jax.ShapeDtypeStruct((s, D_MODEL), F32), jax.ShapeDtypeStruct((s, IN_PAD_WIDTH), BF16),
                   jax.ShapeDtypeStruct((1, D_MODEL), F32), jax.ShapeDtypeStruct((1, D_MODEL), F32)],
        compiler_params=_cparams("arbitrary"),
    )(*pieces, w_in_p, x, scale1p, gx1)


def _matmul_tn(name, a, b, tn):
    s, k = a.shape
    n = b.shape[1]
    tm = min(ATTN_TILE, s)

    def body(a_ref, b_ref, o_ref):
        @pl.when(pl.program_id(1) == 0)
        def _():
            o_ref[...] = jnp.zeros_like(o_ref)
        o_ref[...] += _tn(a_ref[...], b_ref[...])

    return pl.pallas_call(
        body, name=name, grid=(n // tn, s // tm),
        in_specs=[pl.BlockSpec((tm, k), lambda j, i: (i, 0)), pl.BlockSpec((tm, tn), lambda j, i: (i, j))],
        out_specs=pl.BlockSpec((k, tn), lambda j, i: (0, j)),
        out_shape=jax.ShapeDtypeStruct((k, n), F32),
        compiler_params=_cparams("parallel", "arbitrary"),
    )(a, b)


def _pad_cols(a, width):
    return jnp.pad(a, ((0, 0), (0, width - a.shape[1])))


def _pack_w_in(w_in):
    parts, off = [], 0
    for w, wp in zip(IN_SPLITS, IN_PAD):
        parts.append(_pad_cols(w_in[:, off:off + w], wp))
        off += w
    return jnp.concatenate(parts, axis=1)


def _unpack_w_in(g):
    parts, off = [], 0
    for w, wp in zip(IN_SPLITS, IN_PAD):
        parts.append(g[:, off:off + w])
        off += wp
    return jnp.concatenate(parts, axis=1)


def _pack_w_qb(w_qb):
    w = w_qb.reshape(Q_RANK, MLA_HEADS, QK_HEAD)
    return jnp.pad(w, ((0, 0), (0, 0), (0, HEAD_PAD - QK_HEAD))).reshape(Q_RANK, MLA_HEADS * HEAD_PAD)


def _unpack_w_qb(g):
    return g.reshape(Q_RANK, MLA_HEADS, HEAD_PAD)[:, :, :QK_HEAD].reshape(Q_RANK, MLA_HEADS * QK_HEAD)


def _pack_w_kvb(w_kvb):
    w = w_kvb.reshape(KV_RANK, MLA_HEADS, QK_NOPE + V_DIM)
    return jnp.concatenate([w[:, :, :QK_NOPE].reshape(KV_RANK, -1), w[:, :, QK_NOPE:].reshape(KV_RANK, -1)], axis=1)


def _unpack_w_kvb(g):
    gk = g[:, :MLA_HEADS * QK_NOPE].reshape(KV_RANK, MLA_HEADS, QK_NOPE)
    gv = g[:, MLA_HEADS * QK_NOPE:].reshape(KV_RANK, MLA_HEADS, V_DIM)
    return jnp.concatenate([gk, gv], axis=2).reshape(KV_RANK, -1)


def _rope_tables(positions):
    inv_freq = 1.0 / (ROPE_THETA ** (jnp.arange(ROPE_HALF, dtype=F32) / ROPE_HALF))
    ang = positions.astype(F32)[:, None] * inv_freq
    cos, sin = jnp.cos(ang), jnp.sin(ang)
    zeros = jnp.zeros((positions.shape[0], LANE - QK_ROPE), F32)
    return jnp.concatenate([cos, cos, zeros], axis=1), jnp.concatenate([-sin, sin, zeros], axis=1)


def _per_query_rows(a):
    s = a.shape[0]
    return a.reshape(s, MLA_HEADS, LANE)[:, :, 0].T.reshape(MLA_HEADS, 1, s)


def _local_step(x, tgt, positions, mod, w_in, q_norm_g, w_qb, kv_norm_g, w_kvb, conv_w, conv_b, dt_bias,
                a_log, d_skip, ssm_norm_g, w_out, ln_g, ln_b):
    row = lambda v: v.reshape(1, -1)
    shift, scale, gate = mod[:D_MODEL], mod[D_MODEL:2 * D_MODEL], mod[2 * D_MODEL:]
    scale1p = row(1.0 + scale)
    w_in_p = _pack_w_in(w_in).astype(BF16)
    w_qb_p = _pack_w_qb(w_qb).astype(BF16)
    w_kvb_p = _pack_w_kvb(w_kvb).astype(BF16)
    w_out_b = w_out.astype(BF16)
    cos, sin = _rope_tables(positions)
    a_neg = row(jnp.pad(-jnp.exp(a_log), (0, LANE - SSM_HEADS)))
    dskip_x = row(jnp.repeat(d_skip, SSM_P))
    dt_bias_p = row(jnp.pad(dt_bias, (0, LANE - SSM_HEADS)))
    tri = jnp.tril(jnp.ones((CHUNK, CHUNK), F32))
    tril, triu = tri.astype(BF16), tri.T.astype(BF16)

    u_bf, q_lat, kv_lat, z_attn, xbc_raw, dt_raw, z_ssm = _inproj(x, scale1p, row(shift), w_in_p)
    nq_bf, q_att = _qpath(q_lat, row(q_norm_g), w_qb_p, cos, sin)
    nkv_bf, k_att, v_att = _kvpath(kv_lat, row(kv_norm_g), w_kvb_p, cos, sin)
    o, lse = _attn_fwd(q_att, k_att, v_att)
    xbc, dt = _ssd_pre(xbc_raw, dt_raw, conv_w, row(conv_b), dt_bias_p)
    y, o_ssm, hin = _ssd_fwd(xbc, dt, z_ssm, a_neg, dskip_x, row(ssm_norm_g), tril, triu)
    (cat_bf, dmix_bf, gx1, do_bf, dz_attn, delta, dos, loss, d_ln_g, d_ln_b, d_gate) = _outln(
        o, z_attn, o_ssm, w_out_b, x, row(gate), row(ln_g), row(ln_b), tgt)

    g_w_out = _matmul_tn("gw_out", cat_bf, dmix_bf, 512)
    dq_att = _attn_dq(q_att, k_att, v_att, do_bf, lse, delta)
    dk_att, dv = _attn_dkv(q_att, k_att, v_att, do_bf, _per_query_rows(lse), _per_query_rows(delta))
    dq_lat, dqraw_bf, d_q_norm_g = _qbwd(dq_att, q_lat, row(q_norm_g), w_qb_p, cos, sin)
    dkv_lat, dkvraw_bf, d_kv_norm_g = _kvbwd(dk_att, dv, kv_lat, row(kv_norm_g), w_kvb_p, cos, sin)
    g_w_qb = _unpack_w_qb(_matmul_tn("gw_qb", nq_bf, dqraw_bf, MLA_HEADS * HEAD_PAD))
    g_w_kvb = _unpack_w_kvb(_matmul_tn("gw_kvb", nkv_bf, dkvraw_bf, MLA_HEADS * (QK_NOPE + V_DIM)))
    dxa, ddt, dz_ssm, d_ssm_g, ddsk_x, d_a = _ssd_bwd(dos, y, z_ssm, xbc, dt, hin, a_neg, dskip_x, row(ssm_norm_g),
                                                       tril, triu)
    dxbc_raw, ddt_raw, d_conv_w, d_conv_b, d_dt_bias = _ssd_post_bwd(xbc_raw, dxa, ddt, dt_raw, conv_w, row(conv_b),
                                                                     dt_bias_p)
    grad_x, dproj_bf, d_scale, d_shift = _inproj_bwd((dq_lat, dkv_lat, dz_attn, dxbc_raw, ddt_raw, dz_ssm),
                                                     w_in_p, x, scale1p, gx1)
    g_w_in = _unpack_w_in(_matmul_tn("gw_in", u_bf, dproj_bf, 640))
    return dict(
        loss=loss[0, 0], grad_x=grad_x,
        dmod=jnp.concatenate([d_shift[0], d_scale[0], d_gate[0]]),
        w_in=g_w_in, q_norm_g=d_q_norm_g[0], w_qb=g_w_qb, kv_norm_g=d_kv_norm_g[0], w_kvb=g_w_kvb,
        conv_w=d_conv_w, conv_b=d_conv_b[0], dt_bias=d_dt_bias[0, :SSM_HEADS],
        a_log=d_a[0, :SSM_HEADS] * a_neg[0, :SSM_HEADS],
        d_skip=ddsk_x.reshape(SSM_HEADS, SSM_P).sum(axis=1), ssm_norm_g=d_ssm_g[0], w_out=g_w_out,
        ln_g=d_ln_g[0], ln_b=d_ln_b[0])


ADAM_ROWS = 512


def _my_index():
    return 4 * lax.axis_index("x") + 2 * lax.axis_index("y") + lax.axis_index("c")


def _exchange(name, send, gather):
    r = send.shape[-2]

    def body(send_ref, recv_ref, send_sems, recv_sems, local_sem):
        x, y, c = lax.axis_index("x"), lax.axis_index("y"), lax.axis_index("c")
        me = 4 * x + 2 * y + c

        def src(idx):
            return send_ref if gather else send_ref.at[idx]

        own = pltpu.make_async_copy(src(me), recv_ref.at[me], local_sem)
        own.start()
        copies = []
        for k in range(1, N_DEV):
            px, py, pc = x ^ ((k >> 2) & 1), y ^ ((k >> 1) & 1), c ^ (k & 1)
            peer = 4 * px + 2 * py + pc
            copies.append(pltpu.make_async_remote_copy(
                src_ref=src(peer), dst_ref=recv_ref.at[me],
                send_sem=send_sems.at[k - 1], recv_sem=recv_sems.at[k - 1],
                device_id=(px, py, pc), device_id_type=pl.DeviceIdType.MESH))
        for cp in copies:
            cp.start()
        for cp in copies:
            cp.wait()
        own.wait()

    return pl.pallas_call(
        body, name=name,
        in_specs=[pl.BlockSpec(memory_space=pl.ANY)], out_specs=pl.BlockSpec(memory_space=pl.ANY),
        out_shape=jax.ShapeDtypeStruct((N_DEV, r, LANE), send.dtype),
        scratch_shapes=[pltpu.SemaphoreType.DMA((N_DEV - 1,)), pltpu.SemaphoreType.DMA((N_DEV - 1,)),
                        pltpu.SemaphoreType.DMA],
    )(send)


def _flat_rows(parts, row_multiple):
    flat = jnp.concatenate([p.reshape(-1) for p in parts])
    chunk = row_multiple * LANE
    total = -(-flat.shape[0] // chunk) * chunk
    return jnp.pad(flat, (0, total - flat.shape[0])).reshape(-1, LANE)


def _unflat(flat, shapes):
    flat = flat.reshape(-1)
    out, off = [], 0
    for shp in shapes:
        n = math.prod(shp)
        out.append(flat[off:off + n].reshape(shp))
        off += n
    return out


def _adam_update(g, w, m, v):
    m2 = ADAM_B1 * m + (1.0 - ADAM_B1) * g
    v2 = ADAM_B2 * v + (1.0 - ADAM_B2) * (g * g)
    m_hat = m2 / (1.0 - ADAM_B1 ** ADAM_STEP)
    v_hat = v2 / (1.0 - ADAM_B2 ** ADAM_STEP)
    delta = -ADAM_LR * (m_hat / (jnp.sqrt(v_hat) + ADAM_EPS) + ADAM_WD * w)
    return delta, m2, v2


def _adamw_summed(name, parts, w, m, v):
    r = w.shape[0]
    tr = min(ADAM_ROWS, r)

    def body(p_ref, w_ref, m_ref, v_ref, g_ref, d_ref, m2_ref, v2_ref):
        g = p_ref[0]
        for j in range(1, N_DEV):
            g = g + p_ref[j]
        g_ref[...] = g
        d_ref[...], m2_ref[...], v2_ref[...] = _adam_update(g, w_ref[...], m_ref[...], v_ref[...])

    rows = _rows(tr, LANE)
    return pl.pallas_call(
        body, name=name, grid=(r // tr,),
        in_specs=[pl.BlockSpec((N_DEV, tr, LANE), lambda i: (0, i, 0)), rows, rows, rows],
        out_specs=[rows] * 4, out_shape=[jax.ShapeDtypeStruct((r, LANE), F32)] * 4,
        compiler_params=_cparams("parallel"),
    )(parts, w, m, v)


def _modpart(c_all, w_ada, b_cols):
    def body(c_ref, w_ref, b_ref, o_ref):
        o_ref[...] = _nn(c_ref[...].astype(BF16), w_ref[...].astype(BF16)) + b_ref[...]

    return pl.pallas_call(
        body, name="modpart", out_shape=jax.ShapeDtypeStruct((N_DEV, w_ada.shape[1]), F32),
    )(c_all, w_ada, b_cols)


def _adamw_w_ada(c_all_t, dmod_cols, w, m, v):
    def body(c_ref, d_ref, w_ref, m_ref, v_ref, g_ref, dl_ref, m2_ref, v2_ref):
        g = c_ref[:, 0:1] * d_ref[0:1, :]
        for b in range(1, N_DEV):
            g = g + c_ref[:, b:b + 1] * d_ref[b:b + 1, :]
        g_ref[...] = g
        dl_ref[...], m2_ref[...], v2_ref[...] = _adam_update(g, w_ref[...], m_ref[...], v_ref[...])

    return pl.pallas_call(
        body, name="adamw_w_ada", out_shape=[jax.ShapeDtypeStruct(w.shape, F32)] * 4,
        compiler_params=pltpu.CompilerParams(vmem_limit_bytes=VMEM_LIMIT),
    )(c_all_t, dmod_cols, w, m, v)


SHARDED = ("w_in", "w_qb", "w_kvb", "w_out", "conv_w")
REPLICATED = ("b_ada", "q_norm_g", "kv_norm_g", "conv_b", "dt_bias", "a_log", "d_skip", "ssm_norm_g", "ln_g", "ln_b")
WEIGHTS = ("w_ada", "b_ada", "w_in", "q_norm_g", "w_qb", "kv_norm_g", "w_kvb", "conv_w", "conv_b", "dt_bias",
           "a_log", "d_skip", "ssm_norm_g", "w_out", "ln_g", "ln_b")


def _column_blocks(g, name):
    if name == "w_out":
        return g.reshape(N_DEV, g.shape[0] // N_DEV, g.shape[1])
    rows, cols = g.shape
    return g.reshape(rows, N_DEV, cols // N_DEV).transpose(1, 0, 2)


def _from_blocks(blocks, name):
    if name == "w_out":
        return blocks.reshape(-1, blocks.shape[-1])
    n, rows, cols = blocks.shape
    return blocks.transpose(1, 0, 2).reshape(rows, n * cols)


def kernel(x, c, positions, w_ada, b_ada, w_in, q_norm_g, w_qb, kv_norm_g, w_kvb, conv_w, conv_b, dt_bias, a_log, d_skip, ssm_norm_g, w_out, ln_g, ln_b, loss_target, m_w_ada, m_b_ada, m_w_in, m_q_norm_g, m_w_qb, m_kv_norm_g, m_w_kvb, m_conv_w, m_conv_b, m_dt_bias, m_a_log, m_d_skip, m_ssm_norm_g, m_w_out, m_ln_g, m_ln_b, v_w_ada, v_b_ada, v_w_in, v_q_norm_g, v_w_qb, v_kv_norm_g, v_w_kvb, v_conv_w, v_conv_b, v_dt_bias, v_a_log, v_d_skip, v_ssm_norm_g, v_w_out, v_ln_g, v_ln_b):
    given = dict(w_ada=w_ada, b_ada=b_ada, w_in=w_in, q_norm_g=q_norm_g, w_qb=w_qb, kv_norm_g=kv_norm_g, w_kvb=w_kvb,
                 conv_w=conv_w, conv_b=conv_b, dt_bias=dt_bias, a_log=a_log, d_skip=d_skip, ssm_norm_g=ssm_norm_g,
                 w_out=w_out, ln_g=ln_g, ln_b=ln_b)
    mom = dict(w_ada=m_w_ada, b_ada=m_b_ada, w_in=m_w_in, q_norm_g=m_q_norm_g, w_qb=m_w_qb, kv_norm_g=m_kv_norm_g,
               w_kvb=m_w_kvb, conv_w=m_conv_w, conv_b=m_conv_b, dt_bias=m_dt_bias, a_log=m_a_log, d_skip=m_d_skip,
               ssm_norm_g=m_ssm_norm_g, w_out=m_w_out, ln_g=m_ln_g, ln_b=m_ln_b)
    var = dict(w_ada=v_w_ada, b_ada=v_b_ada, w_in=v_w_in, q_norm_g=v_q_norm_g, w_qb=v_w_qb, kv_norm_g=v_kv_norm_g,
               w_kvb=v_w_kvb, conv_w=v_conv_w, conv_b=v_conv_b, dt_bias=v_dt_bias, a_log=v_a_log, d_skip=v_d_skip,
               ssm_norm_g=v_ssm_norm_g, w_out=v_w_out, ln_g=v_ln_g, ln_b=v_ln_b)
    w0 = {k: a[0] for k, a in given.items()}
    m0 = {k: a[0] for k, a in mom.items()}
    v0 = {k: a[0] for k, a in var.items()}
    me = _my_index()

    shard_shapes = [w0[k].shape for k in SHARDED] + [(D_MODEL,)]
    gathered = _exchange("gather_weights", _flat_rows([w0[k] for k in SHARDED] + [c[0]], HALO), gather=True)
    per_dev = [_unflat(gathered[j], shard_shapes) for j in range(N_DEV)]
    full = {k: _from_blocks(jnp.stack([per_dev[j][i] for j in range(N_DEV)]), k) for i, k in enumerate(SHARDED)}
    c_all = jnp.stack([per_dev[j][-1] for j in range(N_DEV)])

    ada_cols = w0["w_ada"].shape[1]
    b_cols = lax.dynamic_slice(w0["b_ada"], (me * ada_cols,), (ada_cols,)).reshape(1, ada_cols)
    mod_all = _exchange("gather_mod", _flat_rows([_modpart(c_all, w0["w_ada"], b_cols)], HALO), gather=True)
    mod_all = mod_all.reshape(N_DEV, -1)[:, :N_DEV * ada_cols].reshape(N_DEV, N_DEV, ada_cols)
    mod = lax.dynamic_index_in_dim(mod_all, me, axis=1, keepdims=False).reshape(-1)

    loc = _local_step(x[0], loss_target[0], positions[0], mod, full["w_in"], w0["q_norm_g"], full["w_qb"],
                      w0["kv_norm_g"], full["w_kvb"], full["conv_w"], w0["conv_b"], w0["dt_bias"], w0["a_log"],
                      w0["d_skip"], w0["ssm_norm_g"], full["w_out"], w0["ln_g"], w0["ln_b"])

    rep_shapes = [w0[k].shape for k in REPLICATED] + [(1,)]
    rep_local = [loc["dmod"]] + [loc[k] for k in REPLICATED[1:]] + [loc["loss"].reshape(1)]
    rep_parts = _exchange("gather_small", _flat_rows(rep_local, HALO), gather=True)
    zero1 = jnp.zeros((1,), F32)
    rep = _adamw_summed("adamw_replicated", rep_parts,
                        _flat_rows([w0[k] for k in REPLICATED] + [zero1], HALO),
                        _flat_rows([m0[k] for k in REPLICATED] + [zero1], HALO),
                        _flat_rows([v0[k] for k in REPLICATED] + [zero1], HALO))
    rep_g, rep_d, rep_m, rep_v = [_unflat(a, rep_shapes) for a in rep]
    loss = rep_g[-1][0]

    dmod_all = rep_parts.reshape(N_DEV, -1)[:, :3 * D_MODEL]
    dmod_cols = lax.dynamic_slice(dmod_all, (0, me * ada_cols), (N_DEV, ada_cols))
    ada = _adamw_w_ada(c_all.T, dmod_cols, w0["w_ada"], m0["w_ada"], v0["w_ada"])

    blocks = [_column_blocks(loc[k], k).reshape(N_DEV, -1) for k in SHARDED]
    send = jnp.concatenate(blocks, axis=1)
    chunk = ADAM_ROWS * LANE
    total = -(-send.shape[1] // chunk) * chunk
    send = jnp.pad(send, ((0, 0), (0, total - send.shape[1]))).reshape(N_DEV, -1, LANE)
    recv = _exchange("scatter_grads", send, gather=False)
    shd_shapes = [w0[k].shape for k in SHARDED]
    shd = _adamw_summed("adamw_sharded", recv,
                        _flat_rows([w0[k] for k in SHARDED], ADAM_ROWS),
                        _flat_rows([m0[k] for k in SHARDED], ADAM_ROWS),
                        _flat_rows([v0[k] for k in SHARDED], ADAM_ROWS))
    shd_g, shd_d, shd_m, shd_v = [_unflat(a, shd_shapes) for a in shd]

    def collect(idx):
        out = {"w_ada": ada[idx]}
        out.update({k: (rep_g, rep_d, rep_m, rep_v)[idx][i] for i, k in enumerate(REPLICATED)})
        out.update({k: (shd_g, shd_d, shd_m, shd_v)[idx][i] for i, k in enumerate(SHARDED)})
        return [out[k][None] for k in WEIGHTS]

    return (loss, loc["grad_x"][None], *collect(0), *collect(1), *collect(2), *collect(3))
```

```python
import functools
import math

import jax
import jax.numpy as jnp
from jax import lax
from jax.experimental import pallas as pl
from jax.experimental.pallas import tpu as pltpu

F32 = jnp.float32
BF16 = jnp.bfloat16

N_DEV = 8
D_MODEL = 1024
MLA_HEADS = 8
QK_NOPE = 128
QK_ROPE = 64
V_DIM = 128
Q_RANK = 384
KV_RANK = 256
QK_HEAD = QK_NOPE + QK_ROPE
HEAD_PAD = 256
ROPE_HALF = QK_ROPE // 2
ROPE_THETA = 10000.0
MLA_WIDTH = MLA_HEADS * V_DIM
SSM_HEADS = 16
SSM_P = 64
SSM_WIDTH = SSM_HEADS * SSM_P
SSM_GROUPS = 2
SSM_N = 128
CONV_K = 4
CHUNK = 128
CONV_CH = SSM_WIDTH + 2 * SSM_GROUPS * SSM_N
MIX_WIDTH = MLA_WIDTH + SSM_WIDTH
IN_SPLITS = (Q_RANK, KV_RANK + QK_ROPE, MLA_WIDTH, CONV_CH, SSM_HEADS, SSM_WIDTH)
IN_WIDTH = sum(IN_SPLITS)
LANE = 128
KV_LAT_PAD = KV_RANK + LANE
IN_PAD = (Q_RANK, KV_LAT_PAD, MLA_WIDTH, CONV_CH, LANE, SSM_WIDTH)
IN_PAD_WIDTH = sum(IN_PAD)
DEEPNORM_ALPHA = 2.0 ** 0.25
RMS_EPS = 1e-6
LN_EPS = 1e-5
ATTN_SCALE = QK_HEAD ** -0.5
LOG2E = math.log2(math.e)
LN2 = math.log(2.0)
Q_PRESCALE = ATTN_SCALE * LOG2E
ADAM_LR, ADAM_B1, ADAM_B2, ADAM_EPS, ADAM_WD, ADAM_STEP = 0.001, 0.9, 0.999, 1e-08, 0.01, 10

ROW_TILE = 256
ATTN_TILE = 512
SSD_ROWS = 512
VMEM_LIMIT = 56 * 1024 * 1024


def _nn(a, b):
    return jnp.dot(a, b, preferred_element_type=F32)


def _nt(a, b):
    return lax.dot_general(a, b, (((1,), (1,)), ((), ())), preferred_element_type=F32)


def _tn(a, b):
    return lax.dot_general(a, b, (((0,), (0,)), ((), ())), preferred_element_type=F32)


def _cparams(*sem):
    return pltpu.CompilerParams(dimension_semantics=sem, vmem_limit_bytes=VMEM_LIMIT)


def _rows(tm, w):
    return pl.BlockSpec((tm, w), lambda i: (i, 0))


def _whole(shape):
    return pl.BlockSpec(shape, lambda i: (0,) * len(shape))


def _sigmoid(z):
    return 1.0 / (1.0 + jnp.exp(-z))


def _lane_iota(shape):
    return lax.broadcasted_iota(jnp.int32, shape, len(shape) - 1)


def _swap_halves(r):
    lane = _lane_iota(r.shape)
    return jnp.where(lane < ROPE_HALF, pltpu.roll(r, LANE - ROPE_HALF, 1),
                     jnp.where(lane < QK_ROPE, pltpu.roll(r, ROPE_HALF, 1), 0.0))


def _rope(r, cos, sin):
    return r * cos + _swap_halves(r) * sin


def _rope_transposed(d, cos, sin):
    return d * cos + _swap_halves(d * sin)


def _rms(x):
    rstd = lax.rsqrt(jnp.mean(x * x, axis=-1, keepdims=True) + RMS_EPS)
    return x * rstd, rstd


def _rms_bwd(dxhat, xhat, rstd):
    return rstd * (dxhat - xhat * jnp.mean(dxhat * xhat, axis=-1, keepdims=True))


def _acc_rows(ref, val):
    @pl.when(pl.program_id(0) == 0)
    def _():
        ref[...] = jnp.zeros_like(ref)
    ref[...] += val


def _colsum(v):
    return jnp.sum(v, axis=0, keepdims=True)


def _inproj(x, scale1p, shift, w_in_p):
    s = x.shape[0]
    tm = ROW_TILE

    def body(x_ref, sc_ref, sh_ref, w_ref, u_ref, *outs):
        u = (x_ref[...] * sc_ref[...] + sh_ref[...]).astype(BF16)
        u_ref[...] = u
        proj = _nn(u, w_ref[...])
        off = 0
        for ref, w in zip(outs, IN_PAD):
            ref[...] = proj[:, off:off + w]
            off += w

    return pl.pallas_call(
        body, name="inproj", grid=(s // tm,),
        in_specs=[_rows(tm, D_MODEL), _whole((1, D_MODEL)), _whole((1, D_MODEL)), _whole((D_MODEL, IN_PAD_WIDTH))],
        out_specs=[_rows(tm, D_MODEL)] + [_rows(tm, w) for w in IN_PAD],
        out_shape=[jax.ShapeDtypeStruct((s, D_MODEL), BF16)] + [jax.ShapeDtypeStruct((s, w), F32) for w in IN_PAD],
        compiler_params=_cparams("parallel"),
    )(x, scale1p, shift, w_in_p)


def _qpath(q_lat, g_q, w_qb_p, cos, sin):
    s = q_lat.shape[0]
    tm = ROW_TILE

    def body(ql_ref, g_ref, w_ref, cos_ref, sin_ref, nq_ref, q_ref):
        xhat, _ = _rms(ql_ref[...])
        nq = (xhat * g_ref[...]).astype(BF16)
        nq_ref[...] = nq
        raw = _nn(nq, w_ref[...]) * Q_PRESCALE
        c, sn = cos_ref[...], sin_ref[...]
        for h in range(MLA_HEADS):
            o = h * HEAD_PAD
            q_ref[:, o:o + QK_NOPE] = raw[:, o:o + QK_NOPE].astype(BF16)
            q_ref[:, o + QK_NOPE:o + HEAD_PAD] = _rope(raw[:, o + QK_NOPE:o + HEAD_PAD], c, sn).astype(BF16)

    return pl.pallas_call(
        body, name="qpath", grid=(s // tm,),
        in_specs=[_rows(tm, Q_RANK), _whole((1, Q_RANK)), _whole((Q_RANK, MLA_HEADS * HEAD_PAD)),
                  _rows(tm, LANE), _rows(tm, LANE)],
        out_specs=[_rows(tm, Q_RANK), _rows(tm, MLA_HEADS * HEAD_PAD)],
        out_shape=[jax.ShapeDtypeStruct((s, Q_RANK), BF16), jax.ShapeDtypeStruct((s, MLA_HEADS * HEAD_PAD), BF16)],
        compiler_params=_cparams("parallel"),
    )(q_lat, g_q, w_qb_p, cos, sin)


def _kvpath(kv_lat, g_kv, w_kvb_p, cos, sin):
    s = kv_lat.shape[0]
    tm = ROW_TILE

    def body(kl_ref, g_ref, w_ref, cos_ref, sin_ref, nkv_ref, k_ref, v_ref):
        kl = kl_ref[...]
        xhat, _ = _rms(kl[:, :KV_RANK])
        nkv = (xhat * g_ref[...]).astype(BF16)
        nkv_ref[...] = nkv
        raw = _nn(nkv, w_ref[...])
        kr = _rope(kl[:, KV_RANK:], cos_ref[...], sin_ref[...]).astype(BF16)
        for h in range(MLA_HEADS):
            o = h * HEAD_PAD
            k_ref[:, o:o + QK_NOPE] = raw[:, h * QK_NOPE:(h + 1) * QK_NOPE].astype(BF16)
            k_ref[:, o + QK_NOPE:o + HEAD_PAD] = kr
        v_ref[...] = raw[:, MLA_HEADS * QK_NOPE:].astype(BF16)

    return pl.pallas_call(
        body, name="kvpath", grid=(s // tm,),
        in_specs=[_rows(tm, KV_LAT_PAD), _whole((1, KV_RANK)), _whole((KV_RANK, MLA_HEADS * (QK_NOPE + V_DIM))),
                  _rows(tm, LANE), _rows(tm, LANE)],
        out_specs=[_rows(tm, KV_RANK), _rows(tm, MLA_HEADS * HEAD_PAD), _rows(tm, MLA_WIDTH)],
        out_shape=[jax.ShapeDtypeStruct((s, KV_RANK), BF16), jax.ShapeDtypeStruct((s, MLA_HEADS * HEAD_PAD), BF16),
                   jax.ShapeDtypeStruct((s, MLA_WIDTH), BF16)],
        compiler_params=_cparams("parallel"),
    )(kv_lat, g_kv, w_kvb_p, cos, sin)


def _causal_mask(t):
    row = lax.broadcasted_iota(jnp.int32, (t, t), 0)
    col = lax.broadcasted_iota(jnp.int32, (t, t), 1)
    return row, col


def _attn_fwd(q, k, v):
    s = q.shape[0]
    t = min(ATTN_TILE, s)
    nq = s // t

    def body(q_ref, k_ref, v_ref, o_ref, lse_ref, m_sc, l_sc, acc_sc, sa_sc, sb_sc):
        i = pl.program_id(1)
        qv = q_ref[...]
        m_sc[...] = jnp.full(m_sc.shape, -jnp.inf, F32)
        l_sc[...] = jnp.zeros(l_sc.shape, F32)
        acc_sc[...] = jnp.zeros(acc_sc.shape, F32)

        def scores(j, s_ref):
            s_ref[...] = _nt(qv, k_ref[pl.ds(pl.multiple_of(j * t, t), t), :])

        def update(s_ref, j, masked):
            vv = v_ref[pl.ds(pl.multiple_of(j * t, t), t), :]
            sc = s_ref[...]
            if masked:
                row, col = _causal_mask(t)
                sc = jnp.where(col <= row, sc, -jnp.inf)
            m_prev = m_sc[...]
            m_new = jnp.maximum(m_prev, jnp.max(sc, axis=1, keepdims=True))
            alpha = jnp.exp2(m_prev - m_new)
            p = jnp.exp2(sc - jnp.tile(m_new, (1, t // LANE)))
            l_sc[...] = alpha * l_sc[...] + jnp.sum(p, axis=1, keepdims=True)
            acc_sc[...] = alpha * acc_sc[...] + _nn(p.astype(BF16), vv)
            m_sc[...] = m_new

        def pair(pp, carry):
            j = 2 * pp
            scores(j + 1, sb_sc)
            update(sa_sc, j, False)
            scores(j + 2, sa_sc)
            update(sb_sc, j + 1, False)
            return carry

        scores(0, sa_sc)
        lax.fori_loop(0, lax.div(i, 2), pair, 0)
        odd = lax.rem(i, 2)

        @pl.when(odd == 1)
        def _():
            scores(i, sb_sc)
            update(sa_sc, i - 1, False)
            update(sb_sc, i, True)

        @pl.when(odd == 0)
        def _():
            update(sa_sc, i, True)

        l = l_sc[...]
        o_ref[...] = acc_sc[...] / l
        lse_ref[...] = m_sc[...] + jnp.log2(l)

    return pl.pallas_call(
        body, name="attn_fwd", grid=(MLA_HEADS, nq),
        in_specs=[pl.BlockSpec((t, HEAD_PAD), lambda h, i: (i, h)),
                  pl.BlockSpec((s, HEAD_PAD), lambda h, i: (0, h)),
                  pl.BlockSpec((s, V_DIM), lambda h, i: (0, h))],
        out_specs=[pl.BlockSpec((t, V_DIM), lambda h, i: (i, h)), pl.BlockSpec((t, LANE), lambda h, i: (i, h))],
        out_shape=[jax.ShapeDtypeStruct((s, MLA_WIDTH), F32), jax.ShapeDtypeStruct((s, MLA_HEADS * LANE), F32)],
        scratch_shapes=[pltpu.VMEM((t, LANE), F32), pltpu.VMEM((t, LANE), F32), pltpu.VMEM((t, V_DIM), F32),
                        pltpu.VMEM((t, t), F32), pltpu.VMEM((t, t), F32)],
        compiler_params=_cparams("parallel", "arbitrary"),
    )(q, k, v)


def _attn_bwd(q, k, v, do, lse_row, delta_row):
    s = q.shape[0]
    t = min(ATTN_TILE, s)
    nq = s // t

    def body(q_ref, k_ref, v_ref, do_ref, lse_ref, dl_ref, dk_ref, dv_ref, dq_hbm,
             dq_sc, dk_sc, dv_sc, sa_sc, sb_sc, pa_sc, pb_sc, sem):
        h = pl.program_id(0)
        j = pl.program_id(1)
        kv_ = k_ref[...]
        vv = v_ref[...]

        @pl.when(j == 0)
        def _():
            dq_sc[...] = jnp.zeros(dq_sc.shape, F32)

        dk_sc[...] = jnp.zeros(dk_sc.shape, F32)
        dv_sc[...] = jnp.zeros(dv_sc.shape, F32)

        def scores(i, s_ref, p_ref):
            off = pl.multiple_of(i * t, t)
            s_ref[...] = _nt(kv_, q_ref[pl.ds(off, t), :])
            p_ref[...] = _nt(vv, do_ref[pl.ds(off, t), :])

        def update(i, s_ref, p_ref, masked):
            off = pl.multiple_of(i * t, t)
            qv = q_ref[pl.ds(off, t), :]
            dov = do_ref[pl.ds(off, t), :]
            sct = s_ref[...]
            if masked:
                row, col = _causal_mask(t)
                sct = jnp.where(row <= col, sct, -jnp.inf)
            pt = jnp.exp2(sct - lse_ref[0, :, pl.ds(off, t)])
            gt = (pt * (p_ref[...] - dl_ref[0, :, pl.ds(off, t)])).astype(BF16)
            dv_sc[...] += _nn(pt.astype(BF16), dov)
            dk_sc[...] += _nn(gt, qv)
            dq_sc[pl.ds(off, t), :] += _tn(gt, kv_)

        rest = nq - 1 - j
        scores(j, sa_sc, pa_sc)

        @pl.when(rest >= 1)
        def _():
            scores(j + 1, sb_sc, pb_sc)

        update(j, sa_sc, pa_sc, True)

        def pair(pp, carry):
            i0 = j + 1 + 2 * pp
            scores(i0 + 1, sa_sc, pa_sc)
            update(i0, sb_sc, pb_sc, False)
            scores(i0 + 2, sb_sc, pb_sc)
            update(i0 + 1, sa_sc, pa_sc, False)
            return carry

        npairs = jnp.where(rest >= 1, lax.div(rest - 1, 2), 0)
        lax.fori_loop(0, npairs, pair, 0)
        left = rest - 2 * npairs
        i1 = j + 1 + 2 * npairs

        @pl.when(left == 1)
        def _():
            update(i1, sb_sc, pb_sc, False)

        @pl.when(left == 2)
        def _():
            scores(i1 + 1, sa_sc, pa_sc)
            update(i1, sb_sc, pb_sc, False)
            update(i1 + 1, sa_sc, pa_sc, False)

        dk_ref[...] = dk_sc[...] * LN2
        dv_ref[...] = dv_sc[...]

        @pl.when(j == nq - 1)
        def _():
            cp = pltpu.make_async_copy(dq_sc, dq_hbm.at[h], sem)
            cp.start()
            cp.wait()

    return pl.pallas_call(
        body, name="attn_bwd", grid=(MLA_HEADS, nq),
        in_specs=[pl.BlockSpec((s, HEAD_PAD), lambda h, j: (0, h)),
                  pl.BlockSpec((t, HEAD_PAD), lambda h, j: (j, h)),
                  pl.BlockSpec((t, V_DIM), lambda h, j: (j, h)),
                  pl.BlockSpec((s, V_DIM), lambda h, j: (0, h)),
                  pl.BlockSpec((1, 1, s), lambda h, j: (h, 0, 0)),
                  pl.BlockSpec((1, 1, s), lambda h, j: (h, 0, 0))],
        out_specs=[pl.BlockSpec((t, HEAD_PAD), lambda h, j: (j, h)), pl.BlockSpec((t, V_DIM), lambda h, j: (j, h)),
                   pl.BlockSpec(memory_space=pl.ANY)],
        out_shape=[jax.ShapeDtypeStruct((s, MLA_HEADS * HEAD_PAD), F32), jax.ShapeDtypeStruct((s, MLA_WIDTH), F32),
                   jax.ShapeDtypeStruct((MLA_HEADS, s, HEAD_PAD), F32)],
        scratch_shapes=[pltpu.VMEM((s, HEAD_PAD), F32), pltpu.VMEM((t, HEAD_PAD), F32), pltpu.VMEM((t, V_DIM), F32),
                        pltpu.VMEM((t, t), F32), pltpu.VMEM((t, t), F32), pltpu.VMEM((t, t), F32),
                        pltpu.VMEM((t, t), F32), pltpu.SemaphoreType.DMA],
        compiler_params=_cparams("arbitrary", "arbitrary"),
    )(q, k, v, do, lse_row, delta_row)


HALO = 8


def _silu(z):
    return z * _sigmoid(z)


def _silu_grad(z):
    sg = _sigmoid(z)
    return sg * (1.0 + z * (1.0 - sg))


def _softplus(x):
    e = jnp.exp(-jnp.abs(x))
    small = e * (1.0 - e * (0.5 - e * (1.0 / 3.0)))
    return jnp.maximum(x, 0.0) + jnp.where(e < 1e-3, small, jnp.log(1.0 + e))


def _conv_taps(xe_ref, w, tm, first):
    acc = None
    for k in range(CONV_K):
        term = xe_ref[pl.ds(HALO + first - (CONV_K - 1) + k, tm), :] * w[k:k + 1, :]
        acc = term if acc is None else acc + term
    return acc


def _ssd_pre(xbc_raw, dt_raw, conv_w, conv_b, dt_bias_p):
    s = xbc_raw.shape[0]
    tm = ROW_TILE
    hb = tm // HALO

    def body(x_ref, prev_ref, dtr_ref, w_ref, b_ref, db_ref, act_ref, dt_ref, xe_sc):
        i = pl.program_id(0)
        xe_sc[pl.ds(0, HALO), :] = jnp.where(i > 0, prev_ref[...], 0.0)
        xe_sc[pl.ds(HALO, tm), :] = x_ref[...]
        pre = _conv_taps(xe_sc, w_ref[...], tm, 0) + b_ref[...]
        act_ref[...] = _silu(pre)
        dt_ref[...] = _softplus(dtr_ref[...] + db_ref[...])

    return pl.pallas_call(
        body, name="ssd_pre", grid=(s // tm,),
        in_specs=[_rows(tm, CONV_CH), pl.BlockSpec((HALO, CONV_CH), lambda i: (jnp.maximum(i * hb - 1, 0), 0)),
                  _rows(tm, LANE), _whole((CONV_K, CONV_CH)), _whole((1, CONV_CH)), _whole((1, LANE))],
        out_specs=[_rows(tm, CONV_CH), _rows(tm, LANE)],
        out_shape=[jax.ShapeDtypeStruct((s, CONV_CH), F32), jax.ShapeDtypeStruct((s, LANE), F32)],
        scratch_shapes=[pltpu.VMEM((tm + HALO, CONV_CH), F32)],
        compiler_params=_cparams("parallel"),
    )(xbc_raw, xbc_raw, dt_raw, conv_w, conv_b, dt_bias_p)


def _split3(a):
    a1 = a.astype(BF16)
    r1 = a - a1.astype(F32)
    a2 = r1.astype(BF16)
    a3 = (r1 - a2.astype(F32)).astype(BF16)
    return a1, a2, a3


def _tri_left(tri, a):
    a1, a2, a3 = _split3(a)
    return _nn(tri, a1) + _nn(tri, a2) + _nn(tri, a3)


def _tri_right(a, tri):
    a1, a2, a3 = _split3(a)
    return _nn(a1, tri) + _nn(a2, tri) + _nn(a3, tri)


def _pair_sel(lane_lo, col_a, col_b):
    return jnp.where(lane_lo, col_a, col_b)


def _chunk_common(dt, a_neg, tril, triu):
    a = dt * a_neg
    lam_c = _tri_left(tril, a)
    lam_r = _tri_right(a.T, triu)
    lam_last = lam_c[CHUNK - 1:CHUNK, :]
    return lam_c, lam_r, lam_last


def _gated_norm_fwd(y, z, g):
    hf = y * _silu(z)
    outs = []
    for grp in range(SSM_GROUPS):
        w = SSM_WIDTH // SSM_GROUPS
        n, _ = _rms(hf[:, grp * w:(grp + 1) * w])
        outs.append(n)
    return jnp.concatenate(outs, axis=1) * g


def _ssd_fwd(xbc, dt, z, a_neg, dskip_x, g_x, tril, triu):
    s = xbc.shape[0]
    tm = min(SSD_ROWS, s)
    cpb = tm // CHUNK
    nc = s // CHUNK

    def body(xbc_ref, dt_ref, z_ref, a_ref, dsk_ref, g_ref, tril_ref, triu_ref, y_ref, o_ref, hin_ref, h_sc):
        @pl.when(pl.program_id(0) == 0)
        def _():
            h_sc[...] = jnp.zeros(h_sc.shape, F32)

        tril, triu = tril_ref[...], triu_ref[...]
        ltri = tril > 0
        lane_lo = _lane_iota((CHUNK, LANE)) < SSM_P
        lane_lo_n = lane_lo

        def chunk(c, carry):
            r0 = pl.multiple_of(c * CHUNK, CHUNK)
            dtc = dt_ref[pl.ds(r0, CHUNK), :]
            lam_c, lam_r, lam_last = _chunk_common(dtc, a_ref[...], tril, triu)
            e_c = jnp.exp(lam_c)
            f_r = jnp.exp(lam_r[:, CHUNK - 1:CHUNK] - lam_r)
            cd = jnp.exp(lam_last)
            for grp in range(SSM_GROUPS):
                bo = SSM_WIDTH + grp * SSM_N
                co = SSM_WIDTH + SSM_GROUPS * SSM_N + grp * SSM_N
                bm = xbc_ref[pl.ds(r0, CHUNK), bo:bo + SSM_N]
                cm = xbc_ref[pl.ds(r0, CHUNK), co:co + SSM_N]
                cm_b = cm.astype(BF16)
                gmat = _nt(cm_b, bm.astype(BF16))
                bt = bm.T
                for pj in range(SSM_HEADS // SSM_GROUPS // 2):
                    ha = grp * (SSM_HEADS // SSM_GROUPS) + 2 * pj
                    hb_ = ha + 1
                    lo = ha * SSM_P
                    xs = xbc_ref[pl.ds(r0, CHUNK), lo:lo + LANE]
                    x2 = xs * _pair_sel(lane_lo, dtc[:, ha:ha + 1], dtc[:, hb_:hb_ + 1])
                    x2b = x2.astype(BF16)
                    ys, sts = [], []
                    for hh in (ha, hb_):
                        seg = lam_c[:, hh:hh + 1] - lam_r[hh:hh + 1, :]
                        dec = jnp.exp(jnp.where(ltri, seg, -jnp.inf))
                        ys.append(_nn((gmat * dec).astype(BF16), x2b))
                        sts.append(_nn((bt * f_r[hh:hh + 1, :]).astype(BF16), x2b))
                    hp = h_sc[:, lo:lo + LANE]
                    hin_ref[c, :, lo:lo + LANE] = hp
                    zz = _nn(cm_b, hp.astype(BF16))
                    e2 = _pair_sel(lane_lo, e_c[:, ha:ha + 1], e_c[:, hb_:hb_ + 1])
                    yv = jnp.where(lane_lo, ys[0], ys[1]) + e2 * zz
                    y_ref[pl.ds(r0, CHUNK), lo:lo + LANE] = yv + xs * dsk_ref[:, lo:lo + LANE]
                    cd2 = _pair_sel(lane_lo_n, cd[:, ha:ha + 1], cd[:, hb_:hb_ + 1])
                    h_sc[:, lo:lo + LANE] = hp * cd2 + jnp.where(lane_lo_n, sts[0], sts[1])
            return carry

        lax.fori_loop(0, cpb, chunk, 0)
        o_ref[...] = _gated_norm_fwd(y_ref[...], z_ref[...], g_ref[...])

    return pl.pallas_call(
        body, name="ssd_fwd", grid=(s // tm,),
        in_specs=[_rows(tm, CONV_CH), _rows(tm, LANE), _rows(tm, SSM_WIDTH), _whole((1, LANE)),
                  _whole((1, SSM_WIDTH)), _whole((1, SSM_WIDTH)), _whole((CHUNK, CHUNK)), _whole((CHUNK, CHUNK))],
        out_specs=[_rows(tm, SSM_WIDTH), _rows(tm, SSM_WIDTH),
                   pl.BlockSpec((cpb, SSM_N, SSM_WIDTH), lambda i: (i, 0, 0))],
        out_shape=[jax.ShapeDtypeStruct((s, SSM_WIDTH), F32), jax.ShapeDtypeStruct((s, SSM_WIDTH), F32),
                   jax.ShapeDtypeStruct((nc, SSM_N, SSM_WIDTH), F32)],
        scratch_shapes=[pltpu.VMEM((SSM_N, SSM_WIDTH), F32)],
        compiler_params=_cparams("arbitrary"),
    )(xbc, dt, z, a_neg, dskip_x, g_x, tril, triu)


def _outln(o, z_attn, o_ssm, w_out, x, gate, ln_g, ln_b, tgt):
    s = x.shape[0]
    tm = ROW_TILE

    def body(o_ref, z_ref, os_ref, w_ref, x_ref, gate_ref, g_ref, b_ref, t_ref,
             cat_ref, dmix_ref, gx_ref, do_ref, dz_ref, dl_ref, dos_ref, loss_ref, dg_ref, db_ref, dgate_ref):
        ov, zv = o_ref[...], z_ref[...]
        sz = _silu(zv)
        cat_ref[:, :MLA_WIDTH] = (ov * sz).astype(BF16)
        cat_ref[:, MLA_WIDTH:] = os_ref[...].astype(BF16)
        w = w_ref[...]
        mixed = _nn(cat_ref[...], w)
        gate_v = gate_ref[...]
        hv = DEEPNORM_ALPHA * x_ref[...] + gate_v * mixed
        mu = jnp.mean(hv, axis=-1, keepdims=True)
        hc = hv - mu
        rstd = lax.rsqrt(jnp.mean(hc * hc, axis=-1, keepdims=True) + LN_EPS)
        xhat = hc * rstd
        g = g_ref[...]
        err = xhat * g + b_ref[...] - t_ref[...]
        _acc_rows(loss_ref, jnp.full((1, LANE), (0.5 / D_MODEL) * jnp.sum(err * err), F32))
        dy = err * (1.0 / D_MODEL)
        _acc_rows(dg_ref, _colsum(dy * xhat))
        _acc_rows(db_ref, _colsum(dy))
        dxhat = dy * g
        dh = rstd * (dxhat - jnp.mean(dxhat, axis=-1, keepdims=True)
                     - xhat * jnp.mean(dxhat * xhat, axis=-1, keepdims=True))
        gx_ref[...] = DEEPNORM_ALPHA * dh
        _acc_rows(dgate_ref, _colsum(dh * mixed))
        dmix = (gate_v * dh).astype(BF16)
        dmix_ref[...] = dmix
        dcat = _nt(dmix, w)
        da = dcat[:, :MLA_WIDTH]
        dos_ref[...] = dcat[:, MLA_WIDTH:]
        dov = da * sz
        do_ref[...] = dov.astype(BF16)
        dz_ref[...] = da * ov * _silu_grad(zv)
        prod = dov * ov
        for h in range(MLA_HEADS):
            dsum = jnp.sum(prod[:, h * V_DIM:(h + 1) * V_DIM], axis=1, keepdims=True)
            dl_ref[:, h * LANE:(h + 1) * LANE] = jnp.broadcast_to(dsum, (tm, LANE))

    vec = _whole((1, D_MODEL))
    return pl.pallas_call(
        body, name="outln", grid=(s // tm,),
        in_specs=[_rows(tm, MLA_WIDTH), _rows(tm, MLA_WIDTH), _rows(tm, SSM_WIDTH), _whole((MIX_WIDTH, D_MODEL)),
                  _rows(tm, D_MODEL), vec, vec, vec, _rows(tm, D_MODEL)],
        out_specs=[_rows(tm, MIX_WIDTH), _rows(tm, D_MODEL), _rows(tm, D_MODEL), _rows(tm, MLA_WIDTH),
                   _rows(tm, MLA_WIDTH), _rows(tm, MLA_HEADS * LANE), _rows(tm, SSM_WIDTH),
                   _whole((1, LANE)), vec, vec, vec],
        out_shape=[jax.ShapeDtypeStruct((s, MIX_WIDTH), BF16), jax.ShapeDtypeStruct((s, D_MODEL), BF16),
                   jax.ShapeDtypeStruct((s, D_MODEL), F32), jax.ShapeDtypeStruct((s, MLA_WIDTH), BF16),
                   jax.ShapeDtypeStruct((s, MLA_WIDTH), F32), jax.ShapeDtypeStruct((s, MLA_HEADS * LANE), F32),
                   jax.ShapeDtypeStruct((s, SSM_WIDTH), F32), jax.ShapeDtypeStruct((1, LANE), F32),
                   jax.ShapeDtypeStruct((1, D_MODEL), F32), jax.ShapeDtypeStruct((1, D_MODEL), F32),
                   jax.ShapeDtypeStruct((1, D_MODEL), F32)],
        compiler_params=_cparams("arbitrary"),
    )(o, z_attn, o_ssm, w_out, x, gate, ln_g, ln_b, tgt)


def _ssd_bwd(dos, y, z, xbc, dt, hin, a_neg, dskip_x, g_x, tril, triu):
    s = xbc.shape[0]
    tm = min(SSD_ROWS, s)
    cpb = tm // CHUNK
    nb = s // tm
    gw = SSM_WIDTH // SSM_GROUPS
    hpg = SSM_HEADS // SSM_GROUPS

    def body(dos_ref, y_ref, z_ref, xbc_ref, dt_ref, hin_ref, a_ref, dsk_ref, g_ref, tril_ref, triu_ref,
             dxbc_ref, ddt_ref, dz_ref, dg_ref, ddsk_ref, da_ref, dh_sc, dy_sc):
        @pl.when(pl.program_id(0) == 0)
        def _():
            dh_sc[...] = jnp.zeros(dh_sc.shape, F32)

        yv, zv, dov = y_ref[...], z_ref[...], dos_ref[...]
        sz = _silu(zv)
        hf = yv * sz
        gv = g_ref[...]
        dgs, dhfs = [], []
        for grp in range(SSM_GROUPS):
            sl = slice(grp * gw, (grp + 1) * gw)
            n, rstd = _rms(hf[:, sl])
            dgs.append(_colsum(dov[:, sl] * n))
            dhfs.append(_rms_bwd(dov[:, sl] * gv[:, sl], n, rstd))
        dhf = jnp.concatenate(dhfs, axis=1)
        _acc_rows(dg_ref, jnp.concatenate(dgs, axis=1))
        dy_sc[...] = dhf * sz
        dz_ref[...] = dhf * yv * _silu_grad(zv)

        tril, triu = tril_ref[...], triu_ref[...]
        ltri = tril > 0
        utri = triu > 0
        lane = _lane_iota((CHUNK, LANE))
        lane1 = _lane_iota((1, LANE))
        lane_lo = lane < SSM_P
        row_last = lax.broadcasted_iota(jnp.int32, (CHUNK, LANE), 0) == CHUNK - 1
        a_neg_v = a_ref[...]

        def chunk(ci, carry):
            dsk_acc, da_acc = carry
            cl = cpb - 1 - ci
            r0 = pl.multiple_of(cl * CHUNK, CHUNK)
            rows = pl.ds(r0, CHUNK)
            dtc = dt_ref[rows, :]
            lam_c, lam_r, lam_last = _chunk_common(dtc, a_neg_v, tril, triu)
            e_c = jnp.exp(lam_c)
            f_c = jnp.exp(lam_last - lam_c)
            cd = jnp.exp(lam_last)
            dlam = jnp.zeros((CHUNK, LANE), F32)
            dlast = jnp.zeros((1, LANE), F32)
            ddt_x = jnp.zeros((CHUNK, LANE), F32)
            dsk_parts = []
            for grp in range(SSM_GROUPS):
                bo = SSM_WIDTH + grp * SSM_N
                co = SSM_WIDTH + SSM_GROUPS * SSM_N + grp * SSM_N
                bm = xbc_ref[rows, bo:bo + SSM_N]
                cm = xbc_ref[rows, co:co + SSM_N]
                bm_b, cm_b = bm.astype(BF16), cm.astype(BF16)
                gmat = _nt(cm_b, bm_b)
                gmat_t = _nt(bm_b, cm_b)
                ct_b = cm.T.astype(BF16)
                acc_dg = jnp.zeros((CHUNK, CHUNK), F32)
                acc_dgt = jnp.zeros((CHUNK, CHUNK), F32)
                d_b = jnp.zeros((CHUNK, SSM_N), F32)
                d_c = jnp.zeros((CHUNK, SSM_N), F32)
                for pj in range(hpg // 2):
                    ha = grp * hpg + 2 * pj
                    hb_ = ha + 1
                    lo = ha * SSM_P
                    xs = xbc_ref[rows, lo:lo + LANE]
                    dt2 = _pair_sel(lane_lo, dtc[:, ha:ha + 1], dtc[:, hb_:hb_ + 1])
                    x2 = xs * dt2
                    x2b = x2.astype(BF16)
                    dy2 = dy_sc[rows, lo:lo + LANE]
                    dy2b = dy2.astype(BF16)
                    hp = hin_ref[cl, :, lo:lo + LANE]
                    hp_b = hp.astype(BF16)
                    dhn = dh_sc[:, lo:lo + LANE]
                    dhn_b = dhn.astype(BF16)
                    e2 = _pair_sel(lane_lo, e_c[:, ha:ha + 1], e_c[:, hb_:hb_ + 1])
                    yo = e2 * _nn(cm_b, hp_b)
                    dzz_b = (e2 * dy2).astype(BF16)
                    d_c = d_c + _nt(dzz_b, hp_b)
                    cd2 = _pair_sel(lane_lo, cd[:, ha:ha + 1], cd[:, hb_:hb_ + 1])
                    dh_sc[:, lo:lo + LANE] = _nn(ct_b, dzz_b) + cd2 * dhn
                    t_yo = dy2 * yo
                    t_hh = dhn * hp
                    dx2 = jnp.zeros((CHUNK, LANE), F32)
                    for hh, msk in ((ha, lane_lo), (hb_, jnp.logical_not(lane_lo))):
                        x2h_b = jnp.where(msk, x2, 0.0).astype(BF16)
                        dy2h_b = jnp.where(msk, dy2, 0.0).astype(BF16)
                        lc = lam_c[:, hh:hh + 1]
                        lr = lam_r[hh:hh + 1, :]
                        dec = jnp.exp(jnp.where(ltri, lc - lr, -jnp.inf))
                        dect = jnp.exp(jnp.where(utri, lr - lc, -jnp.inf))
                        dmd = _nt(dy2h_b, x2b) * dec
                        dmtd = _nt(x2h_b, dy2b) * dect
                        acc_dg = acc_dg + dmd
                        acc_dgt = acc_dgt + dmtd
                        w_row = jnp.sum(dmd * gmat, axis=1, keepdims=True)
                        wt_row = jnp.sum(dmtd * gmat_t, axis=1, keepdims=True)
                        fcol = f_c[:, hh:hh + 1]
                        dx_h = _nn((gmat_t * dect).astype(BF16), dy2b) + _nn((bm * fcol).astype(BF16), dhn_b)
                        qh = _nt(x2h_b, dhn_b)
                        d_b = d_b + fcol * qh
                        dff = jnp.sum(bm * qh, axis=1, keepdims=True) * fcol
                        yo_row = jnp.sum(jnp.where(msk, t_yo, 0.0), axis=1, keepdims=True)
                        dlam_h = w_row - wt_row + yo_row - dff
                        hh_sum = jnp.sum(jnp.sum(jnp.where(msk, t_hh, 0.0), axis=1, keepdims=True), axis=0, keepdims=True)
                        last_h = cd[:, hh:hh + 1] * hh_sum + jnp.sum(dff, axis=0, keepdims=True)
                        dlam = jnp.where(lane == hh, dlam_h, dlam)
                        dlast = jnp.where(lane1 == hh, last_h, dlast)
                        dx2 = jnp.where(msk, dx_h, dx2)
                    dxbc_ref[rows, lo:lo + LANE] = dx2 * dt2 + dy2 * dsk_ref[:, lo:lo + LANE]
                    prod = dx2 * xs
                    for hh, msk in ((ha, lane_lo), (hb_, jnp.logical_not(lane_lo))):
                        col = jnp.sum(jnp.where(msk, prod, 0.0), axis=1, keepdims=True)
                        ddt_x = jnp.where(lane == hh, col, ddt_x)
                    dsk_parts.append(_colsum(dy2 * xs))
                d_c = d_c + _nn(acc_dg.astype(BF16), bm_b)
                d_b = d_b + _nn(acc_dgt.astype(BF16), cm_b)
                dxbc_ref[rows, bo:bo + SSM_N] = d_b
                dxbc_ref[rows, co:co + SSM_N] = d_c
            dlam = dlam + jnp.where(row_last, dlast, 0.0)
            da = _tri_left(triu, dlam)
            ddt_ref[rows, :] = da * a_neg_v + ddt_x
            return dsk_acc + jnp.concatenate(dsk_parts, axis=1), da_acc + _colsum(da * dtc)

        dsk_tot, da_tot = lax.fori_loop(
            0, cpb, chunk, (jnp.zeros((1, SSM_WIDTH), F32), jnp.zeros((1, LANE), F32)))
        _acc_rows(ddsk_ref, dsk_tot)
        _acc_rows(da_ref, da_tot)

    rev = lambda i: (nb - 1 - i, 0)
    rrows = lambda w: pl.BlockSpec((tm, w), rev)
    return pl.pallas_call(
        body, name="ssd_bwd", grid=(nb,),
        in_specs=[rrows(SSM_WIDTH), rrows(SSM_WIDTH), rrows(SSM_WIDTH), rrows(CONV_CH), rrows(LANE),
                  pl.BlockSpec((cpb, SSM_N, SSM_WIDTH), lambda i: (nb - 1 - i, 0, 0)),
                  _whole((1, LANE)), _whole((1, SSM_WIDTH)), _whole((1, SSM_WIDTH)),
                  _whole((CHUNK, CHUNK)), _whole((CHUNK, CHUNK))],
        out_specs=[rrows(CONV_CH), rrows(LANE), rrows(SSM_WIDTH),
                   _whole((1, SSM_WIDTH)), _whole((1, SSM_WIDTH)), _whole((1, LANE))],
        out_shape=[jax.ShapeDtypeStruct((s, CONV_CH), F32), jax.ShapeDtypeStruct((s, LANE), F32),
                   jax.ShapeDtypeStruct((s, SSM_WIDTH), F32), jax.ShapeDtypeStruct((1, SSM_WIDTH), F32),
                   jax.ShapeDtypeStruct((1, SSM_WIDTH), F32), jax.ShapeDtypeStruct((1, LANE), F32)],
        scratch_shapes=[pltpu.VMEM((SSM_N, SSM_WIDTH), F32), pltpu.VMEM((tm, SSM_WIDTH), F32)],
        compiler_params=_cparams("arbitrary"),
    )(dos, y, z, xbc, dt, hin, a_neg, dskip_x, g_x, tril, triu)


def _ssd_post_bwd(xbc_raw, dxa, ddt, dt_raw, conv_w, conv_b, dt_bias_p):
    s = xbc_raw.shape[0]
    tm = ROW_TILE
    hb = tm // HALO
    nt = s // tm
    ext = tm + HALO

    def body(x_ref, prev_ref, next_ref, d_ref, dnext_ref, ddt_ref, dtr_ref, w_ref, b_ref, db_ref,
             dx_ref, ddtr_ref, dw_ref, dcb_ref, ddb_ref, xe_sc, de_sc):
        i = pl.program_id(0)
        w = w_ref[...]
        xe_sc[pl.ds(0, HALO), :] = jnp.where(i > 0, prev_ref[...], 0.0)
        xe_sc[pl.ds(HALO, tm), :] = x_ref[...]
        xe_sc[pl.ds(HALO + tm, HALO), :] = next_ref[...]
        pre = _conv_taps(xe_sc, w, ext, 0) + b_ref[...]
        sg = _silu_grad(pre)
        de_sc[pl.ds(0, tm), :] = d_ref[...] * sg[:tm]
        de_sc[pl.ds(tm, HALO), :] = jnp.where(i < nt - 1, dnext_ref[...] * sg[tm:], 0.0)
        dconv = de_sc[pl.ds(0, tm), :]
        acc = None
        dws = []
        for k in range(CONV_K):
            term = de_sc[pl.ds(CONV_K - 1 - k, tm), :] * w[k:k + 1, :]
            acc = term if acc is None else acc + term
            dws.append(_colsum(dconv * xe_sc[pl.ds(HALO - (CONV_K - 1) + k, tm), :]))
        dx_ref[...] = acc
        _acc_rows(dw_ref, jnp.concatenate(dws, axis=0))
        _acc_rows(dcb_ref, _colsum(dconv))
        ddtr = ddt_ref[...] * _sigmoid(dtr_ref[...] + db_ref[...])
        ddtr_ref[...] = ddtr
        _acc_rows(ddb_ref, _colsum(ddtr))

    halo_prev = pl.BlockSpec((HALO, CONV_CH), lambda i: (jnp.maximum(i * hb - 1, 0), 0))
    halo_next = pl.BlockSpec((HALO, CONV_CH), lambda i: (jnp.minimum((i + 1) * hb, s // HALO - 1), 0))
    return pl.pallas_call(
        body, name="ssd_post_bwd", grid=(nt,),
        in_specs=[_rows(tm, CONV_CH), halo_prev, halo_next, _rows(tm, CONV_CH), halo_next, _rows(tm, LANE),
                  _rows(tm, LANE), _whole((CONV_K, CONV_CH)), _whole((1, CONV_CH)), _whole((1, LANE))],
        out_specs=[_rows(tm, CONV_CH), _rows(tm, LANE), _whole((CONV_K, CONV_CH)), _whole((1, CONV_CH)),
                   _whole((1, LANE))],
        out_shape=[jax.ShapeDtypeStruct((s, CONV_CH), F32), jax.ShapeDtypeStruct((s, LANE), F32),
                   jax.ShapeDtypeStruct((CONV_K, CONV_CH), F32), jax.ShapeDtypeStruct((1, CONV_CH), F32),
                   jax.ShapeDtypeStruct((1, LANE), F32)],
        scratch_shapes=[pltpu.VMEM((tm + 2 * HALO, CONV_CH), F32), pltpu.VMEM((ext, CONV_CH), F32)],
        compiler_params=_cparams("arbitrary"),
    )(xbc_raw, xbc_raw, xbc_raw, dxa, dxa, ddt, dt_raw, conv_w, conv_b, dt_bias_p)


def _qbwd(dq_att, q_lat, g_q, w_qb_p, cos, sin):
    s = q_lat.shape[0]
    tm = ROW_TILE
    wq = MLA_HEADS * HEAD_PAD

    def body(dq_ref, ql_ref, g_ref, w_ref, cos_ref, sin_ref, dql_ref, draw_ref, dg_ref):
        c, sn = cos_ref[...], sin_ref[...]
        for h in range(MLA_HEADS):
            o = h * HEAD_PAD
            dqh = dq_ref[h] * ATTN_SCALE
            draw_ref[:, o:o + QK_NOPE] = dqh[:, :QK_NOPE].astype(BF16)
            draw_ref[:, o + QK_NOPE:o + HEAD_PAD] = _rope_transposed(dqh[:, QK_NOPE:], c, sn).astype(BF16)
        dn = _nt(draw_ref[...], w_ref[...])
        xhat, rstd = _rms(ql_ref[...])
        _acc_rows(dg_ref, _colsum(dn * xhat))
        dql_ref[...] = _rms_bwd(dn * g_ref[...], xhat, rstd)

    return pl.pallas_call(
        body, name="qbwd", grid=(s // tm,),
        in_specs=[pl.BlockSpec((MLA_HEADS, tm, HEAD_PAD), lambda i: (0, i, 0)), _rows(tm, Q_RANK), _whole((1, Q_RANK)),
                  _whole((Q_RANK, wq)), _rows(tm, LANE), _rows(tm, LANE)],
        out_specs=[_rows(tm, Q_RANK), _rows(tm, wq), _whole((1, Q_RANK))],
        out_shape=[jax.ShapeDtypeStruct((s, Q_RANK), F32), jax.ShapeDtypeStruct((s, wq), BF16),
                   jax.ShapeDtypeStruct((1, Q_RANK), F32)],
        compiler_params=_cparams("arbitrary"),
    )(dq_att, q_lat, g_q, w_qb_p, cos, sin)


def _kvbwd(dk_att, dv, kv_lat, g_kv, w_kvb_p, cos, sin):
    s = kv_lat.shape[0]
    tm = ROW_TILE
    wk = MLA_HEADS * HEAD_PAD
    wr = MLA_HEADS * (QK_NOPE + V_DIM)

    def body(dk_ref, dv_ref, kl_ref, g_ref, w_ref, cos_ref, sin_ref, dkl_ref, draw_ref, dg_ref):
        dkr = None
        for h in range(MLA_HEADS):
            o = h * HEAD_PAD
            draw_ref[:, h * QK_NOPE:(h + 1) * QK_NOPE] = dk_ref[:, o:o + QK_NOPE].astype(BF16)
            part = dk_ref[:, o + QK_NOPE:o + HEAD_PAD]
            dkr = part if dkr is None else dkr + part
        draw_ref[:, MLA_HEADS * QK_NOPE:] = dv_ref[...].astype(BF16)
        dn = _nt(draw_ref[...], w_ref[...])
        xhat, rstd = _rms(kl_ref[:, :KV_RANK])
        _acc_rows(dg_ref, _colsum(dn * xhat))
        dkl_ref[:, :KV_RANK] = _rms_bwd(dn * g_ref[...], xhat, rstd)
        dkl_ref[:, KV_RANK:] = _rope_transposed(dkr, cos_ref[...], sin_ref[...])

    return pl.pallas_call(
        body, name="kvbwd", grid=(s // tm,),
        in_specs=[_rows(tm, wk), _rows(tm, MLA_WIDTH), _rows(tm, KV_LAT_PAD), _whole((1, KV_RANK)),
                  _whole((KV_RANK, wr)), _rows(tm, LANE), _rows(tm, LANE)],
        out_specs=[_rows(tm, KV_LAT_PAD), _rows(tm, wr), _whole((1, KV_RANK))],
        out_shape=[jax.ShapeDtypeStruct((s, KV_LAT_PAD), F32), jax.ShapeDtypeStruct((s, wr), BF16),
                   jax.ShapeDtypeStruct((1, KV_RANK), F32)],
        compiler_params=_cparams("arbitrary"),
    )(dk_att, dv, kv_lat, g_kv, w_kvb_p, cos, sin)


def _inproj_bwd(pieces, w_in_p, x, scale1p, gx1):
    s = x.shape[0]
    tm = ROW_TILE

    def body(*refs):
        p_refs = refs[:len(IN_PAD)]
        w_ref, x_ref, sc_ref, gx1_ref, gx_ref, dp_ref, dsc_ref, dsh_ref = refs[len(IN_PAD):]
        off = 0
        for ref, w in zip(p_refs, IN_PAD):
            dp_ref[:, off:off + w] = ref[...].astype(BF16)
            off += w
        du = _nt(dp_ref[...], w_ref[...])
        gx_ref[...] = gx1_ref[...] + du * sc_ref[...]
        _acc_rows(dsc_ref, _colsum(du * x_ref[...]))
        _acc_rows(dsh_ref, _colsum(du))

    vec = _whole((1, D_MODEL))
    return pl.pallas_call(
        body, name="inproj_bwd", grid=(s // tm,),
        in_specs=[_rows(tm, w) for w in IN_PAD] + [_whole((D_MODEL, IN_PAD_WIDTH)), _rows(tm, D_MODEL), vec,
                                                    _rows(tm, D_MODEL)],
        out_specs=[_rows(tm, D_MODEL), _rows(tm, IN_PAD_WIDTH), vec, vec],
        out_shape=[jax.ShapeDtypeStruct((s, D_MODEL), F32), jax.ShapeDtypeStruct((s, IN_PAD_WIDTH), BF16),
                   jax.ShapeDtypeStruct((1, D_MODEL), F32), jax.ShapeDtypeStruct((1, D_MODEL), F32)],
        compiler_params=_cparams("arbitrary"),
    )(*pieces, w_in_p, x, scale1p, gx1)


def _matmul_tn(name, a, b, tn):
    s, k = a.shape
    n = b.shape[1]
    tm = min(ATTN_TILE, s)

    def body(a_ref, b_ref, o_ref):
        @pl.when(pl.program_id(1) == 0)
        def _():
            o_ref[...] = jnp.zeros_like(o_ref)
        o_ref[...] += _tn(a_ref[...], b_ref[...])

    return pl.pallas_call(
        body, name=name, grid=(n // tn, s // tm),
        in_specs=[pl.BlockSpec((tm, k), lambda j, i: (i, 0)), pl.BlockSpec((tm, tn), lambda j, i: (i, j))],
        out_specs=pl.BlockSpec((k, tn), lambda j, i: (0, j)),
        out_shape=jax.ShapeDtypeStruct((k, n), F32),
        compiler_params=_cparams("parallel", "arbitrary"),
    )(a, b)


def _pad_cols(a, width):
    return jnp.pad(a, ((0, 0), (0, width - a.shape[1])))


def _pack_w_in(w_in):
    parts, off = [], 0
    for w, wp in zip(IN_SPLITS, IN_PAD):
        parts.append(_pad_cols(w_in[:, off:off + w], wp))
        off += w
    return jnp.concatenate(parts, axis=1)


def _unpack_w_in(g):
    parts, off = [], 0
    for w, wp in zip(IN_SPLITS, IN_PAD):
        parts.append(g[:, off:off + w])
        off += wp
    return jnp.concatenate(parts, axis=1)


def _pack_w_qb(w_qb):
    w = w_qb.reshape(Q_RANK, MLA_HEADS, QK_HEAD)
    return jnp.pad(w, ((0, 0), (0, 0), (0, HEAD_PAD - QK_HEAD))).reshape(Q_RANK, MLA_HEADS * HEAD_PAD)


def _unpack_w_qb(g):
    return g.reshape(Q_RANK, MLA_HEADS, HEAD_PAD)[:, :, :QK_HEAD].reshape(Q_RANK, MLA_HEADS * QK_HEAD)


def _pack_w_kvb(w_kvb):
    w = w_kvb.reshape(KV_RANK, MLA_HEADS, QK_NOPE + V_DIM)
    return jnp.concatenate([w[:, :, :QK_NOPE].reshape(KV_RANK, -1), w[:, :, QK_NOPE:].reshape(KV_RANK, -1)], axis=1)


def _unpack_w_kvb(g):
    gk = g[:, :MLA_HEADS * QK_NOPE].reshape(KV_RANK, MLA_HEADS, QK_NOPE)
    gv = g[:, MLA_HEADS * QK_NOPE:].reshape(KV_RANK, MLA_HEADS, V_DIM)
    return jnp.concatenate([gk, gv], axis=2).reshape(KV_RANK, -1)


def _rope_tables(positions):
    inv_freq = 1.0 / (ROPE_THETA ** (jnp.arange(ROPE_HALF, dtype=F32) / ROPE_HALF))
    ang = positions.astype(F32)[:, None] * inv_freq
    cos, sin = jnp.cos(ang), jnp.sin(ang)
    zeros = jnp.zeros((positions.shape[0], LANE - QK_ROPE), F32)
    return jnp.concatenate([cos, cos, zeros], axis=1), jnp.concatenate([-sin, sin, zeros], axis=1)


def _per_query_rows(a):
    s = a.shape[0]
    return a.reshape(s, MLA_HEADS, LANE)[:, :, 0].T.reshape(MLA_HEADS, 1, s)


def _local_step(x, tgt, positions, mod, w_in, q_norm_g, w_qb, kv_norm_g, w_kvb, conv_w, conv_b, dt_bias,
                a_log, d_skip, ssm_norm_g, w_out, ln_g, ln_b):
    row = lambda v: v.reshape(1, -1)
    shift, scale, gate = mod[:D_MODEL], mod[D_MODEL:2 * D_MODEL], mod[2 * D_MODEL:]
    scale1p = row(1.0 + scale)
    w_in_p = _pack_w_in(w_in).astype(BF16)
    w_qb_p = _pack_w_qb(w_qb).astype(BF16)
    w_kvb_p = _pack_w_kvb(w_kvb).astype(BF16)
    w_out_b = w_out.astype(BF16)
    cos, sin = _rope_tables(positions)
    a_neg = row(jnp.pad(-jnp.exp(a_log), (0, LANE - SSM_HEADS)))
    dskip_x = row(jnp.repeat(d_skip, SSM_P))
    dt_bias_p = row(jnp.pad(dt_bias, (0, LANE - SSM_HEADS)))
    tri = jnp.tril(jnp.ones((CHUNK, CHUNK), F32))
    tril, triu = tri.astype(BF16), tri.T.astype(BF16)

    u_bf, q_lat, kv_lat, z_attn, xbc_raw, dt_raw, z_ssm = _inproj(x, scale1p, row(shift), w_in_p)
    nq_bf, q_att = _qpath(q_lat, row(q_norm_g), w_qb_p, cos, sin)
    nkv_bf, k_att, v_att = _kvpath(kv_lat, row(kv_norm_g), w_kvb_p, cos, sin)
    o, lse = _attn_fwd(q_att, k_att, v_att)
    xbc, dt = _ssd_pre(xbc_raw, dt_raw, conv_w, row(conv_b), dt_bias_p)
    y, o_ssm, hin = _ssd_fwd(xbc, dt, z_ssm, a_neg, dskip_x, row(ssm_norm_g), tril, triu)
    (cat_bf, dmix_bf, gx1, do_bf, dz_attn, delta, dos, loss, d_ln_g, d_ln_b, d_gate) = _outln(
        o, z_attn, o_ssm, w_out_b, x, row(gate), row(ln_g), row(ln_b), tgt)

    g_w_out = _matmul_tn("gw_out", cat_bf, dmix_bf, 512)
    dk_att, dv, dq_att = _attn_bwd(q_att, k_att, v_att, do_bf, _per_query_rows(lse), _per_query_rows(delta))
    dq_lat, dqraw_bf, d_q_norm_g = _qbwd(dq_att, q_lat, row(q_norm_g), w_qb_p, cos, sin)
    dkv_lat, dkvraw_bf, d_kv_norm_g = _kvbwd(dk_att, dv, kv_lat, row(kv_norm_g), w_kvb_p, cos, sin)
    g_w_qb = _unpack_w_qb(_matmul_tn("gw_qb", nq_bf, dqraw_bf, MLA_HEADS * HEAD_PAD))
    g_w_kvb = _unpack_w_kvb(_matmul_tn("gw_kvb", nkv_bf, dkvraw_bf, MLA_HEADS * (QK_NOPE + V_DIM)))
    dxa, ddt, dz_ssm, d_ssm_g, ddsk_x, d_a = _ssd_bwd(dos, y, z_ssm, xbc, dt, hin, a_neg, dskip_x, row(ssm_norm_g),
                                                       tril, triu)
    dxbc_raw, ddt_raw, d_conv_w, d_conv_b, d_dt_bias = _ssd_post_bwd(xbc_raw, dxa, ddt, dt_raw, conv_w, row(conv_b),
                                                                     dt_bias_p)
    grad_x, dproj_bf, d_scale, d_shift = _inproj_bwd((dq_lat, dkv_lat, dz_attn, dxbc_raw, ddt_raw, dz_ssm),
                                                     w_in_p, x, scale1p, gx1)
    g_w_in = _unpack_w_in(_matmul_tn("gw_in", u_bf, dproj_bf, 640))
    return dict(
        loss=loss[0, 0], grad_x=grad_x,
        dmod=jnp.concatenate([d_shift[0], d_scale[0], d_gate[0]]),
        w_in=g_w_in, q_norm_g=d_q_norm_g[0], w_qb=g_w_qb, kv_norm_g=d_kv_norm_g[0], w_kvb=g_w_kvb,
        conv_w=d_conv_w, conv_b=d_conv_b[0], dt_bias=d_dt_bias[0, :SSM_HEADS],
        a_log=d_a[0, :SSM_HEADS] * a_neg[0, :SSM_HEADS],
        d_skip=ddsk_x.reshape(SSM_HEADS, SSM_P).sum(axis=1), ssm_norm_g=d_ssm_g[0], w_out=g_w_out,
        ln_g=d_ln_g[0], ln_b=d_ln_b[0])


ADAM_ROWS = 512


def _my_index():
    return 4 * lax.axis_index("x") + 2 * lax.axis_index("y") + lax.axis_index("c")


def _exchange(name, send, gather):
    r = send.shape[-2]

    def body(send_ref, recv_ref, send_sems, recv_sems, local_sem):
        x, y, c = lax.axis_index("x"), lax.axis_index("y"), lax.axis_index("c")
        me = 4 * x + 2 * y + c

        def src(idx):
            return send_ref if gather else send_ref.at[idx]

        own = pltpu.make_async_copy(src(me), recv_ref.at[me], local_sem)
        own.start()
        copies = []
        for k in range(1, N_DEV):
            px, py, pc = x ^ ((k >> 2) & 1), y ^ ((k >> 1) & 1), c ^ (k & 1)
            peer = 4 * px + 2 * py + pc
            copies.append(pltpu.make_async_remote_copy(
                src_ref=src(peer), dst_ref=recv_ref.at[me],
                send_sem=send_sems.at[k - 1], recv_sem=recv_sems.at[k - 1],
                device_id=(px, py, pc), device_id_type=pl.DeviceIdType.MESH))
        for cp in copies:
            cp.start()
        for cp in copies:
            cp.wait()
        own.wait()

    return pl.pallas_call(
        body, name=name,
        in_specs=[pl.BlockSpec(memory_space=pl.ANY)], out_specs=pl.BlockSpec(memory_space=pl.ANY),
        out_shape=jax.ShapeDtypeStruct((N_DEV, r, LANE), send.dtype),
        scratch_shapes=[pltpu.SemaphoreType.DMA((N_DEV - 1,)), pltpu.SemaphoreType.DMA((N_DEV - 1,)),
                        pltpu.SemaphoreType.DMA],
    )(send)


def _flat_rows(parts, row_multiple):
    flat = jnp.concatenate([p.reshape(-1) for p in parts])
    chunk = row_multiple * LANE
    total = -(-flat.shape[0] // chunk) * chunk
    return jnp.pad(flat, (0, total - flat.shape[0])).reshape(-1, LANE)


def _unflat(flat, shapes):
    flat = flat.reshape(-1)
    out, off = [], 0
    for shp in shapes:
        n = math.prod(shp)
        out.append(flat[off:off + n].reshape(shp))
        off += n
    return out


def _adam_update(g, w, m, v):
    m2 = ADAM_B1 * m + (1.0 - ADAM_B1) * g
    v2 = ADAM_B2 * v + (1.0 - ADAM_B2) * (g * g)
    m_hat = m2 / (1.0 - ADAM_B1 ** ADAM_STEP)
    v_hat = v2 / (1.0 - ADAM_B2 ** ADAM_STEP)
    delta = -ADAM_LR * (m_hat / (jnp.sqrt(v_hat) + ADAM_EPS) + ADAM_WD * w)
    return delta, m2, v2


def _adamw_summed(name, parts, w, m, v):
    r = w.shape[0]
    tr = min(ADAM_ROWS, r)

    def body(p_ref, w_ref, m_ref, v_ref, g_ref, d_ref, m2_ref, v2_ref):
        g = p_ref[0]
        for j in range(1, N_DEV):
            g = g + p_ref[j]
        g_ref[...] = g
        d_ref[...], m2_ref[...], v2_ref[...] = _adam_update(g, w_ref[...], m_ref[...], v_ref[...])

    rows = _rows(tr, LANE)
    return pl.pallas_call(
        body, name=name, grid=(r // tr,),
        in_specs=[pl.BlockSpec((N_DEV, tr, LANE), lambda i: (0, i, 0)), rows, rows, rows],
        out_specs=[rows] * 4, out_shape=[jax.ShapeDtypeStruct((r, LANE), F32)] * 4,
        compiler_params=_cparams("parallel"),
    )(parts, w, m, v)


def _modpart(c_all, w_ada, b_cols):
    def body(c_ref, w_ref, b_ref, o_ref):
        o_ref[...] = _nn(c_ref[...].astype(BF16), w_ref[...].astype(BF16)) + b_ref[...]

    return pl.pallas_call(
        body, name="modpart", out_shape=jax.ShapeDtypeStruct((N_DEV, w_ada.shape[1]), F32),
    )(c_all, w_ada, b_cols)


def _adamw_w_ada(c_all_t, dmod_cols, w, m, v):
    def body(c_ref, d_ref, w_ref, m_ref, v_ref, g_ref, dl_ref, m2_ref, v2_ref):
        g = c_ref[:, 0:1] * d_ref[0:1, :]
        for b in range(1, N_DEV):
            g = g + c_ref[:, b:b + 1] * d_ref[b:b + 1, :]
        g_ref[...] = g
        dl_ref[...], m2_ref[...], v2_ref[...] = _adam_update(g, w_ref[...], m_ref[...], v_ref[...])

    return pl.pallas_call(
        body, name="adamw_w_ada", out_shape=[jax.ShapeDtypeStruct(w.shape, F32)] * 4,
        compiler_params=pltpu.CompilerParams(vmem_limit_bytes=VMEM_LIMIT),
    )(c_all_t, dmod_cols, w, m, v)


SHARDED = ("w_in", "w_qb", "w_kvb", "w_out", "conv_w")
REPLICATED = ("b_ada", "q_norm_g", "kv_norm_g", "conv_b", "dt_bias", "a_log", "d_skip", "ssm_norm_g", "ln_g", "ln_b")
WEIGHTS = ("w_ada", "b_ada", "w_in", "q_norm_g", "w_qb", "kv_norm_g", "w_kvb", "conv_w", "conv_b", "dt_bias",
           "a_log", "d_skip", "ssm_norm_g", "w_out", "ln_g", "ln_b")


def _column_blocks(g, name):
    if name == "w_out":
        return g.reshape(N_DEV, g.shape[0] // N_DEV, g.shape[1])
    rows, cols = g.shape
    return g.reshape(rows, N_DEV, cols // N_DEV).transpose(1, 0, 2)


def _from_blocks(blocks, name):
    if name == "w_out":
        return blocks.reshape(-1, blocks.shape[-1])
    n, rows, cols = blocks.shape
    return blocks.transpose(1, 0, 2).reshape(rows, n * cols)


def kernel(x, c, positions, w_ada, b_ada, w_in, q_norm_g, w_qb, kv_norm_g, w_kvb, conv_w, conv_b, dt_bias, a_log, d_skip, ssm_norm_g, w_out, ln_g, ln_b, loss_target, m_w_ada, m_b_ada, m_w_in, m_q_norm_g, m_w_qb, m_kv_norm_g, m_w_kvb, m_conv_w, m_conv_b, m_dt_bias, m_a_log, m_d_skip, m_ssm_norm_g, m_w_out, m_ln_g, m_ln_b, v_w_ada, v_b_ada, v_w_in, v_q_norm_g, v_w_qb, v_kv_norm_g, v_w_kvb, v_conv_w, v_conv_b, v_dt_bias, v_a_log, v_d_skip, v_ssm_norm_g, v_w_out, v_ln_g, v_ln_b):
    given = dict(w_ada=w_ada, b_ada=b_ada, w_in=w_in, q_norm_g=q_norm_g, w_qb=w_qb, kv_norm_g=kv_norm_g, w_kvb=w_kvb,
                 conv_w=conv_w, conv_b=conv_b, dt_bias=dt_bias, a_log=a_log, d_skip=d_skip, ssm_norm_g=ssm_norm_g,
                 w_out=w_out, ln_g=ln_g, ln_b=ln_b)
    mom = dict(w_ada=m_w_ada, b_ada=m_b_ada, w_in=m_w_in, q_norm_g=m_q_norm_g, w_qb=m_w_qb, kv_norm_g=m_kv_norm_g,
               w_kvb=m_w_kvb, conv_w=m_conv_w, conv_b=m_conv_b, dt_bias=m_dt_bias, a_log=m_a_log, d_skip=m_d_skip,
               ssm_norm_g=m_ssm_norm_g, w_out=m_w_out, ln_g=m_ln_g, ln_b=m_ln_b)
    var = dict(w_ada=v_w_ada, b_ada=v_b_ada, w_in=v_w_in, q_norm_g=v_q_norm_g, w_qb=v_w_qb, kv_norm_g=v_kv_norm_g,
               w_kvb=v_w_kvb, conv_w=v_conv_w, conv_b=v_conv_b, dt_bias=v_dt_bias, a_log=v_a_log, d_skip=v_d_skip,
               ssm_norm_g=v_ssm_norm_g, w_out=v_w_out, ln_g=v_ln_g, ln_b=v_ln_b)
    w0 = {k: a[0] for k, a in given.items()}
    m0 = {k: a[0] for k, a in mom.items()}
    v0 = {k: a[0] for k, a in var.items()}
    me = _my_index()

    shard_shapes = [w0[k].shape for k in SHARDED] + [(D_MODEL,)]
    gathered = _exchange("gather_weights", _flat_rows([w0[k] for k in SHARDED] + [c[0]], HALO), gather=True)
    per_dev = [_unflat(gathered[j], shard_shapes) for j in range(N_DEV)]
    full = {k: _from_blocks(jnp.stack([per_dev[j][i] for j in range(N_DEV)]), k) for i, k in enumerate(SHARDED)}
    c_all = jnp.stack([per_dev[j][-1] for j in range(N_DEV)])

    ada_cols = w0["w_ada"].shape[1]
    b_cols = lax.dynamic_slice(w0["b_ada"], (me * ada_cols,), (ada_cols,)).reshape(1, ada_cols)
    mod_all = _exchange("gather_mod", _flat_rows([_modpart(c_all, w0["w_ada"], b_cols)], HALO), gather=True)
    mod_all = mod_all.reshape(N_DEV, -1)[:, :N_DEV * ada_cols].reshape(N_DEV, N_DEV, ada_cols)
    mod = lax.dynamic_index_in_dim(mod_all, me, axis=1, keepdims=False).reshape(-1)

    loc = _local_step(x[0], loss_target[0], positions[0], mod, full["w_in"], w0["q_norm_g"], full["w_qb"],
                      w0["kv_norm_g"], full["w_kvb"], full["conv_w"], w0["conv_b"], w0["dt_bias"], w0["a_log"],
                      w0["d_skip"], w0["ssm_norm_g"], full["w_out"], w0["ln_g"], w0["ln_b"])

    rep_shapes = [w0[k].shape for k in REPLICATED] + [(1,)]
    rep_local = [loc["dmod"]] + [loc[k] for k in REPLICATED[1:]] + [loc["loss"].reshape(1)]
    rep_parts = _exchange("gather_small", _flat_rows(rep_local, HALO), gather=True)
    zero1 = jnp.zeros((1,), F32)
    rep = _adamw_summed("adamw_replicated", rep_parts,
                        _flat_rows([w0[k] for k in REPLICATED] + [zero1], HALO),
                        _flat_rows([m0[k] for k in REPLICATED] + [zero1], HALO),
                        _flat_rows([v0[k] for k in REPLICATED] + [zero1], HALO))
    rep_g, rep_d, rep_m, rep_v = [_unflat(a, rep_shapes) for a in rep]
    loss = rep_g[-1][0]

    dmod_all = rep_parts.reshape(N_DEV, -1)[:, :3 * D_MODEL]
    dmod_cols = lax.dynamic_slice(dmod_all, (0, me * ada_cols), (N_DEV, ada_cols))
    ada = _adamw_w_ada(c_all.T, dmod_cols, w0["w_ada"], m0["w_ada"], v0["w_ada"])

    blocks = [_column_blocks(loc[k], k).reshape(N_DEV, -1) for k in SHARDED]
    send = jnp.concatenate(blocks, axis=1)
    chunk = ADAM_ROWS * LANE
    total = -(-send.shape[1] // chunk) * chunk
    send = jnp.pad(send, ((0, 0), (0, total - send.shape[1]))).reshape(N_DEV, -1, LANE)
    recv = _exchange("scatter_grads", send, gather=False)
    shd_shapes = [w0[k].shape for k in SHARDED]
    shd = _adamw_summed("adamw_sharded", recv,
                        _flat_rows([w0[k] for k in SHARDED], ADAM_ROWS),
                        _flat_rows([m0[k] for k in SHARDED], ADAM_ROWS),
                        _flat_rows([v0[k] for k in SHARDED], ADAM_ROWS))
    shd_g, shd_d, shd_m, shd_v = [_unflat(a, shd_shapes) for a in shd]

    def collect(idx):
        out = {"w_ada": ada[idx]}
        out.update({k: (rep_g, rep_d, rep_m, rep_v)[idx][i] for i, k in enumerate(REPLICATED)})
        out.update({k: (shd_g, shd_d, shd_m, shd_v)[idx][i] for i, k in enumerate(SHARDED)})
        return [out[k][None] for k in WEIGHTS]

    return (loss, loc["grad_x"][None], *collect(0), *collect(1), *collect(2), *collect(3))
```

```python
import functools
import math

import jax
import jax.numpy as jnp
from jax import lax
from jax.experimental import pallas as pl
from jax.experimental.pallas import tpu as pltpu

F32 = jnp.float32
BF16 = jnp.bfloat16

N_DEV = 8
D_MODEL = 1024
MLA_HEADS = 8
QK_NOPE = 128
QK_ROPE = 64
V_DIM = 128
Q_RANK = 384
KV_RANK = 256
QK_HEAD = QK_NOPE + QK_ROPE
HEAD_PAD = 256
ROPE_HALF = QK_ROPE // 2
ROPE_THETA = 10000.0
MLA_WIDTH = MLA_HEADS * V_DIM
SSM_HEADS = 16
SSM_P = 64
SSM_WIDTH = SSM_HEADS * SSM_P
SSM_GROUPS = 2
SSM_N = 128
CONV_K = 4
CHUNK = 128
CONV_CH = SSM_WIDTH + 2 * SSM_GROUPS * SSM_N
MIX_WIDTH = MLA_WIDTH + SSM_WIDTH
IN_SPLITS = (Q_RANK, KV_RANK + QK_ROPE, MLA_WIDTH, CONV_CH, SSM_HEADS, SSM_WIDTH)
IN_WIDTH = sum(IN_SPLITS)
LANE = 128
KV_LAT_PAD = KV_RANK + LANE
IN_PAD = (Q_RANK, KV_LAT_PAD, MLA_WIDTH, CONV_CH, LANE, SSM_WIDTH)
IN_PAD_WIDTH = sum(IN_PAD)
DEEPNORM_ALPHA = 2.0 ** 0.25
RMS_EPS = 1e-6
LN_EPS = 1e-5
ATTN_SCALE = QK_HEAD ** -0.5
LOG2E = math.log2(math.e)
LN2 = math.log(2.0)
Q_PRESCALE = ATTN_SCALE * LOG2E
ADAM_LR, ADAM_B1, ADAM_B2, ADAM_EPS, ADAM_WD, ADAM_STEP = 0.001, 0.9, 0.999, 1e-08, 0.01, 10

ROW_TILE = 256
ATTN_TILE = 512
ATTN_UNROLL = 4
SSD_ROWS = 512
VMEM_LIMIT = 56 * 1024 * 1024


def _nn(a, b):
    return jnp.dot(a, b, preferred_element_type=F32)


def _nt(a, b):
    return lax.dot_general(a, b, (((1,), (1,)), ((), ())), preferred_element_type=F32)


def _tn(a, b):
    return lax.dot_general(a, b, (((0,), (0,)), ((), ())), preferred_element_type=F32)


def _cparams(*sem):
    return pltpu.CompilerParams(dimension_semantics=sem, vmem_limit_bytes=VMEM_LIMIT)


def _rows(tm, w):
    return pl.BlockSpec((tm, w), lambda i: (i, 0))


def _whole(shape):
    return pl.BlockSpec(shape, lambda i: (0,) * len(shape))


def _sigmoid(z):
    return 1.0 / (1.0 + jnp.exp(-z))


def _lane_iota(shape):
    return lax.broadcasted_iota(jnp.int32, shape, len(shape) - 1)


def _swap_halves(r):
    lane = _lane_iota(r.shape)
    return jnp.where(lane < ROPE_HALF, pltpu.roll(r, LANE - ROPE_HALF, 1),
                     jnp.where(lane < QK_ROPE, pltpu.roll(r, ROPE_HALF, 1), 0.0))


def _rope(r, cos, sin):
    return r * cos + _swap_halves(r) * sin


def _rope_transposed(d, cos, sin):
    return d * cos + _swap_halves(d * sin)


def _rms(x):
    rstd = lax.rsqrt(jnp.mean(x * x, axis=-1, keepdims=True) + RMS_EPS)
    return x * rstd, rstd


def _rms_bwd(dxhat, xhat, rstd):
    return rstd * (dxhat - xhat * jnp.mean(dxhat * xhat, axis=-1, keepdims=True))


def _acc_rows(ref, val):
    @pl.when(pl.program_id(0) == 0)
    def _():
        ref[...] = jnp.zeros_like(ref)
    ref[...] += val


def _colsum(v):
    return jnp.sum(v, axis=0, keepdims=True)


def _inproj(x, scale1p, shift, w_in_pt):
    s = x.shape[0]
    tm = ROW_TILE

    def body(x_ref, sc_ref, sh_ref, w_ref, u_ref, *outs):
        u = (x_ref[...] * sc_ref[...] + sh_ref[...]).astype(BF16)
        u_ref[...] = u
        proj = _nt(u, w_ref[...])
        off = 0
        for ref, w in zip(outs, IN_PAD):
            ref[...] = proj[:, off:off + w]
            off += w

    return pl.pallas_call(
        body, name="inproj", grid=(s // tm,),
        in_specs=[_rows(tm, D_MODEL), _whole((1, D_MODEL)), _whole((1, D_MODEL)), _whole((IN_PAD_WIDTH, D_MODEL))],
        out_specs=[_rows(tm, D_MODEL)] + [_rows(tm, w) for w in IN_PAD],
        out_shape=[jax.ShapeDtypeStruct((s, D_MODEL), BF16)] + [jax.ShapeDtypeStruct((s, w), F32) for w in IN_PAD],
        compiler_params=_cparams("parallel"),
    )(x, scale1p, shift, w_in_pt)


def _qpath(q_lat, g_q, w_qb_p, cos, sin):
    s = q_lat.shape[0]
    tm = ROW_TILE

    def body(ql_ref, g_ref, w_ref, cos_ref, sin_ref, nq_ref, q_ref):
        xhat, _ = _rms(ql_ref[...])
        nq = (xhat * g_ref[...]).astype(BF16)
        nq_ref[...] = nq
        raw = _nn(nq, w_ref[...]) * Q_PRESCALE
        c, sn = cos_ref[...], sin_ref[...]
        for h in range(MLA_HEADS):
            o = h * HEAD_PAD
            q_ref[:, o:o + QK_NOPE] = raw[:, o:o + QK_NOPE].astype(BF16)
            q_ref[:, o + QK_NOPE:o + HEAD_PAD] = _rope(raw[:, o + QK_NOPE:o + HEAD_PAD], c, sn).astype(BF16)

    return pl.pallas_call(
        body, name="qpath", grid=(s // tm,),
        in_specs=[_rows(tm, Q_RANK), _whole((1, Q_RANK)), _whole((Q_RANK, MLA_HEADS * HEAD_PAD)),
                  _rows(tm, LANE), _rows(tm, LANE)],
        out_specs=[_rows(tm, Q_RANK), _rows(tm, MLA_HEADS * HEAD_PAD)],
        out_shape=[jax.ShapeDtypeStruct((s, Q_RANK), BF16), jax.ShapeDtypeStruct((s, MLA_HEADS * HEAD_PAD), BF16)],
        compiler_params=_cparams("parallel"),
    )(q_lat, g_q, w_qb_p, cos, sin)


def _kvpath(kv_lat, g_kv, w_kvb_p, cos, sin):
    s = kv_lat.shape[0]
    tm = ROW_TILE

    def body(kl_ref, g_ref, w_ref, cos_ref, sin_ref, nkv_ref, k_ref, v_ref):
        kl = kl_ref[...]
        xhat, _ = _rms(kl[:, :KV_RANK])
        nkv = (xhat * g_ref[...]).astype(BF16)
        nkv_ref[...] = nkv
        raw = _nn(nkv, w_ref[...])
        kr = _rope(kl[:, KV_RANK:], cos_ref[...], sin_ref[...]).astype(BF16)
        for h in range(MLA_HEADS):
            o = h * HEAD_PAD
            k_ref[:, o:o + QK_NOPE] = raw[:, h * QK_NOPE:(h + 1) * QK_NOPE].astype(BF16)
            k_ref[:, o + QK_NOPE:o + HEAD_PAD] = kr
        v_ref[...] = raw[:, MLA_HEADS * QK_NOPE:].astype(BF16)

    return pl.pallas_call(
        body, name="kvpath", grid=(s // tm,),
        in_specs=[_rows(tm, KV_LAT_PAD), _whole((1, KV_RANK)), _whole((KV_RANK, MLA_HEADS * (QK_NOPE + V_DIM))),
                  _rows(tm, LANE), _rows(tm, LANE)],
        out_specs=[_rows(tm, KV_RANK), _rows(tm, MLA_HEADS * HEAD_PAD), _rows(tm, MLA_WIDTH)],
        out_shape=[jax.ShapeDtypeStruct((s, KV_RANK), BF16), jax.ShapeDtypeStruct((s, MLA_HEADS * HEAD_PAD), BF16),
                   jax.ShapeDtypeStruct((s, MLA_WIDTH), BF16)],
        compiler_params=_cparams("parallel"),
    )(kv_lat, g_kv, w_kvb_p, cos, sin)


def _causal_mask(t):
    row = lax.broadcasted_iota(jnp.int32, (t, t), 0)
    col = lax.broadcasted_iota(jnp.int32, (t, t), 1)
    return row, col


def _attn_fwd(q, k, v):
    s = q.shape[0]
    t = min(ATTN_TILE, s)
    nq = s // t

    def body(q_ref, k_ref, v_ref, o_ref, lse_ref, m_sc, l_sc, acc_sc, sa_sc, sb_sc):
        i = pl.program_id(1)
        qv = q_ref[...]
        m_sc[...] = jnp.full(m_sc.shape, -jnp.inf, F32)
        l_sc[...] = jnp.zeros(l_sc.shape, F32)
        acc_sc[...] = jnp.zeros(acc_sc.shape, F32)

        def scores(j, s_ref):
            s_ref[...] = _nt(qv, k_ref[pl.ds(pl.multiple_of(j * t, t), t), :])

        def update(s_ref, j, masked):
            vv = v_ref[pl.ds(pl.multiple_of(j * t, t), t), :]
            sc = s_ref[...]
            if masked:
                row, col = _causal_mask(t)
                sc = jnp.where(col <= row, sc, -jnp.inf)
            m_prev = m_sc[...]
            m_new = jnp.maximum(m_prev, jnp.max(sc, axis=1, keepdims=True))
            alpha = jnp.exp2(m_prev - m_new)
            p = jnp.exp2(sc - jnp.tile(m_new, (1, t // LANE)))
            l_sc[...] = alpha * l_sc[...] + jnp.sum(p, axis=1, keepdims=True)
            acc_sc[...] = alpha * acc_sc[...] + _nn(p.astype(BF16), vv)
            m_sc[...] = m_new

        def run(j0, count):
            bufs = (sa_sc, sb_sc)
            for u in range(count):
                scores(j0 + u + 1, bufs[(u + 1) % 2])
                update(bufs[u % 2], j0 + u, False)

        def quad(qq, carry):
            run(ATTN_UNROLL * qq, ATTN_UNROLL)
            return carry

        def pair(pp, carry):
            run(base2 + 2 * pp, 2)
            return carry

        scores(0, sa_sc)
        nquad = lax.div(i, ATTN_UNROLL)
        base2 = ATTN_UNROLL * nquad
        lax.fori_loop(0, nquad, quad, 0)
        lax.fori_loop(0, lax.div(i - base2, 2), pair, 0)
        odd = lax.rem(i, 2)

        @pl.when(odd == 1)
        def _():
            scores(i, sb_sc)
            update(sa_sc, i - 1, False)
            update(sb_sc, i, True)

        @pl.when(odd == 0)
        def _():
            update(sa_sc, i, True)

        l = l_sc[...]
        o_ref[...] = acc_sc[...] / l
        lse_ref[...] = m_sc[...] + jnp.log2(l)

    return pl.pallas_call(
        body, name="attn_fwd", grid=(MLA_HEADS, nq),
        in_specs=[pl.BlockSpec((t, HEAD_PAD), lambda h, i: (i, h)),
                  pl.BlockSpec((s, HEAD_PAD), lambda h, i: (0, h)),
                  pl.BlockSpec((s, V_DIM), lambda h, i: (0, h))],
        out_specs=[pl.BlockSpec((t, V_DIM), lambda h, i: (i, h)), pl.BlockSpec((t, LANE), lambda h, i: (i, h))],
        out_shape=[jax.ShapeDtypeStruct((s, MLA_WIDTH), F32), jax.ShapeDtypeStruct((s, MLA_HEADS * LANE), F32)],
        scratch_shapes=[pltpu.VMEM((t, LANE), F32), pltpu.VMEM((t, LANE), F32), pltpu.VMEM((t, V_DIM), F32),
                        pltpu.VMEM((t, t), F32), pltpu.VMEM((t, t), F32)],
        compiler_params=_cparams("parallel", "arbitrary"),
    )(q, k, v)


def _attn_bwd(q, k, v, do, lse_row, delta_row):
    s = q.shape[0]
    t = min(ATTN_TILE, s)
    nq = s // t

    def body(q_ref, k_ref, v_ref, do_ref, lse_ref, dl_ref, dk_ref, dv_ref, dq_hbm,
             dq_sc, dk_sc, dv_sc, sa_sc, sb_sc, pa_sc, pb_sc, sem):
        h = pl.program_id(0)
        j = pl.program_id(1)
        kv_ = k_ref[...]
        vv = v_ref[...]

        @pl.when(j == 0)
        def _():
            dq_sc[...] = jnp.zeros(dq_sc.shape, F32)

        dk_sc[...] = jnp.zeros(dk_sc.shape, F32)
        dv_sc[...] = jnp.zeros(dv_sc.shape, F32)

        def scores(i, s_ref, p_ref):
            off = pl.multiple_of(i * t, t)
            s_ref[...] = _nt(kv_, q_ref[pl.ds(off, t), :])
            p_ref[...] = _nt(vv, do_ref[pl.ds(off, t), :])

        def update(i, s_ref, p_ref, masked):
            off = pl.multiple_of(i * t, t)
            qv = q_ref[pl.ds(off, t), :]
            dov = do_ref[pl.ds(off, t), :]
            sct = s_ref[...]
            if masked:
                row, col = _causal_mask(t)
                sct = jnp.where(row <= col, sct, -jnp.inf)
            pt = jnp.exp2(sct - lse_ref[0, :, pl.ds(off, t)])
            gt = (pt * (p_ref[...] - dl_ref[0, :, pl.ds(off, t)])).astype(BF16)
            dv_sc[...] += _nn(pt.astype(BF16), dov)
            dk_sc[...] += _nn(gt, qv)
            dq_sc[pl.ds(off, t), :] += _tn(gt, kv_)

        rest = nq - 1 - j
        scores(j, sa_sc, pa_sc)

        @pl.when(rest >= 1)
        def _():
            scores(j + 1, sb_sc, pb_sc)

        update(j, sa_sc, pa_sc, True)

        def run(i0, count):
            bufs = ((sb_sc, pb_sc), (sa_sc, pa_sc))
            for u in range(count):
                scores(i0 + u + 1, *bufs[(u + 1) % 2])
                update(i0 + u, *bufs[u % 2], False)

        def quad(qq, carry):
            run(j + 1 + ATTN_UNROLL * qq, ATTN_UNROLL)
            return carry

        def pair(pp, carry):
            run(base2 + 2 * pp, 2)
            return carry

        nquad = jnp.where(rest >= 1, lax.div(rest - 1, ATTN_UNROLL), 0)
        base2 = j + 1 + ATTN_UNROLL * nquad
        rest2 = rest - ATTN_UNROLL * nquad
        npairs = jnp.where(rest2 >= 1, lax.div(rest2 - 1, 2), 0)
        lax.fori_loop(0, nquad, quad, 0)
        lax.fori_loop(0, npairs, pair, 0)
        left = rest2 - 2 * npairs
        i1 = base2 + 2 * npairs

        @pl.when(left == 1)
        def _():
            update(i1, sb_sc, pb_sc, False)

        @pl.when(left == 2)
        def _():
            scores(i1 + 1, sa_sc, pa_sc)
            update(i1, sb_sc, pb_sc, False)
            update(i1 + 1, sa_sc, pa_sc, False)

        dk_ref[...] = dk_sc[...] * LN2
        dv_ref[...] = dv_sc[...]

        @pl.when(j == nq - 1)
        def _():
            cp = pltpu.make_async_copy(dq_sc, dq_hbm.at[h], sem)
            cp.start()
            cp.wait()

    return pl.pallas_call(
        body, name="attn_bwd", grid=(MLA_HEADS, nq),
        in_specs=[pl.BlockSpec((s, HEAD_PAD), lambda h, j: (0, h)),
                  pl.BlockSpec((t, HEAD_PAD), lambda h, j: (j, h)),
                  pl.BlockSpec((t, V_DIM), lambda h, j: (j, h)),
                  pl.BlockSpec((s, V_DIM), lambda h, j: (0, h)),
                  pl.BlockSpec((1, 1, s), lambda h, j: (h, 0, 0)),
                  pl.BlockSpec((1, 1, s), lambda h, j: (h, 0, 0))],
        out_specs=[pl.BlockSpec((t, HEAD_PAD), lambda h, j: (j, h)), pl.BlockSpec((t, V_DIM), lambda h, j: (j, h)),
                   pl.BlockSpec(memory_space=pl.ANY)],
        out_shape=[jax.ShapeDtypeStruct((s, MLA_HEADS * HEAD_PAD), F32), jax.ShapeDtypeStruct((s, MLA_WIDTH), F32),
                   jax.ShapeDtypeStruct((MLA_HEADS, s, HEAD_PAD), F32)],
        scratch_shapes=[pltpu.VMEM((s, HEAD_PAD), F32), pltpu.VMEM((t, HEAD_PAD), F32), pltpu.VMEM((t, V_DIM), F32),
                        pltpu.VMEM((t, t), F32), pltpu.VMEM((t, t), F32), pltpu.VMEM((t, t), F32),
                        pltpu.VMEM((t, t), F32), pltpu.SemaphoreType.DMA],
        compiler_params=_cparams("arbitrary", "arbitrary"),
    )(q, k, v, do, lse_row, delta_row)


HALO = 8


def _silu(z):
    return z * _sigmoid(z)


def _silu_grad(z):
    sg = _sigmoid(z)
    return sg * (1.0 + z * (1.0 - sg))


def _softplus(x):
    e = jnp.exp(-jnp.abs(x))
    small = e * (1.0 - e * (0.5 - e * (1.0 / 3.0)))
    return jnp.maximum(x, 0.0) + jnp.where(e < 1e-3, small, jnp.log(1.0 + e))


def _conv_taps(xe_ref, w, tm, first):
    acc = None
    for k in range(CONV_K):
        term = xe_ref[pl.ds(HALO + first - (CONV_K - 1) + k, tm), :] * w[k:k + 1, :]
        acc = term if acc is None else acc + term
    return acc


def _ssd_pre(xbc_raw, dt_raw, conv_w, conv_b, dt_bias_p):
    s = xbc_raw.shape[0]
    tm = ROW_TILE
    hb = tm // HALO

    def body(x_ref, prev_ref, dtr_ref, w_ref, b_ref, db_ref, act_ref, dt_ref, xe_sc):
        i = pl.program_id(0)
        xe_sc[pl.ds(0, HALO), :] = jnp.where(i > 0, prev_ref[...], 0.0)
        xe_sc[pl.ds(HALO, tm), :] = x_ref[...]
        pre = _conv_taps(xe_sc, w_ref[...], tm, 0) + b_ref[...]
        act_ref[...] = _silu(pre)
        dt_ref[...] = _softplus(dtr_ref[...] + db_ref[...])

    return pl.pallas_call(
        body, name="ssd_pre", grid=(s // tm,),
        in_specs=[_rows(tm, CONV_CH), pl.BlockSpec((HALO, CONV_CH), lambda i: (jnp.maximum(i * hb - 1, 0), 0)),
                  _rows(tm, LANE), _whole((CONV_K, CONV_CH)), _whole((1, CONV_CH)), _whole((1, LANE))],
        out_specs=[_rows(tm, CONV_CH), _rows(tm, LANE)],
        out_shape=[jax.ShapeDtypeStruct((s, CONV_CH), F32), jax.ShapeDtypeStruct((s, LANE), F32)],
        scratch_shapes=[pltpu.VMEM((tm + HALO, CONV_CH), F32)],
        compiler_params=_cparams("parallel"),
    )(xbc_raw, xbc_raw, dt_raw, conv_w, conv_b, dt_bias_p)


def _split3(a):
    a1 = a.astype(BF16)
    r1 = a - a1.astype(F32)
    a2 = r1.astype(BF16)
    a3 = (r1 - a2.astype(F32)).astype(BF16)
    return a1, a2, a3


def _tri_left(tri, a):
    a1, a2, a3 = _split3(a)
    return _nn(tri, a1) + _nn(tri, a2) + _nn(tri, a3)


def _tri_right(a, tri):
    a1, a2, a3 = _split3(a)
    return _nn(a1, tri) + _nn(a2, tri) + _nn(a3, tri)


def _pair_sel(lane_lo, col_a, col_b):
    return jnp.where(lane_lo, col_a, col_b)


def _chunk_common(dt, a_neg, tril, triu):
    a = dt * a_neg
    lam_c = _tri_left(tril, a)
    lam_r = _tri_right(a.T, triu)
    lam_last = lam_c[CHUNK - 1:CHUNK, :]
    return lam_c, lam_r, lam_last


def _gated_norm_fwd(y, z, g):
    hf = y * _silu(z)
    outs = []
    for grp in range(SSM_GROUPS):
        w = SSM_WIDTH // SSM_GROUPS
        n, _ = _rms(hf[:, grp * w:(grp + 1) * w])
        outs.append(n)
    return jnp.concatenate(outs, axis=1) * g


def _ssd_fwd(xbc, dt, z, a_neg, dskip_x, g_x, tril, triu):
    s = xbc.shape[0]
    tm = min(SSD_ROWS, s)
    cpb = tm // CHUNK
    nc = s // CHUNK

    def body(xbc_ref, dt_ref, z_ref, a_ref, dsk_ref, g_ref, tril_ref, triu_ref, y_ref, o_ref, hin_ref, h_sc):
        @pl.when(pl.program_id(0) == 0)
        def _():
            h_sc[...] = jnp.zeros(h_sc.shape, F32)

        tril, triu = tril_ref[...], triu_ref[...]
        ltri = tril > 0
        lane_lo = _lane_iota((CHUNK, LANE)) < SSM_P
        lane_lo_n = lane_lo

        def chunk(c, carry):
            r0 = pl.multiple_of(c * CHUNK, CHUNK)
            dtc = dt_ref[pl.ds(r0, CHUNK), :]
            lam_c, lam_r, lam_last = _chunk_common(dtc, a_ref[...], tril, triu)
            e_c = jnp.exp(lam_c)
            f_r = jnp.exp(lam_r[:, CHUNK - 1:CHUNK] - lam_r)
            cd = jnp.exp(lam_last)
            for grp in range(SSM_GROUPS):
                bo = SSM_WIDTH + grp * SSM_N
                co = SSM_WIDTH + SSM_GROUPS * SSM_N + grp * SSM_N
                bm = xbc_ref[pl.ds(r0, CHUNK), bo:bo + SSM_N]
                cm = xbc_ref[pl.ds(r0, CHUNK), co:co + SSM_N]
                cm_b = cm.astype(BF16)
                gmat = _nt(cm_b, bm.astype(BF16))
                bt = bm.T
                for pj in range(SSM_HEADS // SSM_GROUPS // 2):
                    ha = grp * (SSM_HEADS // SSM_GROUPS) + 2 * pj
                    hb_ = ha + 1
                    lo = ha * SSM_P
                    xs = xbc_ref[pl.ds(r0, CHUNK), lo:lo + LANE]
                    x2 = xs * _pair_sel(lane_lo, dtc[:, ha:ha + 1], dtc[:, hb_:hb_ + 1])
                    x2b = x2.astype(BF16)
                    ys, sts = [], []
                    for hh in (ha, hb_):
                        seg = lam_c[:, hh:hh + 1] - lam_r[hh:hh + 1, :]
                        dec = jnp.exp(jnp.where(ltri, seg, -jnp.inf))
                        ys.append(_nn((gmat * dec).astype(BF16), x2b))
                        sts.append(_nn((bt * f_r[hh:hh + 1, :]).astype(BF16), x2b))
                    hp = h_sc[:, lo:lo + LANE]
                    hin_ref[c, :, lo:lo + LANE] = hp
                    zz = _nn(cm_b, hp.astype(BF16))
                    e2 = _pair_sel(lane_lo, e_c[:, ha:ha + 1], e_c[:, hb_:hb_ + 1])
                    yv = jnp.where(lane_lo, ys[0], ys[1]) + e2 * zz
                    y_ref[pl.ds(r0, CHUNK), lo:lo + LANE] = yv + xs * dsk_ref[:, lo:lo + LANE]
                    cd2 = _pair_sel(lane_lo_n, cd[:, ha:ha + 1], cd[:, hb_:hb_ + 1])
                    h_sc[:, lo:lo + LANE] = hp * cd2 + jnp.where(lane_lo_n, sts[0], sts[1])
            return carry

        lax.fori_loop(0, cpb, chunk, 0)
        o_ref[...] = _gated_norm_fwd(y_ref[...], z_ref[...], g_ref[...])

    return pl.pallas_call(
        body, name="ssd_fwd", grid=(s // tm,),
        in_specs=[_rows(tm, CONV_CH), _rows(tm, LANE), _rows(tm, SSM_WIDTH), _whole((1, LANE)),
                  _whole((1, SSM_WIDTH)), _whole((1, SSM_WIDTH)), _whole((CHUNK, CHUNK)), _whole((CHUNK, CHUNK))],
        out_specs=[_rows(tm, SSM_WIDTH), _rows(tm, SSM_WIDTH),
                   pl.BlockSpec((cpb, SSM_N, SSM_WIDTH), lambda i: (i, 0, 0))],
        out_shape=[jax.ShapeDtypeStruct((s, SSM_WIDTH), F32), jax.ShapeDtypeStruct((s, SSM_WIDTH), F32),
                   jax.ShapeDtypeStruct((nc, SSM_N, SSM_WIDTH), F32)],
        scratch_shapes=[pltpu.VMEM((SSM_N, SSM_WIDTH), F32)],
        compiler_params=_cparams("arbitrary"),
    )(xbc, dt, z, a_neg, dskip_x, g_x, tril, triu)


def _outln(o, z_attn, o_ssm, w_out, x, gate, ln_g, ln_b, tgt):
    s = x.shape[0]
    tm = ROW_TILE

    def body(o_ref, z_ref, os_ref, w_ref, x_ref, gate_ref, g_ref, b_ref, t_ref,
             cat_ref, dmix_ref, gx_ref, do_ref, dz_ref, dl_ref, dos_ref, loss_ref, dg_ref, db_ref, dgate_ref):
        ov, zv = o_ref[...], z_ref[...]
        sz = _silu(zv)
        cat_ref[:, :MLA_WIDTH] = (ov * sz).astype(BF16)
        cat_ref[:, MLA_WIDTH:] = os_ref[...].astype(BF16)
        w = w_ref[...]
        mixed = _nn(cat_ref[...], w)
        gate_v = gate_ref[...]
        hv = DEEPNORM_ALPHA * x_ref[...] + gate_v * mixed
        mu = jnp.mean(hv, axis=-1, keepdims=True)
        hc = hv - mu
        rstd = lax.rsqrt(jnp.mean(hc * hc, axis=-1, keepdims=True) + LN_EPS)
        xhat = hc * rstd
        g = g_ref[...]
        err = xhat * g + b_ref[...] - t_ref[...]
        _acc_rows(loss_ref, jnp.full((1, LANE), (0.5 / D_MODEL) * jnp.sum(err * err), F32))
        dy = err * (1.0 / D_MODEL)
        _acc_rows(dg_ref, _colsum(dy * xhat))
        _acc_rows(db_ref, _colsum(dy))
        dxhat = dy * g
        dh = rstd * (dxhat - jnp.mean(dxhat, axis=-1, keepdims=True)
                     - xhat * jnp.mean(dxhat * xhat, axis=-1, keepdims=True))
        gx_ref[...] = DEEPNORM_ALPHA * dh
        _acc_rows(dgate_ref, _colsum(dh * mixed))
        dmix = (gate_v * dh).astype(BF16)
        dmix_ref[...] = dmix
        dcat = _nt(dmix, w)
        da = dcat[:, :MLA_WIDTH]
        dos_ref[...] = dcat[:, MLA_WIDTH:]
        dov = da * sz
        do_ref[...] = dov.astype(BF16)
        dz_ref[...] = da * ov * _silu_grad(zv)
        prod = dov * ov
        for h in range(MLA_HEADS):
            dsum = jnp.sum(prod[:, h * V_DIM:(h + 1) * V_DIM], axis=1, keepdims=True)
            dl_ref[:, h * LANE:(h + 1) * LANE] = jnp.broadcast_to(dsum, (tm, LANE))

    vec = _whole((1, D_MODEL))
    return pl.pallas_call(
        body, name="outln", grid=(s // tm,),
        in_specs=[_rows(tm, MLA_WIDTH), _rows(tm, MLA_WIDTH), _rows(tm, SSM_WIDTH), _whole((MIX_WIDTH, D_MODEL)),
                  _rows(tm, D_MODEL), vec, vec, vec, _rows(tm, D_MODEL)],
        out_specs=[_rows(tm, MIX_WIDTH), _rows(tm, D_MODEL), _rows(tm, D_MODEL), _rows(tm, MLA_WIDTH),
                   _rows(tm, MLA_WIDTH), _rows(tm, MLA_HEADS * LANE), _rows(tm, SSM_WIDTH),
                   _whole((1, LANE)), vec, vec, vec],
        out_shape=[jax.ShapeDtypeStruct((s, MIX_WIDTH), BF16), jax.ShapeDtypeStruct((s, D_MODEL), BF16),
                   jax.ShapeDtypeStruct((s, D_MODEL), F32), jax.ShapeDtypeStruct((s, MLA_WIDTH), BF16),
                   jax.ShapeDtypeStruct((s, MLA_WIDTH), F32), jax.ShapeDtypeStruct((s, MLA_HEADS * LANE), F32),
                   jax.ShapeDtypeStruct((s, SSM_WIDTH), F32), jax.ShapeDtypeStruct((1, LANE), F32),
                   jax.ShapeDtypeStruct((1, D_MODEL), F32), jax.ShapeDtypeStruct((1, D_MODEL), F32),
                   jax.ShapeDtypeStruct((1, D_MODEL), F32)],
        compiler_params=_cparams("arbitrary"),
    )(o, z_attn, o_ssm, w_out, x, gate, ln_g, ln_b, tgt)


def _ssd_bwd(dos, y, z, xbc, dt, hin, a_neg, dskip_x, g_x, tril, triu):
    s = xbc.shape[0]
    tm = min(SSD_ROWS, s)
    cpb = tm // CHUNK
    nb = s // tm
    gw = SSM_WIDTH // SSM_GROUPS
    hpg = SSM_HEADS // SSM_GROUPS

    def body(dos_ref, y_ref, z_ref, xbc_ref, dt_ref, hin_ref, a_ref, dsk_ref, g_ref, tril_ref, triu_ref,
             dxbc_ref, ddt_ref, dz_ref, dg_ref, ddsk_ref, da_ref, dh_sc, dy_sc):
        @pl.when(pl.program_id(0) == 0)
        def _():
            dh_sc[...] = jnp.zeros(dh_sc.shape, F32)

        yv, zv, dov = y_ref[...], z_ref[...], dos_ref[...]
        sz = _silu(zv)
        hf = yv * sz
        gv = g_ref[...]
        dgs, dhfs = [], []
        for grp in range(SSM_GROUPS):
            sl = slice(grp * gw, (grp + 1) * gw)
            n, rstd = _rms(hf[:, sl])
            dgs.append(_colsum(dov[:, sl] * n))
            dhfs.append(_rms_bwd(dov[:, sl] * gv[:, sl], n, rstd))
        dhf = jnp.concatenate(dhfs, axis=1)
        _acc_rows(dg_ref, jnp.concatenate(dgs, axis=1))
        dy_sc[...] = dhf * sz
        dz_ref[...] = dhf * yv * _silu_grad(zv)

        tril, triu = tril_ref[...], triu_ref[...]
        ltri = tril > 0
        utri = triu > 0
        lane = _lane_iota((CHUNK, LANE))
        lane1 = _lane_iota((1, LANE))
        lane_lo = lane < SSM_P
        row_last = lax.broadcasted_iota(jnp.int32, (CHUNK, LANE), 0) == CHUNK - 1
        a_neg_v = a_ref[...]

        def chunk(ci, carry):
            dsk_acc, da_acc = carry
            cl = cpb - 1 - ci
            r0 = pl.multiple_of(cl * CHUNK, CHUNK)
            rows = pl.ds(r0, CHUNK)
            dtc = dt_ref[rows, :]
            lam_c, lam_r, lam_last = _chunk_common(dtc, a_neg_v, tril, triu)
            e_c = jnp.exp(lam_c)
            f_c = jnp.exp(lam_last - lam_c)
            cd = jnp.exp(lam_last)
            dlam = jnp.zeros((CHUNK, LANE), F32)
            dlast = jnp.zeros((1, LANE), F32)
            ddt_x = jnp.zeros((CHUNK, LANE), F32)
            dsk_parts = []
            for grp in range(SSM_GROUPS):
                bo = SSM_WIDTH + grp * SSM_N
                co = SSM_WIDTH + SSM_GROUPS * SSM_N + grp * SSM_N
                bm = xbc_ref[rows, bo:bo + SSM_N]
                cm = xbc_ref[rows, co:co + SSM_N]
                bm_b, cm_b = bm.astype(BF16), cm.astype(BF16)
                gmat = _nt(cm_b, bm_b)
                gmat_t = _nt(bm_b, cm_b)
                ct_b = cm.T.astype(BF16)
                acc_dg = jnp.zeros((CHUNK, CHUNK), F32)
                acc_dgt = jnp.zeros((CHUNK, CHUNK), F32)
                d_b = jnp.zeros((CHUNK, SSM_N), F32)
                d_c = jnp.zeros((CHUNK, SSM_N), F32)
                for pj in range(hpg // 2):
                    ha = grp * hpg + 2 * pj
                    hb_ = ha + 1
                    lo = ha * SSM_P
                    xs = xbc_ref[rows, lo:lo + LANE]
                    dt2 = _pair_sel(lane_lo, dtc[:, ha:ha + 1], dtc[:, hb_:hb_ + 1])
                    x2 = xs * dt2
                    x2b = x2.astype(BF16)
                    dy2 = dy_sc[rows, lo:lo + LANE]
                    dy2b = dy2.astype(BF16)
                    hp = hin_ref[cl, :, lo:lo + LANE]
                    hp_b = hp.astype(BF16)
                    dhn = dh_sc[:, lo:lo + LANE]
                    dhn_b = dhn.astype(BF16)
                    e2 = _pair_sel(lane_lo, e_c[:, ha:ha + 1], e_c[:, hb_:hb_ + 1])
                    yo = e2 * _nn(cm_b, hp_b)
                    dzz_b = (e2 * dy2).astype(BF16)
                    d_c = d_c + _nt(dzz_b, hp_b)
                    cd2 = _pair_sel(lane_lo, cd[:, ha:ha + 1], cd[:, hb_:hb_ + 1])
                    dh_sc[:, lo:lo + LANE] = _nn(ct_b, dzz_b) + cd2 * dhn
                    t_yo = dy2 * yo
                    t_hh = dhn * hp
                    dx2 = jnp.zeros((CHUNK, LANE), F32)
                    for hh, msk in ((ha, lane_lo), (hb_, jnp.logical_not(lane_lo))):
                        x2h_b = jnp.where(msk, x2, 0.0).astype(BF16)
                        dy2h_b = jnp.where(msk, dy2, 0.0).astype(BF16)
                        lc = lam_c[:, hh:hh + 1]
                        lr = lam_r[hh:hh + 1, :]
                        dec = jnp.exp(jnp.where(ltri, lc - lr, -jnp.inf))
                        dect = jnp.exp(jnp.where(utri, lr - lc, -jnp.inf))
                        dmd = _nt(dy2h_b, x2b) * dec
                        dmtd = _nt(x2h_b, dy2b) * dect
                        acc_dg = acc_dg + dmd
                        acc_dgt = acc_dgt + dmtd
                        w_row = jnp.sum(dmd * gmat, axis=1, keepdims=True)
                        wt_row = jnp.sum(dmtd * gmat_t, axis=1, keepdims=True)
                        fcol = f_c[:, hh:hh + 1]
                        dx_h = _nn((gmat_t * dect).astype(BF16), dy2b) + _nn((bm * fcol).astype(BF16), dhn_b)
                        qh = _nt(x2h_b, dhn_b)
                        d_b = d_b + fcol * qh
                        dff = jnp.sum(bm * qh, axis=1, keepdims=True) * fcol
                        yo_row = jnp.sum(jnp.where(msk, t_yo, 0.0), axis=1, keepdims=True)
                        dlam_h = w_row - wt_row + yo_row - dff
                        hh_sum = jnp.sum(jnp.sum(jnp.where(msk, t_hh, 0.0), axis=1, keepdims=True), axis=0, keepdims=True)
                        last_h = cd[:, hh:hh + 1] * hh_sum + jnp.sum(dff, axis=0, keepdims=True)
                        dlam = jnp.where(lane == hh, dlam_h, dlam)
                        dlast = jnp.where(lane1 == hh, last_h, dlast)
                        dx2 = jnp.where(msk, dx_h, dx2)
                    dxbc_ref[rows, lo:lo + LANE] = dx2 * dt2 + dy2 * dsk_ref[:, lo:lo + LANE]
                    prod = dx2 * xs
                    for hh, msk in ((ha, lane_lo), (hb_, jnp.logical_not(lane_lo))):
                        col = jnp.sum(jnp.where(msk, prod, 0.0), axis=1, keepdims=True)
                        ddt_x = jnp.where(lane == hh, col, ddt_x)
                    dsk_parts.append(_colsum(dy2 * xs))
                d_c = d_c + _nn(acc_dg.astype(BF16), bm_b)
                d_b = d_b + _nn(acc_dgt.astype(BF16), cm_b)
                dxbc_ref[rows, bo:bo + SSM_N] = d_b
                dxbc_ref[rows, co:co + SSM_N] = d_c
            dlam = dlam + jnp.where(row_last, dlast, 0.0)
            da = _tri_left(triu, dlam)
            ddt_ref[rows, :] = da * a_neg_v + ddt_x
            return dsk_acc + jnp.concatenate(dsk_parts, axis=1), da_acc + _colsum(da * dtc)

        dsk_tot, da_tot = lax.fori_loop(
            0, cpb, chunk, (jnp.zeros((1, SSM_WIDTH), F32), jnp.zeros((1, LANE), F32)))
        _acc_rows(ddsk_ref, dsk_tot)
        _acc_rows(da_ref, da_tot)

    rev = lambda i: (nb - 1 - i, 0)
    rrows = lambda w: pl.BlockSpec((tm, w), rev)
    return pl.pallas_call(
        body, name="ssd_bwd", grid=(nb,),
        in_specs=[rrows(SSM_WIDTH), rrows(SSM_WIDTH), rrows(SSM_WIDTH), rrows(CONV_CH), rrows(LANE),
                  pl.BlockSpec((cpb, SSM_N, SSM_WIDTH), lambda i: (nb - 1 - i, 0, 0)),
                  _whole((1, LANE)), _whole((1, SSM_WIDTH)), _whole((1, SSM_WIDTH)),
                  _whole((CHUNK, CHUNK)), _whole((CHUNK, CHUNK))],
        out_specs=[rrows(CONV_CH), rrows(LANE), rrows(SSM_WIDTH),
                   _whole((1, SSM_WIDTH)), _whole((1, SSM_WIDTH)), _whole((1, LANE))],
        out_shape=[jax.ShapeDtypeStruct((s, CONV_CH), F32), jax.ShapeDtypeStruct((s, LANE), F32),
                   jax.ShapeDtypeStruct((s, SSM_WIDTH), F32), jax.ShapeDtypeStruct((1, SSM_WIDTH), F32),
                   jax.ShapeDtypeStruct((1, SSM_WIDTH), F32), jax.ShapeDtypeStruct((1, LANE), F32)],
        scratch_shapes=[pltpu.VMEM((SSM_N, SSM_WIDTH), F32), pltpu.VMEM((tm, SSM_WIDTH), F32)],
        compiler_params=_cparams("arbitrary"),
    )(dos, y, z, xbc, dt, hin, a_neg, dskip_x, g_x, tril, triu)


def _ssd_post_bwd(xbc_raw, dxa, ddt, dt_raw, conv_w, conv_b, dt_bias_p):
    s = xbc_raw.shape[0]
    tm = ROW_TILE
    hb = tm // HALO
    nt = s // tm
    ext = tm + HALO

    def body(x_ref, prev_ref, next_ref, d_ref, dnext_ref, ddt_ref, dtr_ref, w_ref, b_ref, db_ref,
             dx_ref, ddtr_ref, dw_ref, dcb_ref, ddb_ref, xe_sc, de_sc):
        i = pl.program_id(0)
        w = w_ref[...]
        xe_sc[pl.ds(0, HALO), :] = jnp.where(i > 0, prev_ref[...], 0.0)
        xe_sc[pl.ds(HALO, tm), :] = x_ref[...]
        xe_sc[pl.ds(HALO + tm, HALO), :] = next_ref[...]
        pre = _conv_taps(xe_sc, w, ext, 0) + b_ref[...]
        sg = _silu_grad(pre)
        de_sc[pl.ds(0, tm), :] = d_ref[...] * sg[:tm]
        de_sc[pl.ds(tm, HALO), :] = jnp.where(i < nt - 1, dnext_ref[...] * sg[tm:], 0.0)
        dconv = de_sc[pl.ds(0, tm), :]
        acc = None
        dws = []
        for k in range(CONV_K):
            term = de_sc[pl.ds(CONV_K - 1 - k, tm), :] * w[k:k + 1, :]
            acc = term if acc is None else acc + term
            dws.append(_colsum(dconv * xe_sc[pl.ds(HALO - (CONV_K - 1) + k, tm), :]))
        dx_ref[...] = acc
        _acc_rows(dw_ref, jnp.concatenate(dws, axis=0))
        _acc_rows(dcb_ref, _colsum(dconv))
        ddtr = ddt_ref[...] * _sigmoid(dtr_ref[...] + db_ref[...])
        ddtr_ref[...] = ddtr
        _acc_rows(ddb_ref, _colsum(ddtr))

    halo_prev = pl.BlockSpec((HALO, CONV_CH), lambda i: (jnp.maximum(i * hb - 1, 0), 0))
    halo_next = pl.BlockSpec((HALO, CONV_CH), lambda i: (jnp.minimum((i + 1) * hb, s // HALO - 1), 0))
    return pl.pallas_call(
        body, name="ssd_post_bwd", grid=(nt,),
        in_specs=[_rows(tm, CONV_CH), halo_prev, halo_next, _rows(tm, CONV_CH), halo_next, _rows(tm, LANE),
                  _rows(tm, LANE), _whole((CONV_K, CONV_CH)), _whole((1, CONV_CH)), _whole((1, LANE))],
        out_specs=[_rows(tm, CONV_CH), _rows(tm, LANE), _whole((CONV_K, CONV_CH)), _whole((1, CONV_CH)),
                   _whole((1, LANE))],
        out_shape=[jax.ShapeDtypeStruct((s, CONV_CH), F32), jax.ShapeDtypeStruct((s, LANE), F32),
                   jax.ShapeDtypeStruct((CONV_K, CONV_CH), F32), jax.ShapeDtypeStruct((1, CONV_CH), F32),
                   jax.ShapeDtypeStruct((1, LANE), F32)],
        scratch_shapes=[pltpu.VMEM((tm + 2 * HALO, CONV_CH), F32), pltpu.VMEM((ext, CONV_CH), F32)],
        compiler_params=_cparams("arbitrary"),
    )(xbc_raw, xbc_raw, xbc_raw, dxa, dxa, ddt, dt_raw, conv_w, conv_b, dt_bias_p)


def _qbwd(dq_att, q_lat, g_q, w_qb_p, cos, sin):
    s = q_lat.shape[0]
    tm = ROW_TILE
    wq = MLA_HEADS * HEAD_PAD

    def body(dq_ref, ql_ref, g_ref, w_ref, cos_ref, sin_ref, dql_ref, draw_ref, dg_ref):
        c, sn = cos_ref[...], sin_ref[...]
        for h in range(MLA_HEADS):
            o = h * HEAD_PAD
            dqh = dq_ref[h] * ATTN_SCALE
            draw_ref[:, o:o + QK_NOPE] = dqh[:, :QK_NOPE].astype(BF16)
            draw_ref[:, o + QK_NOPE:o + HEAD_PAD] = _rope_transposed(dqh[:, QK_NOPE:], c, sn).astype(BF16)
        dn = _nt(draw_ref[...], w_ref[...])
        xhat, rstd = _rms(ql_ref[...])
        _acc_rows(dg_ref, _colsum(dn * xhat))
        dql_ref[...] = _rms_bwd(dn * g_ref[...], xhat, rstd)

    return pl.pallas_call(
        body, name="qbwd", grid=(s // tm,),
        in_specs=[pl.BlockSpec((MLA_HEADS, tm, HEAD_PAD), lambda i: (0, i, 0)), _rows(tm, Q_RANK), _whole((1, Q_RANK)),
                  _whole((Q_RANK, wq)), _rows(tm, LANE), _rows(tm, LANE)],
        out_specs=[_rows(tm, Q_RANK), _rows(tm, wq), _whole((1, Q_RANK))],
        out_shape=[jax.ShapeDtypeStruct((s, Q_RANK), F32), jax.ShapeDtypeStruct((s, wq), BF16),
                   jax.ShapeDtypeStruct((1, Q_RANK), F32)],
        compiler_params=_cparams("arbitrary"),
    )(dq_att, q_lat, g_q, w_qb_p, cos, sin)


def _kvbwd(dk_att, dv, kv_lat, g_kv, w_kvb_p, cos, sin):
    s = kv_lat.shape[0]
    tm = ROW_TILE
    wk = MLA_HEADS * HEAD_PAD
    wr = MLA_HEADS * (QK_NOPE + V_DIM)

    def body(dk_ref, dv_ref, kl_ref, g_ref, w_ref, cos_ref, sin_ref, dkl_ref, draw_ref, dg_ref):
        dkr = None
        for h in range(MLA_HEADS):
            o = h * HEAD_PAD
            draw_ref[:, h * QK_NOPE:(h + 1) * QK_NOPE] = dk_ref[:, o:o + QK_NOPE].astype(BF16)
            part = dk_ref[:, o + QK_NOPE:o + HEAD_PAD]
            dkr = part if dkr is None else dkr + part
        draw_ref[:, MLA_HEADS * QK_NOPE:] = dv_ref[...].astype(BF16)
        dn = _nt(draw_ref[...], w_ref[...])
        xhat, rstd = _rms(kl_ref[:, :KV_RANK])
        _acc_rows(dg_ref, _colsum(dn * xhat))
        dkl_ref[:, :KV_RANK] = _rms_bwd(dn * g_ref[...], xhat, rstd)
        dkl_ref[:, KV_RANK:] = _rope_transposed(dkr, cos_ref[...], sin_ref[...])

    return pl.pallas_call(
        body, name="kvbwd", grid=(s // tm,),
        in_specs=[_rows(tm, wk), _rows(tm, MLA_WIDTH), _rows(tm, KV_LAT_PAD), _whole((1, KV_RANK)),
                  _whole((KV_RANK, wr)), _rows(tm, LANE), _rows(tm, LANE)],
        out_specs=[_rows(tm, KV_LAT_PAD), _rows(tm, wr), _whole((1, KV_RANK))],
        out_shape=[jax.ShapeDtypeStruct((s, KV_LAT_PAD), F32), jax.ShapeDtypeStruct((s, wr), BF16),
                   jax.ShapeDtypeStruct((1, KV_RANK), F32)],
        compiler_params=_cparams("arbitrary"),
    )(dk_att, dv, kv_lat, g_kv, w_kvb_p, cos, sin)


def _inproj_bwd(pieces, w_in_pt, x, scale1p, gx1):
    s = x.shape[0]
    tm = ROW_TILE

    def body(*refs):
        p_refs = refs[:len(IN_PAD)]
        w_ref, x_ref, sc_ref, gx1_ref, gx_ref, dp_ref, dsc_ref, dsh_ref = refs[len(IN_PAD):]
        off = 0
        for ref, w in zip(p_refs, IN_PAD):
            dp_ref[:, off:off + w] = ref[...].astype(BF16)
            off += w
        du = _nn(dp_ref[...], w_ref[...])
        gx_ref[...] = gx1_ref[...] + du * sc_ref[...]
        _acc_rows(dsc_ref, _colsum(du * x_ref[...]))
        _acc_rows(dsh_ref, _colsum(du))

    vec = _whole((1, D_MODEL))
    return pl.pallas_call(
        body, name="inproj_bwd", grid=(s // tm,),
        in_specs=[_rows(tm, w) for w in IN_PAD] + [_whole((IN_PAD_WIDTH, D_MODEL)), _rows(tm, D_MODEL), vec,
                                                    _rows(tm, D_MODEL)],
        out_specs=[_rows(tm, D_MODEL), _rows(tm, IN_PAD_WIDTH), vec, vec],
        out_shape=[jax.ShapeDtypeStruct((s, D_MODEL), F32), jax.ShapeDtypeStruct((s, IN_PAD_WIDTH), BF16),
                   jax.ShapeDtypeStruct((1, D_MODEL), F32), jax.ShapeDtypeStruct((1, D_MODEL), F32)],
        compiler_params=_cparams("arbitrary"),
    )(*pieces, w_in_pt, x, scale1p, gx1)


def _matmul_tn_rows(name, a, b, tk):
    s, k = a.shape
    n = b.shape[1]
    tm = min(ATTN_TILE, s)

    def body(a_ref, b_ref, o_ref):
        @pl.when(pl.program_id(1) == 0)
        def _():
            o_ref[...] = jnp.zeros_like(o_ref)
        o_ref[...] += _tn(a_ref[...], b_ref[...])

    return pl.pallas_call(
        body, name=name, grid=(k // tk, s // tm),
        in_specs=[pl.BlockSpec((tm, tk), lambda j, i: (i, j)), pl.BlockSpec((tm, n), lambda j, i: (i, 0))],
        out_specs=pl.BlockSpec((tk, n), lambda j, i: (j, 0)),
        out_shape=jax.ShapeDtypeStruct((k, n), F32),
        compiler_params=_cparams("parallel", "arbitrary"),
    )(a, b)


def _matmul_tn(name, a, b, tn):
    s, k = a.shape
    n = b.shape[1]
    tm = min(ATTN_TILE, s)

    def body(a_ref, b_ref, o_ref):
        @pl.when(pl.program_id(1) == 0)
        def _():
            o_ref[...] = jnp.zeros_like(o_ref)
        o_ref[...] += _tn(a_ref[...], b_ref[...])

    return pl.pallas_call(
        body, name=name, grid=(n // tn, s // tm),
        in_specs=[pl.BlockSpec((tm, k), lambda j, i: (i, 0)), pl.BlockSpec((tm, tn), lambda j, i: (i, j))],
        out_specs=pl.BlockSpec((k, tn), lambda j, i: (0, j)),
        out_shape=jax.ShapeDtypeStruct((k, n), F32),
        compiler_params=_cparams("parallel", "arbitrary"),
    )(a, b)


def _pack_w_in_t(w_in_t):
    parts, off = [], 0
    for w, wp in zip(IN_SPLITS, IN_PAD):
        parts.append(jnp.pad(w_in_t[off:off + w], ((0, wp - w), (0, 0))))
        off += w
    return jnp.concatenate(parts, axis=0)


def _unpack_w_in_t(g):
    parts, off = [], 0
    for w, wp in zip(IN_SPLITS, IN_PAD):
        parts.append(g[off:off + w])
        off += wp
    return jnp.concatenate(parts, axis=0)


def _pack_w_qb(w_qb):
    w = w_qb.reshape(Q_RANK, MLA_HEADS, QK_HEAD)
    return jnp.pad(w, ((0, 0), (0, 0), (0, HEAD_PAD - QK_HEAD))).reshape(Q_RANK, MLA_HEADS * HEAD_PAD)


def _unpack_w_qb(g):
    return g.reshape(Q_RANK, MLA_HEADS, HEAD_PAD)[:, :, :QK_HEAD].reshape(Q_RANK, MLA_HEADS * QK_HEAD)


def _pack_w_kvb(w_kvb):
    w = w_kvb.reshape(KV_RANK, MLA_HEADS, QK_NOPE + V_DIM)
    return jnp.concatenate([w[:, :, :QK_NOPE].reshape(KV_RANK, -1), w[:, :, QK_NOPE:].reshape(KV_RANK, -1)], axis=1)


def _unpack_w_kvb(g):
    gk = g[:, :MLA_HEADS * QK_NOPE].reshape(KV_RANK, MLA_HEADS, QK_NOPE)
    gv = g[:, MLA_HEADS * QK_NOPE:].reshape(KV_RANK, MLA_HEADS, V_DIM)
    return jnp.concatenate([gk, gv], axis=2).reshape(KV_RANK, -1)


def _rope_tables(positions):
    inv_freq = 1.0 / (ROPE_THETA ** (jnp.arange(ROPE_HALF, dtype=F32) / ROPE_HALF))
    ang = positions.astype(F32)[:, None] * inv_freq
    cos, sin = jnp.cos(ang), jnp.sin(ang)
    zeros = jnp.zeros((positions.shape[0], LANE - QK_ROPE), F32)
    return jnp.concatenate([cos, cos, zeros], axis=1), jnp.concatenate([-sin, sin, zeros], axis=1)


def _per_query_rows(a):
    s = a.shape[0]
    return a.reshape(s, MLA_HEADS, LANE)[:, :, 0].T.reshape(MLA_HEADS, 1, s)


def _local_step(x, tgt, positions, mod, w_in_t, q_norm_g, w_qb, kv_norm_g, w_kvb, conv_w, conv_b, dt_bias,
                a_log, d_skip, ssm_norm_g, w_out, ln_g, ln_b):
    row = lambda v: v.reshape(1, -1)
    shift, scale, gate = mod[:D_MODEL], mod[D_MODEL:2 * D_MODEL], mod[2 * D_MODEL:]
    scale1p = row(1.0 + scale)
    w_in_p = _pack_w_in_t(w_in_t.astype(BF16))
    w_qb_p = _pack_w_qb(w_qb).astype(BF16)
    w_kvb_p = _pack_w_kvb(w_kvb).astype(BF16)
    w_out_b = w_out.astype(BF16)
    cos, sin = _rope_tables(positions)
    a_neg = row(jnp.pad(-jnp.exp(a_log), (0, LANE - SSM_HEADS)))
    dskip_x = row(jnp.repeat(d_skip, SSM_P))
    dt_bias_p = row(jnp.pad(dt_bias, (0, LANE - SSM_HEADS)))
    tri = jnp.tril(jnp.ones((CHUNK, CHUNK), F32))
    tril, triu = tri.astype(BF16), tri.T.astype(BF16)

    u_bf, q_lat, kv_lat, z_attn, xbc_raw, dt_raw, z_ssm = _inproj(x, scale1p, row(shift), w_in_p)
    nq_bf, q_att = _qpath(q_lat, row(q_norm_g), w_qb_p, cos, sin)
    nkv_bf, k_att, v_att = _kvpath(kv_lat, row(kv_norm_g), w_kvb_p, cos, sin)
    o, lse = _attn_fwd(q_att, k_att, v_att)
    xbc, dt = _ssd_pre(xbc_raw, dt_raw, conv_w, row(conv_b), dt_bias_p)
    y, o_ssm, hin = _ssd_fwd(xbc, dt, z_ssm, a_neg, dskip_x, row(ssm_norm_g), tril, triu)
    (cat_bf, dmix_bf, gx1, do_bf, dz_attn, delta, dos, loss, d_ln_g, d_ln_b, d_gate) = _outln(
        o, z_attn, o_ssm, w_out_b, x, row(gate), row(ln_g), row(ln_b), tgt)

    g_w_out = _matmul_tn("gw_out", cat_bf, dmix_bf, 512)
    dk_att, dv, dq_att = _attn_bwd(q_att, k_att, v_att, do_bf, _per_query_rows(lse), _per_query_rows(delta))
    dq_lat, dqraw_bf, d_q_norm_g = _qbwd(dq_att, q_lat, row(q_norm_g), w_qb_p, cos, sin)
    dkv_lat, dkvraw_bf, d_kv_norm_g = _kvbwd(dk_att, dv, kv_lat, row(kv_norm_g), w_kvb_p, cos, sin)
    g_w_qb = _unpack_w_qb(_matmul_tn("gw_qb", nq_bf, dqraw_bf, MLA_HEADS * HEAD_PAD))
    g_w_kvb = _unpack_w_kvb(_matmul_tn("gw_kvb", nkv_bf, dkvraw_bf, MLA_HEADS * (QK_NOPE + V_DIM)))
    dxa, ddt, dz_ssm, d_ssm_g, ddsk_x, d_a = _ssd_bwd(dos, y, z_ssm, xbc, dt, hin, a_neg, dskip_x, row(ssm_norm_g),
                                                       tril, triu)
    dxbc_raw, ddt_raw, d_conv_w, d_conv_b, d_dt_bias = _ssd_post_bwd(xbc_raw, dxa, ddt, dt_raw, conv_w, row(conv_b),
                                                                     dt_bias_p)
    grad_x, dproj_bf, d_scale, d_shift = _inproj_bwd((dq_lat, dkv_lat, dz_attn, dxbc_raw, ddt_raw, dz_ssm),
                                                     w_in_p, x, scale1p, gx1)
    g_w_in_t = _unpack_w_in_t(_matmul_tn_rows("gw_in", dproj_bf, u_bf, 640))
    return dict(
        loss=loss[0, 0], grad_x=grad_x,
        dmod=jnp.concatenate([d_shift[0], d_scale[0], d_gate[0]]),
        w_in_t=g_w_in_t, q_norm_g=d_q_norm_g[0], w_qb=g_w_qb, kv_norm_g=d_kv_norm_g[0], w_kvb=g_w_kvb,
        conv_w=d_conv_w, conv_b=d_conv_b[0], dt_bias=d_dt_bias[0, :SSM_HEADS],
        a_log=d_a[0, :SSM_HEADS] * a_neg[0, :SSM_HEADS],
        d_skip=ddsk_x.reshape(SSM_HEADS, SSM_P).sum(axis=1), ssm_norm_g=d_ssm_g[0], w_out=g_w_out,
        ln_g=d_ln_g[0], ln_b=d_ln_b[0])


ADAM_ROWS = 512


def _my_index():
    return 4 * lax.axis_index("x") + 2 * lax.axis_index("y") + lax.axis_index("c")


def _exchange(name, sends, gather):
    n = len(sends)
    peers = N_DEV - 1

    def body(*refs):
        send_refs, recv_refs = refs[:n], refs[n:2 * n]
        send_sems, recv_sems, local_sems = refs[2 * n:]
        x, y, c = lax.axis_index("x"), lax.axis_index("y"), lax.axis_index("c")
        me = 4 * x + 2 * y + c

        def src(a, idx):
            return send_refs[a] if gather else send_refs[a].at[idx]

        owns = [pltpu.make_async_copy(src(a, me), recv_refs[a].at[me], local_sems.at[a]) for a in range(n)]
        for cp in owns:
            cp.start()
        copies = []
        for k in range(1, N_DEV):
            px, py, pc = x ^ ((k >> 2) & 1), y ^ ((k >> 1) & 1), c ^ (k & 1)
            peer = 4 * px + 2 * py + pc
            for a in range(n):
                copies.append(pltpu.make_async_remote_copy(
                    src_ref=src(a, peer), dst_ref=recv_refs[a].at[me],
                    send_sem=send_sems.at[a * peers + k - 1], recv_sem=recv_sems.at[a * peers + k - 1],
                    device_id=(px, py, pc), device_id_type=pl.DeviceIdType.MESH))
        for cp in copies:
            cp.start()
        for cp in copies:
            cp.wait()
        for cp in owns:
            cp.wait()

    block_shape = lambda a: a.shape if gather else a.shape[1:]
    return pl.pallas_call(
        body, name=name,
        in_specs=[pl.BlockSpec(memory_space=pl.ANY)] * n, out_specs=[pl.BlockSpec(memory_space=pl.ANY)] * n,
        out_shape=[jax.ShapeDtypeStruct((N_DEV, *block_shape(a)), a.dtype) for a in sends],
        scratch_shapes=[pltpu.SemaphoreType.DMA((n * peers,)), pltpu.SemaphoreType.DMA((n * peers,)),
                        pltpu.SemaphoreType.DMA((n,))],
    )(*sends)


def _flat_rows(parts, row_multiple):
    flat = jnp.concatenate([p.reshape(-1) for p in parts])
    chunk = row_multiple * LANE
    total = -(-flat.shape[0] // chunk) * chunk
    return jnp.pad(flat, (0, total - flat.shape[0])).reshape(-1, LANE)


def _unflat(flat, shapes):
    flat = flat.reshape(-1)
    out, off = [], 0
    for shp in shapes:
        n = math.prod(shp)
        out.append(flat[off:off + n].reshape(shp))
        off += n
    return out


def _adam_update(g, w, m, v):
    m2 = ADAM_B1 * m + (1.0 - ADAM_B1) * g
    v2 = ADAM_B2 * v + (1.0 - ADAM_B2) * (g * g)
    m_hat = m2 / (1.0 - ADAM_B1 ** ADAM_STEP)
    v_hat = v2 / (1.0 - ADAM_B2 ** ADAM_STEP)
    delta = -ADAM_LR * (m_hat / (jnp.sqrt(v_hat) + ADAM_EPS) + ADAM_WD * w)
    return delta, m2, v2


def _adamw_summed(name, parts, w, m, v):
    r = w.shape[0]
    tr = min(ADAM_ROWS, r)

    def body(p_ref, w_ref, m_ref, v_ref, g_ref, d_ref, m2_ref, v2_ref):
        g = p_ref[0]
        for j in range(1, N_DEV):
            g = g + p_ref[j]
        g_ref[...] = g
        d_ref[...], m2_ref[...], v2_ref[...] = _adam_update(g, w_ref[...], m_ref[...], v_ref[...])

    rows = _rows(tr, LANE)
    return pl.pallas_call(
        body, name=name, grid=(r // tr,),
        in_specs=[pl.BlockSpec((N_DEV, tr, LANE), lambda i: (0, i, 0)), rows, rows, rows],
        out_specs=[rows] * 4, out_shape=[jax.ShapeDtypeStruct((r, LANE), F32)] * 4,
        compiler_params=_cparams("parallel"),
    )(parts, w, m, v)


def _modpart(c_all, w_ada, b_cols):
    def body(c_ref, w_ref, b_ref, o_ref):
        o_ref[...] = _nn(c_ref[...].astype(BF16), w_ref[...].astype(BF16)) + b_ref[...]

    return pl.pallas_call(
        body, name="modpart", out_shape=jax.ShapeDtypeStruct((N_DEV, w_ada.shape[1]), F32),
    )(c_all, w_ada, b_cols)


def _adamw_w_ada(c_all_t, dmod_cols, w, m, v):
    def body(c_ref, d_ref, w_ref, m_ref, v_ref, g_ref, dl_ref, m2_ref, v2_ref):
        g = c_ref[:, 0:1] * d_ref[0:1, :]
        for b in range(1, N_DEV):
            g = g + c_ref[:, b:b + 1] * d_ref[b:b + 1, :]
        g_ref[...] = g
        dl_ref[...], m2_ref[...], v2_ref[...] = _adam_update(g, w_ref[...], m_ref[...], v_ref[...])

    return pl.pallas_call(
        body, name="adamw_w_ada", out_shape=[jax.ShapeDtypeStruct(w.shape, F32)] * 4,
        compiler_params=pltpu.CompilerParams(vmem_limit_bytes=VMEM_LIMIT),
    )(c_all_t, dmod_cols, w, m, v)


W_IN_SHARD = IN_WIDTH // N_DEV
W_IN_SHARD_LANES = -(-W_IN_SHARD // LANE) * LANE
BF16_ROWS = 16
W_IN_SEND_ROWS = -(-W_IN_SHARD // BF16_ROWS) * BF16_ROWS


def _transpose_cast(w_pad):
    def body(w_ref, o_ref):
        o_ref[...] = w_ref[...].T.astype(BF16)

    return pl.pallas_call(
        body, name="w_in_transpose", out_shape=jax.ShapeDtypeStruct(w_pad.shape[::-1], BF16),
        compiler_params=pltpu.CompilerParams(vmem_limit_bytes=VMEM_LIMIT),
    )(w_pad)


def _adamw_w_in(parts, w, m, v):
    rows_t = parts.shape[1]
    d, cols = w.shape
    tb = ROW_TILE

    def body(p_ref, w_ref, m_ref, v_ref, g_ref, d_ref, m2_ref, v2_ref):
        gt = p_ref[0].astype(F32)
        for j in range(1, N_DEV):
            gt = gt + p_ref[j].astype(F32)
        gt = jnp.concatenate([gt, jnp.zeros((W_IN_SHARD_LANES - rows_t, tb), F32)], axis=0)
        g = gt.T[:, :cols]
        g_ref[...] = g
        d_ref[...], m2_ref[...], v2_ref[...] = _adam_update(g, w_ref[...], m_ref[...], v_ref[...])

    blk = _rows(tb, cols)
    return pl.pallas_call(
        body, name="adamw_w_in", grid=(d // tb,),
        in_specs=[pl.BlockSpec((N_DEV, rows_t, tb), lambda i: (0, 0, i)), blk, blk, blk],
        out_specs=[blk] * 4, out_shape=[jax.ShapeDtypeStruct(w.shape, F32)] * 4,
        compiler_params=_cparams("parallel"),
    )(parts, w, m, v)


SHARDED = ("w_qb", "w_kvb", "w_out", "conv_w")
REPLICATED =("b_ada", "q_norm_g", "kv_norm_g", "conv_b", "dt_bias", "a_log", "d_skip", "ssm_norm_g", "ln_g", "ln_b")
WEIGHTS = ("w_ada", "b_ada", "w_in", "q_norm_g", "w_qb", "kv_norm_g", "w_kvb", "conv_w", "conv_b", "dt_bias",
           "a_log", "d_skip", "ssm_norm_g", "w_out", "ln_g", "ln_b")


def _column_blocks(g, name):
    if name == "w_out":
        return g.reshape(N_DEV, g.shape[0] // N_DEV, g.shape[1])
    rows, cols = g.shape
    return g.reshape(rows, N_DEV, cols // N_DEV).transpose(1, 0, 2)


def _from_blocks(blocks, name):
    if name == "w_out":
        return blocks.reshape(-1, blocks.shape[-1])
    n, rows, cols = blocks.shape
    return blocks.transpose(1, 0, 2).reshape(rows, n * cols)


def kernel(x, c, positions, w_ada, b_ada, w_in, q_norm_g, w_qb, kv_norm_g, w_kvb, conv_w, conv_b, dt_bias, a_log, d_skip, ssm_norm_g, w_out, ln_g, ln_b, loss_target, m_w_ada, m_b_ada, m_w_in, m_q_norm_g, m_w_qb, m_kv_norm_g, m_w_kvb, m_conv_w, m_conv_b, m_dt_bias, m_a_log, m_d_skip, m_ssm_norm_g, m_w_out, m_ln_g, m_ln_b, v_w_ada, v_b_ada, v_w_in, v_q_norm_g, v_w_qb, v_kv_norm_g, v_w_kvb, v_conv_w, v_conv_b, v_dt_bias, v_a_log, v_d_skip, v_ssm_norm_g, v_w_out, v_ln_g, v_ln_b):
    given = dict(w_ada=w_ada, b_ada=b_ada, w_in=w_in, q_norm_g=q_norm_g, w_qb=w_qb, kv_norm_g=kv_norm_g, w_kvb=w_kvb,
                 conv_w=conv_w, conv_b=conv_b, dt_bias=dt_bias, a_log=a_log, d_skip=d_skip, ssm_norm_g=ssm_norm_g,
                 w_out=w_out, ln_g=ln_g, ln_b=ln_b)
    mom = dict(w_ada=m_w_ada, b_ada=m_b_ada, w_in=m_w_in, q_norm_g=m_q_norm_g, w_qb=m_w_qb, kv_norm_g=m_kv_norm_g,
               w_kvb=m_w_kvb, conv_w=m_conv_w, conv_b=m_conv_b, dt_bias=m_dt_bias, a_log=m_a_log, d_skip=m_d_skip,
               ssm_norm_g=m_ssm_norm_g, w_out=m_w_out, ln_g=m_ln_g, ln_b=m_ln_b)
    var = dict(w_ada=v_w_ada, b_ada=v_b_ada, w_in=v_w_in, q_norm_g=v_q_norm_g, w_qb=v_w_qb, kv_norm_g=v_kv_norm_g,
               w_kvb=v_w_kvb, conv_w=v_conv_w, conv_b=v_conv_b, dt_bias=v_dt_bias, a_log=v_a_log, d_skip=v_d_skip,
               ssm_norm_g=v_ssm_norm_g, w_out=v_w_out, ln_g=v_ln_g, ln_b=v_ln_b)
    w0 = {k: a[0] for k, a in given.items()}
    m0 = {k: a[0] for k, a in mom.items()}
    v0 = {k: a[0] for k, a in var.items()}
    me = _my_index()

    w_in_rows = _transpose_cast(jnp.pad(w0["w_in"], ((0, 0), (0, W_IN_SHARD_LANES - W_IN_SHARD))))
    shard_shapes = [w0[k].shape for k in SHARDED] + [(D_MODEL,)]
    g_w_in, g_flat = _exchange("gather_weights",
                               [w_in_rows, _flat_rows([w0[k] for k in SHARDED] + [c[0]], HALO)], gather=True)
    w_in_t = g_w_in[:, :W_IN_SHARD, :].reshape(IN_WIDTH, D_MODEL)
    g_flat = g_flat.reshape(N_DEV, -1)
    full, off = {}, 0
    for k, shp in zip(SHARDED + ("c",), shard_shapes):
        n = math.prod(shp)
        full[k] = g_flat[:, off:off + n].reshape(N_DEV, *shp)
        off += n
    c_all = full.pop("c")
    full = {k: _from_blocks(a, k) for k, a in full.items()}

    ada_cols = w0["w_ada"].shape[1]
    b_cols = lax.dynamic_slice(w0["b_ada"], (me * ada_cols,), (ada_cols,)).reshape(1, ada_cols)
    mod_all, = _exchange("gather_mod", [_flat_rows([_modpart(c_all, w0["w_ada"], b_cols)], HALO)], gather=True)
    mod_all = mod_all.reshape(N_DEV, -1)[:, :N_DEV * ada_cols].reshape(N_DEV, N_DEV, ada_cols)
    mod = lax.dynamic_index_in_dim(mod_all, me, axis=1, keepdims=False).reshape(-1)

    loc = _local_step(x[0], loss_target[0], positions[0], mod, w_in_t, w0["q_norm_g"], full["w_qb"],
                      w0["kv_norm_g"], full["w_kvb"], full["conv_w"], w0["conv_b"], w0["dt_bias"], w0["a_log"],
                      w0["d_skip"], w0["ssm_norm_g"], full["w_out"], w0["ln_g"], w0["ln_b"])

    rep_shapes = [w0[k].shape for k in REPLICATED] + [(1,)]
    rep_local = [loc["dmod"]] + [loc[k] for k in REPLICATED[1:]] + [loc["loss"].reshape(1)]
    rep_parts, = _exchange("gather_small", [_flat_rows(rep_local, HALO)], gather=True)
    zero1 = jnp.zeros((1,), F32)
    rep = _adamw_summed("adamw_replicated", rep_parts,
                        _flat_rows([w0[k] for k in REPLICATED] + [zero1], HALO),
                        _flat_rows([m0[k] for k in REPLICATED] + [zero1], HALO),
                        _flat_rows([v0[k] for k in REPLICATED] + [zero1], HALO))
    rep_g, rep_d, rep_m, rep_v = [_unflat(a, rep_shapes) for a in rep]
    loss = rep_g[-1][0]

    dmod_all = rep_parts.reshape(N_DEV, -1)[:, :3 * D_MODEL]
    dmod_cols = lax.dynamic_slice(dmod_all, (0, me * ada_cols), (N_DEV, ada_cols))
    ada = _adamw_w_ada(c_all.T, dmod_cols, w0["w_ada"], m0["w_ada"], v0["w_ada"])

    send_w_in = loc["w_in_t"].astype(BF16).reshape(N_DEV, W_IN_SHARD, D_MODEL)
    send_w_in = jnp.pad(send_w_in, ((0, 0), (0, W_IN_SEND_ROWS - W_IN_SHARD), (0, 0)))
    blocks = [_column_blocks(loc[k], k).reshape(N_DEV, -1) for k in SHARDED]
    send = jnp.concatenate(blocks, axis=1)
    chunk = ADAM_ROWS * LANE
    total = -(-send.shape[1] // chunk) * chunk
    send = jnp.pad(send, ((0, 0), (0, total - send.shape[1]))).reshape(N_DEV, -1, LANE)
    recv_w_in, recv = _exchange("scatter_grads", [send_w_in, send], gather=False)
    w_in_out = _adamw_w_in(recv_w_in, w0["w_in"], m0["w_in"], v0["w_in"])
    shd_shapes = [w0[k].shape for k in SHARDED]
    shd = _adamw_summed("adamw_sharded", recv,
                        _flat_rows([w0[k] for k in SHARDED], ADAM_ROWS),
                        _flat_rows([m0[k] for k in SHARDED], ADAM_ROWS),
                        _flat_rows([v0[k] for k in SHARDED], ADAM_ROWS))
    shd_g, shd_d, shd_m, shd_v = [_unflat(a, shd_shapes) for a in shd]

    def collect(idx):
        out = {"w_ada": ada[idx], "w_in": w_in_out[idx]}
        out.update({k: (rep_g, rep_d, rep_m, rep_v)[idx][i] for i, k in enumerate(REPLICATED)})
        out.update({k: (shd_g, shd_d, shd_m, shd_v)[idx][i] for i, k in enumerate(SHARDED)})
        return [out[k][None] for k in WEIGHTS]

    return (loss, loc["grad_x"][None], *collect(0), *collect(1), *collect(2), *collect(3))
```

```python
import functools
import math

import jax
import jax.numpy as jnp
from jax import lax
from jax.experimental import pallas as pl
from jax.experimental.pallas import tpu as pltpu

F32 = jnp.float32
BF16 = jnp.bfloat16

N_DEV = 8
D_MODEL = 1024
MLA_HEADS = 8
QK_NOPE = 128
QK_ROPE = 64
V_DIM = 128
Q_RANK = 384
KV_RANK = 256
QK_HEAD = QK_NOPE + QK_ROPE
HEAD_PAD = 256
ROPE_HALF = QK_ROPE // 2
ROPE_THETA = 10000.0
MLA_WIDTH = MLA_HEADS * V_DIM
SSM_HEADS = 16
SSM_P = 64
SSM_WIDTH = SSM_HEADS * SSM_P
SSM_GROUPS = 2
SSM_N = 128
CONV_K = 4
CHUNK = 128
CONV_CH = SSM_WIDTH + 2 * SSM_GROUPS * SSM_N
MIX_WIDTH = MLA_WIDTH + SSM_WIDTH
IN_SPLITS = (Q_RANK, KV_RANK + QK_ROPE, MLA_WIDTH, CONV_CH, SSM_HEADS, SSM_WIDTH)
IN_WIDTH = sum(IN_SPLITS)
LANE = 128
KV_LAT_PAD = KV_RANK + LANE
IN_PAD = (Q_RANK, KV_LAT_PAD, MLA_WIDTH, CONV_CH, LANE, SSM_WIDTH)
IN_PAD_WIDTH = sum(IN_PAD)
DEEPNORM_ALPHA = 2.0 ** 0.25
RMS_EPS = 1e-6
LN_EPS = 1e-5
ATTN_SCALE = QK_HEAD ** -0.5
LOG2E = math.log2(math.e)
LN2 = math.log(2.0)
Q_PRESCALE = ATTN_SCALE * LOG2E
ADAM_LR, ADAM_B1, ADAM_B2, ADAM_EPS, ADAM_WD, ADAM_STEP = 0.001, 0.9, 0.999, 1e-08, 0.01, 10

ROW_TILE = 256
ATTN_TILE = 512
ATTN_UNROLL = 4
SSD_ROWS = 512
VMEM_LIMIT = 56 * 1024 * 1024


def _nn(a, b):
    return jnp.dot(a, b, preferred_element_type=F32)


def _nt(a, b):
    return lax.dot_general(a, b, (((1,), (1,)), ((), ())), preferred_element_type=F32)


def _tn(a, b):
    return lax.dot_general(a, b, (((0,), (0,)), ((), ())), preferred_element_type=F32)


def _cparams(*sem):
    return pltpu.CompilerParams(dimension_semantics=sem, vmem_limit_bytes=VMEM_LIMIT)


def _rows(tm, w):
    return pl.BlockSpec((tm, w), lambda i: (i, 0))


def _whole(shape):
    return pl.BlockSpec(shape, lambda i: (0,) * len(shape))


def _sigmoid(z):
    return 1.0 / (1.0 + jnp.exp(-z))


def _lane_iota(shape):
    return lax.broadcasted_iota(jnp.int32, shape, len(shape) - 1)


def _swap_halves(r):
    lane = _lane_iota(r.shape)
    return jnp.where(lane < ROPE_HALF, pltpu.roll(r, LANE - ROPE_HALF, 1),
                     jnp.where(lane < QK_ROPE, pltpu.roll(r, ROPE_HALF, 1), 0.0))


def _rope(r, cos, sin):
    return r * cos + _swap_halves(r) * sin


def _rope_transposed(d, cos, sin):
    return d * cos + _swap_halves(d * sin)


def _rms(x):
    rstd = lax.rsqrt(jnp.mean(x * x, axis=-1, keepdims=True) + RMS_EPS)
    return x * rstd, rstd


def _rms_bwd(dxhat, xhat, rstd):
    return rstd * (dxhat - xhat * jnp.mean(dxhat * xhat, axis=-1, keepdims=True))


def _acc_rows(ref, val):
    @pl.when(pl.program_id(0) == 0)
    def _():
        ref[...] = jnp.zeros_like(ref)
    ref[...] += val


def _colsum(v):
    return jnp.sum(v, axis=0, keepdims=True)


def _inproj(x, scale1p, shift, w_in_pt):
    s = x.shape[0]
    tm = ROW_TILE

    def body(x_ref, sc_ref, sh_ref, w_ref, u_ref, *outs):
        u = (x_ref[...] * sc_ref[...] + sh_ref[...]).astype(BF16)
        u_ref[...] = u
        proj = _nt(u, w_ref[...])
        off = 0
        for ref, w in zip(outs, IN_PAD):
            ref[...] = proj[:, off:off + w]
            off += w

    return pl.pallas_call(
        body, name="inproj", grid=(s // tm,),
        in_specs=[_rows(tm, D_MODEL), _whole((1, D_MODEL)), _whole((1, D_MODEL)), _whole((IN_PAD_WIDTH, D_MODEL))],
        out_specs=[_rows(tm, D_MODEL)] + [_rows(tm, w) for w in IN_PAD],
        out_shape=[jax.ShapeDtypeStruct((s, D_MODEL), BF16)] + [jax.ShapeDtypeStruct((s, w), F32) for w in IN_PAD],
        compiler_params=_cparams("parallel"),
    )(x, scale1p, shift, w_in_pt)


def _qpath(q_lat, g_q, w_qb_p, cos, sin):
    s = q_lat.shape[0]
    tm = ROW_TILE

    def body(ql_ref, g_ref, w_ref, cos_ref, sin_ref, nq_ref, q_ref):
        xhat, _ = _rms(ql_ref[...])
        nq = (xhat * g_ref[...]).astype(BF16)
        nq_ref[...] = nq
        raw = _nn(nq, w_ref[...]) * Q_PRESCALE
        c, sn = cos_ref[...], sin_ref[...]
        for h in range(MLA_HEADS):
            o = h * HEAD_PAD
            q_ref[:, o:o + QK_NOPE] = raw[:, o:o + QK_NOPE].astype(BF16)
            q_ref[:, o + QK_NOPE:o + HEAD_PAD] = _rope(raw[:, o + QK_NOPE:o + HEAD_PAD], c, sn).astype(BF16)

    return pl.pallas_call(
        body, name="qpath", grid=(s // tm,),
        in_specs=[_rows(tm, Q_RANK), _whole((1, Q_RANK)), _whole((Q_RANK, MLA_HEADS * HEAD_PAD)),
                  _rows(tm, LANE), _rows(tm, LANE)],
        out_specs=[_rows(tm, Q_RANK), _rows(tm, MLA_HEADS * HEAD_PAD)],
        out_shape=[jax.ShapeDtypeStruct((s, Q_RANK), BF16), jax.ShapeDtypeStruct((s, MLA_HEADS * HEAD_PAD), BF16)],
        compiler_params=_cparams("parallel"),
    )(q_lat, g_q, w_qb_p, cos, sin)


def _kvpath(kv_lat, g_kv, w_kvb_p, cos, sin):
    s = kv_lat.shape[0]
    tm = ROW_TILE

    def body(kl_ref, g_ref, w_ref, cos_ref, sin_ref, nkv_ref, k_ref, v_ref):
        kl = kl_ref[...]
        xhat, _ = _rms(kl[:, :KV_RANK])
        nkv = (xhat * g_ref[...]).astype(BF16)
        nkv_ref[...] = nkv
        raw = _nn(nkv, w_ref[...])
        kr = _rope(kl[:, KV_RANK:], cos_ref[...], sin_ref[...]).astype(BF16)
        for h in range(MLA_HEADS):
            o = h * HEAD_PAD
            k_ref[:, o:o + QK_NOPE] = raw[:, h * QK_NOPE:(h + 1) * QK_NOPE].astype(BF16)
            k_ref[:, o + QK_NOPE:o + HEAD_PAD] = kr
        v_ref[...] = raw[:, MLA_HEADS * QK_NOPE:].astype(BF16)

    return pl.pallas_call(
        body, name="kvpath", grid=(s // tm,),
        in_specs=[_rows(tm, KV_LAT_PAD), _whole((1, KV_RANK)), _whole((KV_RANK, MLA_HEADS * (QK_NOPE + V_DIM))),
                  _rows(tm, LANE), _rows(tm, LANE)],
        out_specs=[_rows(tm, KV_RANK), _rows(tm, MLA_HEADS * HEAD_PAD), _rows(tm, MLA_WIDTH)],
        out_shape=[jax.ShapeDtypeStruct((s, KV_RANK), BF16), jax.ShapeDtypeStruct((s, MLA_HEADS * HEAD_PAD), BF16),
                   jax.ShapeDtypeStruct((s, MLA_WIDTH), BF16)],
        compiler_params=_cparams("parallel"),
    )(kv_lat, g_kv, w_kvb_p, cos, sin)


def _causal_mask(t):
    row = lax.broadcasted_iota(jnp.int32, (t, t), 0)
    col = lax.broadcasted_iota(jnp.int32, (t, t), 1)
    return row, col


def _attn_fwd(q, k, v):
    s = q.shape[0]
    t = min(ATTN_TILE, s)
    nq = s // t

    def body(q_ref, k_ref, v_ref, o_ref, lse_ref, m_sc, l_sc, acc_sc, sa_sc, sb_sc):
        i = pl.program_id(1)
        qv = q_ref[...]
        m_sc[...] = jnp.full(m_sc.shape, -jnp.inf, F32)
        l_sc[...] = jnp.zeros(l_sc.shape, F32)
        acc_sc[...] = jnp.zeros(acc_sc.shape, F32)

        def scores(j, s_ref):
            s_ref[...] = _nt(qv, k_ref[pl.ds(pl.multiple_of(j * t, t), t), :])

        def update(s_ref, j, masked):
            vv = v_ref[pl.ds(pl.multiple_of(j * t, t), t), :]
            sc = s_ref[...]
            if masked:
                row, col = _causal_mask(t)
                sc = jnp.where(col <= row, sc, -jnp.inf)
            m_prev = m_sc[...]
            m_new = jnp.maximum(m_prev, jnp.max(sc, axis=1, keepdims=True))
            alpha = jnp.exp2(m_prev - m_new)
            p = jnp.exp2(sc - jnp.tile(m_new, (1, t // LANE)))
            l_sc[...] = alpha * l_sc[...] + jnp.sum(p, axis=1, keepdims=True)
            acc_sc[...] = alpha * acc_sc[...] + _nn(p.astype(BF16), vv)
            m_sc[...] = m_new

        def run(j0, count):
            bufs = (sa_sc, sb_sc)
            for u in range(count):
                scores(j0 + u + 1, bufs[(u + 1) % 2])
                update(bufs[u % 2], j0 + u, False)

        def quad(qq, carry):
            run(ATTN_UNROLL * qq, ATTN_UNROLL)
            return carry

        def pair(pp, carry):
            run(base2 + 2 * pp, 2)
            return carry

        scores(0, sa_sc)
        nquad = lax.div(i, ATTN_UNROLL)
        base2 = ATTN_UNROLL * nquad
        lax.fori_loop(0, nquad, quad, 0)
        lax.fori_loop(0, lax.div(i - base2, 2), pair, 0)
        odd = lax.rem(i, 2)

        @pl.when(odd == 1)
        def _():
            scores(i, sb_sc)
            update(sa_sc, i - 1, False)
            update(sb_sc, i, True)

        @pl.when(odd == 0)
        def _():
            update(sa_sc, i, True)

        l = l_sc[...]
        o_ref[...] = acc_sc[...] / l
        lse_ref[0] = (m_sc[...] + jnp.log2(l)).T[0:1, :]

    return pl.pallas_call(
        body, name="attn_fwd", grid=(MLA_HEADS, nq),
        in_specs=[pl.BlockSpec((t, HEAD_PAD), lambda h, i: (i, h)),
                  pl.BlockSpec((s, HEAD_PAD), lambda h, i: (0, h)),
                  pl.BlockSpec((s, V_DIM), lambda h, i: (0, h))],
        out_specs=[pl.BlockSpec((t, V_DIM), lambda h, i: (i, h)), pl.BlockSpec((1, 1, t), lambda h, i: (h, 0, i))],
        out_shape=[jax.ShapeDtypeStruct((s, MLA_WIDTH), F32), jax.ShapeDtypeStruct((MLA_HEADS, 1, s), F32)],
        scratch_shapes=[pltpu.VMEM((t, LANE), F32), pltpu.VMEM((t, LANE), F32), pltpu.VMEM((t, V_DIM), F32),
                        pltpu.VMEM((t, t), F32), pltpu.VMEM((t, t), F32)],
        compiler_params=_cparams("parallel", "arbitrary"),
    )(q, k, v)


def _attn_bwd(q, k, v, do, lse_row, delta_row):
    s = q.shape[0]
    t = min(ATTN_TILE, s)
    nq = s // t

    def body(q_ref, k_ref, v_ref, do_ref, lse_ref, dl_ref, dk_ref, dv_ref, dq_hbm,
             dq_sc, dk_sc, dv_sc, sa_sc, sb_sc, pa_sc, pb_sc, sem):
        h = pl.program_id(0)
        j = pl.program_id(1)
        kv_ = k_ref[...]
        vv = v_ref[...]

        @pl.when(j == 0)
        def _():
            dq_sc[...] = jnp.zeros(dq_sc.shape, F32)

        dk_sc[...] = jnp.zeros(dk_sc.shape, F32)
        dv_sc[...] = jnp.zeros(dv_sc.shape, F32)

        def scores(i, s_ref, p_ref):
            off = pl.multiple_of(i * t, t)
            s_ref[...] = _nt(kv_, q_ref[pl.ds(off, t), :])
            p_ref[...] = _nt(vv, do_ref[pl.ds(off, t), :])

        def update(i, s_ref, p_ref, masked):
            off = pl.multiple_of(i * t, t)
            qv = q_ref[pl.ds(off, t), :]
            dov = do_ref[pl.ds(off, t), :]
            sct = s_ref[...]
            if masked:
                row, col = _causal_mask(t)
                sct = jnp.where(row <= col, sct, -jnp.inf)
            pt = jnp.exp2(sct - lse_ref[0, :, pl.ds(off, t)])
            gt = (pt * (p_ref[...] - dl_ref[0, :, pl.ds(off, t)])).astype(BF16)
            dv_sc[...] += _nn(pt.astype(BF16), dov)
            dk_sc[...] += _nn(gt, qv)
            dq_sc[pl.ds(off, t), :] += _tn(gt, kv_)

        rest = nq - 1 - j
        scores(j, sa_sc, pa_sc)

        @pl.when(rest >= 1)
        def _():
            scores(j + 1, sb_sc, pb_sc)

        update(j, sa_sc, pa_sc, True)

        def run(i0, count):
            bufs = ((sb_sc, pb_sc), (sa_sc, pa_sc))
            for u in range(count):
                scores(i0 + u + 1, *bufs[(u + 1) % 2])
                update(i0 + u, *bufs[u % 2], False)

        def quad(qq, carry):
            run(j + 1 + ATTN_UNROLL * qq, ATTN_UNROLL)
            return carry

        def pair(pp, carry):
            run(base2 + 2 * pp, 2)
            return carry

        nquad = jnp.where(rest >= 1, lax.div(rest - 1, ATTN_UNROLL), 0)
        base2 = j + 1 + ATTN_UNROLL * nquad
        rest2 = rest - ATTN_UNROLL * nquad
        npairs = jnp.where(rest2 >= 1, lax.div(rest2 - 1, 2), 0)
        lax.fori_loop(0, nquad, quad, 0)
        lax.fori_loop(0, npairs, pair, 0)
        left = rest2 - 2 * npairs
        i1 = base2 + 2 * npairs

        @pl.when(left == 1)
        def _():
            update(i1, sb_sc, pb_sc, False)

        @pl.when(left == 2)
        def _():
            scores(i1 + 1, sa_sc, pa_sc)
            update(i1, sb_sc, pb_sc, False)
            update(i1 + 1, sa_sc, pa_sc, False)

        dk_ref[...] = dk_sc[...] * LN2
        dv_ref[...] = dv_sc[...]

        @pl.when(j == nq - 1)
        def _():
            cp = pltpu.make_async_copy(dq_sc, dq_hbm.at[h], sem)
            cp.start()
            cp.wait()

    return pl.pallas_call(
        body, name="attn_bwd", grid=(MLA_HEADS, nq),
        in_specs=[pl.BlockSpec((s, HEAD_PAD), lambda h, j: (0, h)),
                  pl.BlockSpec((t, HEAD_PAD), lambda h, j: (j, h)),
                  pl.BlockSpec((t, V_DIM), lambda h, j: (j, h)),
                  pl.BlockSpec((s, V_DIM), lambda h, j: (0, h)),
                  pl.BlockSpec((1, 1, s), lambda h, j: (h, 0, 0)),
                  pl.BlockSpec((1, 1, s), lambda h, j: (h, 0, 0))],
        out_specs=[pl.BlockSpec((t, HEAD_PAD), lambda h, j: (j, h)), pl.BlockSpec((t, V_DIM), lambda h, j: (j, h)),
                   pl.BlockSpec(memory_space=pl.ANY)],
        out_shape=[jax.ShapeDtypeStruct((s, MLA_HEADS * HEAD_PAD), F32), jax.ShapeDtypeStruct((s, MLA_WIDTH), F32),
                   jax.ShapeDtypeStruct((MLA_HEADS, s, HEAD_PAD), F32)],
        scratch_shapes=[pltpu.VMEM((s, HEAD_PAD), F32), pltpu.VMEM((t, HEAD_PAD), F32), pltpu.VMEM((t, V_DIM), F32),
                        pltpu.VMEM((t, t), F32), pltpu.VMEM((t, t), F32), pltpu.VMEM((t, t), F32),
                        pltpu.VMEM((t, t), F32), pltpu.SemaphoreType.DMA],
        compiler_params=_cparams("arbitrary", "arbitrary"),
    )(q, k, v, do, lse_row, delta_row)


HALO = 8


def _silu(z):
    return z * _sigmoid(z)


def _silu_grad(z):
    sg = _sigmoid(z)
    return sg * (1.0 + z * (1.0 - sg))


def _softplus(x):
    e = jnp.exp(-jnp.abs(x))
    small = e * (1.0 - e * (0.5 - e * (1.0 / 3.0)))
    return jnp.maximum(x, 0.0) + jnp.where(e < 1e-3, small, jnp.log(1.0 + e))


def _conv_taps(xe_ref, w, tm, first):
    acc = None
    for k in range(CONV_K):
        term = xe_ref[pl.ds(HALO + first - (CONV_K - 1) + k, tm), :] * w[k:k + 1, :]
        acc = term if acc is None else acc + term
    return acc


def _ssd_pre(xbc_raw, dt_raw, conv_w, conv_b, dt_bias_p):
    s = xbc_raw.shape[0]
    tm = ROW_TILE
    hb = tm // HALO

    def body(x_ref, prev_ref, dtr_ref, w_ref, b_ref, db_ref, act_ref, dt_ref, xe_sc):
        i = pl.program_id(0)
        xe_sc[pl.ds(0, HALO), :] = jnp.where(i > 0, prev_ref[...], 0.0)
        xe_sc[pl.ds(HALO, tm), :] = x_ref[...]
        pre = _conv_taps(xe_sc, w_ref[...], tm, 0) + b_ref[...]
        act_ref[...] = _silu(pre)
        dt_ref[...] = _softplus(dtr_ref[...] + db_ref[...])

    return pl.pallas_call(
        body, name="ssd_pre", grid=(s // tm,),
        in_specs=[_rows(tm, CONV_CH), pl.BlockSpec((HALO, CONV_CH), lambda i: (jnp.maximum(i * hb - 1, 0), 0)),
                  _rows(tm, LANE), _whole((CONV_K, CONV_CH)), _whole((1, CONV_CH)), _whole((1, LANE))],
        out_specs=[_rows(tm, CONV_CH), _rows(tm, LANE)],
        out_shape=[jax.ShapeDtypeStruct((s, CONV_CH), F32), jax.ShapeDtypeStruct((s, LANE), F32)],
        scratch_shapes=[pltpu.VMEM((tm + HALO, CONV_CH), F32)],
        compiler_params=_cparams("parallel"),
    )(xbc_raw, xbc_raw, dt_raw, conv_w, conv_b, dt_bias_p)


def _split3(a):
    a1 = a.astype(BF16)
    r1 = a - a1.astype(F32)
    a2 = r1.astype(BF16)
    a3 = (r1 - a2.astype(F32)).astype(BF16)
    return a1, a2, a3


def _tri_left(tri, a):
    a1, a2, a3 = _split3(a)
    return _nn(tri, a1) + _nn(tri, a2) + _nn(tri, a3)


def _tri_right(a, tri):
    a1, a2, a3 = _split3(a)
    return _nn(a1, tri) + _nn(a2, tri) + _nn(a3, tri)


def _pair_sel(lane_lo, col_a, col_b):
    return jnp.where(lane_lo, col_a, col_b)


def _chunk_common(dt, a_neg, tril, triu):
    a = dt * a_neg
    lam_c = _tri_left(tril, a)
    lam_r = _tri_right(a.T, triu)
    lam_last = lam_c[CHUNK - 1:CHUNK, :]
    return lam_c, lam_r, lam_last


def _gated_norm_fwd(y, z, g):
    hf = y * _silu(z)
    outs = []
    for grp in range(SSM_GROUPS):
        w = SSM_WIDTH // SSM_GROUPS
        n, _ = _rms(hf[:, grp * w:(grp + 1) * w])
        outs.append(n)
    return jnp.concatenate(outs, axis=1) * g


def _ssd_fwd(xbc, dt, z, a_neg, dskip_x, g_x, tril, triu):
    s = xbc.shape[0]
    tm = min(SSD_ROWS, s)
    cpb = tm // CHUNK
    nc = s // CHUNK

    def body(xbc_ref, dt_ref, z_ref, a_ref, dsk_ref, g_ref, tril_ref, triu_ref, y_ref, o_ref, hin_ref, h_sc):
        @pl.when(pl.program_id(0) == 0)
        def _():
            h_sc[...] = jnp.zeros(h_sc.shape, F32)

        tril, triu = tril_ref[...], triu_ref[...]
        ltri = tril > 0
        lane_lo = _lane_iota((CHUNK, LANE)) < SSM_P
        lane_lo_n = lane_lo

        def chunk(c, carry):
            r0 = pl.multiple_of(c * CHUNK, CHUNK)
            dtc = dt_ref[pl.ds(r0, CHUNK), :]
            lam_c, lam_r, lam_last = _chunk_common(dtc, a_ref[...], tril, triu)
            e_c = jnp.exp(lam_c)
            f_r = jnp.exp(lam_r[:, CHUNK - 1:CHUNK] - lam_r)
            cd = jnp.exp(lam_last)
            for grp in range(SSM_GROUPS):
                bo = SSM_WIDTH + grp * SSM_N
                co = SSM_WIDTH + SSM_GROUPS * SSM_N + grp * SSM_N
                bm = xbc_ref[pl.ds(r0, CHUNK), bo:bo + SSM_N]
                cm = xbc_ref[pl.ds(r0, CHUNK), co:co + SSM_N]
                cm_b = cm.astype(BF16)
                gmat = _nt(cm_b, bm.astype(BF16))
                bt = bm.T
                for pj in range(SSM_HEADS // SSM_GROUPS // 2):
                    ha = grp * (SSM_HEADS // SSM_GROUPS) + 2 * pj
                    hb_ = ha + 1
                    lo = ha * SSM_P
                    xs = xbc_ref[pl.ds(r0, CHUNK), lo:lo + LANE]
                    x2 = xs * _pair_sel(lane_lo, dtc[:, ha:ha + 1], dtc[:, hb_:hb_ + 1])
                    x2b = x2.astype(BF16)
                    ys, sts = [], []
                    for hh in (ha, hb_):
                        seg = lam_c[:, hh:hh + 1] - lam_r[hh:hh + 1, :]
                        dec = jnp.exp(jnp.where(ltri, seg, -jnp.inf))
                        ys.append(_nn((gmat * dec).astype(BF16), x2b))
                        sts.append(_nn((bt * f_r[hh:hh + 1, :]).astype(BF16), x2b))
                    hp = h_sc[:, lo:lo + LANE]
                    hin_ref[c, :, lo:lo + LANE] = hp
                    zz = _nn(cm_b, hp.astype(BF16))
                    e2 = _pair_sel(lane_lo, e_c[:, ha:ha + 1], e_c[:, hb_:hb_ + 1])
                    yv = jnp.where(lane_lo, ys[0], ys[1]) + e2 * zz
                    y_ref[pl.ds(r0, CHUNK), lo:lo + LANE] = yv + xs * dsk_ref[:, lo:lo + LANE]
                    cd2 = _pair_sel(lane_lo_n, cd[:, ha:ha + 1], cd[:, hb_:hb_ + 1])
                    h_sc[:, lo:lo + LANE] = hp * cd2 + jnp.where(lane_lo_n, sts[0], sts[1])
            return carry

        lax.fori_loop(0, cpb, chunk, 0)
        o_ref[...] = _gated_norm_fwd(y_ref[...], z_ref[...], g_ref[...])

    return pl.pallas_call(
        body, name="ssd_fwd", grid=(s // tm,),
        in_specs=[_rows(tm, CONV_CH), _rows(tm, LANE), _rows(tm, SSM_WIDTH), _whole((1, LANE)),
                  _whole((1, SSM_WIDTH)), _whole((1, SSM_WIDTH)), _whole((CHUNK, CHUNK)), _whole((CHUNK, CHUNK))],
        out_specs=[_rows(tm, SSM_WIDTH), _rows(tm, SSM_WIDTH),
                   pl.BlockSpec((cpb, SSM_N, SSM_WIDTH), lambda i: (i, 0, 0))],
        out_shape=[jax.ShapeDtypeStruct((s, SSM_WIDTH), F32), jax.ShapeDtypeStruct((s, SSM_WIDTH), F32),
                   jax.ShapeDtypeStruct((nc, SSM_N, SSM_WIDTH), F32)],
        scratch_shapes=[pltpu.VMEM((SSM_N, SSM_WIDTH), F32)],
        compiler_params=_cparams("arbitrary"),
    )(xbc, dt, z, a_neg, dskip_x, g_x, tril, triu)


def _outln(o, z_attn, o_ssm, w_out, x, gate, ln_g, ln_b, tgt):
    s = x.shape[0]
    tm = ROW_TILE

    def body(o_ref, z_ref, os_ref, w_ref, x_ref, gate_ref, g_ref, b_ref, t_ref,
             cat_ref, dmix_ref, gx_ref, do_ref, dz_ref, dl_ref, dos_ref, loss_ref, dg_ref, db_ref, dgate_ref):
        ov, zv = o_ref[...], z_ref[...]
        sz = _silu(zv)
        cat_ref[:, :MLA_WIDTH] = (ov * sz).astype(BF16)
        cat_ref[:, MLA_WIDTH:] = os_ref[...].astype(BF16)
        w = w_ref[...]
        mixed = _nn(cat_ref[...], w)
        gate_v = gate_ref[...]
        hv = DEEPNORM_ALPHA * x_ref[...] + gate_v * mixed
        mu = jnp.mean(hv, axis=-1, keepdims=True)
        hc = hv - mu
        rstd = lax.rsqrt(jnp.mean(hc * hc, axis=-1, keepdims=True) + LN_EPS)
        xhat = hc * rstd
        g = g_ref[...]
        err = xhat * g + b_ref[...] - t_ref[...]
        _acc_rows(loss_ref, jnp.full((1, LANE), (0.5 / D_MODEL) * jnp.sum(err * err), F32))
        dy = err * (1.0 / D_MODEL)
        _acc_rows(dg_ref, _colsum(dy * xhat))
        _acc_rows(db_ref, _colsum(dy))
        dxhat = dy * g
        dh = rstd * (dxhat - jnp.mean(dxhat, axis=-1, keepdims=True)
                     - xhat * jnp.mean(dxhat * xhat, axis=-1, keepdims=True))
        gx_ref[...] = DEEPNORM_ALPHA * dh
        _acc_rows(dgate_ref, _colsum(dh * mixed))
        dmix = (gate_v * dh).astype(BF16)
        dmix_ref[...] = dmix
        dcat = _nt(dmix, w)
        da = dcat[:, :MLA_WIDTH]
        dos_ref[...] = dcat[:, MLA_WIDTH:]
        dov = da * sz
        do_ref[...] = dov.astype(BF16)
        dz_ref[...] = da * ov * _silu_grad(zv)
        prod = dov * ov
        for h in range(MLA_HEADS):
            dsum = jnp.sum(prod[:, h * V_DIM:(h + 1) * V_DIM], axis=1, keepdims=True)
            dl_ref[h] = jnp.broadcast_to(dsum, (tm, LANE)).T[0:1, :]

    vec = _whole((1, D_MODEL))
    return pl.pallas_call(
        body, name="outln", grid=(s // tm,),
        in_specs=[_rows(tm, MLA_WIDTH), _rows(tm, MLA_WIDTH), _rows(tm, SSM_WIDTH), _whole((MIX_WIDTH, D_MODEL)),
                  _rows(tm, D_MODEL), vec, vec, vec, _rows(tm, D_MODEL)],
        out_specs=[_rows(tm, MIX_WIDTH), _rows(tm, D_MODEL), _rows(tm, D_MODEL), _rows(tm, MLA_WIDTH),
                   _rows(tm, MLA_WIDTH), pl.BlockSpec((MLA_HEADS, 1, tm), lambda i: (0, 0, i)), _rows(tm, SSM_WIDTH),
                   _whole((1, LANE)), vec, vec, vec],
        out_shape=[jax.ShapeDtypeStruct((s, MIX_WIDTH), BF16), jax.ShapeDtypeStruct((s, D_MODEL), BF16),
                   jax.ShapeDtypeStruct((s, D_MODEL), F32), jax.ShapeDtypeStruct((s, MLA_WIDTH), BF16),
                   jax.ShapeDtypeStruct((s, MLA_WIDTH), F32), jax.ShapeDtypeStruct((MLA_HEADS, 1, s), F32),
                   jax.ShapeDtypeStruct((s, SSM_WIDTH), F32), jax.ShapeDtypeStruct((1, LANE), F32),
                   jax.ShapeDtypeStruct((1, D_MODEL), F32), jax.ShapeDtypeStruct((1, D_MODEL), F32),
                   jax.ShapeDtypeStruct((1, D_MODEL), F32)],
        compiler_params=_cparams("arbitrary"),
    )(o, z_attn, o_ssm, w_out, x, gate, ln_g, ln_b, tgt)


def _ssd_bwd(dos, y, z, xbc, dt, hin, a_neg, dskip_x, g_x, tril, triu):
    s = xbc.shape[0]
    tm = min(SSD_ROWS, s)
    cpb = tm // CHUNK
    nb = s // tm
    gw = SSM_WIDTH // SSM_GROUPS
    hpg = SSM_HEADS // SSM_GROUPS

    def body(dos_ref, y_ref, z_ref, xbc_ref, dt_ref, hin_ref, a_ref, dsk_ref, g_ref, tril_ref, triu_ref,
             dxbc_ref, ddt_ref, dz_ref, dg_ref, ddsk_ref, da_ref, dh_sc, dy_sc):
        @pl.when(pl.program_id(0) == 0)
        def _():
            dh_sc[...] = jnp.zeros(dh_sc.shape, F32)

        yv, zv, dov = y_ref[...], z_ref[...], dos_ref[...]
        sz = _silu(zv)
        hf = yv * sz
        gv = g_ref[...]
        dgs, dhfs = [], []
        for grp in range(SSM_GROUPS):
            sl = slice(grp * gw, (grp + 1) * gw)
            n, rstd = _rms(hf[:, sl])
            dgs.append(_colsum(dov[:, sl] * n))
            dhfs.append(_rms_bwd(dov[:, sl] * gv[:, sl], n, rstd))
        dhf = jnp.concatenate(dhfs, axis=1)
        _acc_rows(dg_ref, jnp.concatenate(dgs, axis=1))
        dy_sc[...] = dhf * sz
        dz_ref[...] = dhf * yv * _silu_grad(zv)

        tril, triu = tril_ref[...], triu_ref[...]
        ltri = tril > 0
        utri = triu > 0
        lane = _lane_iota((CHUNK, LANE))
        lane1 = _lane_iota((1, LANE))
        lane_lo = lane < SSM_P
        row_last = lax.broadcasted_iota(jnp.int32, (CHUNK, LANE), 0) == CHUNK - 1
        a_neg_v = a_ref[...]

        def chunk(ci, carry):
            dsk_acc, da_acc = carry
            cl = cpb - 1 - ci
            r0 = pl.multiple_of(cl * CHUNK, CHUNK)
            rows = pl.ds(r0, CHUNK)
            dtc = dt_ref[rows, :]
            lam_c, lam_r, lam_last = _chunk_common(dtc, a_neg_v, tril, triu)
            e_c = jnp.exp(lam_c)
            f_c = jnp.exp(lam_last - lam_c)
            cd = jnp.exp(lam_last)
            dlam = jnp.zeros((CHUNK, LANE), F32)
            dlast = jnp.zeros((1, LANE), F32)
            ddt_x = jnp.zeros((CHUNK, LANE), F32)
            dsk_parts = []
            for grp in range(SSM_GROUPS):
                bo = SSM_WIDTH + grp * SSM_N
                co = SSM_WIDTH + SSM_GROUPS * SSM_N + grp * SSM_N
                bm = xbc_ref[rows, bo:bo + SSM_N]
                cm = xbc_ref[rows, co:co + SSM_N]
                bm_b, cm_b = bm.astype(BF16), cm.astype(BF16)
                gmat = _nt(cm_b, bm_b)
                gmat_t = _nt(bm_b, cm_b)
                ct_b = cm.T.astype(BF16)
                acc_dg = jnp.zeros((CHUNK, CHUNK), F32)
                acc_dgt = jnp.zeros((CHUNK, CHUNK), F32)
                d_b = jnp.zeros((CHUNK, SSM_N), F32)
                d_c = jnp.zeros((CHUNK, SSM_N), F32)
                for pj in range(hpg // 2):
                    ha = grp * hpg + 2 * pj
                    hb_ = ha + 1
                    lo = ha * SSM_P
                    xs = xbc_ref[rows, lo:lo + LANE]
                    dt2 = _pair_sel(lane_lo, dtc[:, ha:ha + 1], dtc[:, hb_:hb_ + 1])
                    x2 = xs * dt2
                    x2b = x2.astype(BF16)
                    dy2 = dy_sc[rows, lo:lo + LANE]
                    dy2b = dy2.astype(BF16)
                    hp = hin_ref[cl, :, lo:lo + LANE]
                    hp_b = hp.astype(BF16)
                    dhn = dh_sc[:, lo:lo + LANE]
                    dhn_b = dhn.astype(BF16)
                    e2 = _pair_sel(lane_lo, e_c[:, ha:ha + 1], e_c[:, hb_:hb_ + 1])
                    yo = e2 * _nn(cm_b, hp_b)
                    dzz_b = (e2 * dy2).astype(BF16)
                    d_c = d_c + _nt(dzz_b, hp_b)
                    cd2 = _pair_sel(lane_lo, cd[:, ha:ha + 1], cd[:, hb_:hb_ + 1])
                    dh_sc[:, lo:lo + LANE] = _nn(ct_b, dzz_b) + cd2 * dhn
                    t_yo = dy2 * yo
                    t_hh = dhn * hp
                    dx2 = jnp.zeros((CHUNK, LANE), F32)
                    for hh, msk in ((ha, lane_lo), (hb_, jnp.logical_not(lane_lo))):
                        x2h_b = jnp.where(msk, x2, 0.0).astype(BF16)
                        dy2h_b = jnp.where(msk, dy2, 0.0).astype(BF16)
                        lc = lam_c[:, hh:hh + 1]
                        lr = lam_r[hh:hh + 1, :]
                        dec = jnp.exp(jnp.where(ltri, lc - lr, -jnp.inf))
                        dect = jnp.exp(jnp.where(utri, lr - lc, -jnp.inf))
                        dmd = _nt(dy2h_b, x2b) * dec
                        dmtd = _nt(x2h_b, dy2b) * dect
                        acc_dg = acc_dg + dmd
                        acc_dgt = acc_dgt + dmtd
                        w_row = jnp.sum(dmd * gmat, axis=1, keepdims=True)
                        wt_row = jnp.sum(dmtd * gmat_t, axis=1, keepdims=True)
                        fcol = f_c[:, hh:hh + 1]
                        dx_h = _nn((gmat_t * dect).astype(BF16), dy2b) + _nn((bm * fcol).astype(BF16), dhn_b)
                        qh = _nt(x2h_b, dhn_b)
                        d_b = d_b + fcol * qh
                        dff = jnp.sum(bm * qh, axis=1, keepdims=True) * fcol
                        yo_row = jnp.sum(jnp.where(msk, t_yo, 0.0), axis=1, keepdims=True)
                        dlam_h = w_row - wt_row + yo_row - dff
                        hh_sum = jnp.sum(jnp.sum(jnp.where(msk, t_hh, 0.0), axis=1, keepdims=True), axis=0, keepdims=True)
                        last_h = cd[:, hh:hh + 1] * hh_sum + jnp.sum(dff, axis=0, keepdims=True)
                        dlam = jnp.where(lane == hh, dlam_h, dlam)
                        dlast = jnp.where(lane1 == hh, last_h, dlast)
                        dx2 = jnp.where(msk, dx_h, dx2)
                    dxbc_ref[rows, lo:lo + LANE] = dx2 * dt2 + dy2 * dsk_ref[:, lo:lo + LANE]
                    prod = dx2 * xs
                    for hh, msk in ((ha, lane_lo), (hb_, jnp.logical_not(lane_lo))):
                        col = jnp.sum(jnp.where(msk, prod, 0.0), axis=1, keepdims=True)
                        ddt_x = jnp.where(lane == hh, col, ddt_x)
                    dsk_parts.append(_colsum(dy2 * xs))
                d_c = d_c + _nn(acc_dg.astype(BF16), bm_b)
                d_b = d_b + _nn(acc_dgt.astype(BF16), cm_b)
                dxbc_ref[rows, bo:bo + SSM_N] = d_b
                dxbc_ref[rows, co:co + SSM_N] = d_c
            dlam = dlam + jnp.where(row_last, dlast, 0.0)
            da = _tri_left(triu, dlam)
            ddt_ref[rows, :] = da * a_neg_v + ddt_x
            return dsk_acc + jnp.concatenate(dsk_parts, axis=1), da_acc + _colsum(da * dtc)

        dsk_tot, da_tot = lax.fori_loop(
            0, cpb, chunk, (jnp.zeros((1, SSM_WIDTH), F32), jnp.zeros((1, LANE), F32)))
        _acc_rows(ddsk_ref, dsk_tot)
        _acc_rows(da_ref, da_tot)

    rev = lambda i: (nb - 1 - i, 0)
    rrows = lambda w: pl.BlockSpec((tm, w), rev)
    return pl.pallas_call(
        body, name="ssd_bwd", grid=(nb,),
        in_specs=[rrows(SSM_WIDTH), rrows(SSM_WIDTH), rrows(SSM_WIDTH), rrows(CONV_CH), rrows(LANE),
                  pl.BlockSpec((cpb, SSM_N, SSM_WIDTH), lambda i: (nb - 1 - i, 0, 0)),
                  _whole((1, LANE)), _whole((1, SSM_WIDTH)), _whole((1, SSM_WIDTH)),
                  _whole((CHUNK, CHUNK)), _whole((CHUNK, CHUNK))],
        out_specs=[rrows(CONV_CH), rrows(LANE), rrows(SSM_WIDTH),
                   _whole((1, SSM_WIDTH)), _whole((1, SSM_WIDTH)), _whole((1, LANE))],
        out_shape=[jax.ShapeDtypeStruct((s, CONV_CH), F32), jax.ShapeDtypeStruct((s, LANE), F32),
                   jax.ShapeDtypeStruct((s, SSM_WIDTH), F32), jax.ShapeDtypeStruct((1, SSM_WIDTH), F32),
                   jax.ShapeDtypeStruct((1, SSM_WIDTH), F32), jax.ShapeDtypeStruct((1, LANE), F32)],
        scratch_shapes=[pltpu.VMEM((SSM_N, SSM_WIDTH), F32), pltpu.VMEM((tm, SSM_WIDTH), F32)],
        compiler_params=_cparams("arbitrary"),
    )(dos, y, z, xbc, dt, hin, a_neg, dskip_x, g_x, tril, triu)


def _ssd_post_bwd(xbc_raw, dxa, ddt, dt_raw, conv_w, conv_b, dt_bias_p):
    s = xbc_raw.shape[0]
    tm = ROW_TILE
    hb = tm // HALO
    nt = s // tm
    ext = tm + HALO

    def body(x_ref, prev_ref, next_ref, d_ref, dnext_ref, ddt_ref, dtr_ref, w_ref, b_ref, db_ref,
             dx_ref, ddtr_ref, dw_ref, dcb_ref, ddb_ref, xe_sc, de_sc):
        i = pl.program_id(0)
        w = w_ref[...]
        xe_sc[pl.ds(0, HALO), :] = jnp.where(i > 0, prev_ref[...], 0.0)
        xe_sc[pl.ds(HALO, tm), :] = x_ref[...]
        xe_sc[pl.ds(HALO + tm, HALO), :] = next_ref[...]
        pre = _conv_taps(xe_sc, w, ext, 0) + b_ref[...]
        sg = _silu_grad(pre)
        de_sc[pl.ds(0, tm), :] = d_ref[...] * sg[:tm]
        de_sc[pl.ds(tm, HALO), :] = jnp.where(i < nt - 1, dnext_ref[...] * sg[tm:], 0.0)
        dconv = de_sc[pl.ds(0, tm), :]
        acc = None
        dws = []
        for k in range(CONV_K):
            term = de_sc[pl.ds(CONV_K - 1 - k, tm), :] * w[k:k + 1, :]
            acc = term if acc is None else acc + term
            dws.append(_colsum(dconv * xe_sc[pl.ds(HALO - (CONV_K - 1) + k, tm), :]))
        dx_ref[...] = acc
        _acc_rows(dw_ref, jnp.concatenate(dws, axis=0))
        _acc_rows(dcb_ref, _colsum(dconv))
        ddtr = ddt_ref[...] * _sigmoid(dtr_ref[...] + db_ref[...])
        ddtr_ref[...] = ddtr
        _acc_rows(ddb_ref, _colsum(ddtr))

    halo_prev = pl.BlockSpec((HALO, CONV_CH), lambda i: (jnp.maximum(i * hb - 1, 0), 0))
    halo_next = pl.BlockSpec((HALO, CONV_CH), lambda i: (jnp.minimum((i + 1) * hb, s // HALO - 1), 0))
    return pl.pallas_call(
        body, name="ssd_post_bwd", grid=(nt,),
        in_specs=[_rows(tm, CONV_CH), halo_prev, halo_next, _rows(tm, CONV_CH), halo_next, _rows(tm, LANE),
                  _rows(tm, LANE), _whole((CONV_K, CONV_CH)), _whole((1, CONV_CH)), _whole((1, LANE))],
        out_specs=[_rows(tm, CONV_CH), _rows(tm, LANE), _whole((CONV_K, CONV_CH)), _whole((1, CONV_CH)),
                   _whole((1, LANE))],
        out_shape=[jax.ShapeDtypeStruct((s, CONV_CH), F32), jax.ShapeDtypeStruct((s, LANE), F32),
                   jax.ShapeDtypeStruct((CONV_K, CONV_CH), F32), jax.ShapeDtypeStruct((1, CONV_CH), F32),
                   jax.ShapeDtypeStruct((1, LANE), F32)],
        scratch_shapes=[pltpu.VMEM((tm + 2 * HALO, CONV_CH), F32), pltpu.VMEM((ext, CONV_CH), F32)],
        compiler_params=_cparams("arbitrary"),
    )(xbc_raw, xbc_raw, xbc_raw, dxa, dxa, ddt, dt_raw, conv_w, conv_b, dt_bias_p)


def _qbwd(dq_att, q_lat, g_q, w_qb_p, cos, sin):
    s = q_lat.shape[0]
    tm = ROW_TILE
    wq = MLA_HEADS * HEAD_PAD

    def body(dq_ref, ql_ref, g_ref, w_ref, cos_ref, sin_ref, dql_ref, draw_ref, dg_ref):
        c, sn = cos_ref[...], sin_ref[...]
        for h in range(MLA_HEADS):
            o = h * HEAD_PAD
            dqh = dq_ref[h] * ATTN_SCALE
            draw_ref[:, o:o + QK_NOPE] = dqh[:, :QK_NOPE].astype(BF16)
            draw_ref[:, o + QK_NOPE:o + HEAD_PAD] = _rope_transposed(dqh[:, QK_NOPE:], c, sn).astype(BF16)
        dn = _nt(draw_ref[...], w_ref[...])
        xhat, rstd = _rms(ql_ref[...])
        _acc_rows(dg_ref, _colsum(dn * xhat))
        dql_ref[...] = _rms_bwd(dn * g_ref[...], xhat, rstd)

    return pl.pallas_call(
        body, name="qbwd", grid=(s // tm,),
        in_specs=[pl.BlockSpec((MLA_HEADS, tm, HEAD_PAD), lambda i: (0, i, 0)), _rows(tm, Q_RANK), _whole((1, Q_RANK)),
                  _whole((Q_RANK, wq)), _rows(tm, LANE), _rows(tm, LANE)],
        out_specs=[_rows(tm, Q_RANK), _rows(tm, wq), _whole((1, Q_RANK))],
        out_shape=[jax.ShapeDtypeStruct((s, Q_RANK), F32), jax.ShapeDtypeStruct((s, wq), BF16),
                   jax.ShapeDtypeStruct((1, Q_RANK), F32)],
        compiler_params=_cparams("arbitrary"),
    )(dq_att, q_lat, g_q, w_qb_p, cos, sin)


def _kvbwd(dk_att, dv, kv_lat, g_kv, w_kvb_p, cos, sin):
    s = kv_lat.shape[0]
    tm = ROW_TILE
    wk = MLA_HEADS * HEAD_PAD
    wr = MLA_HEADS * (QK_NOPE + V_DIM)

    def body(dk_ref, dv_ref, kl_ref, g_ref, w_ref, cos_ref, sin_ref, dkl_ref, draw_ref, dg_ref):
        dkr = None
        for h in range(MLA_HEADS):
            o = h * HEAD_PAD
            draw_ref[:, h * QK_NOPE:(h + 1) * QK_NOPE] = dk_ref[:, o:o + QK_NOPE].astype(BF16)
            part = dk_ref[:, o + QK_NOPE:o + HEAD_PAD]
            dkr = part if dkr is None else dkr + part
        draw_ref[:, MLA_HEADS * QK_NOPE:] = dv_ref[...].astype(BF16)
        dn = _nt(draw_ref[...], w_ref[...])
        xhat, rstd = _rms(kl_ref[:, :KV_RANK])
        _acc_rows(dg_ref, _colsum(dn * xhat))
        dkl_ref[:, :KV_RANK] = _rms_bwd(dn * g_ref[...], xhat, rstd)
        dkl_ref[:, KV_RANK:] = _rope_transposed(dkr, cos_ref[...], sin_ref[...])

    return pl.pallas_call(
        body, name="kvbwd", grid=(s // tm,),
        in_specs=[_rows(tm, wk), _rows(tm, MLA_WIDTH), _rows(tm, KV_LAT_PAD), _whole((1, KV_RANK)),
                  _whole((KV_RANK, wr)), _rows(tm, LANE), _rows(tm, LANE)],
        out_specs=[_rows(tm, KV_LAT_PAD), _rows(tm, wr), _whole((1, KV_RANK))],
        out_shape=[jax.ShapeDtypeStruct((s, KV_LAT_PAD), F32), jax.ShapeDtypeStruct((s, wr), BF16),
                   jax.ShapeDtypeStruct((1, KV_RANK), F32)],
        compiler_params=_cparams("arbitrary"),
    )(dk_att, dv, kv_lat, g_kv, w_kvb_p, cos, sin)


def _inproj_bwd(pieces, w_in_pt, x, scale1p, gx1):
    s = x.shape[0]
    tm = ROW_TILE

    def body(*refs):
        p_refs = refs[:len(IN_PAD)]
        w_ref, x_ref, sc_ref, gx1_ref, gx_ref, dp_ref, dsc_ref, dsh_ref = refs[len(IN_PAD):]
        off = 0
        for ref, w in zip(p_refs, IN_PAD):
            dp_ref[:, off:off + w] = ref[...].astype(BF16)
            off += w
        du = _nn(dp_ref[...], w_ref[...])
        gx_ref[...] = gx1_ref[...] + du * sc_ref[...]
        _acc_rows(dsc_ref, _colsum(du * x_ref[...]))
        _acc_rows(dsh_ref, _colsum(du))

    vec = _whole((1, D_MODEL))
    return pl.pallas_call(
        body, name="inproj_bwd", grid=(s // tm,),
        in_specs=[_rows(tm, w) for w in IN_PAD] + [_whole((IN_PAD_WIDTH, D_MODEL)), _rows(tm, D_MODEL), vec,
                                                    _rows(tm, D_MODEL)],
        out_specs=[_rows(tm, D_MODEL), _rows(tm, IN_PAD_WIDTH), vec, vec],
        out_shape=[jax.ShapeDtypeStruct((s, D_MODEL), F32), jax.ShapeDtypeStruct((s, IN_PAD_WIDTH), BF16),
                   jax.ShapeDtypeStruct((1, D_MODEL), F32), jax.ShapeDtypeStruct((1, D_MODEL), F32)],
        compiler_params=_cparams("arbitrary"),
    )(*pieces, w_in_pt, x, scale1p, gx1)


def _matmul_tn_rows(name, a, b, tk):
    s, k = a.shape
    n = b.shape[1]
    tm = min(ATTN_TILE, s)

    def body(a_ref, b_ref, o_ref):
        @pl.when(pl.program_id(1) == 0)
        def _():
            o_ref[...] = jnp.zeros_like(o_ref)
        o_ref[...] += _tn(a_ref[...], b_ref[...])

    return pl.pallas_call(
        body, name=name, grid=(k // tk, s // tm),
        in_specs=[pl.BlockSpec((tm, tk), lambda j, i: (i, j)), pl.BlockSpec((tm, n), lambda j, i: (i, 0))],
        out_specs=pl.BlockSpec((tk, n), lambda j, i: (j, 0)),
        out_shape=jax.ShapeDtypeStruct((k, n), F32),
        compiler_params=_cparams("parallel", "arbitrary"),
    )(a, b)


def _matmul_tn(name, a, b, tn):
    s, k = a.shape
    n = b.shape[1]
    tm = min(ATTN_TILE, s)

    def body(a_ref, b_ref, o_ref):
        @pl.when(pl.program_id(1) == 0)
        def _():
            o_ref[...] = jnp.zeros_like(o_ref)
        o_ref[...] += _tn(a_ref[...], b_ref[...])

    return pl.pallas_call(
        body, name=name, grid=(n // tn, s // tm),
        in_specs=[pl.BlockSpec((tm, k), lambda j, i: (i, 0)), pl.BlockSpec((tm, tn), lambda j, i: (i, j))],
        out_specs=pl.BlockSpec((k, tn), lambda j, i: (0, j)),
        out_shape=jax.ShapeDtypeStruct((k, n), F32),
        compiler_params=_cparams("parallel", "arbitrary"),
    )(a, b)


def _pack_w_in_t(w_in_t):
    parts, off = [], 0
    for w, wp in zip(IN_SPLITS, IN_PAD):
        parts.append(jnp.pad(w_in_t[off:off + w], ((0, wp - w), (0, 0))))
        off += w
    return jnp.concatenate(parts, axis=0)


def _unpack_w_in_t(g):
    parts, off = [], 0
    for w, wp in zip(IN_SPLITS, IN_PAD):
        parts.append(g[off:off + w])
        off += wp
    return jnp.concatenate(parts, axis=0)


def _pack_w_qb(w_qb):
    w = w_qb.reshape(Q_RANK, MLA_HEADS, QK_HEAD)
    return jnp.pad(w, ((0, 0), (0, 0), (0, HEAD_PAD - QK_HEAD))).reshape(Q_RANK, MLA_HEADS * HEAD_PAD)


def _unpack_w_qb(g):
    return g.reshape(Q_RANK, MLA_HEADS, HEAD_PAD)[:, :, :QK_HEAD].reshape(Q_RANK, MLA_HEADS * QK_HEAD)


def _pack_w_kvb(w_kvb):
    w = w_kvb.reshape(KV_RANK, MLA_HEADS, QK_NOPE + V_DIM)
    return jnp.concatenate([w[:, :, :QK_NOPE].reshape(KV_RANK, -1), w[:, :, QK_NOPE:].reshape(KV_RANK, -1)], axis=1)


def _unpack_w_kvb(g):
    gk = g[:, :MLA_HEADS * QK_NOPE].reshape(KV_RANK, MLA_HEADS, QK_NOPE)
    gv = g[:, MLA_HEADS * QK_NOPE:].reshape(KV_RANK, MLA_HEADS, V_DIM)
    return jnp.concatenate([gk, gv], axis=2).reshape(KV_RANK, -1)


def _rope_tables(positions):
    inv_freq = 1.0 / (ROPE_THETA ** (jnp.arange(ROPE_HALF, dtype=F32) / ROPE_HALF))
    ang = positions.astype(F32)[:, None] * inv_freq
    cos, sin = jnp.cos(ang), jnp.sin(ang)
    zeros = jnp.zeros((positions.shape[0], LANE - QK_ROPE), F32)
    return jnp.concatenate([cos, cos, zeros], axis=1), jnp.concatenate([-sin, sin, zeros], axis=1)


def _local_step(x, tgt, positions, mod, w_in_t, q_norm_g, w_qb_p, kv_norm_g, w_kvb_p, conv_w, conv_b, dt_bias,
                a_log, d_skip, ssm_norm_g, w_out_b, ln_g, ln_b):
    row = lambda v: v.reshape(1, -1)
    shift, scale, gate = mod[:D_MODEL], mod[D_MODEL:2 * D_MODEL], mod[2 * D_MODEL:]
    scale1p = row(1.0 + scale)
    w_in_p = _pack_w_in_t(w_in_t)
    cos, sin = _rope_tables(positions)
    a_neg = row(jnp.pad(-jnp.exp(a_log), (0, LANE - SSM_HEADS)))
    dskip_x = row(jnp.repeat(d_skip, SSM_P))
    dt_bias_p = row(jnp.pad(dt_bias, (0, LANE - SSM_HEADS)))
    tri = jnp.tril(jnp.ones((CHUNK, CHUNK), F32))
    tril, triu = tri.astype(BF16), tri.T.astype(BF16)

    u_bf, q_lat, kv_lat, z_attn, xbc_raw, dt_raw, z_ssm = _inproj(x, scale1p, row(shift), w_in_p)
    nq_bf, q_att = _qpath(q_lat, row(q_norm_g), w_qb_p, cos, sin)
    nkv_bf, k_att, v_att = _kvpath(kv_lat, row(kv_norm_g), w_kvb_p, cos, sin)
    o, lse_rows = _attn_fwd(q_att, k_att, v_att)
    xbc, dt = _ssd_pre(xbc_raw, dt_raw, conv_w, row(conv_b), dt_bias_p)
    y, o_ssm, hin = _ssd_fwd(xbc, dt, z_ssm, a_neg, dskip_x, row(ssm_norm_g), tril, triu)
    (cat_bf, dmix_bf, gx1, do_bf, dz_attn, delta_rows, dos, loss, d_ln_g, d_ln_b, d_gate) = _outln(
        o, z_attn, o_ssm, w_out_b, x, row(gate), row(ln_g), row(ln_b), tgt)

    g_w_out = _matmul_tn("gw_out", cat_bf, dmix_bf, 512)
    dk_att, dv, dq_att = _attn_bwd(q_att, k_att, v_att, do_bf, lse_rows, delta_rows)
    dq_lat, dqraw_bf, d_q_norm_g = _qbwd(dq_att, q_lat, row(q_norm_g), w_qb_p, cos, sin)
    dkv_lat, dkvraw_bf, d_kv_norm_g = _kvbwd(dk_att, dv, kv_lat, row(kv_norm_g), w_kvb_p, cos, sin)
    g_w_qb = _matmul_tn("gw_qb", nq_bf, dqraw_bf, MLA_HEADS * HEAD_PAD)
    g_w_kvb = _matmul_tn("gw_kvb", nkv_bf, dkvraw_bf, MLA_HEADS * (QK_NOPE + V_DIM))
    dxa, ddt, dz_ssm, d_ssm_g, ddsk_x, d_a = _ssd_bwd(dos, y, z_ssm, xbc, dt, hin, a_neg, dskip_x, row(ssm_norm_g),
                                                       tril, triu)
    dxbc_raw, ddt_raw, d_conv_w, d_conv_b, d_dt_bias = _ssd_post_bwd(xbc_raw, dxa, ddt, dt_raw, conv_w, row(conv_b),
                                                                     dt_bias_p)
    grad_x, dproj_bf, d_scale, d_shift = _inproj_bwd((dq_lat, dkv_lat, dz_attn, dxbc_raw, ddt_raw, dz_ssm),
                                                     w_in_p, x, scale1p, gx1)
    g_w_in_t = _unpack_w_in_t(_matmul_tn_rows("gw_in", dproj_bf, u_bf, 640))
    return dict(
        loss=loss[0, 0], grad_x=grad_x,
        dmod=jnp.concatenate([d_shift[0], d_scale[0], d_gate[0]]),
        w_in_t=g_w_in_t, q_norm_g=d_q_norm_g[0], w_qb=g_w_qb, kv_norm_g=d_kv_norm_g[0], w_kvb=g_w_kvb,
        conv_w=d_conv_w, conv_b=d_conv_b[0], dt_bias=d_dt_bias[0, :SSM_HEADS],
        a_log=d_a[0, :SSM_HEADS] * a_neg[0, :SSM_HEADS],
        d_skip=ddsk_x.reshape(SSM_HEADS, SSM_P).sum(axis=1), ssm_norm_g=d_ssm_g[0], w_out=g_w_out,
        ln_g=d_ln_g[0], ln_b=d_ln_b[0])


ADAM_ROWS = 512


def _my_index():
    return 4 * lax.axis_index("x") + 2 * lax.axis_index("y") + lax.axis_index("c")


def _exchange(name, sends, gather):
    n = len(sends)
    peers = N_DEV - 1

    def body(*refs):
        send_refs, recv_refs = refs[:n], refs[n:2 * n]
        send_sems, recv_sems, local_sems = refs[2 * n:]
        x, y, c = lax.axis_index("x"), lax.axis_index("y"), lax.axis_index("c")
        me = 4 * x + 2 * y + c

        def src(a, idx):
            return send_refs[a] if gather else send_refs[a].at[idx]

        owns = [pltpu.make_async_copy(src(a, me), recv_refs[a].at[me], local_sems.at[a]) for a in range(n)]
        for cp in owns:
            cp.start()
        copies = []
        for k in range(1, N_DEV):
            px, py, pc = x ^ ((k >> 2) & 1), y ^ ((k >> 1) & 1), c ^ (k & 1)
            peer = 4 * px + 2 * py + pc
            for a in range(n):
                copies.append(pltpu.make_async_remote_copy(
                    src_ref=src(a, peer), dst_ref=recv_refs[a].at[me],
                    send_sem=send_sems.at[a * peers + k - 1], recv_sem=recv_sems.at[a * peers + k - 1],
                    device_id=(px, py, pc), device_id_type=pl.DeviceIdType.MESH))
        for cp in copies:
            cp.start()
        for cp in copies:
            cp.wait()
        for cp in owns:
            cp.wait()

    block_shape = lambda a: a.shape if gather else a.shape[1:]
    return pl.pallas_call(
        body, name=name,
        in_specs=[pl.BlockSpec(memory_space=pl.ANY)] * n, out_specs=[pl.BlockSpec(memory_space=pl.ANY)] * n,
        out_shape=[jax.ShapeDtypeStruct((N_DEV, *block_shape(a)), a.dtype) for a in sends],
        scratch_shapes=[pltpu.SemaphoreType.DMA((n * peers,)), pltpu.SemaphoreType.DMA((n * peers,)),
                        pltpu.SemaphoreType.DMA((n,))],
    )(*sends)


def _flat_rows(parts, row_multiple):
    flat = jnp.concatenate([p.reshape(-1) for p in parts])
    chunk = row_multiple * LANE
    total = -(-flat.shape[0] // chunk) * chunk
    return jnp.pad(flat, (0, total - flat.shape[0])).reshape(-1, LANE)


def _unflat(flat, shapes):
    flat = flat.reshape(-1)
    out, off = [], 0
    for shp in shapes:
        n = math.prod(shp)
        out.append(flat[off:off + n].reshape(shp))
        off += n
    return out


def _adam_update(g, w, m, v):
    m2 = ADAM_B1 * m + (1.0 - ADAM_B1) * g
    v2 = ADAM_B2 * v + (1.0 - ADAM_B2) * (g * g)
    m_hat = m2 / (1.0 - ADAM_B1 ** ADAM_STEP)
    v_hat = v2 / (1.0 - ADAM_B2 ** ADAM_STEP)
    delta = -ADAM_LR * (m_hat / (jnp.sqrt(v_hat) + ADAM_EPS) + ADAM_WD * w)
    return delta, m2, v2


def _adamw_summed(name, parts, w, m, v):
    r = w.shape[0]
    tr = min(ADAM_ROWS, r)

    def body(p_ref, w_ref, m_ref, v_ref, g_ref, d_ref, m2_ref, v2_ref):
        g = p_ref[0]
        for j in range(1, N_DEV):
            g = g + p_ref[j]
        g_ref[...] = g
        d_ref[...], m2_ref[...], v2_ref[...] = _adam_update(g, w_ref[...], m_ref[...], v_ref[...])

    rows = _rows(tr, LANE)
    return pl.pallas_call(
        body, name=name, grid=(r // tr,),
        in_specs=[pl.BlockSpec((N_DEV, tr, LANE), lambda i: (0, i, 0)), rows, rows, rows],
        out_specs=[rows] * 4, out_shape=[jax.ShapeDtypeStruct((r, LANE), F32)] * 4,
        compiler_params=_cparams("parallel"),
    )(parts, w, m, v)


def _modpart(c_all, w_ada, b_cols):
    def body(c_ref, w_ref, b_ref, o_ref):
        o_ref[...] = _nn(c_ref[...].astype(BF16), w_ref[...].astype(BF16)) + b_ref[...]

    return pl.pallas_call(
        body, name="modpart", out_shape=jax.ShapeDtypeStruct((N_DEV, w_ada.shape[1]), F32),
    )(c_all, w_ada, b_cols)


def _adamw_w_ada(c_all_t, dmod_cols, w, m, v):
    def body(c_ref, d_ref, w_ref, m_ref, v_ref, g_ref, dl_ref, m2_ref, v2_ref):
        g = c_ref[:, 0:1] * d_ref[0:1, :]
        for b in range(1, N_DEV):
            g = g + c_ref[:, b:b + 1] * d_ref[b:b + 1, :]
        g_ref[...] = g
        dl_ref[...], m2_ref[...], v2_ref[...] = _adam_update(g, w_ref[...], m_ref[...], v_ref[...])

    return pl.pallas_call(
        body, name="adamw_w_ada", out_shape=[jax.ShapeDtypeStruct(w.shape, F32)] * 4,
        compiler_params=pltpu.CompilerParams(vmem_limit_bytes=VMEM_LIMIT),
    )(c_all_t, dmod_cols, w, m, v)


W_IN_SHARD = IN_WIDTH // N_DEV
W_IN_SHARD_LANES = -(-W_IN_SHARD // LANE) * LANE
BF16_ROWS = 16
W_IN_SEND_ROWS = -(-W_IN_SHARD // BF16_ROWS) * BF16_ROWS


def _transpose_cast(w_pad):
    def body(w_ref, o_ref):
        o_ref[...] = w_ref[...].T.astype(BF16)

    return pl.pallas_call(
        body, name="w_in_transpose", out_shape=jax.ShapeDtypeStruct(w_pad.shape[::-1], BF16),
        compiler_params=pltpu.CompilerParams(vmem_limit_bytes=VMEM_LIMIT),
    )(w_pad)


def _adamw_w_in(parts, w, m, v):
    rows_t = parts.shape[1]
    d, cols = w.shape
    tb = ROW_TILE

    def body(p_ref, w_ref, m_ref, v_ref, g_ref, d_ref, m2_ref, v2_ref):
        gt = p_ref[0].astype(F32)
        for j in range(1, N_DEV):
            gt = gt + p_ref[j].astype(F32)
        gt = jnp.concatenate([gt, jnp.zeros((W_IN_SHARD_LANES - rows_t, tb), F32)], axis=0)
        g = gt.T[:, :cols]
        g_ref[...] = g
        d_ref[...], m2_ref[...], v2_ref[...] = _adam_update(g, w_ref[...], m_ref[...], v_ref[...])

    blk = _rows(tb, cols)
    return pl.pallas_call(
        body, name="adamw_w_in", grid=(d // tb,),
        in_specs=[pl.BlockSpec((N_DEV, rows_t, tb), lambda i: (0, 0, i)), blk, blk, blk],
        out_specs=[blk] * 4, out_shape=[jax.ShapeDtypeStruct(w.shape, F32)] * 4,
        compiler_params=_cparams("parallel"),
    )(parts, w, m, v)


SHARDED = ("w_qb", "w_kvb", "w_out")
REPLICATED = ("b_ada", "q_norm_g", "kv_norm_g", "conv_b", "dt_bias", "a_log", "d_skip", "ssm_norm_g", "ln_g", "ln_b")
WEIGHTS = ("w_ada", "b_ada", "w_in", "q_norm_g", "w_qb", "kv_norm_g", "w_kvb", "conv_w", "conv_b", "dt_bias",
           "a_log", "d_skip", "ssm_norm_g", "w_out", "ln_g", "ln_b")
HEAD_COLS = QK_NOPE + V_DIM


def _adamw_blocks(name, parts, w, m, v):
    r, c = w.shape
    tr = ROW_TILE if r % ROW_TILE == 0 else r

    def body(p_ref, w_ref, m_ref, v_ref, g_ref, d_ref, m2_ref, v2_ref):
        g = p_ref[0].astype(F32)
        for j in range(1, N_DEV):
            g = g + p_ref[j].astype(F32)
        g_ref[...] = g
        d_ref[...], m2_ref[...], v2_ref[...] = _adam_update(g, w_ref[...], m_ref[...], v_ref[...])

    blk = _rows(tr, c)
    return pl.pallas_call(
        body, name=name, grid=(r // tr,),
        in_specs=[pl.BlockSpec((N_DEV, tr, c), lambda i: (0, i, 0)), blk, blk, blk],
        out_specs=[blk] * 4, out_shape=[jax.ShapeDtypeStruct(w.shape, F32)] * 4,
        compiler_params=_cparams("parallel"),
    )(parts, w, m, v)


def kernel(x, c, positions, w_ada, b_ada, w_in, q_norm_g, w_qb, kv_norm_g, w_kvb, conv_w, conv_b, dt_bias, a_log, d_skip, ssm_norm_g, w_out, ln_g, ln_b, loss_target, m_w_ada, m_b_ada, m_w_in, m_q_norm_g, m_w_qb, m_kv_norm_g, m_w_kvb, m_conv_w, m_conv_b, m_dt_bias, m_a_log, m_d_skip, m_ssm_norm_g, m_w_out, m_ln_g, m_ln_b, v_w_ada, v_b_ada, v_w_in, v_q_norm_g, v_w_qb, v_kv_norm_g, v_w_kvb, v_conv_w, v_conv_b, v_dt_bias, v_a_log, v_d_skip, v_ssm_norm_g, v_w_out, v_ln_g, v_ln_b):
    given = dict(w_ada=w_ada, b_ada=b_ada, w_in=w_in, q_norm_g=q_norm_g, w_qb=w_qb, kv_norm_g=kv_norm_g, w_kvb=w_kvb,
                 conv_w=conv_w, conv_b=conv_b, dt_bias=dt_bias, a_log=a_log, d_skip=d_skip, ssm_norm_g=ssm_norm_g,
                 w_out=w_out, ln_g=ln_g, ln_b=ln_b)
    mom = dict(w_ada=m_w_ada, b_ada=m_b_ada, w_in=m_w_in, q_norm_g=m_q_norm_g, w_qb=m_w_qb, kv_norm_g=m_kv_norm_g,
               w_kvb=m_w_kvb, conv_w=m_conv_w, conv_b=m_conv_b, dt_bias=m_dt_bias, a_log=m_a_log, d_skip=m_d_skip,
               ssm_norm_g=m_ssm_norm_g, w_out=m_w_out, ln_g=m_ln_g, ln_b=m_ln_b)
    var = dict(w_ada=v_w_ada, b_ada=v_b_ada, w_in=v_w_in, q_norm_g=v_q_norm_g, w_qb=v_w_qb, kv_norm_g=v_kv_norm_g,
               w_kvb=v_w_kvb, conv_w=v_conv_w, conv_b=v_conv_b, dt_bias=v_dt_bias, a_log=v_a_log, d_skip=v_d_skip,
               ssm_norm_g=v_ssm_norm_g, w_out=v_w_out, ln_g=v_ln_g, ln_b=v_ln_b)
    w0 = {k: a[0] for k, a in given.items()}
    m0 = {k: a[0] for k, a in mom.items()}
    v0 = {k: a[0] for k, a in var.items()}
    me = _my_index()

    w_in_rows = _transpose_cast(jnp.pad(w0["w_in"], ((0, 0), (0, W_IN_SHARD_LANES - W_IN_SHARD))))
    g_w_in, g_w_qb, g_w_kvb, g_w_out, g_conv_w, c_all = _exchange(
        "gather_weights", [w_in_rows] + [w0[k].astype(BF16) for k in SHARDED] + [w0["conv_w"], c], gather=True)
    c_all = c_all.reshape(N_DEV, D_MODEL)
    w_in_t = g_w_in[:, :W_IN_SHARD, :].reshape(IN_WIDTH, D_MODEL)
    w_qb_p = jnp.pad(g_w_qb, ((0, 0), (0, 0), (0, HEAD_PAD - QK_HEAD))).transpose(1, 0, 2).reshape(Q_RANK, -1)
    w_kvb_p = g_w_kvb.reshape(N_DEV, KV_RANK, 2, QK_NOPE).transpose(1, 2, 0, 3).reshape(KV_RANK, -1)
    w_out_b = g_w_out.reshape(MIX_WIDTH, D_MODEL)
    conv_w_full = g_conv_w.transpose(1, 0, 2).reshape(CONV_K, CONV_CH)

    ada_cols = w0["w_ada"].shape[1]
    b_cols = lax.dynamic_slice(w0["b_ada"], (me * ada_cols,), (ada_cols,)).reshape(1, ada_cols)
    mod_all, = _exchange("gather_mod", [_modpart(c_all, w0["w_ada"], b_cols)], gather=True)
    mod = lax.dynamic_index_in_dim(mod_all, me, axis=1, keepdims=False).reshape(-1)

    loc = _local_step(x[0], loss_target[0], positions[0], mod, w_in_t, w0["q_norm_g"], w_qb_p,
                      w0["kv_norm_g"], w_kvb_p, conv_w_full, w0["conv_b"], w0["dt_bias"], w0["a_log"],
                      w0["d_skip"], w0["ssm_norm_g"], w_out_b, w0["ln_g"], w0["ln_b"])

    rep_shapes = [w0[k].shape for k in REPLICATED] + [(1,)]
    rep_local = [loc["dmod"]] + [loc[k] for k in REPLICATED[1:]] + [loc["loss"].reshape(1)]
    rep_parts, conv_parts = _exchange("gather_small", [_flat_rows(rep_local, HALO), loc["conv_w"]], gather=True)
    conv_cols = w0["conv_w"].shape[1]
    conv_mine = lax.dynamic_slice(conv_parts, (0, 0, me * conv_cols), (N_DEV, CONV_K, conv_cols))
    outs = {"conv_w": _adamw_blocks("adamw_conv_w", conv_mine, w0["conv_w"], m0["conv_w"], v0["conv_w"])}
    zero1 = jnp.zeros((1,), F32)
    rep = _adamw_summed("adamw_replicated", rep_parts,
                        _flat_rows([w0[k] for k in REPLICATED] + [zero1], HALO),
                        _flat_rows([m0[k] for k in REPLICATED] + [zero1], HALO),
                        _flat_rows([v0[k] for k in REPLICATED] + [zero1], HALO))
    rep_g, rep_d, rep_m, rep_v = [_unflat(a, rep_shapes) for a in rep]
    loss = rep_g[-1][0]

    dmod_all = rep_parts.reshape(N_DEV, -1)[:, :3 * D_MODEL]
    dmod_cols = lax.dynamic_slice(dmod_all, (0, me * ada_cols), (N_DEV, ada_cols))
    outs["w_ada"] = _adamw_w_ada(c_all.T, dmod_cols, w0["w_ada"], m0["w_ada"], v0["w_ada"])

    send_w_in = loc["w_in_t"].astype(BF16).reshape(N_DEV, W_IN_SHARD, D_MODEL)
    send_w_in = jnp.pad(send_w_in, ((0, 0), (0, W_IN_SEND_ROWS - W_IN_SHARD), (0, 0)))
    send_w_qb = loc["w_qb"].astype(BF16).reshape(Q_RANK, N_DEV, HEAD_PAD)[:, :, :QK_HEAD].transpose(1, 0, 2)
    send_w_kvb = loc["w_kvb"].astype(BF16).reshape(KV_RANK, 2, N_DEV, QK_NOPE).transpose(2, 0, 1, 3)
    send_w_kvb = send_w_kvb.reshape(N_DEV, KV_RANK, HEAD_COLS)
    send_w_out = loc["w_out"].astype(BF16).reshape(N_DEV, MIX_WIDTH // N_DEV, D_MODEL)
    r_w_in, r_w_qb, r_w_kvb, r_w_out = _exchange(
        "scatter_grads", [send_w_in, send_w_qb, send_w_kvb, send_w_out], gather=False)
    outs["w_in"] = _adamw_w_in(r_w_in, w0["w_in"], m0["w_in"], v0["w_in"])
    for k, parts in zip(SHARDED, (r_w_qb, r_w_kvb, r_w_out)):
        outs[k] = _adamw_blocks("adamw_" + k, parts, w0[k], m0[k], v0[k])

    def collect(idx):
        out = {k: o[idx] for k, o in outs.items()}
        out.update({k: (rep_g, rep_d, rep_m, rep_v)[idx][i] for i, k in enumerate(REPLICATED)})
        return [out[k][None] for k in WEIGHTS]

    return (loss, loc["grad_x"][None], *collect(0), *collect(1), *collect(2), *collect(3))
```

```python
import functools
import math

import jax
import jax.numpy as jnp
from jax import lax
from jax.experimental import pallas as pl
from jax.experimental.pallas import tpu as pltpu

F32 = jnp.float32
BF16 = jnp.bfloat16

N_DEV = 8
D_MODEL = 1024
MLA_HEADS = 8
QK_NOPE = 128
QK_ROPE = 64
V_DIM = 128
Q_RANK = 384
KV_RANK = 256
QK_HEAD = QK_NOPE + QK_ROPE
HEAD_PAD = 256
ROPE_HALF = QK_ROPE // 2
ROPE_THETA = 10000.0
MLA_WIDTH = MLA_HEADS * V_DIM
SSM_HEADS = 16
SSM_P = 64
SSM_WIDTH = SSM_HEADS * SSM_P
SSM_GROUPS = 2
SSM_N = 128
CONV_K = 4
CHUNK = 128
CONV_CH = SSM_WIDTH + 2 * SSM_GROUPS * SSM_N
MIX_WIDTH = MLA_WIDTH + SSM_WIDTH
IN_SPLITS = (Q_RANK, KV_RANK + QK_ROPE, MLA_WIDTH, CONV_CH, SSM_HEADS, SSM_WIDTH)
IN_WIDTH = sum(IN_SPLITS)
LANE = 128
KV_LAT_PAD = KV_RANK + LANE
IN_PAD = (Q_RANK, KV_LAT_PAD, MLA_WIDTH, CONV_CH, LANE, SSM_WIDTH)
IN_PAD_WIDTH = sum(IN_PAD)
DEEPNORM_ALPHA = 2.0 ** 0.25
RMS_EPS = 1e-6
LN_EPS = 1e-5
ATTN_SCALE = QK_HEAD ** -0.5
LOG2E = math.log2(math.e)
LN2 = math.log(2.0)
Q_PRESCALE = ATTN_SCALE * LOG2E
ADAM_LR, ADAM_B1, ADAM_B2, ADAM_EPS, ADAM_WD, ADAM_STEP = 0.001, 0.9, 0.999, 1e-08, 0.01, 10

ROW_TILE = 256
ATTN_TILE = 512
ATTN_UNROLLS = (8, 4, 2)
SSD_ROWS = 512
VMEM_LIMIT = 56 * 1024 * 1024


def _nn(a, b):
    return jnp.dot(a, b, preferred_element_type=F32)


def _nt(a, b):
    return lax.dot_general(a, b, (((1,), (1,)), ((), ())), preferred_element_type=F32)


def _tn(a, b):
    return lax.dot_general(a, b, (((0,), (0,)), ((), ())), preferred_element_type=F32)


def _cparams(*sem):
    return pltpu.CompilerParams(dimension_semantics=sem, vmem_limit_bytes=VMEM_LIMIT)


def _rows(tm, w):
    return pl.BlockSpec((tm, w), lambda i: (i, 0))


def _whole(shape):
    return pl.BlockSpec(shape, lambda i: (0,) * len(shape))


def _sigmoid(z):
    return 1.0 / (1.0 + jnp.exp(-z))


def _lane_iota(shape):
    return lax.broadcasted_iota(jnp.int32, shape, len(shape) - 1)


def _swap_halves(r):
    lane = _lane_iota(r.shape)
    return jnp.where(lane < ROPE_HALF, pltpu.roll(r, LANE - ROPE_HALF, 1),
                     jnp.where(lane < QK_ROPE, pltpu.roll(r, ROPE_HALF, 1), 0.0))


def _rope(r, cos, sin):
    return r * cos + _swap_halves(r) * sin


def _rope_transposed(d, cos, sin):
    return d * cos + _swap_halves(d * sin)


def _rms(x):
    rstd = lax.rsqrt(jnp.mean(x * x, axis=-1, keepdims=True) + RMS_EPS)
    return x * rstd, rstd


def _rms_bwd(dxhat, xhat, rstd):
    return rstd * (dxhat - xhat * jnp.mean(dxhat * xhat, axis=-1, keepdims=True))


def _acc_rows(ref, val):
    @pl.when(pl.program_id(0) == 0)
    def _():
        ref[...] = jnp.zeros_like(ref)
    ref[...] += val


def _colsum(v):
    return jnp.sum(v, axis=0, keepdims=True)


def _inproj(x, scale1p, shift, w_in_pt):
    s = x.shape[0]
    tm = ROW_TILE

    def body(x_ref, sc_ref, sh_ref, w_ref, u_ref, *outs):
        u = (x_ref[...] * sc_ref[...] + sh_ref[...]).astype(BF16)
        u_ref[...] = u
        proj = _nt(u, w_ref[...])
        off = 0
        for ref, w in zip(outs, IN_PAD):
            ref[...] = proj[:, off:off + w]
            off += w

    return pl.pallas_call(
        body, name="inproj", grid=(s // tm,),
        in_specs=[_rows(tm, D_MODEL), _whole((1, D_MODEL)), _whole((1, D_MODEL)), _whole((IN_PAD_WIDTH, D_MODEL))],
        out_specs=[_rows(tm, D_MODEL)] + [_rows(tm, w) for w in IN_PAD],
        out_shape=[jax.ShapeDtypeStruct((s, D_MODEL), BF16)] + [jax.ShapeDtypeStruct((s, w), F32) for w in IN_PAD],
        compiler_params=_cparams("parallel"),
    )(x, scale1p, shift, w_in_pt)


def _qpath(q_lat, g_q, w_qb_p, cos, sin):
    s = q_lat.shape[0]
    tm = ROW_TILE

    def body(ql_ref, g_ref, w_ref, cos_ref, sin_ref, nq_ref, q_ref):
        xhat, _ = _rms(ql_ref[...])
        nq = (xhat * g_ref[...]).astype(BF16)
        nq_ref[...] = nq
        raw = _nn(nq, w_ref[...]) * Q_PRESCALE
        c, sn = cos_ref[...], sin_ref[...]
        for h in range(MLA_HEADS):
            o = h * HEAD_PAD
            q_ref[:, o:o + QK_NOPE] = raw[:, o:o + QK_NOPE].astype(BF16)
            q_ref[:, o + QK_NOPE:o + HEAD_PAD] = _rope(raw[:, o + QK_NOPE:o + HEAD_PAD], c, sn).astype(BF16)

    return pl.pallas_call(
        body, name="qpath", grid=(s // tm,),
        in_specs=[_rows(tm, Q_RANK), _whole((1, Q_RANK)), _whole((Q_RANK, MLA_HEADS * HEAD_PAD)),
                  _rows(tm, LANE), _rows(tm, LANE)],
        out_specs=[_rows(tm, Q_RANK), _rows(tm, MLA_HEADS * HEAD_PAD)],
        out_shape=[jax.ShapeDtypeStruct((s, Q_RANK), BF16), jax.ShapeDtypeStruct((s, MLA_HEADS * HEAD_PAD), BF16)],
        compiler_params=_cparams("parallel"),
    )(q_lat, g_q, w_qb_p, cos, sin)


def _kvpath(kv_lat, g_kv, w_kvb_p, cos, sin):
    s = kv_lat.shape[0]
    tm = ROW_TILE

    def body(kl_ref, g_ref, w_ref, cos_ref, sin_ref, nkv_ref, k_ref, v_ref):
        kl = kl_ref[...]
        xhat, _ = _rms(kl[:, :KV_RANK])
        nkv = (xhat * g_ref[...]).astype(BF16)
        nkv_ref[...] = nkv
        raw = _nn(nkv, w_ref[...])
        kr = _rope(kl[:, KV_RANK:], cos_ref[...], sin_ref[...]).astype(BF16)
        for h in range(MLA_HEADS):
            o = h * HEAD_PAD
            k_ref[:, o:o + QK_NOPE] = raw[:, h * QK_NOPE:(h + 1) * QK_NOPE].astype(BF16)
            k_ref[:, o + QK_NOPE:o + HEAD_PAD] = kr
        v_ref[...] = raw[:, MLA_HEADS * QK_NOPE:].astype(BF16)

    return pl.pallas_call(
        body, name="kvpath", grid=(s // tm,),
        in_specs=[_rows(tm, KV_LAT_PAD), _whole((1, KV_RANK)), _whole((KV_RANK, MLA_HEADS * (QK_NOPE + V_DIM))),
                  _rows(tm, LANE), _rows(tm, LANE)],
        out_specs=[_rows(tm, KV_RANK), _rows(tm, MLA_HEADS * HEAD_PAD), _rows(tm, MLA_WIDTH)],
        out_shape=[jax.ShapeDtypeStruct((s, KV_RANK), BF16), jax.ShapeDtypeStruct((s, MLA_HEADS * HEAD_PAD), BF16),
                   jax.ShapeDtypeStruct((s, MLA_WIDTH), BF16)],
        compiler_params=_cparams("parallel"),
    )(kv_lat, g_kv, w_kvb_p, cos, sin)


def _causal_mask(t):
    row = lax.broadcasted_iota(jnp.int32, (t, t), 0)
    col = lax.broadcasted_iota(jnp.int32, (t, t), 1)
    return row, col


def _attn_fwd(q, k, v):
    s = q.shape[0]
    t = min(ATTN_TILE, s)
    nq = s // t

    def body(q_ref, k_ref, v_ref, o_ref, lse_ref, m_sc, l_sc, acc_sc, sa_sc, sb_sc):
        i = pl.program_id(1)
        qv = q_ref[...]
        m_sc[...] = jnp.full(m_sc.shape, -jnp.inf, F32)
        l_sc[...] = jnp.zeros(l_sc.shape, F32)
        acc_sc[...] = jnp.zeros(acc_sc.shape, F32)

        def scores(j, s_ref):
            s_ref[...] = _nt(qv, k_ref[pl.ds(pl.multiple_of(j * t, t), t), :])

        def update(s_ref, j, masked):
            vv = v_ref[pl.ds(pl.multiple_of(j * t, t), t), :]
            sc = s_ref[...]
            if masked:
                row, col = _causal_mask(t)
                sc = jnp.where(col <= row, sc, -jnp.inf)
            m_prev = m_sc[...]
            m_new = jnp.maximum(m_prev, jnp.max(sc, axis=1, keepdims=True))
            alpha = jnp.exp2(m_prev - m_new)
            p = jnp.exp2(sc - jnp.tile(m_new, (1, t // LANE)))
            l_sc[...] = alpha * l_sc[...] + jnp.sum(p, axis=1, keepdims=True)
            acc_sc[...] = alpha * acc_sc[...] + _nn(p.astype(BF16), vv)
            m_sc[...] = m_new

        def run(j0, count):
            bufs = (sa_sc, sb_sc)
            for u in range(count):
                scores(j0 + u + 1, bufs[(u + 1) % 2])
                update(bufs[u % 2], j0 + u, False)

        scores(0, sa_sc)
        done = 0
        for group in ATTN_UNROLLS:
            def body_(g, carry, base=done, group=group):
                run(base + group * g, group)
                return carry

            n_groups = lax.div(i - done, group)
            lax.fori_loop(0, n_groups, body_, 0)
            done = done + group * n_groups
        odd = lax.rem(i, 2)

        @pl.when(odd == 1)
        def _():
            scores(i, sb_sc)
            update(sa_sc, i - 1, False)
            update(sb_sc, i, True)

        @pl.when(odd == 0)
        def _():
            update(sa_sc, i, True)

        l = l_sc[...]
        o_ref[...] = acc_sc[...] / l
        lse_ref[0] = (m_sc[...] + jnp.log2(l)).T[0:1, :]

    return pl.pallas_call(
        body, name="attn_fwd", grid=(MLA_HEADS, nq),
        in_specs=[pl.BlockSpec((t, HEAD_PAD), lambda h, i: (i, h)),
                  pl.BlockSpec((s, HEAD_PAD), lambda h, i: (0, h)),
                  pl.BlockSpec((s, V_DIM), lambda h, i: (0, h))],
        out_specs=[pl.BlockSpec((t, V_DIM), lambda h, i: (i, h)), pl.BlockSpec((1, 1, t), lambda h, i: (h, 0, i))],
        out_shape=[jax.ShapeDtypeStruct((s, MLA_WIDTH), F32), jax.ShapeDtypeStruct((MLA_HEADS, 1, s), F32)],
        scratch_shapes=[pltpu.VMEM((t, LANE), F32), pltpu.VMEM((t, LANE), F32), pltpu.VMEM((t, V_DIM), F32),
                        pltpu.VMEM((t, t), F32), pltpu.VMEM((t, t), F32)],
        compiler_params=_cparams("parallel", "arbitrary"),
    )(q, k, v)


def _attn_bwd(q, k, v, do, lse_row, delta_row):
    s = q.shape[0]
    t = min(ATTN_TILE, s)
    nq = s // t

    def body(q_ref, k_ref, v_ref, do_ref, lse_ref, dl_ref, dk_ref, dv_ref, dq_hbm,
             dq_sc, dk_sc, dv_sc, sa_sc, sb_sc, pa_sc, pb_sc, sem):
        h = pl.program_id(0)
        j = pl.program_id(1)
        kv_ = k_ref[...]
        vv = v_ref[...]

        @pl.when(j == 0)
        def _():
            dq_sc[...] = jnp.zeros(dq_sc.shape, F32)

        dk_sc[...] = jnp.zeros(dk_sc.shape, F32)
        dv_sc[...] = jnp.zeros(dv_sc.shape, F32)

        def scores(i, s_ref, p_ref):
            off = pl.multiple_of(i * t, t)
            s_ref[...] = _nt(kv_, q_ref[pl.ds(off, t), :])
            p_ref[...] = _nt(vv, do_ref[pl.ds(off, t), :])

        def update(i, s_ref, p_ref, masked):
            off = pl.multiple_of(i * t, t)
            qv = q_ref[pl.ds(off, t), :]
            dov = do_ref[pl.ds(off, t), :]
            sct = s_ref[...]
            if masked:
                row, col = _causal_mask(t)
                sct = jnp.where(row <= col, sct, -jnp.inf)
            pt = jnp.exp2(sct - lse_ref[0, :, pl.ds(off, t)])
            gt = (pt * (p_ref[...] - dl_ref[0, :, pl.ds(off, t)])).astype(BF16)
            dv_sc[...] += _nn(pt.astype(BF16), dov)
            dk_sc[...] += _nn(gt, qv)
            dq_sc[pl.ds(off, t), :] += _tn(gt, kv_)

        rest = nq - 1 - j
        scores(j, sa_sc, pa_sc)

        @pl.when(rest >= 1)
        def _():
            scores(j + 1, sb_sc, pb_sc)

        update(j, sa_sc, pa_sc, True)

        def run(i0, count):
            bufs = ((sb_sc, pb_sc), (sa_sc, pa_sc))
            for u in range(count):
                scores(i0 + u + 1, *bufs[(u + 1) % 2])
                update(i0 + u, *bufs[u % 2], False)

        i1, left = j + 1, rest
        for group in ATTN_UNROLLS:
            def body_(g, carry, base=i1, group=group):
                run(base + group * g, group)
                return carry

            n_groups = jnp.where(left >= 1, lax.div(left - 1, group), 0)
            lax.fori_loop(0, n_groups, body_, 0)
            i1 = i1 + group * n_groups
            left = left - group * n_groups

        @pl.when(left == 1)
        def _():
            update(i1, sb_sc, pb_sc, False)

        @pl.when(left == 2)
        def _():
            scores(i1 + 1, sa_sc, pa_sc)
            update(i1, sb_sc, pb_sc, False)
            update(i1 + 1, sa_sc, pa_sc, False)

        dk_ref[...] = dk_sc[...] * LN2
        dv_ref[...] = dv_sc[...]

        @pl.when(j == nq - 1)
        def _():
            cp = pltpu.make_async_copy(dq_sc, dq_hbm.at[h], sem)
            cp.start()
            cp.wait()

    return pl.pallas_call(
        body, name="attn_bwd", grid=(MLA_HEADS, nq),
        in_specs=[pl.BlockSpec((s, HEAD_PAD), lambda h, j: (0, h)),
                  pl.BlockSpec((t, HEAD_PAD), lambda h, j: (j, h)),
                  pl.BlockSpec((t, V_DIM), lambda h, j: (j, h)),
                  pl.BlockSpec((s, V_DIM), lambda h, j: (0, h)),
                  pl.BlockSpec((1, 1, s), lambda h, j: (h, 0, 0)),
                  pl.BlockSpec((1, 1, s), lambda h, j: (h, 0, 0))],
        out_specs=[pl.BlockSpec((t, HEAD_PAD), lambda h, j: (j, h)), pl.BlockSpec((t, V_DIM), lambda h, j: (j, h)),
                   pl.BlockSpec(memory_space=pl.ANY)],
        out_shape=[jax.ShapeDtypeStruct((s, MLA_HEADS * HEAD_PAD), F32), jax.ShapeDtypeStruct((s, MLA_WIDTH), F32),
                   jax.ShapeDtypeStruct((MLA_HEADS, s, HEAD_PAD), F32)],
        scratch_shapes=[pltpu.VMEM((s, HEAD_PAD), F32), pltpu.VMEM((t, HEAD_PAD), F32), pltpu.VMEM((t, V_DIM), F32),
                        pltpu.VMEM((t, t), F32), pltpu.VMEM((t, t), F32), pltpu.VMEM((t, t), F32),
                        pltpu.VMEM((t, t), F32), pltpu.SemaphoreType.DMA],
        compiler_params=_cparams("arbitrary", "arbitrary"),
    )(q, k, v, do, lse_row, delta_row)


HALO = 8


def _silu(z):
    return z * _sigmoid(z)


def _silu_grad(z):
    sg = _sigmoid(z)
    return sg * (1.0 + z * (1.0 - sg))


def _softplus(x):
    e = jnp.exp(-jnp.abs(x))
    small = e * (1.0 - e * (0.5 - e * (1.0 / 3.0)))
    return jnp.maximum(x, 0.0) + jnp.where(e < 1e-3, small, jnp.log(1.0 + e))


def _conv_taps(xe_ref, w, tm, first):
    acc = None
    for k in range(CONV_K):
        term = xe_ref[pl.ds(HALO + first - (CONV_K - 1) + k, tm), :] * w[k:k + 1, :]
        acc = term if acc is None else acc + term
    return acc


def _ssd_pre(xbc_raw, dt_raw, conv_w, conv_b, dt_bias_p):
    s = xbc_raw.shape[0]
    tm = ROW_TILE
    hb = tm // HALO

    def body(x_ref, prev_ref, dtr_ref, w_ref, b_ref, db_ref, act_ref, dt_ref, xe_sc):
        i = pl.program_id(0)
        xe_sc[pl.ds(0, HALO), :] = jnp.where(i > 0, prev_ref[...], 0.0)
        xe_sc[pl.ds(HALO, tm), :] = x_ref[...]
        pre = _conv_taps(xe_sc, w_ref[...], tm, 0) + b_ref[...]
        act_ref[...] = _silu(pre)
        dt_ref[...] = _softplus(dtr_ref[...] + db_ref[...])

    return pl.pallas_call(
        body, name="ssd_pre", grid=(s // tm,),
        in_specs=[_rows(tm, CONV_CH), pl.BlockSpec((HALO, CONV_CH), lambda i: (jnp.maximum(i * hb - 1, 0), 0)),
                  _rows(tm, LANE), _whole((CONV_K, CONV_CH)), _whole((1, CONV_CH)), _whole((1, LANE))],
        out_specs=[_rows(tm, CONV_CH), _rows(tm, LANE)],
        out_shape=[jax.ShapeDtypeStruct((s, CONV_CH), F32), jax.ShapeDtypeStruct((s, LANE), F32)],
        scratch_shapes=[pltpu.VMEM((tm + HALO, CONV_CH), F32)],
        compiler_params=_cparams("parallel"),
    )(xbc_raw, xbc_raw, dt_raw, conv_w, conv_b, dt_bias_p)


def _split3(a):
    a1 = a.astype(BF16)
    r1 = a - a1.astype(F32)
    a2 = r1.astype(BF16)
    a3 = (r1 - a2.astype(F32)).astype(BF16)
    return a1, a2, a3


def _tri_left(tri, a):
    a1, a2, a3 = _split3(a)
    return _nn(tri, a1) + _nn(tri, a2) + _nn(tri, a3)


def _tri_right(a, tri):
    a1, a2, a3 = _split3(a)
    return _nn(a1, tri) + _nn(a2, tri) + _nn(a3, tri)


def _pair_sel(lane_lo, col_a, col_b):
    return jnp.where(lane_lo, col_a, col_b)


def _chunk_common(dt, a_neg, tril, triu):
    a = dt * a_neg
    lam_c = _tri_left(tril, a)
    lam_r = _tri_right(a.T, triu)
    lam_last = lam_c[CHUNK - 1:CHUNK, :]
    return lam_c, lam_r, lam_last


def _gated_norm_fwd(y, z, g):
    hf = y * _silu(z)
    outs = []
    for grp in range(SSM_GROUPS):
        w = SSM_WIDTH // SSM_GROUPS
        n, _ = _rms(hf[:, grp * w:(grp + 1) * w])
        outs.append(n)
    return jnp.concatenate(outs, axis=1) * g


def _ssd_fwd(xbc, dt, z, a_neg, dskip_x, g_x, tril, triu):
    s = xbc.shape[0]
    tm = min(SSD_ROWS, s)
    cpb = tm // CHUNK
    nc = s // CHUNK

    def body(xbc_ref, dt_ref, z_ref, a_ref, dsk_ref, g_ref, tril_ref, triu_ref, y_ref, o_ref, hin_ref, h_sc):
        @pl.when(pl.program_id(0) == 0)
        def _():
            h_sc[...] = jnp.zeros(h_sc.shape, F32)

        tril, triu = tril_ref[...], triu_ref[...]
        ltri = tril > 0
        lane_lo = _lane_iota((CHUNK, LANE)) < SSM_P
        lane_lo_n = lane_lo

        def chunk(c, carry):
            r0 = pl.multiple_of(c * CHUNK, CHUNK)
            dtc = dt_ref[pl.ds(r0, CHUNK), :]
            lam_c, lam_r, lam_last = _chunk_common(dtc, a_ref[...], tril, triu)
            e_c = jnp.exp(lam_c)
            f_r = jnp.exp(lam_r[:, CHUNK - 1:CHUNK] - lam_r)
            cd = jnp.exp(lam_last)
            for grp in range(SSM_GROUPS):
                bo = SSM_WIDTH + grp * SSM_N
                co = SSM_WIDTH + SSM_GROUPS * SSM_N + grp * SSM_N
                bm = xbc_ref[pl.ds(r0, CHUNK), bo:bo + SSM_N]
                cm = xbc_ref[pl.ds(r0, CHUNK), co:co + SSM_N]
                cm_b = cm.astype(BF16)
                gmat = _nt(cm_b, bm.astype(BF16))
                bt = bm.T
                for pj in range(SSM_HEADS // SSM_GROUPS // 2):
                    ha = grp * (SSM_HEADS // SSM_GROUPS) + 2 * pj
                    hb_ = ha + 1
                    lo = ha * SSM_P
                    xs = xbc_ref[pl.ds(r0, CHUNK), lo:lo + LANE]
                    x2 = xs * _pair_sel(lane_lo, dtc[:, ha:ha + 1], dtc[:, hb_:hb_ + 1])
                    x2b = x2.astype(BF16)
                    ys, sts = [], []
                    for hh in (ha, hb_):
                        seg = lam_c[:, hh:hh + 1] - lam_r[hh:hh + 1, :]
                        dec = jnp.exp(jnp.where(ltri, seg, -jnp.inf))
                        ys.append(_nn((gmat * dec).astype(BF16), x2b))
                        sts.append(_nn((bt * f_r[hh:hh + 1, :]).astype(BF16), x2b))
                    hp = h_sc[:, lo:lo + LANE]
                    hin_ref[c, :, lo:lo + LANE] = hp
                    zz = _nn(cm_b, hp.astype(BF16))
                    e2 = _pair_sel(lane_lo, e_c[:, ha:ha + 1], e_c[:, hb_:hb_ + 1])
                    yv = jnp.where(lane_lo, ys[0], ys[1]) + e2 * zz
                    y_ref[pl.ds(r0, CHUNK), lo:lo + LANE] = yv + xs * dsk_ref[:, lo:lo + LANE]
                    cd2 = _pair_sel(lane_lo_n, cd[:, ha:ha + 1], cd[:, hb_:hb_ + 1])
                    h_sc[:, lo:lo + LANE] = hp * cd2 + jnp.where(lane_lo_n, sts[0], sts[1])
            return carry

        lax.fori_loop(0, cpb, chunk, 0)
        o_ref[...] = _gated_norm_fwd(y_ref[...], z_ref[...], g_ref[...])

    return pl.pallas_call(
        body, name="ssd_fwd", grid=(s // tm,),
        in_specs=[_rows(tm, CONV_CH), _rows(tm, LANE), _rows(tm, SSM_WIDTH), _whole((1, LANE)),
                  _whole((1, SSM_WIDTH)), _whole((1, SSM_WIDTH)), _whole((CHUNK, CHUNK)), _whole((CHUNK, CHUNK))],
        out_specs=[_rows(tm, SSM_WIDTH), _rows(tm, SSM_WIDTH),
                   pl.BlockSpec((cpb, SSM_N, SSM_WIDTH), lambda i: (i, 0, 0))],
        out_shape=[jax.ShapeDtypeStruct((s, SSM_WIDTH), F32), jax.ShapeDtypeStruct((s, SSM_WIDTH), F32),
                   jax.ShapeDtypeStruct((nc, SSM_N, SSM_WIDTH), F32)],
        scratch_shapes=[pltpu.VMEM((SSM_N, SSM_WIDTH), F32)],
        compiler_params=_cparams("arbitrary"),
    )(xbc, dt, z, a_neg, dskip_x, g_x, tril, triu)


def _outln(o, z_attn, o_ssm, w_out, x, gate, ln_g, ln_b, tgt):
    s = x.shape[0]
    tm = ROW_TILE

    def body(o_ref, z_ref, os_ref, w_ref, x_ref, gate_ref, g_ref, b_ref, t_ref,
             cat_ref, dmix_ref, gx_ref, do_ref, dz_ref, dl_ref, dos_ref, loss_ref, dg_ref, db_ref, dgate_ref):
        ov, zv = o_ref[...], z_ref[...]
        sz = _silu(zv)
        cat_ref[:, :MLA_WIDTH] = (ov * sz).astype(BF16)
        cat_ref[:, MLA_WIDTH:] = os_ref[...].astype(BF16)
        w = w_ref[...]
        mixed = _nn(cat_ref[...], w)
        gate_v = gate_ref[...]
        hv = DEEPNORM_ALPHA * x_ref[...] + gate_v * mixed
        mu = jnp.mean(hv, axis=-1, keepdims=True)
        hc = hv - mu
        rstd = lax.rsqrt(jnp.mean(hc * hc, axis=-1, keepdims=True) + LN_EPS)
        xhat = hc * rstd
        g = g_ref[...]
        err = xhat * g + b_ref[...] - t_ref[...]
        _acc_rows(loss_ref, jnp.full((1, LANE), (0.5 / D_MODEL) * jnp.sum(err * err), F32))
        dy = err * (1.0 / D_MODEL)
        _acc_rows(dg_ref, _colsum(dy * xhat))
        _acc_rows(db_ref, _colsum(dy))
        dxhat = dy * g
        dh = rstd * (dxhat - jnp.mean(dxhat, axis=-1, keepdims=True)
                     - xhat * jnp.mean(dxhat * xhat, axis=-1, keepdims=True))
        gx_ref[...] = DEEPNORM_ALPHA * dh
        _acc_rows(dgate_ref, _colsum(dh * mixed))
        dmix = (gate_v * dh).astype(BF16)
        dmix_ref[...] = dmix
        dcat = _nt(dmix, w)
        da = dcat[:, :MLA_WIDTH]
        dos_ref[...] = dcat[:, MLA_WIDTH:]
        dov = da * sz
        do_ref[...] = dov.astype(BF16)
        dz_ref[...] = da * ov * _silu_grad(zv)
        prod = dov * ov
        for h in range(MLA_HEADS):
            dsum = jnp.sum(prod[:, h * V_DIM:(h + 1) * V_DIM], axis=1, keepdims=True)
            dl_ref[h] = jnp.broadcast_to(dsum, (tm, LANE)).T[0:1, :]

    vec = _whole((1, D_MODEL))
    return pl.pallas_call(
        body, name="outln", grid=(s // tm,),
        in_specs=[_rows(tm, MLA_WIDTH), _rows(tm, MLA_WIDTH), _rows(tm, SSM_WIDTH), _whole((MIX_WIDTH, D_MODEL)),
                  _rows(tm, D_MODEL), vec, vec, vec, _rows(tm, D_MODEL)],
        out_specs=[_rows(tm, MIX_WIDTH), _rows(tm, D_MODEL), _rows(tm, D_MODEL), _rows(tm, MLA_WIDTH),
                   _rows(tm, MLA_WIDTH), pl.BlockSpec((MLA_HEADS, 1, tm), lambda i: (0, 0, i)), _rows(tm, SSM_WIDTH),
                   _whole((1, LANE)), vec, vec, vec],
        out_shape=[jax.ShapeDtypeStruct((s, MIX_WIDTH), BF16), jax.ShapeDtypeStruct((s, D_MODEL), BF16),
                   jax.ShapeDtypeStruct((s, D_MODEL), F32), jax.ShapeDtypeStruct((s, MLA_WIDTH), BF16),
                   jax.ShapeDtypeStruct((s, MLA_WIDTH), F32), jax.ShapeDtypeStruct((MLA_HEADS, 1, s), F32),
                   jax.ShapeDtypeStruct((s, SSM_WIDTH), F32), jax.ShapeDtypeStruct((1, LANE), F32),
                   jax.ShapeDtypeStruct((1, D_MODEL), F32), jax.ShapeDtypeStruct((1, D_MODEL), F32),
                   jax.ShapeDtypeStruct((1, D_MODEL), F32)],
        compiler_params=_cparams("arbitrary"),
    )(o, z_attn, o_ssm, w_out, x, gate, ln_g, ln_b, tgt)


def _ssd_bwd(dos, y, z, xbc, dt, hin, a_neg, dskip_x, g_x, tril, triu):
    s = xbc.shape[0]
    tm = min(SSD_ROWS, s)
    cpb = tm // CHUNK
    nb = s // tm
    gw = SSM_WIDTH // SSM_GROUPS
    hpg = SSM_HEADS // SSM_GROUPS

    def body(dos_ref, y_ref, z_ref, xbc_ref, dt_ref, hin_ref, a_ref, dsk_ref, g_ref, tril_ref, triu_ref,
             dxbc_ref, ddt_ref, dz_ref, dg_ref, ddsk_ref, da_ref, dh_sc, dy_sc):
        @pl.when(pl.program_id(0) == 0)
        def _():
            dh_sc[...] = jnp.zeros(dh_sc.shape, F32)

        yv, zv, dov = y_ref[...], z_ref[...], dos_ref[...]
        sz = _silu(zv)
        hf = yv * sz
        gv = g_ref[...]
        dgs, dhfs = [], []
        for grp in range(SSM_GROUPS):
            sl = slice(grp * gw, (grp + 1) * gw)
            n, rstd = _rms(hf[:, sl])
            dgs.append(_colsum(dov[:, sl] * n))
            dhfs.append(_rms_bwd(dov[:, sl] * gv[:, sl], n, rstd))
        dhf = jnp.concatenate(dhfs, axis=1)
        _acc_rows(dg_ref, jnp.concatenate(dgs, axis=1))
        dy_sc[...] = dhf * sz
        dz_ref[...] = dhf * yv * _silu_grad(zv)

        tril, triu = tril_ref[...], triu_ref[...]
        ltri = tril > 0
        utri = triu > 0
        lane = _lane_iota((CHUNK, LANE))
        lane1 = _lane_iota((1, LANE))
        lane_lo = lane < SSM_P
        row_last = lax.broadcasted_iota(jnp.int32, (CHUNK, LANE), 0) == CHUNK - 1
        a_neg_v = a_ref[...]

        def chunk(ci, carry):
            dsk_acc, da_acc = carry
            cl = cpb - 1 - ci
            r0 = pl.multiple_of(cl * CHUNK, CHUNK)
            rows = pl.ds(r0, CHUNK)
            dtc = dt_ref[rows, :]
            lam_c, lam_r, lam_last = _chunk_common(dtc, a_neg_v, tril, triu)
            e_c = jnp.exp(lam_c)
            f_c = jnp.exp(lam_last - lam_c)
            cd = jnp.exp(lam_last)
            dlam = jnp.zeros((CHUNK, LANE), F32)
            dlast = jnp.zeros((1, LANE), F32)
            ddt_x = jnp.zeros((CHUNK, LANE), F32)
            dsk_parts = []
            for grp in range(SSM_GROUPS):
                bo = SSM_WIDTH + grp * SSM_N
                co = SSM_WIDTH + SSM_GROUPS * SSM_N + grp * SSM_N
                bm = xbc_ref[rows, bo:bo + SSM_N]
                cm = xbc_ref[rows, co:co + SSM_N]
                bm_b, cm_b = bm.astype(BF16), cm.astype(BF16)
                gmat = _nt(cm_b, bm_b)
                gmat_t = _nt(bm_b, cm_b)
                ct_b = cm.T.astype(BF16)
                acc_dg = jnp.zeros((CHUNK, CHUNK), F32)
                acc_dgt = jnp.zeros((CHUNK, CHUNK), F32)
                d_b = jnp.zeros((CHUNK, SSM_N), F32)
                d_c = jnp.zeros((CHUNK, SSM_N), F32)
                for pj in range(hpg // 2):
                    ha = grp * hpg + 2 * pj
                    hb_ = ha + 1
                    lo = ha * SSM_P
                    xs = xbc_ref[rows, lo:lo + LANE]
                    dt2 = _pair_sel(lane_lo, dtc[:, ha:ha + 1], dtc[:, hb_:hb_ + 1])
                    x2 = xs * dt2
                    x2b = x2.astype(BF16)
                    dy2 = dy_sc[rows, lo:lo + LANE]
                    dy2b = dy2.astype(BF16)
                    hp = hin_ref[cl, :, lo:lo + LANE]
                    hp_b = hp.astype(BF16)
                    dhn = dh_sc[:, lo:lo + LANE]
                    dhn_b = dhn.astype(BF16)
                    e2 = _pair_sel(lane_lo, e_c[:, ha:ha + 1], e_c[:, hb_:hb_ + 1])
                    yo = e2 * _nn(cm_b, hp_b)
                    dzz_b = (e2 * dy2).astype(BF16)
                    d_c = d_c + _nt(dzz_b, hp_b)
                    cd2 = _pair_sel(lane_lo, cd[:, ha:ha + 1], cd[:, hb_:hb_ + 1])
                    dh_sc[:, lo:lo + LANE] = _nn(ct_b, dzz_b) + cd2 * dhn
                    t_yo = dy2 * yo
                    t_hh = dhn * hp
                    dx2 = jnp.zeros((CHUNK, LANE), F32)
                    heads = ((ha, lane_lo), (hb_, jnp.logical_not(lane_lo)))
                    for hh, msk in heads:
                        x2h_b = jnp.where(msk, x2, 0.0).astype(BF16)
                        dy2h_b = jnp.where(msk, dy2, 0.0).astype(BF16)
                        lc = lam_c[:, hh:hh + 1]
                        lr = lam_r[hh:hh + 1, :]
                        dec = jnp.exp(jnp.where(ltri, lc - lr, -jnp.inf))
                        dect = jnp.exp(jnp.where(utri, lr - lc, -jnp.inf))
                        dmd = _nt(dy2h_b, x2b) * dec
                        dmtd = _nt(x2h_b, dy2b) * dect
                        acc_dg = acc_dg + dmd
                        acc_dgt = acc_dgt + dmtd
                        w_row = jnp.sum(dmd * gmat, axis=1, keepdims=True)
                        wt_row = jnp.sum(dmtd * gmat_t, axis=1, keepdims=True)
                        fcol = f_c[:, hh:hh + 1]
                        dx_h = _nn((gmat_t * dect).astype(BF16), dy2b) + _nn((bm * fcol).astype(BF16), dhn_b)
                        qh = _nt(x2h_b, dhn_b)
                        d_b = d_b + fcol * qh
                        dff = jnp.sum(bm * qh, axis=1, keepdims=True) * fcol
                        yo_row = jnp.sum(jnp.where(msk, t_yo, 0.0), axis=1, keepdims=True)
                        dlam_h = w_row - wt_row + yo_row - dff
                        hh_sum = jnp.sum(jnp.sum(jnp.where(msk, t_hh, 0.0), axis=1, keepdims=True), axis=0, keepdims=True)
                        last_h = cd[:, hh:hh + 1] * hh_sum + jnp.sum(dff, axis=0, keepdims=True)
                        dlam = jnp.where(lane == hh, dlam_h, dlam)
                        dlast = jnp.where(lane1 == hh, last_h, dlast)
                        dx2 = jnp.where(msk, dx_h, dx2)
                    dxbc_ref[rows, lo:lo + LANE] = dx2 * dt2 + dy2 * dsk_ref[:, lo:lo + LANE]
                    prod = dx2 * xs
                    for hh, msk in heads:
                        col = jnp.sum(jnp.where(msk, prod, 0.0), axis=1, keepdims=True)
                        ddt_x = jnp.where(lane == hh, col, ddt_x)
                    dsk_parts.append(_colsum(dy2 * xs))
                d_c = d_c + _nn(acc_dg.astype(BF16), bm_b)
                d_b = d_b + _nn(acc_dgt.astype(BF16), cm_b)
                dxbc_ref[rows, bo:bo + SSM_N] = d_b
                dxbc_ref[rows, co:co + SSM_N] = d_c
            dlam = dlam + jnp.where(row_last, dlast, 0.0)
            da = _tri_left(triu, dlam)
            ddt_ref[rows, :] = da * a_neg_v + ddt_x
            return dsk_acc + jnp.concatenate(dsk_parts, axis=1), da_acc + _colsum(da * dtc)

        dsk_tot, da_tot = lax.fori_loop(
            0, cpb, chunk, (jnp.zeros((1, SSM_WIDTH), F32), jnp.zeros((1, LANE), F32)))
        _acc_rows(ddsk_ref, dsk_tot)
        _acc_rows(da_ref, da_tot)

    rev = lambda i: (nb - 1 - i, 0)
    rrows = lambda w: pl.BlockSpec((tm, w), rev)
    return pl.pallas_call(
        body, name="ssd_bwd", grid=(nb,),
        in_specs=[rrows(SSM_WIDTH), rrows(SSM_WIDTH), rrows(SSM_WIDTH), rrows(CONV_CH), rrows(LANE),
                  pl.BlockSpec((cpb, SSM_N, SSM_WIDTH), lambda i: (nb - 1 - i, 0, 0)),
                  _whole((1, LANE)), _whole((1, SSM_WIDTH)), _whole((1, SSM_WIDTH)),
                  _whole((CHUNK, CHUNK)), _whole((CHUNK, CHUNK))],
        out_specs=[rrows(CONV_CH), rrows(LANE), rrows(SSM_WIDTH),
                   _whole((1, SSM_WIDTH)), _whole((1, SSM_WIDTH)), _whole((1, LANE))],
        out_shape=[jax.ShapeDtypeStruct((s, CONV_CH), F32), jax.ShapeDtypeStruct((s, LANE), F32),
                   jax.ShapeDtypeStruct((s, SSM_WIDTH), F32), jax.ShapeDtypeStruct((1, SSM_WIDTH), F32),
                   jax.ShapeDtypeStruct((1, SSM_WIDTH), F32), jax.ShapeDtypeStruct((1, LANE), F32)],
        scratch_shapes=[pltpu.VMEM((SSM_N, SSM_WIDTH), F32), pltpu.VMEM((tm, SSM_WIDTH), F32)],
        compiler_params=_cparams("arbitrary"),
    )(dos, y, z, xbc, dt, hin, a_neg, dskip_x, g_x, tril, triu)


def _ssd_post_bwd(xbc_raw, dxa, ddt, dt_raw, conv_w, conv_b, dt_bias_p):
    s = xbc_raw.shape[0]
    tm = ROW_TILE
    hb = tm // HALO
    nt = s // tm
    ext = tm + HALO

    def body(x_ref, prev_ref, next_ref, d_ref, dnext_ref, ddt_ref, dtr_ref, w_ref, b_ref, db_ref,
             dx_ref, ddtr_ref, dw_ref, dcb_ref, ddb_ref, xe_sc, de_sc):
        i = pl.program_id(0)
        w = w_ref[...]
        xe_sc[pl.ds(0, HALO), :] = jnp.where(i > 0, prev_ref[...], 0.0)
        xe_sc[pl.ds(HALO, tm), :] = x_ref[...]
        xe_sc[pl.ds(HALO + tm, HALO), :] = next_ref[...]
        pre = _conv_taps(xe_sc, w, ext, 0) + b_ref[...]
        sg = _silu_grad(pre)
        de_sc[pl.ds(0, tm), :] = d_ref[...] * sg[:tm]
        de_sc[pl.ds(tm, HALO), :] = jnp.where(i < nt - 1, dnext_ref[...] * sg[tm:], 0.0)
        dconv = de_sc[pl.ds(0, tm), :]
        acc = None
        dws = []
        for k in range(CONV_K):
            term = de_sc[pl.ds(CONV_K - 1 - k, tm), :] * w[k:k + 1, :]
            acc = term if acc is None else acc + term
            dws.append(_colsum(dconv * xe_sc[pl.ds(HALO - (CONV_K - 1) + k, tm), :]))
        dx_ref[...] = acc
        _acc_rows(dw_ref, jnp.concatenate(dws, axis=0))
        _acc_rows(dcb_ref, _colsum(dconv))
        ddtr = ddt_ref[...] * _sigmoid(dtr_ref[...] + db_ref[...])
        ddtr_ref[...] = ddtr
        _acc_rows(ddb_ref, _colsum(ddtr))

    halo_prev = pl.BlockSpec((HALO, CONV_CH), lambda i: (jnp.maximum(i * hb - 1, 0), 0))
    halo_next = pl.BlockSpec((HALO, CONV_CH), lambda i: (jnp.minimum((i + 1) * hb, s // HALO - 1), 0))
    return pl.pallas_call(
        body, name="ssd_post_bwd", grid=(nt,),
        in_specs=[_rows(tm, CONV_CH), halo_prev, halo_next, _rows(tm, CONV_CH), halo_next, _rows(tm, LANE),
                  _rows(tm, LANE), _whole((CONV_K, CONV_CH)), _whole((1, CONV_CH)), _whole((1, LANE))],
        out_specs=[_rows(tm, CONV_CH), _rows(tm, LANE), _whole((CONV_K, CONV_CH)), _whole((1, CONV_CH)),
                   _whole((1, LANE))],
        out_shape=[jax.ShapeDtypeStruct((s, CONV_CH), F32), jax.ShapeDtypeStruct((s, LANE), F32),
                   jax.ShapeDtypeStruct((CONV_K, CONV_CH), F32), jax.ShapeDtypeStruct((1, CONV_CH), F32),
                   jax.ShapeDtypeStruct((1, LANE), F32)],
        scratch_shapes=[pltpu.VMEM((tm + 2 * HALO, CONV_CH), F32), pltpu.VMEM((ext, CONV_CH), F32)],
        compiler_params=_cparams("arbitrary"),
    )(xbc_raw, xbc_raw, xbc_raw, dxa, dxa, ddt, dt_raw, conv_w, conv_b, dt_bias_p)


def _qbwd(dq_att, q_lat, g_q, w_qb_p, cos, sin):
    s = q_lat.shape[0]
    tm = ROW_TILE
    wq = MLA_HEADS * HEAD_PAD

    def body(dq_ref, ql_ref, g_ref, w_ref, cos_ref, sin_ref, dql_ref, draw_ref, dg_ref):
        c, sn = cos_ref[...], sin_ref[...]
        for h in range(MLA_HEADS):
            o = h * HEAD_PAD
            dqh = dq_ref[h] * ATTN_SCALE
            draw_ref[:, o:o + QK_NOPE] = dqh[:, :QK_NOPE].astype(BF16)
            draw_ref[:, o + QK_NOPE:o + HEAD_PAD] = _rope_transposed(dqh[:, QK_NOPE:], c, sn).astype(BF16)
        dn = _nt(draw_ref[...], w_ref[...])
        xhat, rstd = _rms(ql_ref[...])
        _acc_rows(dg_ref, _colsum(dn * xhat))
        dql_ref[...] = _rms_bwd(dn * g_ref[...], xhat, rstd)

    return pl.pallas_call(
        body, name="qbwd", grid=(s // tm,),
        in_specs=[pl.BlockSpec((MLA_HEADS, tm, HEAD_PAD), lambda i: (0, i, 0)), _rows(tm, Q_RANK), _whole((1, Q_RANK)),
                  _whole((Q_RANK, wq)), _rows(tm, LANE), _rows(tm, LANE)],
        out_specs=[_rows(tm, Q_RANK), _rows(tm, wq), _whole((1, Q_RANK))],
        out_shape=[jax.ShapeDtypeStruct((s, Q_RANK), F32), jax.ShapeDtypeStruct((s, wq), BF16),
                   jax.ShapeDtypeStruct((1, Q_RANK), F32)],
        compiler_params=_cparams("arbitrary"),
    )(dq_att, q_lat, g_q, w_qb_p, cos, sin)


def _kvbwd(dk_att, dv, kv_lat, g_kv, w_kvb_p, cos, sin):
    s = kv_lat.shape[0]
    tm = ROW_TILE
    wk = MLA_HEADS * HEAD_PAD
    wr = MLA_HEADS * (QK_NOPE + V_DIM)

    def body(dk_ref, dv_ref, kl_ref, g_ref, w_ref, cos_ref, sin_ref, dkl_ref, draw_ref, dg_ref):
        dkr = None
        for h in range(MLA_HEADS):
            o = h * HEAD_PAD
            draw_ref[:, h * QK_NOPE:(h + 1) * QK_NOPE] = dk_ref[:, o:o + QK_NOPE].astype(BF16)
            part = dk_ref[:, o + QK_NOPE:o + HEAD_PAD]
            dkr = part if dkr is None else dkr + part
        draw_ref[:, MLA_HEADS * QK_NOPE:] = dv_ref[...].astype(BF16)
        dn = _nt(draw_ref[...], w_ref[...])
        xhat, rstd = _rms(kl_ref[:, :KV_RANK])
        _acc_rows(dg_ref, _colsum(dn * xhat))
        dkl_ref[:, :KV_RANK] = _rms_bwd(dn * g_ref[...], xhat, rstd)
        dkl_ref[:, KV_RANK:] = _rope_transposed(dkr, cos_ref[...], sin_ref[...])

    return pl.pallas_call(
        body, name="kvbwd", grid=(s // tm,),
        in_specs=[_rows(tm, wk), _rows(tm, MLA_WIDTH), _rows(tm, KV_LAT_PAD), _whole((1, KV_RANK)),
                  _whole((KV_RANK, wr)), _rows(tm, LANE), _rows(tm, LANE)],
        out_specs=[_rows(tm, KV_LAT_PAD), _rows(tm, wr), _whole((1, KV_RANK))],
        out_shape=[jax.ShapeDtypeStruct((s, KV_LAT_PAD), F32), jax.ShapeDtypeStruct((s, wr), BF16),
                   jax.ShapeDtypeStruct((1, KV_RANK), F32)],
        compiler_params=_cparams("arbitrary"),
    )(dk_att, dv, kv_lat, g_kv, w_kvb_p, cos, sin)


def _inproj_bwd(pieces, w_in_pt, x, scale1p, gx1):
    s = x.shape[0]
    tm = ROW_TILE

    def body(*refs):
        p_refs = refs[:len(IN_PAD)]
        w_ref, x_ref, sc_ref, gx1_ref, gx_ref, dp_ref, dsc_ref, dsh_ref = refs[len(IN_PAD):]
        off = 0
        for ref, w in zip(p_refs, IN_PAD):
            dp_ref[:, off:off + w] = ref[...].astype(BF16)
            off += w
        du = _nn(dp_ref[...], w_ref[...])
        gx_ref[...] = gx1_ref[...] + du * sc_ref[...]
        _acc_rows(dsc_ref, _colsum(du * x_ref[...]))
        _acc_rows(dsh_ref, _colsum(du))

    vec = _whole((1, D_MODEL))
    return pl.pallas_call(
        body, name="inproj_bwd", grid=(s // tm,),
        in_specs=[_rows(tm, w) for w in IN_PAD] + [_whole((IN_PAD_WIDTH, D_MODEL)), _rows(tm, D_MODEL), vec,
                                                    _rows(tm, D_MODEL)],
        out_specs=[_rows(tm, D_MODEL), _rows(tm, IN_PAD_WIDTH), vec, vec],
        out_shape=[jax.ShapeDtypeStruct((s, D_MODEL), F32), jax.ShapeDtypeStruct((s, IN_PAD_WIDTH), BF16),
                   jax.ShapeDtypeStruct((1, D_MODEL), F32), jax.ShapeDtypeStruct((1, D_MODEL), F32)],
        compiler_params=_cparams("arbitrary"),
    )(*pieces, w_in_pt, x, scale1p, gx1)


def _matmul_tn_rows(name, a, b, tk):
    s, k = a.shape
    n = b.shape[1]
    tm = min(ATTN_TILE, s)

    def body(a_ref, b_ref, o_ref):
        @pl.when(pl.program_id(1) == 0)
        def _():
            o_ref[...] = jnp.zeros_like(o_ref)
        o_ref[...] += _tn(a_ref[...], b_ref[...])

    return pl.pallas_call(
        body, name=name, grid=(k // tk, s // tm),
        in_specs=[pl.BlockSpec((tm, tk), lambda j, i: (i, j)), pl.BlockSpec((tm, n), lambda j, i: (i, 0))],
        out_specs=pl.BlockSpec((tk, n), lambda j, i: (j, 0)),
        out_shape=jax.ShapeDtypeStruct((k, n), F32),
        compiler_params=_cparams("parallel", "arbitrary"),
    )(a, b)


def _matmul_tn(name, a, b, tn):
    s, k = a.shape
    n = b.shape[1]
    tm = min(ATTN_TILE, s)

    def body(a_ref, b_ref, o_ref):
        @pl.when(pl.program_id(1) == 0)
        def _():
            o_ref[...] = jnp.zeros_like(o_ref)
        o_ref[...] += _tn(a_ref[...], b_ref[...])

    return pl.pallas_call(
        body, name=name, grid=(n // tn, s // tm),
        in_specs=[pl.BlockSpec((tm, k), lambda j, i: (i, 0)), pl.BlockSpec((tm, tn), lambda j, i: (i, j))],
        out_specs=pl.BlockSpec((k, tn), lambda j, i: (0, j)),
        out_shape=jax.ShapeDtypeStruct((k, n), F32),
        compiler_params=_cparams("parallel", "arbitrary"),
    )(a, b)


def _pack_w_in_t(w_in_t):
    parts, off = [], 0
    for w, wp in zip(IN_SPLITS, IN_PAD):
        parts.append(jnp.pad(w_in_t[off:off + w], ((0, wp - w), (0, 0))))
        off += w
    return jnp.concatenate(parts, axis=0)


def _unpack_w_in_t(g):
    parts, off = [], 0
    for w, wp in zip(IN_SPLITS, IN_PAD):
        parts.append(g[off:off + w])
        off += wp
    return jnp.concatenate(parts, axis=0)


def _pack_w_qb(w_qb):
    w = w_qb.reshape(Q_RANK, MLA_HEADS, QK_HEAD)
    return jnp.pad(w, ((0, 0), (0, 0), (0, HEAD_PAD - QK_HEAD))).reshape(Q_RANK, MLA_HEADS * HEAD_PAD)


def _unpack_w_qb(g):
    return g.reshape(Q_RANK, MLA_HEADS, HEAD_PAD)[:, :, :QK_HEAD].reshape(Q_RANK, MLA_HEADS * QK_HEAD)


def _pack_w_kvb(w_kvb):
    w = w_kvb.reshape(KV_RANK, MLA_HEADS, QK_NOPE + V_DIM)
    return jnp.concatenate([w[:, :, :QK_NOPE].reshape(KV_RANK, -1), w[:, :, QK_NOPE:].reshape(KV_RANK, -1)], axis=1)


def _unpack_w_kvb(g):
    gk = g[:, :MLA_HEADS * QK_NOPE].reshape(KV_RANK, MLA_HEADS, QK_NOPE)
    gv = g[:, MLA_HEADS * QK_NOPE:].reshape(KV_RANK, MLA_HEADS, V_DIM)
    return jnp.concatenate([gk, gv], axis=2).reshape(KV_RANK, -1)


def _rope_tables(positions):
    inv_freq = 1.0 / (ROPE_THETA ** (jnp.arange(ROPE_HALF, dtype=F32) / ROPE_HALF))
    ang = positions.astype(F32)[:, None] * inv_freq
    cos, sin = jnp.cos(ang), jnp.sin(ang)
    zeros = jnp.zeros((positions.shape[0], LANE - QK_ROPE), F32)
    return jnp.concatenate([cos, cos, zeros], axis=1), jnp.concatenate([-sin, sin, zeros], axis=1)


def _local_step(x, tgt, positions, mod, w_in_t, q_norm_g, w_qb_p, kv_norm_g, w_kvb_p, conv_w, conv_b, dt_bias,
                a_log, d_skip, ssm_norm_g, w_out_b, ln_g, ln_b):
    row = lambda v: v.reshape(1, -1)
    shift, scale, gate = mod[:D_MODEL], mod[D_MODEL:2 * D_MODEL], mod[2 * D_MODEL:]
    scale1p = row(1.0 + scale)
    w_in_p = _pack_w_in_t(w_in_t)
    cos, sin = _rope_tables(positions)
    a_neg = row(jnp.pad(-jnp.exp(a_log), (0, LANE - SSM_HEADS)))
    dskip_x = row(jnp.repeat(d_skip, SSM_P))
    dt_bias_p = row(jnp.pad(dt_bias, (0, LANE - SSM_HEADS)))
    tri = jnp.tril(jnp.ones((CHUNK, CHUNK), F32))
    tril, triu = tri.astype(BF16), tri.T.astype(BF16)

    u_bf, q_lat, kv_lat, z_attn, xbc_raw, dt_raw, z_ssm = _inproj(x, scale1p, row(shift), w_in_p)
    nq_bf, q_att = _qpath(q_lat, row(q_norm_g), w_qb_p, cos, sin)
    nkv_bf, k_att, v_att = _kvpath(kv_lat, row(kv_norm_g), w_kvb_p, cos, sin)
    o, lse_rows = _attn_fwd(q_att, k_att, v_att)
    xbc, dt = _ssd_pre(xbc_raw, dt_raw, conv_w, row(conv_b), dt_bias_p)
    y, o_ssm, hin = _ssd_fwd(xbc, dt, z_ssm, a_neg, dskip_x, row(ssm_norm_g), tril, triu)
    (cat_bf, dmix_bf, gx1, do_bf, dz_attn, delta_rows, dos, loss, d_ln_g, d_ln_b, d_gate) = _outln(
        o, z_attn, o_ssm, w_out_b, x, row(gate), row(ln_g), row(ln_b), tgt)

    g_w_out = _matmul_tn("gw_out", cat_bf, dmix_bf, 512)
    dk_att, dv, dq_att = _attn_bwd(q_att, k_att, v_att, do_bf, lse_rows, delta_rows)
    dq_lat, dqraw_bf, d_q_norm_g = _qbwd(dq_att, q_lat, row(q_norm_g), w_qb_p, cos, sin)
    dkv_lat, dkvraw_bf, d_kv_norm_g = _kvbwd(dk_att, dv, kv_lat, row(kv_norm_g), w_kvb_p, cos, sin)
    g_w_qb = _matmul_tn("gw_qb", nq_bf, dqraw_bf, MLA_HEADS * HEAD_PAD)
    g_w_kvb = _matmul_tn("gw_kvb", nkv_bf, dkvraw_bf, MLA_HEADS * (QK_NOPE + V_DIM))
    dxa, ddt, dz_ssm, d_ssm_g, ddsk_x, d_a = _ssd_bwd(dos, y, z_ssm, xbc, dt, hin, a_neg, dskip_x, row(ssm_norm_g),
                                                       tril, triu)
    dxbc_raw, ddt_raw, d_conv_w, d_conv_b, d_dt_bias = _ssd_post_bwd(xbc_raw, dxa, ddt, dt_raw, conv_w, row(conv_b),
                                                                     dt_bias_p)
    grad_x, dproj_bf, d_scale, d_shift = _inproj_bwd((dq_lat, dkv_lat, dz_attn, dxbc_raw, ddt_raw, dz_ssm),
                                                     w_in_p, x, scale1p, gx1)
    g_w_in_t = _unpack_w_in_t(_matmul_tn_rows("gw_in", dproj_bf, u_bf, 640))
    return dict(
        loss=loss[0, 0], grad_x=grad_x,
        dmod=jnp.concatenate([d_shift[0], d_scale[0], d_gate[0]]),
        w_in_t=g_w_in_t, q_norm_g=d_q_norm_g[0], w_qb=g_w_qb, kv_norm_g=d_kv_norm_g[0], w_kvb=g_w_kvb,
        conv_w=d_conv_w, conv_b=d_conv_b[0], dt_bias=d_dt_bias[0, :SSM_HEADS],
        a_log=d_a[0, :SSM_HEADS] * a_neg[0, :SSM_HEADS],
        d_skip=ddsk_x.reshape(SSM_HEADS, SSM_P).sum(axis=1), ssm_norm_g=d_ssm_g[0], w_out=g_w_out,
        ln_g=d_ln_g[0], ln_b=d_ln_b[0])


ADAM_ROWS = 512


def _my_index():
    return 4 * lax.axis_index("x") + 2 * lax.axis_index("y") + lax.axis_index("c")


def _exchange(name, sends, gather):
    n = len(sends)
    peers = N_DEV - 1

    def body(*refs):
        send_refs, recv_refs = refs[:n], refs[n:2 * n]
        send_sems, recv_sems, local_sems = refs[2 * n:]
        x, y, c = lax.axis_index("x"), lax.axis_index("y"), lax.axis_index("c")
        me = 4 * x + 2 * y + c

        def src(a, idx):
            return send_refs[a] if gather else send_refs[a].at[idx]

        owns = [pltpu.make_async_copy(src(a, me), recv_refs[a].at[me], local_sems.at[a]) for a in range(n)]
        for cp in owns:
            cp.start()
        copies = []
        for k in range(1, N_DEV):
            px, py, pc = x ^ ((k >> 2) & 1), y ^ ((k >> 1) & 1), c ^ (k & 1)
            peer = 4 * px + 2 * py + pc
            for a in range(n):
                copies.append(pltpu.make_async_remote_copy(
                    src_ref=src(a, peer), dst_ref=recv_refs[a].at[me],
                    send_sem=send_sems.at[a * peers + k - 1], recv_sem=recv_sems.at[a * peers + k - 1],
                    device_id=(px, py, pc), device_id_type=pl.DeviceIdType.MESH))
        for cp in copies:
            cp.start()
        for cp in copies:
            cp.wait()
        for cp in owns:
            cp.wait()

    block_shape = lambda a: a.shape if gather else a.shape[1:]
    return pl.pallas_call(
        body, name=name,
        in_specs=[pl.BlockSpec(memory_space=pl.ANY)] * n, out_specs=[pl.BlockSpec(memory_space=pl.ANY)] * n,
        out_shape=[jax.ShapeDtypeStruct((N_DEV, *block_shape(a)), a.dtype) for a in sends],
        scratch_shapes=[pltpu.SemaphoreType.DMA((n * peers,)), pltpu.SemaphoreType.DMA((n * peers,)),
                        pltpu.SemaphoreType.DMA((n,))],
    )(*sends)


def _gather_two_level(name, sends):
    n = len(sends)
    per = N_DEV - 1

    def body(*refs):
        send_refs, recv_refs = refs[:n], refs[n:2 * n]
        send_sems, recv_sems, local_sems = refs[2 * n:]
        x, y, c = lax.axis_index("x"), lax.axis_index("y"), lax.axis_index("c")
        sibling = (x, y, 1 - c)
        chips = [(1 - x, y), (x, 1 - y), (1 - x, 1 - y)]

        def idx(px, py, pc):
            return 4 * px + 2 * py + pc

        def copy(a, k, block, to, src=None):
            slot = recv_refs[a].at[idx(*block)]
            return pltpu.make_async_remote_copy(
                src_ref=slot if src is None else src, dst_ref=slot,
                send_sem=send_sems.at[a * per + k], recv_sem=recv_sems.at[a * per + k],
                device_id=to, device_id_type=pl.DeviceIdType.MESH)

        me = (x, y, c)
        owns = [pltpu.make_async_copy(send_refs[a], recv_refs[a].at[idx(*me)], local_sems.at[a]) for a in range(n)]
        for cp in owns:
            cp.start()
        first = [copy(a, 0, me, sibling, src=send_refs[a]) for a in range(n)]
        first += [copy(a, 1 + j, me, (*chip, c), src=send_refs[a]) for j, chip in enumerate(chips) for a in range(n)]
        for cp in first:
            cp.start()
        passed = []
        for j, chip in enumerate(chips):
            for a in range(n):
                copy(a, 1 + j, (*chip, c), me).wait_recv()
                fwd = copy(a, 4 + j, (*chip, c), sibling)
                fwd.start()
                passed.append(fwd)
        for a in range(n):
            copy(a, 0, sibling, me).wait_recv()
            for j, chip in enumerate(chips):
                copy(a, 4 + j, (*chip, 1 - c), me).wait_recv()
        for cp in first + passed:
            cp.wait_send()
        for cp in owns:
            cp.wait()

    return pl.pallas_call(
        body, name=name,
        in_specs=[pl.BlockSpec(memory_space=pl.ANY)] * n, out_specs=[pl.BlockSpec(memory_space=pl.ANY)] * n,
        out_shape=[jax.ShapeDtypeStruct((N_DEV, *a.shape), a.dtype) for a in sends],
        scratch_shapes=[pltpu.SemaphoreType.DMA((n * per,)), pltpu.SemaphoreType.DMA((n * per,)),
                        pltpu.SemaphoreType.DMA((n,))],
    )(*sends)


def _flat_rows(parts, row_multiple):
    flat = jnp.concatenate([p.reshape(-1) for p in parts])
    chunk = row_multiple * LANE
    total = -(-flat.shape[0] // chunk) * chunk
    return jnp.pad(flat, (0, total - flat.shape[0])).reshape(-1, LANE)


def _unflat(flat, shapes):
    flat = flat.reshape(-1)
    out, off = [], 0
    for shp in shapes:
        n = math.prod(shp)
        out.append(flat[off:off + n].reshape(shp))
        off += n
    return out


def _adam_update(g, w, m, v):
    m2 = ADAM_B1 * m + (1.0 - ADAM_B1) * g
    v2 = ADAM_B2 * v + (1.0 - ADAM_B2) * (g * g)
    m_hat = m2 / (1.0 - ADAM_B1 ** ADAM_STEP)
    v_hat = v2 / (1.0 - ADAM_B2 ** ADAM_STEP)
    delta = -ADAM_LR * (m_hat / (jnp.sqrt(v_hat) + ADAM_EPS) + ADAM_WD * w)
    return delta, m2, v2


def _adamw_summed(name, parts, w, m, v):
    r = w.shape[0]
    tr = min(ADAM_ROWS, r)

    def body(p_ref, w_ref, m_ref, v_ref, g_ref, d_ref, m2_ref, v2_ref):
        g = p_ref[0]
        for j in range(1, N_DEV):
            g = g + p_ref[j]
        g_ref[...] = g
        d_ref[...], m2_ref[...], v2_ref[...] = _adam_update(g, w_ref[...], m_ref[...], v_ref[...])

    rows = _rows(tr, LANE)
    return pl.pallas_call(
        body, name=name, grid=(r // tr,),
        in_specs=[pl.BlockSpec((N_DEV, tr, LANE), lambda i: (0, i, 0)), rows, rows, rows],
        out_specs=[rows] * 4, out_shape=[jax.ShapeDtypeStruct((r, LANE), F32)] * 4,
        compiler_params=_cparams("parallel"),
    )(parts, w, m, v)


def _modpart(c_all, w_ada, b_cols):
    def body(c_ref, w_ref, b_ref, o_ref):
        o_ref[...] = _nn(c_ref[...].astype(BF16), w_ref[...].astype(BF16)) + b_ref[...]

    return pl.pallas_call(
        body, name="modpart", out_shape=jax.ShapeDtypeStruct((N_DEV, w_ada.shape[1]), F32),
    )(c_all, w_ada, b_cols)


def _adamw_w_ada(c_all_t, dmod_cols, w, m, v):
    def body(c_ref, d_ref, w_ref, m_ref, v_ref, g_ref, dl_ref, m2_ref, v2_ref):
        g = c_ref[:, 0:1] * d_ref[0:1, :]
        for b in range(1, N_DEV):
            g = g + c_ref[:, b:b + 1] * d_ref[b:b + 1, :]
        g_ref[...] = g
        dl_ref[...], m2_ref[...], v2_ref[...] = _adam_update(g, w_ref[...], m_ref[...], v_ref[...])

    return pl.pallas_call(
        body, name="adamw_w_ada", out_shape=[jax.ShapeDtypeStruct(w.shape, F32)] * 4,
        compiler_params=pltpu.CompilerParams(vmem_limit_bytes=VMEM_LIMIT),
    )(c_all_t, dmod_cols, w, m, v)


W_IN_SHARD = IN_WIDTH // N_DEV
W_IN_SHARD_LANES = -(-W_IN_SHARD // LANE) * LANE
BF16_ROWS = 16
W_IN_SEND_ROWS = -(-W_IN_SHARD // BF16_ROWS) * BF16_ROWS


def _transpose_cast(w_pad):
    def body(w_ref, o_ref):
        o_ref[...] = w_ref[...].T.astype(BF16)

    return pl.pallas_call(
        body, name="w_in_transpose", out_shape=jax.ShapeDtypeStruct(w_pad.shape[::-1], BF16),
        compiler_params=pltpu.CompilerParams(vmem_limit_bytes=VMEM_LIMIT),
    )(w_pad)


def _adamw_w_in(parts, w, m, v):
    rows_t = parts.shape[1]
    d, cols = w.shape
    tb = ROW_TILE

    def body(p_ref, w_ref, m_ref, v_ref, g_ref, d_ref, m2_ref, v2_ref):
        gt = p_ref[0].astype(F32)
        for j in range(1, N_DEV):
            gt = gt + p_ref[j].astype(F32)
        gt = jnp.concatenate([gt, jnp.zeros((W_IN_SHARD_LANES - rows_t, tb), F32)], axis=0)
        g = gt.T[:, :cols]
        g_ref[...] = g
        d_ref[...], m2_ref[...], v2_ref[...] = _adam_update(g, w_ref[...], m_ref[...], v_ref[...])

    blk = _rows(tb, cols)
    return pl.pallas_call(
        body, name="adamw_w_in", grid=(d // tb,),
        in_specs=[pl.BlockSpec((N_DEV, rows_t, tb), lambda i: (0, 0, i)), blk, blk, blk],
        out_specs=[blk] * 4, out_shape=[jax.ShapeDtypeStruct(w.shape, F32)] * 4,
        compiler_params=_cparams("parallel"),
    )(parts, w, m, v)


SHARDED = ("w_qb", "w_kvb", "w_out")
REPLICATED = ("b_ada", "q_norm_g", "kv_norm_g", "conv_b", "dt_bias", "a_log", "d_skip", "ssm_norm_g", "ln_g", "ln_b")
WEIGHTS = ("w_ada", "b_ada", "w_in", "q_norm_g", "w_qb", "kv_norm_g", "w_kvb", "conv_w", "conv_b", "dt_bias",
           "a_log", "d_skip", "ssm_norm_g", "w_out", "ln_g", "ln_b")
HEAD_COLS = QK_NOPE + V_DIM


def _adamw_blocks(name, parts, w, m, v):
    r, c = w.shape
    tr = ROW_TILE if r % ROW_TILE == 0 else r

    def body(p_ref, w_ref, m_ref, v_ref, g_ref, d_ref, m2_ref, v2_ref):
        g = p_ref[0].astype(F32)
        for j in range(1, N_DEV):
            g = g + p_ref[j].astype(F32)
        g_ref[...] = g
        d_ref[...], m2_ref[...], v2_ref[...] = _adam_update(g, w_ref[...], m_ref[...], v_ref[...])

    blk = _rows(tr, c)
    return pl.pallas_call(
        body, name=name, grid=(r // tr,),
        in_specs=[pl.BlockSpec((N_DEV, tr, c), lambda i: (0, i, 0)), blk, blk, blk],
        out_specs=[blk] * 4, out_shape=[jax.ShapeDtypeStruct(w.shape, F32)] * 4,
        compiler_params=_cparams("parallel"),
    )(parts, w, m, v)


def kernel(x, c, positions, w_ada, b_ada, w_in, q_norm_g, w_qb, kv_norm_g, w_kvb, conv_w, conv_b, dt_bias, a_log, d_skip, ssm_norm_g, w_out, ln_g, ln_b, loss_target, m_w_ada, m_b_ada, m_w_in, m_q_norm_g, m_w_qb, m_kv_norm_g, m_w_kvb, m_conv_w, m_conv_b, m_dt_bias, m_a_log, m_d_skip, m_ssm_norm_g, m_w_out, m_ln_g, m_ln_b, v_w_ada, v_b_ada, v_w_in, v_q_norm_g, v_w_qb, v_kv_norm_g, v_w_kvb, v_conv_w, v_conv_b, v_dt_bias, v_a_log, v_d_skip, v_ssm_norm_g, v_w_out, v_ln_g, v_ln_b):
    given = dict(w_ada=w_ada, b_ada=b_ada, w_in=w_in, q_norm_g=q_norm_g, w_qb=w_qb, kv_norm_g=kv_norm_g, w_kvb=w_kvb,
                 conv_w=conv_w, conv_b=conv_b, dt_bias=dt_bias, a_log=a_log, d_skip=d_skip, ssm_norm_g=ssm_norm_g,
                 w_out=w_out, ln_g=ln_g, ln_b=ln_b)
    mom = dict(w_ada=m_w_ada, b_ada=m_b_ada, w_in=m_w_in, q_norm_g=m_q_norm_g, w_qb=m_w_qb, kv_norm_g=m_kv_norm_g,
               w_kvb=m_w_kvb, conv_w=m_conv_w, conv_b=m_conv_b, dt_bias=m_dt_bias, a_log=m_a_log, d_skip=m_d_skip,
               ssm_norm_g=m_ssm_norm_g, w_out=m_w_out, ln_g=m_ln_g, ln_b=m_ln_b)
    var = dict(w_ada=v_w_ada, b_ada=v_b_ada, w_in=v_w_in, q_norm_g=v_q_norm_g, w_qb=v_w_qb, kv_norm_g=v_kv_norm_g,
               w_kvb=v_w_kvb, conv_w=v_conv_w, conv_b=v_conv_b, dt_bias=v_dt_bias, a_log=v_a_log, d_skip=v_d_skip,
               ssm_norm_g=v_ssm_norm_g, w_out=v_w_out, ln_g=v_ln_g, ln_b=v_ln_b)
    w0 = {k: a[0] for k, a in given.items()}
    m0 = {k: a[0] for k, a in mom.items()}
    v0 = {k: a[0] for k, a in var.items()}
    me = _my_index()

    w_in_rows = _transpose_cast(jnp.pad(w0["w_in"], ((0, 0), (0, W_IN_SHARD_LANES - W_IN_SHARD))))
    g_w_in, g_w_qb, g_w_kvb, g_w_out, g_conv_w, c_all = _gather_two_level(
        "gather_weights", [w_in_rows] + [w0[k].astype(BF16) for k in SHARDED] + [w0["conv_w"], c])
    c_all = c_all.reshape(N_DEV, D_MODEL)
    w_in_t = g_w_in[:, :W_IN_SHARD, :].reshape(IN_WIDTH, D_MODEL)
    w_qb_p = jnp.pad(g_w_qb, ((0, 0), (0, 0), (0, HEAD_PAD - QK_HEAD))).transpose(1, 0, 2).reshape(Q_RANK, -1)
    w_kvb_p = g_w_kvb.reshape(N_DEV, KV_RANK, 2, QK_NOPE).transpose(1, 2, 0, 3).reshape(KV_RANK, -1)
    w_out_b = g_w_out.reshape(MIX_WIDTH, D_MODEL)
    conv_w_full = g_conv_w.transpose(1, 0, 2).reshape(CONV_K, CONV_CH)

    ada_cols = w0["w_ada"].shape[1]
    b_cols = lax.dynamic_slice(w0["b_ada"], (me * ada_cols,), (ada_cols,)).reshape(1, ada_cols)
    mod_all, = _exchange("gather_mod", [_modpart(c_all, w0["w_ada"], b_cols)], gather=True)
    mod = lax.dynamic_index_in_dim(mod_all, me, axis=1, keepdims=False).reshape(-1)

    loc = _local_step(x[0], loss_target[0], positions[0], mod, w_in_t, w0["q_norm_g"], w_qb_p,
                      w0["kv_norm_g"], w_kvb_p, conv_w_full, w0["conv_b"], w0["dt_bias"], w0["a_log"],
                      w0["d_skip"], w0["ssm_norm_g"], w_out_b, w0["ln_g"], w0["ln_b"])

    rep_shapes = [w0[k].shape for k in REPLICATED] + [(1,)]
    rep_local = [loc["dmod"]] + [loc[k] for k in REPLICATED[1:]] + [loc["loss"].reshape(1)]
    rep_parts, conv_parts = _exchange("gather_small", [_flat_rows(rep_local, HALO), loc["conv_w"]], gather=True)
    conv_cols = w0["conv_w"].shape[1]
    conv_mine = lax.dynamic_slice(conv_parts, (0, 0, me * conv_cols), (N_DEV, CONV_K, conv_cols))
    outs = {"conv_w": _adamw_blocks("adamw_conv_w", conv_mine, w0["conv_w"], m0["conv_w"], v0["conv_w"])}
    zero1 = jnp.zeros((1,), F32)
    rep = _adamw_summed("adamw_replicated", rep_parts,
                        _flat_rows([w0[k] for k in REPLICATED] + [zero1], HALO),
                        _flat_rows([m0[k] for k in REPLICATED] + [zero1], HALO),
                        _flat_rows([v0[k] for k in REPLICATED] + [zero1], HALO))
    rep_g, rep_d, rep_m, rep_v = [_unflat(a, rep_shapes) for a in rep]
    loss = rep_g[-1][0]

    dmod_all = rep_parts.reshape(N_DEV, -1)[:, :3 * D_MODEL]
    dmod_cols = lax.dynamic_slice(dmod_all, (0, me * ada_cols), (N_DEV, ada_cols))
    outs["w_ada"] = _adamw_w_ada(c_all.T, dmod_cols, w0["w_ada"], m0["w_ada"], v0["w_ada"])

    send_w_in = loc["w_in_t"].astype(BF16).reshape(N_DEV, W_IN_SHARD, D_MODEL)
    send_w_in = jnp.pad(send_w_in, ((0, 0), (0, W_IN_SEND_ROWS - W_IN_SHARD), (0, 0)))
    send_w_qb = loc["w_qb"].astype(BF16).reshape(Q_RANK, N_DEV, HEAD_PAD)[:, :, :QK_HEAD].transpose(1, 0, 2)
    send_w_kvb = loc["w_kvb"].astype(BF16).reshape(KV_RANK, 2, N_DEV, QK_NOPE).transpose(2, 0, 1, 3)
    send_w_kvb = send_w_kvb.reshape(N_DEV, KV_RANK, HEAD_COLS)
    send_w_out = loc["w_out"].astype(BF16).reshape(N_DEV, MIX_WIDTH // N_DEV, D_MODEL)
    r_w_in, r_w_qb, r_w_kvb, r_w_out = _exchange(
        "scatter_grads", [send_w_in, send_w_qb, send_w_kvb, send_w_out], gather=False)
    outs["w_in"] = _adamw_w_in(r_w_in, w0["w_in"], m0["w_in"], v0["w_in"])
    for k, parts in zip(SHARDED, (r_w_qb, r_w_kvb, r_w_out)):
        outs[k] = _adamw_blocks("adamw_" + k, parts, w0[k], m0[k], v0[k])

    def collect(idx):
        out = {k: o[idx] for k, o in outs.items()}
        out.update({k: (rep_g, rep_d, rep_m, rep_v)[idx][i] for i, k in enumerate(REPLICATED)})
        return [out[k][None] for k in WEIGHTS]

    return (loss, loc["grad_x"][None], *collect(0), *collect(1), *collect(2), *collect(3))
```

```python
import functools
import math

import jax
import jax.numpy as jnp
from jax import lax
from jax.experimental import pallas as pl
from jax.experimental.pallas import tpu as pltpu

F32 = jnp.float32
BF16 = jnp.bfloat16

N_DEV = 8
D_MODEL = 1024
MLA_HEADS = 8
QK_NOPE = 128
QK_ROPE = 64
V_DIM = 128
Q_RANK = 384
KV_RANK = 256
QK_HEAD = QK_NOPE + QK_ROPE
HEAD_PAD = 256
ROPE_HALF = QK_ROPE // 2
ROPE_THETA = 10000.0
MLA_WIDTH = MLA_HEADS * V_DIM
SSM_HEADS = 16
SSM_P = 64
SSM_WIDTH = SSM_HEADS * SSM_P
SSM_GROUPS = 2
SSM_N = 128
CONV_K = 4
CHUNK = 128
CONV_CH = SSM_WIDTH + 2 * SSM_GROUPS * SSM_N
MIX_WIDTH = MLA_WIDTH + SSM_WIDTH
IN_SPLITS = (Q_RANK, KV_RANK + QK_ROPE, MLA_WIDTH, CONV_CH, SSM_HEADS, SSM_WIDTH)
IN_WIDTH = sum(IN_SPLITS)
LANE = 128
KV_LAT_PAD = KV_RANK + LANE
IN_PAD = (Q_RANK, KV_LAT_PAD, MLA_WIDTH, CONV_CH, LANE, SSM_WIDTH)
IN_PAD_WIDTH = sum(IN_PAD)
DEEPNORM_ALPHA = 2.0 ** 0.25
RMS_EPS = 1e-6
LN_EPS = 1e-5
ATTN_SCALE = QK_HEAD ** -0.5
LOG2E = math.log2(math.e)
LN2 = math.log(2.0)
Q_PRESCALE = ATTN_SCALE * LOG2E
ADAM_LR, ADAM_B1, ADAM_B2, ADAM_EPS, ADAM_WD, ADAM_STEP = 0.001, 0.9, 0.999, 1e-08, 0.01, 10

ROW_TILE = 256
ATTN_TILE = 512
ATTN_UNROLLS = (8, 4, 2)
SSD_ROWS = 512
VMEM_LIMIT = 56 * 1024 * 1024


def _nn(a, b):
    return jnp.dot(a, b, preferred_element_type=F32)


def _nt(a, b):
    return lax.dot_general(a, b, (((1,), (1,)), ((), ())), preferred_element_type=F32)


def _tn(a, b):
    return lax.dot_general(a, b, (((0,), (0,)), ((), ())), preferred_element_type=F32)


def _cparams(*sem):
    return pltpu.CompilerParams(dimension_semantics=sem, vmem_limit_bytes=VMEM_LIMIT)


def _rows(tm, w):
    return pl.BlockSpec((tm, w), lambda i: (i, 0))


def _whole(shape):
    return pl.BlockSpec(shape, lambda i: (0,) * len(shape))


def _sigmoid(z):
    return 1.0 / (1.0 + jnp.exp(-z))


def _lane_iota(shape):
    return lax.broadcasted_iota(jnp.int32, shape, len(shape) - 1)


def _swap_halves(r):
    lane = _lane_iota(r.shape)
    return jnp.where(lane < ROPE_HALF, pltpu.roll(r, LANE - ROPE_HALF, 1),
                     jnp.where(lane < QK_ROPE, pltpu.roll(r, ROPE_HALF, 1), 0.0))


def _rope(r, cos, sin):
    return r * cos + _swap_halves(r) * sin


def _rope_transposed(d, cos, sin):
    return d * cos + _swap_halves(d * sin)


def _rms(x):
    rstd = lax.rsqrt(jnp.mean(x * x, axis=-1, keepdims=True) + RMS_EPS)
    return x * rstd, rstd


def _rms_bwd(dxhat, xhat, rstd):
    return rstd * (dxhat - xhat * jnp.mean(dxhat * xhat, axis=-1, keepdims=True))


def _acc_rows(ref, val):
    @pl.when(pl.program_id(0) == 0)
    def _():
        ref[...] = jnp.zeros_like(ref)
    ref[...] += val


def _colsum(v):
    return jnp.sum(v, axis=0, keepdims=True)


def _inproj(x, scale1p, shift, w_in_pt):
    s = x.shape[0]
    tm = ROW_TILE

    def body(x_ref, sc_ref, sh_ref, w_ref, u_ref, *outs):
        u = (x_ref[...] * sc_ref[...] + sh_ref[...]).astype(BF16)
        u_ref[...] = u
        proj = _nt(u, w_ref[...])
        off = 0
        for ref, w in zip(outs, IN_PAD):
            ref[...] = proj[:, off:off + w]
            off += w

    return pl.pallas_call(
        body, name="inproj", grid=(s // tm,),
        in_specs=[_rows(tm, D_MODEL), _whole((1, D_MODEL)), _whole((1, D_MODEL)), _whole((IN_PAD_WIDTH, D_MODEL))],
        out_specs=[_rows(tm, D_MODEL)] + [_rows(tm, w) for w in IN_PAD],
        out_shape=[jax.ShapeDtypeStruct((s, D_MODEL), BF16)] + [jax.ShapeDtypeStruct((s, w), F32) for w in IN_PAD],
        compiler_params=_cparams("parallel"),
    )(x, scale1p, shift, w_in_pt)


def _qpath(q_lat, g_q, w_qb_p, cos, sin):
    s = q_lat.shape[0]
    tm = ROW_TILE

    def body(ql_ref, g_ref, w_ref, cos_ref, sin_ref, nq_ref, q_ref):
        xhat, _ = _rms(ql_ref[...])
        nq = (xhat * g_ref[...]).astype(BF16)
        nq_ref[...] = nq
        raw = _nn(nq, w_ref[...]) * Q_PRESCALE
        c, sn = cos_ref[...], sin_ref[...]
        for h in range(MLA_HEADS):
            o = h * HEAD_PAD
            q_ref[:, o:o + QK_NOPE] = raw[:, o:o + QK_NOPE].astype(BF16)
            q_ref[:, o + QK_NOPE:o + HEAD_PAD] = _rope(raw[:, o + QK_NOPE:o + HEAD_PAD], c, sn).astype(BF16)

    return pl.pallas_call(
        body, name="qpath", grid=(s // tm,),
        in_specs=[_rows(tm, Q_RANK), _whole((1, Q_RANK)), _whole((Q_RANK, MLA_HEADS * HEAD_PAD)),
                  _rows(tm, LANE), _rows(tm, LANE)],
        out_specs=[_rows(tm, Q_RANK), _rows(tm, MLA_HEADS * HEAD_PAD)],
        out_shape=[jax.ShapeDtypeStruct((s, Q_RANK), BF16), jax.ShapeDtypeStruct((s, MLA_HEADS * HEAD_PAD), BF16)],
        compiler_params=_cparams("parallel"),
    )(q_lat, g_q, w_qb_p, cos, sin)


def _kvpath(kv_lat, g_kv, w_kvb_p, cos, sin):
    s = kv_lat.shape[0]
    tm = ROW_TILE

    def body(kl_ref, g_ref, w_ref, cos_ref, sin_ref, nkv_ref, k_ref, v_ref):
        kl = kl_ref[...]
        xhat, _ = _rms(kl[:, :KV_RANK])
        nkv = (xhat * g_ref[...]).astype(BF16)
        nkv_ref[...] = nkv
        raw = _nn(nkv, w_ref[...])
        kr = _rope(kl[:, KV_RANK:], cos_ref[...], sin_ref[...]).astype(BF16)
        for h in range(MLA_HEADS):
            o = h * HEAD_PAD
            k_ref[:, o:o + QK_NOPE] = raw[:, h * QK_NOPE:(h + 1) * QK_NOPE].astype(BF16)
            k_ref[:, o + QK_NOPE:o + HEAD_PAD] = kr
        v_ref[...] = raw[:, MLA_HEADS * QK_NOPE:].astype(BF16)

    return pl.pallas_call(
        body, name="kvpath", grid=(s // tm,),
        in_specs=[_rows(tm, KV_LAT_PAD), _whole((1, KV_RANK)), _whole((KV_RANK, MLA_HEADS * (QK_NOPE + V_DIM))),
                  _rows(tm, LANE), _rows(tm, LANE)],
        out_specs=[_rows(tm, KV_RANK), _rows(tm, MLA_HEADS * HEAD_PAD), _rows(tm, MLA_WIDTH)],
        out_shape=[jax.ShapeDtypeStruct((s, KV_RANK), BF16), jax.ShapeDtypeStruct((s, MLA_HEADS * HEAD_PAD), BF16),
                   jax.ShapeDtypeStruct((s, MLA_WIDTH), BF16)],
        compiler_params=_cparams("parallel"),
    )(kv_lat, g_kv, w_kvb_p, cos, sin)


def _causal_mask(t):
    row = lax.broadcasted_iota(jnp.int32, (t, t), 0)
    col = lax.broadcasted_iota(jnp.int32, (t, t), 1)
    return row, col


def _attn_fwd(q, k, v):
    s = q.shape[0]
    t = min(ATTN_TILE, s)
    nq = s // t

    def body(q_ref, k_ref, v_ref, o_ref, lse_ref, m_sc, l_sc, acc_sc, sa_sc, sb_sc):
        i = pl.program_id(1)
        qv = q_ref[...]
        m_sc[...] = jnp.full(m_sc.shape, -jnp.inf, F32)
        l_sc[...] = jnp.zeros(l_sc.shape, F32)
        acc_sc[...] = jnp.zeros(acc_sc.shape, F32)

        def scores(j, s_ref):
            s_ref[...] = _nt(qv, k_ref[pl.ds(pl.multiple_of(j * t, t), t), :])

        def update(s_ref, j, masked):
            vv = v_ref[pl.ds(pl.multiple_of(j * t, t), t), :]
            sc = s_ref[...]
            if masked:
                row, col = _causal_mask(t)
                sc = jnp.where(col <= row, sc, -jnp.inf)
            m_prev = m_sc[...]
            m_new = jnp.maximum(m_prev, jnp.max(sc, axis=1, keepdims=True))
            alpha = jnp.exp2(m_prev - m_new)
            p = jnp.exp2(sc - jnp.tile(m_new, (1, t // LANE)))
            l_sc[...] = alpha * l_sc[...] + jnp.sum(p, axis=1, keepdims=True)
            acc_sc[...] = alpha * acc_sc[...] + _nn(p.astype(BF16), vv)
            m_sc[...] = m_new

        def run(j0, count):
            bufs = (sa_sc, sb_sc)
            for u in range(count):
                scores(j0 + u + 1, bufs[(u + 1) % 2])
                update(bufs[u % 2], j0 + u, False)

        scores(0, sa_sc)
        done = 0
        for group in ATTN_UNROLLS:
            def body_(g, carry, base=done, group=group):
                run(base + group * g, group)
                return carry

            n_groups = lax.div(i - done, group)
            lax.fori_loop(0, n_groups, body_, 0)
            done = done + group * n_groups
        odd = lax.rem(i, 2)

        @pl.when(odd == 1)
        def _():
            scores(i, sb_sc)
            update(sa_sc, i - 1, False)
            update(sb_sc, i, True)

        @pl.when(odd == 0)
        def _():
            update(sa_sc, i, True)

        l = l_sc[...]
        o_ref[...] = acc_sc[...] / l
        lse_ref[0] = (m_sc[...] + jnp.log2(l)).T[0:1, :]

    return pl.pallas_call(
        body, name="attn_fwd", grid=(MLA_HEADS, nq),
        in_specs=[pl.BlockSpec((t, HEAD_PAD), lambda h, i: (i, h)),
                  pl.BlockSpec((s, HEAD_PAD), lambda h, i: (0, h)),
                  pl.BlockSpec((s, V_DIM), lambda h, i: (0, h))],
        out_specs=[pl.BlockSpec((t, V_DIM), lambda h, i: (i, h)), pl.BlockSpec((1, 1, t), lambda h, i: (h, 0, i))],
        out_shape=[jax.ShapeDtypeStruct((s, MLA_WIDTH), F32), jax.ShapeDtypeStruct((MLA_HEADS, 1, s), F32)],
        scratch_shapes=[pltpu.VMEM((t, LANE), F32), pltpu.VMEM((t, LANE), F32), pltpu.VMEM((t, V_DIM), F32),
                        pltpu.VMEM((t, t), F32), pltpu.VMEM((t, t), F32)],
        compiler_params=_cparams("parallel", "arbitrary"),
    )(q, k, v)


def _attn_bwd(q, k, v, do, lse_row, delta_row):
    s = q.shape[0]
    t = min(ATTN_TILE, s)
    nq = s // t

    def body(q_ref, k_ref, v_ref, do_ref, lse_ref, dl_ref, dk_ref, dv_ref, dq_hbm,
             dq_sc, dk_sc, dv_sc, sa_sc, sb_sc, pa_sc, pb_sc, sem):
        h = pl.program_id(0)
        j = pl.program_id(1)
        kv_ = k_ref[...]
        vv = v_ref[...]

        @pl.when(j == 0)
        def _():
            dq_sc[...] = jnp.zeros(dq_sc.shape, F32)

        dk_sc[...] = jnp.zeros(dk_sc.shape, F32)
        dv_sc[...] = jnp.zeros(dv_sc.shape, F32)

        def scores(i, s_ref, p_ref):
            off = pl.multiple_of(i * t, t)
            s_ref[...] = _nt(kv_, q_ref[pl.ds(off, t), :])
            p_ref[...] = _nt(vv, do_ref[pl.ds(off, t), :])

        def update(i, s_ref, p_ref, masked):
            off = pl.multiple_of(i * t, t)
            qv = q_ref[pl.ds(off, t), :]
            dov = do_ref[pl.ds(off, t), :]
            sct = s_ref[...]
            if masked:
                row, col = _causal_mask(t)
                sct = jnp.where(row <= col, sct, -jnp.inf)
            pt = jnp.exp2(sct - lse_ref[0, :, pl.ds(off, t)])
            gt = (pt * (p_ref[...] - dl_ref[0, :, pl.ds(off, t)])).astype(BF16)
            dv_sc[...] += _nn(pt.astype(BF16), dov)
            dk_sc[...] += _nn(gt, qv)
            dq_sc[pl.ds(off, t), :] += _tn(gt, kv_)

        rest = nq - 1 - j
        scores(j, sa_sc, pa_sc)

        @pl.when(rest >= 1)
        def _():
            scores(j + 1, sb_sc, pb_sc)

        update(j, sa_sc, pa_sc, True)

        def run(i0, count):
            bufs = ((sb_sc, pb_sc), (sa_sc, pa_sc))
            for u in range(count):
                scores(i0 + u + 1, *bufs[(u + 1) % 2])
                update(i0 + u, *bufs[u % 2], False)

        i1, left = j + 1, rest
        for group in ATTN_UNROLLS:
            def body_(g, carry, base=i1, group=group):
                run(base + group * g, group)
                return carry

            n_groups = jnp.where(left >= 1, lax.div(left - 1, group), 0)
            lax.fori_loop(0, n_groups, body_, 0)
            i1 = i1 + group * n_groups
            left = left - group * n_groups

        @pl.when(left == 1)
        def _():
            update(i1, sb_sc, pb_sc, False)

        @pl.when(left == 2)
        def _():
            scores(i1 + 1, sa_sc, pa_sc)
            update(i1, sb_sc, pb_sc, False)
            update(i1 + 1, sa_sc, pa_sc, False)

        dk_ref[...] = dk_sc[...] * LN2
        dv_ref[...] = dv_sc[...]

        @pl.when(j == nq - 1)
        def _():
            cp = pltpu.make_async_copy(dq_sc, dq_hbm.at[h], sem)
            cp.start()
            cp.wait()

    return pl.pallas_call(
        body, name="attn_bwd", grid=(MLA_HEADS, nq),
        in_specs=[pl.BlockSpec((s, HEAD_PAD), lambda h, j: (0, h)),
                  pl.BlockSpec((t, HEAD_PAD), lambda h, j: (j, h)),
                  pl.BlockSpec((t, V_DIM), lambda h, j: (j, h)),
                  pl.BlockSpec((s, V_DIM), lambda h, j: (0, h)),
                  pl.BlockSpec((1, 1, s), lambda h, j: (h, 0, 0)),
                  pl.BlockSpec((1, 1, s), lambda h, j: (h, 0, 0))],
        out_specs=[pl.BlockSpec((t, HEAD_PAD), lambda h, j: (j, h)), pl.BlockSpec((t, V_DIM), lambda h, j: (j, h)),
                   pl.BlockSpec(memory_space=pl.ANY)],
        out_shape=[jax.ShapeDtypeStruct((s, MLA_HEADS * HEAD_PAD), F32), jax.ShapeDtypeStruct((s, MLA_WIDTH), F32),
                   jax.ShapeDtypeStruct((MLA_HEADS, s, HEAD_PAD), F32)],
        scratch_shapes=[pltpu.VMEM((s, HEAD_PAD), F32), pltpu.VMEM((t, HEAD_PAD), F32), pltpu.VMEM((t, V_DIM), F32),
                        pltpu.VMEM((t, t), F32), pltpu.VMEM((t, t), F32), pltpu.VMEM((t, t), F32),
                        pltpu.VMEM((t, t), F32), pltpu.SemaphoreType.DMA],
        compiler_params=_cparams("arbitrary", "arbitrary"),
    )(q, k, v, do, lse_row, delta_row)


HALO = 8


def _silu(z):
    return z * _sigmoid(z)


def _silu_grad(z):
    sg = _sigmoid(z)
    return sg * (1.0 + z * (1.0 - sg))


def _softplus(x):
    e = jnp.exp(-jnp.abs(x))
    small = e * (1.0 - e * (0.5 - e * (1.0 / 3.0)))
    return jnp.maximum(x, 0.0) + jnp.where(e < 1e-3, small, jnp.log(1.0 + e))


def _conv_taps(xe_ref, w, tm, first):
    acc = None
    for k in range(CONV_K):
        term = xe_ref[pl.ds(HALO + first - (CONV_K - 1) + k, tm), :] * w[k:k + 1, :]
        acc = term if acc is None else acc + term
    return acc


def _ssd_pre(xbc_raw, dt_raw, conv_w, conv_b, dt_bias_p):
    s = xbc_raw.shape[0]
    tm = ROW_TILE
    hb = tm // HALO

    def body(x_ref, prev_ref, dtr_ref, w_ref, b_ref, db_ref, act_ref, dt_ref, xe_sc):
        i = pl.program_id(0)
        xe_sc[pl.ds(0, HALO), :] = jnp.where(i > 0, prev_ref[...], 0.0)
        xe_sc[pl.ds(HALO, tm), :] = x_ref[...]
        pre = _conv_taps(xe_sc, w_ref[...], tm, 0) + b_ref[...]
        act_ref[...] = _silu(pre)
        dt_ref[...] = _softplus(dtr_ref[...] + db_ref[...])

    return pl.pallas_call(
        body, name="ssd_pre", grid=(s // tm,),
        in_specs=[_rows(tm, CONV_CH), pl.BlockSpec((HALO, CONV_CH), lambda i: (jnp.maximum(i * hb - 1, 0), 0)),
                  _rows(tm, LANE), _whole((CONV_K, CONV_CH)), _whole((1, CONV_CH)), _whole((1, LANE))],
        out_specs=[_rows(tm, CONV_CH), _rows(tm, LANE)],
        out_shape=[jax.ShapeDtypeStruct((s, CONV_CH), F32), jax.ShapeDtypeStruct((s, LANE), F32)],
        scratch_shapes=[pltpu.VMEM((tm + HALO, CONV_CH), F32)],
        compiler_params=_cparams("parallel"),
    )(xbc_raw, xbc_raw, dt_raw, conv_w, conv_b, dt_bias_p)


def _split3(a):
    a1 = a.astype(BF16)
    r1 = a - a1.astype(F32)
    a2 = r1.astype(BF16)
    a3 = (r1 - a2.astype(F32)).astype(BF16)
    return a1, a2, a3


def _tri_left(tri, a):
    a1, a2, a3 = _split3(a)
    return _nn(tri, a1) + _nn(tri, a2) + _nn(tri, a3)


def _tri_right(a, tri):
    a1, a2, a3 = _split3(a)
    return _nn(a1, tri) + _nn(a2, tri) + _nn(a3, tri)


def _pair_sel(lane_lo, col_a, col_b):
    return jnp.where(lane_lo, col_a, col_b)


def _chunk_common(dt, a_neg, tril, triu):
    a = dt * a_neg
    lam_c = _tri_left(tril, a)
    lam_r = _tri_right(a.T, triu)
    lam_last = lam_c[CHUNK - 1:CHUNK, :]
    return lam_c, lam_r, lam_last


def _gated_norm_fwd(y, z, g):
    hf = y * _silu(z)
    outs = []
    for grp in range(SSM_GROUPS):
        w = SSM_WIDTH // SSM_GROUPS
        n, _ = _rms(hf[:, grp * w:(grp + 1) * w])
        outs.append(n)
    return jnp.concatenate(outs, axis=1) * g


def _ssd_fwd(xbc, dt, z, a_neg, dskip_x, g_x, tril, triu):
    s = xbc.shape[0]
    tm = min(SSD_ROWS, s)
    cpb = tm // CHUNK
    nc = s // CHUNK

    def body(xbc_ref, dt_ref, z_ref, a_ref, dsk_ref, g_ref, tril_ref, triu_ref, y_ref, o_ref, hin_ref, h_sc):
        @pl.when(pl.program_id(0) == 0)
        def _():
            h_sc[...] = jnp.zeros(h_sc.shape, F32)

        tril, triu = tril_ref[...], triu_ref[...]
        ltri = tril > 0
        lane_lo = _lane_iota((CHUNK, LANE)) < SSM_P

        def chunk(c, carry):
            r0 = pl.multiple_of(c * CHUNK, CHUNK)
            dtc = dt_ref[pl.ds(r0, CHUNK), :]
            lam_c, lam_r, lam_last = _chunk_common(dtc, a_ref[...], tril, triu)
            e_c = jnp.exp(lam_c)
            f_r = jnp.exp(lam_r[:, CHUNK - 1:CHUNK] - lam_r)
            cd = jnp.exp(lam_last)
            for grp in range(SSM_GROUPS):
                bo = SSM_WIDTH + grp * SSM_N
                co = SSM_WIDTH + SSM_GROUPS * SSM_N + grp * SSM_N
                bm = xbc_ref[pl.ds(r0, CHUNK), bo:bo + SSM_N]
                cm = xbc_ref[pl.ds(r0, CHUNK), co:co + SSM_N]
                cm_b = cm.astype(BF16)
                gmat = _nt(cm_b, bm.astype(BF16))
                bt = bm.T
                for pj in range(SSM_HEADS // SSM_GROUPS // 2):
                    ha = grp * (SSM_HEADS // SSM_GROUPS) + 2 * pj
                    hb_ = ha + 1
                    lo = ha * SSM_P
                    xs = xbc_ref[pl.ds(r0, CHUNK), lo:lo + LANE]
                    x2 = xs * _pair_sel(lane_lo, dtc[:, ha:ha + 1], dtc[:, hb_:hb_ + 1])
                    x2b = x2.astype(BF16)
                    ys, sts = [], []
                    for hh in (ha, hb_):
                        seg = lam_c[:, hh:hh + 1] - lam_r[hh:hh + 1, :]
                        dec = jnp.exp(jnp.where(ltri, seg, -jnp.inf))
                        ys.append(_nn((gmat * dec).astype(BF16), x2b))
                        sts.append(_nn((bt * f_r[hh:hh + 1, :]).astype(BF16), x2b))
                    hp = h_sc[:, lo:lo + LANE]
                    hin_ref[c, :, lo:lo + LANE] = hp
                    zz = _nn(cm_b, hp.astype(BF16))
                    e2 = _pair_sel(lane_lo, e_c[:, ha:ha + 1], e_c[:, hb_:hb_ + 1])
                    yv = jnp.where(lane_lo, ys[0], ys[1]) + e2 * zz
                    y_ref[pl.ds(r0, CHUNK), lo:lo + LANE] = yv + xs * dsk_ref[:, lo:lo + LANE]
                    cd2 = _pair_sel(lane_lo, cd[:, ha:ha + 1], cd[:, hb_:hb_ + 1])
                    h_sc[:, lo:lo + LANE] = hp * cd2 + jnp.where(lane_lo, sts[0], sts[1])
            return carry

        lax.fori_loop(0, cpb, chunk, 0)
        o_ref[...] = _gated_norm_fwd(y_ref[...], z_ref[...], g_ref[...])

    return pl.pallas_call(
        body, name="ssd_fwd", grid=(s // tm,),
        in_specs=[_rows(tm, CONV_CH), _rows(tm, LANE), _rows(tm, SSM_WIDTH), _whole((1, LANE)),
                  _whole((1, SSM_WIDTH)), _whole((1, SSM_WIDTH)), _whole((CHUNK, CHUNK)), _whole((CHUNK, CHUNK))],
        out_specs=[_rows(tm, SSM_WIDTH), _rows(tm, SSM_WIDTH),
                   pl.BlockSpec((cpb, SSM_N, SSM_WIDTH), lambda i: (i, 0, 0))],
        out_shape=[jax.ShapeDtypeStruct((s, SSM_WIDTH), F32), jax.ShapeDtypeStruct((s, SSM_WIDTH), F32),
                   jax.ShapeDtypeStruct((nc, SSM_N, SSM_WIDTH), F32)],
        scratch_shapes=[pltpu.VMEM((SSM_N, SSM_WIDTH), F32)],
        compiler_params=_cparams("arbitrary"),
    )(xbc, dt, z, a_neg, dskip_x, g_x, tril, triu)


def _outln(o, z_attn, o_ssm, w_out, x, gate, ln_g, ln_b, tgt):
    s = x.shape[0]
    tm = ROW_TILE

    def body(o_ref, z_ref, os_ref, w_ref, x_ref, gate_ref, g_ref, b_ref, t_ref,
             cat_ref, dmix_ref, gx_ref, do_ref, dz_ref, dl_ref, dos_ref, loss_ref, dg_ref, db_ref, dgate_ref):
        ov, zv = o_ref[...], z_ref[...]
        sz = _silu(zv)
        cat_ref[:, :MLA_WIDTH] = (ov * sz).astype(BF16)
        cat_ref[:, MLA_WIDTH:] = os_ref[...].astype(BF16)
        w = w_ref[...]
        mixed = _nn(cat_ref[...], w)
        gate_v = gate_ref[...]
        hv = DEEPNORM_ALPHA * x_ref[...] + gate_v * mixed
        mu = jnp.mean(hv, axis=-1, keepdims=True)
        hc = hv - mu
        rstd = lax.rsqrt(jnp.mean(hc * hc, axis=-1, keepdims=True) + LN_EPS)
        xhat = hc * rstd
        g = g_ref[...]
        err = xhat * g + b_ref[...] - t_ref[...]
        _acc_rows(loss_ref, jnp.full((1, LANE), (0.5 / D_MODEL) * jnp.sum(err * err), F32))
        dy = err * (1.0 / D_MODEL)
        _acc_rows(dg_ref, _colsum(dy * xhat))
        _acc_rows(db_ref, _colsum(dy))
        dxhat = dy * g
        dh = rstd * (dxhat - jnp.mean(dxhat, axis=-1, keepdims=True)
                     - xhat * jnp.mean(dxhat * xhat, axis=-1, keepdims=True))
        gx_ref[...] = DEEPNORM_ALPHA * dh
        _acc_rows(dgate_ref, _colsum(dh * mixed))
        dmix = (gate_v * dh).astype(BF16)
        dmix_ref[...] = dmix
        dcat = _nt(dmix, w)
        da = dcat[:, :MLA_WIDTH]
        dos_ref[...] = dcat[:, MLA_WIDTH:]
        dov = da * sz
        do_ref[...] = dov.astype(BF16)
        dz_ref[...] = da * ov * _silu_grad(zv)
        prod = dov * ov
        for h in range(MLA_HEADS):
            dsum = jnp.sum(prod[:, h * V_DIM:(h + 1) * V_DIM], axis=1, keepdims=True)
            dl_ref[h] = jnp.broadcast_to(dsum, (tm, LANE)).T[0:1, :]

    vec = _whole((1, D_MODEL))
    return pl.pallas_call(
        body, name="outln", grid=(s // tm,),
        in_specs=[_rows(tm, MLA_WIDTH), _rows(tm, MLA_WIDTH), _rows(tm, SSM_WIDTH), _whole((MIX_WIDTH, D_MODEL)),
                  _rows(tm, D_MODEL), vec, vec, vec, _rows(tm, D_MODEL)],
        out_specs=[_rows(tm, MIX_WIDTH), _rows(tm, D_MODEL), _rows(tm, D_MODEL), _rows(tm, MLA_WIDTH),
                   _rows(tm, MLA_WIDTH), pl.BlockSpec((MLA_HEADS, 1, tm), lambda i: (0, 0, i)), _rows(tm, SSM_WIDTH),
                   _whole((1, LANE)), vec, vec, vec],
        out_shape=[jax.ShapeDtypeStruct((s, MIX_WIDTH), BF16), jax.ShapeDtypeStruct((s, D_MODEL), BF16),
                   jax.ShapeDtypeStruct((s, D_MODEL), F32), jax.ShapeDtypeStruct((s, MLA_WIDTH), BF16),
                   jax.ShapeDtypeStruct((s, MLA_WIDTH), F32), jax.ShapeDtypeStruct((MLA_HEADS, 1, s), F32),
                   jax.ShapeDtypeStruct((s, SSM_WIDTH), F32), jax.ShapeDtypeStruct((1, LANE), F32),
                   jax.ShapeDtypeStruct((1, D_MODEL), F32), jax.ShapeDtypeStruct((1, D_MODEL), F32),
                   jax.ShapeDtypeStruct((1, D_MODEL), F32)],
        compiler_params=_cparams("arbitrary"),
    )(o, z_attn, o_ssm, w_out, x, gate, ln_g, ln_b, tgt)


def _ssd_bwd(dos, y, z, xbc, dt, hin, a_neg, dskip_x, g_x, tril, triu, expand):
    s = xbc.shape[0]
    tm = min(SSD_ROWS, s)
    cpb = tm // CHUNK
    nb = s // tm
    gw = SSM_WIDTH // SSM_GROUPS
    hpg = SSM_HEADS // SSM_GROUPS

    def body(dos_ref, y_ref, z_ref, xbc_ref, dt_ref, hin_ref, a_ref, dsk_ref, g_ref, tril_ref, triu_ref, exp_ref,
             dxbc_ref, ddt_ref, dz_ref, dg_ref, ddsk_ref, da_ref, dh_sc, dy_sc):
        @pl.when(pl.program_id(0) == 0)
        def _():
            dh_sc[...] = jnp.zeros(dh_sc.shape, F32)

        yv, zv, dov = y_ref[...], z_ref[...], dos_ref[...]
        sz = _silu(zv)
        hf = yv * sz
        gv = g_ref[...]
        dgs, dhfs = [], []
        for grp in range(SSM_GROUPS):
            sl = slice(grp * gw, (grp + 1) * gw)
            n, rstd = _rms(hf[:, sl])
            dgs.append(_colsum(dov[:, sl] * n))
            dhfs.append(_rms_bwd(dov[:, sl] * gv[:, sl], n, rstd))
        dhf = jnp.concatenate(dhfs, axis=1)
        _acc_rows(dg_ref, jnp.concatenate(dgs, axis=1))
        dy_sc[...] = dhf * sz
        dz_ref[...] = dhf * yv * _silu_grad(zv)

        tril, triu, expand = tril_ref[...], triu_ref[...], exp_ref[...]
        ltri = tril > 0
        utri = triu > 0
        lane = _lane_iota((CHUNK, LANE))
        lane1 = _lane_iota((1, LANE))
        lane_lo = lane < SSM_P
        row_last = lax.broadcasted_iota(jnp.int32, (CHUNK, LANE), 0) == CHUNK - 1
        a_neg_v = a_ref[...]

        def chunk(ci, carry):
            dsk_acc, da_acc = carry
            cl = cpb - 1 - ci
            r0 = pl.multiple_of(cl * CHUNK, CHUNK)
            rows = pl.ds(r0, CHUNK)
            dtc = dt_ref[rows, :]
            lam_c, lam_r, lam_last = _chunk_common(dtc, a_neg_v, tril, triu)
            e_c = jnp.exp(lam_c)
            f_c = jnp.exp(lam_last - lam_c)
            cd = jnp.exp(lam_last)
            dt_x, e_x, f_x = _tri_right(dtc, expand), _tri_right(e_c, expand), _tri_right(f_c, expand)
            cd_x = _tri_right(jnp.broadcast_to(cd, (HALO, LANE)), expand)[0:1, :]
            dlam = jnp.zeros((CHUNK, LANE), F32)
            dlast = jnp.zeros((1, LANE), F32)
            ddt_x = jnp.zeros((CHUNK, LANE), F32)
            dsk_parts = []
            for grp in range(SSM_GROUPS):
                bo = SSM_WIDTH + grp * SSM_N
                co = SSM_WIDTH + SSM_GROUPS * SSM_N + grp * SSM_N
                bm = xbc_ref[rows, bo:bo + SSM_N]
                cm = xbc_ref[rows, co:co + SSM_N]
                bm_b, cm_b = bm.astype(BF16), cm.astype(BF16)
                gmat = _nt(cm_b, bm_b)
                gmat_t = _nt(bm_b, cm_b)
                ct_b = cm.T.astype(BF16)
                acc_dg = jnp.zeros((CHUNK, CHUNK), F32)
                acc_dgt = jnp.zeros((CHUNK, CHUNK), F32)
                d_b = jnp.zeros((CHUNK, SSM_N), F32)
                d_c = jnp.zeros((CHUNK, SSM_N), F32)
                for pj in range(hpg // 2):
                    ha = grp * hpg + 2 * pj
                    hb_ = ha + 1
                    lo = ha * SSM_P
                    blk = slice(lo, lo + LANE)
                    xs = xbc_ref[rows, blk]
                    dt2, e2, f2, cd2 = dt_x[:, blk], e_x[:, blk], f_x[:, blk], cd_x[:, blk]
                    x2 = xs * dt2
                    x2b = x2.astype(BF16)
                    dy2 = dy_sc[rows, blk]
                    dy2b = dy2.astype(BF16)
                    hp = hin_ref[cl, :, blk]
                    hp_b = hp.astype(BF16)
                    dhn = dh_sc[:, blk]
                    dhn_b = dhn.astype(BF16)
                    yo = e2 * _nn(cm_b, hp_b)
                    dzz_b = (e2 * dy2).astype(BF16)
                    d_c = d_c + _nt(dzz_b, hp_b)
                    dh_sc[:, blk] = _nn(ct_b, dzz_b) + cd2 * dhn
                    dxs2 = f2 * _nn(bm_b, dhn_b)
                    d_b = d_b + _nt((f2 * x2).astype(BF16), dhn_b)
                    xd = x2 * dxs2
                    t_lam = dy2 * yo - xd
                    t_last = cd2 * (dhn * hp) + xd
                    dxd2 = jnp.zeros((CHUNK, LANE), F32)
                    heads = ((ha, lane_lo), (hb_, jnp.logical_not(lane_lo)))
                    for hh, msk in heads:
                        x2h_b = jnp.where(msk, x2, 0.0).astype(BF16)
                        dy2h_b = jnp.where(msk, dy2, 0.0).astype(BF16)
                        seg = lam_c[:, hh:hh + 1] - lam_r[hh:hh + 1, :]
                        dec = jnp.exp(jnp.where(ltri, seg, -jnp.inf))
                        dect = jnp.exp(jnp.where(utri, -seg, -jnp.inf))
                        dmd = _nt(dy2h_b, x2b) * dec
                        dmtd = _nt(x2h_b, dy2b) * dect
                        acc_dg = acc_dg + dmd
                        acc_dgt = acc_dgt + dmtd
                        dlam_h = jnp.sum(dmd * gmat - dmtd * gmat_t + jnp.where(msk, t_lam, 0.0), axis=1, keepdims=True)
                        last_h = jnp.sum(jnp.sum(jnp.where(msk, t_last, 0.0), axis=0, keepdims=True), axis=1, keepdims=True)
                        dlam = jnp.where(lane == hh, dlam_h, dlam)
                        dlast = jnp.where(lane1 == hh, last_h, dlast)
                        dxd2 = jnp.where(msk, _nn((gmat_t * dect).astype(BF16), dy2b), dxd2)
                    dx2 = dxd2 + dxs2
                    dxbc_ref[rows, blk] = dx2 * dt2 + dy2 * dsk_ref[:, blk]
                    prod = dx2 * xs
                    for hh, msk in heads:
                        col = jnp.sum(jnp.where(msk, prod, 0.0), axis=1, keepdims=True)
                        ddt_x = jnp.where(lane == hh, col, ddt_x)
                    dsk_parts.append(_colsum(dy2 * xs))
                d_c = d_c + _nn(acc_dg.astype(BF16), bm_b)
                d_b = d_b + _nn(acc_dgt.astype(BF16), cm_b)
                dxbc_ref[rows, bo:bo + SSM_N] = d_b
                dxbc_ref[rows, co:co + SSM_N] = d_c
            dlam = dlam + jnp.where(row_last, dlast, 0.0)
            da = _tri_left(triu, dlam)
            ddt_ref[rows, :] = da * a_neg_v + ddt_x
            return dsk_acc + jnp.concatenate(dsk_parts, axis=1), da_acc + _colsum(da * dtc)

        dsk_tot, da_tot = lax.fori_loop(
            0, cpb, chunk, (jnp.zeros((1, SSM_WIDTH), F32), jnp.zeros((1, LANE), F32)))
        _acc_rows(ddsk_ref, dsk_tot)
        _acc_rows(da_ref, da_tot)

    rev = lambda i: (nb - 1 - i, 0)
    rrows = lambda w: pl.BlockSpec((tm, w), rev)
    return pl.pallas_call(
        body, name="ssd_bwd", grid=(nb,),
        in_specs=[rrows(SSM_WIDTH), rrows(SSM_WIDTH), rrows(SSM_WIDTH), rrows(CONV_CH), rrows(LANE),
                  pl.BlockSpec((cpb, SSM_N, SSM_WIDTH), lambda i: (nb - 1 - i, 0, 0)),
                  _whole((1, LANE)), _whole((1, SSM_WIDTH)), _whole((1, SSM_WIDTH)),
                  _whole((CHUNK, CHUNK)), _whole((CHUNK, CHUNK)), _whole((LANE, SSM_WIDTH))],
        out_specs=[rrows(CONV_CH), rrows(LANE), rrows(SSM_WIDTH),
                   _whole((1, SSM_WIDTH)), _whole((1, SSM_WIDTH)), _whole((1, LANE))],
        out_shape=[jax.ShapeDtypeStruct((s, CONV_CH), F32), jax.ShapeDtypeStruct((s, LANE), F32),
                   jax.ShapeDtypeStruct((s, SSM_WIDTH), F32), jax.ShapeDtypeStruct((1, SSM_WIDTH), F32),
                   jax.ShapeDtypeStruct((1, SSM_WIDTH), F32), jax.ShapeDtypeStruct((1, LANE), F32)],
        scratch_shapes=[pltpu.VMEM((SSM_N, SSM_WIDTH), F32), pltpu.VMEM((tm, SSM_WIDTH), F32)],
        compiler_params=_cparams("arbitrary"),
    )(dos, y, z, xbc, dt, hin, a_neg, dskip_x, g_x, tril, triu, expand)


def _ssd_post_bwd(xbc_raw, dxa, ddt, dt_raw, conv_w, conv_b, dt_bias_p):
    s = xbc_raw.shape[0]
    tm = ROW_TILE
    hb = tm // HALO
    nt = s // tm
    ext = tm + HALO

    def body(x_ref, prev_ref, next_ref, d_ref, dnext_ref, ddt_ref, dtr_ref, w_ref, b_ref, db_ref,
             dx_ref, ddtr_ref, dw_ref, dcb_ref, ddb_ref, xe_sc, de_sc):
        i = pl.program_id(0)
        w = w_ref[...]
        xe_sc[pl.ds(0, HALO), :] = jnp.where(i > 0, prev_ref[...], 0.0)
        xe_sc[pl.ds(HALO, tm), :] = x_ref[...]
        xe_sc[pl.ds(HALO + tm, HALO), :] = next_ref[...]
        pre = _conv_taps(xe_sc, w, ext, 0) + b_ref[...]
        sg = _silu_grad(pre)
        de_sc[pl.ds(0, tm), :] = d_ref[...] * sg[:tm]
        de_sc[pl.ds(tm, HALO), :] = jnp.where(i < nt - 1, dnext_ref[...] * sg[tm:], 0.0)
        dconv = de_sc[pl.ds(0, tm), :]
        acc = None
        dws = []
        for k in range(CONV_K):
            term = de_sc[pl.ds(CONV_K - 1 - k, tm), :] * w[k:k + 1, :]
            acc = term if acc is None else acc + term
            dws.append(_colsum(dconv * xe_sc[pl.ds(HALO - (CONV_K - 1) + k, tm), :]))
        dx_ref[...] = acc
        _acc_rows(dw_ref, jnp.concatenate(dws, axis=0))
        _acc_rows(dcb_ref, _colsum(dconv))
        ddtr = ddt_ref[...] * _sigmoid(dtr_ref[...] + db_ref[...])
        ddtr_ref[...] = ddtr
        _acc_rows(ddb_ref, _colsum(ddtr))

    halo_prev = pl.BlockSpec((HALO, CONV_CH), lambda i: (jnp.maximum(i * hb - 1, 0), 0))
    halo_next = pl.BlockSpec((HALO, CONV_CH), lambda i: (jnp.minimum((i + 1) * hb, s // HALO - 1), 0))
    return pl.pallas_call(
        body, name="ssd_post_bwd", grid=(nt,),
        in_specs=[_rows(tm, CONV_CH), halo_prev, halo_next, _rows(tm, CONV_CH), halo_next, _rows(tm, LANE),
                  _rows(tm, LANE), _whole((CONV_K, CONV_CH)), _whole((1, CONV_CH)), _whole((1, LANE))],
        out_specs=[_rows(tm, CONV_CH), _rows(tm, LANE), _whole((CONV_K, CONV_CH)), _whole((1, CONV_CH)),
                   _whole((1, LANE))],
        out_shape=[jax.ShapeDtypeStruct((s, CONV_CH), F32), jax.ShapeDtypeStruct((s, LANE), F32),
                   jax.ShapeDtypeStruct((CONV_K, CONV_CH), F32), jax.ShapeDtypeStruct((1, CONV_CH), F32),
                   jax.ShapeDtypeStruct((1, LANE), F32)],
        scratch_shapes=[pltpu.VMEM((tm + 2 * HALO, CONV_CH), F32), pltpu.VMEM((ext, CONV_CH), F32)],
        compiler_params=_cparams("arbitrary"),
    )(xbc_raw, xbc_raw, xbc_raw, dxa, dxa, ddt, dt_raw, conv_w, conv_b, dt_bias_p)


def _qbwd(dq_att, q_lat, g_q, w_qb_p, cos, sin):
    s = q_lat.shape[0]
    tm = ROW_TILE
    wq = MLA_HEADS * HEAD_PAD

    def body(dq_ref, ql_ref, g_ref, w_ref, cos_ref, sin_ref, dql_ref, draw_ref, dg_ref):
        c, sn = cos_ref[...], sin_ref[...]
        for h in range(MLA_HEADS):
            o = h * HEAD_PAD
            dqh = dq_ref[h] * ATTN_SCALE
            draw_ref[:, o:o + QK_NOPE] = dqh[:, :QK_NOPE].astype(BF16)
            draw_ref[:, o + QK_NOPE:o + HEAD_PAD] = _rope_transposed(dqh[:, QK_NOPE:], c, sn).astype(BF16)
        dn = _nt(draw_ref[...], w_ref[...])
        xhat, rstd = _rms(ql_ref[...])
        _acc_rows(dg_ref, _colsum(dn * xhat))
        dql_ref[...] = _rms_bwd(dn * g_ref[...], xhat, rstd)

    return pl.pallas_call(
        body, name="qbwd", grid=(s // tm,),
        in_specs=[pl.BlockSpec((MLA_HEADS, tm, HEAD_PAD), lambda i: (0, i, 0)), _rows(tm, Q_RANK), _whole((1, Q_RANK)),
                  _whole((Q_RANK, wq)), _rows(tm, LANE), _rows(tm, LANE)],
        out_specs=[_rows(tm, Q_RANK), _rows(tm, wq), _whole((1, Q_RANK))],
        out_shape=[jax.ShapeDtypeStruct((s, Q_RANK), F32), jax.ShapeDtypeStruct((s, wq), BF16),
                   jax.ShapeDtypeStruct((1, Q_RANK), F32)],
        compiler_params=_cparams("arbitrary"),
    )(dq_att, q_lat, g_q, w_qb_p, cos, sin)


def _kvbwd(dk_att, dv, kv_lat, g_kv, w_kvb_p, cos, sin):
    s = kv_lat.shape[0]
    tm = ROW_TILE
    wk = MLA_HEADS * HEAD_PAD
    wr = MLA_HEADS * (QK_NOPE + V_DIM)

    def body(dk_ref, dv_ref, kl_ref, g_ref, w_ref, cos_ref, sin_ref, dkl_ref, draw_ref, dg_ref):
        dkr = None
        for h in range(MLA_HEADS):
            o = h * HEAD_PAD
            draw_ref[:, h * QK_NOPE:(h + 1) * QK_NOPE] = dk_ref[:, o:o + QK_NOPE].astype(BF16)
            part = dk_ref[:, o + QK_NOPE:o + HEAD_PAD]
            dkr = part if dkr is None else dkr + part
        draw_ref[:, MLA_HEADS * QK_NOPE:] = dv_ref[...].astype(BF16)
        dn = _nt(draw_ref[...], w_ref[...])
        xhat, rstd = _rms(kl_ref[:, :KV_RANK])
        _acc_rows(dg_ref, _colsum(dn * xhat))
        dkl_ref[:, :KV_RANK] = _rms_bwd(dn * g_ref[...], xhat, rstd)
        dkl_ref[:, KV_RANK:] = _rope_transposed(dkr, cos_ref[...], sin_ref[...])

    return pl.pallas_call(
        body, name="kvbwd", grid=(s // tm,),
        in_specs=[_rows(tm, wk), _rows(tm, MLA_WIDTH), _rows(tm, KV_LAT_PAD), _whole((1, KV_RANK)),
                  _whole((KV_RANK, wr)), _rows(tm, LANE), _rows(tm, LANE)],
        out_specs=[_rows(tm, KV_LAT_PAD), _rows(tm, wr), _whole((1, KV_RANK))],
        out_shape=[jax.ShapeDtypeStruct((s, KV_LAT_PAD), F32), jax.ShapeDtypeStruct((s, wr), BF16),
                   jax.ShapeDtypeStruct((1, KV_RANK), F32)],
        compiler_params=_cparams("arbitrary"),
    )(dk_att, dv, kv_lat, g_kv, w_kvb_p, cos, sin)


def _inproj_bwd(pieces, w_in_pt, x, scale1p, gx1):
    s = x.shape[0]
    tm = ROW_TILE

    def body(*refs):
        p_refs = refs[:len(IN_PAD)]
        w_ref, x_ref, sc_ref, gx1_ref, gx_ref, dp_ref, dsc_ref, dsh_ref = refs[len(IN_PAD):]
        off = 0
        for ref, w in zip(p_refs, IN_PAD):
            dp_ref[:, off:off + w] = ref[...].astype(BF16)
            off += w
        du = _nn(dp_ref[...], w_ref[...])
        gx_ref[...] = gx1_ref[...] + du * sc_ref[...]
        _acc_rows(dsc_ref, _colsum(du * x_ref[...]))
        _acc_rows(dsh_ref, _colsum(du))

    vec = _whole((1, D_MODEL))
    return pl.pallas_call(
        body, name="inproj_bwd", grid=(s // tm,),
        in_specs=[_rows(tm, w) for w in IN_PAD] + [_whole((IN_PAD_WIDTH, D_MODEL)), _rows(tm, D_MODEL), vec,
                                                    _rows(tm, D_MODEL)],
        out_specs=[_rows(tm, D_MODEL), _rows(tm, IN_PAD_WIDTH), vec, vec],
        out_shape=[jax.ShapeDtypeStruct((s, D_MODEL), F32), jax.ShapeDtypeStruct((s, IN_PAD_WIDTH), BF16),
                   jax.ShapeDtypeStruct((1, D_MODEL), F32), jax.ShapeDtypeStruct((1, D_MODEL), F32)],
        compiler_params=_cparams("arbitrary"),
    )(*pieces, w_in_pt, x, scale1p, gx1)


def _matmul_tn_rows(name, a, b, tk):
    s, k = a.shape
    n = b.shape[1]
    tm = min(ATTN_TILE, s)

    def body(a_ref, b_ref, o_ref):
        @pl.when(pl.program_id(1) == 0)
        def _():
            o_ref[...] = jnp.zeros_like(o_ref)
        o_ref[...] += _tn(a_ref[...], b_ref[...])

    return pl.pallas_call(
        body, name=name, grid=(k // tk, s // tm),
        in_specs=[pl.BlockSpec((tm, tk), lambda j, i: (i, j)), pl.BlockSpec((tm, n), lambda j, i: (i, 0))],
        out_specs=pl.BlockSpec((tk, n), lambda j, i: (j, 0)),
        out_shape=jax.ShapeDtypeStruct((k, n), F32),
        compiler_params=_cparams("parallel", "arbitrary"),
    )(a, b)


def _matmul_tn(name, a, b, tn):
    s, k = a.shape
    n = b.shape[1]
    tm = min(ATTN_TILE, s)

    def body(a_ref, b_ref, o_ref):
        @pl.when(pl.program_id(1) == 0)
        def _():
            o_ref[...] = jnp.zeros_like(o_ref)
        o_ref[...] += _tn(a_ref[...], b_ref[...])

    return pl.pallas_call(
        body, name=name, grid=(n // tn, s // tm),
        in_specs=[pl.BlockSpec((tm, k), lambda j, i: (i, 0)), pl.BlockSpec((tm, tn), lambda j, i: (i, j))],
        out_specs=pl.BlockSpec((k, tn), lambda j, i: (0, j)),
        out_shape=jax.ShapeDtypeStruct((k, n), F32),
        compiler_params=_cparams("parallel", "arbitrary"),
    )(a, b)


def _pack_w_in_t(w_in_t):
    parts, off = [], 0
    for w, wp in zip(IN_SPLITS, IN_PAD):
        parts.append(jnp.pad(w_in_t[off:off + w], ((0, wp - w), (0, 0))))
        off += w
    return jnp.concatenate(parts, axis=0)


def _unpack_w_in_t(g):
    parts, off = [], 0
    for w, wp in zip(IN_SPLITS, IN_PAD):
        parts.append(g[off:off + w])
        off += wp
    return jnp.concatenate(parts, axis=0)


def _pack_w_qb(w_qb):
    w = w_qb.reshape(Q_RANK, MLA_HEADS, QK_HEAD)
    return jnp.pad(w, ((0, 0), (0, 0), (0, HEAD_PAD - QK_HEAD))).reshape(Q_RANK, MLA_HEADS * HEAD_PAD)


def _unpack_w_qb(g):
    return g.reshape(Q_RANK, MLA_HEADS, HEAD_PAD)[:, :, :QK_HEAD].reshape(Q_RANK, MLA_HEADS * QK_HEAD)


def _pack_w_kvb(w_kvb):
    w = w_kvb.reshape(KV_RANK, MLA_HEADS, QK_NOPE + V_DIM)
    return jnp.concatenate([w[:, :, :QK_NOPE].reshape(KV_RANK, -1), w[:, :, QK_NOPE:].reshape(KV_RANK, -1)], axis=1)


def _unpack_w_kvb(g):
    gk = g[:, :MLA_HEADS * QK_NOPE].reshape(KV_RANK, MLA_HEADS, QK_NOPE)
    gv = g[:, MLA_HEADS * QK_NOPE:].reshape(KV_RANK, MLA_HEADS, V_DIM)
    return jnp.concatenate([gk, gv], axis=2).reshape(KV_RANK, -1)


def _rope_tables(positions):
    inv_freq = 1.0 / (ROPE_THETA ** (jnp.arange(ROPE_HALF, dtype=F32) / ROPE_HALF))
    ang = positions.astype(F32)[:, None] * inv_freq
    cos, sin = jnp.cos(ang), jnp.sin(ang)
    zeros = jnp.zeros((positions.shape[0], LANE - QK_ROPE), F32)
    return jnp.concatenate([cos, cos, zeros], axis=1), jnp.concatenate([-sin, sin, zeros], axis=1)


def _local_step(x, tgt, positions, mod, w_in_t, q_norm_g, w_qb_p, kv_norm_g, w_kvb_p, conv_w, conv_b, dt_bias,
                a_log, d_skip, ssm_norm_g, w_out_b, ln_g, ln_b):
    row = lambda v: v.reshape(1, -1)
    shift, scale, gate = mod[:D_MODEL], mod[D_MODEL:2 * D_MODEL], mod[2 * D_MODEL:]
    scale1p = row(1.0 + scale)
    w_in_p = _pack_w_in_t(w_in_t)
    cos, sin = _rope_tables(positions)
    a_neg = row(jnp.pad(-jnp.exp(a_log), (0, LANE - SSM_HEADS)))
    dskip_x = row(jnp.repeat(d_skip, SSM_P))
    dt_bias_p = row(jnp.pad(dt_bias, (0, LANE - SSM_HEADS)))
    tri = jnp.tril(jnp.ones((CHUNK, CHUNK), F32))
    tril, triu = tri.astype(BF16), tri.T.astype(BF16)

    u_bf, q_lat, kv_lat, z_attn, xbc_raw, dt_raw, z_ssm = _inproj(x, scale1p, row(shift), w_in_p)
    nq_bf, q_att = _qpath(q_lat, row(q_norm_g), w_qb_p, cos, sin)
    nkv_bf, k_att, v_att = _kvpath(kv_lat, row(kv_norm_g), w_kvb_p, cos, sin)
    o, lse_rows = _attn_fwd(q_att, k_att, v_att)
    xbc, dt = _ssd_pre(xbc_raw, dt_raw, conv_w, row(conv_b), dt_bias_p)
    expand = jnp.repeat(jnp.eye(LANE, SSM_HEADS, dtype=BF16), SSM_P, axis=1)
    y, o_ssm, hin = _ssd_fwd(xbc, dt, z_ssm, a_neg, dskip_x, row(ssm_norm_g), tril, triu)
    (cat_bf, dmix_bf, gx1, do_bf, dz_attn, delta_rows, dos, loss, d_ln_g, d_ln_b, d_gate) = _outln(
        o, z_attn, o_ssm, w_out_b, x, row(gate), row(ln_g), row(ln_b), tgt)

    g_w_out = _matmul_tn("gw_out", cat_bf, dmix_bf, 512)
    dk_att, dv, dq_att = _attn_bwd(q_att, k_att, v_att, do_bf, lse_rows, delta_rows)
    dq_lat, dqraw_bf, d_q_norm_g = _qbwd(dq_att, q_lat, row(q_norm_g), w_qb_p, cos, sin)
    dkv_lat, dkvraw_bf, d_kv_norm_g = _kvbwd(dk_att, dv, kv_lat, row(kv_norm_g), w_kvb_p, cos, sin)
    g_w_qb = _matmul_tn("gw_qb", nq_bf, dqraw_bf, MLA_HEADS * HEAD_PAD)
    g_w_kvb = _matmul_tn("gw_kvb", nkv_bf, dkvraw_bf, MLA_HEADS * (QK_NOPE + V_DIM))
    dxa, ddt, dz_ssm, d_ssm_g, ddsk_x, d_a = _ssd_bwd(dos, y, z_ssm, xbc, dt, hin, a_neg, dskip_x, row(ssm_norm_g),
                                                       tril, triu, expand)
    dxbc_raw, ddt_raw, d_conv_w, d_conv_b, d_dt_bias = _ssd_post_bwd(xbc_raw, dxa, ddt, dt_raw, conv_w, row(conv_b),
                                                                     dt_bias_p)
    grad_x, dproj_bf, d_scale, d_shift = _inproj_bwd((dq_lat, dkv_lat, dz_attn, dxbc_raw, ddt_raw, dz_ssm),
                                                     w_in_p, x, scale1p, gx1)
    g_w_in_t = _unpack_w_in_t(_matmul_tn_rows("gw_in", dproj_bf, u_bf, 640))
    return dict(
        loss=loss[0, 0], grad_x=grad_x,
        dmod=jnp.concatenate([d_shift[0], d_scale[0], d_gate[0]]),
        w_in_t=g_w_in_t, q_norm_g=d_q_norm_g[0], w_qb=g_w_qb, kv_norm_g=d_kv_norm_g[0], w_kvb=g_w_kvb,
        conv_w=d_conv_w, conv_b=d_conv_b[0], dt_bias=d_dt_bias[0, :SSM_HEADS],
        a_log=d_a[0, :SSM_HEADS] * a_neg[0, :SSM_HEADS],
        d_skip=ddsk_x.reshape(SSM_HEADS, SSM_P).sum(axis=1), ssm_norm_g=d_ssm_g[0], w_out=g_w_out,
        ln_g=d_ln_g[0], ln_b=d_ln_b[0])


ADAM_ROWS = 512


def _my_index():
    return 4 * lax.axis_index("x") + 2 * lax.axis_index("y") + lax.axis_index("c")


def _exchange(name, sends, gather):
    n = len(sends)
    peers = N_DEV - 1

    def body(*refs):
        send_refs, recv_refs = refs[:n], refs[n:2 * n]
        send_sems, recv_sems, local_sems = refs[2 * n:]
        x, y, c = lax.axis_index("x"), lax.axis_index("y"), lax.axis_index("c")
        me = 4 * x + 2 * y + c

        def src(a, idx):
            return send_refs[a] if gather else send_refs[a].at[idx]

        owns = [pltpu.make_async_copy(src(a, me), recv_refs[a].at[me], local_sems.at[a]) for a in range(n)]
        for cp in owns:
            cp.start()
        copies = []
        for k in range(1, N_DEV):
            px, py, pc = x ^ ((k >> 2) & 1), y ^ ((k >> 1) & 1), c ^ (k & 1)
            peer = 4 * px + 2 * py + pc
            for a in range(n):
                copies.append(pltpu.make_async_remote_copy(
                    src_ref=src(a, peer), dst_ref=recv_refs[a].at[me],
                    send_sem=send_sems.at[a * peers + k - 1], recv_sem=recv_sems.at[a * peers + k - 1],
                    device_id=(px, py, pc), device_id_type=pl.DeviceIdType.MESH))
        for cp in copies:
            cp.start()
        for cp in copies:
            cp.wait()
        for cp in owns:
            cp.wait()

    block_shape = lambda a: a.shape if gather else a.shape[1:]
    return pl.pallas_call(
        body, name=name,
        in_specs=[pl.BlockSpec(memory_space=pl.ANY)] * n, out_specs=[pl.BlockSpec(memory_space=pl.ANY)] * n,
        out_shape=[jax.ShapeDtypeStruct((N_DEV, *block_shape(a)), a.dtype) for a in sends],
        scratch_shapes=[pltpu.SemaphoreType.DMA((n * peers,)), pltpu.SemaphoreType.DMA((n * peers,)),
                        pltpu.SemaphoreType.DMA((n,))],
    )(*sends)


def _gather_two_level(name, sends):
    n = len(sends)
    per = N_DEV - 1

    def body(*refs):
        send_refs, recv_refs = refs[:n], refs[n:2 * n]
        send_sems, recv_sems, local_sems = refs[2 * n:]
        x, y, c = lax.axis_index("x"), lax.axis_index("y"), lax.axis_index("c")
        sibling = (x, y, 1 - c)
        chips = [(1 - x, y), (x, 1 - y), (1 - x, 1 - y)]

        def idx(px, py, pc):
            return 4 * px + 2 * py + pc

        def copy(a, k, block, to, src=None):
            slot = recv_refs[a].at[idx(*block)]
            return pltpu.make_async_remote_copy(
                src_ref=slot if src is None else src, dst_ref=slot,
                send_sem=send_sems.at[a * per + k], recv_sem=recv_sems.at[a * per + k],
                device_id=to, device_id_type=pl.DeviceIdType.MESH)

        me = (x, y, c)
        owns = [pltpu.make_async_copy(send_refs[a], recv_refs[a].at[idx(*me)], local_sems.at[a]) for a in range(n)]
        for cp in owns:
            cp.start()
        first = [copy(a, 0, me, sibling, src=send_refs[a]) for a in range(n)]
        first += [copy(a, 1 + j, me, (*chip, c), src=send_refs[a]) for j, chip in enumerate(chips) for a in range(n)]
        for cp in first:
            cp.start()
        passed = []
        for j, chip in enumerate(chips):
            for a in range(n):
                copy(a, 1 + j, (*chip, c), me).wait_recv()
                fwd = copy(a, 4 + j, (*chip, c), sibling)
                fwd.start()
                passed.append(fwd)
        for a in range(n):
            copy(a, 0, sibling, me).wait_recv()
            for j, chip in enumerate(chips):
                copy(a, 4 + j, (*chip, 1 - c), me).wait_recv()
        for cp in first + passed:
            cp.wait_send()
        for cp in owns:
            cp.wait()

    return pl.pallas_call(
        body, name=name,
        in_specs=[pl.BlockSpec(memory_space=pl.ANY)] * n, out_specs=[pl.BlockSpec(memory_space=pl.ANY)] * n,
        out_shape=[jax.ShapeDtypeStruct((N_DEV, *a.shape), a.dtype) for a in sends],
        scratch_shapes=[pltpu.SemaphoreType.DMA((n * per,)), pltpu.SemaphoreType.DMA((n * per,)),
                        pltpu.SemaphoreType.DMA((n,))],
    )(*sends)


def _flat_rows(parts, row_multiple):
    flat = jnp.concatenate([p.reshape(-1) for p in parts])
    chunk = row_multiple * LANE
    total = -(-flat.shape[0] // chunk) * chunk
    return jnp.pad(flat, (0, total - flat.shape[0])).reshape(-1, LANE)


def _unflat(flat, shapes):
    flat = flat.reshape(-1)
    out, off = [], 0
    for shp in shapes:
        n = math.prod(shp)
        out.append(flat[off:off + n].reshape(shp))
        off += n
    return out


def _adam_update(g, w, m, v):
    m2 = ADAM_B1 * m + (1.0 - ADAM_B1) * g
    v2 = ADAM_B2 * v + (1.0 - ADAM_B2) * (g * g)
    m_hat = m2 / (1.0 - ADAM_B1 ** ADAM_STEP)
    v_hat = v2 / (1.0 - ADAM_B2 ** ADAM_STEP)
    delta = -ADAM_LR * (m_hat / (jnp.sqrt(v_hat) + ADAM_EPS) + ADAM_WD * w)
    return delta, m2, v2


def _adamw_summed(name, parts, w, m, v):
    r = w.shape[0]
    tr = min(ADAM_ROWS, r)

    def body(p_ref, w_ref, m_ref, v_ref, g_ref, d_ref, m2_ref, v2_ref):
        g = p_ref[0]
        for j in range(1, N_DEV):
            g = g + p_ref[j]
        g_ref[...] = g
        d_ref[...], m2_ref[...], v2_ref[...] = _adam_update(g, w_ref[...], m_ref[...], v_ref[...])

    rows = _rows(tr, LANE)
    return pl.pallas_call(
        body, name=name, grid=(r // tr,),
        in_specs=[pl.BlockSpec((N_DEV, tr, LANE), lambda i: (0, i, 0)), rows, rows, rows],
        out_specs=[rows] * 4, out_shape=[jax.ShapeDtypeStruct((r, LANE), F32)] * 4,
        compiler_params=_cparams("parallel"),
    )(parts, w, m, v)


def _modpart(c_all, w_ada, b_cols):
    def body(c_ref, w_ref, b_ref, o_ref):
        o_ref[...] = _nn(c_ref[...].astype(BF16), w_ref[...].astype(BF16)) + b_ref[...]

    return pl.pallas_call(
        body, name="modpart", out_shape=jax.ShapeDtypeStruct((N_DEV, w_ada.shape[1]), F32),
    )(c_all, w_ada, b_cols)


def _adamw_w_ada(c_all_t, dmod_cols, w, m, v):
    def body(c_ref, d_ref, w_ref, m_ref, v_ref, g_ref, dl_ref, m2_ref, v2_ref):
        g = c_ref[:, 0:1] * d_ref[0:1, :]
        for b in range(1, N_DEV):
            g = g + c_ref[:, b:b + 1] * d_ref[b:b + 1, :]
        g_ref[...] = g
        dl_ref[...], m2_ref[...], v2_ref[...] = _adam_update(g, w_ref[...], m_ref[...], v_ref[...])

    return pl.pallas_call(
        body, name="adamw_w_ada", out_shape=[jax.ShapeDtypeStruct(w.shape, F32)] * 4,
        compiler_params=pltpu.CompilerParams(vmem_limit_bytes=VMEM_LIMIT),
    )(c_all_t, dmod_cols, w, m, v)


W_IN_SHARD = IN_WIDTH // N_DEV
W_IN_SHARD_LANES = -(-W_IN_SHARD // LANE) * LANE
BF16_ROWS = 16
W_IN_SEND_ROWS = -(-W_IN_SHARD // BF16_ROWS) * BF16_ROWS


def _transpose_cast(w_pad):
    def body(w_ref, o_ref):
        o_ref[...] = w_ref[...].T.astype(BF16)

    return pl.pallas_call(
        body, name="w_in_transpose", out_shape=jax.ShapeDtypeStruct(w_pad.shape[::-1], BF16),
        compiler_params=pltpu.CompilerParams(vmem_limit_bytes=VMEM_LIMIT),
    )(w_pad)


def _adamw_w_in(parts, w, m, v):
    rows_t = parts.shape[1]
    d, cols = w.shape
    tb = ROW_TILE

    def body(p_ref, w_ref, m_ref, v_ref, g_ref, d_ref, m2_ref, v2_ref):
        gt = p_ref[0].astype(F32)
        for j in range(1, N_DEV):
            gt = gt + p_ref[j].astype(F32)
        gt = jnp.concatenate([gt, jnp.zeros((W_IN_SHARD_LANES - rows_t, tb), F32)], axis=0)
        g = gt.T[:, :cols]
        g_ref[...] = g
        d_ref[...], m2_ref[...], v2_ref[...] = _adam_update(g, w_ref[...], m_ref[...], v_ref[...])

    blk = _rows(tb, cols)
    return pl.pallas_call(
        body, name="adamw_w_in", grid=(d // tb,),
        in_specs=[pl.BlockSpec((N_DEV, rows_t, tb), lambda i: (0, 0, i)), blk, blk, blk],
        out_specs=[blk] * 4, out_shape=[jax.ShapeDtypeStruct(w.shape, F32)] * 4,
        compiler_params=_cparams("parallel"),
    )(parts, w, m, v)


SHARDED = ("w_qb", "w_kvb", "w_out")
REPLICATED = ("b_ada", "q_norm_g", "kv_norm_g", "conv_b", "dt_bias", "a_log", "d_skip", "ssm_norm_g", "ln_g", "ln_b")
WEIGHTS = ("w_ada", "b_ada", "w_in", "q_norm_g", "w_qb", "kv_norm_g", "w_kvb", "conv_w", "conv_b", "dt_bias",
           "a_log", "d_skip", "ssm_norm_g", "w_out", "ln_g", "ln_b")
HEAD_COLS = QK_NOPE + V_DIM


def _adamw_blocks(name, parts, w, m, v):
    r, c = w.shape
    tr = ROW_TILE if r % ROW_TILE == 0 else r

    def body(p_ref, w_ref, m_ref, v_ref, g_ref, d_ref, m2_ref, v2_ref):
        g = p_ref[0].astype(F32)
        for j in range(1, N_DEV):
            g = g + p_ref[j].astype(F32)
        g_ref[...] = g
        d_ref[...], m2_ref[...], v2_ref[...] = _adam_update(g, w_ref[...], m_ref[...], v_ref[...])

    blk = _rows(tr, c)
    return pl.pallas_call(
        body, name=name, grid=(r // tr,),
        in_specs=[pl.BlockSpec((N_DEV, tr, c), lambda i: (0, i, 0)), blk, blk, blk],
        out_specs=[blk] * 4, out_shape=[jax.ShapeDtypeStruct(w.shape, F32)] * 4,
        compiler_params=_cparams("parallel"),
    )(parts, w, m, v)


def kernel(x, c, positions, w_ada, b_ada, w_in, q_norm_g, w_qb, kv_norm_g, w_kvb, conv_w, conv_b, dt_bias, a_log, d_skip, ssm_norm_g, w_out, ln_g, ln_b, loss_target, m_w_ada, m_b_ada, m_w_in, m_q_norm_g, m_w_qb, m_kv_norm_g, m_w_kvb, m_conv_w, m_conv_b, m_dt_bias, m_a_log, m_d_skip, m_ssm_norm_g, m_w_out, m_ln_g, m_ln_b, v_w_ada, v_b_ada, v_w_in, v_q_norm_g, v_w_qb, v_kv_norm_g, v_w_kvb, v_conv_w, v_conv_b, v_dt_bias, v_a_log, v_d_skip, v_ssm_norm_g, v_w_out, v_ln_g, v_ln_b):
    given = dict(w_ada=w_ada, b_ada=b_ada, w_in=w_in, q_norm_g=q_norm_g, w_qb=w_qb, kv_norm_g=kv_norm_g, w_kvb=w_kvb,
                 conv_w=conv_w, conv_b=conv_b, dt_bias=dt_bias, a_log=a_log, d_skip=d_skip, ssm_norm_g=ssm_norm_g,
                 w_out=w_out, ln_g=ln_g, ln_b=ln_b)
    mom = dict(w_ada=m_w_ada, b_ada=m_b_ada, w_in=m_w_in, q_norm_g=m_q_norm_g, w_qb=m_w_qb, kv_norm_g=m_kv_norm_g,
               w_kvb=m_w_kvb, conv_w=m_conv_w, conv_b=m_conv_b, dt_bias=m_dt_bias, a_log=m_a_log, d_skip=m_d_skip,
               ssm_norm_g=m_ssm_norm_g, w_out=m_w_out, ln_g=m_ln_g, ln_b=m_ln_b)
    var = dict(w_ada=v_w_ada, b_ada=v_b_ada, w_in=v_w_in, q_norm_g=v_q_norm_g, w_qb=v_w_qb, kv_norm_g=v_kv_norm_g,
               w_kvb=v_w_kvb, conv_w=v_conv_w, conv_b=v_conv_b, dt_bias=v_dt_bias, a_log=v_a_log, d_skip=v_d_skip,
               ssm_norm_g=v_ssm_norm_g, w_out=v_w_out, ln_g=v_ln_g, ln_b=v_ln_b)
    w0 = {k: a[0] for k, a in given.items()}
    m0 = {k: a[0] for k, a in mom.items()}
    v0 = {k: a[0] for k, a in var.items()}
    me = _my_index()

    w_in_rows = _transpose_cast(jnp.pad(w0["w_in"], ((0, 0), (0, W_IN_SHARD_LANES - W_IN_SHARD))))
    g_w_in, g_w_qb, g_w_kvb, g_w_out, g_conv_w, c_all = _gather_two_level(
        "gather_weights", [w_in_rows] + [w0[k].astype(BF16) for k in SHARDED] + [w0["conv_w"], c])
    c_all = c_all.reshape(N_DEV, D_MODEL)
    w_in_t = g_w_in[:, :W_IN_SHARD, :].reshape(IN_WIDTH, D_MODEL)
    w_qb_p = jnp.pad(g_w_qb, ((0, 0), (0, 0), (0, HEAD_PAD - QK_HEAD))).transpose(1, 0, 2).reshape(Q_RANK, -1)
    w_kvb_p = g_w_kvb.reshape(N_DEV, KV_RANK, 2, QK_NOPE).transpose(1, 2, 0, 3).reshape(KV_RANK, -1)
    w_out_b = g_w_out.reshape(MIX_WIDTH, D_MODEL)
    conv_w_full = g_conv_w.transpose(1, 0, 2).reshape(CONV_K, CONV_CH)

    ada_cols = w0["w_ada"].shape[1]
    b_cols = lax.dynamic_slice(w0["b_ada"], (me * ada_cols,), (ada_cols,)).reshape(1, ada_cols)
    mod_all, = _exchange("gather_mod", [_modpart(c_all, w0["w_ada"], b_cols)], gather=True)
    mod = lax.dynamic_index_in_dim(mod_all, me, axis=1, keepdims=False).reshape(-1)

    loc = _local_step(x[0], loss_target[0], positions[0], mod, w_in_t, w0["q_norm_g"], w_qb_p,
                      w0["kv_norm_g"], w_kvb_p, conv_w_full, w0["conv_b"], w0["dt_bias"], w0["a_log"],
                      w0["d_skip"], w0["ssm_norm_g"], w_out_b, w0["ln_g"], w0["ln_b"])

    rep_shapes = [w0[k].shape for k in REPLICATED] + [(1,)]
    rep_local = [loc["dmod"]] + [loc[k] for k in REPLICATED[1:]] + [loc["loss"].reshape(1)]
    rep_parts, conv_parts = _exchange("gather_small", [_flat_rows(rep_local, HALO), loc["conv_w"]], gather=True)
    conv_cols = w0["conv_w"].shape[1]
    conv_mine = lax.dynamic_slice(conv_parts, (0, 0, me * conv_cols), (N_DEV, CONV_K, conv_cols))
    outs = {"conv_w": _adamw_blocks("adamw_conv_w", conv_mine, w0["conv_w"], m0["conv_w"], v0["conv_w"])}
    zero1 = jnp.zeros((1,), F32)
    rep = _adamw_summed("adamw_replicated", rep_parts,
                        _flat_rows([w0[k] for k in REPLICATED] + [zero1], HALO),
                        _flat_rows([m0[k] for k in REPLICATED] + [zero1], HALO),
                        _flat_rows([v0[k] for k in REPLICATED] + [zero1], HALO))
    rep_g, rep_d, rep_m, rep_v = [_unflat(a, rep_shapes) for a in rep]
    loss = rep_g[-1][0]

    dmod_all = rep_parts.reshape(N_DEV, -1)[:, :3 * D_MODEL]
    dmod_cols = lax.dynamic_slice(dmod_all, (0, me * ada_cols), (N_DEV, ada_cols))
    outs["w_ada"] = _adamw_w_ada(c_all.T, dmod_cols, w0["w_ada"], m0["w_ada"], v0["w_ada"])

    send_w_in = loc["w_in_t"].astype(BF16).reshape(N_DEV, W_IN_SHARD, D_MODEL)
    send_w_in = jnp.pad(send_w_in, ((0, 0), (0, W_IN_SEND_ROWS - W_IN_SHARD), (0, 0)))
    send_w_qb = loc["w_qb"].astype(BF16).reshape(Q_RANK, N_DEV, HEAD_PAD)[:, :, :QK_HEAD].transpose(1, 0, 2)
    send_w_kvb = loc["w_kvb"].astype(BF16).reshape(KV_RANK, 2, N_DEV, QK_NOPE).transpose(2, 0, 1, 3)
    send_w_kvb = send_w_kvb.reshape(N_DEV, KV_RANK, HEAD_COLS)
    send_w_out = loc["w_out"].astype(BF16).reshape(N_DEV, MIX_WIDTH // N_DEV, D_MODEL)
    r_w_in, r_w_qb, r_w_kvb, r_w_out = _exchange(
        "scatter_grads", [send_w_in, send_w_qb, send_w_kvb, send_w_out], gather=False)
    outs["w_in"] = _adamw_w_in(r_w_in, w0["w_in"], m0["w_in"], v0["w_in"])
    for k, parts in zip(SHARDED, (r_w_qb, r_w_kvb, r_w_out)):
        outs[k] = _adamw_blocks("adamw_" + k, parts, w0[k], m0[k], v0[k])

    def collect(idx):
        out = {k: o[idx] for k, o in outs.items()}
        out.update({k: (rep_g, rep_d, rep_m, rep_v)[idx][i] for i, k in enumerate(REPLICATED)})
        return [out[k][None] for k in WEIGHTS]

    return (loss, loc["grad_x"][None], *collect(0), *collect(1), *collect(2), *collect(3))
```

```python
import functools
import math

import jax
import jax.numpy as jnp
from jax import lax
from jax.experimental import pallas as pl
from jax.experimental.pallas import tpu as pltpu

F32 = jnp.float32
BF16 = jnp.bfloat16

N_DEV = 8
D_MODEL = 1024
MLA_HEADS = 8
QK_NOPE = 128
QK_ROPE = 64
V_DIM = 128
Q_RANK = 384
KV_RANK = 256
QK_HEAD = QK_NOPE + QK_ROPE
HEAD_PAD = 256
ROPE_HALF = QK_ROPE // 2
ROPE_THETA = 10000.0
MLA_WIDTH = MLA_HEADS * V_DIM
SSM_HEADS = 16
SSM_P = 64
SSM_WIDTH = SSM_HEADS * SSM_P
SSM_GROUPS = 2
SSM_N = 128
CONV_K = 4
CHUNK = 128
CONV_CH = SSM_WIDTH + 2 * SSM_GROUPS * SSM_N
MIX_WIDTH = MLA_WIDTH + SSM_WIDTH
IN_SPLITS = (Q_RANK, KV_RANK + QK_ROPE, MLA_WIDTH, CONV_CH, SSM_HEADS, SSM_WIDTH)
IN_WIDTH = sum(IN_SPLITS)
LANE = 128
KV_LAT_PAD = KV_RANK + LANE
IN_PAD = (Q_RANK, KV_LAT_PAD, MLA_WIDTH, CONV_CH, LANE, SSM_WIDTH)
IN_PAD_WIDTH = sum(IN_PAD)
DEEPNORM_ALPHA = 2.0 ** 0.25
RMS_EPS = 1e-6
LN_EPS = 1e-5
ATTN_SCALE = QK_HEAD ** -0.5
LOG2E = math.log2(math.e)
LN2 = math.log(2.0)
Q_PRESCALE = ATTN_SCALE * LOG2E
ADAM_LR, ADAM_B1, ADAM_B2, ADAM_EPS, ADAM_WD, ADAM_STEP = 0.001, 0.9, 0.999, 1e-08, 0.01, 10

ROW_TILE = 256
ATTN_TILE = 512
ATTN_UNROLLS = (8, 4, 2)
SSD_ROWS = 512
GRAD_ROWS = 2048
VMEM_LIMIT = 56 * 1024 * 1024


def _nn(a, b):
    return jnp.dot(a, b, preferred_element_type=F32)


def _nt(a, b):
    return lax.dot_general(a, b, (((1,), (1,)), ((), ())), preferred_element_type=F32)


def _tn(a, b):
    return lax.dot_general(a, b, (((0,), (0,)), ((), ())), preferred_element_type=F32)


def _cparams(*sem):
    return pltpu.CompilerParams(dimension_semantics=sem, vmem_limit_bytes=VMEM_LIMIT)


def _rows(tm, w):
    return pl.BlockSpec((tm, w), lambda i: (i, 0))


def _whole(shape):
    return pl.BlockSpec(shape, lambda i: (0,) * len(shape))


def _sigmoid(z):
    return 1.0 / (1.0 + jnp.exp(-z))


def _lane_iota(shape):
    return lax.broadcasted_iota(jnp.int32, shape, len(shape) - 1)


def _swap_halves(r):
    lane = _lane_iota(r.shape)
    return jnp.where(lane < ROPE_HALF, pltpu.roll(r, LANE - ROPE_HALF, 1),
                     jnp.where(lane < QK_ROPE, pltpu.roll(r, ROPE_HALF, 1), 0.0))


def _rope(r, cos, sin):
    return r * cos + _swap_halves(r) * sin


def _rope_transposed(d, cos, sin):
    return d * cos + _swap_halves(d * sin)


def _rms(x):
    rstd = lax.rsqrt(jnp.mean(x * x, axis=-1, keepdims=True) + RMS_EPS)
    return x * rstd, rstd


def _rms_bwd(dxhat, xhat, rstd):
    return rstd * (dxhat - xhat * jnp.mean(dxhat * xhat, axis=-1, keepdims=True))


def _acc_rows(ref, val):
    @pl.when(pl.program_id(0) == 0)
    def _():
        ref[...] = jnp.zeros_like(ref)
    ref[...] += val


def _colsum(v):
    return jnp.sum(v, axis=0, keepdims=True)


def _inproj(x, scale1p, shift, w_in_pt):
    s = x.shape[0]
    tm = ROW_TILE

    def body(x_ref, sc_ref, sh_ref, w_ref, u_ref, *outs):
        u = (x_ref[...] * sc_ref[...] + sh_ref[...]).astype(BF16)
        u_ref[...] = u
        proj = _nt(u, w_ref[...])
        off = 0
        for ref, w in zip(outs, IN_PAD):
            ref[...] = proj[:, off:off + w]
            off += w

    return pl.pallas_call(
        body, name="inproj", grid=(s // tm,),
        in_specs=[_rows(tm, D_MODEL), _whole((1, D_MODEL)), _whole((1, D_MODEL)), _whole((IN_PAD_WIDTH, D_MODEL))],
        out_specs=[_rows(tm, D_MODEL)] + [_rows(tm, w) for w in IN_PAD],
        out_shape=[jax.ShapeDtypeStruct((s, D_MODEL), BF16)] + [jax.ShapeDtypeStruct((s, w), F32) for w in IN_PAD],
        compiler_params=_cparams("parallel"),
    )(x, scale1p, shift, w_in_pt)


def _qpath(q_lat, g_q, w_qb_p, cos, sin):
    s = q_lat.shape[0]
    tm = ROW_TILE

    def body(ql_ref, g_ref, w_ref, cos_ref, sin_ref, nq_ref, q_ref):
        xhat, _ = _rms(ql_ref[...])
        nq = (xhat * g_ref[...]).astype(BF16)
        nq_ref[...] = nq
        raw = _nn(nq, w_ref[...]) * Q_PRESCALE
        c, sn = cos_ref[...], sin_ref[...]
        for h in range(MLA_HEADS):
            o = h * HEAD_PAD
            q_ref[:, o:o + QK_NOPE] = raw[:, o:o + QK_NOPE].astype(BF16)
            q_ref[:, o + QK_NOPE:o + HEAD_PAD] = _rope(raw[:, o + QK_NOPE:o + HEAD_PAD], c, sn).astype(BF16)

    return pl.pallas_call(
        body, name="qpath", grid=(s // tm,),
        in_specs=[_rows(tm, Q_RANK), _whole((1, Q_RANK)), _whole((Q_RANK, MLA_HEADS * HEAD_PAD)),
                  _rows(tm, LANE), _rows(tm, LANE)],
        out_specs=[_rows(tm, Q_RANK), _rows(tm, MLA_HEADS * HEAD_PAD)],
        out_shape=[jax.ShapeDtypeStruct((s, Q_RANK), BF16), jax.ShapeDtypeStruct((s, MLA_HEADS * HEAD_PAD), BF16)],
        compiler_params=_cparams("parallel"),
    )(q_lat, g_q, w_qb_p, cos, sin)


def _kvpath(kv_lat, g_kv, w_kvb_p, cos, sin):
    s = kv_lat.shape[0]
    tm = ROW_TILE

    def body(kl_ref, g_ref, w_ref, cos_ref, sin_ref, nkv_ref, k_ref, v_ref):
        kl = kl_ref[...]
        xhat, _ = _rms(kl[:, :KV_RANK])
        nkv = (xhat * g_ref[...]).astype(BF16)
        nkv_ref[...] = nkv
        raw = _nn(nkv, w_ref[...])
        kr = _rope(kl[:, KV_RANK:], cos_ref[...], sin_ref[...]).astype(BF16)
        for h in range(MLA_HEADS):
            o = h * HEAD_PAD
            k_ref[:, o:o + QK_NOPE] = raw[:, h * QK_NOPE:(h + 1) * QK_NOPE].astype(BF16)
            k_ref[:, o + QK_NOPE:o + HEAD_PAD] = kr
        v_ref[...] = raw[:, MLA_HEADS * QK_NOPE:].astype(BF16)

    return pl.pallas_call(
        body, name="kvpath", grid=(s // tm,),
        in_specs=[_rows(tm, KV_LAT_PAD), _whole((1, KV_RANK)), _whole((KV_RANK, MLA_HEADS * (QK_NOPE + V_DIM))),
                  _rows(tm, LANE), _rows(tm, LANE)],
        out_specs=[_rows(tm, KV_RANK), _rows(tm, MLA_HEADS * HEAD_PAD), _rows(tm, MLA_WIDTH)],
        out_shape=[jax.ShapeDtypeStruct((s, KV_RANK), BF16), jax.ShapeDtypeStruct((s, MLA_HEADS * HEAD_PAD), BF16),
                   jax.ShapeDtypeStruct((s, MLA_WIDTH), BF16)],
        compiler_params=_cparams("parallel"),
    )(kv_lat, g_kv, w_kvb_p, cos, sin)


def _causal_mask(t):
    row = lax.broadcasted_iota(jnp.int32, (t, t), 0)
    col = lax.broadcasted_iota(jnp.int32, (t, t), 1)
    return row, col


def _attn_fwd(q, k, v):
    s = q.shape[0]
    t = min(ATTN_TILE, s)
    nq = s // t

    def body(q_ref, k_ref, v_ref, o_ref, lse_ref, m_sc, l_sc, acc_sc, sa_sc, sb_sc):
        i = pl.program_id(1)
        qv = q_ref[...]
        m_sc[...] = jnp.full(m_sc.shape, -jnp.inf, F32)
        l_sc[...] = jnp.zeros(l_sc.shape, F32)
        acc_sc[...] = jnp.zeros(acc_sc.shape, F32)

        def scores(j, s_ref):
            s_ref[...] = _nt(qv, k_ref[pl.ds(pl.multiple_of(j * t, t), t), :])

        def update(s_ref, j, masked):
            vv = v_ref[pl.ds(pl.multiple_of(j * t, t), t), :]
            sc = s_ref[...]
            if masked:
                row, col = _causal_mask(t)
                sc = jnp.where(col <= row, sc, -jnp.inf)
            m_prev = m_sc[...]
            m_new = jnp.maximum(m_prev, jnp.max(sc, axis=1, keepdims=True))
            alpha = jnp.exp2(m_prev - m_new)
            p = jnp.exp2(sc - jnp.tile(m_new, (1, t // LANE)))
            l_sc[...] = alpha * l_sc[...] + jnp.sum(p, axis=1, keepdims=True)
            acc_sc[...] = alpha * acc_sc[...] + _nn(p.astype(BF16), vv)
            m_sc[...] = m_new

        def run(j0, count):
            bufs = (sa_sc, sb_sc)
            for u in range(count):
                scores(j0 + u + 1, bufs[(u + 1) % 2])
                update(bufs[u % 2], j0 + u, False)

        scores(0, sa_sc)
        done = 0
        for group in ATTN_UNROLLS:
            def body_(g, carry, base=done, group=group):
                run(base + group * g, group)
                return carry

            n_groups = lax.div(i - done, group)
            lax.fori_loop(0, n_groups, body_, 0)
            done = done + group * n_groups
        odd = lax.rem(i, 2)

        @pl.when(odd == 1)
        def _():
            scores(i, sb_sc)
            update(sa_sc, i - 1, False)
            update(sb_sc, i, True)

        @pl.when(odd == 0)
        def _():
            update(sa_sc, i, True)

        l = l_sc[...]
        o_ref[...] = acc_sc[...] / l
        lse_ref[0] = (m_sc[...] + jnp.log2(l)).T[0:1, :]

    return pl.pallas_call(
        body, name="attn_fwd", grid=(MLA_HEADS, nq),
        in_specs=[pl.BlockSpec((t, HEAD_PAD), lambda h, i: (i, h)),
                  pl.BlockSpec((s, HEAD_PAD), lambda h, i: (0, h)),
                  pl.BlockSpec((s, V_DIM), lambda h, i: (0, h))],
        out_specs=[pl.BlockSpec((t, V_DIM), lambda h, i: (i, h)), pl.BlockSpec((1, 1, t), lambda h, i: (h, 0, i))],
        out_shape=[jax.ShapeDtypeStruct((s, MLA_WIDTH), F32), jax.ShapeDtypeStruct((MLA_HEADS, 1, s), F32)],
        scratch_shapes=[pltpu.VMEM((t, LANE), F32), pltpu.VMEM((t, LANE), F32), pltpu.VMEM((t, V_DIM), F32),
                        pltpu.VMEM((t, t), F32), pltpu.VMEM((t, t), F32)],
        compiler_params=_cparams("parallel", "arbitrary"),
    )(q, k, v)


def _attn_bwd(q, k, v, do, lse_row, delta_row):
    s = q.shape[0]
    t = min(ATTN_TILE, s)
    nq = s // t

    def body(q_ref, k_ref, v_ref, do_ref, lse_ref, dl_ref, dk_ref, dv_ref, dq_hbm,
             dq_sc, dk_sc, dv_sc, sa_sc, sb_sc, pa_sc, pb_sc, sem):
        h = pl.program_id(0)
        j = pl.program_id(1)
        kv_ = k_ref[...]
        vv = v_ref[...]

        @pl.when(j == 0)
        def _():
            dq_sc[...] = jnp.zeros(dq_sc.shape, F32)

        dk_sc[...] = jnp.zeros(dk_sc.shape, F32)
        dv_sc[...] = jnp.zeros(dv_sc.shape, F32)

        def scores(i, s_ref, p_ref):
            off = pl.multiple_of(i * t, t)
            s_ref[...] = _nt(kv_, q_ref[pl.ds(off, t), :])
            p_ref[...] = _nt(vv, do_ref[pl.ds(off, t), :])

        def update(i, s_ref, p_ref, masked):
            off = pl.multiple_of(i * t, t)
            qv = q_ref[pl.ds(off, t), :]
            dov = do_ref[pl.ds(off, t), :]
            sct = s_ref[...]
            if masked:
                row, col = _causal_mask(t)
                sct = jnp.where(row <= col, sct, -jnp.inf)
            pt = jnp.exp2(sct - lse_ref[0, :, pl.ds(off, t)])
            gt = (pt * (p_ref[...] - dl_ref[0, :, pl.ds(off, t)])).astype(BF16)
            dv_sc[...] += _nn(pt.astype(BF16), dov)
            dk_sc[...] += _nn(gt, qv)
            dq_sc[pl.ds(off, t), :] += _tn(gt, kv_)

        rest = nq - 1 - j
        scores(j, sa_sc, pa_sc)

        @pl.when(rest >= 1)
        def _():
            scores(j + 1, sb_sc, pb_sc)

        update(j, sa_sc, pa_sc, True)

        def run(i0, count):
            bufs = ((sb_sc, pb_sc), (sa_sc, pa_sc))
            for u in range(count):
                scores(i0 + u + 1, *bufs[(u + 1) % 2])
                update(i0 + u, *bufs[u % 2], False)

        i1, left = j + 1, rest
        for group in ATTN_UNROLLS:
            def body_(g, carry, base=i1, group=group):
                run(base + group * g, group)
                return carry

            n_groups = jnp.where(left >= 1, lax.div(left - 1, group), 0)
            lax.fori_loop(0, n_groups, body_, 0)
            i1 = i1 + group * n_groups
            left = left - group * n_groups

        @pl.when(left == 1)
        def _():
            update(i1, sb_sc, pb_sc, False)

        @pl.when(left == 2)
        def _():
            scores(i1 + 1, sa_sc, pa_sc)
            update(i1, sb_sc, pb_sc, False)
            update(i1 + 1, sa_sc, pa_sc, False)

        dk_ref[...] = dk_sc[...] * LN2
        dv_ref[...] = dv_sc[...]

        def out_copy(jj):
            rows = pl.ds(pl.multiple_of(jj * t, t), t)
            return pltpu.make_async_copy(dq_sc.at[rows, :], dq_hbm.at[h, rows, :], sem)

        @pl.when(j > 0)
        def _():
            out_copy(j - 1).wait()

        out_copy(j).start()

        @pl.when(j == nq - 1)
        def _():
            out_copy(j).wait()

    return pl.pallas_call(
        body, name="attn_bwd", grid=(MLA_HEADS, nq),
        in_specs=[pl.BlockSpec((s, HEAD_PAD), lambda h, j: (0, h)),
                  pl.BlockSpec((t, HEAD_PAD), lambda h, j: (j, h)),
                  pl.BlockSpec((t, V_DIM), lambda h, j: (j, h)),
                  pl.BlockSpec((s, V_DIM), lambda h, j: (0, h)),
                  pl.BlockSpec((1, 1, s), lambda h, j: (h, 0, 0)),
                  pl.BlockSpec((1, 1, s), lambda h, j: (h, 0, 0))],
        out_specs=[pl.BlockSpec((t, HEAD_PAD), lambda h, j: (j, h)), pl.BlockSpec((t, V_DIM), lambda h, j: (j, h)),
                   pl.BlockSpec(memory_space=pl.ANY)],
        out_shape=[jax.ShapeDtypeStruct((s, MLA_HEADS * HEAD_PAD), F32), jax.ShapeDtypeStruct((s, MLA_WIDTH), F32),
                   jax.ShapeDtypeStruct((MLA_HEADS, s, HEAD_PAD), F32)],
        scratch_shapes=[pltpu.VMEM((s, HEAD_PAD), F32), pltpu.VMEM((t, HEAD_PAD), F32), pltpu.VMEM((t, V_DIM), F32),
                        pltpu.VMEM((t, t), F32), pltpu.VMEM((t, t), F32), pltpu.VMEM((t, t), F32),
                        pltpu.VMEM((t, t), F32), pltpu.SemaphoreType.DMA],
        compiler_params=_cparams("arbitrary", "arbitrary"),
    )(q, k, v, do, lse_row, delta_row)


HALO = 8


def _silu(z):
    return z * _sigmoid(z)


def _silu_grad(z):
    sg = _sigmoid(z)
    return sg * (1.0 + z * (1.0 - sg))


def _softplus(x):
    e = jnp.exp(-jnp.abs(x))
    small = e * (1.0 - e * (0.5 - e * (1.0 / 3.0)))
    return jnp.maximum(x, 0.0) + jnp.where(e < 1e-3, small, jnp.log(1.0 + e))


def _conv_taps(xe_ref, w, tm, first):
    acc = None
    for k in range(CONV_K):
        term = xe_ref[pl.ds(HALO + first - (CONV_K - 1) + k, tm), :] * w[k:k + 1, :]
        acc = term if acc is None else acc + term
    return acc


def _ssd_pre(xbc_raw, dt_raw, conv_w, conv_b, dt_bias_p):
    s = xbc_raw.shape[0]
    tm = ROW_TILE
    hb = tm // HALO

    def body(x_ref, prev_ref, dtr_ref, w_ref, b_ref, db_ref, act_ref, dt_ref, xe_sc):
        i = pl.program_id(0)
        xe_sc[pl.ds(0, HALO), :] = jnp.where(i > 0, prev_ref[...], 0.0)
        xe_sc[pl.ds(HALO, tm), :] = x_ref[...]
        pre = _conv_taps(xe_sc, w_ref[...], tm, 0) + b_ref[...]
        act_ref[...] = _silu(pre)
        dt_ref[...] = _softplus(dtr_ref[...] + db_ref[...])

    return pl.pallas_call(
        body, name="ssd_pre", grid=(s // tm,),
        in_specs=[_rows(tm, CONV_CH), pl.BlockSpec((HALO, CONV_CH), lambda i: (jnp.maximum(i * hb - 1, 0), 0)),
                  _rows(tm, LANE), _whole((CONV_K, CONV_CH)), _whole((1, CONV_CH)), _whole((1, LANE))],
        out_specs=[_rows(tm, CONV_CH), _rows(tm, LANE)],
        out_shape=[jax.ShapeDtypeStruct((s, CONV_CH), F32), jax.ShapeDtypeStruct((s, LANE), F32)],
        scratch_shapes=[pltpu.VMEM((tm + HALO, CONV_CH), F32)],
        compiler_params=_cparams("parallel"),
    )(xbc_raw, xbc_raw, dt_raw, conv_w, conv_b, dt_bias_p)


def _split3(a):
    a1 = a.astype(BF16)
    r1 = a - a1.astype(F32)
    a2 = r1.astype(BF16)
    a3 = (r1 - a2.astype(F32)).astype(BF16)
    return a1, a2, a3


def _tri_left(tri, a):
    a1, a2, a3 = _split3(a)
    return _nn(tri, a1) + _nn(tri, a2) + _nn(tri, a3)


def _tri_right(a, tri):
    a1, a2, a3 = _split3(a)
    return _nn(a1, tri) + _nn(a2, tri) + _nn(a3, tri)


def _pair_sel(lane_lo, col_a, col_b):
    return jnp.where(lane_lo, col_a, col_b)


def _chunk_common(dt, a_neg, tril, triu):
    a = dt * a_neg
    lam_c = _tri_left(tril, a)
    lam_r = _tri_right(a.T, triu)
    lam_last = lam_c[CHUNK - 1:CHUNK, :]
    return lam_c, lam_r, lam_last


def _gated_norm_fwd(y, z, g):
    hf = y * _silu(z)
    outs = []
    for grp in range(SSM_GROUPS):
        w = SSM_WIDTH // SSM_GROUPS
        n, _ = _rms(hf[:, grp * w:(grp + 1) * w])
        outs.append(n)
    return jnp.concatenate(outs, axis=1) * g


def _ssd_fwd(xbc, dt, z, a_neg, dskip_x, g_x, tril, triu):
    s = xbc.shape[0]
    tm = min(SSD_ROWS, s)
    cpb = tm // CHUNK
    nc = s // CHUNK

    def body(xbc_ref, dt_ref, z_ref, a_ref, dsk_ref, g_ref, tril_ref, triu_ref, y_ref, o_ref, hin_ref, h_sc):
        @pl.when(pl.program_id(0) == 0)
        def _():
            h_sc[...] = jnp.zeros(h_sc.shape, F32)

        tril, triu = tril_ref[...], triu_ref[...]
        ltri = tril > 0
        lane_lo = _lane_iota((CHUNK, LANE)) < SSM_P

        def chunk(c, carry):
            r0 = pl.multiple_of(c * CHUNK, CHUNK)
            dtc = dt_ref[pl.ds(r0, CHUNK), :]
            lam_c, lam_r, lam_last = _chunk_common(dtc, a_ref[...], tril, triu)
            e_c = jnp.exp(lam_c)
            f_r = jnp.exp(lam_r[:, CHUNK - 1:CHUNK] - lam_r)
            cd = jnp.exp(lam_last)
            for grp in range(SSM_GROUPS):
                bo = SSM_WIDTH + grp * SSM_N
                co = SSM_WIDTH + SSM_GROUPS * SSM_N + grp * SSM_N
                bm = xbc_ref[pl.ds(r0, CHUNK), bo:bo + SSM_N]
                cm = xbc_ref[pl.ds(r0, CHUNK), co:co + SSM_N]
                cm_b = cm.astype(BF16)
                gmat = _nt(cm_b, bm.astype(BF16))
                bt = bm.T
                for pj in range(SSM_HEADS // SSM_GROUPS // 2):
                    ha = grp * (SSM_HEADS // SSM_GROUPS) + 2 * pj
                    hb_ = ha + 1
                    lo = ha * SSM_P
                    xs = xbc_ref[pl.ds(r0, CHUNK), lo:lo + LANE]
                    x2 = xs * _pair_sel(lane_lo, dtc[:, ha:ha + 1], dtc[:, hb_:hb_ + 1])
                    x2b = x2.astype(BF16)
                    ys, sts = [], []
                    for hh in (ha, hb_):
                        seg = lam_c[:, hh:hh + 1] - lam_r[hh:hh + 1, :]
                        dec = jnp.exp(jnp.where(ltri, seg, -jnp.inf))
                        ys.append(_nn((gmat * dec).astype(BF16), x2b))
                        sts.append(_nn((bt * f_r[hh:hh + 1, :]).astype(BF16), x2b))
                    hp = h_sc[:, lo:lo + LANE]
                    hin_ref[c, :, lo:lo + LANE] = hp
                    zz = _nn(cm_b, hp.astype(BF16))
                    e2 = _pair_sel(lane_lo, e_c[:, ha:ha + 1], e_c[:, hb_:hb_ + 1])
                    yv = jnp.where(lane_lo, ys[0], ys[1]) + e2 * zz
                    y_ref[pl.ds(r0, CHUNK), lo:lo + LANE] = yv + xs * dsk_ref[:, lo:lo + LANE]
                    cd2 = _pair_sel(lane_lo, cd[:, ha:ha + 1], cd[:, hb_:hb_ + 1])
                    h_sc[:, lo:lo + LANE] = hp * cd2 + jnp.where(lane_lo, sts[0], sts[1])
            return carry

        lax.fori_loop(0, cpb, chunk, 0)
        o_ref[...] = _gated_norm_fwd(y_ref[...], z_ref[...], g_ref[...])

    return pl.pallas_call(
        body, name="ssd_fwd", grid=(s // tm,),
        in_specs=[_rows(tm, CONV_CH), _rows(tm, LANE), _rows(tm, SSM_WIDTH), _whole((1, LANE)),
                  _whole((1, SSM_WIDTH)), _whole((1, SSM_WIDTH)), _whole((CHUNK, CHUNK)), _whole((CHUNK, CHUNK))],
        out_specs=[_rows(tm, SSM_WIDTH), _rows(tm, SSM_WIDTH),
                   pl.BlockSpec((cpb, SSM_N, SSM_WIDTH), lambda i: (i, 0, 0))],
        out_shape=[jax.ShapeDtypeStruct((s, SSM_WIDTH), F32), jax.ShapeDtypeStruct((s, SSM_WIDTH), F32),
                   jax.ShapeDtypeStruct((nc, SSM_N, SSM_WIDTH), F32)],
        scratch_shapes=[pltpu.VMEM((SSM_N, SSM_WIDTH), F32)],
        compiler_params=_cparams("arbitrary"),
    )(xbc, dt, z, a_neg, dskip_x, g_x, tril, triu)


def _outln(o, z_attn, o_ssm, w_out, x, gate, ln_g, ln_b, tgt):
    s = x.shape[0]
    tm = ROW_TILE

    def body(o_ref, z_ref, os_ref, w_ref, x_ref, gate_ref, g_ref, b_ref, t_ref,
             cat_ref, dmix_ref, gx_ref, do_ref, dz_ref, dl_ref, dos_ref, loss_ref, dg_ref, db_ref, dgate_ref):
        ov, zv = o_ref[...], z_ref[...]
        sz = _silu(zv)
        cat_ref[:, :MLA_WIDTH] = (ov * sz).astype(BF16)
        cat_ref[:, MLA_WIDTH:] = os_ref[...].astype(BF16)
        w = w_ref[...]
        mixed = _nn(cat_ref[...], w)
        gate_v = gate_ref[...]
        hv = DEEPNORM_ALPHA * x_ref[...] + gate_v * mixed
        mu = jnp.mean(hv, axis=-1, keepdims=True)
        hc = hv - mu
        rstd = lax.rsqrt(jnp.mean(hc * hc, axis=-1, keepdims=True) + LN_EPS)
        xhat = hc * rstd
        g = g_ref[...]
        err = xhat * g + b_ref[...] - t_ref[...]
        _acc_rows(loss_ref, jnp.full((1, LANE), (0.5 / D_MODEL) * jnp.sum(err * err), F32))
        dy = err * (1.0 / D_MODEL)
        _acc_rows(dg_ref, _colsum(dy * xhat))
        _acc_rows(db_ref, _colsum(dy))
        dxhat = dy * g
        dh = rstd * (dxhat - jnp.mean(dxhat, axis=-1, keepdims=True)
                     - xhat * jnp.mean(dxhat * xhat, axis=-1, keepdims=True))
        gx_ref[...] = DEEPNORM_ALPHA * dh
        _acc_rows(dgate_ref, _colsum(dh * mixed))
        dmix = (gate_v * dh).astype(BF16)
        dmix_ref[...] = dmix
        dcat = _nt(dmix, w)
        da = dcat[:, :MLA_WIDTH]
        dos_ref[...] = dcat[:, MLA_WIDTH:]
        dov = da * sz
        do_ref[...] = dov.astype(BF16)
        dz_ref[...] = da * ov * _silu_grad(zv)
        prod = dov * ov
        for h in range(MLA_HEADS):
            dsum = jnp.sum(prod[:, h * V_DIM:(h + 1) * V_DIM], axis=1, keepdims=True)
            dl_ref[h] = jnp.broadcast_to(dsum, (tm, LANE)).T[0:1, :]

    vec = _whole((1, D_MODEL))
    return pl.pallas_call(
        body, name="outln", grid=(s // tm,),
        in_specs=[_rows(tm, MLA_WIDTH), _rows(tm, MLA_WIDTH), _rows(tm, SSM_WIDTH), _whole((MIX_WIDTH, D_MODEL)),
                  _rows(tm, D_MODEL), vec, vec, vec, _rows(tm, D_MODEL)],
        out_specs=[_rows(tm, MIX_WIDTH), _rows(tm, D_MODEL), _rows(tm, D_MODEL), _rows(tm, MLA_WIDTH),
                   _rows(tm, MLA_WIDTH), pl.BlockSpec((MLA_HEADS, 1, tm), lambda i: (0, 0, i)), _rows(tm, SSM_WIDTH),
                   _whole((1, LANE)), vec, vec, vec],
        out_shape=[jax.ShapeDtypeStruct((s, MIX_WIDTH), BF16), jax.ShapeDtypeStruct((s, D_MODEL), BF16),
                   jax.ShapeDtypeStruct((s, D_MODEL), F32), jax.ShapeDtypeStruct((s, MLA_WIDTH), BF16),
                   jax.ShapeDtypeStruct((s, MLA_WIDTH), F32), jax.ShapeDtypeStruct((MLA_HEADS, 1, s), F32),
                   jax.ShapeDtypeStruct((s, SSM_WIDTH), F32), jax.ShapeDtypeStruct((1, LANE), F32),
                   jax.ShapeDtypeStruct((1, D_MODEL), F32), jax.ShapeDtypeStruct((1, D_MODEL), F32),
                   jax.ShapeDtypeStruct((1, D_MODEL), F32)],
        compiler_params=_cparams("arbitrary"),
    )(o, z_attn, o_ssm, w_out, x, gate, ln_g, ln_b, tgt)


def _ssd_bwd(dos, y, z, xbc, dt, hin, a_neg, dskip_x, g_x, tril, triu, expand):
    s = xbc.shape[0]
    tm = min(SSD_ROWS, s)
    cpb = tm // CHUNK
    nb = s // tm
    gw = SSM_WIDTH // SSM_GROUPS
    hpg = SSM_HEADS // SSM_GROUPS

    def body(dos_ref, y_ref, z_ref, xbc_ref, dt_ref, hin_ref, a_ref, dsk_ref, g_ref, tril_ref, triu_ref, exp_ref,
             dxbc_ref, ddt_ref, dz_ref, dg_ref, ddsk_ref, da_ref, dh_sc, dy_sc):
        @pl.when(pl.program_id(0) == 0)
        def _():
            dh_sc[...] = jnp.zeros(dh_sc.shape, F32)

        yv, zv, dov = y_ref[...], z_ref[...], dos_ref[...]
        sz = _silu(zv)
        hf = yv * sz
        gv = g_ref[...]
        dgs, dhfs = [], []
        for grp in range(SSM_GROUPS):
            sl = slice(grp * gw, (grp + 1) * gw)
            n, rstd = _rms(hf[:, sl])
            dgs.append(_colsum(dov[:, sl] * n))
            dhfs.append(_rms_bwd(dov[:, sl] * gv[:, sl], n, rstd))
        dhf = jnp.concatenate(dhfs, axis=1)
        _acc_rows(dg_ref, jnp.concatenate(dgs, axis=1))
        dy_sc[...] = dhf * sz
        dz_ref[...] = dhf * yv * _silu_grad(zv)

        tril, triu, expand = tril_ref[...], triu_ref[...], exp_ref[...]
        ltri = tril > 0
        utri = triu > 0
        lane = _lane_iota((CHUNK, LANE))
        lane1 = _lane_iota((1, LANE))
        lane_lo = lane < SSM_P
        row_last = lax.broadcasted_iota(jnp.int32, (CHUNK, LANE), 0) == CHUNK - 1
        a_neg_v = a_ref[...]

        def chunk(ci, carry):
            dsk_acc, da_acc = carry
            cl = cpb - 1 - ci
            r0 = pl.multiple_of(cl * CHUNK, CHUNK)
            rows = pl.ds(r0, CHUNK)
            dtc = dt_ref[rows, :]
            lam_c, lam_r, lam_last = _chunk_common(dtc, a_neg_v, tril, triu)
            e_c = jnp.exp(lam_c)
            f_c = jnp.exp(lam_last - lam_c)
            cd = jnp.exp(lam_last)
            dt_x, e_x, f_x = _tri_right(dtc, expand), _tri_right(e_c, expand), _tri_right(f_c, expand)
            cd_x = _tri_right(jnp.broadcast_to(cd, (HALO, LANE)), expand)[0:1, :]
            dlam = jnp.zeros((CHUNK, LANE), F32)
            dlast = jnp.zeros((1, LANE), F32)
            ddt_x = jnp.zeros((CHUNK, LANE), F32)
            dsk_parts = []
            for grp in range(SSM_GROUPS):
                bo = SSM_WIDTH + grp * SSM_N
                co = SSM_WIDTH + SSM_GROUPS * SSM_N + grp * SSM_N
                bm = xbc_ref[rows, bo:bo + SSM_N]
                cm = xbc_ref[rows, co:co + SSM_N]
                bm_b, cm_b = bm.astype(BF16), cm.astype(BF16)
                gmat = _nt(cm_b, bm_b)
                gmat_t = _nt(bm_b, cm_b)
                ct_b = cm.T.astype(BF16)
                acc_dg = jnp.zeros((CHUNK, CHUNK), F32)
                acc_dgt = jnp.zeros((CHUNK, CHUNK), F32)
                d_b = jnp.zeros((CHUNK, SSM_N), F32)
                d_c = jnp.zeros((CHUNK, SSM_N), F32)
                for pj in range(hpg // 2):
                    ha = grp * hpg + 2 * pj
                    hb_ = ha + 1
                    lo = ha * SSM_P
                    blk = slice(lo, lo + LANE)
                    xs = xbc_ref[rows, blk]
                    dt2, e2, f2, cd2 = dt_x[:, blk], e_x[:, blk], f_x[:, blk], cd_x[:, blk]
                    x2 = xs * dt2
                    x2b = x2.astype(BF16)
                    dy2 = dy_sc[rows, blk]
                    dy2b = dy2.astype(BF16)
                    hp = hin_ref[cl, :, blk]
                    hp_b = hp.astype(BF16)
                    dhn = dh_sc[:, blk]
                    dhn_b = dhn.astype(BF16)
                    yo = e2 * _nn(cm_b, hp_b)
                    dzz_b = (e2 * dy2).astype(BF16)
                    d_c = d_c + _nt(dzz_b, hp_b)
                    dh_sc[:, blk] = _nn(ct_b, dzz_b) + cd2 * dhn
                    dxs2 = f2 * _nn(bm_b, dhn_b)
                    d_b = d_b + _nt((f2 * x2).astype(BF16), dhn_b)
                    xd = x2 * dxs2
                    t_lam = dy2 * yo - xd
                    t_last = cd2 * (dhn * hp) + xd
                    dxd2 = jnp.zeros((CHUNK, LANE), F32)
                    heads = ((ha, lane_lo), (hb_, jnp.logical_not(lane_lo)))
                    for hh, msk in heads:
                        x2h_b = jnp.where(msk, x2, 0.0).astype(BF16)
                        dy2h_b = jnp.where(msk, dy2, 0.0).astype(BF16)
                        seg = lam_c[:, hh:hh + 1] - lam_r[hh:hh + 1, :]
                        dec = jnp.exp(jnp.where(ltri, seg, -jnp.inf))
                        dect = jnp.exp(jnp.where(utri, -seg, -jnp.inf))
                        dmd = _nt(dy2h_b, x2b) * dec
                        dmtd = _nt(x2h_b, dy2b) * dect
                        acc_dg = acc_dg + dmd
                        acc_dgt = acc_dgt + dmtd
                        dlam_h = jnp.sum(dmd * gmat - dmtd * gmat_t + jnp.where(msk, t_lam, 0.0), axis=1, keepdims=True)
                        last_h = jnp.sum(jnp.sum(jnp.where(msk, t_last, 0.0), axis=0, keepdims=True), axis=1, keepdims=True)
                        dlam = jnp.where(lane == hh, dlam_h, dlam)
                        dlast = jnp.where(lane1 == hh, last_h, dlast)
                        dxd2 = jnp.where(msk, _nn((gmat_t * dect).astype(BF16), dy2b), dxd2)
                    dx2 = dxd2 + dxs2
                    dxbc_ref[rows, blk] = dx2 * dt2 + dy2 * dsk_ref[:, blk]
                    prod = dx2 * xs
                    for hh, msk in heads:
                        col = jnp.sum(jnp.where(msk, prod, 0.0), axis=1, keepdims=True)
                        ddt_x = jnp.where(lane == hh, col, ddt_x)
                    dsk_parts.append(_colsum(dy2 * xs))
                d_c = d_c + _nn(acc_dg.astype(BF16), bm_b)
                d_b = d_b + _nn(acc_dgt.astype(BF16), cm_b)
                dxbc_ref[rows, bo:bo + SSM_N] = d_b
                dxbc_ref[rows, co:co + SSM_N] = d_c
            dlam = dlam + jnp.where(row_last, dlast, 0.0)
            da = _tri_left(triu, dlam)
            ddt_ref[rows, :] = da * a_neg_v + ddt_x
            return dsk_acc + jnp.concatenate(dsk_parts, axis=1), da_acc + _colsum(da * dtc)

        dsk_tot, da_tot = lax.fori_loop(
            0, cpb, chunk, (jnp.zeros((1, SSM_WIDTH), F32), jnp.zeros((1, LANE), F32)))
        _acc_rows(ddsk_ref, dsk_tot)
        _acc_rows(da_ref, da_tot)

    rev = lambda i: (nb - 1 - i, 0)
    rrows = lambda w: pl.BlockSpec((tm, w), rev)
    return pl.pallas_call(
        body, name="ssd_bwd", grid=(nb,),
        in_specs=[rrows(SSM_WIDTH), rrows(SSM_WIDTH), rrows(SSM_WIDTH), rrows(CONV_CH), rrows(LANE),
                  pl.BlockSpec((cpb, SSM_N, SSM_WIDTH), lambda i: (nb - 1 - i, 0, 0)),
                  _whole((1, LANE)), _whole((1, SSM_WIDTH)), _whole((1, SSM_WIDTH)),
                  _whole((CHUNK, CHUNK)), _whole((CHUNK, CHUNK)), _whole((LANE, SSM_WIDTH))],
        out_specs=[rrows(CONV_CH), rrows(LANE), rrows(SSM_WIDTH),
                   _whole((1, SSM_WIDTH)), _whole((1, SSM_WIDTH)), _whole((1, LANE))],
        out_shape=[jax.ShapeDtypeStruct((s, CONV_CH), F32), jax.ShapeDtypeStruct((s, LANE), F32),
                   jax.ShapeDtypeStruct((s, SSM_WIDTH), F32), jax.ShapeDtypeStruct((1, SSM_WIDTH), F32),
                   jax.ShapeDtypeStruct((1, SSM_WIDTH), F32), jax.ShapeDtypeStruct((1, LANE), F32)],
        scratch_shapes=[pltpu.VMEM((SSM_N, SSM_WIDTH), F32), pltpu.VMEM((tm, SSM_WIDTH), F32)],
        compiler_params=_cparams("arbitrary"),
    )(dos, y, z, xbc, dt, hin, a_neg, dskip_x, g_x, tril, triu, expand)


def _ssd_post_bwd(xbc_raw, dxa, ddt, dt_raw, conv_w, conv_b, dt_bias_p):
    s = xbc_raw.shape[0]
    tm = ROW_TILE
    hb = tm // HALO
    nt = s // tm
    ext = tm + HALO

    def body(x_ref, prev_ref, next_ref, d_ref, dnext_ref, ddt_ref, dtr_ref, w_ref, b_ref, db_ref,
             dx_ref, ddtr_ref, dw_ref, dcb_ref, ddb_ref, xe_sc, de_sc):
        i = pl.program_id(0)
        w = w_ref[...]
        xe_sc[pl.ds(0, HALO), :] = jnp.where(i > 0, prev_ref[...], 0.0)
        xe_sc[pl.ds(HALO, tm), :] = x_ref[...]
        xe_sc[pl.ds(HALO + tm, HALO), :] = next_ref[...]
        pre = _conv_taps(xe_sc, w, ext, 0) + b_ref[...]
        sg = _silu_grad(pre)
        de_sc[pl.ds(0, tm), :] = d_ref[...] * sg[:tm]
        de_sc[pl.ds(tm, HALO), :] = jnp.where(i < nt - 1, dnext_ref[...] * sg[tm:], 0.0)
        dconv = de_sc[pl.ds(0, tm), :]
        acc = None
        dws = []
        for k in range(CONV_K):
            term = de_sc[pl.ds(CONV_K - 1 - k, tm), :] * w[k:k + 1, :]
            acc = term if acc is None else acc + term
            dws.append(_colsum(dconv * xe_sc[pl.ds(HALO - (CONV_K - 1) + k, tm), :]))
        dx_ref[...] = acc
        _acc_rows(dw_ref, jnp.concatenate(dws, axis=0))
        _acc_rows(dcb_ref, _colsum(dconv))
        ddtr = ddt_ref[...] * _sigmoid(dtr_ref[...] + db_ref[...])
        ddtr_ref[...] = ddtr
        _acc_rows(ddb_ref, _colsum(ddtr))

    halo_prev = pl.BlockSpec((HALO, CONV_CH), lambda i: (jnp.maximum(i * hb - 1, 0), 0))
    halo_next = pl.BlockSpec((HALO, CONV_CH), lambda i: (jnp.minimum((i + 1) * hb, s // HALO - 1), 0))
    return pl.pallas_call(
        body, name="ssd_post_bwd", grid=(nt,),
        in_specs=[_rows(tm, CONV_CH), halo_prev, halo_next, _rows(tm, CONV_CH), halo_next, _rows(tm, LANE),
                  _rows(tm, LANE), _whole((CONV_K, CONV_CH)), _whole((1, CONV_CH)), _whole((1, LANE))],
        out_specs=[_rows(tm, CONV_CH), _rows(tm, LANE), _whole((CONV_K, CONV_CH)), _whole((1, CONV_CH)),
                   _whole((1, LANE))],
        out_shape=[jax.ShapeDtypeStruct((s, CONV_CH), F32), jax.ShapeDtypeStruct((s, LANE), F32),
                   jax.ShapeDtypeStruct((CONV_K, CONV_CH), F32), jax.ShapeDtypeStruct((1, CONV_CH), F32),
                   jax.ShapeDtypeStruct((1, LANE), F32)],
        scratch_shapes=[pltpu.VMEM((tm + 2 * HALO, CONV_CH), F32), pltpu.VMEM((ext, CONV_CH), F32)],
        compiler_params=_cparams("arbitrary"),
    )(xbc_raw, xbc_raw, xbc_raw, dxa, dxa, ddt, dt_raw, conv_w, conv_b, dt_bias_p)


def _qbwd(dq_att, q_lat, g_q, w_qb_p, cos, sin):
    s = q_lat.shape[0]
    tm = ROW_TILE
    wq = MLA_HEADS * HEAD_PAD

    def body(dq_ref, ql_ref, g_ref, w_ref, cos_ref, sin_ref, dql_ref, draw_ref, dg_ref):
        c, sn = cos_ref[...], sin_ref[...]
        for h in range(MLA_HEADS):
            o = h * HEAD_PAD
            dqh = dq_ref[h] * ATTN_SCALE
            draw_ref[:, o:o + QK_NOPE] = dqh[:, :QK_NOPE].astype(BF16)
            draw_ref[:, o + QK_NOPE:o + HEAD_PAD] = _rope_transposed(dqh[:, QK_NOPE:], c, sn).astype(BF16)
        dn = _nt(draw_ref[...], w_ref[...])
        xhat, rstd = _rms(ql_ref[...])
        _acc_rows(dg_ref, _colsum(dn * xhat))
        dql_ref[...] = _rms_bwd(dn * g_ref[...], xhat, rstd)

    return pl.pallas_call(
        body, name="qbwd", grid=(s // tm,),
        in_specs=[pl.BlockSpec((MLA_HEADS, tm, HEAD_PAD), lambda i: (0, i, 0)), _rows(tm, Q_RANK), _whole((1, Q_RANK)),
                  _whole((Q_RANK, wq)), _rows(tm, LANE), _rows(tm, LANE)],
        out_specs=[_rows(tm, Q_RANK), _rows(tm, wq), _whole((1, Q_RANK))],
        out_shape=[jax.ShapeDtypeStruct((s, Q_RANK), F32), jax.ShapeDtypeStruct((s, wq), BF16),
                   jax.ShapeDtypeStruct((1, Q_RANK), F32)],
        compiler_params=_cparams("arbitrary"),
    )(dq_att, q_lat, g_q, w_qb_p, cos, sin)


def _kvbwd(dk_att, dv, kv_lat, g_kv, w_kvb_p, cos, sin):
    s = kv_lat.shape[0]
    tm = ROW_TILE
    wk = MLA_HEADS * HEAD_PAD
    wr = MLA_HEADS * (QK_NOPE + V_DIM)

    def body(dk_ref, dv_ref, kl_ref, g_ref, w_ref, cos_ref, sin_ref, dkl_ref, draw_ref, dg_ref):
        dkr = None
        for h in range(MLA_HEADS):
            o = h * HEAD_PAD
            draw_ref[:, h * QK_NOPE:(h + 1) * QK_NOPE] = dk_ref[:, o:o + QK_NOPE].astype(BF16)
            part = dk_ref[:, o + QK_NOPE:o + HEAD_PAD]
            dkr = part if dkr is None else dkr + part
        draw_ref[:, MLA_HEADS * QK_NOPE:] = dv_ref[...].astype(BF16)
        dn = _nt(draw_ref[...], w_ref[...])
        xhat, rstd = _rms(kl_ref[:, :KV_RANK])
        _acc_rows(dg_ref, _colsum(dn * xhat))
        dkl_ref[:, :KV_RANK] = _rms_bwd(dn * g_ref[...], xhat, rstd)
        dkl_ref[:, KV_RANK:] = _rope_transposed(dkr, cos_ref[...], sin_ref[...])

    return pl.pallas_call(
        body, name="kvbwd", grid=(s // tm,),
        in_specs=[_rows(tm, wk), _rows(tm, MLA_WIDTH), _rows(tm, KV_LAT_PAD), _whole((1, KV_RANK)),
                  _whole((KV_RANK, wr)), _rows(tm, LANE), _rows(tm, LANE)],
        out_specs=[_rows(tm, KV_LAT_PAD), _rows(tm, wr), _whole((1, KV_RANK))],
        out_shape=[jax.ShapeDtypeStruct((s, KV_LAT_PAD), F32), jax.ShapeDtypeStruct((s, wr), BF16),
                   jax.ShapeDtypeStruct((1, KV_RANK), F32)],
        compiler_params=_cparams("arbitrary"),
    )(dk_att, dv, kv_lat, g_kv, w_kvb_p, cos, sin)


def _inproj_bwd(pieces, w_in_pt, x, scale1p, gx1):
    s = x.shape[0]
    tm = ROW_TILE

    def body(*refs):
        p_refs = refs[:len(IN_PAD)]
        w_ref, x_ref, sc_ref, gx1_ref, gx_ref, dp_ref, dsc_ref, dsh_ref = refs[len(IN_PAD):]
        off = 0
        for ref, w in zip(p_refs, IN_PAD):
            dp_ref[:, off:off + w] = ref[...].astype(BF16)
            off += w
        du = _nn(dp_ref[...], w_ref[...])
        gx_ref[...] = gx1_ref[...] + du * sc_ref[...]
        _acc_rows(dsc_ref, _colsum(du * x_ref[...]))
        _acc_rows(dsh_ref, _colsum(du))

    vec = _whole((1, D_MODEL))
    return pl.pallas_call(
        body, name="inproj_bwd", grid=(s // tm,),
        in_specs=[_rows(tm, w) for w in IN_PAD] + [_whole((IN_PAD_WIDTH, D_MODEL)), _rows(tm, D_MODEL), vec,
                                                    _rows(tm, D_MODEL)],
        out_specs=[_rows(tm, D_MODEL), _rows(tm, IN_PAD_WIDTH), vec, vec],
        out_shape=[jax.ShapeDtypeStruct((s, D_MODEL), F32), jax.ShapeDtypeStruct((s, IN_PAD_WIDTH), BF16),
                   jax.ShapeDtypeStruct((1, D_MODEL), F32), jax.ShapeDtypeStruct((1, D_MODEL), F32)],
        compiler_params=_cparams("arbitrary"),
    )(*pieces, w_in_pt, x, scale1p, gx1)


def _matmul_tn_rows(name, a, b, tk):
    s, k = a.shape
    n = b.shape[1]
    tm = min(GRAD_ROWS, s)

    def body(a_ref, b_ref, o_ref):
        @pl.when(pl.program_id(1) == 0)
        def _():
            o_ref[...] = jnp.zeros_like(o_ref)
        o_ref[...] += _tn(a_ref[...], b_ref[...])

    return pl.pallas_call(
        body, name=name, grid=(k // tk, s // tm),
        in_specs=[pl.BlockSpec((tm, tk), lambda j, i: (i, j)), pl.BlockSpec((tm, n), lambda j, i: (i, 0))],
        out_specs=pl.BlockSpec((tk, n), lambda j, i: (j, 0)),
        out_shape=jax.ShapeDtypeStruct((k, n), F32),
        compiler_params=_cparams("parallel", "arbitrary"),
    )(a, b)


def _matmul_tn(name, a, b, tn):
    s, k = a.shape
    n = b.shape[1]
    tm = min(GRAD_ROWS, s)

    def body(a_ref, b_ref, o_ref):
        @pl.when(pl.program_id(1) == 0)
        def _():
            o_ref[...] = jnp.zeros_like(o_ref)
        o_ref[...] += _tn(a_ref[...], b_ref[...])

    return pl.pallas_call(
        body, name=name, grid=(n // tn, s // tm),
        in_specs=[pl.BlockSpec((tm, k), lambda j, i: (i, 0)), pl.BlockSpec((tm, tn), lambda j, i: (i, j))],
        out_specs=pl.BlockSpec((k, tn), lambda j, i: (0, j)),
        out_shape=jax.ShapeDtypeStruct((k, n), F32),
        compiler_params=_cparams("parallel", "arbitrary"),
    )(a, b)


def _pack_w_in_t(w_in_t):
    parts, off = [], 0
    for w, wp in zip(IN_SPLITS, IN_PAD):
        parts.append(jnp.pad(w_in_t[off:off + w], ((0, wp - w), (0, 0))))
        off += w
    return jnp.concatenate(parts, axis=0)


def _unpack_w_in_t(g):
    parts, off = [], 0
    for w, wp in zip(IN_SPLITS, IN_PAD):
        parts.append(g[off:off + w])
        off += wp
    return jnp.concatenate(parts, axis=0)


def _pack_w_qb(w_qb):
    w = w_qb.reshape(Q_RANK, MLA_HEADS, QK_HEAD)
    return jnp.pad(w, ((0, 0), (0, 0), (0, HEAD_PAD - QK_HEAD))).reshape(Q_RANK, MLA_HEADS * HEAD_PAD)


def _unpack_w_qb(g):
    return g.reshape(Q_RANK, MLA_HEADS, HEAD_PAD)[:, :, :QK_HEAD].reshape(Q_RANK, MLA_HEADS * QK_HEAD)


def _pack_w_kvb(w_kvb):
    w = w_kvb.reshape(KV_RANK, MLA_HEADS, QK_NOPE + V_DIM)
    return jnp.concatenate([w[:, :, :QK_NOPE].reshape(KV_RANK, -1), w[:, :, QK_NOPE:].reshape(KV_RANK, -1)], axis=1)


def _unpack_w_kvb(g):
    gk = g[:, :MLA_HEADS * QK_NOPE].reshape(KV_RANK, MLA_HEADS, QK_NOPE)
    gv = g[:, MLA_HEADS * QK_NOPE:].reshape(KV_RANK, MLA_HEADS, V_DIM)
    return jnp.concatenate([gk, gv], axis=2).reshape(KV_RANK, -1)


def _rope_tables(positions):
    inv_freq = 1.0 / (ROPE_THETA ** (jnp.arange(ROPE_HALF, dtype=F32) / ROPE_HALF))
    ang = positions.astype(F32)[:, None] * inv_freq
    cos, sin = jnp.cos(ang), jnp.sin(ang)
    zeros = jnp.zeros((positions.shape[0], LANE - QK_ROPE), F32)
    return jnp.concatenate([cos, cos, zeros], axis=1), jnp.concatenate([-sin, sin, zeros], axis=1)


def _local_step(x, tgt, positions, mod, w_in_t, q_norm_g, w_qb_p, kv_norm_g, w_kvb_p, conv_w, conv_b, dt_bias,
                a_log, d_skip, ssm_norm_g, w_out_b, ln_g, ln_b):
    row = lambda v: v.reshape(1, -1)
    shift, scale, gate = mod[:D_MODEL], mod[D_MODEL:2 * D_MODEL], mod[2 * D_MODEL:]
    scale1p = row(1.0 + scale)
    w_in_p = _pack_w_in_t(w_in_t)
    cos, sin = _rope_tables(positions)
    a_neg = row(jnp.pad(-jnp.exp(a_log), (0, LANE - SSM_HEADS)))
    dskip_x = row(jnp.repeat(d_skip, SSM_P))
    dt_bias_p = row(jnp.pad(dt_bias, (0, LANE - SSM_HEADS)))
    tri = jnp.tril(jnp.ones((CHUNK, CHUNK), F32))
    tril, triu = tri.astype(BF16), tri.T.astype(BF16)

    u_bf, q_lat, kv_lat, z_attn, xbc_raw, dt_raw, z_ssm = _inproj(x, scale1p, row(shift), w_in_p)
    nq_bf, q_att = _qpath(q_lat, row(q_norm_g), w_qb_p, cos, sin)
    nkv_bf, k_att, v_att = _kvpath(kv_lat, row(kv_norm_g), w_kvb_p, cos, sin)
    o, lse_rows = _attn_fwd(q_att, k_att, v_att)
    xbc, dt = _ssd_pre(xbc_raw, dt_raw, conv_w, row(conv_b), dt_bias_p)
    expand = jnp.repeat(jnp.eye(LANE, SSM_HEADS, dtype=BF16), SSM_P, axis=1)
    y, o_ssm, hin = _ssd_fwd(xbc, dt, z_ssm, a_neg, dskip_x, row(ssm_norm_g), tril, triu)
    (cat_bf, dmix_bf, gx1, do_bf, dz_attn, delta_rows, dos, loss, d_ln_g, d_ln_b, d_gate) = _outln(
        o, z_attn, o_ssm, w_out_b, x, row(gate), row(ln_g), row(ln_b), tgt)

    g_w_out = _matmul_tn("gw_out", cat_bf, dmix_bf, 512)
    dk_att, dv, dq_att = _attn_bwd(q_att, k_att, v_att, do_bf, lse_rows, delta_rows)
    dq_lat, dqraw_bf, d_q_norm_g = _qbwd(dq_att, q_lat, row(q_norm_g), w_qb_p, cos, sin)
    dkv_lat, dkvraw_bf, d_kv_norm_g = _kvbwd(dk_att, dv, kv_lat, row(kv_norm_g), w_kvb_p, cos, sin)
    g_w_qb = _matmul_tn("gw_qb", nq_bf, dqraw_bf, MLA_HEADS * HEAD_PAD)
    g_w_kvb = _matmul_tn("gw_kvb", nkv_bf, dkvraw_bf, MLA_HEADS * (QK_NOPE + V_DIM))
    dxa, ddt, dz_ssm, d_ssm_g, ddsk_x, d_a = _ssd_bwd(dos, y, z_ssm, xbc, dt, hin, a_neg, dskip_x, row(ssm_norm_g),
                                                       tril, triu, expand)
    dxbc_raw, ddt_raw, d_conv_w, d_conv_b, d_dt_bias = _ssd_post_bwd(xbc_raw, dxa, ddt, dt_raw, conv_w, row(conv_b),
                                                                     dt_bias_p)
    grad_x, dproj_bf, d_scale, d_shift = _inproj_bwd((dq_lat, dkv_lat, dz_attn, dxbc_raw, ddt_raw, dz_ssm),
                                                     w_in_p, x, scale1p, gx1)
    g_w_in_t = _unpack_w_in_t(_matmul_tn_rows("gw_in", dproj_bf, u_bf, 896))
    return dict(
        loss=loss[0, 0], grad_x=grad_x,
        dmod=jnp.concatenate([d_shift[0], d_scale[0], d_gate[0]]),
        w_in_t=g_w_in_t, q_norm_g=d_q_norm_g[0], w_qb=g_w_qb, kv_norm_g=d_kv_norm_g[0], w_kvb=g_w_kvb,
        conv_w=d_conv_w, conv_b=d_conv_b[0], dt_bias=d_dt_bias[0, :SSM_HEADS],
        a_log=d_a[0, :SSM_HEADS] * a_neg[0, :SSM_HEADS],
        d_skip=ddsk_x.reshape(SSM_HEADS, SSM_P).sum(axis=1), ssm_norm_g=d_ssm_g[0], w_out=g_w_out,
        ln_g=d_ln_g[0], ln_b=d_ln_b[0])


ADAM_ROWS = 512


def _my_index():
    return 4 * lax.axis_index("x") + 2 * lax.axis_index("y") + lax.axis_index("c")


def _exchange(name, sends, gather):
    n = len(sends)
    peers = N_DEV - 1

    def body(*refs):
        send_refs, recv_refs = refs[:n], refs[n:2 * n]
        send_sems, recv_sems, local_sems = refs[2 * n:]
        x, y, c = lax.axis_index("x"), lax.axis_index("y"), lax.axis_index("c")
        me = 4 * x + 2 * y + c

        def src(a, idx):
            return send_refs[a] if gather else send_refs[a].at[idx]

        owns = [pltpu.make_async_copy(src(a, me), recv_refs[a].at[me], local_sems.at[a]) for a in range(n)]
        for cp in owns:
            cp.start()
        copies = []
        for k in range(1, N_DEV):
            px, py, pc = x ^ ((k >> 2) & 1), y ^ ((k >> 1) & 1), c ^ (k & 1)
            peer = 4 * px + 2 * py + pc
            for a in range(n):
                copies.append(pltpu.make_async_remote_copy(
                    src_ref=src(a, peer), dst_ref=recv_refs[a].at[me],
                    send_sem=send_sems.at[a * peers + k - 1], recv_sem=recv_sems.at[a * peers + k - 1],
                    device_id=(px, py, pc), device_id_type=pl.DeviceIdType.MESH))
        for cp in copies:
            cp.start()
        for cp in copies:
            cp.wait()
        for cp in owns:
            cp.wait()

    block_shape = lambda a: a.shape if gather else a.shape[1:]
    return pl.pallas_call(
        body, name=name,
        in_specs=[pl.BlockSpec(memory_space=pl.ANY)] * n, out_specs=[pl.BlockSpec(memory_space=pl.ANY)] * n,
        out_shape=[jax.ShapeDtypeStruct((N_DEV, *block_shape(a)), a.dtype) for a in sends],
        scratch_shapes=[pltpu.SemaphoreType.DMA((n * peers,)), pltpu.SemaphoreType.DMA((n * peers,)),
                        pltpu.SemaphoreType.DMA((n,))],
    )(*sends)


def _gather_two_level(name, sends):
    n = len(sends)
    per = N_DEV - 1

    def body(*refs):
        send_refs, recv_refs = refs[:n], refs[n:2 * n]
        send_sems, recv_sems, local_sems = refs[2 * n:]
        x, y, c = lax.axis_index("x"), lax.axis_index("y"), lax.axis_index("c")
        sibling = (x, y, 1 - c)
        chips = [(1 - x, y), (x, 1 - y), (1 - x, 1 - y)]

        def idx(px, py, pc):
            return 4 * px + 2 * py + pc

        def copy(a, k, block, to, src=None):
            slot = recv_refs[a].at[idx(*block)]
            return pltpu.make_async_remote_copy(
                src_ref=slot if src is None else src, dst_ref=slot,
                send_sem=send_sems.at[a * per + k], recv_sem=recv_sems.at[a * per + k],
                device_id=to, device_id_type=pl.DeviceIdType.MESH)

        me = (x, y, c)
        owns = [pltpu.make_async_copy(send_refs[a], recv_refs[a].at[idx(*me)], local_sems.at[a]) for a in range(n)]
        for cp in owns:
            cp.start()
        first = [copy(a, 0, me, sibling, src=send_refs[a]) for a in range(n)]
        first += [copy(a, 1 + j, me, (*chip, c), src=send_refs[a]) for j, chip in enumerate(chips) for a in range(n)]
        for cp in first:
            cp.start()
        passed = []
        for j, chip in enumerate(chips):
            for a in range(n):
                copy(a, 1 + j, (*chip, c), me).wait_recv()
                fwd = copy(a, 4 + j, (*chip, c), sibling)
                fwd.start()
                passed.append(fwd)
        for a in range(n):
            copy(a, 0, sibling, me).wait_recv()
            for j, chip in enumerate(chips):
                copy(a, 4 + j, (*chip, 1 - c), me).wait_recv()
        for cp in first + passed:
            cp.wait_send()
        for cp in owns:
            cp.wait()

    return pl.pallas_call(
        body, name=name,
        in_specs=[pl.BlockSpec(memory_space=pl.ANY)] * n, out_specs=[pl.BlockSpec(memory_space=pl.ANY)] * n,
        out_shape=[jax.ShapeDtypeStruct((N_DEV, *a.shape), a.dtype) for a in sends],
        scratch_shapes=[pltpu.SemaphoreType.DMA((n * per,)), pltpu.SemaphoreType.DMA((n * per,)),
                        pltpu.SemaphoreType.DMA((n,))],
    )(*sends)


def _flat_rows(parts, row_multiple):
    flat = jnp.concatenate([p.reshape(-1) for p in parts])
    chunk = row_multiple * LANE
    total = -(-flat.shape[0] // chunk) * chunk
    return jnp.pad(flat, (0, total - flat.shape[0])).reshape(-1, LANE)


def _unflat(flat, shapes):
    flat = flat.reshape(-1)
    out, off = [], 0
    for shp in shapes:
        n = math.prod(shp)
        out.append(flat[off:off + n].reshape(shp))
        off += n
    return out


def _adam_update(g, w, m, v):
    m2 = ADAM_B1 * m + (1.0 - ADAM_B1) * g
    v2 = ADAM_B2 * v + (1.0 - ADAM_B2) * (g * g)
    m_hat = m2 / (1.0 - ADAM_B1 ** ADAM_STEP)
    v_hat = v2 / (1.0 - ADAM_B2 ** ADAM_STEP)
    delta = -ADAM_LR * (m_hat / (jnp.sqrt(v_hat) + ADAM_EPS) + ADAM_WD * w)
    return delta, m2, v2


def _adamw_summed(name, parts, w, m, v):
    r = w.shape[0]
    tr = min(ADAM_ROWS, r)

    def body(p_ref, w_ref, m_ref, v_ref, g_ref, d_ref, m2_ref, v2_ref):
        g = p_ref[0]
        for j in range(1, N_DEV):
            g = g + p_ref[j]
        g_ref[...] = g
        d_ref[...], m2_ref[...], v2_ref[...] = _adam_update(g, w_ref[...], m_ref[...], v_ref[...])

    rows = _rows(tr, LANE)
    return pl.pallas_call(
        body, name=name, grid=(r // tr,),
        in_specs=[pl.BlockSpec((N_DEV, tr, LANE), lambda i: (0, i, 0)), rows, rows, rows],
        out_specs=[rows] * 4, out_shape=[jax.ShapeDtypeStruct((r, LANE), F32)] * 4,
        compiler_params=_cparams("parallel"),
    )(parts, w, m, v)


def _modpart(c_all, w_ada, b_cols):
    def body(c_ref, w_ref, b_ref, o_ref):
        o_ref[...] = _nn(c_ref[...].astype(BF16), w_ref[...].astype(BF16)) + b_ref[...]

    return pl.pallas_call(
        body, name="modpart", out_shape=jax.ShapeDtypeStruct((N_DEV, w_ada.shape[1]), F32),
    )(c_all, w_ada, b_cols)


def _adamw_w_ada(c_all_t, dmod_cols, w, m, v):
    def body(c_ref, d_ref, w_ref, m_ref, v_ref, g_ref, dl_ref, m2_ref, v2_ref):
        g = c_ref[:, 0:1] * d_ref[0:1, :]
        for b in range(1, N_DEV):
            g = g + c_ref[:, b:b + 1] * d_ref[b:b + 1, :]
        g_ref[...] = g
        dl_ref[...], m2_ref[...], v2_ref[...] = _adam_update(g, w_ref[...], m_ref[...], v_ref[...])

    return pl.pallas_call(
        body, name="adamw_w_ada", out_shape=[jax.ShapeDtypeStruct(w.shape, F32)] * 4,
        compiler_params=pltpu.CompilerParams(vmem_limit_bytes=VMEM_LIMIT),
    )(c_all_t, dmod_cols, w, m, v)


W_IN_SHARD = IN_WIDTH // N_DEV
W_IN_SHARD_LANES = -(-W_IN_SHARD // LANE) * LANE
BF16_ROWS = 16
W_IN_SEND_ROWS = -(-W_IN_SHARD // BF16_ROWS) * BF16_ROWS


def _transpose_cast(w_pad):
    def body(w_ref, o_ref):
        o_ref[...] = w_ref[...].T.astype(BF16)

    return pl.pallas_call(
        body, name="w_in_transpose", out_shape=jax.ShapeDtypeStruct(w_pad.shape[::-1], BF16),
        compiler_params=pltpu.CompilerParams(vmem_limit_bytes=VMEM_LIMIT),
    )(w_pad)


def _adamw_w_in(parts, w, m, v):
    rows_t = parts.shape[1]
    d, cols = w.shape
    tb = ROW_TILE

    def body(p_ref, w_ref, m_ref, v_ref, g_ref, d_ref, m2_ref, v2_ref):
        gt = p_ref[0].astype(F32)
        for j in range(1, N_DEV):
            gt = gt + p_ref[j].astype(F32)
        gt = jnp.concatenate([gt, jnp.zeros((W_IN_SHARD_LANES - rows_t, tb), F32)], axis=0)
        g = gt.T[:, :cols]
        g_ref[...] = g
        d_ref[...], m2_ref[...], v2_ref[...] = _adam_update(g, w_ref[...], m_ref[...], v_ref[...])

    blk = _rows(tb, cols)
    return pl.pallas_call(
        body, name="adamw_w_in", grid=(d // tb,),
        in_specs=[pl.BlockSpec((N_DEV, rows_t, tb), lambda i: (0, 0, i)), blk, blk, blk],
        out_specs=[blk] * 4, out_shape=[jax.ShapeDtypeStruct(w.shape, F32)] * 4,
        compiler_params=_cparams("parallel"),
    )(parts, w, m, v)


SHARDED = ("w_qb", "w_kvb", "w_out")
REPLICATED = ("b_ada", "q_norm_g", "kv_norm_g", "conv_b", "dt_bias", "a_log", "d_skip", "ssm_norm_g", "ln_g", "ln_b")
WEIGHTS = ("w_ada", "b_ada", "w_in", "q_norm_g", "w_qb", "kv_norm_g", "w_kvb", "conv_w", "conv_b", "dt_bias",
           "a_log", "d_skip", "ssm_norm_g", "w_out", "ln_g", "ln_b")
HEAD_COLS = QK_NOPE + V_DIM


def _adamw_blocks(name, parts, w, m, v):
    r, c = w.shape
    tr = ROW_TILE if r % ROW_TILE == 0 else r

    def body(p_ref, w_ref, m_ref, v_ref, g_ref, d_ref, m2_ref, v2_ref):
        g = p_ref[0].astype(F32)
        for j in range(1, N_DEV):
            g = g + p_ref[j].astype(F32)
        g_ref[...] = g
        d_ref[...], m2_ref[...], v2_ref[...] = _adam_update(g, w_ref[...], m_ref[...], v_ref[...])

    blk = _rows(tr, c)
    return pl.pallas_call(
        body, name=name, grid=(r // tr,),
        in_specs=[pl.BlockSpec((N_DEV, tr, c), lambda i: (0, i, 0)), blk, blk, blk],
        out_specs=[blk] * 4, out_shape=[jax.ShapeDtypeStruct(w.shape, F32)] * 4,
        compiler_params=_cparams("parallel"),
    )(parts, w, m, v)


def kernel(x, c, positions, w_ada, b_ada, w_in, q_norm_g, w_qb, kv_norm_g, w_kvb, conv_w, conv_b, dt_bias, a_log, d_skip, ssm_norm_g, w_out, ln_g, ln_b, loss_target, m_w_ada, m_b_ada, m_w_in, m_q_norm_g, m_w_qb, m_kv_norm_g, m_w_kvb, m_conv_w, m_conv_b, m_dt_bias, m_a_log, m_d_skip, m_ssm_norm_g, m_w_out, m_ln_g, m_ln_b, v_w_ada, v_b_ada, v_w_in, v_q_norm_g, v_w_qb, v_kv_norm_g, v_w_kvb, v_conv_w, v_conv_b, v_dt_bias, v_a_log, v_d_skip, v_ssm_norm_g, v_w_out, v_ln_g, v_ln_b):
    given = dict(w_ada=w_ada, b_ada=b_ada, w_in=w_in, q_norm_g=q_norm_g, w_qb=w_qb, kv_norm_g=kv_norm_g, w_kvb=w_kvb,
                 conv_w=conv_w, conv_b=conv_b, dt_bias=dt_bias, a_log=a_log, d_skip=d_skip, ssm_norm_g=ssm_norm_g,
                 w_out=w_out, ln_g=ln_g, ln_b=ln_b)
    mom = dict(w_ada=m_w_ada, b_ada=m_b_ada, w_in=m_w_in, q_norm_g=m_q_norm_g, w_qb=m_w_qb, kv_norm_g=m_kv_norm_g,
               w_kvb=m_w_kvb, conv_w=m_conv_w, conv_b=m_conv_b, dt_bias=m_dt_bias, a_log=m_a_log, d_skip=m_d_skip,
               ssm_norm_g=m_ssm_norm_g, w_out=m_w_out, ln_g=m_ln_g, ln_b=m_ln_b)
    var = dict(w_ada=v_w_ada, b_ada=v_b_ada, w_in=v_w_in, q_norm_g=v_q_norm_g, w_qb=v_w_qb, kv_norm_g=v_kv_norm_g,
               w_kvb=v_w_kvb, conv_w=v_conv_w, conv_b=v_conv_b, dt_bias=v_dt_bias, a_log=v_a_log, d_skip=v_d_skip,
               ssm_norm_g=v_ssm_norm_g, w_out=v_w_out, ln_g=v_ln_g, ln_b=v_ln_b)
    w0 = {k: a[0] for k, a in given.items()}
    m0 = {k: a[0] for k, a in mom.items()}
    v0 = {k: a[0] for k, a in var.items()}
    me = _my_index()

    w_in_rows = _transpose_cast(jnp.pad(w0["w_in"], ((0, 0), (0, W_IN_SHARD_LANES - W_IN_SHARD))))
    g_w_in, g_w_qb, g_w_kvb, g_w_out, g_conv_w, c_all = _gather_two_level(
        "gather_weights", [w_in_rows] + [w0[k].astype(BF16) for k in SHARDED] + [w0["conv_w"], c])
    c_all = c_all.reshape(N_DEV, D_MODEL)
    w_in_t = g_w_in[:, :W_IN_SHARD, :].reshape(IN_WIDTH, D_MODEL)
    w_qb_p = jnp.pad(g_w_qb, ((0, 0), (0, 0), (0, HEAD_PAD - QK_HEAD))).transpose(1, 0, 2).reshape(Q_RANK, -1)
    w_kvb_p = g_w_kvb.reshape(N_DEV, KV_RANK, 2, QK_NOPE).transpose(1, 2, 0, 3).reshape(KV_RANK, -1)
    w_out_b = g_w_out.reshape(MIX_WIDTH, D_MODEL)
    conv_w_full = g_conv_w.transpose(1, 0, 2).reshape(CONV_K, CONV_CH)

    ada_cols = w0["w_ada"].shape[1]
    b_cols = lax.dynamic_slice(w0["b_ada"], (me * ada_cols,), (ada_cols,)).reshape(1, ada_cols)
    mod_all, = _exchange("gather_mod", [_modpart(c_all, w0["w_ada"], b_cols)], gather=True)
    mod = lax.dynamic_index_in_dim(mod_all, me, axis=1, keepdims=False).reshape(-1)

    loc = _local_step(x[0], loss_target[0], positions[0], mod, w_in_t, w0["q_norm_g"], w_qb_p,
                      w0["kv_norm_g"], w_kvb_p, conv_w_full, w0["conv_b"], w0["dt_bias"], w0["a_log"],
                      w0["d_skip"], w0["ssm_norm_g"], w_out_b, w0["ln_g"], w0["ln_b"])

    rep_shapes = [w0[k].shape for k in REPLICATED] + [(1,)]
    rep_local = [loc["dmod"]] + [loc[k] for k in REPLICATED[1:]] + [loc["loss"].reshape(1)]
    rep_parts, conv_parts = _exchange("gather_small", [_flat_rows(rep_local, HALO), loc["conv_w"]], gather=True)
    conv_cols = w0["conv_w"].shape[1]
    conv_mine = lax.dynamic_slice(conv_parts, (0, 0, me * conv_cols), (N_DEV, CONV_K, conv_cols))
    outs = {"conv_w": _adamw_blocks("adamw_conv_w", conv_mine, w0["conv_w"], m0["conv_w"], v0["conv_w"])}
    zero1 = jnp.zeros((1,), F32)
    rep = _adamw_summed("adamw_replicated", rep_parts,
                        _flat_rows([w0[k] for k in REPLICATED] + [zero1], HALO),
                        _flat_rows([m0[k] for k in REPLICATED] + [zero1], HALO),
                        _flat_rows([v0[k] for k in REPLICATED] + [zero1], HALO))
    rep_g, rep_d, rep_m, rep_v = [_unflat(a, rep_shapes) for a in rep]
    loss = rep_g[-1][0]

    dmod_all = rep_parts.reshape(N_DEV, -1)[:, :3 * D_MODEL]
    dmod_cols = lax.dynamic_slice(dmod_all, (0, me * ada_cols), (N_DEV, ada_cols))
    outs["w_ada"] = _adamw_w_ada(c_all.T, dmod_cols, w0["w_ada"], m0["w_ada"], v0["w_ada"])

    send_w_in = loc["w_in_t"].astype(BF16).reshape(N_DEV, W_IN_SHARD, D_MODEL)
    send_w_in = jnp.pad(send_w_in, ((0, 0), (0, W_IN_SEND_ROWS - W_IN_SHARD), (0, 0)))
    send_w_qb = loc["w_qb"].astype(BF16).reshape(Q_RANK, N_DEV, HEAD_PAD)[:, :, :QK_HEAD].transpose(1, 0, 2)
    send_w_kvb = loc["w_kvb"].astype(BF16).reshape(KV_RANK, 2, N_DEV, QK_NOPE).transpose(2, 0, 1, 3)
    send_w_kvb = send_w_kvb.reshape(N_DEV, KV_RANK, HEAD_COLS)
    send_w_out = loc["w_out"].astype(BF16).reshape(N_DEV, MIX_WIDTH // N_DEV, D_MODEL)
    r_w_in, r_w_qb, r_w_kvb, r_w_out = _exchange(
        "scatter_grads", [send_w_in, send_w_qb, send_w_kvb, send_w_out], gather=False)
    outs["w_in"] = _adamw_w_in(r_w_in, w0["w_in"], m0["w_in"], v0["w_in"])
    for k, parts in zip(SHARDED, (r_w_qb, r_w_kvb, r_w_out)):
        outs[k] = _adamw_blocks("adamw_" + k, parts, w0[k], m0[k], v0[k])

    def collect(idx):
        out = {k: o[idx] for k, o in outs.items()}
        out.update({k: (rep_g, rep_d, rep_m, rep_v)[idx][i] for i, k in enumerate(REPLICATED)})
        return [out[k][None] for k in WEIGHTS]

    return (loss, loc["grad_x"][None], *collect(0), *collect(1), *collect(2), *collect(3))
```

```python
import functools
import math

import jax
import jax.numpy as jnp
from jax import lax
from jax.experimental import pallas as pl
from jax.experimental.pallas import tpu as pltpu

F32 = jnp.float32
BF16 = jnp.bfloat16

N_DEV = 8
D_MODEL = 1024
MLA_HEADS = 8
QK_NOPE = 128
QK_ROPE = 64
V_DIM = 128
Q_RANK = 384
KV_RANK = 256
QK_HEAD = QK_NOPE + QK_ROPE
HEAD_PAD = 256
ROPE_HALF = QK_ROPE // 2
ROPE_THETA = 10000.0
MLA_WIDTH = MLA_HEADS * V_DIM
SSM_HEADS = 16
SSM_P = 64
SSM_WIDTH = SSM_HEADS * SSM_P
SSM_GROUPS = 2
SSM_N = 128
CONV_K = 4
CHUNK = 128
CONV_CH = SSM_WIDTH + 2 * SSM_GROUPS * SSM_N
MIX_WIDTH = MLA_WIDTH + SSM_WIDTH
IN_SPLITS = (Q_RANK, KV_RANK + QK_ROPE, MLA_WIDTH, CONV_CH, SSM_HEADS, SSM_WIDTH)
IN_WIDTH = sum(IN_SPLITS)
LANE = 128
KV_LAT_PAD = KV_RANK + LANE
IN_PAD = (Q_RANK, KV_LAT_PAD, MLA_WIDTH, CONV_CH, LANE, SSM_WIDTH)
IN_PAD_WIDTH = sum(IN_PAD)
DEEPNORM_ALPHA = 2.0 ** 0.25
RMS_EPS = 1e-6
LN_EPS = 1e-5
ATTN_SCALE = QK_HEAD ** -0.5
LOG2E = math.log2(math.e)
LN2 = math.log(2.0)
Q_PRESCALE = ATTN_SCALE * LOG2E
ADAM_LR, ADAM_B1, ADAM_B2, ADAM_EPS, ADAM_WD, ADAM_STEP = 0.001, 0.9, 0.999, 1e-08, 0.01, 10

ROW_TILE = 512
ROW_TILE_WIDE = 256
ATTN_TILE = 512
ATTN_UNROLLS = (8, 4, 2)
SSD_ROWS = 512
GRAD_ROWS = 2048
VMEM_LIMIT = 56 * 1024 * 1024


def _nn(a, b):
    return jnp.dot(a, b, preferred_element_type=F32)


def _nt(a, b):
    return lax.dot_general(a, b, (((1,), (1,)), ((), ())), preferred_element_type=F32)


def _tn(a, b):
    return lax.dot_general(a, b, (((0,), (0,)), ((), ())), preferred_element_type=F32)


def _cparams(*sem):
    return pltpu.CompilerParams(dimension_semantics=sem, vmem_limit_bytes=VMEM_LIMIT)


def _rows(tm, w):
    return pl.BlockSpec((tm, w), lambda i: (i, 0))


def _whole(shape):
    return pl.BlockSpec(shape, lambda i: (0,) * len(shape))


def _sigmoid(z):
    return 1.0 / (1.0 + jnp.exp(-z))


def _lane_iota(shape):
    return lax.broadcasted_iota(jnp.int32, shape, len(shape) - 1)


def _swap_halves(r):
    lane = _lane_iota(r.shape)
    return jnp.where(lane < ROPE_HALF, pltpu.roll(r, LANE - ROPE_HALF, 1),
                     jnp.where(lane < QK_ROPE, pltpu.roll(r, ROPE_HALF, 1), 0.0))


def _rope(r, cos, sin):
    return r * cos + _swap_halves(r) * sin


def _rope_transposed(d, cos, sin):
    return d * cos + _swap_halves(d * sin)


def _rms(x):
    rstd = lax.rsqrt(jnp.mean(x * x, axis=-1, keepdims=True) + RMS_EPS)
    return x * rstd, rstd


def _rms_bwd(dxhat, xhat, rstd):
    return rstd * (dxhat - xhat * jnp.mean(dxhat * xhat, axis=-1, keepdims=True))


def _acc_rows(ref, val):
    @pl.when(pl.program_id(0) == 0)
    def _():
        ref[...] = jnp.zeros_like(ref)
    ref[...] += val


def _colsum(v):
    return jnp.sum(v, axis=0, keepdims=True)


def _inproj(x, scale1p, shift, w_in_pt):
    s = x.shape[0]
    tm = ROW_TILE

    def body(x_ref, sc_ref, sh_ref, w_ref, u_ref, *outs):
        u = (x_ref[...] * sc_ref[...] + sh_ref[...]).astype(BF16)
        u_ref[...] = u
        proj = _nt(u, w_ref[...])
        off = 0
        for ref, w in zip(outs, IN_PAD):
            ref[...] = proj[:, off:off + w]
            off += w

    return pl.pallas_call(
        body, name="inproj", grid=(s // tm,),
        in_specs=[_rows(tm, D_MODEL), _whole((1, D_MODEL)), _whole((1, D_MODEL)), _whole((IN_PAD_WIDTH, D_MODEL))],
        out_specs=[_rows(tm, D_MODEL)] + [_rows(tm, w) for w in IN_PAD],
        out_shape=[jax.ShapeDtypeStruct((s, D_MODEL), BF16)] + [jax.ShapeDtypeStruct((s, w), F32) for w in IN_PAD],
        compiler_params=_cparams("parallel"),
    )(x, scale1p, shift, w_in_pt)


def _qpath(q_lat, g_q, w_qb_p, cos, sin):
    s = q_lat.shape[0]
    tm = ROW_TILE

    def body(ql_ref, g_ref, w_ref, cos_ref, sin_ref, nq_ref, q_ref):
        xhat, _ = _rms(ql_ref[...])
        nq = (xhat * g_ref[...]).astype(BF16)
        nq_ref[...] = nq
        raw = _nn(nq, w_ref[...]) * Q_PRESCALE
        c, sn = cos_ref[...], sin_ref[...]
        for h in range(MLA_HEADS):
            o = h * HEAD_PAD
            q_ref[:, o:o + QK_NOPE] = raw[:, o:o + QK_NOPE].astype(BF16)
            q_ref[:, o + QK_NOPE:o + HEAD_PAD] = _rope(raw[:, o + QK_NOPE:o + HEAD_PAD], c, sn).astype(BF16)

    return pl.pallas_call(
        body, name="qpath", grid=(s // tm,),
        in_specs=[_rows(tm, Q_RANK), _whole((1, Q_RANK)), _whole((Q_RANK, MLA_HEADS * HEAD_PAD)),
                  _rows(tm, LANE), _rows(tm, LANE)],
        out_specs=[_rows(tm, Q_RANK), _rows(tm, MLA_HEADS * HEAD_PAD)],
        out_shape=[jax.ShapeDtypeStruct((s, Q_RANK), BF16), jax.ShapeDtypeStruct((s, MLA_HEADS * HEAD_PAD), BF16)],
        compiler_params=_cparams("parallel"),
    )(q_lat, g_q, w_qb_p, cos, sin)


def _kvpath(kv_lat, g_kv, w_kvb_p, cos, sin):
    s = kv_lat.shape[0]
    tm = ROW_TILE

    def body(kl_ref, g_ref, w_ref, cos_ref, sin_ref, nkv_ref, k_ref, v_ref):
        kl = kl_ref[...]
        xhat, _ = _rms(kl[:, :KV_RANK])
        nkv = (xhat * g_ref[...]).astype(BF16)
        nkv_ref[...] = nkv
        raw = _nn(nkv, w_ref[...])
        kr = _rope(kl[:, KV_RANK:], cos_ref[...], sin_ref[...]).astype(BF16)
        for h in range(MLA_HEADS):
            o = h * HEAD_PAD
            k_ref[:, o:o + QK_NOPE] = raw[:, h * QK_NOPE:(h + 1) * QK_NOPE].astype(BF16)
            k_ref[:, o + QK_NOPE:o + HEAD_PAD] = kr
        v_ref[...] = raw[:, MLA_HEADS * QK_NOPE:].astype(BF16)

    return pl.pallas_call(
        body, name="kvpath", grid=(s // tm,),
        in_specs=[_rows(tm, KV_LAT_PAD), _whole((1, KV_RANK)), _whole((KV_RANK, MLA_HEADS * (QK_NOPE + V_DIM))),
                  _rows(tm, LANE), _rows(tm, LANE)],
        out_specs=[_rows(tm, KV_RANK), _rows(tm, MLA_HEADS * HEAD_PAD), _rows(tm, MLA_WIDTH)],
        out_shape=[jax.ShapeDtypeStruct((s, KV_RANK), BF16), jax.ShapeDtypeStruct((s, MLA_HEADS * HEAD_PAD), BF16),
                   jax.ShapeDtypeStruct((s, MLA_WIDTH), BF16)],
        compiler_params=_cparams("parallel"),
    )(kv_lat, g_kv, w_kvb_p, cos, sin)


def _causal_mask(t):
    row = lax.broadcasted_iota(jnp.int32, (t, t), 0)
    col = lax.broadcasted_iota(jnp.int32, (t, t), 1)
    return row, col


def _attn_fwd(q, k, v):
    s = q.shape[0]
    t = min(ATTN_TILE, s)
    nq = s // t

    def body(q_ref, k_ref, v_ref, o_ref, lse_ref, m_sc, l_sc, acc_sc, sa_sc, sb_sc):
        i = pl.program_id(1)
        qv = q_ref[...]
        m_sc[...] = jnp.full(m_sc.shape, -jnp.inf, F32)
        l_sc[...] = jnp.zeros(l_sc.shape, F32)
        acc_sc[...] = jnp.zeros(acc_sc.shape, F32)

        def scores(j, s_ref):
            s_ref[...] = _nt(qv, k_ref[pl.ds(pl.multiple_of(j * t, t), t), :])

        def update(s_ref, j, masked):
            vv = v_ref[pl.ds(pl.multiple_of(j * t, t), t), :]
            sc = s_ref[...]
            if masked:
                row, col = _causal_mask(t)
                sc = jnp.where(col <= row, sc, -jnp.inf)
            m_prev = m_sc[...]
            m_new = jnp.maximum(m_prev, jnp.max(sc, axis=1, keepdims=True))
            alpha = jnp.exp2(m_prev - m_new)
            p = jnp.exp2(sc - jnp.tile(m_new, (1, t // LANE)))
            l_sc[...] = alpha * l_sc[...] + jnp.sum(p, axis=1, keepdims=True)
            acc_sc[...] = alpha * acc_sc[...] + _nn(p.astype(BF16), vv)
            m_sc[...] = m_new

        def run(j0, count):
            bufs = (sa_sc, sb_sc)
            for u in range(count):
                scores(j0 + u + 1, bufs[(u + 1) % 2])
                update(bufs[u % 2], j0 + u, False)

        scores(0, sa_sc)
        done = 0
        for group in ATTN_UNROLLS:
            def body_(g, carry, base=done, group=group):
                run(base + group * g, group)
                return carry

            n_groups = lax.div(i - done, group)
            lax.fori_loop(0, n_groups, body_, 0)
            done = done + group * n_groups
        odd = lax.rem(i, 2)

        @pl.when(odd == 1)
        def _():
            scores(i, sb_sc)
            update(sa_sc, i - 1, False)
            update(sb_sc, i, True)

        @pl.when(odd == 0)
        def _():
            update(sa_sc, i, True)

        l = l_sc[...]
        o_ref[...] = acc_sc[...] / l
        lse_ref[0] = (m_sc[...] + jnp.log2(l)).T[0:1, :]

    return pl.pallas_call(
        body, name="attn_fwd", grid=(MLA_HEADS, nq),
        in_specs=[pl.BlockSpec((t, HEAD_PAD), lambda h, i: (i, h)),
                  pl.BlockSpec((s, HEAD_PAD), lambda h, i: (0, h)),
                  pl.BlockSpec((s, V_DIM), lambda h, i: (0, h))],
        out_specs=[pl.BlockSpec((t, V_DIM), lambda h, i: (i, h)), pl.BlockSpec((1, 1, t), lambda h, i: (h, 0, i))],
        out_shape=[jax.ShapeDtypeStruct((s, MLA_WIDTH), F32), jax.ShapeDtypeStruct((MLA_HEADS, 1, s), F32)],
        scratch_shapes=[pltpu.VMEM((t, LANE), F32), pltpu.VMEM((t, LANE), F32), pltpu.VMEM((t, V_DIM), F32),
                        pltpu.VMEM((t, t), F32), pltpu.VMEM((t, t), F32)],
        compiler_params=_cparams("parallel", "arbitrary"),
    )(q, k, v)


def _attn_bwd(q, k, v, do, lse_row, delta_row):
    s = q.shape[0]
    t = min(ATTN_TILE, s)
    nq = s // t

    def body(q_ref, k_ref, v_ref, do_ref, lse_ref, dl_ref, dk_ref, dv_ref, dq_hbm,
             dq_sc, dk_sc, dv_sc, sa_sc, sb_sc, pa_sc, pb_sc, sem, stage_sc):
        h = pl.program_id(0)
        j = pl.program_id(1)
        kv_ = k_ref[...]
        vv = v_ref[...]

        @pl.when(j == 0)
        def _():
            dq_sc[...] = jnp.zeros(dq_sc.shape, F32)

        dk_sc[...] = jnp.zeros(dk_sc.shape, F32)
        dv_sc[...] = jnp.zeros(dv_sc.shape, F32)

        def scores(i, s_ref, p_ref):
            off = pl.multiple_of(i * t, t)
            s_ref[...] = _nt(kv_, q_ref[pl.ds(off, t), :])
            p_ref[...] = _nt(vv, do_ref[pl.ds(off, t), :])

        def update(i, s_ref, p_ref, masked):
            off = pl.multiple_of(i * t, t)
            qv = q_ref[pl.ds(off, t), :]
            dov = do_ref[pl.ds(off, t), :]
            sct = s_ref[...]
            if masked:
                row, col = _causal_mask(t)
                sct = jnp.where(row <= col, sct, -jnp.inf)
            pt = jnp.exp2(sct - lse_ref[0, :, pl.ds(off, t)])
            gt = (pt * (p_ref[...] - dl_ref[0, :, pl.ds(off, t)])).astype(BF16)
            dv_sc[...] += _nn(pt.astype(BF16), dov)
            dk_sc[...] += _nn(gt, qv)
            dq_sc[pl.ds(off, t), :] += _tn(gt, kv_)

        rest = nq - 1 - j
        scores(j, sa_sc, pa_sc)

        @pl.when(rest >= 1)
        def _():
            scores(j + 1, sb_sc, pb_sc)

        update(j, sa_sc, pa_sc, True)

        def run(i0, count):
            bufs = ((sb_sc, pb_sc), (sa_sc, pa_sc))
            for u in range(count):
                scores(i0 + u + 1, *bufs[(u + 1) % 2])
                update(i0 + u, *bufs[u % 2], False)

        i1, left = j + 1, rest
        for group in ATTN_UNROLLS:
            def body_(g, carry, base=i1, group=group):
                run(base + group * g, group)
                return carry

            n_groups = jnp.where(left >= 1, lax.div(left - 1, group), 0)
            lax.fori_loop(0, n_groups, body_, 0)
            i1 = i1 + group * n_groups
            left = left - group * n_groups

        @pl.when(left == 1)
        def _():
            update(i1, sb_sc, pb_sc, False)

        @pl.when(left == 2)
        def _():
            scores(i1 + 1, sa_sc, pa_sc)
            update(i1, sb_sc, pb_sc, False)
            update(i1 + 1, sa_sc, pa_sc, False)

        dk_ref[...] = (dk_sc[...] * LN2).astype(BF16)
        dv_ref[...] = dv_sc[...].astype(BF16)

        def out_copy(jj):
            rows = pl.ds(pl.multiple_of(jj * t, t), t)
            return pltpu.make_async_copy(stage_sc, dq_hbm.at[h, rows, :], sem)

        @pl.when(j > 0)
        def _():
            out_copy(j - 1).wait()

        stage_sc[...] = dq_sc[pl.ds(pl.multiple_of(j * t, t), t), :].astype(BF16)
        out_copy(j).start()

        @pl.when(j == nq - 1)
        def _():
            out_copy(j).wait()

    return pl.pallas_call(
        body, name="attn_bwd", grid=(MLA_HEADS, nq),
        in_specs=[pl.BlockSpec((s, HEAD_PAD), lambda h, j: (0, h)),
                  pl.BlockSpec((t, HEAD_PAD), lambda h, j: (j, h)),
                  pl.BlockSpec((t, V_DIM), lambda h, j: (j, h)),
                  pl.BlockSpec((s, V_DIM), lambda h, j: (0, h)),
                  pl.BlockSpec((1, 1, s), lambda h, j: (h, 0, 0)),
                  pl.BlockSpec((1, 1, s), lambda h, j: (h, 0, 0))],
        out_specs=[pl.BlockSpec((t, HEAD_PAD), lambda h, j: (j, h)), pl.BlockSpec((t, V_DIM), lambda h, j: (j, h)),
                   pl.BlockSpec(memory_space=pl.ANY)],
        out_shape=[jax.ShapeDtypeStruct((s, MLA_HEADS * HEAD_PAD), BF16), jax.ShapeDtypeStruct((s, MLA_WIDTH), BF16),
                   jax.ShapeDtypeStruct((MLA_HEADS, s, HEAD_PAD), BF16)],
        scratch_shapes=[pltpu.VMEM((s, HEAD_PAD), F32), pltpu.VMEM((t, HEAD_PAD), F32), pltpu.VMEM((t, V_DIM), F32),
                        pltpu.VMEM((t, t), F32), pltpu.VMEM((t, t), F32), pltpu.VMEM((t, t), F32),
                        pltpu.VMEM((t, t), F32), pltpu.SemaphoreType.DMA, pltpu.VMEM((t, HEAD_PAD), BF16)],
        compiler_params=_cparams("arbitrary", "arbitrary"),
    )(q, k, v, do, lse_row, delta_row)


HALO = 8


def _silu(z):
    return z * _sigmoid(z)


def _silu_grad(z):
    sg = _sigmoid(z)
    return sg * (1.0 + z * (1.0 - sg))


def _softplus(x):
    e = jnp.exp(-jnp.abs(x))
    small = e * (1.0 - e * (0.5 - e * (1.0 / 3.0)))
    return jnp.maximum(x, 0.0) + jnp.where(e < 1e-3, small, jnp.log(1.0 + e))


def _conv_taps(xe_ref, w, tm, first):
    acc = None
    for k in range(CONV_K):
        term = xe_ref[pl.ds(HALO + first - (CONV_K - 1) + k, tm), :] * w[k:k + 1, :]
        acc = term if acc is None else acc + term
    return acc


def _ssd_pre(xbc_raw, dt_raw, conv_w, conv_b, dt_bias_p):
    s = xbc_raw.shape[0]
    tm = ROW_TILE
    hb = tm // HALO

    def body(x_ref, prev_ref, dtr_ref, w_ref, b_ref, db_ref, act_ref, dt_ref, xe_sc):
        i = pl.program_id(0)
        xe_sc[pl.ds(0, HALO), :] = jnp.where(i > 0, prev_ref[...], 0.0)
        xe_sc[pl.ds(HALO, tm), :] = x_ref[...]
        pre = _conv_taps(xe_sc, w_ref[...], tm, 0) + b_ref[...]
        act_ref[...] = _silu(pre)
        dt_ref[...] = _softplus(dtr_ref[...] + db_ref[...])

    return pl.pallas_call(
        body, name="ssd_pre", grid=(s // tm,),
        in_specs=[_rows(tm, CONV_CH), pl.BlockSpec((HALO, CONV_CH), lambda i: (jnp.maximum(i * hb - 1, 0), 0)),
                  _rows(tm, LANE), _whole((CONV_K, CONV_CH)), _whole((1, CONV_CH)), _whole((1, LANE))],
        out_specs=[_rows(tm, CONV_CH), _rows(tm, LANE)],
        out_shape=[jax.ShapeDtypeStruct((s, CONV_CH), F32), jax.ShapeDtypeStruct((s, LANE), F32)],
        scratch_shapes=[pltpu.VMEM((tm + HALO, CONV_CH), F32)],
        compiler_params=_cparams("parallel"),
    )(xbc_raw, xbc_raw, dt_raw, conv_w, conv_b, dt_bias_p)


def _split3(a):
    a1 = a.astype(BF16)
    r1 = a - a1.astype(F32)
    a2 = r1.astype(BF16)
    a3 = (r1 - a2.astype(F32)).astype(BF16)
    return a1, a2, a3


def _tri_left(tri, a):
    a1, a2, a3 = _split3(a)
    return _nn(tri, a1) + _nn(tri, a2) + _nn(tri, a3)


def _tri_right(a, tri):
    a1, a2, a3 = _split3(a)
    return _nn(a1, tri) + _nn(a2, tri) + _nn(a3, tri)


def _pair_sel(lane_lo, col_a, col_b):
    return jnp.where(lane_lo, col_a, col_b)


def _chunk_common(dt, a_neg, tril, triu):
    a = dt * a_neg
    lam_c = _tri_left(tril, a)
    lam_r = _tri_right(a.T, triu)
    lam_last = lam_c[CHUNK - 1:CHUNK, :]
    return lam_c, lam_r, lam_last


def _gated_norm_fwd(y, z, g):
    hf = y * _silu(z)
    outs = []
    for grp in range(SSM_GROUPS):
        w = SSM_WIDTH // SSM_GROUPS
        n, _ = _rms(hf[:, grp * w:(grp + 1) * w])
        outs.append(n)
    return jnp.concatenate(outs, axis=1) * g


def _ssd_fwd(xbc, dt, z, a_neg, dskip_x, g_x, tril, triu):
    s = xbc.shape[0]
    tm = min(SSD_ROWS, s)
    cpb = tm // CHUNK
    nc = s // CHUNK

    def body(xbc_ref, dt_ref, z_ref, a_ref, dsk_ref, g_ref, tril_ref, triu_ref, y_ref, o_ref, hin_ref, h_sc):
        @pl.when(pl.program_id(0) == 0)
        def _():
            h_sc[...] = jnp.zeros(h_sc.shape, F32)

        tril, triu = tril_ref[...], triu_ref[...]
        ltri = tril > 0
        lane_lo = _lane_iota((CHUNK, LANE)) < SSM_P

        def chunk(c, carry):
            r0 = pl.multiple_of(c * CHUNK, CHUNK)
            dtc = dt_ref[pl.ds(r0, CHUNK), :]
            lam_c, lam_r, lam_last = _chunk_common(dtc, a_ref[...], tril, triu)
            e_c = jnp.exp(lam_c)
            f_r = jnp.exp(lam_r[:, CHUNK - 1:CHUNK] - lam_r)
            cd = jnp.exp(lam_last)
            for grp in range(SSM_GROUPS):
                bo = SSM_WIDTH + grp * SSM_N
                co = SSM_WIDTH + SSM_GROUPS * SSM_N + grp * SSM_N
                bm = xbc_ref[pl.ds(r0, CHUNK), bo:bo + SSM_N]
                cm = xbc_ref[pl.ds(r0, CHUNK), co:co + SSM_N]
                cm_b = cm.astype(BF16)
                gmat = _nt(cm_b, bm.astype(BF16))
                bt = bm.T
                for pj in range(SSM_HEADS // SSM_GROUPS // 2):
                    ha = grp * (SSM_HEADS // SSM_GROUPS) + 2 * pj
                    hb_ = ha + 1
                    lo = ha * SSM_P
                    xs = xbc_ref[pl.ds(r0, CHUNK), lo:lo + LANE]
                    x2 = xs * _pair_sel(lane_lo, dtc[:, ha:ha + 1], dtc[:, hb_:hb_ + 1])
                    x2b = x2.astype(BF16)
                    ys, sts = [], []
                    for hh in (ha, hb_):
                        seg = lam_c[:, hh:hh + 1] - lam_r[hh:hh + 1, :]
                        dec = jnp.exp(jnp.where(ltri, seg, -jnp.inf))
                        ys.append(_nn((gmat * dec).astype(BF16), x2b))
                        sts.append(_nn((bt * f_r[hh:hh + 1, :]).astype(BF16), x2b))
                    hp = h_sc[:, lo:lo + LANE]
                    hin_ref[c, :, lo:lo + LANE] = hp
                    zz = _nn(cm_b, hp.astype(BF16))
                    e2 = _pair_sel(lane_lo, e_c[:, ha:ha + 1], e_c[:, hb_:hb_ + 1])
                    yv = jnp.where(lane_lo, ys[0], ys[1]) + e2 * zz
                    y_ref[pl.ds(r0, CHUNK), lo:lo + LANE] = yv + xs * dsk_ref[:, lo:lo + LANE]
                    cd2 = _pair_sel(lane_lo, cd[:, ha:ha + 1], cd[:, hb_:hb_ + 1])
                    h_sc[:, lo:lo + LANE] = hp * cd2 + jnp.where(lane_lo, sts[0], sts[1])
            return carry

        lax.fori_loop(0, cpb, chunk, 0)
        o_ref[...] = _gated_norm_fwd(y_ref[...], z_ref[...], g_ref[...])

    return pl.pallas_call(
        body, name="ssd_fwd", grid=(s // tm,),
        in_specs=[_rows(tm, CONV_CH), _rows(tm, LANE), _rows(tm, SSM_WIDTH), _whole((1, LANE)),
                  _whole((1, SSM_WIDTH)), _whole((1, SSM_WIDTH)), _whole((CHUNK, CHUNK)), _whole((CHUNK, CHUNK))],
        out_specs=[_rows(tm, SSM_WIDTH), _rows(tm, SSM_WIDTH),
                   pl.BlockSpec((cpb, SSM_N, SSM_WIDTH), lambda i: (i, 0, 0))],
        out_shape=[jax.ShapeDtypeStruct((s, SSM_WIDTH), F32), jax.ShapeDtypeStruct((s, SSM_WIDTH), F32),
                   jax.ShapeDtypeStruct((nc, SSM_N, SSM_WIDTH), F32)],
        scratch_shapes=[pltpu.VMEM((SSM_N, SSM_WIDTH), F32)],
        compiler_params=_cparams("arbitrary"),
    )(xbc, dt, z, a_neg, dskip_x, g_x, tril, triu)


def _outln(o, z_attn, o_ssm, w_out, x, gate, ln_g, ln_b, tgt):
    s = x.shape[0]
    tm = min(ROW_TILE_WIDE, s)

    def body(o_ref, z_ref, os_ref, w_ref, x_ref, gate_ref, g_ref, b_ref, t_ref,
             cat_ref, dmix_ref, gx_ref, do_ref, dz_ref, dl_ref, dos_ref, loss_ref, dg_ref, db_ref, dgate_ref):
        ov, zv = o_ref[...], z_ref[...]
        sz = _silu(zv)
        cat_ref[:, :MLA_WIDTH] = (ov * sz).astype(BF16)
        cat_ref[:, MLA_WIDTH:] = os_ref[...].astype(BF16)
        w = w_ref[...]
        mixed = _nn(cat_ref[...], w)
        gate_v = gate_ref[...]
        hv = DEEPNORM_ALPHA * x_ref[...] + gate_v * mixed
        mu = jnp.mean(hv, axis=-1, keepdims=True)
        hc = hv - mu
        rstd = lax.rsqrt(jnp.mean(hc * hc, axis=-1, keepdims=True) + LN_EPS)
        xhat = hc * rstd
        g = g_ref[...]
        err = xhat * g + b_ref[...] - t_ref[...]
        _acc_rows(loss_ref, jnp.full((1, LANE), (0.5 / D_MODEL) * jnp.sum(err * err), F32))
        dy = err * (1.0 / D_MODEL)
        _acc_rows(dg_ref, _colsum(dy * xhat))
        _acc_rows(db_ref, _colsum(dy))
        dxhat = dy * g
        dh = rstd * (dxhat - jnp.mean(dxhat, axis=-1, keepdims=True)
                     - xhat * jnp.mean(dxhat * xhat, axis=-1, keepdims=True))
        gx_ref[...] = DEEPNORM_ALPHA * dh
        _acc_rows(dgate_ref, _colsum(dh * mixed))
        dmix = (gate_v * dh).astype(BF16)
        dmix_ref[...] = dmix
        dcat = _nt(dmix, w)
        da = dcat[:, :MLA_WIDTH]
        dos_ref[...] = dcat[:, MLA_WIDTH:]
        dov = da * sz
        do_ref[...] = dov.astype(BF16)
        dz_ref[...] = da * ov * _silu_grad(zv)
        prod = dov * ov
        for h in range(MLA_HEADS):
            dsum = jnp.sum(prod[:, h * V_DIM:(h + 1) * V_DIM], axis=1, keepdims=True)
            dl_ref[h] = jnp.broadcast_to(dsum, (tm, LANE)).T[0:1, :]

    vec = _whole((1, D_MODEL))
    return pl.pallas_call(
        body, name="outln", grid=(s // tm,),
        in_specs=[_rows(tm, MLA_WIDTH), _rows(tm, MLA_WIDTH), _rows(tm, SSM_WIDTH), _whole((MIX_WIDTH, D_MODEL)),
                  _rows(tm, D_MODEL), vec, vec, vec, _rows(tm, D_MODEL)],
        out_specs=[_rows(tm, MIX_WIDTH), _rows(tm, D_MODEL), _rows(tm, D_MODEL), _rows(tm, MLA_WIDTH),
                   _rows(tm, MLA_WIDTH), pl.BlockSpec((MLA_HEADS, 1, tm), lambda i: (0, 0, i)), _rows(tm, SSM_WIDTH),
                   _whole((1, LANE)), vec, vec, vec],
        out_shape=[jax.ShapeDtypeStruct((s, MIX_WIDTH), BF16), jax.ShapeDtypeStruct((s, D_MODEL), BF16),
                   jax.ShapeDtypeStruct((s, D_MODEL), F32), jax.ShapeDtypeStruct((s, MLA_WIDTH), BF16),
                   jax.ShapeDtypeStruct((s, MLA_WIDTH), F32), jax.ShapeDtypeStruct((MLA_HEADS, 1, s), F32),
                   jax.ShapeDtypeStruct((s, SSM_WIDTH), F32), jax.ShapeDtypeStruct((1, LANE), F32),
                   jax.ShapeDtypeStruct((1, D_MODEL), F32), jax.ShapeDtypeStruct((1, D_MODEL), F32),
                   jax.ShapeDtypeStruct((1, D_MODEL), F32)],
        compiler_params=_cparams("arbitrary"),
    )(o, z_attn, o_ssm, w_out, x, gate, ln_g, ln_b, tgt)


def _ssd_bwd(dos, y, z, xbc, dt, hin, a_neg, dskip_x, g_x, tril, triu, expand):
    s = xbc.shape[0]
    tm = min(SSD_ROWS, s)
    cpb = tm // CHUNK
    nb = s // tm
    gw = SSM_WIDTH // SSM_GROUPS
    hpg = SSM_HEADS // SSM_GROUPS

    def body(dos_ref, y_ref, z_ref, xbc_ref, dt_ref, hin_ref, a_ref, dsk_ref, g_ref, tril_ref, triu_ref, exp_ref,
             dxbc_ref, ddt_ref, dz_ref, dg_ref, ddsk_ref, da_ref, dh_sc, dy_sc):
        @pl.when(pl.program_id(0) == 0)
        def _():
            dh_sc[...] = jnp.zeros(dh_sc.shape, F32)

        yv, zv, dov = y_ref[...], z_ref[...], dos_ref[...]
        sz = _silu(zv)
        hf = yv * sz
        gv = g_ref[...]
        dgs, dhfs = [], []
        for grp in range(SSM_GROUPS):
            sl = slice(grp * gw, (grp + 1) * gw)
            n, rstd = _rms(hf[:, sl])
            dgs.append(_colsum(dov[:, sl] * n))
            dhfs.append(_rms_bwd(dov[:, sl] * gv[:, sl], n, rstd))
        dhf = jnp.concatenate(dhfs, axis=1)
        _acc_rows(dg_ref, jnp.concatenate(dgs, axis=1))
        dy_sc[...] = dhf * sz
        dz_ref[...] = dhf * yv * _silu_grad(zv)

        tril, triu, expand = tril_ref[...], triu_ref[...], exp_ref[...]
        ltri = tril > 0
        utri = triu > 0
        lane = _lane_iota((CHUNK, LANE))
        lane1 = _lane_iota((1, LANE))
        lane_lo = lane < SSM_P
        row_last = lax.broadcasted_iota(jnp.int32, (CHUNK, LANE), 0) == CHUNK - 1
        a_neg_v = a_ref[...]

        def chunk(ci, carry):
            dsk_acc, da_acc = carry
            cl = cpb - 1 - ci
            r0 = pl.multiple_of(cl * CHUNK, CHUNK)
            rows = pl.ds(r0, CHUNK)
            dtc = dt_ref[rows, :]
            lam_c, lam_r, lam_last = _chunk_common(dtc, a_neg_v, tril, triu)
            e_c = jnp.exp(lam_c)
            f_c = jnp.exp(lam_last - lam_c)
            cd = jnp.exp(lam_last)
            dt_x, e_x, f_x = _tri_right(dtc, expand), _tri_right(e_c, expand), _tri_right(f_c, expand)
            cd_x = _tri_right(jnp.broadcast_to(cd, (HALO, LANE)), expand)[0:1, :]
            dlam = jnp.zeros((CHUNK, LANE), F32)
            dlast = jnp.zeros((1, LANE), F32)
            ddt_x = jnp.zeros((CHUNK, LANE), F32)
            dsk_parts = []
            for grp in range(SSM_GROUPS):
                bo = SSM_WIDTH + grp * SSM_N
                co = SSM_WIDTH + SSM_GROUPS * SSM_N + grp * SSM_N
                bm = xbc_ref[rows, bo:bo + SSM_N]
                cm = xbc_ref[rows, co:co + SSM_N]
                bm_b, cm_b = bm.astype(BF16), cm.astype(BF16)
                gmat = _nt(cm_b, bm_b)
                gmat_t = _nt(bm_b, cm_b)
                ct_b = cm.T.astype(BF16)
                acc_dg = jnp.zeros((CHUNK, CHUNK), F32)
                acc_dgt = jnp.zeros((CHUNK, CHUNK), F32)
                d_b = jnp.zeros((CHUNK, SSM_N), F32)
                d_c = jnp.zeros((CHUNK, SSM_N), F32)
                for pj in range(hpg // 2):
                    ha = grp * hpg + 2 * pj
                    hb_ = ha + 1
                    lo = ha * SSM_P
                    blk = slice(lo, lo + LANE)
                    xs = xbc_ref[rows, blk]
                    dt2, e2, f2, cd2 = dt_x[:, blk], e_x[:, blk], f_x[:, blk], cd_x[:, blk]
                    x2 = xs * dt2
                    x2b = x2.astype(BF16)
                    dy2 = dy_sc[rows, blk]
                    dy2b = dy2.astype(BF16)
                    hp = hin_ref[cl, :, blk]
                    hp_b = hp.astype(BF16)
                    dhn = dh_sc[:, blk]
                    dhn_b = dhn.astype(BF16)
                    yo = e2 * _nn(cm_b, hp_b)
                    dzz_b = (e2 * dy2).astype(BF16)
                    d_c = d_c + _nt(dzz_b, hp_b)
                    dh_sc[:, blk] = _nn(ct_b, dzz_b) + cd2 * dhn
                    dxs2 = f2 * _nn(bm_b, dhn_b)
                    d_b = d_b + _nt((f2 * x2).astype(BF16), dhn_b)
                    xd = x2 * dxs2
                    t_lam = dy2 * yo - xd
                    t_last = cd2 * (dhn * hp) + xd
                    dxd2 = jnp.zeros((CHUNK, LANE), F32)
                    heads = ((ha, lane_lo), (hb_, jnp.logical_not(lane_lo)))
                    for hh, msk in heads:
                        x2h_b = jnp.where(msk, x2, 0.0).astype(BF16)
                        dy2h_b = jnp.where(msk, dy2, 0.0).astype(BF16)
                        seg = lam_c[:, hh:hh + 1] - lam_r[hh:hh + 1, :]
                        dec = jnp.exp(jnp.where(ltri, seg, -jnp.inf))
                        dect = jnp.exp(jnp.where(utri, -seg, -jnp.inf))
                        dmd = _nt(dy2h_b, x2b) * dec
                        dmtd = _nt(x2h_b, dy2b) * dect
                        acc_dg = acc_dg + dmd
                        acc_dgt = acc_dgt + dmtd
                        dlam_h = jnp.sum(dmd * gmat - dmtd * gmat_t + jnp.where(msk, t_lam, 0.0), axis=1, keepdims=True)
                        last_h = jnp.sum(jnp.sum(jnp.where(msk, t_last, 0.0), axis=0, keepdims=True), axis=1, keepdims=True)
                        dlam = jnp.where(lane == hh, dlam_h, dlam)
                        dlast = jnp.where(lane1 == hh, last_h, dlast)
                        dxd2 = jnp.where(msk, _nn((gmat_t * dect).astype(BF16), dy2b), dxd2)
                    dx2 = dxd2 + dxs2
                    dxbc_ref[rows, blk] = dx2 * dt2 + dy2 * dsk_ref[:, blk]
                    prod = dx2 * xs
                    for hh, msk in heads:
                        col = jnp.sum(jnp.where(msk, prod, 0.0), axis=1, keepdims=True)
                        ddt_x = jnp.where(lane == hh, col, ddt_x)
                    dsk_parts.append(_colsum(dy2 * xs))
                d_c = d_c + _nn(acc_dg.astype(BF16), bm_b)
                d_b = d_b + _nn(acc_dgt.astype(BF16), cm_b)
                dxbc_ref[rows, bo:bo + SSM_N] = d_b
                dxbc_ref[rows, co:co + SSM_N] = d_c
            dlam = dlam + jnp.where(row_last, dlast, 0.0)
            da = _tri_left(triu, dlam)
            ddt_ref[rows, :] = da * a_neg_v + ddt_x
            return dsk_acc + jnp.concatenate(dsk_parts, axis=1), da_acc + _colsum(da * dtc)

        dsk_tot, da_tot = lax.fori_loop(
            0, cpb, chunk, (jnp.zeros((1, SSM_WIDTH), F32), jnp.zeros((1, LANE), F32)))
        _acc_rows(ddsk_ref, dsk_tot)
        _acc_rows(da_ref, da_tot)

    rev = lambda i: (nb - 1 - i, 0)
    rrows = lambda w: pl.BlockSpec((tm, w), rev)
    return pl.pallas_call(
        body, name="ssd_bwd", grid=(nb,),
        in_specs=[rrows(SSM_WIDTH), rrows(SSM_WIDTH), rrows(SSM_WIDTH), rrows(CONV_CH), rrows(LANE),
                  pl.BlockSpec((cpb, SSM_N, SSM_WIDTH), lambda i: (nb - 1 - i, 0, 0)),
                  _whole((1, LANE)), _whole((1, SSM_WIDTH)), _whole((1, SSM_WIDTH)),
                  _whole((CHUNK, CHUNK)), _whole((CHUNK, CHUNK)), _whole((LANE, SSM_WIDTH))],
        out_specs=[rrows(CONV_CH), rrows(LANE), rrows(SSM_WIDTH),
                   _whole((1, SSM_WIDTH)), _whole((1, SSM_WIDTH)), _whole((1, LANE))],
        out_shape=[jax.ShapeDtypeStruct((s, CONV_CH), F32), jax.ShapeDtypeStruct((s, LANE), F32),
                   jax.ShapeDtypeStruct((s, SSM_WIDTH), F32), jax.ShapeDtypeStruct((1, SSM_WIDTH), F32),
                   jax.ShapeDtypeStruct((1, SSM_WIDTH), F32), jax.ShapeDtypeStruct((1, LANE), F32)],
        scratch_shapes=[pltpu.VMEM((SSM_N, SSM_WIDTH), F32), pltpu.VMEM((tm, SSM_WIDTH), F32)],
        compiler_params=_cparams("arbitrary"),
    )(dos, y, z, xbc, dt, hin, a_neg, dskip_x, g_x, tril, triu, expand)


def _ssd_post_bwd(xbc_raw, dxa, ddt, dt_raw, conv_w, conv_b, dt_bias_p):
    s = xbc_raw.shape[0]
    tm = ROW_TILE
    hb = tm // HALO
    nt = s // tm
    ext = tm + HALO

    def body(x_ref, prev_ref, next_ref, d_ref, dnext_ref, ddt_ref, dtr_ref, w_ref, b_ref, db_ref,
             dx_ref, ddtr_ref, dw_ref, dcb_ref, ddb_ref, xe_sc, de_sc):
        i = pl.program_id(0)
        w = w_ref[...]
        xe_sc[pl.ds(0, HALO), :] = jnp.where(i > 0, prev_ref[...], 0.0)
        xe_sc[pl.ds(HALO, tm), :] = x_ref[...]
        xe_sc[pl.ds(HALO + tm, HALO), :] = next_ref[...]
        pre = _conv_taps(xe_sc, w, ext, 0) + b_ref[...]
        sg = _silu_grad(pre)
        de_sc[pl.ds(0, tm), :] = d_ref[...] * sg[:tm]
        de_sc[pl.ds(tm, HALO), :] = jnp.where(i < nt - 1, dnext_ref[...] * sg[tm:], 0.0)
        dconv = de_sc[pl.ds(0, tm), :]
        acc = None
        dws = []
        for k in range(CONV_K):
            term = de_sc[pl.ds(CONV_K - 1 - k, tm), :] * w[k:k + 1, :]
            acc = term if acc is None else acc + term
            dws.append(_colsum(dconv * xe_sc[pl.ds(HALO - (CONV_K - 1) + k, tm), :]))
        dx_ref[...] = acc
        _acc_rows(dw_ref, jnp.concatenate(dws, axis=0))
        _acc_rows(dcb_ref, _colsum(dconv))
        ddtr = ddt_ref[...] * _sigmoid(dtr_ref[...] + db_ref[...])
        ddtr_ref[...] = ddtr
        _acc_rows(ddb_ref, _colsum(ddtr))

    halo_prev = pl.BlockSpec((HALO, CONV_CH), lambda i: (jnp.maximum(i * hb - 1, 0), 0))
    halo_next = pl.BlockSpec((HALO, CONV_CH), lambda i: (jnp.minimum((i + 1) * hb, s // HALO - 1), 0))
    return pl.pallas_call(
        body, name="ssd_post_bwd", grid=(nt,),
        in_specs=[_rows(tm, CONV_CH), halo_prev, halo_next, _rows(tm, CONV_CH), halo_next, _rows(tm, LANE),
                  _rows(tm, LANE), _whole((CONV_K, CONV_CH)), _whole((1, CONV_CH)), _whole((1, LANE))],
        out_specs=[_rows(tm, CONV_CH), _rows(tm, LANE), _whole((CONV_K, CONV_CH)), _whole((1, CONV_CH)),
                   _whole((1, LANE))],
        out_shape=[jax.ShapeDtypeStruct((s, CONV_CH), F32), jax.ShapeDtypeStruct((s, LANE), F32),
                   jax.ShapeDtypeStruct((CONV_K, CONV_CH), F32), jax.ShapeDtypeStruct((1, CONV_CH), F32),
                   jax.ShapeDtypeStruct((1, LANE), F32)],
        scratch_shapes=[pltpu.VMEM((tm + 2 * HALO, CONV_CH), F32), pltpu.VMEM((ext, CONV_CH), F32)],
        compiler_params=_cparams("arbitrary"),
    )(xbc_raw, xbc_raw, xbc_raw, dxa, dxa, ddt, dt_raw, conv_w, conv_b, dt_bias_p)


def _qbwd(dq_att, q_lat, g_q, w_qb_p, cos, sin):
    s = q_lat.shape[0]
    tm = ROW_TILE
    wq = MLA_HEADS * HEAD_PAD

    def body(dq_ref, ql_ref, g_ref, w_ref, cos_ref, sin_ref, dql_ref, draw_ref, dg_ref):
        c, sn = cos_ref[...], sin_ref[...]
        for h in range(MLA_HEADS):
            o = h * HEAD_PAD
            dqh = dq_ref[h].astype(F32) * ATTN_SCALE
            draw_ref[:, o:o + QK_NOPE] = dqh[:, :QK_NOPE].astype(BF16)
            draw_ref[:, o + QK_NOPE:o + HEAD_PAD] = _rope_transposed(dqh[:, QK_NOPE:], c, sn).astype(BF16)
        dn = _nt(draw_ref[...], w_ref[...])
        xhat, rstd = _rms(ql_ref[...])
        _acc_rows(dg_ref, _colsum(dn * xhat))
        dql_ref[...] = _rms_bwd(dn * g_ref[...], xhat, rstd)

    return pl.pallas_call(
        body, name="qbwd", grid=(s // tm,),
        in_specs=[pl.BlockSpec((MLA_HEADS, tm, HEAD_PAD), lambda i: (0, i, 0)), _rows(tm, Q_RANK), _whole((1, Q_RANK)),
                  _whole((Q_RANK, wq)), _rows(tm, LANE), _rows(tm, LANE)],
        out_specs=[_rows(tm, Q_RANK), _rows(tm, wq), _whole((1, Q_RANK))],
        out_shape=[jax.ShapeDtypeStruct((s, Q_RANK), F32), jax.ShapeDtypeStruct((s, wq), BF16),
                   jax.ShapeDtypeStruct((1, Q_RANK), F32)],
        compiler_params=_cparams("arbitrary"),
    )(dq_att, q_lat, g_q, w_qb_p, cos, sin)


def _kvbwd(dk_att, dv, kv_lat, g_kv, w_kvb_p, cos, sin):
    s = kv_lat.shape[0]
    tm = ROW_TILE
    wk = MLA_HEADS * HEAD_PAD
    wr = MLA_HEADS * (QK_NOPE + V_DIM)

    def body(dk_ref, dv_ref, kl_ref, g_ref, w_ref, cos_ref, sin_ref, dkl_ref, draw_ref, dg_ref):
        dkr = None
        for h in range(MLA_HEADS):
            o = h * HEAD_PAD
            draw_ref[:, h * QK_NOPE:(h + 1) * QK_NOPE] = dk_ref[:, o:o + QK_NOPE].astype(BF16)
            part = dk_ref[:, o + QK_NOPE:o + HEAD_PAD].astype(F32)
            dkr = part if dkr is None else dkr + part
        draw_ref[:, MLA_HEADS * QK_NOPE:] = dv_ref[...].astype(BF16)
        dn = _nt(draw_ref[...], w_ref[...])
        xhat, rstd = _rms(kl_ref[:, :KV_RANK])
        _acc_rows(dg_ref, _colsum(dn * xhat))
        dkl_ref[:, :KV_RANK] = _rms_bwd(dn * g_ref[...], xhat, rstd)
        dkl_ref[:, KV_RANK:] = _rope_transposed(dkr, cos_ref[...], sin_ref[...])

    return pl.pallas_call(
        body, name="kvbwd", grid=(s // tm,),
        in_specs=[_rows(tm, wk), _rows(tm, MLA_WIDTH), _rows(tm, KV_LAT_PAD), _whole((1, KV_RANK)),
                  _whole((KV_RANK, wr)), _rows(tm, LANE), _rows(tm, LANE)],
        out_specs=[_rows(tm, KV_LAT_PAD), _rows(tm, wr), _whole((1, KV_RANK))],
        out_shape=[jax.ShapeDtypeStruct((s, KV_LAT_PAD), F32), jax.ShapeDtypeStruct((s, wr), BF16),
                   jax.ShapeDtypeStruct((1, KV_RANK), F32)],
        compiler_params=_cparams("arbitrary"),
    )(dk_att, dv, kv_lat, g_kv, w_kvb_p, cos, sin)


def _inproj_bwd(pieces, w_in_pt, x, scale1p, gx1):
    s = x.shape[0]
    tm = min(ROW_TILE_WIDE, s)

    def body(*refs):
        p_refs = refs[:len(IN_PAD)]
        w_ref, x_ref, sc_ref, gx1_ref, gx_ref, dp_ref, dsc_ref, dsh_ref = refs[len(IN_PAD):]
        off = 0
        for ref, w in zip(p_refs, IN_PAD):
            dp_ref[:, off:off + w] = ref[...].astype(BF16)
            off += w
        du = _nn(dp_ref[...], w_ref[...])
        gx_ref[...] = gx1_ref[...] + du * sc_ref[...]
        _acc_rows(dsc_ref, _colsum(du * x_ref[...]))
        _acc_rows(dsh_ref, _colsum(du))

    vec = _whole((1, D_MODEL))
    return pl.pallas_call(
        body, name="inproj_bwd", grid=(s // tm,),
        in_specs=[_rows(tm, w) for w in IN_PAD] + [_whole((IN_PAD_WIDTH, D_MODEL)), _rows(tm, D_MODEL), vec,
                                                    _rows(tm, D_MODEL)],
        out_specs=[_rows(tm, D_MODEL), _rows(tm, IN_PAD_WIDTH), vec, vec],
        out_shape=[jax.ShapeDtypeStruct((s, D_MODEL), F32), jax.ShapeDtypeStruct((s, IN_PAD_WIDTH), BF16),
                   jax.ShapeDtypeStruct((1, D_MODEL), F32), jax.ShapeDtypeStruct((1, D_MODEL), F32)],
        compiler_params=_cparams("arbitrary"),
    )(*pieces, w_in_pt, x, scale1p, gx1)


def _matmul_tn_rows(name, a, b, tk):
    s, k = a.shape
    n = b.shape[1]
    tm = min(GRAD_ROWS, s)

    def body(a_ref, b_ref, o_ref):
        @pl.when(pl.program_id(1) == 0)
        def _():
            o_ref[...] = jnp.zeros_like(o_ref)
        o_ref[...] += _tn(a_ref[...], b_ref[...])

    return pl.pallas_call(
        body, name=name, grid=(k // tk, s // tm),
        in_specs=[pl.BlockSpec((tm, tk), lambda j, i: (i, j)), pl.BlockSpec((tm, n), lambda j, i: (i, 0))],
        out_specs=pl.BlockSpec((tk, n), lambda j, i: (j, 0)),
        out_shape=jax.ShapeDtypeStruct((k, n), F32),
        compiler_params=_cparams("parallel", "arbitrary"),
    )(a, b)


def _matmul_tn(name, a, b, tn):
    s, k = a.shape
    n = b.shape[1]
    tm = min(GRAD_ROWS, s)

    def body(a_ref, b_ref, o_ref):
        @pl.when(pl.program_id(1) == 0)
        def _():
            o_ref[...] = jnp.zeros_like(o_ref)
        o_ref[...] += _tn(a_ref[...], b_ref[...])

    return pl.pallas_call(
        body, name=name, grid=(n // tn, s // tm),
        in_specs=[pl.BlockSpec((tm, k), lambda j, i: (i, 0)), pl.BlockSpec((tm, tn), lambda j, i: (i, j))],
        out_specs=pl.BlockSpec((k, tn), lambda j, i: (0, j)),
        out_shape=jax.ShapeDtypeStruct((k, n), F32),
        compiler_params=_cparams("parallel", "arbitrary"),
    )(a, b)


def _pack_w_in_t(w_in_t):
    parts, off = [], 0
    for w, wp in zip(IN_SPLITS, IN_PAD):
        parts.append(jnp.pad(w_in_t[off:off + w], ((0, wp - w), (0, 0))))
        off += w
    return jnp.concatenate(parts, axis=0)


def _unpack_w_in_t(g):
    parts, off = [], 0
    for w, wp in zip(IN_SPLITS, IN_PAD):
        parts.append(g[off:off + w])
        off += wp
    return jnp.concatenate(parts, axis=0)


def _pack_w_qb(w_qb):
    w = w_qb.reshape(Q_RANK, MLA_HEADS, QK_HEAD)
    return jnp.pad(w, ((0, 0), (0, 0), (0, HEAD_PAD - QK_HEAD))).reshape(Q_RANK, MLA_HEADS * HEAD_PAD)


def _unpack_w_qb(g):
    return g.reshape(Q_RANK, MLA_HEADS, HEAD_PAD)[:, :, :QK_HEAD].reshape(Q_RANK, MLA_HEADS * QK_HEAD)


def _pack_w_kvb(w_kvb):
    w = w_kvb.reshape(KV_RANK, MLA_HEADS, QK_NOPE + V_DIM)
    return jnp.concatenate([w[:, :, :QK_NOPE].reshape(KV_RANK, -1), w[:, :, QK_NOPE:].reshape(KV_RANK, -1)], axis=1)


def _unpack_w_kvb(g):
    gk = g[:, :MLA_HEADS * QK_NOPE].reshape(KV_RANK, MLA_HEADS, QK_NOPE)
    gv = g[:, MLA_HEADS * QK_NOPE:].reshape(KV_RANK, MLA_HEADS, V_DIM)
    return jnp.concatenate([gk, gv], axis=2).reshape(KV_RANK, -1)


def _rope_tables(positions):
    inv_freq = 1.0 / (ROPE_THETA ** (jnp.arange(ROPE_HALF, dtype=F32) / ROPE_HALF))
    ang = positions.astype(F32)[:, None] * inv_freq
    cos, sin = jnp.cos(ang), jnp.sin(ang)
    zeros = jnp.zeros((positions.shape[0], LANE - QK_ROPE), F32)
    return jnp.concatenate([cos, cos, zeros], axis=1), jnp.concatenate([-sin, sin, zeros], axis=1)


def _local_step(x, tgt, positions, mod, w_in_t, q_norm_g, w_qb_p, kv_norm_g, w_kvb_p, conv_w, conv_b, dt_bias,
                a_log, d_skip, ssm_norm_g, w_out_b, ln_g, ln_b):
    row = lambda v: v.reshape(1, -1)
    shift, scale, gate = mod[:D_MODEL], mod[D_MODEL:2 * D_MODEL], mod[2 * D_MODEL:]
    scale1p = row(1.0 + scale)
    w_in_p = _pack_w_in_t(w_in_t)
    cos, sin = _rope_tables(positions)
    a_neg = row(jnp.pad(-jnp.exp(a_log), (0, LANE - SSM_HEADS)))
    dskip_x = row(jnp.repeat(d_skip, SSM_P))
    dt_bias_p = row(jnp.pad(dt_bias, (0, LANE - SSM_HEADS)))
    tri = jnp.tril(jnp.ones((CHUNK, CHUNK), F32))
    tril, triu = tri.astype(BF16), tri.T.astype(BF16)

    u_bf, q_lat, kv_lat, z_attn, xbc_raw, dt_raw, z_ssm = _inproj(x, scale1p, row(shift), w_in_p)
    nq_bf, q_att = _qpath(q_lat, row(q_norm_g), w_qb_p, cos, sin)
    nkv_bf, k_att, v_att = _kvpath(kv_lat, row(kv_norm_g), w_kvb_p, cos, sin)
    o, lse_rows = _attn_fwd(q_att, k_att, v_att)
    xbc, dt = _ssd_pre(xbc_raw, dt_raw, conv_w, row(conv_b), dt_bias_p)
    expand = jnp.repeat(jnp.eye(LANE, SSM_HEADS, dtype=BF16), SSM_P, axis=1)
    y, o_ssm, hin = _ssd_fwd(xbc, dt, z_ssm, a_neg, dskip_x, row(ssm_norm_g), tril, triu)
    (cat_bf, dmix_bf, gx1, do_bf, dz_attn, delta_rows, dos, loss, d_ln_g, d_ln_b, d_gate) = _outln(
        o, z_attn, o_ssm, w_out_b, x, row(gate), row(ln_g), row(ln_b), tgt)

    g_w_out = _matmul_tn("gw_out", cat_bf, dmix_bf, 512)
    dk_att, dv, dq_att = _attn_bwd(q_att, k_att, v_att, do_bf, lse_rows, delta_rows)
    dq_lat, dqraw_bf, d_q_norm_g = _qbwd(dq_att, q_lat, row(q_norm_g), w_qb_p, cos, sin)
    dkv_lat, dkvraw_bf, d_kv_norm_g = _kvbwd(dk_att, dv, kv_lat, row(kv_norm_g), w_kvb_p, cos, sin)
    g_w_qb = _matmul_tn("gw_qb", nq_bf, dqraw_bf, MLA_HEADS * HEAD_PAD)
    g_w_kvb = _matmul_tn("gw_kvb", nkv_bf, dkvraw_bf, MLA_HEADS * (QK_NOPE + V_DIM))
    dxa, ddt, dz_ssm, d_ssm_g, ddsk_x, d_a = _ssd_bwd(dos, y, z_ssm, xbc, dt, hin, a_neg, dskip_x, row(ssm_norm_g),
                                                       tril, triu, expand)
    dxbc_raw, ddt_raw, d_conv_w, d_conv_b, d_dt_bias = _ssd_post_bwd(xbc_raw, dxa, ddt, dt_raw, conv_w, row(conv_b),
                                                                     dt_bias_p)
    grad_x, dproj_bf, d_scale, d_shift = _inproj_bwd((dq_lat, dkv_lat, dz_attn, dxbc_raw, ddt_raw, dz_ssm),
                                                     w_in_p, x, scale1p, gx1)
    g_w_in_t = _unpack_w_in_t(_matmul_tn_rows("gw_in", dproj_bf, u_bf, 896))
    return dict(
        loss=loss[0, 0], grad_x=grad_x,
        dmod=jnp.concatenate([d_shift[0], d_scale[0], d_gate[0]]),
        w_in_t=g_w_in_t, q_norm_g=d_q_norm_g[0], w_qb=g_w_qb, kv_norm_g=d_kv_norm_g[0], w_kvb=g_w_kvb,
        conv_w=d_conv_w, conv_b=d_conv_b[0], dt_bias=d_dt_bias[0, :SSM_HEADS],
        a_log=d_a[0, :SSM_HEADS] * a_neg[0, :SSM_HEADS],
        d_skip=ddsk_x.reshape(SSM_HEADS, SSM_P).sum(axis=1), ssm_norm_g=d_ssm_g[0], w_out=g_w_out,
        ln_g=d_ln_g[0], ln_b=d_ln_b[0])


ADAM_ROWS = 512


def _my_index():
    return 4 * lax.axis_index("x") + 2 * lax.axis_index("y") + lax.axis_index("c")


def _exchange(name, sends, gather):
    n = len(sends)
    peers = N_DEV - 1

    def body(*refs):
        send_refs, recv_refs = refs[:n], refs[n:2 * n]
        send_sems, recv_sems, local_sems = refs[2 * n:]
        x, y, c = lax.axis_index("x"), lax.axis_index("y"), lax.axis_index("c")
        me = 4 * x + 2 * y + c

        def src(a, idx):
            return send_refs[a] if gather else send_refs[a].at[idx]

        owns = [pltpu.make_async_copy(src(a, me), recv_refs[a].at[me], local_sems.at[a]) for a in range(n)]
        for cp in owns:
            cp.start()
        copies = []
        for k in range(1, N_DEV):
            px, py, pc = x ^ ((k >> 2) & 1), y ^ ((k >> 1) & 1), c ^ (k & 1)
            peer = 4 * px + 2 * py + pc
            for a in range(n):
                copies.append(pltpu.make_async_remote_copy(
                    src_ref=src(a, peer), dst_ref=recv_refs[a].at[me],
                    send_sem=send_sems.at[a * peers + k - 1], recv_sem=recv_sems.at[a * peers + k - 1],
                    device_id=(px, py, pc), device_id_type=pl.DeviceIdType.MESH))
        for cp in copies:
            cp.start()
        for cp in copies:
            cp.wait()
        for cp in owns:
            cp.wait()

    block_shape = lambda a: a.shape if gather else a.shape[1:]
    return pl.pallas_call(
        body, name=name,
        in_specs=[pl.BlockSpec(memory_space=pl.ANY)] * n, out_specs=[pl.BlockSpec(memory_space=pl.ANY)] * n,
        out_shape=[jax.ShapeDtypeStruct((N_DEV, *block_shape(a)), a.dtype) for a in sends],
        scratch_shapes=[pltpu.SemaphoreType.DMA((n * peers,)), pltpu.SemaphoreType.DMA((n * peers,)),
                        pltpu.SemaphoreType.DMA((n,))],
    )(*sends)


def _gather_two_level(name, sends):
    n = len(sends)
    per = N_DEV - 1

    def body(*refs):
        send_refs, recv_refs = refs[:n], refs[n:2 * n]
        send_sems, recv_sems, local_sems = refs[2 * n:]
        x, y, c = lax.axis_index("x"), lax.axis_index("y"), lax.axis_index("c")
        sibling = (x, y, 1 - c)
        chips = [(1 - x, y), (x, 1 - y), (1 - x, 1 - y)]

        def idx(px, py, pc):
            return 4 * px + 2 * py + pc

        def copy(a, k, block, to, src=None):
            slot = recv_refs[a].at[idx(*block)]
            return pltpu.make_async_remote_copy(
                src_ref=slot if src is None else src, dst_ref=slot,
                send_sem=send_sems.at[a * per + k], recv_sem=recv_sems.at[a * per + k],
                device_id=to, device_id_type=pl.DeviceIdType.MESH)

        me = (x, y, c)
        owns = [pltpu.make_async_copy(send_refs[a], recv_refs[a].at[idx(*me)], local_sems.at[a]) for a in range(n)]
        for cp in owns:
            cp.start()
        first = [copy(a, 0, me, sibling, src=send_refs[a]) for a in range(n)]
        first += [copy(a, 1 + j, me, (*chip, c), src=send_refs[a]) for j, chip in enumerate(chips) for a in range(n)]
        for cp in first:
            cp.start()
        passed = []
        for j, chip in enumerate(chips):
            for a in range(n):
                copy(a, 1 + j, (*chip, c), me).wait_recv()
                fwd = copy(a, 4 + j, (*chip, c), sibling)
                fwd.start()
                passed.append(fwd)
        for a in range(n):
            copy(a, 0, sibling, me).wait_recv()
            for j, chip in enumerate(chips):
                copy(a, 4 + j, (*chip, 1 - c), me).wait_recv()
        for cp in first + passed:
            cp.wait_send()
        for cp in owns:
            cp.wait()

    return pl.pallas_call(
        body, name=name,
        in_specs=[pl.BlockSpec(memory_space=pl.ANY)] * n, out_specs=[pl.BlockSpec(memory_space=pl.ANY)] * n,
        out_shape=[jax.ShapeDtypeStruct((N_DEV, *a.shape), a.dtype) for a in sends],
        scratch_shapes=[pltpu.SemaphoreType.DMA((n * per,)), pltpu.SemaphoreType.DMA((n * per,)),
                        pltpu.SemaphoreType.DMA((n,))],
    )(*sends)


def _flat_rows(parts, row_multiple):
    flat = jnp.concatenate([p.reshape(-1) for p in parts])
    chunk = row_multiple * LANE
    total = -(-flat.shape[0] // chunk) * chunk
    return jnp.pad(flat, (0, total - flat.shape[0])).reshape(-1, LANE)


def _unflat(flat, shapes):
    flat = flat.reshape(-1)
    out, off = [], 0
    for shp in shapes:
        n = math.prod(shp)
        out.append(flat[off:off + n].reshape(shp))
        off += n
    return out


def _adam_update(g, w, m, v):
    m2 = ADAM_B1 * m + (1.0 - ADAM_B1) * g
    v2 = ADAM_B2 * v + (1.0 - ADAM_B2) * (g * g)
    m_hat = m2 / (1.0 - ADAM_B1 ** ADAM_STEP)
    v_hat = v2 / (1.0 - ADAM_B2 ** ADAM_STEP)
    delta = -ADAM_LR * (m_hat / (jnp.sqrt(v_hat) + ADAM_EPS) + ADAM_WD * w)
    return delta, m2, v2


def _adamw_summed(name, parts, w, m, v):
    r = w.shape[0]
    tr = min(ADAM_ROWS, r)

    def body(p_ref, w_ref, m_ref, v_ref, g_ref, d_ref, m2_ref, v2_ref):
        g = p_ref[0]
        for j in range(1, N_DEV):
            g = g + p_ref[j]
        g_ref[...] = g
        d_ref[...], m2_ref[...], v2_ref[...] = _adam_update(g, w_ref[...], m_ref[...], v_ref[...])

    rows = _rows(tr, LANE)
    return pl.pallas_call(
        body, name=name, grid=(r // tr,),
        in_specs=[pl.BlockSpec((N_DEV, tr, LANE), lambda i: (0, i, 0)), rows, rows, rows],
        out_specs=[rows] * 4, out_shape=[jax.ShapeDtypeStruct((r, LANE), F32)] * 4,
        compiler_params=_cparams("parallel"),
    )(parts, w, m, v)


def _modpart(c_all, w_ada, b_cols):
    def body(c_ref, w_ref, b_ref, o_ref):
        o_ref[...] = _nn(c_ref[...].astype(BF16), w_ref[...].astype(BF16)) + b_ref[...]

    return pl.pallas_call(
        body, name="modpart", out_shape=jax.ShapeDtypeStruct((N_DEV, w_ada.shape[1]), F32),
    )(c_all, w_ada, b_cols)


def _adamw_w_ada(c_all_t, dmod_cols, w, m, v):
    def body(c_ref, d_ref, w_ref, m_ref, v_ref, g_ref, dl_ref, m2_ref, v2_ref):
        g = c_ref[:, 0:1] * d_ref[0:1, :]
        for b in range(1, N_DEV):
            g = g + c_ref[:, b:b + 1] * d_ref[b:b + 1, :]
        g_ref[...] = g
        dl_ref[...], m2_ref[...], v2_ref[...] = _adam_update(g, w_ref[...], m_ref[...], v_ref[...])

    return pl.pallas_call(
        body, name="adamw_w_ada", out_shape=[jax.ShapeDtypeStruct(w.shape, F32)] * 4,
        compiler_params=pltpu.CompilerParams(vmem_limit_bytes=VMEM_LIMIT),
    )(c_all_t, dmod_cols, w, m, v)


W_IN_SHARD = IN_WIDTH // N_DEV
W_IN_SHARD_LANES = -(-W_IN_SHARD // LANE) * LANE
BF16_ROWS = 16
W_IN_SEND_ROWS = -(-W_IN_SHARD // BF16_ROWS) * BF16_ROWS


def _transpose_cast(w_pad):
    def body(w_ref, o_ref):
        o_ref[...] = w_ref[...].T.astype(BF16)

    return pl.pallas_call(
        body, name="w_in_transpose", out_shape=jax.ShapeDtypeStruct(w_pad.shape[::-1], BF16),
        compiler_params=pltpu.CompilerParams(vmem_limit_bytes=VMEM_LIMIT),
    )(w_pad)


def _adamw_w_in(parts, w, m, v):
    rows_t = parts.shape[1]
    d, cols = w.shape
    tb = ROW_TILE

    def body(p_ref, w_ref, m_ref, v_ref, g_ref, d_ref, m2_ref, v2_ref):
        gt = p_ref[0].astype(F32)
        for j in range(1, N_DEV):
            gt = gt + p_ref[j].astype(F32)
        gt = jnp.concatenate([gt, jnp.zeros((W_IN_SHARD_LANES - rows_t, tb), F32)], axis=0)
        g = gt.T[:, :cols]
        g_ref[...] = g
        d_ref[...], m2_ref[...], v2_ref[...] = _adam_update(g, w_ref[...], m_ref[...], v_ref[...])

    blk = _rows(tb, cols)
    return pl.pallas_call(
        body, name="adamw_w_in", grid=(d // tb,),
        in_specs=[pl.BlockSpec((N_DEV, rows_t, tb), lambda i: (0, 0, i)), blk, blk, blk],
        out_specs=[blk] * 4, out_shape=[jax.ShapeDtypeStruct(w.shape, F32)] * 4,
        compiler_params=_cparams("parallel"),
    )(parts, w, m, v)


SHARDED = ("w_qb", "w_kvb", "w_out")
REPLICATED = ("b_ada", "q_norm_g", "kv_norm_g", "conv_b", "dt_bias", "a_log", "d_skip", "ssm_norm_g", "ln_g", "ln_b")
WEIGHTS = ("w_ada", "b_ada", "w_in", "q_norm_g", "w_qb", "kv_norm_g", "w_kvb", "conv_w", "conv_b", "dt_bias",
           "a_log", "d_skip", "ssm_norm_g", "w_out", "ln_g", "ln_b")
HEAD_COLS = QK_NOPE + V_DIM


def _adamw_blocks(name, parts, w, m, v):
    r, c = w.shape
    tr = ROW_TILE if r % ROW_TILE == 0 else r

    def body(p_ref, w_ref, m_ref, v_ref, g_ref, d_ref, m2_ref, v2_ref):
        g = p_ref[0].astype(F32)
        for j in range(1, N_DEV):
            g = g + p_ref[j].astype(F32)
        g_ref[...] = g
        d_ref[...], m2_ref[...], v2_ref[...] = _adam_update(g, w_ref[...], m_ref[...], v_ref[...])

    blk = _rows(tr, c)
    return pl.pallas_call(
        body, name=name, grid=(r // tr,),
        in_specs=[pl.BlockSpec((N_DEV, tr, c), lambda i: (0, i, 0)), blk, blk, blk],
        out_specs=[blk] * 4, out_shape=[jax.ShapeDtypeStruct(w.shape, F32)] * 4,
        compiler_params=_cparams("parallel"),
    )(parts, w, m, v)


def kernel(x, c, positions, w_ada, b_ada, w_in, q_norm_g, w_qb, kv_norm_g, w_kvb, conv_w, conv_b, dt_bias, a_log, d_skip, ssm_norm_g, w_out, ln_g, ln_b, loss_target, m_w_ada, m_b_ada, m_w_in, m_q_norm_g, m_w_qb, m_kv_norm_g, m_w_kvb, m_conv_w, m_conv_b, m_dt_bias, m_a_log, m_d_skip, m_ssm_norm_g, m_w_out, m_ln_g, m_ln_b, v_w_ada, v_b_ada, v_w_in, v_q_norm_g, v_w_qb, v_kv_norm_g, v_w_kvb, v_conv_w, v_conv_b, v_dt_bias, v_a_log, v_d_skip, v_ssm_norm_g, v_w_out, v_ln_g, v_ln_b):
    given = dict(w_ada=w_ada, b_ada=b_ada, w_in=w_in, q_norm_g=q_norm_g, w_qb=w_qb, kv_norm_g=kv_norm_g, w_kvb=w_kvb,
                 conv_w=conv_w, conv_b=conv_b, dt_bias=dt_bias, a_log=a_log, d_skip=d_skip, ssm_norm_g=ssm_norm_g,
                 w_out=w_out, ln_g=ln_g, ln_b=ln_b)
    mom = dict(w_ada=m_w_ada, b_ada=m_b_ada, w_in=m_w_in, q_norm_g=m_q_norm_g, w_qb=m_w_qb, kv_norm_g=m_kv_norm_g,
               w_kvb=m_w_kvb, conv_w=m_conv_w, conv_b=m_conv_b, dt_bias=m_dt_bias, a_log=m_a_log, d_skip=m_d_skip,
               ssm_norm_g=m_ssm_norm_g, w_out=m_w_out, ln_g=m_ln_g, ln_b=m_ln_b)
    var = dict(w_ada=v_w_ada, b_ada=v_b_ada, w_in=v_w_in, q_norm_g=v_q_norm_g, w_qb=v_w_qb, kv_norm_g=v_kv_norm_g,
               w_kvb=v_w_kvb, conv_w=v_conv_w, conv_b=v_conv_b, dt_bias=v_dt_bias, a_log=v_a_log, d_skip=v_d_skip,
               ssm_norm_g=v_ssm_norm_g, w_out=v_w_out, ln_g=v_ln_g, ln_b=v_ln_b)
    w0 = {k: a[0] for k, a in given.items()}
    m0 = {k: a[0] for k, a in mom.items()}
    v0 = {k: a[0] for k, a in var.items()}
    me = _my_index()

    w_in_rows = _transpose_cast(jnp.pad(w0["w_in"], ((0, 0), (0, W_IN_SHARD_LANES - W_IN_SHARD))))
    g_w_in, g_w_qb, g_w_kvb, g_w_out, g_conv_w, c_all = _gather_two_level(
        "gather_weights", [w_in_rows] + [w0[k].astype(BF16) for k in SHARDED] + [w0["conv_w"], c])
    c_all = c_all.reshape(N_DEV, D_MODEL)
    w_in_t = g_w_in[:, :W_IN_SHARD, :].reshape(IN_WIDTH, D_MODEL)
    w_qb_p = jnp.pad(g_w_qb, ((0, 0), (0, 0), (0, HEAD_PAD - QK_HEAD))).transpose(1, 0, 2).reshape(Q_RANK, -1)
    w_kvb_p = g_w_kvb.reshape(N_DEV, KV_RANK, 2, QK_NOPE).transpose(1, 2, 0, 3).reshape(KV_RANK, -1)
    w_out_b = g_w_out.reshape(MIX_WIDTH, D_MODEL)
    conv_w_full = g_conv_w.transpose(1, 0, 2).reshape(CONV_K, CONV_CH)

    ada_cols = w0["w_ada"].shape[1]
    b_cols = lax.dynamic_slice(w0["b_ada"], (me * ada_cols,), (ada_cols,)).reshape(1, ada_cols)
    mod_all, = _exchange("gather_mod", [_modpart(c_all, w0["w_ada"], b_cols)], gather=True)
    mod = lax.dynamic_index_in_dim(mod_all, me, axis=1, keepdims=False).reshape(-1)

    loc = _local_step(x[0], loss_target[0], positions[0], mod, w_in_t, w0["q_norm_g"], w_qb_p,
                      w0["kv_norm_g"], w_kvb_p, conv_w_full, w0["conv_b"], w0["dt_bias"], w0["a_log"],
                      w0["d_skip"], w0["ssm_norm_g"], w_out_b, w0["ln_g"], w0["ln_b"])

    rep_shapes = [w0[k].shape for k in REPLICATED] + [(1,)]
    rep_local = [loc["dmod"]] + [loc[k] for k in REPLICATED[1:]] + [loc["loss"].reshape(1)]
    rep_parts, conv_parts = _exchange("gather_small", [_flat_rows(rep_local, HALO), loc["conv_w"]], gather=True)
    conv_cols = w0["conv_w"].shape[1]
    conv_mine = lax.dynamic_slice(conv_parts, (0, 0, me * conv_cols), (N_DEV, CONV_K, conv_cols))
    outs = {"conv_w": _adamw_blocks("adamw_conv_w", conv_mine, w0["conv_w"], m0["conv_w"], v0["conv_w"])}
    zero1 = jnp.zeros((1,), F32)
    rep = _adamw_summed("adamw_replicated", rep_parts,
                        _flat_rows([w0[k] for k in REPLICATED] + [zero1], HALO),
                        _flat_rows([m0[k] for k in REPLICATED] + [zero1], HALO),
                        _flat_rows([v0[k] for k in REPLICATED] + [zero1], HALO))
    rep_g, rep_d, rep_m, rep_v = [_unflat(a, rep_shapes) for a in rep]
    loss = rep_g[-1][0]

    dmod_all = rep_parts.reshape(N_DEV, -1)[:, :3 * D_MODEL]
    dmod_cols = lax.dynamic_slice(dmod_all, (0, me * ada_cols), (N_DEV, ada_cols))
    outs["w_ada"] = _adamw_w_ada(c_all.T, dmod_cols, w0["w_ada"], m0["w_ada"], v0["w_ada"])

    send_w_in = loc["w_in_t"].astype(BF16).reshape(N_DEV, W_IN_SHARD, D_MODEL)
    send_w_in = jnp.pad(send_w_in, ((0, 0), (0, W_IN_SEND_ROWS - W_IN_SHARD), (0, 0)))
    send_w_qb = loc["w_qb"].astype(BF16).reshape(Q_RANK, N_DEV, HEAD_PAD)[:, :, :QK_HEAD].transpose(1, 0, 2)
    send_w_kvb = loc["w_kvb"].astype(BF16).reshape(KV_RANK, 2, N_DEV, QK_NOPE).transpose(2, 0, 1, 3)
    send_w_kvb = send_w_kvb.reshape(N_DEV, KV_RANK, HEAD_COLS)
    send_w_out = loc["w_out"].astype(BF16).reshape(N_DEV, MIX_WIDTH // N_DEV, D_MODEL)
    r_w_in, r_w_qb, r_w_kvb, r_w_out = _exchange(
        "scatter_grads", [send_w_in, send_w_qb, send_w_kvb, send_w_out], gather=False)
    outs["w_in"] = _adamw_w_in(r_w_in, w0["w_in"], m0["w_in"], v0["w_in"])
    for k, parts in zip(SHARDED, (r_w_qb, r_w_kvb, r_w_out)):
        outs[k] = _adamw_blocks("adamw_" + k, parts, w0[k], m0[k], v0[k])

    def collect(idx):
        out = {k: o[idx] for k, o in outs.items()}
        out.update({k: (rep_g, rep_d, rep_m, rep_v)[idx][i] for i, k in enumerate(REPLICATED)})
        return [out[k][None] for k in WEIGHTS]

    return (loss, loc["grad_x"][None], *collect(0), *collect(1), *collect(2), *collect(3))
```

```python
import math

import jax
import jax.numpy as jnp
from jax import lax
from jax.experimental import pallas as pl
from jax.experimental.pallas import tpu as pltpu

F32 = jnp.float32
BF16 = jnp.bfloat16

N_DEV = 8
D_MODEL = 1024
MLA_HEADS = 8
QK_NOPE = 128
QK_ROPE = 64
V_DIM = 128
Q_RANK = 384
KV_RANK = 256
QK_HEAD = QK_NOPE + QK_ROPE
HEAD_PAD = 256
ROPE_HALF = QK_ROPE // 2
ROPE_THETA = 10000.0
MLA_WIDTH = MLA_HEADS * V_DIM
SSM_HEADS = 16
SSM_P = 64
SSM_WIDTH = SSM_HEADS * SSM_P
SSM_GROUPS = 2
SSM_N = 128
CONV_K = 4
CHUNK = 128
CONV_CH = SSM_WIDTH + 2 * SSM_GROUPS * SSM_N
MIX_WIDTH = MLA_WIDTH + SSM_WIDTH
IN_SPLITS = (Q_RANK, KV_RANK + QK_ROPE, MLA_WIDTH, CONV_CH, SSM_HEADS, SSM_WIDTH)
IN_WIDTH = sum(IN_SPLITS)
LANE = 128
KV_LAT_PAD = KV_RANK + LANE
IN_PAD = (Q_RANK, KV_LAT_PAD, MLA_WIDTH, CONV_CH, LANE, SSM_WIDTH)
IN_PAD_WIDTH = sum(IN_PAD)
DEEPNORM_ALPHA = 2.0 ** 0.25
RMS_EPS = 1e-6
LN_EPS = 1e-5
ATTN_SCALE = QK_HEAD ** -0.5
LOG2E = math.log2(math.e)
LN2 = math.log(2.0)
Q_PRESCALE = ATTN_SCALE * LOG2E
ADAM_LR, ADAM_B1, ADAM_B2, ADAM_EPS, ADAM_WD, ADAM_STEP = 0.001, 0.9, 0.999, 1e-08, 0.01, 10

ROW_TILE = 512
ROW_TILE_WIDE = 256
ATTN_TILE = 512
ATTN_UNROLLS = (8, 4, 2)
SSD_ROWS = 512
GRAD_ROWS = 2048
VMEM_LIMIT = 56 * 1024 * 1024


def _nn(a, b):
    return jnp.dot(a, b, preferred_element_type=F32)


def _nt(a, b):
    return lax.dot_general(a, b, (((1,), (1,)), ((), ())), preferred_element_type=F32)


def _tn(a, b):
    return lax.dot_general(a, b, (((0,), (0,)), ((), ())), preferred_element_type=F32)


def _cparams(*sem):
    return pltpu.CompilerParams(dimension_semantics=sem, vmem_limit_bytes=VMEM_LIMIT)


def _rows(tm, w):
    return pl.BlockSpec((tm, w), lambda i: (i, 0))


def _whole(shape):
    return pl.BlockSpec(shape, lambda i: (0,) * len(shape))


def _whole_once(shape):
    return pl.BlockSpec(shape, lambda i: (0,) * len(shape), pipeline_mode=pl.Buffered(1))


def _sigmoid(z):
    return 1.0 / (1.0 + jnp.exp(-z))


def _lane_iota(shape):
    return lax.broadcasted_iota(jnp.int32, shape, len(shape) - 1)


def _swap_halves(r):
    lane = _lane_iota(r.shape)
    return jnp.where(lane < ROPE_HALF, pltpu.roll(r, LANE - ROPE_HALF, 1),
                     jnp.where(lane < QK_ROPE, pltpu.roll(r, ROPE_HALF, 1), 0.0))


def _rope(r, cos, sin):
    return r * cos + _swap_halves(r) * sin


def _rope_transposed(d, cos, sin):
    return d * cos + _swap_halves(d * sin)


def _rms(x):
    rstd = lax.rsqrt(jnp.mean(x * x, axis=-1, keepdims=True) + RMS_EPS)
    return x * rstd, rstd


def _rms_bwd(dxhat, xhat, rstd):
    return rstd * (dxhat - xhat * jnp.mean(dxhat * xhat, axis=-1, keepdims=True))


def _acc_rows(ref, val):
    @pl.when(pl.program_id(0) == 0)
    def _():
        ref[...] = jnp.zeros_like(ref)
    ref[...] += val


def _colsum(v):
    return jnp.sum(v, axis=0, keepdims=True)


def _inproj(x, scale1p, shift, w_in_pt):
    s = x.shape[0]
    tm = ROW_TILE

    def body(x_ref, sc_ref, sh_ref, w_ref, u_ref, *outs):
        u = (x_ref[...] * sc_ref[...] + sh_ref[...]).astype(BF16)
        u_ref[...] = u
        proj = _nt(u, w_ref[...])
        off = 0
        for ref, w in zip(outs, IN_PAD):
            ref[...] = proj[:, off:off + w]
            off += w

    return pl.pallas_call(
        body, name="inproj", grid=(s // tm,),
        in_specs=[_rows(tm, D_MODEL), _whole((1, D_MODEL)), _whole((1, D_MODEL)), _whole((IN_PAD_WIDTH, D_MODEL))],
        out_specs=[_rows(tm, D_MODEL)] + [_rows(tm, w) for w in IN_PAD],
        out_shape=[jax.ShapeDtypeStruct((s, D_MODEL), BF16)] + [jax.ShapeDtypeStruct((s, w), F32) for w in IN_PAD],
        compiler_params=_cparams("parallel"),
    )(x, scale1p, shift, w_in_pt)


def _qpath(q_lat, g_q, w_qb_p, cos, sin):
    s = q_lat.shape[0]
    tm = ROW_TILE

    def body(ql_ref, g_ref, w_ref, cos_ref, sin_ref, nq_ref, q_ref):
        xhat, _ = _rms(ql_ref[...])
        nq = (xhat * g_ref[...]).astype(BF16)
        nq_ref[...] = nq
        raw = _nn(nq, w_ref[...]) * Q_PRESCALE
        c, sn = cos_ref[...], sin_ref[...]
        for h in range(MLA_HEADS):
            o = h * HEAD_PAD
            q_ref[:, o:o + QK_NOPE] = raw[:, o:o + QK_NOPE].astype(BF16)
            q_ref[:, o + QK_NOPE:o + HEAD_PAD] = _rope(raw[:, o + QK_NOPE:o + HEAD_PAD], c, sn).astype(BF16)

    return pl.pallas_call(
        body, name="qpath", grid=(s // tm,),
        in_specs=[_rows(tm, Q_RANK), _whole((1, Q_RANK)), _whole((Q_RANK, MLA_HEADS * HEAD_PAD)),
                  _rows(tm, LANE), _rows(tm, LANE)],
        out_specs=[_rows(tm, Q_RANK), _rows(tm, MLA_HEADS * HEAD_PAD)],
        out_shape=[jax.ShapeDtypeStruct((s, Q_RANK), BF16), jax.ShapeDtypeStruct((s, MLA_HEADS * HEAD_PAD), BF16)],
        compiler_params=_cparams("parallel"),
    )(q_lat, g_q, w_qb_p, cos, sin)


def _kvpath(kv_lat, g_kv, w_kvb_p, cos, sin):
    s = kv_lat.shape[0]
    tm = ROW_TILE

    def body(kl_ref, g_ref, w_ref, cos_ref, sin_ref, nkv_ref, k_ref, v_ref):
        kl = kl_ref[...]
        xhat, _ = _rms(kl[:, :KV_RANK])
        nkv = (xhat * g_ref[...]).astype(BF16)
        nkv_ref[...] = nkv
        raw = _nn(nkv, w_ref[...])
        kr = _rope(kl[:, KV_RANK:], cos_ref[...], sin_ref[...]).astype(BF16)
        for h in range(MLA_HEADS):
            o = h * HEAD_PAD
            k_ref[:, o:o + QK_NOPE] = raw[:, h * QK_NOPE:(h + 1) * QK_NOPE].astype(BF16)
            k_ref[:, o + QK_NOPE:o + HEAD_PAD] = kr
        v_ref[...] = raw[:, MLA_HEADS * QK_NOPE:].astype(BF16)

    return pl.pallas_call(
        body, name="kvpath", grid=(s // tm,),
        in_specs=[_rows(tm, KV_LAT_PAD), _whole((1, KV_RANK)), _whole((KV_RANK, MLA_HEADS * (QK_NOPE + V_DIM))),
                  _rows(tm, LANE), _rows(tm, LANE)],
        out_specs=[_rows(tm, KV_RANK), _rows(tm, MLA_HEADS * HEAD_PAD), _rows(tm, MLA_WIDTH)],
        out_shape=[jax.ShapeDtypeStruct((s, KV_RANK), BF16), jax.ShapeDtypeStruct((s, MLA_HEADS * HEAD_PAD), BF16),
                   jax.ShapeDtypeStruct((s, MLA_WIDTH), BF16)],
        compiler_params=_cparams("parallel"),
    )(kv_lat, g_kv, w_kvb_p, cos, sin)


def _causal_mask(t):
    row = lax.broadcasted_iota(jnp.int32, (t, t), 0)
    col = lax.broadcasted_iota(jnp.int32, (t, t), 1)
    return row, col


def _attn_fwd(q, k, v):
    s = q.shape[0]
    t = min(ATTN_TILE, s)
    nq = s // t

    def body(q_ref, k_ref, v_ref, o_ref, lse_ref, m_sc, l_sc, acc_sc, sa_sc, sb_sc):
        i = pl.program_id(1)
        qv = q_ref[...]
        m_sc[...] = jnp.full(m_sc.shape, -jnp.inf, F32)
        l_sc[...] = jnp.zeros(l_sc.shape, F32)
        acc_sc[...] = jnp.zeros(acc_sc.shape, F32)

        def scores(j, s_ref):
            s_ref[...] = _nt(qv, k_ref[pl.ds(pl.multiple_of(j * t, t), t), :])

        def update(s_ref, j, masked):
            vv = v_ref[pl.ds(pl.multiple_of(j * t, t), t), :]
            sc = s_ref[...]
            if masked:
                row, col = _causal_mask(t)
                sc = jnp.where(col <= row, sc, -jnp.inf)
            m_prev = m_sc[...]
            m_new = jnp.maximum(m_prev, jnp.max(sc, axis=1, keepdims=True))
            alpha = jnp.exp2(m_prev - m_new)
            p = jnp.exp2(sc - jnp.tile(m_new, (1, t // LANE)))
            l_sc[...] = alpha * l_sc[...] + jnp.sum(p, axis=1, keepdims=True)
            acc_sc[...] = alpha * acc_sc[...] + _nn(p.astype(BF16), vv)
            m_sc[...] = m_new

        def run(j0, count):
            bufs = (sa_sc, sb_sc)
            for u in range(count):
                scores(j0 + u + 1, bufs[(u + 1) % 2])
                update(bufs[u % 2], j0 + u, False)

        scores(0, sa_sc)
        done = 0
        for group in ATTN_UNROLLS:
            def body_(g, carry, base=done, group=group):
                run(base + group * g, group)
                return carry

            n_groups = lax.div(i - done, group)
            lax.fori_loop(0, n_groups, body_, 0)
            done = done + group * n_groups
        odd = lax.rem(i, 2)

        @pl.when(odd == 1)
        def _():
            scores(i, sb_sc)
            update(sa_sc, i - 1, False)
            update(sb_sc, i, True)

        @pl.when(odd == 0)
        def _():
            update(sa_sc, i, True)

        l = l_sc[...]
        o_ref[...] = acc_sc[...] / l
        lse_ref[0] = (m_sc[...] + jnp.log2(l)).T[0:1, :]

    return pl.pallas_call(
        body, name="attn_fwd", grid=(MLA_HEADS, nq),
        in_specs=[pl.BlockSpec((t, HEAD_PAD), lambda h, i: (i, h)),
                  pl.BlockSpec((s, HEAD_PAD), lambda h, i: (0, h)),
                  pl.BlockSpec((s, V_DIM), lambda h, i: (0, h))],
        out_specs=[pl.BlockSpec((t, V_DIM), lambda h, i: (i, h)), pl.BlockSpec((1, 1, t), lambda h, i: (h, 0, i))],
        out_shape=[jax.ShapeDtypeStruct((s, MLA_WIDTH), F32), jax.ShapeDtypeStruct((MLA_HEADS, 1, s), F32)],
        scratch_shapes=[pltpu.VMEM((t, LANE), F32), pltpu.VMEM((t, LANE), F32), pltpu.VMEM((t, V_DIM), F32),
                        pltpu.VMEM((t, t), F32), pltpu.VMEM((t, t), F32)],
        compiler_params=_cparams("parallel", "arbitrary"),
    )(q, k, v)


def _attn_bwd(q, k, v, do, lse_row, delta_row):
    s = q.shape[0]
    t = min(ATTN_TILE, s)
    nq = s // t

    def body(q_ref, k_ref, v_ref, do_ref, lse_ref, dl_ref, dk_ref, dv_ref, dq_hbm,
             dq_sc, dk_sc, dv_sc, sa_sc, sb_sc, pa_sc, pb_sc, sem, stage_sc):
        h = pl.program_id(0)
        j = pl.program_id(1)
        kv_ = k_ref[...]
        vv = v_ref[...]

        @pl.when(j == 0)
        def _():
            dq_sc[...] = jnp.zeros(dq_sc.shape, F32)

        dk_sc[...] = jnp.zeros(dk_sc.shape, F32)
        dv_sc[...] = jnp.zeros(dv_sc.shape, F32)

        def scores(i, s_ref, p_ref):
            off = pl.multiple_of(i * t, t)
            s_ref[...] = _nt(kv_, q_ref[pl.ds(off, t), :])
            p_ref[...] = _nt(vv, do_ref[pl.ds(off, t), :])

        def update(i, s_ref, p_ref, masked):
            off = pl.multiple_of(i * t, t)
            qv = q_ref[pl.ds(off, t), :]
            dov = do_ref[pl.ds(off, t), :]
            sct = s_ref[...]
            if masked:
                row, col = _causal_mask(t)
                sct = jnp.where(row <= col, sct, -jnp.inf)
            pt = jnp.exp2(sct - lse_ref[0, :, pl.ds(off, t)])
            gt = (pt * (p_ref[...] - dl_ref[0, :, pl.ds(off, t)])).astype(BF16)
            dv_sc[...] += _nn(pt.astype(BF16), dov)
            dk_sc[...] += _nn(gt, qv)
            dq_sc[pl.ds(off, t), :] += _tn(gt, kv_)

        rest = nq - 1 - j
        scores(j, sa_sc, pa_sc)

        @pl.when(rest >= 1)
        def _():
            scores(j + 1, sb_sc, pb_sc)

        update(j, sa_sc, pa_sc, True)

        def run(i0, count):
            bufs = ((sb_sc, pb_sc), (sa_sc, pa_sc))
            for u in range(count):
                scores(i0 + u + 1, *bufs[(u + 1) % 2])
                update(i0 + u, *bufs[u % 2], False)

        i1, left = j + 1, rest
        for group in ATTN_UNROLLS:
            def body_(g, carry, base=i1, group=group):
                run(base + group * g, group)
                return carry

            n_groups = jnp.where(left >= 1, lax.div(left - 1, group), 0)
            lax.fori_loop(0, n_groups, body_, 0)
            i1 = i1 + group * n_groups
            left = left - group * n_groups

        @pl.when(left == 1)
        def _():
            update(i1, sb_sc, pb_sc, False)

        @pl.when(left == 2)
        def _():
            scores(i1 + 1, sa_sc, pa_sc)
            update(i1, sb_sc, pb_sc, False)
            update(i1 + 1, sa_sc, pa_sc, False)

        dk_ref[...] = (dk_sc[...] * LN2).astype(BF16)
        dv_ref[...] = dv_sc[...].astype(BF16)

        def out_copy(jj):
            rows = pl.ds(pl.multiple_of(jj * t, t), t)
            return pltpu.make_async_copy(stage_sc, dq_hbm.at[h, rows, :], sem)

        @pl.when(j > 0)
        def _():
            out_copy(j - 1).wait()

        stage_sc[...] = dq_sc[pl.ds(pl.multiple_of(j * t, t), t), :].astype(BF16)
        out_copy(j).start()

        @pl.when(j == nq - 1)
        def _():
            out_copy(j).wait()

    return pl.pallas_call(
        body, name="attn_bwd", grid=(MLA_HEADS, nq),
        in_specs=[pl.BlockSpec((s, HEAD_PAD), lambda h, j: (0, h)),
                  pl.BlockSpec((t, HEAD_PAD), lambda h, j: (j, h)),
                  pl.BlockSpec((t, V_DIM), lambda h, j: (j, h)),
                  pl.BlockSpec((s, V_DIM), lambda h, j: (0, h)),
                  pl.BlockSpec((1, 1, s), lambda h, j: (h, 0, 0)),
                  pl.BlockSpec((1, 1, s), lambda h, j: (h, 0, 0))],
        out_specs=[pl.BlockSpec((t, HEAD_PAD), lambda h, j: (j, h)), pl.BlockSpec((t, V_DIM), lambda h, j: (j, h)),
                   pl.BlockSpec(memory_space=pl.ANY)],
        out_shape=[jax.ShapeDtypeStruct((s, MLA_HEADS * HEAD_PAD), BF16), jax.ShapeDtypeStruct((s, MLA_WIDTH), BF16),
                   jax.ShapeDtypeStruct((MLA_HEADS, s, HEAD_PAD), BF16)],
        scratch_shapes=[pltpu.VMEM((s, HEAD_PAD), F32), pltpu.VMEM((t, HEAD_PAD), F32), pltpu.VMEM((t, V_DIM), F32),
                        pltpu.VMEM((t, t), F32), pltpu.VMEM((t, t), F32), pltpu.VMEM((t, t), F32),
                        pltpu.VMEM((t, t), F32), pltpu.SemaphoreType.DMA, pltpu.VMEM((t, HEAD_PAD), BF16)],
        compiler_params=_cparams("arbitrary", "arbitrary"),
    )(q, k, v, do, lse_row, delta_row)


HALO = 8


def _silu(z):
    return z * _sigmoid(z)


def _silu_grad(z):
    sg = _sigmoid(z)
    return sg * (1.0 + z * (1.0 - sg))


def _softplus(x):
    e = jnp.exp(-jnp.abs(x))
    small = e * (1.0 - e * (0.5 - e * (1.0 / 3.0)))
    return jnp.maximum(x, 0.0) + jnp.where(e < 1e-3, small, jnp.log(1.0 + e))


def _conv_taps(xe_ref, w, tm, first):
    acc = None
    for k in range(CONV_K):
        term = xe_ref[pl.ds(HALO + first - (CONV_K - 1) + k, tm), :] * w[k:k + 1, :]
        acc = term if acc is None else acc + term
    return acc


def _ssd_pre(xbc_raw, dt_raw, conv_w, conv_b, dt_bias_p):
    s = xbc_raw.shape[0]
    tm = ROW_TILE
    hb = tm // HALO

    def body(x_ref, prev_ref, dtr_ref, w_ref, b_ref, db_ref, act_ref, dt_ref, xe_sc):
        i = pl.program_id(0)
        xe_sc[pl.ds(0, HALO), :] = jnp.where(i > 0, prev_ref[...], 0.0)
        xe_sc[pl.ds(HALO, tm), :] = x_ref[...]
        pre = _conv_taps(xe_sc, w_ref[...], tm, 0) + b_ref[...]
        act_ref[...] = _silu(pre)
        dt_ref[...] = _softplus(dtr_ref[...] + db_ref[...])

    return pl.pallas_call(
        body, name="ssd_pre", grid=(s // tm,),
        in_specs=[_rows(tm, CONV_CH), pl.BlockSpec((HALO, CONV_CH), lambda i: (jnp.maximum(i * hb - 1, 0), 0)),
                  _rows(tm, LANE), _whole((CONV_K, CONV_CH)), _whole((1, CONV_CH)), _whole((1, LANE))],
        out_specs=[_rows(tm, CONV_CH), _rows(tm, LANE)],
        out_shape=[jax.ShapeDtypeStruct((s, CONV_CH), F32), jax.ShapeDtypeStruct((s, LANE), F32)],
        scratch_shapes=[pltpu.VMEM((tm + HALO, CONV_CH), F32)],
        compiler_params=_cparams("parallel"),
    )(xbc_raw, xbc_raw, dt_raw, conv_w, conv_b, dt_bias_p)


def _split3(a):
    a1 = a.astype(BF16)
    r1 = a - a1.astype(F32)
    a2 = r1.astype(BF16)
    a3 = (r1 - a2.astype(F32)).astype(BF16)
    return a1, a2, a3


def _tri_left(tri, a):
    a1, a2, a3 = _split3(a)
    return _nn(tri, a1) + _nn(tri, a2) + _nn(tri, a3)


def _tri_right(a, tri):
    a1, a2, a3 = _split3(a)
    return _nn(a1, tri) + _nn(a2, tri) + _nn(a3, tri)


def _pair_sel(lane_lo, col_a, col_b):
    return jnp.where(lane_lo, col_a, col_b)


def _chunk_common(dt, a_neg, tril, triu):
    a = dt * a_neg
    lam_c = _tri_left(tril, a)
    lam_r = _tri_right(a.T, triu)
    lam_last = lam_c[CHUNK - 1:CHUNK, :]
    return lam_c, lam_r, lam_last


def _gated_norm_fwd(y, z, g):
    hf = y * _silu(z)
    outs = []
    for grp in range(SSM_GROUPS):
        w = SSM_WIDTH // SSM_GROUPS
        n, _ = _rms(hf[:, grp * w:(grp + 1) * w])
        outs.append(n)
    return jnp.concatenate(outs, axis=1) * g


def _ssd_fwd(xbc, dt, z, a_neg, dskip_x, g_x, tril, triu):
    s = xbc.shape[0]
    tm = min(SSD_ROWS, s)
    cpb = tm // CHUNK
    nc = s // CHUNK

    def body(xbc_ref, dt_ref, z_ref, a_ref, dsk_ref, g_ref, tril_ref, triu_ref, y_ref, o_ref, hin_ref, h_sc):
        @pl.when(pl.program_id(0) == 0)
        def _():
            h_sc[...] = jnp.zeros(h_sc.shape, F32)

        tril, triu = tril_ref[...], triu_ref[...]
        ltri = tril > 0
        lane_lo = _lane_iota((CHUNK, LANE)) < SSM_P

        def chunk(c, carry):
            r0 = pl.multiple_of(c * CHUNK, CHUNK)
            dtc = dt_ref[pl.ds(r0, CHUNK), :]
            lam_c, lam_r, lam_last = _chunk_common(dtc, a_ref[...], tril, triu)
            e_c = jnp.exp(lam_c)
            f_r = jnp.exp(lam_r[:, CHUNK - 1:CHUNK] - lam_r)
            cd = jnp.exp(lam_last)
            for grp in range(SSM_GROUPS):
                bo = SSM_WIDTH + grp * SSM_N
                co = SSM_WIDTH + SSM_GROUPS * SSM_N + grp * SSM_N
                bm = xbc_ref[pl.ds(r0, CHUNK), bo:bo + SSM_N]
                cm = xbc_ref[pl.ds(r0, CHUNK), co:co + SSM_N]
                cm_b = cm.astype(BF16)
                gmat = _nt(cm_b, bm.astype(BF16))
                bt = bm.T
                for pj in range(SSM_HEADS // SSM_GROUPS // 2):
                    ha = grp * (SSM_HEADS // SSM_GROUPS) + 2 * pj
                    hb_ = ha + 1
                    lo = ha * SSM_P
                    xs = xbc_ref[pl.ds(r0, CHUNK), lo:lo + LANE]
                    x2 = xs * _pair_sel(lane_lo, dtc[:, ha:ha + 1], dtc[:, hb_:hb_ + 1])
                    x2b = x2.astype(BF16)
                    ys, sts = [], []
                    for hh in (ha, hb_):
                        seg = lam_c[:, hh:hh + 1] - lam_r[hh:hh + 1, :]
                        dec = jnp.exp(jnp.where(ltri, seg, -jnp.inf))
                        ys.append(_nn((gmat * dec).astype(BF16), x2b))
                        sts.append(_nn((bt * f_r[hh:hh + 1, :]).astype(BF16), x2b))
                    hp = h_sc[:, lo:lo + LANE]
                    hin_ref[c, :, lo:lo + LANE] = hp
                    zz = _nn(cm_b, hp.astype(BF16))
                    e2 = _pair_sel(lane_lo, e_c[:, ha:ha + 1], e_c[:, hb_:hb_ + 1])
                    yv = jnp.where(lane_lo, ys[0], ys[1]) + e2 * zz
                    y_ref[pl.ds(r0, CHUNK), lo:lo + LANE] = yv + xs * dsk_ref[:, lo:lo + LANE]
                    cd2 = _pair_sel(lane_lo, cd[:, ha:ha + 1], cd[:, hb_:hb_ + 1])
                    h_sc[:, lo:lo + LANE] = hp * cd2 + jnp.where(lane_lo, sts[0], sts[1])
            return carry

        lax.fori_loop(0, cpb, chunk, 0)
        o_ref[...] = _gated_norm_fwd(y_ref[...], z_ref[...], g_ref[...])

    return pl.pallas_call(
        body, name="ssd_fwd", grid=(s // tm,),
        in_specs=[_rows(tm, CONV_CH), _rows(tm, LANE), _rows(tm, SSM_WIDTH), _whole((1, LANE)),
                  _whole((1, SSM_WIDTH)), _whole((1, SSM_WIDTH)), _whole((CHUNK, CHUNK)), _whole((CHUNK, CHUNK))],
        out_specs=[_rows(tm, SSM_WIDTH), _rows(tm, SSM_WIDTH),
                   pl.BlockSpec((cpb, SSM_N, SSM_WIDTH), lambda i: (i, 0, 0))],
        out_shape=[jax.ShapeDtypeStruct((s, SSM_WIDTH), F32), jax.ShapeDtypeStruct((s, SSM_WIDTH), F32),
                   jax.ShapeDtypeStruct((nc, SSM_N, SSM_WIDTH), F32)],
        scratch_shapes=[pltpu.VMEM((SSM_N, SSM_WIDTH), F32)],
        compiler_params=_cparams("arbitrary"),
    )(xbc, dt, z, a_neg, dskip_x, g_x, tril, triu)


def _outln(o, z_attn, o_ssm, w_out, x, gate, ln_g, ln_b, tgt):
    s = x.shape[0]
    tm = min(ROW_TILE_WIDE, s)

    def body(o_ref, z_ref, os_ref, w_ref, x_ref, gate_ref, g_ref, b_ref, t_ref,
             cat_ref, dmix_ref, gx_ref, do_ref, dz_ref, dl_ref, dos_ref, loss_ref, dg_ref, db_ref, dgate_ref):
        ov, zv = o_ref[...], z_ref[...]
        sz = _silu(zv)
        cat_ref[:, :MLA_WIDTH] = (ov * sz).astype(BF16)
        cat_ref[:, MLA_WIDTH:] = os_ref[...].astype(BF16)
        w = w_ref[...]
        mixed = _nn(cat_ref[...], w)
        gate_v = gate_ref[...]
        hv = DEEPNORM_ALPHA * x_ref[...] + gate_v * mixed
        mu = jnp.mean(hv, axis=-1, keepdims=True)
        hc = hv - mu
        rstd = lax.rsqrt(jnp.mean(hc * hc, axis=-1, keepdims=True) + LN_EPS)
        xhat = hc * rstd
        g = g_ref[...]
        err = xhat * g + b_ref[...] - t_ref[...]
        _acc_rows(loss_ref, jnp.full((1, LANE), (0.5 / D_MODEL) * jnp.sum(err * err), F32))
        dy = err * (1.0 / D_MODEL)
        _acc_rows(dg_ref, _colsum(dy * xhat))
        _acc_rows(db_ref, _colsum(dy))
        dxhat = dy * g
        dh = rstd * (dxhat - jnp.mean(dxhat, axis=-1, keepdims=True)
                     - xhat * jnp.mean(dxhat * xhat, axis=-1, keepdims=True))
        gx_ref[...] = DEEPNORM_ALPHA * dh
        _acc_rows(dgate_ref, _colsum(dh * mixed))
        dmix = (gate_v * dh).astype(BF16)
        dmix_ref[...] = dmix
        dcat = _nt(dmix, w)
        da = dcat[:, :MLA_WIDTH]
        dos_ref[...] = dcat[:, MLA_WIDTH:]
        dov = da * sz
        do_ref[...] = dov.astype(BF16)
        dz_ref[...] = da * ov * _silu_grad(zv)
        prod = dov * ov
        for h in range(MLA_HEADS):
            dsum = jnp.sum(prod[:, h * V_DIM:(h + 1) * V_DIM], axis=1, keepdims=True)
            dl_ref[h] = jnp.broadcast_to(dsum, (tm, LANE)).T[0:1, :]

    vec = _whole((1, D_MODEL))
    return pl.pallas_call(
        body, name="outln", grid=(s // tm,),
        in_specs=[_rows(tm, MLA_WIDTH), _rows(tm, MLA_WIDTH), _rows(tm, SSM_WIDTH), _whole((MIX_WIDTH, D_MODEL)),
                  _rows(tm, D_MODEL), vec, vec, vec, _rows(tm, D_MODEL)],
        out_specs=[_rows(tm, MIX_WIDTH), _rows(tm, D_MODEL), _rows(tm, D_MODEL), _rows(tm, MLA_WIDTH),
                   _rows(tm, MLA_WIDTH), pl.BlockSpec((MLA_HEADS, 1, tm), lambda i: (0, 0, i)), _rows(tm, SSM_WIDTH),
                   _whole((1, LANE)), vec, vec, vec],
        out_shape=[jax.ShapeDtypeStruct((s, MIX_WIDTH), BF16), jax.ShapeDtypeStruct((s, D_MODEL), BF16),
                   jax.ShapeDtypeStruct((s, D_MODEL), F32), jax.ShapeDtypeStruct((s, MLA_WIDTH), BF16),
                   jax.ShapeDtypeStruct((s, MLA_WIDTH), F32), jax.ShapeDtypeStruct((MLA_HEADS, 1, s), F32),
                   jax.ShapeDtypeStruct((s, SSM_WIDTH), F32), jax.ShapeDtypeStruct((1, LANE), F32),
                   jax.ShapeDtypeStruct((1, D_MODEL), F32), jax.ShapeDtypeStruct((1, D_MODEL), F32),
                   jax.ShapeDtypeStruct((1, D_MODEL), F32)],
        compiler_params=_cparams("arbitrary"),
    )(o, z_attn, o_ssm, w_out, x, gate, ln_g, ln_b, tgt)


def _ssd_bwd(dos, y, z, xbc, dt, hin, a_neg, dskip_x, g_x, tril, triu, expand):
    s = xbc.shape[0]
    tm = min(SSD_ROWS, s)
    cpb = tm // CHUNK
    nb = s // tm
    gw = SSM_WIDTH // SSM_GROUPS
    hpg = SSM_HEADS // SSM_GROUPS

    def body(dos_ref, y_ref, z_ref, xbc_ref, dt_ref, hin_ref, a_ref, dsk_ref, g_ref, tril_ref, triu_ref, exp_ref,
             dxbc_ref, ddt_ref, dz_ref, dg_ref, ddsk_ref, da_ref, dh_sc, dy_sc):
        @pl.when(pl.program_id(0) == 0)
        def _():
            dh_sc[...] = jnp.zeros(dh_sc.shape, F32)

        yv, zv, dov = y_ref[...], z_ref[...], dos_ref[...]
        sz = _silu(zv)
        hf = yv * sz
        gv = g_ref[...]
        dgs, dhfs = [], []
        for grp in range(SSM_GROUPS):
            sl = slice(grp * gw, (grp + 1) * gw)
            n, rstd = _rms(hf[:, sl])
            dgs.append(_colsum(dov[:, sl] * n))
            dhfs.append(_rms_bwd(dov[:, sl] * gv[:, sl], n, rstd))
        dhf = jnp.concatenate(dhfs, axis=1)
        _acc_rows(dg_ref, jnp.concatenate(dgs, axis=1))
        dy_sc[...] = dhf * sz
        dz_ref[...] = dhf * yv * _silu_grad(zv)

        tril, triu, expand = tril_ref[...], triu_ref[...], exp_ref[...]
        ltri = tril > 0
        utri = triu > 0
        lane = _lane_iota((CHUNK, LANE))
        lane1 = _lane_iota((1, LANE))
        lane_lo = lane < SSM_P
        row_last = lax.broadcasted_iota(jnp.int32, (CHUNK, LANE), 0) == CHUNK - 1
        a_neg_v = a_ref[...]

        def chunk(ci, carry):
            dsk_acc, da_acc = carry
            cl = cpb - 1 - ci
            r0 = pl.multiple_of(cl * CHUNK, CHUNK)
            rows = pl.ds(r0, CHUNK)
            dtc = dt_ref[rows, :]
            lam_c, lam_r, lam_last = _chunk_common(dtc, a_neg_v, tril, triu)
            e_c = jnp.exp(lam_c)
            f_c = jnp.exp(lam_last - lam_c)
            cd = jnp.exp(lam_last)
            dt_x, e_x, f_x = _tri_right(dtc, expand), _tri_right(e_c, expand), _tri_right(f_c, expand)
            cd_x = _tri_right(jnp.broadcast_to(cd, (HALO, LANE)), expand)[0:1, :]
            dlam = jnp.zeros((CHUNK, LANE), F32)
            dlast = jnp.zeros((1, LANE), F32)
            ddt_x = jnp.zeros((CHUNK, LANE), F32)
            dsk_parts = []
            for grp in range(SSM_GROUPS):
                bo = SSM_WIDTH + grp * SSM_N
                co = SSM_WIDTH + SSM_GROUPS * SSM_N + grp * SSM_N
                bm = xbc_ref[rows, bo:bo + SSM_N]
                cm = xbc_ref[rows, co:co + SSM_N]
                bm_b, cm_b = bm.astype(BF16), cm.astype(BF16)
                gmat = _nt(cm_b, bm_b)
                gmat_t = _nt(bm_b, cm_b)
                ct_b = cm.T.astype(BF16)
                acc_dg = jnp.zeros((CHUNK, CHUNK), F32)
                acc_dgt = jnp.zeros((CHUNK, CHUNK), F32)
                d_b = jnp.zeros((CHUNK, SSM_N), F32)
                d_c = jnp.zeros((CHUNK, SSM_N), F32)
                for pj in range(hpg // 2):
                    ha = grp * hpg + 2 * pj
                    hb_ = ha + 1
                    lo = ha * SSM_P
                    blk = slice(lo, lo + LANE)
                    xs = xbc_ref[rows, blk]
                    dt2, e2, f2, cd2 = dt_x[:, blk], e_x[:, blk], f_x[:, blk], cd_x[:, blk]
                    x2 = xs * dt2
                    x2b = x2.astype(BF16)
                    dy2 = dy_sc[rows, blk]
                    dy2b = dy2.astype(BF16)
                    hp = hin_ref[cl, :, blk]
                    hp_b = hp.astype(BF16)
                    dhn = dh_sc[:, blk]
                    dhn_b = dhn.astype(BF16)
                    yo = e2 * _nn(cm_b, hp_b)
                    dzz_b = (e2 * dy2).astype(BF16)
                    d_c = d_c + _nt(dzz_b, hp_b)
                    dh_sc[:, blk] = _nn(ct_b, dzz_b) + cd2 * dhn
                    dxs2 = f2 * _nn(bm_b, dhn_b)
                    d_b = d_b + _nt((f2 * x2).astype(BF16), dhn_b)
                    xd = x2 * dxs2
                    t_lam = dy2 * yo - xd
                    t_last = cd2 * (dhn * hp) + xd
                    dxd2 = jnp.zeros((CHUNK, LANE), F32)
                    heads = ((ha, lane_lo), (hb_, jnp.logical_not(lane_lo)))
                    for hh, msk in heads:
                        x2h_b = jnp.where(msk, x2, 0.0).astype(BF16)
                        dy2h_b = jnp.where(msk, dy2, 0.0).astype(BF16)
                        seg = lam_c[:, hh:hh + 1] - lam_r[hh:hh + 1, :]
                        dec = jnp.exp(jnp.where(ltri, seg, -jnp.inf))
                        dect = jnp.exp(jnp.where(utri, -seg, -jnp.inf))
                        dmd = _nt(dy2h_b, x2b) * dec
                        dmtd = _nt(x2h_b, dy2b) * dect
                        acc_dg = acc_dg + dmd
                        acc_dgt = acc_dgt + dmtd
                        dlam_h = jnp.sum(dmd * gmat - dmtd * gmat_t + jnp.where(msk, t_lam, 0.0), axis=1, keepdims=True)
                        last_h = jnp.sum(jnp.sum(jnp.where(msk, t_last, 0.0), axis=0, keepdims=True), axis=1, keepdims=True)
                        dlam = jnp.where(lane == hh, dlam_h, dlam)
                        dlast = jnp.where(lane1 == hh, last_h, dlast)
                        dxd2 = jnp.where(msk, _nn((gmat_t * dect).astype(BF16), dy2b), dxd2)
                    dx2 = dxd2 + dxs2
                    dxbc_ref[rows, blk] = dx2 * dt2 + dy2 * dsk_ref[:, blk]
                    prod = dx2 * xs
                    for hh, msk in heads:
                        col = jnp.sum(jnp.where(msk, prod, 0.0), axis=1, keepdims=True)
                        ddt_x = jnp.where(lane == hh, col, ddt_x)
                    dsk_parts.append(_colsum(dy2 * xs))
                d_c = d_c + _nn(acc_dg.astype(BF16), bm_b)
                d_b = d_b + _nn(acc_dgt.astype(BF16), cm_b)
                dxbc_ref[rows, bo:bo + SSM_N] = d_b
                dxbc_ref[rows, co:co + SSM_N] = d_c
            dlam = dlam + jnp.where(row_last, dlast, 0.0)
            da = _tri_left(triu, dlam)
            ddt_ref[rows, :] = da * a_neg_v + ddt_x
            return dsk_acc + jnp.concatenate(dsk_parts, axis=1), da_acc + _colsum(da * dtc)

        dsk_tot, da_tot = lax.fori_loop(
            0, cpb, chunk, (jnp.zeros((1, SSM_WIDTH), F32), jnp.zeros((1, LANE), F32)))
        _acc_rows(ddsk_ref, dsk_tot)
        _acc_rows(da_ref, da_tot)

    rev = lambda i: (nb - 1 - i, 0)
    rrows = lambda w: pl.BlockSpec((tm, w), rev)
    return pl.pallas_call(
        body, name="ssd_bwd", grid=(nb,),
        in_specs=[rrows(SSM_WIDTH), rrows(SSM_WIDTH), rrows(SSM_WIDTH), rrows(CONV_CH), rrows(LANE),
                  pl.BlockSpec((cpb, SSM_N, SSM_WIDTH), lambda i: (nb - 1 - i, 0, 0)),
                  _whole((1, LANE)), _whole((1, SSM_WIDTH)), _whole((1, SSM_WIDTH)),
                  _whole((CHUNK, CHUNK)), _whole((CHUNK, CHUNK)), _whole((LANE, SSM_WIDTH))],
        out_specs=[rrows(CONV_CH), rrows(LANE), rrows(SSM_WIDTH),
                   _whole((1, SSM_WIDTH)), _whole((1, SSM_WIDTH)), _whole((1, LANE))],
        out_shape=[jax.ShapeDtypeStruct((s, CONV_CH), F32), jax.ShapeDtypeStruct((s, LANE), F32),
                   jax.ShapeDtypeStruct((s, SSM_WIDTH), F32), jax.ShapeDtypeStruct((1, SSM_WIDTH), F32),
                   jax.ShapeDtypeStruct((1, SSM_WIDTH), F32), jax.ShapeDtypeStruct((1, LANE), F32)],
        scratch_shapes=[pltpu.VMEM((SSM_N, SSM_WIDTH), F32), pltpu.VMEM((tm, SSM_WIDTH), F32)],
        compiler_params=_cparams("arbitrary"),
    )(dos, y, z, xbc, dt, hin, a_neg, dskip_x, g_x, tril, triu, expand)


def _ssd_post_bwd(xbc_raw, dxa, ddt, dt_raw, conv_w, conv_b, dt_bias_p):
    s = xbc_raw.shape[0]
    tm = ROW_TILE
    hb = tm // HALO
    nt = s // tm
    ext = tm + HALO

    def body(x_ref, prev_ref, next_ref, d_ref, dnext_ref, ddt_ref, dtr_ref, w_ref, b_ref, db_ref,
             dx_ref, ddtr_ref, dw_ref, dcb_ref, ddb_ref, xe_sc, de_sc):
        i = pl.program_id(0)
        w = w_ref[...]
        xe_sc[pl.ds(0, HALO), :] = jnp.where(i > 0, prev_ref[...], 0.0)
        xe_sc[pl.ds(HALO, tm), :] = x_ref[...]
        xe_sc[pl.ds(HALO + tm, HALO), :] = next_ref[...]
        pre = _conv_taps(xe_sc, w, ext, 0) + b_ref[...]
        sg = _silu_grad(pre)
        de_sc[pl.ds(0, tm), :] = d_ref[...] * sg[:tm]
        de_sc[pl.ds(tm, HALO), :] = jnp.where(i < nt - 1, dnext_ref[...] * sg[tm:], 0.0)
        dconv = de_sc[pl.ds(0, tm), :]
        acc = None
        dws = []
        for k in range(CONV_K):
            term = de_sc[pl.ds(CONV_K - 1 - k, tm), :] * w[k:k + 1, :]
            acc = term if acc is None else acc + term
            dws.append(_colsum(dconv * xe_sc[pl.ds(HALO - (CONV_K - 1) + k, tm), :]))
        dx_ref[...] = acc
        _acc_rows(dw_ref, jnp.concatenate(dws, axis=0))
        _acc_rows(dcb_ref, _colsum(dconv))
        ddtr = ddt_ref[...] * _sigmoid(dtr_ref[...] + db_ref[...])
        ddtr_ref[...] = ddtr
        _acc_rows(ddb_ref, _colsum(ddtr))

    halo_prev = pl.BlockSpec((HALO, CONV_CH), lambda i: (jnp.maximum(i * hb - 1, 0), 0))
    halo_next = pl.BlockSpec((HALO, CONV_CH), lambda i: (jnp.minimum((i + 1) * hb, s // HALO - 1), 0))
    return pl.pallas_call(
        body, name="ssd_post_bwd", grid=(nt,),
        in_specs=[_rows(tm, CONV_CH), halo_prev, halo_next, _rows(tm, CONV_CH), halo_next, _rows(tm, LANE),
                  _rows(tm, LANE), _whole((CONV_K, CONV_CH)), _whole((1, CONV_CH)), _whole((1, LANE))],
        out_specs=[_rows(tm, CONV_CH), _rows(tm, LANE), _whole((CONV_K, CONV_CH)), _whole((1, CONV_CH)),
                   _whole((1, LANE))],
        out_shape=[jax.ShapeDtypeStruct((s, CONV_CH), F32), jax.ShapeDtypeStruct((s, LANE), F32),
                   jax.ShapeDtypeStruct((CONV_K, CONV_CH), F32), jax.ShapeDtypeStruct((1, CONV_CH), F32),
                   jax.ShapeDtypeStruct((1, LANE), F32)],
        scratch_shapes=[pltpu.VMEM((tm + 2 * HALO, CONV_CH), F32), pltpu.VMEM((ext, CONV_CH), F32)],
        compiler_params=_cparams("arbitrary"),
    )(xbc_raw, xbc_raw, xbc_raw, dxa, dxa, ddt, dt_raw, conv_w, conv_b, dt_bias_p)


def _qbwd(dq_att, q_lat, g_q, w_qb_p, cos, sin):
    s = q_lat.shape[0]
    tm = ROW_TILE
    wq = MLA_HEADS * HEAD_PAD

    def body(dq_ref, ql_ref, g_ref, w_ref, cos_ref, sin_ref, dql_ref, draw_ref, dg_ref):
        c, sn = cos_ref[...], sin_ref[...]
        for h in range(MLA_HEADS):
            o = h * HEAD_PAD
            dqh = dq_ref[h].astype(F32) * ATTN_SCALE
            draw_ref[:, o:o + QK_NOPE] = dqh[:, :QK_NOPE].astype(BF16)
            draw_ref[:, o + QK_NOPE:o + HEAD_PAD] = _rope_transposed(dqh[:, QK_NOPE:], c, sn).astype(BF16)
        dn = _nt(draw_ref[...], w_ref[...])
        xhat, rstd = _rms(ql_ref[...])
        _acc_rows(dg_ref, _colsum(dn * xhat))
        dql_ref[...] = _rms_bwd(dn * g_ref[...], xhat, rstd)

    return pl.pallas_call(
        body, name="qbwd", grid=(s // tm,),
        in_specs=[pl.BlockSpec((MLA_HEADS, tm, HEAD_PAD), lambda i: (0, i, 0)), _rows(tm, Q_RANK), _whole((1, Q_RANK)),
                  _whole((Q_RANK, wq)), _rows(tm, LANE), _rows(tm, LANE)],
        out_specs=[_rows(tm, Q_RANK), _rows(tm, wq), _whole((1, Q_RANK))],
        out_shape=[jax.ShapeDtypeStruct((s, Q_RANK), F32), jax.ShapeDtypeStruct((s, wq), BF16),
                   jax.ShapeDtypeStruct((1, Q_RANK), F32)],
        compiler_params=_cparams("arbitrary"),
    )(dq_att, q_lat, g_q, w_qb_p, cos, sin)


def _kvbwd(dk_att, dv, kv_lat, g_kv, w_kvb_p, cos, sin):
    s = kv_lat.shape[0]
    tm = ROW_TILE
    wk = MLA_HEADS * HEAD_PAD
    wr = MLA_HEADS * (QK_NOPE + V_DIM)

    def body(dk_ref, dv_ref, kl_ref, g_ref, w_ref, cos_ref, sin_ref, dkl_ref, draw_ref, dg_ref):
        dkr = None
        for h in range(MLA_HEADS):
            o = h * HEAD_PAD
            draw_ref[:, h * QK_NOPE:(h + 1) * QK_NOPE] = dk_ref[:, o:o + QK_NOPE].astype(BF16)
            part = dk_ref[:, o + QK_NOPE:o + HEAD_PAD].astype(F32)
            dkr = part if dkr is None else dkr + part
        draw_ref[:, MLA_HEADS * QK_NOPE:] = dv_ref[...].astype(BF16)
        dn = _nt(draw_ref[...], w_ref[...])
        xhat, rstd = _rms(kl_ref[:, :KV_RANK])
        _acc_rows(dg_ref, _colsum(dn * xhat))
        dkl_ref[:, :KV_RANK] = _rms_bwd(dn * g_ref[...], xhat, rstd)
        dkl_ref[:, KV_RANK:] = _rope_transposed(dkr, cos_ref[...], sin_ref[...])

    return pl.pallas_call(
        body, name="kvbwd", grid=(s // tm,),
        in_specs=[_rows(tm, wk), _rows(tm, MLA_WIDTH), _rows(tm, KV_LAT_PAD), _whole((1, KV_RANK)),
                  _whole((KV_RANK, wr)), _rows(tm, LANE), _rows(tm, LANE)],
        out_specs=[_rows(tm, KV_LAT_PAD), _rows(tm, wr), _whole((1, KV_RANK))],
        out_shape=[jax.ShapeDtypeStruct((s, KV_LAT_PAD), F32), jax.ShapeDtypeStruct((s, wr), BF16),
                   jax.ShapeDtypeStruct((1, KV_RANK), F32)],
        compiler_params=_cparams("arbitrary"),
    )(dk_att, dv, kv_lat, g_kv, w_kvb_p, cos, sin)


def _inproj_bwd(pieces, w_in_pt, x, scale1p, gx1):
    s = x.shape[0]
    tm = min(ROW_TILE, s)

    def body(*refs):
        p_refs = refs[:len(IN_PAD)]
        w_ref, x_ref, sc_ref, gx1_ref, gx_ref, dp_ref, dsc_ref, dsh_ref = refs[len(IN_PAD):]
        off = 0
        for ref, w in zip(p_refs, IN_PAD):
            dp_ref[:, off:off + w] = ref[...].astype(BF16)
            off += w
        du = _nn(dp_ref[...], w_ref[...])
        gx_ref[...] = gx1_ref[...] + du * sc_ref[...]
        _acc_rows(dsc_ref, _colsum(du * x_ref[...]))
        _acc_rows(dsh_ref, _colsum(du))

    vec = _whole((1, D_MODEL))
    return pl.pallas_call(
        body, name="inproj_bwd", grid=(s // tm,),
        in_specs=[_rows(tm, w) for w in IN_PAD] + [_whole_once((IN_PAD_WIDTH, D_MODEL)), _rows(tm, D_MODEL), vec,
                                                    _rows(tm, D_MODEL)],
        out_specs=[_rows(tm, D_MODEL), _rows(tm, IN_PAD_WIDTH), vec, vec],
        out_shape=[jax.ShapeDtypeStruct((s, D_MODEL), F32), jax.ShapeDtypeStruct((s, IN_PAD_WIDTH), BF16),
                   jax.ShapeDtypeStruct((1, D_MODEL), F32), jax.ShapeDtypeStruct((1, D_MODEL), F32)],
        compiler_params=_cparams("arbitrary"),
    )(*pieces, w_in_pt, x, scale1p, gx1)


def _matmul_tn_rows(name, a, b, tk):
    s, k = a.shape
    n = b.shape[1]
    tm = min(GRAD_ROWS, s)

    def body(a_ref, b_ref, o_ref):
        @pl.when(pl.program_id(1) == 0)
        def _():
            o_ref[...] = jnp.zeros_like(o_ref)
        o_ref[...] += _tn(a_ref[...], b_ref[...])

    return pl.pallas_call(
        body, name=name, grid=(k // tk, s // tm),
        in_specs=[pl.BlockSpec((tm, tk), lambda j, i: (i, j)), pl.BlockSpec((tm, n), lambda j, i: (i, 0))],
        out_specs=pl.BlockSpec((tk, n), lambda j, i: (j, 0)),
        out_shape=jax.ShapeDtypeStruct((k, n), F32),
        compiler_params=_cparams("parallel", "arbitrary"),
    )(a, b)


def _matmul_tn(name, a, b, tn):
    s, k = a.shape
    n = b.shape[1]
    tm = min(GRAD_ROWS, s)

    def body(a_ref, b_ref, o_ref):
        @pl.when(pl.program_id(1) == 0)
        def _():
            o_ref[...] = jnp.zeros_like(o_ref)
        o_ref[...] += _tn(a_ref[...], b_ref[...])

    return pl.pallas_call(
        body, name=name, grid=(n // tn, s // tm),
        in_specs=[pl.BlockSpec((tm, k), lambda j, i: (i, 0)), pl.BlockSpec((tm, tn), lambda j, i: (i, j))],
        out_specs=pl.BlockSpec((k, tn), lambda j, i: (0, j)),
        out_shape=jax.ShapeDtypeStruct((k, n), F32),
        compiler_params=_cparams("parallel", "arbitrary"),
    )(a, b)


def _pack_w_in_t(w_in_t):
    parts, off = [], 0
    for w, wp in zip(IN_SPLITS, IN_PAD):
        parts.append(jnp.pad(w_in_t[off:off + w], ((0, wp - w), (0, 0))))
        off += w
    return jnp.concatenate(parts, axis=0)


def _unpack_w_in_t(g):
    parts, off = [], 0
    for w, wp in zip(IN_SPLITS, IN_PAD):
        parts.append(g[off:off + w])
        off += wp
    return jnp.concatenate(parts, axis=0)


def _rope_tables(positions):
    inv_freq = 1.0 / (ROPE_THETA ** (jnp.arange(ROPE_HALF, dtype=F32) / ROPE_HALF))
    ang = positions.astype(F32)[:, None] * inv_freq
    cos, sin = jnp.cos(ang), jnp.sin(ang)
    zeros = jnp.zeros((positions.shape[0], LANE - QK_ROPE), F32)
    return jnp.concatenate([cos, cos, zeros], axis=1), jnp.concatenate([-sin, sin, zeros], axis=1)


def _local_step(x, tgt, positions, mod, w_in_t, q_norm_g, w_qb_p, kv_norm_g, w_kvb_p, conv_w, conv_b, dt_bias,
                a_log, d_skip, ssm_norm_g, w_out_b, ln_g, ln_b):
    row = lambda v: v.reshape(1, -1)
    shift, scale, gate = mod[:D_MODEL], mod[D_MODEL:2 * D_MODEL], mod[2 * D_MODEL:]
    scale1p = row(1.0 + scale)
    w_in_p = _pack_w_in_t(w_in_t)
    cos, sin = _rope_tables(positions)
    a_neg = row(jnp.pad(-jnp.exp(a_log), (0, LANE - SSM_HEADS)))
    dskip_x = row(jnp.repeat(d_skip, SSM_P))
    dt_bias_p = row(jnp.pad(dt_bias, (0, LANE - SSM_HEADS)))
    tri = jnp.tril(jnp.ones((CHUNK, CHUNK), F32))
    tril, triu = tri.astype(BF16), tri.T.astype(BF16)

    u_bf, q_lat, kv_lat, z_attn, xbc_raw, dt_raw, z_ssm = _inproj(x, scale1p, row(shift), w_in_p)
    nq_bf, q_att = _qpath(q_lat, row(q_norm_g), w_qb_p, cos, sin)
    nkv_bf, k_att, v_att = _kvpath(kv_lat, row(kv_norm_g), w_kvb_p, cos, sin)
    o, lse_rows = _attn_fwd(q_att, k_att, v_att)
    xbc, dt = _ssd_pre(xbc_raw, dt_raw, conv_w, row(conv_b), dt_bias_p)
    expand = jnp.repeat(jnp.eye(LANE, SSM_HEADS, dtype=BF16), SSM_P, axis=1)
    y, o_ssm, hin = _ssd_fwd(xbc, dt, z_ssm, a_neg, dskip_x, row(ssm_norm_g), tril, triu)
    (cat_bf, dmix_bf, gx1, do_bf, dz_attn, delta_rows, dos, loss, d_ln_g, d_ln_b, d_gate) = _outln(
        o, z_attn, o_ssm, w_out_b, x, row(gate), row(ln_g), row(ln_b), tgt)

    g_w_out = _matmul_tn("gw_out", cat_bf, dmix_bf, 512)
    dk_att, dv, dq_att = _attn_bwd(q_att, k_att, v_att, do_bf, lse_rows, delta_rows)
    dq_lat, dqraw_bf, d_q_norm_g = _qbwd(dq_att, q_lat, row(q_norm_g), w_qb_p, cos, sin)
    dkv_lat, dkvraw_bf, d_kv_norm_g = _kvbwd(dk_att, dv, kv_lat, row(kv_norm_g), w_kvb_p, cos, sin)
    g_w_qb = _matmul_tn("gw_qb", nq_bf, dqraw_bf, MLA_HEADS * HEAD_PAD)
    g_w_kvb = _matmul_tn("gw_kvb", nkv_bf, dkvraw_bf, MLA_HEADS * (QK_NOPE + V_DIM))
    dxa, ddt, dz_ssm, d_ssm_g, ddsk_x, d_a = _ssd_bwd(dos, y, z_ssm, xbc, dt, hin, a_neg, dskip_x, row(ssm_norm_g),
                                                       tril, triu, expand)
    dxbc_raw, ddt_raw, d_conv_w, d_conv_b, d_dt_bias = _ssd_post_bwd(xbc_raw, dxa, ddt, dt_raw, conv_w, row(conv_b),
                                                                     dt_bias_p)
    grad_x, dproj_bf, d_scale, d_shift = _inproj_bwd((dq_lat, dkv_lat, dz_attn, dxbc_raw, ddt_raw, dz_ssm),
                                                     w_in_p, x, scale1p, gx1)
    g_w_in_t = _unpack_w_in_t(_matmul_tn_rows("gw_in", dproj_bf, u_bf, 896))
    return dict(
        loss=loss[0, 0], grad_x=grad_x,
        dmod=jnp.concatenate([d_shift[0], d_scale[0], d_gate[0]]),
        w_in_t=g_w_in_t, q_norm_g=d_q_norm_g[0], w_qb=g_w_qb, kv_norm_g=d_kv_norm_g[0], w_kvb=g_w_kvb,
        conv_w=d_conv_w, conv_b=d_conv_b[0], dt_bias=d_dt_bias[0, :SSM_HEADS],
        a_log=d_a[0, :SSM_HEADS] * a_neg[0, :SSM_HEADS],
        d_skip=ddsk_x.reshape(SSM_HEADS, SSM_P).sum(axis=1), ssm_norm_g=d_ssm_g[0], w_out=g_w_out,
        ln_g=d_ln_g[0], ln_b=d_ln_b[0])


ADAM_ROWS = 512


def _my_index():
    return 4 * lax.axis_index("x") + 2 * lax.axis_index("y") + lax.axis_index("c")


def _exchange(name, sends, gather):
    n = len(sends)
    peers = N_DEV - 1

    def body(*refs):
        send_refs, recv_refs = refs[:n], refs[n:2 * n]
        send_sems, recv_sems, local_sems = refs[2 * n:]
        x, y, c = lax.axis_index("x"), lax.axis_index("y"), lax.axis_index("c")
        me = 4 * x + 2 * y + c

        def src(a, idx):
            return send_refs[a] if gather else send_refs[a].at[idx]

        owns = [pltpu.make_async_copy(src(a, me), recv_refs[a].at[me], local_sems.at[a]) for a in range(n)]
        for cp in owns:
            cp.start()
        copies = []
        for k in range(1, N_DEV):
            px, py, pc = x ^ ((k >> 2) & 1), y ^ ((k >> 1) & 1), c ^ (k & 1)
            peer = 4 * px + 2 * py + pc
            for a in range(n):
                copies.append(pltpu.make_async_remote_copy(
                    src_ref=src(a, peer), dst_ref=recv_refs[a].at[me],
                    send_sem=send_sems.at[a * peers + k - 1], recv_sem=recv_sems.at[a * peers + k - 1],
                    device_id=(px, py, pc), device_id_type=pl.DeviceIdType.MESH))
        for cp in copies:
            cp.start()
        for cp in copies:
            cp.wait()
        for cp in owns:
            cp.wait()

    block_shape = lambda a: a.shape if gather else a.shape[1:]
    return pl.pallas_call(
        body, name=name,
        in_specs=[pl.BlockSpec(memory_space=pl.ANY)] * n, out_specs=[pl.BlockSpec(memory_space=pl.ANY)] * n,
        out_shape=[jax.ShapeDtypeStruct((N_DEV, *block_shape(a)), a.dtype) for a in sends],
        scratch_shapes=[pltpu.SemaphoreType.DMA((n * peers,)), pltpu.SemaphoreType.DMA((n * peers,)),
                        pltpu.SemaphoreType.DMA((n,))],
    )(*sends)


def _gather_two_level(name, sends):
    n = len(sends)
    per = N_DEV - 1

    def body(*refs):
        send_refs, recv_refs = refs[:n], refs[n:2 * n]
        send_sems, recv_sems, local_sems = refs[2 * n:]
        x, y, c = lax.axis_index("x"), lax.axis_index("y"), lax.axis_index("c")
        sibling = (x, y, 1 - c)
        chips = [(1 - x, y), (x, 1 - y), (1 - x, 1 - y)]

        def idx(px, py, pc):
            return 4 * px + 2 * py + pc

        def copy(a, k, block, to, src=None):
            slot = recv_refs[a].at[idx(*block)]
            return pltpu.make_async_remote_copy(
                src_ref=slot if src is None else src, dst_ref=slot,
                send_sem=send_sems.at[a * per + k], recv_sem=recv_sems.at[a * per + k],
                device_id=to, device_id_type=pl.DeviceIdType.MESH)

        me = (x, y, c)
        owns = [pltpu.make_async_copy(send_refs[a], recv_refs[a].at[idx(*me)], local_sems.at[a]) for a in range(n)]
        for cp in owns:
            cp.start()
        first = [copy(a, 0, me, sibling, src=send_refs[a]) for a in range(n)]
        first += [copy(a, 1 + j, me, (*chip, c), src=send_refs[a]) for j, chip in enumerate(chips) for a in range(n)]
        for cp in first:
            cp.start()
        passed = []
        for j, chip in enumerate(chips):
            for a in range(n):
                copy(a, 1 + j, (*chip, c), me).wait_recv()
                fwd = copy(a, 4 + j, (*chip, c), sibling)
                fwd.start()
                passed.append(fwd)
        for a in range(n):
            copy(a, 0, sibling, me).wait_recv()
            for j, chip in enumerate(chips):
                copy(a, 4 + j, (*chip, 1 - c), me).wait_recv()
        for cp in first + passed:
            cp.wait_send()
        for cp in owns:
            cp.wait()

    return pl.pallas_call(
        body, name=name,
        in_specs=[pl.BlockSpec(memory_space=pl.ANY)] * n, out_specs=[pl.BlockSpec(memory_space=pl.ANY)] * n,
        out_shape=[jax.ShapeDtypeStruct((N_DEV, *a.shape), a.dtype) for a in sends],
        scratch_shapes=[pltpu.SemaphoreType.DMA((n * per,)), pltpu.SemaphoreType.DMA((n * per,)),
                        pltpu.SemaphoreType.DMA((n,))],
    )(*sends)


def _flat_rows(parts, row_multiple):
    flat = jnp.concatenate([p.reshape(-1) for p in parts])
    chunk = row_multiple * LANE
    total = -(-flat.shape[0] // chunk) * chunk
    return jnp.pad(flat, (0, total - flat.shape[0])).reshape(-1, LANE)


def _unflat(flat, shapes):
    flat = flat.reshape(-1)
    out, off = [], 0
    for shp in shapes:
        n = math.prod(shp)
        out.append(flat[off:off + n].reshape(shp))
        off += n
    return out


def _adam_update(g, w, m, v):
    m2 = ADAM_B1 * m + (1.0 - ADAM_B1) * g
    v2 = ADAM_B2 * v + (1.0 - ADAM_B2) * (g * g)
    m_hat = m2 / (1.0 - ADAM_B1 ** ADAM_STEP)
    v_hat = v2 / (1.0 - ADAM_B2 ** ADAM_STEP)
    delta = -ADAM_LR * (m_hat / (jnp.sqrt(v_hat) + ADAM_EPS) + ADAM_WD * w)
    return delta, m2, v2


def _adamw_summed(name, parts, w, m, v):
    r = w.shape[0]
    tr = min(ADAM_ROWS, r)

    def body(p_ref, w_ref, m_ref, v_ref, g_ref, d_ref, m2_ref, v2_ref):
        g = p_ref[0]
        for j in range(1, N_DEV):
            g = g + p_ref[j]
        g_ref[...] = g
        d_ref[...], m2_ref[...], v2_ref[...] = _adam_update(g, w_ref[...], m_ref[...], v_ref[...])

    rows = _rows(tr, LANE)
    return pl.pallas_call(
        body, name=name, grid=(r // tr,),
        in_specs=[pl.BlockSpec((N_DEV, tr, LANE), lambda i: (0, i, 0)), rows, rows, rows],
        out_specs=[rows] * 4, out_shape=[jax.ShapeDtypeStruct((r, LANE), F32)] * 4,
        compiler_params=_cparams("parallel"),
    )(parts, w, m, v)


def _modpart(c_all, w_ada, b_cols):
    def body(c_ref, w_ref, b_ref, o_ref):
        o_ref[...] = _nn(c_ref[...].astype(BF16), w_ref[...].astype(BF16)) + b_ref[...]

    return pl.pallas_call(
        body, name="modpart", out_shape=jax.ShapeDtypeStruct((N_DEV, w_ada.shape[1]), F32),
    )(c_all, w_ada, b_cols)


def _adamw_w_ada(c_all_t, dmod_cols, w, m, v):
    def body(c_ref, d_ref, w_ref, m_ref, v_ref, g_ref, dl_ref, m2_ref, v2_ref):
        g = c_ref[:, 0:1] * d_ref[0:1, :]
        for b in range(1, N_DEV):
            g = g + c_ref[:, b:b + 1] * d_ref[b:b + 1, :]
        g_ref[...] = g
        dl_ref[...], m2_ref[...], v2_ref[...] = _adam_update(g, w_ref[...], m_ref[...], v_ref[...])

    return pl.pallas_call(
        body, name="adamw_w_ada", out_shape=[jax.ShapeDtypeStruct(w.shape, F32)] * 4,
        compiler_params=pltpu.CompilerParams(vmem_limit_bytes=VMEM_LIMIT),
    )(c_all_t, dmod_cols, w, m, v)


W_IN_SHARD = IN_WIDTH // N_DEV
W_IN_SHARD_LANES = -(-W_IN_SHARD // LANE) * LANE
BF16_ROWS = 16
W_IN_SEND_ROWS = -(-W_IN_SHARD // BF16_ROWS) * BF16_ROWS


def _transpose_cast(w_pad):
    def body(w_ref, o_ref):
        o_ref[...] = w_ref[...].T.astype(BF16)

    return pl.pallas_call(
        body, name="w_in_transpose", out_shape=jax.ShapeDtypeStruct(w_pad.shape[::-1], BF16),
        compiler_params=pltpu.CompilerParams(vmem_limit_bytes=VMEM_LIMIT),
    )(w_pad)


def _adamw_w_in(parts, w, m, v):
    rows_t = parts.shape[1]
    d, cols = w.shape
    tb = ROW_TILE

    def body(p_ref, w_ref, m_ref, v_ref, g_ref, d_ref, m2_ref, v2_ref):
        gt = p_ref[0].astype(F32)
        for j in range(1, N_DEV):
            gt = gt + p_ref[j].astype(F32)
        gt = jnp.concatenate([gt, jnp.zeros((W_IN_SHARD_LANES - rows_t, tb), F32)], axis=0)
        g = gt.T[:, :cols]
        g_ref[...] = g
        d_ref[...], m2_ref[...], v2_ref[...] = _adam_update(g, w_ref[...], m_ref[...], v_ref[...])

    blk = _rows(tb, cols)
    return pl.pallas_call(
        body, name="adamw_w_in", grid=(d // tb,),
        in_specs=[pl.BlockSpec((N_DEV, rows_t, tb), lambda i: (0, 0, i)), blk, blk, blk],
        out_specs=[blk] * 4, out_shape=[jax.ShapeDtypeStruct(w.shape, F32)] * 4,
        compiler_params=_cparams("parallel"),
    )(parts, w, m, v)


SHARDED = ("w_qb", "w_kvb", "w_out")
REPLICATED = ("b_ada", "q_norm_g", "kv_norm_g", "conv_b", "dt_bias", "a_log", "d_skip", "ssm_norm_g", "ln_g", "ln_b")
WEIGHTS = ("w_ada", "b_ada", "w_in", "q_norm_g", "w_qb", "kv_norm_g", "w_kvb", "conv_w", "conv_b", "dt_bias",
           "a_log", "d_skip", "ssm_norm_g", "w_out", "ln_g", "ln_b")
HEAD_COLS = QK_NOPE + V_DIM


def _adamw_blocks(name, parts, w, m, v):
    r, c = w.shape
    tr = ROW_TILE if r % ROW_TILE == 0 else r

    def body(p_ref, w_ref, m_ref, v_ref, g_ref, d_ref, m2_ref, v2_ref):
        g = p_ref[0].astype(F32)
        for j in range(1, N_DEV):
            g = g + p_ref[j].astype(F32)
        g_ref[...] = g
        d_ref[...], m2_ref[...], v2_ref[...] = _adam_update(g, w_ref[...], m_ref[...], v_ref[...])

    blk = _rows(tr, c)
    return pl.pallas_call(
        body, name=name, grid=(r // tr,),
        in_specs=[pl.BlockSpec((N_DEV, tr, c), lambda i: (0, i, 0)), blk, blk, blk],
        out_specs=[blk] * 4, out_shape=[jax.ShapeDtypeStruct(w.shape, F32)] * 4,
        compiler_params=_cparams("parallel"),
    )(parts, w, m, v)


def kernel(x, c, positions, w_ada, b_ada, w_in, q_norm_g, w_qb, kv_norm_g, w_kvb, conv_w, conv_b, dt_bias, a_log, d_skip, ssm_norm_g, w_out, ln_g, ln_b, loss_target, m_w_ada, m_b_ada, m_w_in, m_q_norm_g, m_w_qb, m_kv_norm_g, m_w_kvb, m_conv_w, m_conv_b, m_dt_bias, m_a_log, m_d_skip, m_ssm_norm_g, m_w_out, m_ln_g, m_ln_b, v_w_ada, v_b_ada, v_w_in, v_q_norm_g, v_w_qb, v_kv_norm_g, v_w_kvb, v_conv_w, v_conv_b, v_dt_bias, v_a_log, v_d_skip, v_ssm_norm_g, v_w_out, v_ln_g, v_ln_b):
    given = dict(w_ada=w_ada, b_ada=b_ada, w_in=w_in, q_norm_g=q_norm_g, w_qb=w_qb, kv_norm_g=kv_norm_g, w_kvb=w_kvb,
                 conv_w=conv_w, conv_b=conv_b, dt_bias=dt_bias, a_log=a_log, d_skip=d_skip, ssm_norm_g=ssm_norm_g,
                 w_out=w_out, ln_g=ln_g, ln_b=ln_b)
    mom = dict(w_ada=m_w_ada, b_ada=m_b_ada, w_in=m_w_in, q_norm_g=m_q_norm_g, w_qb=m_w_qb, kv_norm_g=m_kv_norm_g,
               w_kvb=m_w_kvb, conv_w=m_conv_w, conv_b=m_conv_b, dt_bias=m_dt_bias, a_log=m_a_log, d_skip=m_d_skip,
               ssm_norm_g=m_ssm_norm_g, w_out=m_w_out, ln_g=m_ln_g, ln_b=m_ln_b)
    var = dict(w_ada=v_w_ada, b_ada=v_b_ada, w_in=v_w_in, q_norm_g=v_q_norm_g, w_qb=v_w_qb, kv_norm_g=v_kv_norm_g,
               w_kvb=v_w_kvb, conv_w=v_conv_w, conv_b=v_conv_b, dt_bias=v_dt_bias, a_log=v_a_log, d_skip=v_d_skip,
               ssm_norm_g=v_ssm_norm_g, w_out=v_w_out, ln_g=v_ln_g, ln_b=v_ln_b)
    w0 = {k: a[0] for k, a in given.items()}
    m0 = {k: a[0] for k, a in mom.items()}
    v0 = {k: a[0] for k, a in var.items()}
    me = _my_index()

    w_in_rows = _transpose_cast(jnp.pad(w0["w_in"], ((0, 0), (0, W_IN_SHARD_LANES - W_IN_SHARD))))
    g_w_in, g_w_qb, g_w_kvb, g_w_out, g_conv_w, c_all = _gather_two_level(
        "gather_weights", [w_in_rows] + [w0[k].astype(BF16) for k in SHARDED] + [w0["conv_w"], c])
    c_all = c_all.reshape(N_DEV, D_MODEL)
    w_in_t = g_w_in[:, :W_IN_SHARD, :].reshape(IN_WIDTH, D_MODEL)
    w_qb_p = jnp.pad(g_w_qb, ((0, 0), (0, 0), (0, HEAD_PAD - QK_HEAD))).transpose(1, 0, 2).reshape(Q_RANK, -1)
    w_kvb_p = g_w_kvb.reshape(N_DEV, KV_RANK, 2, QK_NOPE).transpose(1, 2, 0, 3).reshape(KV_RANK, -1)
    w_out_b = g_w_out.reshape(MIX_WIDTH, D_MODEL)
    conv_w_full = g_conv_w.transpose(1, 0, 2).reshape(CONV_K, CONV_CH)

    ada_cols = w0["w_ada"].shape[1]
    b_cols = lax.dynamic_slice(w0["b_ada"], (me * ada_cols,), (ada_cols,)).reshape(1, ada_cols)
    mod_all, = _exchange("gather_mod", [_modpart(c_all, w0["w_ada"], b_cols)], gather=True)
    mod = lax.dynamic_index_in_dim(mod_all, me, axis=1, keepdims=False).reshape(-1)

    loc = _local_step(x[0], loss_target[0], positions[0], mod, w_in_t, w0["q_norm_g"], w_qb_p,
                      w0["kv_norm_g"], w_kvb_p, conv_w_full, w0["conv_b"], w0["dt_bias"], w0["a_log"],
                      w0["d_skip"], w0["ssm_norm_g"], w_out_b, w0["ln_g"], w0["ln_b"])

    rep_shapes = [w0[k].shape for k in REPLICATED] + [(1,)]
    rep_local = [loc["dmod"]] + [loc[k] for k in REPLICATED[1:]] + [loc["loss"].reshape(1)]
    rep_parts, conv_parts = _exchange("gather_small", [_flat_rows(rep_local, HALO), loc["conv_w"]], gather=True)
    conv_cols = w0["conv_w"].shape[1]
    conv_mine = lax.dynamic_slice(conv_parts, (0, 0, me * conv_cols), (N_DEV, CONV_K, conv_cols))
    outs = {"conv_w": _adamw_blocks("adamw_conv_w", conv_mine, w0["conv_w"], m0["conv_w"], v0["conv_w"])}
    zero1 = jnp.zeros((1,), F32)
    rep = _adamw_summed("adamw_replicated", rep_parts,
                        _flat_rows([w0[k] for k in REPLICATED] + [zero1], HALO),
                        _flat_rows([m0[k] for k in REPLICATED] + [zero1], HALO),
                        _flat_rows([v0[k] for k in REPLICATED] + [zero1], HALO))
    rep_g, rep_d, rep_m, rep_v = [_unflat(a, rep_shapes) for a in rep]
    loss = rep_g[-1][0]

    dmod_all = rep_parts.reshape(N_DEV, -1)[:, :3 * D_MODEL]
    dmod_cols = lax.dynamic_slice(dmod_all, (0, me * ada_cols), (N_DEV, ada_cols))
    outs["w_ada"] = _adamw_w_ada(c_all.T, dmod_cols, w0["w_ada"], m0["w_ada"], v0["w_ada"])

    send_w_in = loc["w_in_t"].astype(BF16).reshape(N_DEV, W_IN_SHARD, D_MODEL)
    send_w_in = jnp.pad(send_w_in, ((0, 0), (0, W_IN_SEND_ROWS - W_IN_SHARD), (0, 0)))
    send_w_qb = loc["w_qb"].astype(BF16).reshape(Q_RANK, N_DEV, HEAD_PAD)[:, :, :QK_HEAD].transpose(1, 0, 2)
    send_w_kvb = loc["w_kvb"].astype(BF16).reshape(KV_RANK, 2, N_DEV, QK_NOPE).transpose(2, 0, 1, 3)
    send_w_kvb = send_w_kvb.reshape(N_DEV, KV_RANK, HEAD_COLS)
    send_w_out = loc["w_out"].astype(BF16).reshape(N_DEV, MIX_WIDTH // N_DEV, D_MODEL)
    r_w_in, r_w_qb, r_w_kvb, r_w_out = _exchange(
        "scatter_grads", [send_w_in, send_w_qb, send_w_kvb, send_w_out], gather=False)
    outs["w_in"] = _adamw_w_in(r_w_in, w0["w_in"], m0["w_in"], v0["w_in"])
    for k, parts in zip(SHARDED, (r_w_qb, r_w_kvb, r_w_out)):
        outs[k] = _adamw_blocks("adamw_" + k, parts, w0[k], m0[k], v0[k])

    def collect(idx):
        out = {k: o[idx] for k, o in outs.items()}
        out.update({k: (rep_g, rep_d, rep_m, rep_v)[idx][i] for i, k in enumerate(REPLICATED)})
        return [out[k][None] for k in WEIGHTS]

    return (loss, loc["grad_x"][None], *collect(0), *collect(1), *collect(2), *collect(3))
```

```python
import math

import jax
import jax.numpy as jnp
from jax import lax
from jax.experimental import pallas as pl
from jax.experimental.pallas import tpu as pltpu

F32 = jnp.float32
BF16 = jnp.bfloat16

N_DEV = 8
D_MODEL = 1024
MLA_HEADS = 8
QK_NOPE = 128
QK_ROPE = 64
V_DIM = 128
Q_RANK = 384
KV_RANK = 256
QK_HEAD = QK_NOPE + QK_ROPE
HEAD_PAD = 256
ROPE_HALF = QK_ROPE // 2
ROPE_THETA = 10000.0
MLA_WIDTH = MLA_HEADS * V_DIM
SSM_HEADS = 16
SSM_P = 64
SSM_WIDTH = SSM_HEADS * SSM_P
SSM_GROUPS = 2
SSM_N = 128
CONV_K = 4
CHUNK = 128
CONV_CH = SSM_WIDTH + 2 * SSM_GROUPS * SSM_N
MIX_WIDTH = MLA_WIDTH + SSM_WIDTH
IN_SPLITS = (Q_RANK, KV_RANK + QK_ROPE, MLA_WIDTH, CONV_CH, SSM_HEADS, SSM_WIDTH)
IN_WIDTH = sum(IN_SPLITS)
LANE = 128
KV_LAT_PAD = KV_RANK + LANE
IN_PAD = (Q_RANK, KV_LAT_PAD, MLA_WIDTH, CONV_CH, LANE, SSM_WIDTH)
IN_PAD_WIDTH = sum(IN_PAD)
DEEPNORM_ALPHA = 2.0 ** 0.25
RMS_EPS = 1e-6
LN_EPS = 1e-5
ATTN_SCALE = QK_HEAD ** -0.5
LOG2E = math.log2(math.e)
LN2 = math.log(2.0)
Q_PRESCALE = ATTN_SCALE * LOG2E
ADAM_LR, ADAM_B1, ADAM_B2, ADAM_EPS, ADAM_WD, ADAM_STEP = 0.001, 0.9, 0.999, 1e-08, 0.01, 10

ROW_TILE = 512
ROW_TILE_WIDE = 256
ATTN_TILE = 512
ATTN_UNROLLS = (8, 4, 2)
SSD_ROWS = 512
GRAD_ROWS = 2048
VMEM_LIMIT = 56 * 1024 * 1024


def _nn(a, b):
    return jnp.dot(a, b, preferred_element_type=F32)


def _nt(a, b):
    return lax.dot_general(a, b, (((1,), (1,)), ((), ())), preferred_element_type=F32)


def _tn(a, b):
    return lax.dot_general(a, b, (((0,), (0,)), ((), ())), preferred_element_type=F32)


def _cparams(*sem):
    return pltpu.CompilerParams(dimension_semantics=sem, vmem_limit_bytes=VMEM_LIMIT)


def _rows(tm, w):
    return pl.BlockSpec((tm, w), lambda i: (i, 0))


def _whole(shape):
    return pl.BlockSpec(shape, lambda i: (0,) * len(shape))


def _whole_once(shape):
    return pl.BlockSpec(shape, lambda i: (0,) * len(shape), pipeline_mode=pl.Buffered(1))


def _sigmoid(z):
    return 1.0 / (1.0 + jnp.exp(-z))


def _lane_iota(shape):
    return lax.broadcasted_iota(jnp.int32, shape, len(shape) - 1)


def _swap_halves(r):
    lane = _lane_iota(r.shape)
    return jnp.where(lane < ROPE_HALF, pltpu.roll(r, LANE - ROPE_HALF, 1),
                     jnp.where(lane < QK_ROPE, pltpu.roll(r, ROPE_HALF, 1), 0.0))


def _rope(r, cos, sin):
    return r * cos + _swap_halves(r) * sin


def _rope_transposed(d, cos, sin):
    return d * cos + _swap_halves(d * sin)


def _rms(x):
    rstd = lax.rsqrt(jnp.mean(x * x, axis=-1, keepdims=True) + RMS_EPS)
    return x * rstd, rstd


def _rms_bwd(dxhat, xhat, rstd):
    return rstd * (dxhat - xhat * jnp.mean(dxhat * xhat, axis=-1, keepdims=True))


def _acc_rows(ref, val):
    @pl.when(pl.program_id(0) == 0)
    def _():
        ref[...] = jnp.zeros_like(ref)
    ref[...] += val


def _colsum(v):
    return jnp.sum(v, axis=0, keepdims=True)


def _inproj(x, scale1p, shift, w_in_pt):
    s = x.shape[0]
    tm = ROW_TILE

    def body(x_ref, sc_ref, sh_ref, w_ref, u_ref, *outs):
        u = (x_ref[...] * sc_ref[...] + sh_ref[...]).astype(BF16)
        u_ref[...] = u
        proj = _nt(u, w_ref[...])
        off = 0
        for ref, w in zip(outs, IN_PAD):
            ref[...] = proj[:, off:off + w]
            off += w

    return pl.pallas_call(
        body, name="inproj", grid=(s // tm,),
        in_specs=[_rows(tm, D_MODEL), _whole((1, D_MODEL)), _whole((1, D_MODEL)), _whole((IN_PAD_WIDTH, D_MODEL))],
        out_specs=[_rows(tm, D_MODEL)] + [_rows(tm, w) for w in IN_PAD],
        out_shape=[jax.ShapeDtypeStruct((s, D_MODEL), BF16)] + [jax.ShapeDtypeStruct((s, w), F32) for w in IN_PAD],
        compiler_params=_cparams("parallel"),
    )(x, scale1p, shift, w_in_pt)


def _qpath(q_lat, g_q, w_qb_p, cos, sin):
    s = q_lat.shape[0]
    tm = ROW_TILE

    def body(ql_ref, g_ref, w_ref, cos_ref, sin_ref, nq_ref, q_ref):
        xhat, _ = _rms(ql_ref[...])
        nq = (xhat * g_ref[...]).astype(BF16)
        nq_ref[...] = nq
        raw = _nn(nq, w_ref[...]) * Q_PRESCALE
        c, sn = cos_ref[...], sin_ref[...]
        for h in range(MLA_HEADS):
            o = h * HEAD_PAD
            q_ref[:, o:o + QK_NOPE] = raw[:, o:o + QK_NOPE].astype(BF16)
            q_ref[:, o + QK_NOPE:o + HEAD_PAD] = _rope(raw[:, o + QK_NOPE:o + HEAD_PAD], c, sn).astype(BF16)

    return pl.pallas_call(
        body, name="qpath", grid=(s // tm,),
        in_specs=[_rows(tm, Q_RANK), _whole((1, Q_RANK)), _whole((Q_RANK, MLA_HEADS * HEAD_PAD)),
                  _rows(tm, LANE), _rows(tm, LANE)],
        out_specs=[_rows(tm, Q_RANK), _rows(tm, MLA_HEADS * HEAD_PAD)],
        out_shape=[jax.ShapeDtypeStruct((s, Q_RANK), BF16), jax.ShapeDtypeStruct((s, MLA_HEADS * HEAD_PAD), BF16)],
        compiler_params=_cparams("parallel"),
    )(q_lat, g_q, w_qb_p, cos, sin)


def _kvpath(kv_lat, g_kv, w_kvb_p, cos, sin):
    s = kv_lat.shape[0]
    tm = ROW_TILE

    def body(kl_ref, g_ref, w_ref, cos_ref, sin_ref, nkv_ref, k_ref, v_ref, vt_ref):
        kl = kl_ref[...]
        xhat, _ = _rms(kl[:, :KV_RANK])
        nkv = (xhat * g_ref[...]).astype(BF16)
        nkv_ref[...] = nkv
        raw = _nn(nkv, w_ref[...])
        kr = _rope(kl[:, KV_RANK:], cos_ref[...], sin_ref[...]).astype(BF16)
        for h in range(MLA_HEADS):
            o = h * HEAD_PAD
            k_ref[:, o:o + QK_NOPE] = raw[:, h * QK_NOPE:(h + 1) * QK_NOPE].astype(BF16)
            k_ref[:, o + QK_NOPE:o + HEAD_PAD] = kr
        vals = raw[:, MLA_HEADS * QK_NOPE:]
        v_ref[...] = vals.astype(BF16)
        vt_ref[...] = vals.T.astype(BF16)

    return pl.pallas_call(
        body, name="kvpath", grid=(s // tm,),
        in_specs=[_rows(tm, KV_LAT_PAD), _whole((1, KV_RANK)), _whole((KV_RANK, MLA_HEADS * (QK_NOPE + V_DIM))),
                  _rows(tm, LANE), _rows(tm, LANE)],
        out_specs=[_rows(tm, KV_RANK), _rows(tm, MLA_HEADS * HEAD_PAD), _rows(tm, MLA_WIDTH),
                   pl.BlockSpec((MLA_WIDTH, tm), lambda i: (0, i))],
        out_shape=[jax.ShapeDtypeStruct((s, KV_RANK), BF16), jax.ShapeDtypeStruct((s, MLA_HEADS * HEAD_PAD), BF16),
                   jax.ShapeDtypeStruct((s, MLA_WIDTH), BF16), jax.ShapeDtypeStruct((MLA_WIDTH, s), BF16)],
        compiler_params=_cparams("parallel"),
    )(kv_lat, g_kv, w_kvb_p, cos, sin)


def _causal_mask(t):
    row = lax.broadcasted_iota(jnp.int32, (t, t), 0)
    col = lax.broadcasted_iota(jnp.int32, (t, t), 1)
    return row, col


def _attn_fwd(q, k, vt):
    s = q.shape[0]
    t = min(ATTN_TILE, s)
    nq = s // t

    def body(q_ref, k_ref, vt_ref, o_ref, lse_ref, m_sc, l_sc, acc_sc, sa_sc, sb_sc):
        i = pl.program_id(1)
        qv = q_ref[...]
        m_sc[...] = jnp.full(m_sc.shape, -jnp.inf, F32)
        l_sc[...] = jnp.zeros(l_sc.shape, F32)
        acc_sc[...] = jnp.zeros(acc_sc.shape, F32)

        def scores(j, s_ref):
            s_ref[...] = _nt(k_ref[pl.ds(pl.multiple_of(j * t, t), t), :], qv)

        def update(s_ref, j, masked):
            vt = vt_ref[:, pl.ds(pl.multiple_of(j * t, t), t)]
            sc = s_ref[...]
            if masked:
                row, col = _causal_mask(t)
                sc = jnp.where(row <= col, sc, -jnp.inf)
            m_prev = m_sc[...]
            m_new = jnp.maximum(m_prev, jnp.max(sc, axis=0, keepdims=True))
            alpha = jnp.exp2(m_prev - m_new)
            p = jnp.exp2(sc - m_new)
            l_sc[...] = alpha * l_sc[...] + jnp.sum(p, axis=0, keepdims=True)
            acc_sc[...] = alpha * acc_sc[...] + _nn(vt, p.astype(BF16))
            m_sc[...] = m_new

        def run(j0, count):
            bufs = (sa_sc, sb_sc)
            for u in range(count):
                scores(j0 + u + 1, bufs[(u + 1) % 2])
                update(bufs[u % 2], j0 + u, False)

        scores(0, sa_sc)
        done = 0
        for group in ATTN_UNROLLS:
            def body_(g, carry, base=done, group=group):
                run(base + group * g, group)
                return carry

            n_groups = lax.div(i - done, group)
            lax.fori_loop(0, n_groups, body_, 0)
            done = done + group * n_groups
        odd = lax.rem(i, 2)

        @pl.when(odd == 1)
        def _():
            scores(i, sb_sc)
            update(sa_sc, i - 1, False)
            update(sb_sc, i, True)

        @pl.when(odd == 0)
        def _():
            update(sa_sc, i, True)

        l = l_sc[...]
        o_ref[...] = (acc_sc[...] / l).T
        lse_ref[0] = m_sc[...] + jnp.log2(l)

    return pl.pallas_call(
        body, name="attn_fwd", grid=(MLA_HEADS, nq),
        in_specs=[pl.BlockSpec((t, HEAD_PAD), lambda h, i: (i, h)),
                  pl.BlockSpec((s, HEAD_PAD), lambda h, i: (0, h)),
                  pl.BlockSpec((V_DIM, s), lambda h, i: (h, 0))],
        out_specs=[pl.BlockSpec((t, V_DIM), lambda h, i: (i, h)), pl.BlockSpec((1, 1, t), lambda h, i: (h, 0, i))],
        out_shape=[jax.ShapeDtypeStruct((s, MLA_WIDTH), F32), jax.ShapeDtypeStruct((MLA_HEADS, 1, s), F32)],
        scratch_shapes=[pltpu.VMEM((1, t), F32), pltpu.VMEM((1, t), F32), pltpu.VMEM((V_DIM, t), F32),
                        pltpu.VMEM((t, t), F32), pltpu.VMEM((t, t), F32)],
        compiler_params=_cparams("parallel", "arbitrary"),
    )(q, k, vt)


def _attn_bwd(q, k, v, do, lse_row, delta_row):
    s = q.shape[0]
    t = min(ATTN_TILE, s)
    nq = s // t

    def body(q_ref, k_ref, v_ref, do_ref, lse_ref, dl_ref, dk_ref, dv_ref, dq_hbm,
             dq_sc, dk_sc, dv_sc, sa_sc, sb_sc, pa_sc, pb_sc, sem, stage_sc):
        h = pl.program_id(0)
        j = pl.program_id(1)
        kv_ = k_ref[...]
        vv = v_ref[...]

        @pl.when(j == 0)
        def _():
            dq_sc[...] = jnp.zeros(dq_sc.shape, F32)

        dk_sc[...] = jnp.zeros(dk_sc.shape, F32)
        dv_sc[...] = jnp.zeros(dv_sc.shape, F32)

        def scores(i, s_ref, p_ref):
            off = pl.multiple_of(i * t, t)
            s_ref[...] = _nt(kv_, q_ref[pl.ds(off, t), :])
            p_ref[...] = _nt(vv, do_ref[pl.ds(off, t), :])

        def update(i, s_ref, p_ref, masked):
            off = pl.multiple_of(i * t, t)
            qv = q_ref[pl.ds(off, t), :]
            dov = do_ref[pl.ds(off, t), :]
            sct = s_ref[...]
            if masked:
                row, col = _causal_mask(t)
                sct = jnp.where(row <= col, sct, -jnp.inf)
            pt = jnp.exp2(sct - lse_ref[0, :, pl.ds(off, t)])
            gt = (pt * (p_ref[...] - dl_ref[0, :, pl.ds(off, t)])).astype(BF16)
            dv_sc[...] += _nn(pt.astype(BF16), dov)
            dk_sc[...] += _nn(gt, qv)
            dq_sc[pl.ds(off, t), :] += _tn(gt, kv_)

        rest = nq - 1 - j
        scores(j, sa_sc, pa_sc)

        @pl.when(rest >= 1)
        def _():
            scores(j + 1, sb_sc, pb_sc)

        update(j, sa_sc, pa_sc, True)

        def run(i0, count):
            bufs = ((sb_sc, pb_sc), (sa_sc, pa_sc))
            for u in range(count):
                scores(i0 + u + 1, *bufs[(u + 1) % 2])
                update(i0 + u, *bufs[u % 2], False)

        i1, left = j + 1, rest
        for group in ATTN_UNROLLS:
            def body_(g, carry, base=i1, group=group):
                run(base + group * g, group)
                return carry

            n_groups = jnp.where(left >= 1, lax.div(left - 1, group), 0)
            lax.fori_loop(0, n_groups, body_, 0)
            i1 = i1 + group * n_groups
            left = left - group * n_groups

        @pl.when(left == 1)
        def _():
            update(i1, sb_sc, pb_sc, False)

        @pl.when(left == 2)
        def _():
            scores(i1 + 1, sa_sc, pa_sc)
            update(i1, sb_sc, pb_sc, False)
            update(i1 + 1, sa_sc, pa_sc, False)

        dk_ref[...] = (dk_sc[...] * LN2).astype(BF16)
        dv_ref[...] = dv_sc[...].astype(BF16)

        def out_copy(jj):
            rows = pl.ds(pl.multiple_of(jj * t, t), t)
            return pltpu.make_async_copy(stage_sc, dq_hbm.at[h, rows, :], sem)

        @pl.when(j > 0)
        def _():
            out_copy(j - 1).wait()

        stage_sc[...] = dq_sc[pl.ds(pl.multiple_of(j * t, t), t), :].astype(BF16)
        out_copy(j).start()

        @pl.when(j == nq - 1)
        def _():
            out_copy(j).wait()

    return pl.pallas_call(
        body, name="attn_bwd", grid=(MLA_HEADS, nq),
        in_specs=[pl.BlockSpec((s, HEAD_PAD), lambda h, j: (0, h)),
                  pl.BlockSpec((t, HEAD_PAD), lambda h, j: (j, h)),
                  pl.BlockSpec((t, V_DIM), lambda h, j: (j, h)),
                  pl.BlockSpec((s, V_DIM), lambda h, j: (0, h)),
                  pl.BlockSpec((1, 1, s), lambda h, j: (h, 0, 0)),
                  pl.BlockSpec((1, 1, s), lambda h, j: (h, 0, 0))],
        out_specs=[pl.BlockSpec((t, HEAD_PAD), lambda h, j: (j, h)), pl.BlockSpec((t, V_DIM), lambda h, j: (j, h)),
                   pl.BlockSpec(memory_space=pl.ANY)],
        out_shape=[jax.ShapeDtypeStruct((s, MLA_HEADS * HEAD_PAD), BF16), jax.ShapeDtypeStruct((s, MLA_WIDTH), BF16),
                   jax.ShapeDtypeStruct((MLA_HEADS, s, HEAD_PAD), BF16)],
        scratch_shapes=[pltpu.VMEM((s, HEAD_PAD), F32), pltpu.VMEM((t, HEAD_PAD), F32), pltpu.VMEM((t, V_DIM), F32),
                        pltpu.VMEM((t, t), F32), pltpu.VMEM((t, t), F32), pltpu.VMEM((t, t), F32),
                        pltpu.VMEM((t, t), F32), pltpu.SemaphoreType.DMA, pltpu.VMEM((t, HEAD_PAD), BF16)],
        compiler_params=_cparams("arbitrary", "arbitrary"),
    )(q, k, v, do, lse_row, delta_row)


HALO = 8


def _silu(z):
    return z * _sigmoid(z)


def _silu_grad(z):
    sg = _sigmoid(z)
    return sg * (1.0 + z * (1.0 - sg))


def _softplus(x):
    e = jnp.exp(-jnp.abs(x))
    small = e * (1.0 - e * (0.5 - e * (1.0 / 3.0)))
    return jnp.maximum(x, 0.0) + jnp.where(e < 1e-3, small, jnp.log(1.0 + e))


def _conv_taps(xe_ref, w, tm, first):
    acc = None
    for k in range(CONV_K):
        term = xe_ref[pl.ds(HALO + first - (CONV_K - 1) + k, tm), :] * w[k:k + 1, :]
        acc = term if acc is None else acc + term
    return acc


def _ssd_pre(xbc_raw, dt_raw, conv_w, conv_b, dt_bias_p):
    s = xbc_raw.shape[0]
    tm = ROW_TILE
    hb = tm // HALO

    def body(x_ref, prev_ref, dtr_ref, w_ref, b_ref, db_ref, act_ref, dt_ref, xe_sc):
        i = pl.program_id(0)
        xe_sc[pl.ds(0, HALO), :] = jnp.where(i > 0, prev_ref[...], 0.0)
        xe_sc[pl.ds(HALO, tm), :] = x_ref[...]
        pre = _conv_taps(xe_sc, w_ref[...], tm, 0) + b_ref[...]
        act_ref[...] = _silu(pre)
        dt_ref[...] = _softplus(dtr_ref[...] + db_ref[...])

    return pl.pallas_call(
        body, name="ssd_pre", grid=(s // tm,),
        in_specs=[_rows(tm, CONV_CH), pl.BlockSpec((HALO, CONV_CH), lambda i: (jnp.maximum(i * hb - 1, 0), 0)),
                  _rows(tm, LANE), _whole((CONV_K, CONV_CH)), _whole((1, CONV_CH)), _whole((1, LANE))],
        out_specs=[_rows(tm, CONV_CH), _rows(tm, LANE)],
        out_shape=[jax.ShapeDtypeStruct((s, CONV_CH), F32), jax.ShapeDtypeStruct((s, LANE), F32)],
        scratch_shapes=[pltpu.VMEM((tm + HALO, CONV_CH), F32)],
        compiler_params=_cparams("parallel"),
    )(xbc_raw, xbc_raw, dt_raw, conv_w, conv_b, dt_bias_p)


def _split3(a):
    a1 = a.astype(BF16)
    r1 = a - a1.astype(F32)
    a2 = r1.astype(BF16)
    a3 = (r1 - a2.astype(F32)).astype(BF16)
    return a1, a2, a3


def _tri_left(tri, a):
    a1, a2, a3 = _split3(a)
    return _nn(tri, a1) + _nn(tri, a2) + _nn(tri, a3)


def _tri_right(a, tri):
    a1, a2, a3 = _split3(a)
    return _nn(a1, tri) + _nn(a2, tri) + _nn(a3, tri)


def _pair_sel(lane_lo, col_a, col_b):
    return jnp.where(lane_lo, col_a, col_b)


def _chunk_common(dt, a_neg, tril, triu):
    a = dt * a_neg
    lam_c = _tri_left(tril, a)
    lam_r = _tri_right(a.T, triu)
    lam_last = lam_c[CHUNK - 1:CHUNK, :]
    return lam_c, lam_r, lam_last


def _gated_norm_fwd(y, z, g):
    hf = y * _silu(z)
    outs = []
    for grp in range(SSM_GROUPS):
        w = SSM_WIDTH // SSM_GROUPS
        n, _ = _rms(hf[:, grp * w:(grp + 1) * w])
        outs.append(n)
    return jnp.concatenate(outs, axis=1) * g


def _ssd_fwd(xbc, dt, z, a_neg, dskip_x, g_x, tril, triu):
    s = xbc.shape[0]
    tm = min(SSD_ROWS, s)
    cpb = tm // CHUNK
    nc = s // CHUNK

    def body(xbc_ref, dt_ref, z_ref, a_ref, dsk_ref, g_ref, tril_ref, triu_ref, y_ref, o_ref, hin_ref, h_sc):
        @pl.when(pl.program_id(0) == 0)
        def _():
            h_sc[...] = jnp.zeros(h_sc.shape, F32)

        tril, triu = tril_ref[...], triu_ref[...]
        ltri = tril > 0
        lane_lo = _lane_iota((CHUNK, LANE)) < SSM_P

        def chunk(c, carry):
            r0 = pl.multiple_of(c * CHUNK, CHUNK)
            dtc = dt_ref[pl.ds(r0, CHUNK), :]
            lam_c, lam_r, lam_last = _chunk_common(dtc, a_ref[...], tril, triu)
            e_c = jnp.exp(lam_c)
            f_r = jnp.exp(lam_r[:, CHUNK - 1:CHUNK] - lam_r)
            cd = jnp.exp(lam_last)
            for grp in range(SSM_GROUPS):
                bo = SSM_WIDTH + grp * SSM_N
                co = SSM_WIDTH + SSM_GROUPS * SSM_N + grp * SSM_N
                bm = xbc_ref[pl.ds(r0, CHUNK), bo:bo + SSM_N]
                cm = xbc_ref[pl.ds(r0, CHUNK), co:co + SSM_N]
                cm_b = cm.astype(BF16)
                gmat = _nt(cm_b, bm.astype(BF16))
                bt = bm.T
                for pj in range(SSM_HEADS // SSM_GROUPS // 2):
                    ha = grp * (SSM_HEADS // SSM_GROUPS) + 2 * pj
                    hb_ = ha + 1
                    lo = ha * SSM_P
                    xs = xbc_ref[pl.ds(r0, CHUNK), lo:lo + LANE]
                    x2 = xs * _pair_sel(lane_lo, dtc[:, ha:ha + 1], dtc[:, hb_:hb_ + 1])
                    x2b = x2.astype(BF16)
                    ys, sts = [], []
                    for hh in (ha, hb_):
                        seg = lam_c[:, hh:hh + 1] - lam_r[hh:hh + 1, :]
                        dec = jnp.exp(jnp.where(ltri, seg, -jnp.inf))
                        ys.append(_nn((gmat * dec).astype(BF16), x2b))
                        sts.append(_nn((bt * f_r[hh:hh + 1, :]).astype(BF16), x2b))
                    hp = h_sc[:, lo:lo + LANE]
                    hin_ref[c, :, lo:lo + LANE] = hp
                    zz = _nn(cm_b, hp.astype(BF16))
                    e2 = _pair_sel(lane_lo, e_c[:, ha:ha + 1], e_c[:, hb_:hb_ + 1])
                    yv = jnp.where(lane_lo, ys[0], ys[1]) + e2 * zz
                    y_ref[pl.ds(r0, CHUNK), lo:lo + LANE] = yv + xs * dsk_ref[:, lo:lo + LANE]
                    cd2 = _pair_sel(lane_lo, cd[:, ha:ha + 1], cd[:, hb_:hb_ + 1])
                    h_sc[:, lo:lo + LANE] = hp * cd2 + jnp.where(lane_lo, sts[0], sts[1])
            return carry

        lax.fori_loop(0, cpb, chunk, 0)
        o_ref[...] = _gated_norm_fwd(y_ref[...], z_ref[...], g_ref[...])

    return pl.pallas_call(
        body, name="ssd_fwd", grid=(s // tm,),
        in_specs=[_rows(tm, CONV_CH), _rows(tm, LANE), _rows(tm, SSM_WIDTH), _whole((1, LANE)),
                  _whole((1, SSM_WIDTH)), _whole((1, SSM_WIDTH)), _whole((CHUNK, CHUNK)), _whole((CHUNK, CHUNK))],
        out_specs=[_rows(tm, SSM_WIDTH), _rows(tm, SSM_WIDTH),
                   pl.BlockSpec((cpb, SSM_N, SSM_WIDTH), lambda i: (i, 0, 0))],
        out_shape=[jax.ShapeDtypeStruct((s, SSM_WIDTH), F32), jax.ShapeDtypeStruct((s, SSM_WIDTH), F32),
                   jax.ShapeDtypeStruct((nc, SSM_N, SSM_WIDTH), F32)],
        scratch_shapes=[pltpu.VMEM((SSM_N, SSM_WIDTH), F32)],
        compiler_params=_cparams("arbitrary"),
    )(xbc, dt, z, a_neg, dskip_x, g_x, tril, triu)


def _outln(o, z_attn, o_ssm, w_out, x, gate, ln_g, ln_b, tgt):
    s = x.shape[0]
    tm = min(ROW_TILE_WIDE, s)

    def body(o_ref, z_ref, os_ref, w_ref, x_ref, gate_ref, g_ref, b_ref, t_ref,
             cat_ref, dmix_ref, gx_ref, do_ref, dz_ref, dl_ref, dos_ref, loss_ref, dg_ref, db_ref, dgate_ref):
        ov, zv = o_ref[...], z_ref[...]
        sz = _silu(zv)
        cat_ref[:, :MLA_WIDTH] = (ov * sz).astype(BF16)
        cat_ref[:, MLA_WIDTH:] = os_ref[...].astype(BF16)
        w = w_ref[...]
        mixed = _nn(cat_ref[...], w)
        gate_v = gate_ref[...]
        hv = DEEPNORM_ALPHA * x_ref[...] + gate_v * mixed
        mu = jnp.mean(hv, axis=-1, keepdims=True)
        hc = hv - mu
        rstd = lax.rsqrt(jnp.mean(hc * hc, axis=-1, keepdims=True) + LN_EPS)
        xhat = hc * rstd
        g = g_ref[...]
        err = xhat * g + b_ref[...] - t_ref[...]
        _acc_rows(loss_ref, jnp.full((1, LANE), (0.5 / D_MODEL) * jnp.sum(err * err), F32))
        dy = err * (1.0 / D_MODEL)
        _acc_rows(dg_ref, _colsum(dy * xhat))
        _acc_rows(db_ref, _colsum(dy))
        dxhat = dy * g
        dh = rstd * (dxhat - jnp.mean(dxhat, axis=-1, keepdims=True)
                     - xhat * jnp.mean(dxhat * xhat, axis=-1, keepdims=True))
        gx_ref[...] = DEEPNORM_ALPHA * dh
        _acc_rows(dgate_ref, _colsum(dh * mixed))
        dmix = (gate_v * dh).astype(BF16)
        dmix_ref[...] = dmix
        dcat = _nt(dmix, w)
        da = dcat[:, :MLA_WIDTH]
        dos_ref[...] = dcat[:, MLA_WIDTH:]
        dov = da * sz
        do_ref[...] = dov.astype(BF16)
        dz_ref[...] = da * ov * _silu_grad(zv)
        prod = dov * ov
        for h in range(MLA_HEADS):
            dsum = jnp.sum(prod[:, h * V_DIM:(h + 1) * V_DIM], axis=1, keepdims=True)
            dl_ref[h] = jnp.broadcast_to(dsum, (tm, LANE)).T[0:1, :]

    vec = _whole((1, D_MODEL))
    return pl.pallas_call(
        body, name="outln", grid=(s // tm,),
        in_specs=[_rows(tm, MLA_WIDTH), _rows(tm, MLA_WIDTH), _rows(tm, SSM_WIDTH), _whole((MIX_WIDTH, D_MODEL)),
                  _rows(tm, D_MODEL), vec, vec, vec, _rows(tm, D_MODEL)],
        out_specs=[_rows(tm, MIX_WIDTH), _rows(tm, D_MODEL), _rows(tm, D_MODEL), _rows(tm, MLA_WIDTH),
                   _rows(tm, MLA_WIDTH), pl.BlockSpec((MLA_HEADS, 1, tm), lambda i: (0, 0, i)), _rows(tm, SSM_WIDTH),
                   _whole((1, LANE)), vec, vec, vec],
        out_shape=[jax.ShapeDtypeStruct((s, MIX_WIDTH), BF16), jax.ShapeDtypeStruct((s, D_MODEL), BF16),
                   jax.ShapeDtypeStruct((s, D_MODEL), F32), jax.ShapeDtypeStruct((s, MLA_WIDTH), BF16),
                   jax.ShapeDtypeStruct((s, MLA_WIDTH), F32), jax.ShapeDtypeStruct((MLA_HEADS, 1, s), F32),
                   jax.ShapeDtypeStruct((s, SSM_WIDTH), F32), jax.ShapeDtypeStruct((1, LANE), F32),
                   jax.ShapeDtypeStruct((1, D_MODEL), F32), jax.ShapeDtypeStruct((1, D_MODEL), F32),
                   jax.ShapeDtypeStruct((1, D_MODEL), F32)],
        compiler_params=_cparams("arbitrary"),
    )(o, z_attn, o_ssm, w_out, x, gate, ln_g, ln_b, tgt)


def _ssd_bwd(dos, y, z, xbc, dt, hin, a_neg, dskip_x, g_x, tril, triu, expand):
    s = xbc.shape[0]
    tm = min(SSD_ROWS, s)
    cpb = tm // CHUNK
    nb = s // tm
    gw = SSM_WIDTH // SSM_GROUPS
    hpg = SSM_HEADS // SSM_GROUPS

    def body(dos_ref, y_ref, z_ref, xbc_ref, dt_ref, hin_ref, a_ref, dsk_ref, g_ref, tril_ref, triu_ref, exp_ref,
             dxbc_ref, ddt_ref, dz_ref, dg_ref, ddsk_ref, da_ref, dh_sc, dy_sc):
        @pl.when(pl.program_id(0) == 0)
        def _():
            dh_sc[...] = jnp.zeros(dh_sc.shape, F32)

        yv, zv, dov = y_ref[...], z_ref[...], dos_ref[...]
        sz = _silu(zv)
        hf = yv * sz
        gv = g_ref[...]
        dgs, dhfs = [], []
        for grp in range(SSM_GROUPS):
            sl = slice(grp * gw, (grp + 1) * gw)
            n, rstd = _rms(hf[:, sl])
            dgs.append(_colsum(dov[:, sl] * n))
            dhfs.append(_rms_bwd(dov[:, sl] * gv[:, sl], n, rstd))
        dhf = jnp.concatenate(dhfs, axis=1)
        _acc_rows(dg_ref, jnp.concatenate(dgs, axis=1))
        dy_sc[...] = dhf * sz
        dz_ref[...] = dhf * yv * _silu_grad(zv)

        tril, triu, expand = tril_ref[...], triu_ref[...], exp_ref[...]
        ltri = tril > 0
        utri = triu > 0
        lane = _lane_iota((CHUNK, LANE))
        lane1 = _lane_iota((1, LANE))
        lane_lo = lane < SSM_P
        row_last = lax.broadcasted_iota(jnp.int32, (CHUNK, LANE), 0) == CHUNK - 1
        a_neg_v = a_ref[...]

        def chunk(ci, carry):
            dsk_acc, da_acc = carry
            cl = cpb - 1 - ci
            r0 = pl.multiple_of(cl * CHUNK, CHUNK)
            rows = pl.ds(r0, CHUNK)
            dtc = dt_ref[rows, :]
            lam_c, lam_r, lam_last = _chunk_common(dtc, a_neg_v, tril, triu)
            e_c = jnp.exp(lam_c)
            f_c = jnp.exp(lam_last - lam_c)
            cd = jnp.exp(lam_last)
            dt_x, e_x, f_x = _tri_right(dtc, expand), _tri_right(e_c, expand), _tri_right(f_c, expand)
            cd_x = _tri_right(jnp.broadcast_to(cd, (HALO, LANE)), expand)[0:1, :]
            dlam = jnp.zeros((CHUNK, LANE), F32)
            dlast = jnp.zeros((1, LANE), F32)
            ddt_x = jnp.zeros((CHUNK, LANE), F32)
            dsk_parts = []
            for grp in range(SSM_GROUPS):
                bo = SSM_WIDTH + grp * SSM_N
                co = SSM_WIDTH + SSM_GROUPS * SSM_N + grp * SSM_N
                bm = xbc_ref[rows, bo:bo + SSM_N]
                cm = xbc_ref[rows, co:co + SSM_N]
                bm_b, cm_b = bm.astype(BF16), cm.astype(BF16)
                gmat = _nt(cm_b, bm_b)
                gmat_t = _nt(bm_b, cm_b)
                ct_b = cm.T.astype(BF16)
                acc_dg = jnp.zeros((CHUNK, CHUNK), F32)
                acc_dgt = jnp.zeros((CHUNK, CHUNK), F32)
                d_b = jnp.zeros((CHUNK, SSM_N), F32)
                d_c = jnp.zeros((CHUNK, SSM_N), F32)
                for pj in range(hpg // 2):
                    ha = grp * hpg + 2 * pj
                    hb_ = ha + 1
                    lo = ha * SSM_P
                    blk = slice(lo, lo + LANE)
                    xs = xbc_ref[rows, blk]
                    dt2, e2, f2, cd2 = dt_x[:, blk], e_x[:, blk], f_x[:, blk], cd_x[:, blk]
                    x2 = xs * dt2
                    x2b = x2.astype(BF16)
                    dy2 = dy_sc[rows, blk]
                    dy2b = dy2.astype(BF16)
                    hp = hin_ref[cl, :, blk]
                    hp_b = hp.astype(BF16)
                    dhn = dh_sc[:, blk]
                    dhn_b = dhn.astype(BF16)
                    yo = e2 * _nn(cm_b, hp_b)
                    dzz_b = (e2 * dy2).astype(BF16)
                    d_c = d_c + _nt(dzz_b, hp_b)
                    dh_sc[:, blk] = _nn(ct_b, dzz_b) + cd2 * dhn
                    dxs2 = f2 * _nn(bm_b, dhn_b)
                    d_b = d_b + _nt((f2 * x2).astype(BF16), dhn_b)
                    xd = x2 * dxs2
                    t_lam = dy2 * yo - xd
                    t_last = cd2 * (dhn * hp) + xd
                    dxd2 = jnp.zeros((CHUNK, LANE), F32)
                    heads = ((ha, lane_lo), (hb_, jnp.logical_not(lane_lo)))
                    for hh, msk in heads:
                        x2h_b = jnp.where(msk, x2, 0.0).astype(BF16)
                        dy2h_b = jnp.where(msk, dy2, 0.0).astype(BF16)
                        seg = lam_c[:, hh:hh + 1] - lam_r[hh:hh + 1, :]
                        dec = jnp.exp(jnp.where(ltri, seg, -jnp.inf))
                        dect = jnp.exp(jnp.where(utri, -seg, -jnp.inf))
                        dmd = _nt(dy2h_b, x2b) * dec
                        dmtd = _nt(x2h_b, dy2b) * dect
                        acc_dg = acc_dg + dmd
                        acc_dgt = acc_dgt + dmtd
                        dlam_h = jnp.sum(dmd * gmat - dmtd * gmat_t + jnp.where(msk, t_lam, 0.0), axis=1, keepdims=True)
                        last_h = jnp.sum(jnp.sum(jnp.where(msk, t_last, 0.0), axis=0, keepdims=True), axis=1, keepdims=True)
                        dlam = jnp.where(lane == hh, dlam_h, dlam)
                        dlast = jnp.where(lane1 == hh, last_h, dlast)
                        dxd2 = jnp.where(msk, _nn((gmat_t * dect).astype(BF16), dy2b), dxd2)
                    dx2 = dxd2 + dxs2
                    dxbc_ref[rows, blk] = dx2 * dt2 + dy2 * dsk_ref[:, blk]
                    prod = dx2 * xs
                    for hh, msk in heads:
                        col = jnp.sum(jnp.where(msk, prod, 0.0), axis=1, keepdims=True)
                        ddt_x = jnp.where(lane == hh, col, ddt_x)
                    dsk_parts.append(_colsum(dy2 * xs))
                d_c = d_c + _nn(acc_dg.astype(BF16), bm_b)
                d_b = d_b + _nn(acc_dgt.astype(BF16), cm_b)
                dxbc_ref[rows, bo:bo + SSM_N] = d_b
                dxbc_ref[rows, co:co + SSM_N] = d_c
            dlam = dlam + jnp.where(row_last, dlast, 0.0)
            da = _tri_left(triu, dlam)
            ddt_ref[rows, :] = da * a_neg_v + ddt_x
            return dsk_acc + jnp.concatenate(dsk_parts, axis=1), da_acc + _colsum(da * dtc)

        dsk_tot, da_tot = lax.fori_loop(
            0, cpb, chunk, (jnp.zeros((1, SSM_WIDTH), F32), jnp.zeros((1, LANE), F32)))
        _acc_rows(ddsk_ref, dsk_tot)
        _acc_rows(da_ref, da_tot)

    rev = lambda i: (nb - 1 - i, 0)
    rrows = lambda w: pl.BlockSpec((tm, w), rev)
    return pl.pallas_call(
        body, name="ssd_bwd", grid=(nb,),
        in_specs=[rrows(SSM_WIDTH), rrows(SSM_WIDTH), rrows(SSM_WIDTH), rrows(CONV_CH), rrows(LANE),
                  pl.BlockSpec((cpb, SSM_N, SSM_WIDTH), lambda i: (nb - 1 - i, 0, 0)),
                  _whole((1, LANE)), _whole((1, SSM_WIDTH)), _whole((1, SSM_WIDTH)),
                  _whole((CHUNK, CHUNK)), _whole((CHUNK, CHUNK)), _whole((LANE, SSM_WIDTH))],
        out_specs=[rrows(CONV_CH), rrows(LANE), rrows(SSM_WIDTH),
                   _whole((1, SSM_WIDTH)), _whole((1, SSM_WIDTH)), _whole((1, LANE))],
        out_shape=[jax.ShapeDtypeStruct((s, CONV_CH), F32), jax.ShapeDtypeStruct((s, LANE), F32),
                   jax.ShapeDtypeStruct((s, SSM_WIDTH), F32), jax.ShapeDtypeStruct((1, SSM_WIDTH), F32),
                   jax.ShapeDtypeStruct((1, SSM_WIDTH), F32), jax.ShapeDtypeStruct((1, LANE), F32)],
        scratch_shapes=[pltpu.VMEM((SSM_N, SSM_WIDTH), F32), pltpu.VMEM((tm, SSM_WIDTH), F32)],
        compiler_params=_cparams("arbitrary"),
    )(dos, y, z, xbc, dt, hin, a_neg, dskip_x, g_x, tril, triu, expand)


def _ssd_post_bwd(xbc_raw, dxa, ddt, dt_raw, conv_w, conv_b, dt_bias_p):
    s = xbc_raw.shape[0]
    tm = ROW_TILE
    hb = tm // HALO
    nt = s // tm
    ext = tm + HALO

    def body(x_ref, prev_ref, next_ref, d_ref, dnext_ref, ddt_ref, dtr_ref, w_ref, b_ref, db_ref,
             dx_ref, ddtr_ref, dw_ref, dcb_ref, ddb_ref, xe_sc, de_sc):
        i = pl.program_id(0)
        w = w_ref[...]
        xe_sc[pl.ds(0, HALO), :] = jnp.where(i > 0, prev_ref[...], 0.0)
        xe_sc[pl.ds(HALO, tm), :] = x_ref[...]
        xe_sc[pl.ds(HALO + tm, HALO), :] = next_ref[...]
        pre = _conv_taps(xe_sc, w, ext, 0) + b_ref[...]
        sg = _silu_grad(pre)
        de_sc[pl.ds(0, tm), :] = d_ref[...] * sg[:tm]
        de_sc[pl.ds(tm, HALO), :] = jnp.where(i < nt - 1, dnext_ref[...] * sg[tm:], 0.0)
        dconv = de_sc[pl.ds(0, tm), :]
        acc = None
        dws = []
        for k in range(CONV_K):
            term = de_sc[pl.ds(CONV_K - 1 - k, tm), :] * w[k:k + 1, :]
            acc = term if acc is None else acc + term
            dws.append(_colsum(dconv * xe_sc[pl.ds(HALO - (CONV_K - 1) + k, tm), :]))
        dx_ref[...] = acc
        _acc_rows(dw_ref, jnp.concatenate(dws, axis=0))
        _acc_rows(dcb_ref, _colsum(dconv))
        ddtr = ddt_ref[...] * _sigmoid(dtr_ref[...] + db_ref[...])
        ddtr_ref[...] = ddtr
        _acc_rows(ddb_ref, _colsum(ddtr))

    halo_prev = pl.BlockSpec((HALO, CONV_CH), lambda i: (jnp.maximum(i * hb - 1, 0), 0))
    halo_next = pl.BlockSpec((HALO, CONV_CH), lambda i: (jnp.minimum((i + 1) * hb, s // HALO - 1), 0))
    return pl.pallas_call(
        body, name="ssd_post_bwd", grid=(nt,),
        in_specs=[_rows(tm, CONV_CH), halo_prev, halo_next, _rows(tm, CONV_CH), halo_next, _rows(tm, LANE),
                  _rows(tm, LANE), _whole((CONV_K, CONV_CH)), _whole((1, CONV_CH)), _whole((1, LANE))],
        out_specs=[_rows(tm, CONV_CH), _rows(tm, LANE), _whole((CONV_K, CONV_CH)), _whole((1, CONV_CH)),
                   _whole((1, LANE))],
        out_shape=[jax.ShapeDtypeStruct((s, CONV_CH), F32), jax.ShapeDtypeStruct((s, LANE), F32),
                   jax.ShapeDtypeStruct((CONV_K, CONV_CH), F32), jax.ShapeDtypeStruct((1, CONV_CH), F32),
                   jax.ShapeDtypeStruct((1, LANE), F32)],
        scratch_shapes=[pltpu.VMEM((tm + 2 * HALO, CONV_CH), F32), pltpu.VMEM((ext, CONV_CH), F32)],
        compiler_params=_cparams("arbitrary"),
    )(xbc_raw, xbc_raw, xbc_raw, dxa, dxa, ddt, dt_raw, conv_w, conv_b, dt_bias_p)


def _qbwd(dq_att, q_lat, g_q, w_qb_p, cos, sin):
    s = q_lat.shape[0]
    tm = ROW_TILE
    wq = MLA_HEADS * HEAD_PAD

    def body(dq_ref, ql_ref, g_ref, w_ref, cos_ref, sin_ref, dql_ref, draw_ref, dg_ref):
        c, sn = cos_ref[...], sin_ref[...]
        for h in range(MLA_HEADS):
            o = h * HEAD_PAD
            dqh = dq_ref[h].astype(F32) * ATTN_SCALE
            draw_ref[:, o:o + QK_NOPE] = dqh[:, :QK_NOPE].astype(BF16)
            draw_ref[:, o + QK_NOPE:o + HEAD_PAD] = _rope_transposed(dqh[:, QK_NOPE:], c, sn).astype(BF16)
        dn = _nt(draw_ref[...], w_ref[...])
        xhat, rstd = _rms(ql_ref[...])
        _acc_rows(dg_ref, _colsum(dn * xhat))
        dql_ref[...] = _rms_bwd(dn * g_ref[...], xhat, rstd)

    return pl.pallas_call(
        body, name="qbwd", grid=(s // tm,),
        in_specs=[pl.BlockSpec((MLA_HEADS, tm, HEAD_PAD), lambda i: (0, i, 0)), _rows(tm, Q_RANK), _whole((1, Q_RANK)),
                  _whole((Q_RANK, wq)), _rows(tm, LANE), _rows(tm, LANE)],
        out_specs=[_rows(tm, Q_RANK), _rows(tm, wq), _whole((1, Q_RANK))],
        out_shape=[jax.ShapeDtypeStruct((s, Q_RANK), F32), jax.ShapeDtypeStruct((s, wq), BF16),
                   jax.ShapeDtypeStruct((1, Q_RANK), F32)],
        compiler_params=_cparams("arbitrary"),
    )(dq_att, q_lat, g_q, w_qb_p, cos, sin)


def _kvbwd(dk_att, dv, kv_lat, g_kv, w_kvb_p, cos, sin):
    s = kv_lat.shape[0]
    tm = ROW_TILE
    wk = MLA_HEADS * HEAD_PAD
    wr = MLA_HEADS * (QK_NOPE + V_DIM)

    def body(dk_ref, dv_ref, kl_ref, g_ref, w_ref, cos_ref, sin_ref, dkl_ref, draw_ref, dg_ref):
        dkr = None
        for h in range(MLA_HEADS):
            o = h * HEAD_PAD
            draw_ref[:, h * QK_NOPE:(h + 1) * QK_NOPE] = dk_ref[:, o:o + QK_NOPE].astype(BF16)
            part = dk_ref[:, o + QK_NOPE:o + HEAD_PAD].astype(F32)
            dkr = part if dkr is None else dkr + part
        draw_ref[:, MLA_HEADS * QK_NOPE:] = dv_ref[...].astype(BF16)
        dn = _nt(draw_ref[...], w_ref[...])
        xhat, rstd = _rms(kl_ref[:, :KV_RANK])
        _acc_rows(dg_ref, _colsum(dn * xhat))
        dkl_ref[:, :KV_RANK] = _rms_bwd(dn * g_ref[...], xhat, rstd)
        dkl_ref[:, KV_RANK:] = _rope_transposed(dkr, cos_ref[...], sin_ref[...])

    return pl.pallas_call(
        body, name="kvbwd", grid=(s // tm,),
        in_specs=[_rows(tm, wk), _rows(tm, MLA_WIDTH), _rows(tm, KV_LAT_PAD), _whole((1, KV_RANK)),
                  _whole((KV_RANK, wr)), _rows(tm, LANE), _rows(tm, LANE)],
        out_specs=[_rows(tm, KV_LAT_PAD), _rows(tm, wr), _whole((1, KV_RANK))],
        out_shape=[jax.ShapeDtypeStruct((s, KV_LAT_PAD), F32), jax.ShapeDtypeStruct((s, wr), BF16),
                   jax.ShapeDtypeStruct((1, KV_RANK), F32)],
        compiler_params=_cparams("arbitrary"),
    )(dk_att, dv, kv_lat, g_kv, w_kvb_p, cos, sin)


def _inproj_bwd(pieces, w_in_pt, x, scale1p, gx1):
    s = x.shape[0]
    tm = min(ROW_TILE, s)

    def body(*refs):
        p_refs = refs[:len(IN_PAD)]
        w_ref, x_ref, sc_ref, gx1_ref, gx_ref, dp_ref, dsc_ref, dsh_ref = refs[len(IN_PAD):]
        off = 0
        for ref, w in zip(p_refs, IN_PAD):
            dp_ref[:, off:off + w] = ref[...].astype(BF16)
            off += w
        du = _nn(dp_ref[...], w_ref[...])
        gx_ref[...] = gx1_ref[...] + du * sc_ref[...]
        _acc_rows(dsc_ref, _colsum(du * x_ref[...]))
        _acc_rows(dsh_ref, _colsum(du))

    vec = _whole((1, D_MODEL))
    return pl.pallas_call(
        body, name="inproj_bwd", grid=(s // tm,),
        in_specs=[_rows(tm, w) for w in IN_PAD] + [_whole_once((IN_PAD_WIDTH, D_MODEL)), _rows(tm, D_MODEL), vec,
                                                    _rows(tm, D_MODEL)],
        out_specs=[_rows(tm, D_MODEL), _rows(tm, IN_PAD_WIDTH), vec, vec],
        out_shape=[jax.ShapeDtypeStruct((s, D_MODEL), F32), jax.ShapeDtypeStruct((s, IN_PAD_WIDTH), BF16),
                   jax.ShapeDtypeStruct((1, D_MODEL), F32), jax.ShapeDtypeStruct((1, D_MODEL), F32)],
        compiler_params=_cparams("arbitrary"),
    )(*pieces, w_in_pt, x, scale1p, gx1)


def _matmul_tn_rows(name, a, b, tk):
    s, k = a.shape
    n = b.shape[1]
    tm = min(GRAD_ROWS, s)

    def body(a_ref, b_ref, o_ref):
        @pl.when(pl.program_id(1) == 0)
        def _():
            o_ref[...] = jnp.zeros_like(o_ref)
        o_ref[...] += _tn(a_ref[...], b_ref[...])

    return pl.pallas_call(
        body, name=name, grid=(k // tk, s // tm),
        in_specs=[pl.BlockSpec((tm, tk), lambda j, i: (i, j)), pl.BlockSpec((tm, n), lambda j, i: (i, 0))],
        out_specs=pl.BlockSpec((tk, n), lambda j, i: (j, 0)),
        out_shape=jax.ShapeDtypeStruct((k, n), F32),
        compiler_params=_cparams("parallel", "arbitrary"),
    )(a, b)


def _matmul_tn(name, a, b, tn):
    s, k = a.shape
    n = b.shape[1]
    tm = min(GRAD_ROWS, s)

    def body(a_ref, b_ref, o_ref):
        @pl.when(pl.program_id(1) == 0)
        def _():
            o_ref[...] = jnp.zeros_like(o_ref)
        o_ref[...] += _tn(a_ref[...], b_ref[...])

    return pl.pallas_call(
        body, name=name, grid=(n // tn, s // tm),
        in_specs=[pl.BlockSpec((tm, k), lambda j, i: (i, 0)), pl.BlockSpec((tm, tn), lambda j, i: (i, j))],
        out_specs=pl.BlockSpec((k, tn), lambda j, i: (0, j)),
        out_shape=jax.ShapeDtypeStruct((k, n), F32),
        compiler_params=_cparams("parallel", "arbitrary"),
    )(a, b)


def _pack_w_in_t(w_in_t):
    parts, off = [], 0
    for w, wp in zip(IN_SPLITS, IN_PAD):
        parts.append(jnp.pad(w_in_t[off:off + w], ((0, wp - w), (0, 0))))
        off += w
    return jnp.concatenate(parts, axis=0)


def _unpack_w_in_t(g):
    parts, off = [], 0
    for w, wp in zip(IN_SPLITS, IN_PAD):
        parts.append(g[off:off + w])
        off += wp
    return jnp.concatenate(parts, axis=0)


def _rope_tables(positions):
    inv_freq = 1.0 / (ROPE_THETA ** (jnp.arange(ROPE_HALF, dtype=F32) / ROPE_HALF))
    ang = positions.astype(F32)[:, None] * inv_freq
    cos, sin = jnp.cos(ang), jnp.sin(ang)
    zeros = jnp.zeros((positions.shape[0], LANE - QK_ROPE), F32)
    return jnp.concatenate([cos, cos, zeros], axis=1), jnp.concatenate([-sin, sin, zeros], axis=1)


def _local_step(x, tgt, positions, mod, w_in_t, q_norm_g, w_qb_p, kv_norm_g, w_kvb_p, conv_w, conv_b, dt_bias,
                a_log, d_skip, ssm_norm_g, w_out_b, ln_g, ln_b):
    row = lambda v: v.reshape(1, -1)
    shift, scale, gate = mod[:D_MODEL], mod[D_MODEL:2 * D_MODEL], mod[2 * D_MODEL:]
    scale1p = row(1.0 + scale)
    w_in_p = _pack_w_in_t(w_in_t)
    cos, sin = _rope_tables(positions)
    a_neg = row(jnp.pad(-jnp.exp(a_log), (0, LANE - SSM_HEADS)))
    dskip_x = row(jnp.repeat(d_skip, SSM_P))
    dt_bias_p = row(jnp.pad(dt_bias, (0, LANE - SSM_HEADS)))
    tri = jnp.tril(jnp.ones((CHUNK, CHUNK), F32))
    tril, triu = tri.astype(BF16), tri.T.astype(BF16)

    u_bf, q_lat, kv_lat, z_attn, xbc_raw, dt_raw, z_ssm = _inproj(x, scale1p, row(shift), w_in_p)
    nq_bf, q_att = _qpath(q_lat, row(q_norm_g), w_qb_p, cos, sin)
    nkv_bf, k_att, v_att, vt_att = _kvpath(kv_lat, row(kv_norm_g), w_kvb_p, cos, sin)
    o, lse_rows = _attn_fwd(q_att, k_att, vt_att)
    xbc, dt = _ssd_pre(xbc_raw, dt_raw, conv_w, row(conv_b), dt_bias_p)
    expand = jnp.repeat(jnp.eye(LANE, SSM_HEADS, dtype=BF16), SSM_P, axis=1)
    y, o_ssm, hin = _ssd_fwd(xbc, dt, z_ssm, a_neg, dskip_x, row(ssm_norm_g), tril, triu)
    (cat_bf, dmix_bf, gx1, do_bf, dz_attn, delta_rows, dos, loss, d_ln_g, d_ln_b, d_gate) = _outln(
        o, z_attn, o_ssm, w_out_b, x, row(gate), row(ln_g), row(ln_b), tgt)

    g_w_out = _matmul_tn("gw_out", cat_bf, dmix_bf, 512)
    dk_att, dv, dq_att = _attn_bwd(q_att, k_att, v_att, do_bf, lse_rows, delta_rows)
    dq_lat, dqraw_bf, d_q_norm_g = _qbwd(dq_att, q_lat, row(q_norm_g), w_qb_p, cos, sin)
    dkv_lat, dkvraw_bf, d_kv_norm_g = _kvbwd(dk_att, dv, kv_lat, row(kv_norm_g), w_kvb_p, cos, sin)
    g_w_qb = _matmul_tn("gw_qb", nq_bf, dqraw_bf, MLA_HEADS * HEAD_PAD)
    g_w_kvb = _matmul_tn("gw_kvb", nkv_bf, dkvraw_bf, MLA_HEADS * (QK_NOPE + V_DIM))
    dxa, ddt, dz_ssm, d_ssm_g, ddsk_x, d_a = _ssd_bwd(dos, y, z_ssm, xbc, dt, hin, a_neg, dskip_x, row(ssm_norm_g),
                                                       tril, triu, expand)
    dxbc_raw, ddt_raw, d_conv_w, d_conv_b, d_dt_bias = _ssd_post_bwd(xbc_raw, dxa, ddt, dt_raw, conv_w, row(conv_b),
                                                                     dt_bias_p)
    grad_x, dproj_bf, d_scale, d_shift = _inproj_bwd((dq_lat, dkv_lat, dz_attn, dxbc_raw, ddt_raw, dz_ssm),
                                                     w_in_p, x, scale1p, gx1)
    g_w_in_t = _unpack_w_in_t(_matmul_tn_rows("gw_in", dproj_bf, u_bf, 896))
    return dict(
        loss=loss[0, 0], grad_x=grad_x,
        dmod=jnp.concatenate([d_shift[0], d_scale[0], d_gate[0]]),
        w_in_t=g_w_in_t, q_norm_g=d_q_norm_g[0], w_qb=g_w_qb, kv_norm_g=d_kv_norm_g[0], w_kvb=g_w_kvb,
        conv_w=d_conv_w, conv_b=d_conv_b[0], dt_bias=d_dt_bias[0, :SSM_HEADS],
        a_log=d_a[0, :SSM_HEADS] * a_neg[0, :SSM_HEADS],
        d_skip=ddsk_x.reshape(SSM_HEADS, SSM_P).sum(axis=1), ssm_norm_g=d_ssm_g[0], w_out=g_w_out,
        ln_g=d_ln_g[0], ln_b=d_ln_b[0])


ADAM_ROWS = 512


def _my_index():
    return 4 * lax.axis_index("x") + 2 * lax.axis_index("y") + lax.axis_index("c")


def _exchange(name, sends, gather):
    n = len(sends)
    peers = N_DEV - 1

    def body(*refs):
        send_refs, recv_refs = refs[:n], refs[n:2 * n]
        send_sems, recv_sems, local_sems = refs[2 * n:]
        x, y, c = lax.axis_index("x"), lax.axis_index("y"), lax.axis_index("c")
        me = 4 * x + 2 * y + c

        def src(a, idx):
            return send_refs[a] if gather else send_refs[a].at[idx]

        owns = [pltpu.make_async_copy(src(a, me), recv_refs[a].at[me], local_sems.at[a]) for a in range(n)]
        for cp in owns:
            cp.start()
        copies = []
        for k in range(1, N_DEV):
            px, py, pc = x ^ ((k >> 2) & 1), y ^ ((k >> 1) & 1), c ^ (k & 1)
            peer = 4 * px + 2 * py + pc
            for a in range(n):
                copies.append(pltpu.make_async_remote_copy(
                    src_ref=src(a, peer), dst_ref=recv_refs[a].at[me],
                    send_sem=send_sems.at[a * peers + k - 1], recv_sem=recv_sems.at[a * peers + k - 1],
                    device_id=(px, py, pc), device_id_type=pl.DeviceIdType.MESH))
        for cp in copies:
            cp.start()
        for cp in copies:
            cp.wait()
        for cp in owns:
            cp.wait()

    block_shape = lambda a: a.shape if gather else a.shape[1:]
    return pl.pallas_call(
        body, name=name,
        in_specs=[pl.BlockSpec(memory_space=pl.ANY)] * n, out_specs=[pl.BlockSpec(memory_space=pl.ANY)] * n,
        out_shape=[jax.ShapeDtypeStruct((N_DEV, *block_shape(a)), a.dtype) for a in sends],
        scratch_shapes=[pltpu.SemaphoreType.DMA((n * peers,)), pltpu.SemaphoreType.DMA((n * peers,)),
                        pltpu.SemaphoreType.DMA((n,))],
    )(*sends)


def _gather_two_level(name, sends):
    n = len(sends)
    per = N_DEV - 1

    def body(*refs):
        send_refs, recv_refs = refs[:n], refs[n:2 * n]
        send_sems, recv_sems, local_sems = refs[2 * n:]
        x, y, c = lax.axis_index("x"), lax.axis_index("y"), lax.axis_index("c")
        sibling = (x, y, 1 - c)
        chips = [(1 - x, y), (x, 1 - y), (1 - x, 1 - y)]

        def idx(px, py, pc):
            return 4 * px + 2 * py + pc

        def copy(a, k, block, to, src=None):
            slot = recv_refs[a].at[idx(*block)]
            return pltpu.make_async_remote_copy(
                src_ref=slot if src is None else src, dst_ref=slot,
                send_sem=send_sems.at[a * per + k], recv_sem=recv_sems.at[a * per + k],
                device_id=to, device_id_type=pl.DeviceIdType.MESH)

        me = (x, y, c)
        owns = [pltpu.make_async_copy(send_refs[a], recv_refs[a].at[idx(*me)], local_sems.at[a]) for a in range(n)]
        for cp in owns:
            cp.start()
        first = [copy(a, 0, me, sibling, src=send_refs[a]) for a in range(n)]
        first += [copy(a, 1 + j, me, (*chip, c), src=send_refs[a]) for j, chip in enumerate(chips) for a in range(n)]
        for cp in first:
            cp.start()
        passed = []
        for j, chip in enumerate(chips):
            for a in range(n):
                copy(a, 1 + j, (*chip, c), me).wait_recv()
                fwd = copy(a, 4 + j, (*chip, c), sibling)
                fwd.start()
                passed.append(fwd)
        for a in range(n):
            copy(a, 0, sibling, me).wait_recv()
            for j, chip in enumerate(chips):
                copy(a, 4 + j, (*chip, 1 - c), me).wait_recv()
        for cp in first + passed:
            cp.wait_send()
        for cp in owns:
            cp.wait()

    return pl.pallas_call(
        body, name=name,
        in_specs=[pl.BlockSpec(memory_space=pl.ANY)] * n, out_specs=[pl.BlockSpec(memory_space=pl.ANY)] * n,
        out_shape=[jax.ShapeDtypeStruct((N_DEV, *a.shape), a.dtype) for a in sends],
        scratch_shapes=[pltpu.SemaphoreType.DMA((n * per,)), pltpu.SemaphoreType.DMA((n * per,)),
                        pltpu.SemaphoreType.DMA((n,))],
    )(*sends)


def _flat_rows(parts, row_multiple):
    flat = jnp.concatenate([p.reshape(-1) for p in parts])
    chunk = row_multiple * LANE
    total = -(-flat.shape[0] // chunk) * chunk
    return jnp.pad(flat, (0, total - flat.shape[0])).reshape(-1, LANE)


def _unflat(flat, shapes):
    flat = flat.reshape(-1)
    out, off = [], 0
    for shp in shapes:
        n = math.prod(shp)
        out.append(flat[off:off + n].reshape(shp))
        off += n
    return out


def _adam_update(g, w, m, v):
    m2 = ADAM_B1 * m + (1.0 - ADAM_B1) * g
    v2 = ADAM_B2 * v + (1.0 - ADAM_B2) * (g * g)
    m_hat = m2 / (1.0 - ADAM_B1 ** ADAM_STEP)
    v_hat = v2 / (1.0 - ADAM_B2 ** ADAM_STEP)
    delta = -ADAM_LR * (m_hat / (jnp.sqrt(v_hat) + ADAM_EPS) + ADAM_WD * w)
    return delta, m2, v2


def _adamw_summed(name, parts, w, m, v):
    r = w.shape[0]
    tr = min(ADAM_ROWS, r)

    def body(p_ref, w_ref, m_ref, v_ref, g_ref, d_ref, m2_ref, v2_ref):
        g = p_ref[0]
        for j in range(1, N_DEV):
            g = g + p_ref[j]
        g_ref[...] = g
        d_ref[...], m2_ref[...], v2_ref[...] = _adam_update(g, w_ref[...], m_ref[...], v_ref[...])

    rows = _rows(tr, LANE)
    return pl.pallas_call(
        body, name=name, grid=(r // tr,),
        in_specs=[pl.BlockSpec((N_DEV, tr, LANE), lambda i: (0, i, 0)), rows, rows, rows],
        out_specs=[rows] * 4, out_shape=[jax.ShapeDtypeStruct((r, LANE), F32)] * 4,
        compiler_params=_cparams("parallel"),
    )(parts, w, m, v)


def _modpart(c_all, w_ada, b_cols):
    def body(c_ref, w_ref, b_ref, o_ref):
        o_ref[...] = _nn(c_ref[...].astype(BF16), w_ref[...].astype(BF16)) + b_ref[...]

    return pl.pallas_call(
        body, name="modpart", out_shape=jax.ShapeDtypeStruct((N_DEV, w_ada.shape[1]), F32),
    )(c_all, w_ada, b_cols)


def _adamw_w_ada(c_all_t, dmod_cols, w, m, v):
    def body(c_ref, d_ref, w_ref, m_ref, v_ref, g_ref, dl_ref, m2_ref, v2_ref):
        g = c_ref[:, 0:1] * d_ref[0:1, :]
        for b in range(1, N_DEV):
            g = g + c_ref[:, b:b + 1] * d_ref[b:b + 1, :]
        g_ref[...] = g
        dl_ref[...], m2_ref[...], v2_ref[...] = _adam_update(g, w_ref[...], m_ref[...], v_ref[...])

    return pl.pallas_call(
        body, name="adamw_w_ada", out_shape=[jax.ShapeDtypeStruct(w.shape, F32)] * 4,
        compiler_params=pltpu.CompilerParams(vmem_limit_bytes=VMEM_LIMIT),
    )(c_all_t, dmod_cols, w, m, v)


W_IN_SHARD = IN_WIDTH // N_DEV
W_IN_SHARD_LANES = -(-W_IN_SHARD // LANE) * LANE
BF16_ROWS = 16
W_IN_SEND_ROWS = -(-W_IN_SHARD // BF16_ROWS) * BF16_ROWS


def _transpose_cast(w_pad):
    def body(w_ref, o_ref):
        o_ref[...] = w_ref[...].T.astype(BF16)

    return pl.pallas_call(
        body, name="w_in_transpose", out_shape=jax.ShapeDtypeStruct(w_pad.shape[::-1], BF16),
        compiler_params=pltpu.CompilerParams(vmem_limit_bytes=VMEM_LIMIT),
    )(w_pad)


def _adamw_w_in(parts, w, m, v):
    rows_t = parts.shape[1]
    d, cols = w.shape
    tb = ROW_TILE

    def body(p_ref, w_ref, m_ref, v_ref, g_ref, d_ref, m2_ref, v2_ref):
        gt = p_ref[0].astype(F32)
        for j in range(1, N_DEV):
            gt = gt + p_ref[j].astype(F32)
        gt = jnp.concatenate([gt, jnp.zeros((W_IN_SHARD_LANES - rows_t, tb), F32)], axis=0)
        g = gt.T[:, :cols]
        g_ref[...] = g
        d_ref[...], m2_ref[...], v2_ref[...] = _adam_update(g, w_ref[...], m_ref[...], v_ref[...])

    blk = _rows(tb, cols)
    return pl.pallas_call(
        body, name="adamw_w_in", grid=(d // tb,),
        in_specs=[pl.BlockSpec((N_DEV, rows_t, tb), lambda i: (0, 0, i)), blk, blk, blk],
        out_specs=[blk] * 4, out_shape=[jax.ShapeDtypeStruct(w.shape, F32)] * 4,
        compiler_params=_cparams("parallel"),
    )(parts, w, m, v)


SHARDED = ("w_qb", "w_kvb", "w_out")
REPLICATED = ("b_ada", "q_norm_g", "kv_norm_g", "conv_b", "dt_bias", "a_log", "d_skip", "ssm_norm_g", "ln_g", "ln_b")
WEIGHTS = ("w_ada", "b_ada", "w_in", "q_norm_g", "w_qb", "kv_norm_g", "w_kvb", "conv_w", "conv_b", "dt_bias",
           "a_log", "d_skip", "ssm_norm_g", "w_out", "ln_g", "ln_b")
HEAD_COLS = QK_NOPE + V_DIM


def _adamw_blocks(name, parts, w, m, v):
    r, c = w.shape
    tr = ROW_TILE if r % ROW_TILE == 0 else r

    def body(p_ref, w_ref, m_ref, v_ref, g_ref, d_ref, m2_ref, v2_ref):
        g = p_ref[0].astype(F32)
        for j in range(1, N_DEV):
            g = g + p_ref[j].astype(F32)
        g_ref[...] = g
        d_ref[...], m2_ref[...], v2_ref[...] = _adam_update(g, w_ref[...], m_ref[...], v_ref[...])

    blk = _rows(tr, c)
    return pl.pallas_call(
        body, name=name, grid=(r // tr,),
        in_specs=[pl.BlockSpec((N_DEV, tr, c), lambda i: (0, i, 0)), blk, blk, blk],
        out_specs=[blk] * 4, out_shape=[jax.ShapeDtypeStruct(w.shape, F32)] * 4,
        compiler_params=_cparams("parallel"),
    )(parts, w, m, v)


def kernel(x, c, positions, w_ada, b_ada, w_in, q_norm_g, w_qb, kv_norm_g, w_kvb, conv_w, conv_b, dt_bias, a_log, d_skip, ssm_norm_g, w_out, ln_g, ln_b, loss_target, m_w_ada, m_b_ada, m_w_in, m_q_norm_g, m_w_qb, m_kv_norm_g, m_w_kvb, m_conv_w, m_conv_b, m_dt_bias, m_a_log, m_d_skip, m_ssm_norm_g, m_w_out, m_ln_g, m_ln_b, v_w_ada, v_b_ada, v_w_in, v_q_norm_g, v_w_qb, v_kv_norm_g, v_w_kvb, v_conv_w, v_conv_b, v_dt_bias, v_a_log, v_d_skip, v_ssm_norm_g, v_w_out, v_ln_g, v_ln_b):
    given = dict(w_ada=w_ada, b_ada=b_ada, w_in=w_in, q_norm_g=q_norm_g, w_qb=w_qb, kv_norm_g=kv_norm_g, w_kvb=w_kvb,
                 conv_w=conv_w, conv_b=conv_b, dt_bias=dt_bias, a_log=a_log, d_skip=d_skip, ssm_norm_g=ssm_norm_g,
                 w_out=w_out, ln_g=ln_g, ln_b=ln_b)
    mom = dict(w_ada=m_w_ada, b_ada=m_b_ada, w_in=m_w_in, q_norm_g=m_q_norm_g, w_qb=m_w_qb, kv_norm_g=m_kv_norm_g,
               w_kvb=m_w_kvb, conv_w=m_conv_w, conv_b=m_conv_b, dt_bias=m_dt_bias, a_log=m_a_log, d_skip=m_d_skip,
               ssm_norm_g=m_ssm_norm_g, w_out=m_w_out, ln_g=m_ln_g, ln_b=m_ln_b)
    var = dict(w_ada=v_w_ada, b_ada=v_b_ada, w_in=v_w_in, q_norm_g=v_q_norm_g, w_qb=v_w_qb, kv_norm_g=v_kv_norm_g,
               w_kvb=v_w_kvb, conv_w=v_conv_w, conv_b=v_conv_b, dt_bias=v_dt_bias, a_log=v_a_log, d_skip=v_d_skip,
               ssm_norm_g=v_ssm_norm_g, w_out=v_w_out, ln_g=v_ln_g, ln_b=v_ln_b)
    w0 = {k: a[0] for k, a in given.items()}
    m0 = {k: a[0] for k, a in mom.items()}
    v0 = {k: a[0] for k, a in var.items()}
    me = _my_index()

    w_in_rows = _transpose_cast(jnp.pad(w0["w_in"], ((0, 0), (0, W_IN_SHARD_LANES - W_IN_SHARD))))
    g_w_in, g_w_qb, g_w_kvb, g_w_out, g_conv_w, c_all = _gather_two_level(
        "gather_weights", [w_in_rows] + [w0[k].astype(BF16) for k in SHARDED] + [w0["conv_w"], c])
    c_all = c_all.reshape(N_DEV, D_MODEL)
    w_in_t = g_w_in[:, :W_IN_SHARD, :].reshape(IN_WIDTH, D_MODEL)
    w_qb_p = jnp.pad(g_w_qb, ((0, 0), (0, 0), (0, HEAD_PAD - QK_HEAD))).transpose(1, 0, 2).reshape(Q_RANK, -1)
    w_kvb_p = g_w_kvb.reshape(N_DEV, KV_RANK, 2, QK_NOPE).transpose(1, 2, 0, 3).reshape(KV_RANK, -1)
    w_out_b = g_w_out.reshape(MIX_WIDTH, D_MODEL)
    conv_w_full = g_conv_w.transpose(1, 0, 2).reshape(CONV_K, CONV_CH)

    ada_cols = w0["w_ada"].shape[1]
    b_cols = lax.dynamic_slice(w0["b_ada"], (me * ada_cols,), (ada_cols,)).reshape(1, ada_cols)
    mod_all, = _exchange("gather_mod", [_modpart(c_all, w0["w_ada"], b_cols)], gather=True)
    mod = lax.dynamic_index_in_dim(mod_all, me, axis=1, keepdims=False).reshape(-1)

    loc = _local_step(x[0], loss_target[0], positions[0], mod, w_in_t, w0["q_norm_g"], w_qb_p,
                      w0["kv_norm_g"], w_kvb_p, conv_w_full, w0["conv_b"], w0["dt_bias"], w0["a_log"],
                      w0["d_skip"], w0["ssm_norm_g"], w_out_b, w0["ln_g"], w0["ln_b"])

    rep_shapes = [w0[k].shape for k in REPLICATED] + [(1,)]
    rep_local = [loc["dmod"]] + [loc[k] for k in REPLICATED[1:]] + [loc["loss"].reshape(1)]
    rep_parts, conv_parts = _exchange("gather_small", [_flat_rows(rep_local, HALO), loc["conv_w"]], gather=True)
    conv_cols = w0["conv_w"].shape[1]
    conv_mine = lax.dynamic_slice(conv_parts, (0, 0, me * conv_cols), (N_DEV, CONV_K, conv_cols))
    outs = {"conv_w": _adamw_blocks("adamw_conv_w", conv_mine, w0["conv_w"], m0["conv_w"], v0["conv_w"])}
    zero1 = jnp.zeros((1,), F32)
    rep = _adamw_summed("adamw_replicated", rep_parts,
                        _flat_rows([w0[k] for k in REPLICATED] + [zero1], HALO),
                        _flat_rows([m0[k] for k in REPLICATED] + [zero1], HALO),
                        _flat_rows([v0[k] for k in REPLICATED] + [zero1], HALO))
    rep_g, rep_d, rep_m, rep_v = [_unflat(a, rep_shapes) for a in rep]
    loss = rep_g[-1][0]

    dmod_all = rep_parts.reshape(N_DEV, -1)[:, :3 * D_MODEL]
    dmod_cols = lax.dynamic_slice(dmod_all, (0, me * ada_cols), (N_DEV, ada_cols))
    outs["w_ada"] = _adamw_w_ada(c_all.T, dmod_cols, w0["w_ada"], m0["w_ada"], v0["w_ada"])

    send_w_in = loc["w_in_t"].astype(BF16).reshape(N_DEV, W_IN_SHARD, D_MODEL)
    send_w_in = jnp.pad(send_w_in, ((0, 0), (0, W_IN_SEND_ROWS - W_IN_SHARD), (0, 0)))
    send_w_qb = loc["w_qb"].astype(BF16).reshape(Q_RANK, N_DEV, HEAD_PAD)[:, :, :QK_HEAD].transpose(1, 0, 2)
    send_w_kvb = loc["w_kvb"].astype(BF16).reshape(KV_RANK, 2, N_DEV, QK_NOPE).transpose(2, 0, 1, 3)
    send_w_kvb = send_w_kvb.reshape(N_DEV, KV_RANK, HEAD_COLS)
    send_w_out = loc["w_out"].astype(BF16).reshape(N_DEV, MIX_WIDTH // N_DEV, D_MODEL)
    r_w_in, r_w_qb, r_w_kvb, r_w_out = _exchange(
        "scatter_grads", [send_w_in, send_w_qb, send_w_kvb, send_w_out], gather=False)
    outs["w_in"] = _adamw_w_in(r_w_in, w0["w_in"], m0["w_in"], v0["w_in"])
    for k, parts in zip(SHARDED, (r_w_qb, r_w_kvb, r_w_out)):
        outs[k] = _adamw_blocks("adamw_" + k, parts, w0[k], m0[k], v0[k])

    def collect(idx):
        out = {k: o[idx] for k, o in outs.items()}
        out.update({k: (rep_g, rep_d, rep_m, rep_v)[idx][i] for i, k in enumerate(REPLICATED)})
        return [out[k][None] for k in WEIGHTS]

    return (loss, loc["grad_x"][None], *collect(0), *collect(1), *collect(2), *collect(3))
```

```python
import math

import jax
import jax.numpy as jnp
from jax import lax
from jax.experimental import pallas as pl
from jax.experimental.pallas import tpu as pltpu

F32 = jnp.float32
BF16 = jnp.bfloat16

N_DEV = 8
D_MODEL = 1024
MLA_HEADS = 8
QK_NOPE = 128
QK_ROPE = 64
V_DIM = 128
Q_RANK = 384
KV_RANK = 256
QK_HEAD = QK_NOPE + QK_ROPE
HEAD_PAD = 256
ROPE_HALF = QK_ROPE // 2
ROPE_THETA = 10000.0
MLA_WIDTH = MLA_HEADS * V_DIM
SSM_HEADS = 16
SSM_P = 64
SSM_WIDTH = SSM_HEADS * SSM_P
SSM_GROUPS = 2
SSM_N = 128
CONV_K = 4
CHUNK = 128
CONV_CH = SSM_WIDTH + 2 * SSM_GROUPS * SSM_N
MIX_WIDTH = MLA_WIDTH + SSM_WIDTH
IN_SPLITS = (Q_RANK, KV_RANK + QK_ROPE, MLA_WIDTH, CONV_CH, SSM_HEADS, SSM_WIDTH)
IN_WIDTH = sum(IN_SPLITS)
LANE = 128
KV_LAT_PAD = KV_RANK + LANE
IN_PAD = (Q_RANK, KV_LAT_PAD, MLA_WIDTH, CONV_CH, LANE, SSM_WIDTH)
IN_PAD_WIDTH = sum(IN_PAD)
DEEPNORM_ALPHA = 2.0 ** 0.25
RMS_EPS = 1e-6
LN_EPS = 1e-5
ATTN_SCALE = QK_HEAD ** -0.5
LOG2E = math.log2(math.e)
LN2 = math.log(2.0)
Q_PRESCALE = ATTN_SCALE * LOG2E
ADAM_LR, ADAM_B1, ADAM_B2, ADAM_EPS, ADAM_WD, ADAM_STEP = 0.001, 0.9, 0.999, 1e-08, 0.01, 10

ROW_TILE = 512
ROW_TILE_WIDE = 256
ATTN_TILE = 512
ATTN_UNROLLS = (8, 4, 2)
SSD_ROWS = 512
GRAD_ROWS = 2048
VMEM_LIMIT = 56 * 1024 * 1024


def _nn(a, b):
    return jnp.dot(a, b, preferred_element_type=F32)


def _nt(a, b):
    return lax.dot_general(a, b, (((1,), (1,)), ((), ())), preferred_element_type=F32)


def _tn(a, b):
    return lax.dot_general(a, b, (((0,), (0,)), ((), ())), preferred_element_type=F32)


def _cparams(*sem):
    return pltpu.CompilerParams(dimension_semantics=sem, vmem_limit_bytes=VMEM_LIMIT)


def _rows(tm, w):
    return pl.BlockSpec((tm, w), lambda i: (i, 0))


def _whole(shape):
    return pl.BlockSpec(shape, lambda i: (0,) * len(shape))


def _whole_once(shape):
    return pl.BlockSpec(shape, lambda i: (0,) * len(shape), pipeline_mode=pl.Buffered(1))


def _sigmoid(z):
    return 1.0 / (1.0 + jnp.exp(-z))


def _lane_iota(shape):
    return lax.broadcasted_iota(jnp.int32, shape, len(shape) - 1)


def _swap_halves(r):
    lane = _lane_iota(r.shape)
    return jnp.where(lane < ROPE_HALF, pltpu.roll(r, LANE - ROPE_HALF, 1),
                     jnp.where(lane < QK_ROPE, pltpu.roll(r, ROPE_HALF, 1), 0.0))


def _rope(r, cos, sin):
    return r * cos + _swap_halves(r) * sin


def _rope_transposed(d, cos, sin):
    return d * cos + _swap_halves(d * sin)


def _rms(x):
    rstd = lax.rsqrt(jnp.mean(x * x, axis=-1, keepdims=True) + RMS_EPS)
    return x * rstd, rstd


def _rms_bwd(dxhat, xhat, rstd):
    return rstd * (dxhat - xhat * jnp.mean(dxhat * xhat, axis=-1, keepdims=True))


def _acc_rows(ref, val):
    @pl.when(pl.program_id(0) == 0)
    def _():
        ref[...] = jnp.zeros_like(ref)
    ref[...] += val


def _colsum(v):
    return jnp.sum(v, axis=0, keepdims=True)


def _inproj(x, scale1p, shift, w_in_pt):
    s = x.shape[0]
    tm = ROW_TILE

    def body(x_ref, sc_ref, sh_ref, w_ref, u_ref, *outs):
        u = (x_ref[...] * sc_ref[...] + sh_ref[...]).astype(BF16)
        u_ref[...] = u
        proj = _nt(u, w_ref[...])
        off = 0
        for ref, w in zip(outs, IN_PAD):
            ref[...] = proj[:, off:off + w]
            off += w

    return pl.pallas_call(
        body, name="inproj", grid=(s // tm,),
        in_specs=[_rows(tm, D_MODEL), _whole((1, D_MODEL)), _whole((1, D_MODEL)), _whole((IN_PAD_WIDTH, D_MODEL))],
        out_specs=[_rows(tm, D_MODEL)] + [_rows(tm, w) for w in IN_PAD],
        out_shape=[jax.ShapeDtypeStruct((s, D_MODEL), BF16)] + [jax.ShapeDtypeStruct((s, w), F32) for w in IN_PAD],
        compiler_params=_cparams("parallel"),
    )(x, scale1p, shift, w_in_pt)


def _qpath(q_lat, g_q, w_qb_p, cos, sin):
    s = q_lat.shape[0]
    tm = ROW_TILE

    def body(ql_ref, g_ref, w_ref, cos_ref, sin_ref, nq_ref, q_ref):
        xhat, _ = _rms(ql_ref[...])
        nq = (xhat * g_ref[...]).astype(BF16)
        nq_ref[...] = nq
        raw = _nn(nq, w_ref[...]) * Q_PRESCALE
        c, sn = cos_ref[...], sin_ref[...]
        for h in range(MLA_HEADS):
            o = h * HEAD_PAD
            q_ref[:, o:o + QK_NOPE] = raw[:, o:o + QK_NOPE].astype(BF16)
            q_ref[:, o + QK_NOPE:o + HEAD_PAD] = _rope(raw[:, o + QK_NOPE:o + HEAD_PAD], c, sn).astype(BF16)

    return pl.pallas_call(
        body, name="qpath", grid=(s // tm,),
        in_specs=[_rows(tm, Q_RANK), _whole((1, Q_RANK)), _whole((Q_RANK, MLA_HEADS * HEAD_PAD)),
                  _rows(tm, LANE), _rows(tm, LANE)],
        out_specs=[_rows(tm, Q_RANK), _rows(tm, MLA_HEADS * HEAD_PAD)],
        out_shape=[jax.ShapeDtypeStruct((s, Q_RANK), BF16), jax.ShapeDtypeStruct((s, MLA_HEADS * HEAD_PAD), BF16)],
        compiler_params=_cparams("parallel"),
    )(q_lat, g_q, w_qb_p, cos, sin)


def _kvpath(kv_lat, g_kv, w_kvb_p, cos, sin):
    s = kv_lat.shape[0]
    tm = ROW_TILE

    def body(kl_ref, g_ref, w_ref, cos_ref, sin_ref, nkv_ref, k_ref, v_ref, vt_ref):
        kl = kl_ref[...]
        xhat, _ = _rms(kl[:, :KV_RANK])
        nkv = (xhat * g_ref[...]).astype(BF16)
        nkv_ref[...] = nkv
        raw = _nn(nkv, w_ref[...])
        kr = _rope(kl[:, KV_RANK:], cos_ref[...], sin_ref[...]).astype(BF16)
        for h in range(MLA_HEADS):
            o = h * HEAD_PAD
            k_ref[:, o:o + QK_NOPE] = raw[:, h * QK_NOPE:(h + 1) * QK_NOPE].astype(BF16)
            k_ref[:, o + QK_NOPE:o + HEAD_PAD] = kr
        vals = raw[:, MLA_HEADS * QK_NOPE:]
        v_ref[...] = vals.astype(BF16)
        vt_ref[...] = vals.T.astype(BF16)

    return pl.pallas_call(
        body, name="kvpath", grid=(s // tm,),
        in_specs=[_rows(tm, KV_LAT_PAD), _whole((1, KV_RANK)), _whole((KV_RANK, MLA_HEADS * (QK_NOPE + V_DIM))),
                  _rows(tm, LANE), _rows(tm, LANE)],
        out_specs=[_rows(tm, KV_RANK), _rows(tm, MLA_HEADS * HEAD_PAD), _rows(tm, MLA_WIDTH),
                   pl.BlockSpec((MLA_WIDTH, tm), lambda i: (0, i))],
        out_shape=[jax.ShapeDtypeStruct((s, KV_RANK), BF16), jax.ShapeDtypeStruct((s, MLA_HEADS * HEAD_PAD), BF16),
                   jax.ShapeDtypeStruct((s, MLA_WIDTH), BF16), jax.ShapeDtypeStruct((MLA_WIDTH, s), BF16)],
        compiler_params=_cparams("parallel"),
    )(kv_lat, g_kv, w_kvb_p, cos, sin)


def _causal_mask(t):
    row = lax.broadcasted_iota(jnp.int32, (t, t), 0)
    col = lax.broadcasted_iota(jnp.int32, (t, t), 1)
    return row, col


def _attn_fwd(q, k, vt):
    s = q.shape[0]
    t = min(ATTN_TILE, s)
    nq = s // t

    def body(q_ref, k_ref, vt_ref, o_ref, lse_ref, m_sc, l_sc, acc_sc, sa_sc, sb_sc):
        i = pl.program_id(1)
        qv = q_ref[...]
        m_sc[...] = jnp.full(m_sc.shape, -jnp.inf, F32)
        l_sc[...] = jnp.zeros(l_sc.shape, F32)
        acc_sc[...] = jnp.zeros(acc_sc.shape, F32)

        def scores(j, s_ref):
            s_ref[...] = _nt(k_ref[pl.ds(pl.multiple_of(j * t, t), t), :], qv)

        def update(s_ref, j, masked):
            vt = vt_ref[:, pl.ds(pl.multiple_of(j * t, t), t)]
            sc = s_ref[...]
            if masked:
                row, col = _causal_mask(t)
                sc = jnp.where(row <= col, sc, -jnp.inf)
            m_prev = m_sc[...]
            m_new = jnp.maximum(m_prev, jnp.max(sc, axis=0, keepdims=True))
            alpha = jnp.exp2(m_prev - m_new)
            p = jnp.exp2(sc - m_new)
            l_sc[...] = alpha * l_sc[...] + jnp.sum(p, axis=0, keepdims=True)
            acc_sc[...] = alpha * acc_sc[...] + _nn(vt, p.astype(BF16))
            m_sc[...] = m_new

        def run(j0, count):
            bufs = (sa_sc, sb_sc)
            for u in range(count):
                scores(j0 + u + 1, bufs[(u + 1) % 2])
                update(bufs[u % 2], j0 + u, False)

        scores(0, sa_sc)
        done = 0
        for group in ATTN_UNROLLS:
            def body_(g, carry, base=done, group=group):
                run(base + group * g, group)
                return carry

            n_groups = lax.div(i - done, group)
            lax.fori_loop(0, n_groups, body_, 0)
            done = done + group * n_groups
        odd = lax.rem(i, 2)

        @pl.when(odd == 1)
        def _():
            scores(i, sb_sc)
            update(sa_sc, i - 1, False)
            update(sb_sc, i, True)

        @pl.when(odd == 0)
        def _():
            update(sa_sc, i, True)

        l = l_sc[...]
        o_ref[...] = (acc_sc[...] / l).T
        lse_ref[0] = m_sc[...] + jnp.log2(l)

    return pl.pallas_call(
        body, name="attn_fwd", grid=(MLA_HEADS, nq),
        in_specs=[pl.BlockSpec((t, HEAD_PAD), lambda h, i: (i, h)),
                  pl.BlockSpec((s, HEAD_PAD), lambda h, i: (0, h)),
                  pl.BlockSpec((V_DIM, s), lambda h, i: (h, 0))],
        out_specs=[pl.BlockSpec((t, V_DIM), lambda h, i: (i, h)), pl.BlockSpec((1, 1, t), lambda h, i: (h, 0, i))],
        out_shape=[jax.ShapeDtypeStruct((s, MLA_WIDTH), F32), jax.ShapeDtypeStruct((MLA_HEADS, 1, s), F32)],
        scratch_shapes=[pltpu.VMEM((1, t), F32), pltpu.VMEM((1, t), F32), pltpu.VMEM((V_DIM, t), F32),
                        pltpu.VMEM((t, t), F32), pltpu.VMEM((t, t), F32)],
        compiler_params=_cparams("parallel", "arbitrary"),
    )(q, k, vt)


def _attn_bwd(q, k, v, dot, lse_row, delta_row):
    s = q.shape[0]
    t = min(ATTN_TILE, s)
    nq = s // t

    def body(q_ref, k_ref, v_ref, do_ref, lse_ref, dl_ref, dk_ref, dv_ref, dq_hbm,
             dq_sc, dk_sc, dv_sc, sa_sc, sb_sc, pa_sc, pb_sc, sem, stage_sc):
        h = pl.program_id(0)
        j = pl.program_id(1)
        kv_ = k_ref[...]
        vv = v_ref[...]
        kt = kv_.astype(F32).T.astype(BF16)

        @pl.when(j == 0)
        def _():
            dq_sc[...] = jnp.zeros(dq_sc.shape, F32)

        dk_sc[...] = jnp.zeros(dk_sc.shape, F32)
        dv_sc[...] = jnp.zeros(dv_sc.shape, F32)

        def scores(i, s_ref, p_ref):
            off = pl.multiple_of(i * t, t)
            s_ref[...] = _nt(kv_, q_ref[pl.ds(off, t), :])
            p_ref[...] = _nn(vv, do_ref[:, pl.ds(off, t)])

        def update(i, s_ref, p_ref, masked):
            off = pl.multiple_of(i * t, t)
            qv = q_ref[pl.ds(off, t), :]
            sct = s_ref[...]
            if masked:
                row, col = _causal_mask(t)
                sct = jnp.where(row <= col, sct, -jnp.inf)
            pt = jnp.exp2(sct - lse_ref[0, :, pl.ds(off, t)])
            gt = (pt * (p_ref[...] - dl_ref[0, :, pl.ds(off, t)])).astype(BF16)
            dv_sc[...] += _nt(do_ref[:, pl.ds(off, t)], pt.astype(BF16))
            dk_sc[...] += _nn(gt, qv)
            dq_sc[:, pl.ds(off, t)] += _nn(kt, gt)

        rest = nq - 1 - j
        scores(j, sa_sc, pa_sc)

        @pl.when(rest >= 1)
        def _():
            scores(j + 1, sb_sc, pb_sc)

        update(j, sa_sc, pa_sc, True)

        def run(i0, count):
            bufs = ((sb_sc, pb_sc), (sa_sc, pa_sc))
            for u in range(count):
                scores(i0 + u + 1, *bufs[(u + 1) % 2])
                update(i0 + u, *bufs[u % 2], False)

        i1, left = j + 1, rest
        for group in ATTN_UNROLLS:
            def body_(g, carry, base=i1, group=group):
                run(base + group * g, group)
                return carry

            n_groups = jnp.where(left >= 1, lax.div(left - 1, group), 0)
            lax.fori_loop(0, n_groups, body_, 0)
            i1 = i1 + group * n_groups
            left = left - group * n_groups

        @pl.when(left == 1)
        def _():
            update(i1, sb_sc, pb_sc, False)

        @pl.when(left == 2)
        def _():
            scores(i1 + 1, sa_sc, pa_sc)
            update(i1, sb_sc, pb_sc, False)
            update(i1 + 1, sa_sc, pa_sc, False)

        dk_ref[...] = (dk_sc[...] * LN2).astype(BF16)
        dv_ref[...] = dv_sc[...].T.astype(BF16)

        def out_copy(jj):
            rows = pl.ds(pl.multiple_of(jj * t, t), t)
            return pltpu.make_async_copy(stage_sc, dq_hbm.at[h, rows, :], sem)

        @pl.when(j > 0)
        def _():
            out_copy(j - 1).wait()

        stage_sc[...] = dq_sc[:, pl.ds(pl.multiple_of(j * t, t), t)].T.astype(BF16)
        out_copy(j).start()

        @pl.when(j == nq - 1)
        def _():
            out_copy(j).wait()

    return pl.pallas_call(
        body, name="attn_bwd", grid=(MLA_HEADS, nq),
        in_specs=[pl.BlockSpec((s, HEAD_PAD), lambda h, j: (0, h)),
                  pl.BlockSpec((t, HEAD_PAD), lambda h, j: (j, h)),
                  pl.BlockSpec((t, V_DIM), lambda h, j: (j, h)),
                  pl.BlockSpec((V_DIM, s), lambda h, j: (h, 0)),
                  pl.BlockSpec((1, 1, s), lambda h, j: (h, 0, 0)),
                  pl.BlockSpec((1, 1, s), lambda h, j: (h, 0, 0))],
        out_specs=[pl.BlockSpec((t, HEAD_PAD), lambda h, j: (j, h)), pl.BlockSpec((t, V_DIM), lambda h, j: (j, h)),
                   pl.BlockSpec(memory_space=pl.ANY)],
        out_shape=[jax.ShapeDtypeStruct((s, MLA_HEADS * HEAD_PAD), BF16), jax.ShapeDtypeStruct((s, MLA_WIDTH), BF16),
                   jax.ShapeDtypeStruct((MLA_HEADS, s, HEAD_PAD), BF16)],
        scratch_shapes=[pltpu.VMEM((HEAD_PAD, s), F32), pltpu.VMEM((t, HEAD_PAD), F32), pltpu.VMEM((V_DIM, t), F32),
                        pltpu.VMEM((t, t), F32), pltpu.VMEM((t, t), F32), pltpu.VMEM((t, t), F32),
                        pltpu.VMEM((t, t), F32), pltpu.SemaphoreType.DMA, pltpu.VMEM((t, HEAD_PAD), BF16)],
        compiler_params=_cparams("arbitrary", "arbitrary"),
    )(q, k, v, dot, lse_row, delta_row)


HALO = 8


def _silu(z):
    return z * _sigmoid(z)


def _silu_grad(z):
    sg = _sigmoid(z)
    return sg * (1.0 + z * (1.0 - sg))


def _softplus(x):
    e = jnp.exp(-jnp.abs(x))
    small = e * (1.0 - e * (0.5 - e * (1.0 / 3.0)))
    return jnp.maximum(x, 0.0) + jnp.where(e < 1e-3, small, jnp.log(1.0 + e))


def _conv_taps(xe_ref, w, tm, first):
    acc = None
    for k in range(CONV_K):
        term = xe_ref[pl.ds(HALO + first - (CONV_K - 1) + k, tm), :] * w[k:k + 1, :]
        acc = term if acc is None else acc + term
    return acc


def _ssd_pre(xbc_raw, dt_raw, conv_w, conv_b, dt_bias_p):
    s = xbc_raw.shape[0]
    tm = ROW_TILE
    hb = tm // HALO

    def body(x_ref, prev_ref, dtr_ref, w_ref, b_ref, db_ref, act_ref, dt_ref, xe_sc):
        i = pl.program_id(0)
        xe_sc[pl.ds(0, HALO), :] = jnp.where(i > 0, prev_ref[...], 0.0)
        xe_sc[pl.ds(HALO, tm), :] = x_ref[...]
        pre = _conv_taps(xe_sc, w_ref[...], tm, 0) + b_ref[...]
        act_ref[...] = _silu(pre)
        dt_ref[...] = _softplus(dtr_ref[...] + db_ref[...])

    return pl.pallas_call(
        body, name="ssd_pre", grid=(s // tm,),
        in_specs=[_rows(tm, CONV_CH), pl.BlockSpec((HALO, CONV_CH), lambda i: (jnp.maximum(i * hb - 1, 0), 0)),
                  _rows(tm, LANE), _whole((CONV_K, CONV_CH)), _whole((1, CONV_CH)), _whole((1, LANE))],
        out_specs=[_rows(tm, CONV_CH), _rows(tm, LANE)],
        out_shape=[jax.ShapeDtypeStruct((s, CONV_CH), F32), jax.ShapeDtypeStruct((s, LANE), F32)],
        scratch_shapes=[pltpu.VMEM((tm + HALO, CONV_CH), F32)],
        compiler_params=_cparams("parallel"),
    )(xbc_raw, xbc_raw, dt_raw, conv_w, conv_b, dt_bias_p)


def _split3(a):
    a1 = a.astype(BF16)
    r1 = a - a1.astype(F32)
    a2 = r1.astype(BF16)
    a3 = (r1 - a2.astype(F32)).astype(BF16)
    return a1, a2, a3


def _tri_left(tri, a):
    a1, a2, a3 = _split3(a)
    return _nn(tri, a1) + _nn(tri, a2) + _nn(tri, a3)


def _tri_right(a, tri):
    a1, a2, a3 = _split3(a)
    return _nn(a1, tri) + _nn(a2, tri) + _nn(a3, tri)


def _pair_sel(lane_lo, col_a, col_b):
    return jnp.where(lane_lo, col_a, col_b)


def _chunk_common(dt, a_neg, tril, triu):
    a = dt * a_neg
    lam_c = _tri_left(tril, a)
    lam_r = _tri_right(a.T, triu)
    lam_last = lam_c[CHUNK - 1:CHUNK, :]
    return lam_c, lam_r, lam_last


def _gated_norm_fwd(y, z, g):
    hf = y * _silu(z)
    outs = []
    for grp in range(SSM_GROUPS):
        w = SSM_WIDTH // SSM_GROUPS
        n, _ = _rms(hf[:, grp * w:(grp + 1) * w])
        outs.append(n)
    return jnp.concatenate(outs, axis=1) * g


def _ssd_fwd(xbc, dt, z, a_neg, dskip_x, g_x, tril, triu):
    s = xbc.shape[0]
    tm = min(SSD_ROWS, s)
    cpb = tm // CHUNK
    nc = s // CHUNK

    def body(xbc_ref, dt_ref, z_ref, a_ref, dsk_ref, g_ref, tril_ref, triu_ref, y_ref, o_ref, hin_ref, h_sc):
        @pl.when(pl.program_id(0) == 0)
        def _():
            h_sc[...] = jnp.zeros(h_sc.shape, F32)

        tril, triu = tril_ref[...], triu_ref[...]
        ltri = tril > 0
        lane_lo = _lane_iota((CHUNK, LANE)) < SSM_P

        def chunk(c, carry):
            r0 = pl.multiple_of(c * CHUNK, CHUNK)
            dtc = dt_ref[pl.ds(r0, CHUNK), :]
            lam_c, lam_r, lam_last = _chunk_common(dtc, a_ref[...], tril, triu)
            e_c = jnp.exp(lam_c)
            f_r = jnp.exp(lam_r[:, CHUNK - 1:CHUNK] - lam_r)
            cd = jnp.exp(lam_last)
            for grp in range(SSM_GROUPS):
                bo = SSM_WIDTH + grp * SSM_N
                co = SSM_WIDTH + SSM_GROUPS * SSM_N + grp * SSM_N
                bm = xbc_ref[pl.ds(r0, CHUNK), bo:bo + SSM_N]
                cm = xbc_ref[pl.ds(r0, CHUNK), co:co + SSM_N]
                cm_b = cm.astype(BF16)
                gmat = _nt(cm_b, bm.astype(BF16))
                bt = bm.T
                for pj in range(SSM_HEADS // SSM_GROUPS // 2):
                    ha = grp * (SSM_HEADS // SSM_GROUPS) + 2 * pj
                    hb_ = ha + 1
                    lo = ha * SSM_P
                    xs = xbc_ref[pl.ds(r0, CHUNK), lo:lo + LANE]
                    x2 = xs * _pair_sel(lane_lo, dtc[:, ha:ha + 1], dtc[:, hb_:hb_ + 1])
                    x2b = x2.astype(BF16)
                    ys, sts = [], []
                    for hh in (ha, hb_):
                        seg = lam_c[:, hh:hh + 1] - lam_r[hh:hh + 1, :]
                        dec = jnp.exp(jnp.where(ltri, seg, -jnp.inf))
                        ys.append(_nn((gmat * dec).astype(BF16), x2b))
                        sts.append(_nn((bt * f_r[hh:hh + 1, :]).astype(BF16), x2b))
                    hp = h_sc[:, lo:lo + LANE]
                    hin_ref[c, :, lo:lo + LANE] = hp
                    zz = _nn(cm_b, hp.astype(BF16))
                    e2 = _pair_sel(lane_lo, e_c[:, ha:ha + 1], e_c[:, hb_:hb_ + 1])
                    yv = jnp.where(lane_lo, ys[0], ys[1]) + e2 * zz
                    y_ref[pl.ds(r0, CHUNK), lo:lo + LANE] = yv + xs * dsk_ref[:, lo:lo + LANE]
                    cd2 = _pair_sel(lane_lo, cd[:, ha:ha + 1], cd[:, hb_:hb_ + 1])
                    h_sc[:, lo:lo + LANE] = hp * cd2 + jnp.where(lane_lo, sts[0], sts[1])
            return carry

        lax.fori_loop(0, cpb, chunk, 0)
        o_ref[...] = _gated_norm_fwd(y_ref[...], z_ref[...], g_ref[...])

    return pl.pallas_call(
        body, name="ssd_fwd", grid=(s // tm,),
        in_specs=[_rows(tm, CONV_CH), _rows(tm, LANE), _rows(tm, SSM_WIDTH), _whole((1, LANE)),
                  _whole((1, SSM_WIDTH)), _whole((1, SSM_WIDTH)), _whole((CHUNK, CHUNK)), _whole((CHUNK, CHUNK))],
        out_specs=[_rows(tm, SSM_WIDTH), _rows(tm, SSM_WIDTH),
                   pl.BlockSpec((cpb, SSM_N, SSM_WIDTH), lambda i: (i, 0, 0))],
        out_shape=[jax.ShapeDtypeStruct((s, SSM_WIDTH), F32), jax.ShapeDtypeStruct((s, SSM_WIDTH), F32),
                   jax.ShapeDtypeStruct((nc, SSM_N, SSM_WIDTH), F32)],
        scratch_shapes=[pltpu.VMEM((SSM_N, SSM_WIDTH), F32)],
        compiler_params=_cparams("arbitrary"),
    )(xbc, dt, z, a_neg, dskip_x, g_x, tril, triu)


def _outln(o, z_attn, o_ssm, w_out, x, gate, ln_g, ln_b, tgt):
    s = x.shape[0]
    tm = min(ROW_TILE_WIDE, s)

    def body(o_ref, z_ref, os_ref, w_ref, x_ref, gate_ref, g_ref, b_ref, t_ref,
             cat_ref, dmix_ref, gx_ref, do_ref, dz_ref, dl_ref, dos_ref, loss_ref, dg_ref, db_ref, dgate_ref):
        ov, zv = o_ref[...], z_ref[...]
        sz = _silu(zv)
        cat_ref[:, :MLA_WIDTH] = (ov * sz).astype(BF16)
        cat_ref[:, MLA_WIDTH:] = os_ref[...].astype(BF16)
        w = w_ref[...]
        mixed = _nn(cat_ref[...], w)
        gate_v = gate_ref[...]
        hv = DEEPNORM_ALPHA * x_ref[...] + gate_v * mixed
        mu = jnp.mean(hv, axis=-1, keepdims=True)
        hc = hv - mu
        rstd = lax.rsqrt(jnp.mean(hc * hc, axis=-1, keepdims=True) + LN_EPS)
        xhat = hc * rstd
        g = g_ref[...]
        err = xhat * g + b_ref[...] - t_ref[...]
        _acc_rows(loss_ref, jnp.full((1, LANE), (0.5 / D_MODEL) * jnp.sum(err * err), F32))
        dy = err * (1.0 / D_MODEL)
        _acc_rows(dg_ref, _colsum(dy * xhat))
        _acc_rows(db_ref, _colsum(dy))
        dxhat = dy * g
        dh = rstd * (dxhat - jnp.mean(dxhat, axis=-1, keepdims=True)
                     - xhat * jnp.mean(dxhat * xhat, axis=-1, keepdims=True))
        gx_ref[...] = DEEPNORM_ALPHA * dh
        _acc_rows(dgate_ref, _colsum(dh * mixed))
        dmix = (gate_v * dh).astype(BF16)
        dmix_ref[...] = dmix
        dcat = _nt(dmix, w)
        da = dcat[:, :MLA_WIDTH]
        dos_ref[...] = dcat[:, MLA_WIDTH:]
        dov = da * sz
        do_ref[...] = dov.T.astype(BF16)
        dz_ref[...] = da * ov * _silu_grad(zv)
        prod = dov * ov
        for h in range(MLA_HEADS):
            dsum = jnp.sum(prod[:, h * V_DIM:(h + 1) * V_DIM], axis=1, keepdims=True)
            dl_ref[h] = jnp.broadcast_to(dsum, (tm, LANE)).T[0:1, :]

    vec = _whole((1, D_MODEL))
    return pl.pallas_call(
        body, name="outln", grid=(s // tm,),
        in_specs=[_rows(tm, MLA_WIDTH), _rows(tm, MLA_WIDTH), _rows(tm, SSM_WIDTH), _whole((MIX_WIDTH, D_MODEL)),
                  _rows(tm, D_MODEL), vec, vec, vec, _rows(tm, D_MODEL)],
        out_specs=[_rows(tm, MIX_WIDTH), _rows(tm, D_MODEL), _rows(tm, D_MODEL),
                   pl.BlockSpec((MLA_WIDTH, tm), lambda i: (0, i)),
                   _rows(tm, MLA_WIDTH), pl.BlockSpec((MLA_HEADS, 1, tm), lambda i: (0, 0, i)), _rows(tm, SSM_WIDTH),
                   _whole((1, LANE)), vec, vec, vec],
        out_shape=[jax.ShapeDtypeStruct((s, MIX_WIDTH), BF16), jax.ShapeDtypeStruct((s, D_MODEL), BF16),
                   jax.ShapeDtypeStruct((s, D_MODEL), F32), jax.ShapeDtypeStruct((MLA_WIDTH, s), BF16),
                   jax.ShapeDtypeStruct((s, MLA_WIDTH), F32), jax.ShapeDtypeStruct((MLA_HEADS, 1, s), F32),
                   jax.ShapeDtypeStruct((s, SSM_WIDTH), F32), jax.ShapeDtypeStruct((1, LANE), F32),
                   jax.ShapeDtypeStruct((1, D_MODEL), F32), jax.ShapeDtypeStruct((1, D_MODEL), F32),
                   jax.ShapeDtypeStruct((1, D_MODEL), F32)],
        compiler_params=_cparams("arbitrary"),
    )(o, z_attn, o_ssm, w_out, x, gate, ln_g, ln_b, tgt)


def _ssd_bwd(dos, y, z, xbc, dt, hin, a_neg, dskip_x, g_x, tril, triu, expand):
    s = xbc.shape[0]
    tm = min(SSD_ROWS, s)
    cpb = tm // CHUNK
    nb = s // tm
    gw = SSM_WIDTH // SSM_GROUPS
    hpg = SSM_HEADS // SSM_GROUPS

    def body(dos_ref, y_ref, z_ref, xbc_ref, dt_ref, hin_ref, a_ref, dsk_ref, g_ref, tril_ref, triu_ref, exp_ref,
             dxbc_ref, ddt_ref, dz_ref, dg_ref, ddsk_ref, da_ref, dh_sc, dy_sc):
        @pl.when(pl.program_id(0) == 0)
        def _():
            dh_sc[...] = jnp.zeros(dh_sc.shape, F32)

        yv, zv, dov = y_ref[...], z_ref[...], dos_ref[...]
        sz = _silu(zv)
        hf = yv * sz
        gv = g_ref[...]
        dgs, dhfs = [], []
        for grp in range(SSM_GROUPS):
            sl = slice(grp * gw, (grp + 1) * gw)
            n, rstd = _rms(hf[:, sl])
            dgs.append(_colsum(dov[:, sl] * n))
            dhfs.append(_rms_bwd(dov[:, sl] * gv[:, sl], n, rstd))
        dhf = jnp.concatenate(dhfs, axis=1)
        _acc_rows(dg_ref, jnp.concatenate(dgs, axis=1))
        dy_sc[...] = dhf * sz
        dz_ref[...] = dhf * yv * _silu_grad(zv)

        tril, triu, expand = tril_ref[...], triu_ref[...], exp_ref[...]
        ltri = tril > 0
        utri = triu > 0
        lane = _lane_iota((CHUNK, LANE))
        lane1 = _lane_iota((1, LANE))
        lane_lo = lane < SSM_P
        row_last = lax.broadcasted_iota(jnp.int32, (CHUNK, LANE), 0) == CHUNK - 1
        a_neg_v = a_ref[...]

        def chunk(ci, carry):
            dsk_acc, da_acc = carry
            cl = cpb - 1 - ci
            r0 = pl.multiple_of(cl * CHUNK, CHUNK)
            rows = pl.ds(r0, CHUNK)
            dtc = dt_ref[rows, :]
            lam_c, lam_r, lam_last = _chunk_common(dtc, a_neg_v, tril, triu)
            e_c = jnp.exp(lam_c)
            f_c = jnp.exp(lam_last - lam_c)
            cd = jnp.exp(lam_last)
            dt_x, e_x, f_x = _tri_right(dtc, expand), _tri_right(e_c, expand), _tri_right(f_c, expand)
            cd_x = _tri_right(jnp.broadcast_to(cd, (HALO, LANE)), expand)[0:1, :]
            dlam = jnp.zeros((CHUNK, LANE), F32)
            dlast = jnp.zeros((1, LANE), F32)
            ddt_x = jnp.zeros((CHUNK, LANE), F32)
            dsk_parts = []
            for grp in range(SSM_GROUPS):
                bo = SSM_WIDTH + grp * SSM_N
                co = SSM_WIDTH + SSM_GROUPS * SSM_N + grp * SSM_N
                bm = xbc_ref[rows, bo:bo + SSM_N]
                cm = xbc_ref[rows, co:co + SSM_N]
                bm_b, cm_b = bm.astype(BF16), cm.astype(BF16)
                gmat = _nt(cm_b, bm_b)
                gmat_t = _nt(bm_b, cm_b)
                ct_b = cm.T.astype(BF16)
                acc_dg = jnp.zeros((CHUNK, CHUNK), F32)
                acc_dgt = jnp.zeros((CHUNK, CHUNK), F32)
                d_b = jnp.zeros((CHUNK, SSM_N), F32)
                d_c = jnp.zeros((CHUNK, SSM_N), F32)
                for pj in range(hpg // 2):
                    ha = grp * hpg + 2 * pj
                    hb_ = ha + 1
                    lo = ha * SSM_P
                    blk = slice(lo, lo + LANE)
                    xs = xbc_ref[rows, blk]
                    dt2, e2, f2, cd2 = dt_x[:, blk], e_x[:, blk], f_x[:, blk], cd_x[:, blk]
                    x2 = xs * dt2
                    x2b = x2.astype(BF16)
                    dy2 = dy_sc[rows, blk]
                    dy2b = dy2.astype(BF16)
                    hp = hin_ref[cl, :, blk]
                    hp_b = hp.astype(BF16)
                    dhn = dh_sc[:, blk]
                    dhn_b = dhn.astype(BF16)
                    yo = e2 * _nn(cm_b, hp_b)
                    dzz_b = (e2 * dy2).astype(BF16)
                    d_c = d_c + _nt(dzz_b, hp_b)
                    dh_sc[:, blk] = _nn(ct_b, dzz_b) + cd2 * dhn
                    dxs2 = f2 * _nn(bm_b, dhn_b)
                    d_b = d_b + _nt((f2 * x2).astype(BF16), dhn_b)
                    xd = x2 * dxs2
                    t_lam = dy2 * yo - xd
                    t_last = cd2 * (dhn * hp) + xd
                    dxd2 = jnp.zeros((CHUNK, LANE), F32)
                    heads = ((ha, lane_lo), (hb_, jnp.logical_not(lane_lo)))
                    for hh, msk in heads:
                        x2h_b = jnp.where(msk, x2, 0.0).astype(BF16)
                        dy2h_b = jnp.where(msk, dy2, 0.0).astype(BF16)
                        seg = lam_c[:, hh:hh + 1] - lam_r[hh:hh + 1, :]
                        dec = jnp.exp(jnp.where(ltri, seg, -jnp.inf))
                        dect = jnp.exp(jnp.where(utri, -seg, -jnp.inf))
                        dmd = _nt(dy2h_b, x2b) * dec
                        dmtd = _nt(x2h_b, dy2b) * dect
                        acc_dg = acc_dg + dmd
                        acc_dgt = acc_dgt + dmtd
                        dlam_h = jnp.sum(dmd * gmat - dmtd * gmat_t + jnp.where(msk, t_lam, 0.0), axis=1, keepdims=True)
                        last_h = jnp.sum(jnp.sum(jnp.where(msk, t_last, 0.0), axis=0, keepdims=True), axis=1, keepdims=True)
                        dlam = jnp.where(lane == hh, dlam_h, dlam)
                        dlast = jnp.where(lane1 == hh, last_h, dlast)
                        dxd2 = jnp.where(msk, _nn((gmat_t * dect).astype(BF16), dy2b), dxd2)
                    dx2 = dxd2 + dxs2
                    dxbc_ref[rows, blk] = dx2 * dt2 + dy2 * dsk_ref[:, blk]
                    prod = dx2 * xs
                    for hh, msk in heads:
                        col = jnp.sum(jnp.where(msk, prod, 0.0), axis=1, keepdims=True)
                        ddt_x = jnp.where(lane == hh, col, ddt_x)
                    dsk_parts.append(_colsum(dy2 * xs))
                d_c = d_c + _nn(acc_dg.astype(BF16), bm_b)
                d_b = d_b + _nn(acc_dgt.astype(BF16), cm_b)
                dxbc_ref[rows, bo:bo + SSM_N] = d_b
                dxbc_ref[rows, co:co + SSM_N] = d_c
            dlam = dlam + jnp.where(row_last, dlast, 0.0)
            da = _tri_left(triu, dlam)
            ddt_ref[rows, :] = da * a_neg_v + ddt_x
            return dsk_acc + jnp.concatenate(dsk_parts, axis=1), da_acc + _colsum(da * dtc)

        dsk_tot, da_tot = lax.fori_loop(
            0, cpb, chunk, (jnp.zeros((1, SSM_WIDTH), F32), jnp.zeros((1, LANE), F32)))
        _acc_rows(ddsk_ref, dsk_tot)
        _acc_rows(da_ref, da_tot)

    rev = lambda i: (nb - 1 - i, 0)
    rrows = lambda w: pl.BlockSpec((tm, w), rev)
    return pl.pallas_call(
        body, name="ssd_bwd", grid=(nb,),
        in_specs=[rrows(SSM_WIDTH), rrows(SSM_WIDTH), rrows(SSM_WIDTH), rrows(CONV_CH), rrows(LANE),
                  pl.BlockSpec((cpb, SSM_N, SSM_WIDTH), lambda i: (nb - 1 - i, 0, 0)),
                  _whole((1, LANE)), _whole((1, SSM_WIDTH)), _whole((1, SSM_WIDTH)),
                  _whole((CHUNK, CHUNK)), _whole((CHUNK, CHUNK)), _whole((LANE, SSM_WIDTH))],
        out_specs=[rrows(CONV_CH), rrows(LANE), rrows(SSM_WIDTH),
                   _whole((1, SSM_WIDTH)), _whole((1, SSM_WIDTH)), _whole((1, LANE))],
        out_shape=[jax.ShapeDtypeStruct((s, CONV_CH), F32), jax.ShapeDtypeStruct((s, LANE), F32),
                   jax.ShapeDtypeStruct((s, SSM_WIDTH), F32), jax.ShapeDtypeStruct((1, SSM_WIDTH), F32),
                   jax.ShapeDtypeStruct((1, SSM_WIDTH), F32), jax.ShapeDtypeStruct((1, LANE), F32)],
        scratch_shapes=[pltpu.VMEM((SSM_N, SSM_WIDTH), F32), pltpu.VMEM((tm, SSM_WIDTH), F32)],
        compiler_params=_cparams("arbitrary"),
    )(dos, y, z, xbc, dt, hin, a_neg, dskip_x, g_x, tril, triu, expand)


def _ssd_post_bwd(xbc_raw, dxa, ddt, dt_raw, conv_w, conv_b, dt_bias_p):
    s = xbc_raw.shape[0]
    tm = ROW_TILE
    hb = tm // HALO
    nt = s // tm
    ext = tm + HALO

    def body(x_ref, prev_ref, next_ref, d_ref, dnext_ref, ddt_ref, dtr_ref, w_ref, b_ref, db_ref,
             dx_ref, ddtr_ref, dw_ref, dcb_ref, ddb_ref, xe_sc, de_sc):
        i = pl.program_id(0)
        w = w_ref[...]
        xe_sc[pl.ds(0, HALO), :] = jnp.where(i > 0, prev_ref[...], 0.0)
        xe_sc[pl.ds(HALO, tm), :] = x_ref[...]
        xe_sc[pl.ds(HALO + tm, HALO), :] = next_ref[...]
        pre = _conv_taps(xe_sc, w, ext, 0) + b_ref[...]
        sg = _silu_grad(pre)
        de_sc[pl.ds(0, tm), :] = d_ref[...] * sg[:tm]
        de_sc[pl.ds(tm, HALO), :] = jnp.where(i < nt - 1, dnext_ref[...] * sg[tm:], 0.0)
        dconv = de_sc[pl.ds(0, tm), :]
        acc = None
        dws = []
        for k in range(CONV_K):
            term = de_sc[pl.ds(CONV_K - 1 - k, tm), :] * w[k:k + 1, :]
            acc = term if acc is None else acc + term
            dws.append(_colsum(dconv * xe_sc[pl.ds(HALO - (CONV_K - 1) + k, tm), :]))
        dx_ref[...] = acc
        _acc_rows(dw_ref, jnp.concatenate(dws, axis=0))
        _acc_rows(dcb_ref, _colsum(dconv))
        ddtr = ddt_ref[...] * _sigmoid(dtr_ref[...] + db_ref[...])
        ddtr_ref[...] = ddtr
        _acc_rows(ddb_ref, _colsum(ddtr))

    halo_prev = pl.BlockSpec((HALO, CONV_CH), lambda i: (jnp.maximum(i * hb - 1, 0), 0))
    halo_next = pl.BlockSpec((HALO, CONV_CH), lambda i: (jnp.minimum((i + 1) * hb, s // HALO - 1), 0))
    return pl.pallas_call(
        body, name="ssd_post_bwd", grid=(nt,),
        in_specs=[_rows(tm, CONV_CH), halo_prev, halo_next, _rows(tm, CONV_CH), halo_next, _rows(tm, LANE),
                  _rows(tm, LANE), _whole((CONV_K, CONV_CH)), _whole((1, CONV_CH)), _whole((1, LANE))],
        out_specs=[_rows(tm, CONV_CH), _rows(tm, LANE), _whole((CONV_K, CONV_CH)), _whole((1, CONV_CH)),
                   _whole((1, LANE))],
        out_shape=[jax.ShapeDtypeStruct((s, CONV_CH), F32), jax.ShapeDtypeStruct((s, LANE), F32),
                   jax.ShapeDtypeStruct((CONV_K, CONV_CH), F32), jax.ShapeDtypeStruct((1, CONV_CH), F32),
                   jax.ShapeDtypeStruct((1, LANE), F32)],
        scratch_shapes=[pltpu.VMEM((tm + 2 * HALO, CONV_CH), F32), pltpu.VMEM((ext, CONV_CH), F32)],
        compiler_params=_cparams("arbitrary"),
    )(xbc_raw, xbc_raw, xbc_raw, dxa, dxa, ddt, dt_raw, conv_w, conv_b, dt_bias_p)


def _qbwd(dq_att, q_lat, g_q, w_qb_p, cos, sin):
    s = q_lat.shape[0]
    tm = ROW_TILE
    wq = MLA_HEADS * HEAD_PAD

    def body(dq_ref, ql_ref, g_ref, w_ref, cos_ref, sin_ref, dql_ref, draw_ref, dg_ref):
        c, sn = cos_ref[...], sin_ref[...]
        for h in range(MLA_HEADS):
            o = h * HEAD_PAD
            dqh = dq_ref[h].astype(F32) * ATTN_SCALE
            draw_ref[:, o:o + QK_NOPE] = dqh[:, :QK_NOPE].astype(BF16)
            draw_ref[:, o + QK_NOPE:o + HEAD_PAD] = _rope_transposed(dqh[:, QK_NOPE:], c, sn).astype(BF16)
        dn = _nt(draw_ref[...], w_ref[...])
        xhat, rstd = _rms(ql_ref[...])
        _acc_rows(dg_ref, _colsum(dn * xhat))
        dql_ref[...] = _rms_bwd(dn * g_ref[...], xhat, rstd)

    return pl.pallas_call(
        body, name="qbwd", grid=(s // tm,),
        in_specs=[pl.BlockSpec((MLA_HEADS, tm, HEAD_PAD), lambda i: (0, i, 0)), _rows(tm, Q_RANK), _whole((1, Q_RANK)),
                  _whole((Q_RANK, wq)), _rows(tm, LANE), _rows(tm, LANE)],
        out_specs=[_rows(tm, Q_RANK), _rows(tm, wq), _whole((1, Q_RANK))],
        out_shape=[jax.ShapeDtypeStruct((s, Q_RANK), F32), jax.ShapeDtypeStruct((s, wq), BF16),
                   jax.ShapeDtypeStruct((1, Q_RANK), F32)],
        compiler_params=_cparams("arbitrary"),
    )(dq_att, q_lat, g_q, w_qb_p, cos, sin)


def _kvbwd(dk_att, dv, kv_lat, g_kv, w_kvb_p, cos, sin):
    s = kv_lat.shape[0]
    tm = ROW_TILE
    wk = MLA_HEADS * HEAD_PAD
    wr = MLA_HEADS * (QK_NOPE + V_DIM)

    def body(dk_ref, dv_ref, kl_ref, g_ref, w_ref, cos_ref, sin_ref, dkl_ref, draw_ref, dg_ref):
        dkr = None
        for h in range(MLA_HEADS):
            o = h * HEAD_PAD
            draw_ref[:, h * QK_NOPE:(h + 1) * QK_NOPE] = dk_ref[:, o:o + QK_NOPE].astype(BF16)
            part = dk_ref[:, o + QK_NOPE:o + HEAD_PAD].astype(F32)
            dkr = part if dkr is None else dkr + part
        draw_ref[:, MLA_HEADS * QK_NOPE:] = dv_ref[...].astype(BF16)
        dn = _nt(draw_ref[...], w_ref[...])
        xhat, rstd = _rms(kl_ref[:, :KV_RANK])
        _acc_rows(dg_ref, _colsum(dn * xhat))
        dkl_ref[:, :KV_RANK] = _rms_bwd(dn * g_ref[...], xhat, rstd)
        dkl_ref[:, KV_RANK:] = _rope_transposed(dkr, cos_ref[...], sin_ref[...])

    return pl.pallas_call(
        body, name="kvbwd", grid=(s // tm,),
        in_specs=[_rows(tm, wk), _rows(tm, MLA_WIDTH), _rows(tm, KV_LAT_PAD), _whole((1, KV_RANK)),
                  _whole((KV_RANK, wr)), _rows(tm, LANE), _rows(tm, LANE)],
        out_specs=[_rows(tm, KV_LAT_PAD), _rows(tm, wr), _whole((1, KV_RANK))],
        out_shape=[jax.ShapeDtypeStruct((s, KV_LAT_PAD), F32), jax.ShapeDtypeStruct((s, wr), BF16),
                   jax.ShapeDtypeStruct((1, KV_RANK), F32)],
        compiler_params=_cparams("arbitrary"),
    )(dk_att, dv, kv_lat, g_kv, w_kvb_p, cos, sin)


def _inproj_bwd(pieces, w_in_pt, x, scale1p, gx1):
    s = x.shape[0]
    tm = min(ROW_TILE, s)

    def body(*refs):
        p_refs = refs[:len(IN_PAD)]
        w_ref, x_ref, sc_ref, gx1_ref, gx_ref, dp_ref, dsc_ref, dsh_ref = refs[len(IN_PAD):]
        off = 0
        for ref, w in zip(p_refs, IN_PAD):
            dp_ref[:, off:off + w] = ref[...].astype(BF16)
            off += w
        du = _nn(dp_ref[...], w_ref[...])
        gx_ref[...] = gx1_ref[...] + du * sc_ref[...]
        _acc_rows(dsc_ref, _colsum(du * x_ref[...]))
        _acc_rows(dsh_ref, _colsum(du))

    vec = _whole((1, D_MODEL))
    return pl.pallas_call(
        body, name="inproj_bwd", grid=(s // tm,),
        in_specs=[_rows(tm, w) for w in IN_PAD] + [_whole_once((IN_PAD_WIDTH, D_MODEL)), _rows(tm, D_MODEL), vec,
                                                    _rows(tm, D_MODEL)],
        out_specs=[_rows(tm, D_MODEL), _rows(tm, IN_PAD_WIDTH), vec, vec],
        out_shape=[jax.ShapeDtypeStruct((s, D_MODEL), F32), jax.ShapeDtypeStruct((s, IN_PAD_WIDTH), BF16),
                   jax.ShapeDtypeStruct((1, D_MODEL), F32), jax.ShapeDtypeStruct((1, D_MODEL), F32)],
        compiler_params=_cparams("arbitrary"),
    )(*pieces, w_in_pt, x, scale1p, gx1)


def _matmul_tn_rows(name, a, b, tk):
    s, k = a.shape
    n = b.shape[1]
    tm = min(GRAD_ROWS, s)

    def body(a_ref, b_ref, o_ref):
        @pl.when(pl.program_id(1) == 0)
        def _():
            o_ref[...] = jnp.zeros_like(o_ref)
        o_ref[...] += _tn(a_ref[...], b_ref[...])

    return pl.pallas_call(
        body, name=name, grid=(k // tk, s // tm),
        in_specs=[pl.BlockSpec((tm, tk), lambda j, i: (i, j)), pl.BlockSpec((tm, n), lambda j, i: (i, 0))],
        out_specs=pl.BlockSpec((tk, n), lambda j, i: (j, 0)),
        out_shape=jax.ShapeDtypeStruct((k, n), F32),
        compiler_params=_cparams("parallel", "arbitrary"),
    )(a, b)


def _matmul_tn(name, a, b, tn):
    s, k = a.shape
    n = b.shape[1]
    tm = min(GRAD_ROWS, s)

    def body(a_ref, b_ref, o_ref):
        @pl.when(pl.program_id(1) == 0)
        def _():
            o_ref[...] = jnp.zeros_like(o_ref)
        o_ref[...] += _tn(a_ref[...], b_ref[...])

    return pl.pallas_call(
        body, name=name, grid=(n // tn, s // tm),
        in_specs=[pl.BlockSpec((tm, k), lambda j, i: (i, 0)), pl.BlockSpec((tm, tn), lambda j, i: (i, j))],
        out_specs=pl.BlockSpec((k, tn), lambda j, i: (0, j)),
        out_shape=jax.ShapeDtypeStruct((k, n), F32),
        compiler_params=_cparams("parallel", "arbitrary"),
    )(a, b)


def _pack_w_in_t(w_in_t):
    parts, off = [], 0
    for w, wp in zip(IN_SPLITS, IN_PAD):
        parts.append(jnp.pad(w_in_t[off:off + w], ((0, wp - w), (0, 0))))
        off += w
    return jnp.concatenate(parts, axis=0)


def _unpack_w_in_t(g):
    parts, off = [], 0
    for w, wp in zip(IN_SPLITS, IN_PAD):
        parts.append(g[off:off + w])
        off += wp
    return jnp.concatenate(parts, axis=0)


def _rope_tables(positions):
    inv_freq = 1.0 / (ROPE_THETA ** (jnp.arange(ROPE_HALF, dtype=F32) / ROPE_HALF))
    ang = positions.astype(F32)[:, None] * inv_freq
    cos, sin = jnp.cos(ang), jnp.sin(ang)
    zeros = jnp.zeros((positions.shape[0], LANE - QK_ROPE), F32)
    return jnp.concatenate([cos, cos, zeros], axis=1), jnp.concatenate([-sin, sin, zeros], axis=1)


def _local_step(x, tgt, positions, mod, w_in_t, q_norm_g, w_qb_p, kv_norm_g, w_kvb_p, conv_w, conv_b, dt_bias,
                a_log, d_skip, ssm_norm_g, w_out_b, ln_g, ln_b):
    row = lambda v: v.reshape(1, -1)
    shift, scale, gate = mod[:D_MODEL], mod[D_MODEL:2 * D_MODEL], mod[2 * D_MODEL:]
    scale1p = row(1.0 + scale)
    w_in_p = _pack_w_in_t(w_in_t)
    cos, sin = _rope_tables(positions)
    a_neg = row(jnp.pad(-jnp.exp(a_log), (0, LANE - SSM_HEADS)))
    dskip_x = row(jnp.repeat(d_skip, SSM_P))
    dt_bias_p = row(jnp.pad(dt_bias, (0, LANE - SSM_HEADS)))
    tri = jnp.tril(jnp.ones((CHUNK, CHUNK), F32))
    tril, triu = tri.astype(BF16), tri.T.astype(BF16)

    u_bf, q_lat, kv_lat, z_attn, xbc_raw, dt_raw, z_ssm = _inproj(x, scale1p, row(shift), w_in_p)
    nq_bf, q_att = _qpath(q_lat, row(q_norm_g), w_qb_p, cos, sin)
    nkv_bf, k_att, v_att, vt_att = _kvpath(kv_lat, row(kv_norm_g), w_kvb_p, cos, sin)
    o, lse_rows = _attn_fwd(q_att, k_att, vt_att)
    xbc, dt = _ssd_pre(xbc_raw, dt_raw, conv_w, row(conv_b), dt_bias_p)
    expand = jnp.repeat(jnp.eye(LANE, SSM_HEADS, dtype=BF16), SSM_P, axis=1)
    y, o_ssm, hin = _ssd_fwd(xbc, dt, z_ssm, a_neg, dskip_x, row(ssm_norm_g), tril, triu)
    (cat_bf, dmix_bf, gx1, do_t, dz_attn, delta_rows, dos, loss, d_ln_g, d_ln_b, d_gate) = _outln(
        o, z_attn, o_ssm, w_out_b, x, row(gate), row(ln_g), row(ln_b), tgt)

    g_w_out = _matmul_tn("gw_out", cat_bf, dmix_bf, 512)
    dk_att, dv, dq_att = _attn_bwd(q_att, k_att, v_att, do_t, lse_rows, delta_rows)
    dq_lat, dqraw_bf, d_q_norm_g = _qbwd(dq_att, q_lat, row(q_norm_g), w_qb_p, cos, sin)
    dkv_lat, dkvraw_bf, d_kv_norm_g = _kvbwd(dk_att, dv, kv_lat, row(kv_norm_g), w_kvb_p, cos, sin)
    g_w_qb = _matmul_tn("gw_qb", nq_bf, dqraw_bf, MLA_HEADS * HEAD_PAD)
    g_w_kvb = _matmul_tn("gw_kvb", nkv_bf, dkvraw_bf, MLA_HEADS * (QK_NOPE + V_DIM))
    dxa, ddt, dz_ssm, d_ssm_g, ddsk_x, d_a = _ssd_bwd(dos, y, z_ssm, xbc, dt, hin, a_neg, dskip_x, row(ssm_norm_g),
                                                       tril, triu, expand)
    dxbc_raw, ddt_raw, d_conv_w, d_conv_b, d_dt_bias = _ssd_post_bwd(xbc_raw, dxa, ddt, dt_raw, conv_w, row(conv_b),
                                                                     dt_bias_p)
    grad_x, dproj_bf, d_scale, d_shift = _inproj_bwd((dq_lat, dkv_lat, dz_attn, dxbc_raw, ddt_raw, dz_ssm),
                                                     w_in_p, x, scale1p, gx1)
    g_w_in_t = _unpack_w_in_t(_matmul_tn_rows("gw_in", dproj_bf, u_bf, 896))
    return dict(
        loss=loss[0, 0], grad_x=grad_x,
        dmod=jnp.concatenate([d_shift[0], d_scale[0], d_gate[0]]),
        w_in_t=g_w_in_t, q_norm_g=d_q_norm_g[0], w_qb=g_w_qb, kv_norm_g=d_kv_norm_g[0], w_kvb=g_w_kvb,
        conv_w=d_conv_w, conv_b=d_conv_b[0], dt_bias=d_dt_bias[0, :SSM_HEADS],
        a_log=d_a[0, :SSM_HEADS] * a_neg[0, :SSM_HEADS],
        d_skip=ddsk_x.reshape(SSM_HEADS, SSM_P).sum(axis=1), ssm_norm_g=d_ssm_g[0], w_out=g_w_out,
        ln_g=d_ln_g[0], ln_b=d_ln_b[0])


ADAM_ROWS = 512


def _my_index():
    return 4 * lax.axis_index("x") + 2 * lax.axis_index("y") + lax.axis_index("c")


def _exchange(name, sends, gather):
    n = len(sends)
    peers = N_DEV - 1

    def body(*refs):
        send_refs, recv_refs = refs[:n], refs[n:2 * n]
        send_sems, recv_sems, local_sems = refs[2 * n:]
        x, y, c = lax.axis_index("x"), lax.axis_index("y"), lax.axis_index("c")
        me = 4 * x + 2 * y + c

        def src(a, idx):
            return send_refs[a] if gather else send_refs[a].at[idx]

        owns = [pltpu.make_async_copy(src(a, me), recv_refs[a].at[me], local_sems.at[a]) for a in range(n)]
        for cp in owns:
            cp.start()
        copies = []
        for k in range(1, N_DEV):
            px, py, pc = x ^ ((k >> 2) & 1), y ^ ((k >> 1) & 1), c ^ (k & 1)
            peer = 4 * px + 2 * py + pc
            for a in range(n):
                copies.append(pltpu.make_async_remote_copy(
                    src_ref=src(a, peer), dst_ref=recv_refs[a].at[me],
                    send_sem=send_sems.at[a * peers + k - 1], recv_sem=recv_sems.at[a * peers + k - 1],
                    device_id=(px, py, pc), device_id_type=pl.DeviceIdType.MESH))
        for cp in copies:
            cp.start()
        for cp in copies:
            cp.wait()
        for cp in owns:
            cp.wait()

    block_shape = lambda a: a.shape if gather else a.shape[1:]
    return pl.pallas_call(
        body, name=name,
        in_specs=[pl.BlockSpec(memory_space=pl.ANY)] * n, out_specs=[pl.BlockSpec(memory_space=pl.ANY)] * n,
        out_shape=[jax.ShapeDtypeStruct((N_DEV, *block_shape(a)), a.dtype) for a in sends],
        scratch_shapes=[pltpu.SemaphoreType.DMA((n * peers,)), pltpu.SemaphoreType.DMA((n * peers,)),
                        pltpu.SemaphoreType.DMA((n,))],
    )(*sends)


def _gather_two_level(name, sends):
    n = len(sends)
    per = N_DEV - 1

    def body(*refs):
        send_refs, recv_refs = refs[:n], refs[n:2 * n]
        send_sems, recv_sems, local_sems = refs[2 * n:]
        x, y, c = lax.axis_index("x"), lax.axis_index("y"), lax.axis_index("c")
        sibling = (x, y, 1 - c)
        chips = [(1 - x, y), (x, 1 - y), (1 - x, 1 - y)]

        def idx(px, py, pc):
            return 4 * px + 2 * py + pc

        def copy(a, k, block, to, src=None):
            slot = recv_refs[a].at[idx(*block)]
            return pltpu.make_async_remote_copy(
                src_ref=slot if src is None else src, dst_ref=slot,
                send_sem=send_sems.at[a * per + k], recv_sem=recv_sems.at[a * per + k],
                device_id=to, device_id_type=pl.DeviceIdType.MESH)

        me = (x, y, c)
        owns = [pltpu.make_async_copy(send_refs[a], recv_refs[a].at[idx(*me)], local_sems.at[a]) for a in range(n)]
        for cp in owns:
            cp.start()
        first = [copy(a, 0, me, sibling, src=send_refs[a]) for a in range(n)]
        first += [copy(a, 1 + j, me, (*chip, c), src=send_refs[a]) for j, chip in enumerate(chips) for a in range(n)]
        for cp in first:
            cp.start()
        passed = []
        for j, chip in enumerate(chips):
            for a in range(n):
                copy(a, 1 + j, (*chip, c), me).wait_recv()
                fwd = copy(a, 4 + j, (*chip, c), sibling)
                fwd.start()
                passed.append(fwd)
        for a in range(n):
            copy(a, 0, sibling, me).wait_recv()
            for j, chip in enumerate(chips):
                copy(a, 4 + j, (*chip, 1 - c), me).wait_recv()
        for cp in first + passed:
            cp.wait_send()
        for cp in owns:
            cp.wait()

    return pl.pallas_call(
        body, name=name,
        in_specs=[pl.BlockSpec(memory_space=pl.ANY)] * n, out_specs=[pl.BlockSpec(memory_space=pl.ANY)] * n,
        out_shape=[jax.ShapeDtypeStruct((N_DEV, *a.shape), a.dtype) for a in sends],
        scratch_shapes=[pltpu.SemaphoreType.DMA((n * per,)), pltpu.SemaphoreType.DMA((n * per,)),
                        pltpu.SemaphoreType.DMA((n,))],
    )(*sends)


def _flat_rows(parts, row_multiple):
    flat = jnp.concatenate([p.reshape(-1) for p in parts])
    chunk = row_multiple * LANE
    total = -(-flat.shape[0] // chunk) * chunk
    return jnp.pad(flat, (0, total - flat.shape[0])).reshape(-1, LANE)


def _unflat(flat, shapes):
    flat = flat.reshape(-1)
    out, off = [], 0
    for shp in shapes:
        n = math.prod(shp)
        out.append(flat[off:off + n].reshape(shp))
        off += n
    return out


def _adam_update(g, w, m, v):
    m2 = ADAM_B1 * m + (1.0 - ADAM_B1) * g
    v2 = ADAM_B2 * v + (1.0 - ADAM_B2) * (g * g)
    m_hat = m2 / (1.0 - ADAM_B1 ** ADAM_STEP)
    v_hat = v2 / (1.0 - ADAM_B2 ** ADAM_STEP)
    delta = -ADAM_LR * (m_hat / (jnp.sqrt(v_hat) + ADAM_EPS) + ADAM_WD * w)
    return delta, m2, v2


def _adamw_summed(name, parts, w, m, v):
    r = w.shape[0]
    tr = min(ADAM_ROWS, r)

    def body(p_ref, w_ref, m_ref, v_ref, g_ref, d_ref, m2_ref, v2_ref):
        g = p_ref[0]
        for j in range(1, N_DEV):
            g = g + p_ref[j]
        g_ref[...] = g
        d_ref[...], m2_ref[...], v2_ref[...] = _adam_update(g, w_ref[...], m_ref[...], v_ref[...])

    rows = _rows(tr, LANE)
    return pl.pallas_call(
        body, name=name, grid=(r // tr,),
        in_specs=[pl.BlockSpec((N_DEV, tr, LANE), lambda i: (0, i, 0)), rows, rows, rows],
        out_specs=[rows] * 4, out_shape=[jax.ShapeDtypeStruct((r, LANE), F32)] * 4,
        compiler_params=_cparams("parallel"),
    )(parts, w, m, v)


def _modpart(c_all, w_ada, b_cols):
    def body(c_ref, w_ref, b_ref, o_ref):
        o_ref[...] = _nn(c_ref[...].astype(BF16), w_ref[...].astype(BF16)) + b_ref[...]

    return pl.pallas_call(
        body, name="modpart", out_shape=jax.ShapeDtypeStruct((N_DEV, w_ada.shape[1]), F32),
    )(c_all, w_ada, b_cols)


def _adamw_w_ada(c_all_t, dmod_cols, w, m, v):
    def body(c_ref, d_ref, w_ref, m_ref, v_ref, g_ref, dl_ref, m2_ref, v2_ref):
        g = c_ref[:, 0:1] * d_ref[0:1, :]
        for b in range(1, N_DEV):
            g = g + c_ref[:, b:b + 1] * d_ref[b:b + 1, :]
        g_ref[...] = g
        dl_ref[...], m2_ref[...], v2_ref[...] = _adam_update(g, w_ref[...], m_ref[...], v_ref[...])

    return pl.pallas_call(
        body, name="adamw_w_ada", out_shape=[jax.ShapeDtypeStruct(w.shape, F32)] * 4,
        compiler_params=pltpu.CompilerParams(vmem_limit_bytes=VMEM_LIMIT),
    )(c_all_t, dmod_cols, w, m, v)


W_IN_SHARD = IN_WIDTH // N_DEV
W_IN_SHARD_LANES = -(-W_IN_SHARD // LANE) * LANE
BF16_ROWS = 16
W_IN_SEND_ROWS = -(-W_IN_SHARD // BF16_ROWS) * BF16_ROWS


def _transpose_cast(w_pad):
    def body(w_ref, o_ref):
        o_ref[...] = w_ref[...].T.astype(BF16)

    return pl.pallas_call(
        body, name="w_in_transpose", out_shape=jax.ShapeDtypeStruct(w_pad.shape[::-1], BF16),
        compiler_params=pltpu.CompilerParams(vmem_limit_bytes=VMEM_LIMIT),
    )(w_pad)


def _adamw_w_in(parts, w, m, v):
    rows_t = parts.shape[1]
    d, cols = w.shape
    tb = ROW_TILE

    def body(p_ref, w_ref, m_ref, v_ref, g_ref, d_ref, m2_ref, v2_ref):
        gt = p_ref[0].astype(F32)
        for j in range(1, N_DEV):
            gt = gt + p_ref[j].astype(F32)
        gt = jnp.concatenate([gt, jnp.zeros((W_IN_SHARD_LANES - rows_t, tb), F32)], axis=0)
        g = gt.T[:, :cols]
        g_ref[...] = g
        d_ref[...], m2_ref[...], v2_ref[...] = _adam_update(g, w_ref[...], m_ref[...], v_ref[...])

    blk = _rows(tb, cols)
    return pl.pallas_call(
        body, name="adamw_w_in", grid=(d // tb,),
        in_specs=[pl.BlockSpec((N_DEV, rows_t, tb), lambda i: (0, 0, i)), blk, blk, blk],
        out_specs=[blk] * 4, out_shape=[jax.ShapeDtypeStruct(w.shape, F32)] * 4,
        compiler_params=_cparams("parallel"),
    )(parts, w, m, v)


SHARDED = ("w_qb", "w_kvb", "w_out")
REPLICATED = ("b_ada", "q_norm_g", "kv_norm_g", "conv_b", "dt_bias", "a_log", "d_skip", "ssm_norm_g", "ln_g", "ln_b")
WEIGHTS = ("w_ada", "b_ada", "w_in", "q_norm_g", "w_qb", "kv_norm_g", "w_kvb", "conv_w", "conv_b", "dt_bias",
           "a_log", "d_skip", "ssm_norm_g", "w_out", "ln_g", "ln_b")
HEAD_COLS = QK_NOPE + V_DIM


def _adamw_blocks(name, parts, w, m, v):
    r, c = w.shape
    tr = ROW_TILE if r % ROW_TILE == 0 else r

    def body(p_ref, w_ref, m_ref, v_ref, g_ref, d_ref, m2_ref, v2_ref):
        g = p_ref[0].astype(F32)
        for j in range(1, N_DEV):
            g = g + p_ref[j].astype(F32)
        g_ref[...] = g
        d_ref[...], m2_ref[...], v2_ref[...] = _adam_update(g, w_ref[...], m_ref[...], v_ref[...])

    blk = _rows(tr, c)
    return pl.pallas_call(
        body, name=name, grid=(r // tr,),
        in_specs=[pl.BlockSpec((N_DEV, tr, c), lambda i: (0, i, 0)), blk, blk, blk],
        out_specs=[blk] * 4, out_shape=[jax.ShapeDtypeStruct(w.shape, F32)] * 4,
        compiler_params=_cparams("parallel"),
    )(parts, w, m, v)


def kernel(x, c, positions, w_ada, b_ada, w_in, q_norm_g, w_qb, kv_norm_g, w_kvb, conv_w, conv_b, dt_bias, a_log, d_skip, ssm_norm_g, w_out, ln_g, ln_b, loss_target, m_w_ada, m_b_ada, m_w_in, m_q_norm_g, m_w_qb, m_kv_norm_g, m_w_kvb, m_conv_w, m_conv_b, m_dt_bias, m_a_log, m_d_skip, m_ssm_norm_g, m_w_out, m_ln_g, m_ln_b, v_w_ada, v_b_ada, v_w_in, v_q_norm_g, v_w_qb, v_kv_norm_g, v_w_kvb, v_conv_w, v_conv_b, v_dt_bias, v_a_log, v_d_skip, v_ssm_norm_g, v_w_out, v_ln_g, v_ln_b):
    given = dict(w_ada=w_ada, b_ada=b_ada, w_in=w_in, q_norm_g=q_norm_g, w_qb=w_qb, kv_norm_g=kv_norm_g, w_kvb=w_kvb,
                 conv_w=conv_w, conv_b=conv_b, dt_bias=dt_bias, a_log=a_log, d_skip=d_skip, ssm_norm_g=ssm_norm_g,
                 w_out=w_out, ln_g=ln_g, ln_b=ln_b)
    mom = dict(w_ada=m_w_ada, b_ada=m_b_ada, w_in=m_w_in, q_norm_g=m_q_norm_g, w_qb=m_w_qb, kv_norm_g=m_kv_norm_g,
               w_kvb=m_w_kvb, conv_w=m_conv_w, conv_b=m_conv_b, dt_bias=m_dt_bias, a_log=m_a_log, d_skip=m_d_skip,
               ssm_norm_g=m_ssm_norm_g, w_out=m_w_out, ln_g=m_ln_g, ln_b=m_ln_b)
    var = dict(w_ada=v_w_ada, b_ada=v_b_ada, w_in=v_w_in, q_norm_g=v_q_norm_g, w_qb=v_w_qb, kv_norm_g=v_kv_norm_g,
               w_kvb=v_w_kvb, conv_w=v_conv_w, conv_b=v_conv_b, dt_bias=v_dt_bias, a_log=v_a_log, d_skip=v_d_skip,
               ssm_norm_g=v_ssm_norm_g, w_out=v_w_out, ln_g=v_ln_g, ln_b=v_ln_b)
    w0 = {k: a[0] for k, a in given.items()}
    m0 = {k: a[0] for k, a in mom.items()}
    v0 = {k: a[0] for k, a in var.items()}
    me = _my_index()

    w_in_rows = _transpose_cast(jnp.pad(w0["w_in"], ((0, 0), (0, W_IN_SHARD_LANES - W_IN_SHARD))))
    g_w_in, g_w_qb, g_w_kvb, g_w_out, g_conv_w, c_all = _gather_two_level(
        "gather_weights", [w_in_rows] + [w0[k].astype(BF16) for k in SHARDED] + [w0["conv_w"], c])
    c_all = c_all.reshape(N_DEV, D_MODEL)
    w_in_t = g_w_in[:, :W_IN_SHARD, :].reshape(IN_WIDTH, D_MODEL)
    w_qb_p = jnp.pad(g_w_qb, ((0, 0), (0, 0), (0, HEAD_PAD - QK_HEAD))).transpose(1, 0, 2).reshape(Q_RANK, -1)
    w_kvb_p = g_w_kvb.reshape(N_DEV, KV_RANK, 2, QK_NOPE).transpose(1, 2, 0, 3).reshape(KV_RANK, -1)
    w_out_b = g_w_out.reshape(MIX_WIDTH, D_MODEL)
    conv_w_full = g_conv_w.transpose(1, 0, 2).reshape(CONV_K, CONV_CH)

    ada_cols = w0["w_ada"].shape[1]
    b_cols = lax.dynamic_slice(w0["b_ada"], (me * ada_cols,), (ada_cols,)).reshape(1, ada_cols)
    mod_all, = _exchange("gather_mod", [_modpart(c_all, w0["w_ada"], b_cols)], gather=True)
    mod = lax.dynamic_index_in_dim(mod_all, me, axis=1, keepdims=False).reshape(-1)

    loc = _local_step(x[0], loss_target[0], positions[0], mod, w_in_t, w0["q_norm_g"], w_qb_p,
                      w0["kv_norm_g"], w_kvb_p, conv_w_full, w0["conv_b"], w0["dt_bias"], w0["a_log"],
                      w0["d_skip"], w0["ssm_norm_g"], w_out_b, w0["ln_g"], w0["ln_b"])

    rep_shapes = [w0[k].shape for k in REPLICATED] + [(1,)]
    rep_local = [loc["dmod"]] + [loc[k] for k in REPLICATED[1:]] + [loc["loss"].reshape(1)]
    rep_parts, conv_parts = _exchange("gather_small", [_flat_rows(rep_local, HALO), loc["conv_w"]], gather=True)
    conv_cols = w0["conv_w"].shape[1]
    conv_mine = lax.dynamic_slice(conv_parts, (0, 0, me * conv_cols), (N_DEV, CONV_K, conv_cols))
    outs = {"conv_w": _adamw_blocks("adamw_conv_w", conv_mine, w0["conv_w"], m0["conv_w"], v0["conv_w"])}
    zero1 = jnp.zeros((1,), F32)
    rep = _adamw_summed("adamw_replicated", rep_parts,
                        _flat_rows([w0[k] for k in REPLICATED] + [zero1], HALO),
                        _flat_rows([m0[k] for k in REPLICATED] + [zero1], HALO),
                        _flat_rows([v0[k] for k in REPLICATED] + [zero1], HALO))
    rep_g, rep_d, rep_m, rep_v = [_unflat(a, rep_shapes) for a in rep]
    loss = rep_g[-1][0]

    dmod_all = rep_parts.reshape(N_DEV, -1)[:, :3 * D_MODEL]
    dmod_cols = lax.dynamic_slice(dmod_all, (0, me * ada_cols), (N_DEV, ada_cols))
    outs["w_ada"] = _adamw_w_ada(c_all.T, dmod_cols, w0["w_ada"], m0["w_ada"], v0["w_ada"])

    send_w_in = loc["w_in_t"].astype(BF16).reshape(N_DEV, W_IN_SHARD, D_MODEL)
    send_w_in = jnp.pad(send_w_in, ((0, 0), (0, W_IN_SEND_ROWS - W_IN_SHARD), (0, 0)))
    send_w_qb = loc["w_qb"].astype(BF16).reshape(Q_RANK, N_DEV, HEAD_PAD)[:, :, :QK_HEAD].transpose(1, 0, 2)
    send_w_kvb = loc["w_kvb"].astype(BF16).reshape(KV_RANK, 2, N_DEV, QK_NOPE).transpose(2, 0, 1, 3)
    send_w_kvb = send_w_kvb.reshape(N_DEV, KV_RANK, HEAD_COLS)
    send_w_out = loc["w_out"].astype(BF16).reshape(N_DEV, MIX_WIDTH // N_DEV, D_MODEL)
    r_w_in, r_w_qb, r_w_kvb, r_w_out = _exchange(
        "scatter_grads", [send_w_in, send_w_qb, send_w_kvb, send_w_out], gather=False)
    outs["w_in"] = _adamw_w_in(r_w_in, w0["w_in"], m0["w_in"], v0["w_in"])
    for k, parts in zip(SHARDED, (r_w_qb, r_w_kvb, r_w_out)):
        outs[k] = _adamw_blocks("adamw_" + k, parts, w0[k], m0[k], v0[k])

    def collect(idx):
        out = {k: o[idx] for k, o in outs.items()}
        out.update({k: (rep_g, rep_d, rep_m, rep_v)[idx][i] for i, k in enumerate(REPLICATED)})
        return [out[k][None] for k in WEIGHTS]

    return (loss, loc["grad_x"][None], *collect(0), *collect(1), *collect(2), *collect(3))
```

```python
import math

import jax
import jax.numpy as jnp
from jax import lax
from jax.experimental import pallas as pl
from jax.experimental.pallas import tpu as pltpu

F32 = jnp.float32
BF16 = jnp.bfloat16

N_DEV = 8
D_MODEL = 1024
MLA_HEADS = 8
QK_NOPE = 128
QK_ROPE = 64
V_DIM = 128
Q_RANK = 384
KV_RANK = 256
QK_HEAD = QK_NOPE + QK_ROPE
HEAD_PAD = 256
ROPE_HALF = QK_ROPE // 2
ROPE_THETA = 10000.0
MLA_WIDTH = MLA_HEADS * V_DIM
SSM_HEADS = 16
SSM_P = 64
SSM_WIDTH = SSM_HEADS * SSM_P
SSM_GROUPS = 2
SSM_N = 128
CONV_K = 4
CHUNK = 128
CONV_CH = SSM_WIDTH + 2 * SSM_GROUPS * SSM_N
MIX_WIDTH = MLA_WIDTH + SSM_WIDTH
IN_SPLITS = (Q_RANK, KV_RANK + QK_ROPE, MLA_WIDTH, CONV_CH, SSM_HEADS, SSM_WIDTH)
IN_WIDTH = sum(IN_SPLITS)
LANE = 128
KV_LAT_PAD = KV_RANK + LANE
IN_PAD = (Q_RANK, KV_LAT_PAD, MLA_WIDTH, CONV_CH, LANE, SSM_WIDTH)
IN_PAD_WIDTH = sum(IN_PAD)
DEEPNORM_ALPHA = 2.0 ** 0.25
RMS_EPS = 1e-6
LN_EPS = 1e-5
ATTN_SCALE = QK_HEAD ** -0.5
LOG2E = math.log2(math.e)
LN2 = math.log(2.0)
Q_PRESCALE = ATTN_SCALE * LOG2E
ADAM_LR, ADAM_B1, ADAM_B2, ADAM_EPS, ADAM_WD, ADAM_STEP = 0.001, 0.9, 0.999, 1e-08, 0.01, 10

ROW_TILE = 512
ROW_TILE_WIDE = 256
ATTN_TILE = 512
ATTN_UNROLLS = (8, 4, 2)
SSD_ROWS = 512
GRAD_ROWS = 2048
VMEM_LIMIT = 56 * 1024 * 1024


def _nn(a, b):
    return jnp.dot(a, b, preferred_element_type=F32)


def _nt(a, b):
    return lax.dot_general(a, b, (((1,), (1,)), ((), ())), preferred_element_type=F32)


def _tn(a, b):
    return lax.dot_general(a, b, (((0,), (0,)), ((), ())), preferred_element_type=F32)


def _cparams(*sem):
    return pltpu.CompilerParams(dimension_semantics=sem, vmem_limit_bytes=VMEM_LIMIT)


def _rows(tm, w):
    return pl.BlockSpec((tm, w), lambda i: (i, 0))


def _whole(shape):
    return pl.BlockSpec(shape, lambda i: (0,) * len(shape))


def _whole_once(shape):
    return pl.BlockSpec(shape, lambda i: (0,) * len(shape), pipeline_mode=pl.Buffered(1))


def _sigmoid(z):
    return 1.0 / (1.0 + jnp.exp(-z))


def _lane_iota(shape):
    return lax.broadcasted_iota(jnp.int32, shape, len(shape) - 1)


def _swap_halves(r):
    lane = _lane_iota(r.shape)
    return jnp.where(lane < ROPE_HALF, pltpu.roll(r, LANE - ROPE_HALF, 1),
                     jnp.where(lane < QK_ROPE, pltpu.roll(r, ROPE_HALF, 1), 0.0))


def _rope(r, cos, sin):
    return r * cos + _swap_halves(r) * sin


def _rope_transposed(d, cos, sin):
    return d * cos + _swap_halves(d * sin)


def _rms(x):
    rstd = lax.rsqrt(jnp.mean(x * x, axis=-1, keepdims=True) + RMS_EPS)
    return x * rstd, rstd


def _rms_bwd(dxhat, xhat, rstd):
    return rstd * (dxhat - xhat * jnp.mean(dxhat * xhat, axis=-1, keepdims=True))


def _acc_rows(ref, val):
    @pl.when(pl.program_id(0) == 0)
    def _():
        ref[...] = jnp.zeros_like(ref)
    ref[...] += val


def _colsum(v):
    return jnp.sum(v, axis=0, keepdims=True)


def _inproj(x, scale1p, shift, w_in_pt):
    s = x.shape[0]
    tm = ROW_TILE

    def body(x_ref, sc_ref, sh_ref, w_ref, u_ref, *outs):
        u = (x_ref[...] * sc_ref[...] + sh_ref[...]).astype(BF16)
        u_ref[...] = u
        proj = _nt(u, w_ref[...])
        off = 0
        for ref, w in zip(outs, IN_PAD):
            ref[...] = proj[:, off:off + w]
            off += w

    return pl.pallas_call(
        body, name="inproj", grid=(s // tm,),
        in_specs=[_rows(tm, D_MODEL), _whole((1, D_MODEL)), _whole((1, D_MODEL)), _whole((IN_PAD_WIDTH, D_MODEL))],
        out_specs=[_rows(tm, D_MODEL)] + [_rows(tm, w) for w in IN_PAD],
        out_shape=[jax.ShapeDtypeStruct((s, D_MODEL), BF16)] + [jax.ShapeDtypeStruct((s, w), F32) for w in IN_PAD],
        compiler_params=_cparams("parallel"),
    )(x, scale1p, shift, w_in_pt)


def _qpath(q_lat, g_q, w_qb_p, cos, sin):
    s = q_lat.shape[0]
    tm = ROW_TILE

    def body(ql_ref, g_ref, w_ref, cos_ref, sin_ref, nq_ref, q_ref):
        xhat, _ = _rms(ql_ref[...])
        nq = (xhat * g_ref[...]).astype(BF16)
        nq_ref[...] = nq
        raw = _nn(nq, w_ref[...]) * Q_PRESCALE
        c, sn = cos_ref[...], sin_ref[...]
        for h in range(MLA_HEADS):
            o = h * HEAD_PAD
            q_ref[:, o:o + QK_NOPE] = raw[:, o:o + QK_NOPE].astype(BF16)
            q_ref[:, o + QK_NOPE:o + HEAD_PAD] = _rope(raw[:, o + QK_NOPE:o + HEAD_PAD], c, sn).astype(BF16)

    return pl.pallas_call(
        body, name="qpath", grid=(s // tm,),
        in_specs=[_rows(tm, Q_RANK), _whole((1, Q_RANK)), _whole((Q_RANK, MLA_HEADS * HEAD_PAD)),
                  _rows(tm, LANE), _rows(tm, LANE)],
        out_specs=[_rows(tm, Q_RANK), _rows(tm, MLA_HEADS * HEAD_PAD)],
        out_shape=[jax.ShapeDtypeStruct((s, Q_RANK), BF16), jax.ShapeDtypeStruct((s, MLA_HEADS * HEAD_PAD), BF16)],
        compiler_params=_cparams("parallel"),
    )(q_lat, g_q, w_qb_p, cos, sin)


def _kvpath(kv_lat, g_kv, w_kvb_p, cos, sin):
    s = kv_lat.shape[0]
    tm = ROW_TILE

    def body(kl_ref, g_ref, w_ref, cos_ref, sin_ref, nkv_ref, k_ref, v_ref, vt_ref):
        kl = kl_ref[...]
        xhat, _ = _rms(kl[:, :KV_RANK])
        nkv = (xhat * g_ref[...]).astype(BF16)
        nkv_ref[...] = nkv
        raw = _nn(nkv, w_ref[...])
        kr = _rope(kl[:, KV_RANK:], cos_ref[...], sin_ref[...]).astype(BF16)
        for h in range(MLA_HEADS):
            o = h * HEAD_PAD
            k_ref[:, o:o + QK_NOPE] = raw[:, h * QK_NOPE:(h + 1) * QK_NOPE].astype(BF16)
            k_ref[:, o + QK_NOPE:o + HEAD_PAD] = kr
        vals = raw[:, MLA_HEADS * QK_NOPE:]
        v_ref[...] = vals.astype(BF16)
        vt_ref[...] = vals.T.astype(BF16)

    return pl.pallas_call(
        body, name="kvpath", grid=(s // tm,),
        in_specs=[_rows(tm, KV_LAT_PAD), _whole((1, KV_RANK)), _whole((KV_RANK, MLA_HEADS * (QK_NOPE + V_DIM))),
                  _rows(tm, LANE), _rows(tm, LANE)],
        out_specs=[_rows(tm, KV_RANK), _rows(tm, MLA_HEADS * HEAD_PAD), _rows(tm, MLA_WIDTH),
                   pl.BlockSpec((MLA_WIDTH, tm), lambda i: (0, i))],
        out_shape=[jax.ShapeDtypeStruct((s, KV_RANK), BF16), jax.ShapeDtypeStruct((s, MLA_HEADS * HEAD_PAD), BF16),
                   jax.ShapeDtypeStruct((s, MLA_WIDTH), BF16), jax.ShapeDtypeStruct((MLA_WIDTH, s), BF16)],
        compiler_params=_cparams("parallel"),
    )(kv_lat, g_kv, w_kvb_p, cos, sin)


def _causal_mask(t):
    row = lax.broadcasted_iota(jnp.int32, (t, t), 0)
    col = lax.broadcasted_iota(jnp.int32, (t, t), 1)
    return row, col


def _attn_fwd(q, k, vt):
    s = q.shape[0]
    t = min(ATTN_TILE, s)
    nq = s // t

    def body(q_ref, k_ref, vt_ref, o_ref, lse_ref, m_sc, l_sc, acc_sc, sa_sc, sb_sc):
        i = pl.program_id(1)
        qv = q_ref[...]
        m_sc[...] = jnp.full(m_sc.shape, -jnp.inf, F32)
        l_sc[...] = jnp.zeros(l_sc.shape, F32)
        acc_sc[...] = jnp.zeros(acc_sc.shape, F32)

        def scores(j, s_ref):
            s_ref[...] = _nt(k_ref[pl.ds(pl.multiple_of(j * t, t), t), :], qv)

        def update(s_ref, j, masked):
            vt = vt_ref[:, pl.ds(pl.multiple_of(j * t, t), t)]
            sc = s_ref[...]
            if masked:
                row, col = _causal_mask(t)
                sc = jnp.where(row <= col, sc, -jnp.inf)
            m_prev = m_sc[...]
            m_new = jnp.maximum(m_prev, jnp.max(sc, axis=0, keepdims=True))
            alpha = jnp.exp2(m_prev - m_new)
            p = jnp.exp2(sc - m_new)
            l_sc[...] = alpha * l_sc[...] + jnp.sum(p, axis=0, keepdims=True)
            acc_sc[...] = alpha * acc_sc[...] + _nn(vt, p.astype(BF16))
            m_sc[...] = m_new

        def run(j0, count):
            bufs = (sa_sc, sb_sc)
            for u in range(count):
                scores(j0 + u + 1, bufs[(u + 1) % 2])
                update(bufs[u % 2], j0 + u, False)

        scores(0, sa_sc)
        done = 0
        for group in ATTN_UNROLLS:
            def body_(g, carry, base=done, group=group):
                run(base + group * g, group)
                return carry

            n_groups = lax.div(i - done, group)
            lax.fori_loop(0, n_groups, body_, 0)
            done = done + group * n_groups
        odd = lax.rem(i, 2)

        @pl.when(odd == 1)
        def _():
            scores(i, sb_sc)
            update(sa_sc, i - 1, False)
            update(sb_sc, i, True)

        @pl.when(odd == 0)
        def _():
            update(sa_sc, i, True)

        l = l_sc[...]
        o_ref[...] = (acc_sc[...] / l).T
        lse_ref[0] = m_sc[...] + jnp.log2(l)

    return pl.pallas_call(
        body, name="attn_fwd", grid=(MLA_HEADS, nq),
        in_specs=[pl.BlockSpec((t, HEAD_PAD), lambda h, i: (i, h)),
                  pl.BlockSpec((s, HEAD_PAD), lambda h, i: (0, h)),
                  pl.BlockSpec((V_DIM, s), lambda h, i: (h, 0))],
        out_specs=[pl.BlockSpec((t, V_DIM), lambda h, i: (i, h)), pl.BlockSpec((1, 1, t), lambda h, i: (h, 0, i))],
        out_shape=[jax.ShapeDtypeStruct((s, MLA_WIDTH), F32), jax.ShapeDtypeStruct((MLA_HEADS, 1, s), F32)],
        scratch_shapes=[pltpu.VMEM((1, t), F32), pltpu.VMEM((1, t), F32), pltpu.VMEM((V_DIM, t), F32),
                        pltpu.VMEM((t, t), F32), pltpu.VMEM((t, t), F32)],
        compiler_params=_cparams("parallel", "arbitrary"),
    )(q, k, vt)


def _attn_bwd(q, k, v, dot, lse_row, delta_row):
    s = q.shape[0]
    t = min(ATTN_TILE, s)
    nq = s // t

    def body(q_hbm, k_ref, v_ref, do_hbm, lse_ref, dl_ref, dk_ref, dv_ref, dq_hbm,
             dq_sc, dk_sc, dv_sc, sa_sc, sb_sc, pa_sc, pb_sc, sem, stage_sc, q_buf, do_buf, fetch_sems):
        h = pl.program_id(0)
        j = pl.program_id(1)
        kv_ = k_ref[...]
        vv = v_ref[...]
        kt = kv_.astype(F32).T.astype(BF16)
        slot = lax.rem(h, 2)

        def fetch(hh, sl):
            return (pltpu.make_async_copy(q_hbm.at[:, pl.ds(pl.multiple_of(hh * HEAD_PAD, HEAD_PAD), HEAD_PAD)],
                                          q_buf.at[sl], fetch_sems.at[sl, 0]),
                    pltpu.make_async_copy(do_hbm.at[pl.ds(pl.multiple_of(hh * V_DIM, V_DIM), V_DIM), :],
                                          do_buf.at[sl], fetch_sems.at[sl, 1]))

        @pl.when(j == 0)
        def _():
            @pl.when(h == 0)
            def _():
                for cp in fetch(0, 0):
                    cp.start()

            for cp in fetch(h, slot):
                cp.wait()

            @pl.when(h + 1 < MLA_HEADS)
            def _():
                for cp in fetch(h + 1, 1 - slot):
                    cp.start()

            dq_sc[...] = jnp.zeros(dq_sc.shape, F32)

        dk_sc[...] = jnp.zeros(dk_sc.shape, F32)
        dv_sc[...] = jnp.zeros(dv_sc.shape, F32)

        def q_rows(off):
            return q_buf[slot, pl.ds(off, t), :]

        def do_cols(off):
            return do_buf[slot, :, pl.ds(off, t)]

        def scores(i, s_ref, p_ref):
            off = pl.multiple_of(i * t, t)
            s_ref[...] = _nt(kv_, q_rows(off))
            p_ref[...] = _nn(vv, do_cols(off))

        def update(i, s_ref, p_ref, masked):
            off = pl.multiple_of(i * t, t)
            qv = q_rows(off)
            sct = s_ref[...]
            if masked:
                row, col = _causal_mask(t)
                sct = jnp.where(row <= col, sct, -jnp.inf)
            pt = jnp.exp2(sct - lse_ref[0, :, pl.ds(off, t)])
            gt = (pt * (p_ref[...] - dl_ref[0, :, pl.ds(off, t)])).astype(BF16)
            dv_sc[...] += _nt(do_cols(off), pt.astype(BF16))
            dk_sc[...] += _nn(gt, qv)
            dq_sc[:, pl.ds(off, t)] += _nn(kt, gt)

        rest = nq - 1 - j
        scores(j, sa_sc, pa_sc)

        @pl.when(rest >= 1)
        def _():
            scores(j + 1, sb_sc, pb_sc)

        update(j, sa_sc, pa_sc, True)

        def run(i0, count):
            bufs = ((sb_sc, pb_sc), (sa_sc, pa_sc))
            for u in range(count):
                scores(i0 + u + 1, *bufs[(u + 1) % 2])
                update(i0 + u, *bufs[u % 2], False)

        i1, left = j + 1, rest
        for group in ATTN_UNROLLS:
            def body_(g, carry, base=i1, group=group):
                run(base + group * g, group)
                return carry

            n_groups = jnp.where(left >= 1, lax.div(left - 1, group), 0)
            lax.fori_loop(0, n_groups, body_, 0)
            i1 = i1 + group * n_groups
            left = left - group * n_groups

        @pl.when(left == 1)
        def _():
            update(i1, sb_sc, pb_sc, False)

        @pl.when(left == 2)
        def _():
            scores(i1 + 1, sa_sc, pa_sc)
            update(i1, sb_sc, pb_sc, False)
            update(i1 + 1, sa_sc, pa_sc, False)

        dk_ref[...] = (dk_sc[...] * LN2).astype(BF16)
        dv_ref[...] = dv_sc[...].T.astype(BF16)

        def out_copy(jj):
            rows = pl.ds(pl.multiple_of(jj * t, t), t)
            return pltpu.make_async_copy(stage_sc, dq_hbm.at[h, rows, :], sem)

        @pl.when(j > 0)
        def _():
            out_copy(j - 1).wait()

        stage_sc[...] = dq_sc[:, pl.ds(pl.multiple_of(j * t, t), t)].T.astype(BF16)
        out_copy(j).start()

        @pl.when(j == nq - 1)
        def _():
            out_copy(j).wait()

    return pl.pallas_call(
        body, name="attn_bwd", grid=(MLA_HEADS, nq),
        in_specs=[pl.BlockSpec(memory_space=pl.ANY),
                  pl.BlockSpec((t, HEAD_PAD), lambda h, j: (j, h)),
                  pl.BlockSpec((t, V_DIM), lambda h, j: (j, h)),
                  pl.BlockSpec(memory_space=pl.ANY),
                  pl.BlockSpec((1, 1, s), lambda h, j: (h, 0, 0)),
                  pl.BlockSpec((1, 1, s), lambda h, j: (h, 0, 0))],
        out_specs=[pl.BlockSpec((t, HEAD_PAD), lambda h, j: (j, h)), pl.BlockSpec((t, V_DIM), lambda h, j: (j, h)),
                   pl.BlockSpec(memory_space=pl.ANY)],
        out_shape=[jax.ShapeDtypeStruct((s, MLA_HEADS * HEAD_PAD), BF16), jax.ShapeDtypeStruct((s, MLA_WIDTH), BF16),
                   jax.ShapeDtypeStruct((MLA_HEADS, s, HEAD_PAD), BF16)],
        scratch_shapes=[pltpu.VMEM((HEAD_PAD, s), F32), pltpu.VMEM((t, HEAD_PAD), F32), pltpu.VMEM((V_DIM, t), F32),
                        pltpu.VMEM((t, t), F32), pltpu.VMEM((t, t), F32), pltpu.VMEM((t, t), F32),
                        pltpu.VMEM((t, t), F32), pltpu.SemaphoreType.DMA, pltpu.VMEM((t, HEAD_PAD), BF16),
                        pltpu.VMEM((2, s, HEAD_PAD), BF16), pltpu.VMEM((2, V_DIM, s), BF16),
                        pltpu.SemaphoreType.DMA((2, 2))],
        compiler_params=_cparams("arbitrary", "arbitrary"),
    )(q, k, v, dot, lse_row, delta_row)


HALO = 8


def _silu(z):
    return z * _sigmoid(z)


def _silu_grad(z):
    sg = _sigmoid(z)
    return sg * (1.0 + z * (1.0 - sg))


def _softplus(x):
    e = jnp.exp(-jnp.abs(x))
    small = e * (1.0 - e * (0.5 - e * (1.0 / 3.0)))
    return jnp.maximum(x, 0.0) + jnp.where(e < 1e-3, small, jnp.log(1.0 + e))


def _conv_taps(xe_ref, w, tm, first):
    acc = None
    for k in range(CONV_K):
        term = xe_ref[pl.ds(HALO + first - (CONV_K - 1) + k, tm), :] * w[k:k + 1, :]
        acc = term if acc is None else acc + term
    return acc


def _ssd_pre(xbc_raw, dt_raw, conv_w, conv_b, dt_bias_p):
    s = xbc_raw.shape[0]
    tm = ROW_TILE
    hb = tm // HALO

    def body(x_ref, prev_ref, dtr_ref, w_ref, b_ref, db_ref, act_ref, dt_ref, xe_sc):
        i = pl.program_id(0)
        xe_sc[pl.ds(0, HALO), :] = jnp.where(i > 0, prev_ref[...], 0.0)
        xe_sc[pl.ds(HALO, tm), :] = x_ref[...]
        pre = _conv_taps(xe_sc, w_ref[...], tm, 0) + b_ref[...]
        act_ref[...] = _silu(pre)
        dt_ref[...] = _softplus(dtr_ref[...] + db_ref[...])

    return pl.pallas_call(
        body, name="ssd_pre", grid=(s // tm,),
        in_specs=[_rows(tm, CONV_CH), pl.BlockSpec((HALO, CONV_CH), lambda i: (jnp.maximum(i * hb - 1, 0), 0)),
                  _rows(tm, LANE), _whole((CONV_K, CONV_CH)), _whole((1, CONV_CH)), _whole((1, LANE))],
        out_specs=[_rows(tm, CONV_CH), _rows(tm, LANE)],
        out_shape=[jax.ShapeDtypeStruct((s, CONV_CH), F32), jax.ShapeDtypeStruct((s, LANE), F32)],
        scratch_shapes=[pltpu.VMEM((tm + HALO, CONV_CH), F32)],
        compiler_params=_cparams("parallel"),
    )(xbc_raw, xbc_raw, dt_raw, conv_w, conv_b, dt_bias_p)


def _split3(a):
    a1 = a.astype(BF16)
    r1 = a - a1.astype(F32)
    a2 = r1.astype(BF16)
    a3 = (r1 - a2.astype(F32)).astype(BF16)
    return a1, a2, a3


def _tri_left(tri, a):
    a1, a2, a3 = _split3(a)
    return _nn(tri, a1) + _nn(tri, a2) + _nn(tri, a3)


def _tri_right(a, tri):
    a1, a2, a3 = _split3(a)
    return _nn(a1, tri) + _nn(a2, tri) + _nn(a3, tri)


def _pair_sel(lane_lo, col_a, col_b):
    return jnp.where(lane_lo, col_a, col_b)


def _chunk_common(dt, a_neg, tril, triu):
    a = dt * a_neg
    lam_c = _tri_left(tril, a)
    lam_r = _tri_right(a.T, triu)
    lam_last = lam_c[CHUNK - 1:CHUNK, :]
    return lam_c, lam_r, lam_last


def _gated_norm_fwd(y, z, g):
    hf = y * _silu(z)
    outs = []
    for grp in range(SSM_GROUPS):
        w = SSM_WIDTH // SSM_GROUPS
        n, _ = _rms(hf[:, grp * w:(grp + 1) * w])
        outs.append(n)
    return jnp.concatenate(outs, axis=1) * g


def _ssd_fwd(xbc, dt, z, a_neg, dskip_x, g_x, tril, triu):
    s = xbc.shape[0]
    tm = min(SSD_ROWS, s)
    cpb = tm // CHUNK
    nc = s // CHUNK

    def body(xbc_ref, dt_ref, z_ref, a_ref, dsk_ref, g_ref, tril_ref, triu_ref, y_ref, o_ref, hin_ref, h_sc):
        @pl.when(pl.program_id(0) == 0)
        def _():
            h_sc[...] = jnp.zeros(h_sc.shape, F32)

        tril, triu = tril_ref[...], triu_ref[...]
        ltri = tril > 0
        lane_lo = _lane_iota((CHUNK, LANE)) < SSM_P

        def chunk(c, carry):
            r0 = pl.multiple_of(c * CHUNK, CHUNK)
            dtc = dt_ref[pl.ds(r0, CHUNK), :]
            lam_c, lam_r, lam_last = _chunk_common(dtc, a_ref[...], tril, triu)
            e_c = jnp.exp(lam_c)
            f_r = jnp.exp(lam_r[:, CHUNK - 1:CHUNK] - lam_r)
            cd = jnp.exp(lam_last)
            for grp in range(SSM_GROUPS):
                bo = SSM_WIDTH + grp * SSM_N
                co = SSM_WIDTH + SSM_GROUPS * SSM_N + grp * SSM_N
                bm = xbc_ref[pl.ds(r0, CHUNK), bo:bo + SSM_N]
                cm = xbc_ref[pl.ds(r0, CHUNK), co:co + SSM_N]
                cm_b = cm.astype(BF16)
                gmat = _nt(cm_b, bm.astype(BF16))
                bt = bm.T
                for pj in range(SSM_HEADS // SSM_GROUPS // 2):
                    ha = grp * (SSM_HEADS // SSM_GROUPS) + 2 * pj
                    hb_ = ha + 1
                    lo = ha * SSM_P
                    xs = xbc_ref[pl.ds(r0, CHUNK), lo:lo + LANE]
                    x2 = xs * _pair_sel(lane_lo, dtc[:, ha:ha + 1], dtc[:, hb_:hb_ + 1])
                    x2b = x2.astype(BF16)
                    ys, sts = [], []
                    for hh in (ha, hb_):
                        seg = lam_c[:, hh:hh + 1] - lam_r[hh:hh + 1, :]
                        dec = jnp.exp(jnp.where(ltri, seg, -jnp.inf))
                        ys.append(_nn((gmat * dec).astype(BF16), x2b))
                        sts.append(_nn((bt * f_r[hh:hh + 1, :]).astype(BF16), x2b))
                    hp = h_sc[:, lo:lo + LANE]
                    hin_ref[c, :, lo:lo + LANE] = hp
                    zz = _nn(cm_b, hp.astype(BF16))
                    e2 = _pair_sel(lane_lo, e_c[:, ha:ha + 1], e_c[:, hb_:hb_ + 1])
                    yv = jnp.where(lane_lo, ys[0], ys[1]) + e2 * zz
                    y_ref[pl.ds(r0, CHUNK), lo:lo + LANE] = yv + xs * dsk_ref[:, lo:lo + LANE]
                    cd2 = _pair_sel(lane_lo, cd[:, ha:ha + 1], cd[:, hb_:hb_ + 1])
                    h_sc[:, lo:lo + LANE] = hp * cd2 + jnp.where(lane_lo, sts[0], sts[1])
            return carry

        lax.fori_loop(0, cpb, chunk, 0)
        o_ref[...] = _gated_norm_fwd(y_ref[...], z_ref[...], g_ref[...])

    return pl.pallas_call(
        body, name="ssd_fwd", grid=(s // tm,),
        in_specs=[_rows(tm, CONV_CH), _rows(tm, LANE), _rows(tm, SSM_WIDTH), _whole((1, LANE)),
                  _whole((1, SSM_WIDTH)), _whole((1, SSM_WIDTH)), _whole((CHUNK, CHUNK)), _whole((CHUNK, CHUNK))],
        out_specs=[_rows(tm, SSM_WIDTH), _rows(tm, SSM_WIDTH),
                   pl.BlockSpec((cpb, SSM_N, SSM_WIDTH), lambda i: (i, 0, 0))],
        out_shape=[jax.ShapeDtypeStruct((s, SSM_WIDTH), F32), jax.ShapeDtypeStruct((s, SSM_WIDTH), F32),
                   jax.ShapeDtypeStruct((nc, SSM_N, SSM_WIDTH), F32)],
        scratch_shapes=[pltpu.VMEM((SSM_N, SSM_WIDTH), F32)],
        compiler_params=_cparams("arbitrary"),
    )(xbc, dt, z, a_neg, dskip_x, g_x, tril, triu)


def _outln(o, z_attn, o_ssm, w_out, x, gate, ln_g, ln_b, tgt):
    s = x.shape[0]
    tm = min(ROW_TILE_WIDE, s)

    def body(o_ref, z_ref, os_ref, w_ref, x_ref, gate_ref, g_ref, b_ref, t_ref,
             cat_ref, dmix_ref, gx_ref, do_ref, dz_ref, dl_ref, dos_ref, loss_ref, dg_ref, db_ref, dgate_ref):
        ov, zv = o_ref[...], z_ref[...]
        sz = _silu(zv)
        cat_ref[:, :MLA_WIDTH] = (ov * sz).astype(BF16)
        cat_ref[:, MLA_WIDTH:] = os_ref[...].astype(BF16)
        w = w_ref[...]
        mixed = _nn(cat_ref[...], w)
        gate_v = gate_ref[...]
        hv = DEEPNORM_ALPHA * x_ref[...] + gate_v * mixed
        mu = jnp.mean(hv, axis=-1, keepdims=True)
        hc = hv - mu
        rstd = lax.rsqrt(jnp.mean(hc * hc, axis=-1, keepdims=True) + LN_EPS)
        xhat = hc * rstd
        g = g_ref[...]
        err = xhat * g + b_ref[...] - t_ref[...]
        _acc_rows(loss_ref, jnp.full((1, LANE), (0.5 / D_MODEL) * jnp.sum(err * err), F32))
        dy = err * (1.0 / D_MODEL)
        _acc_rows(dg_ref, _colsum(dy * xhat))
        _acc_rows(db_ref, _colsum(dy))
        dxhat = dy * g
        dh = rstd * (dxhat - jnp.mean(dxhat, axis=-1, keepdims=True)
                     - xhat * jnp.mean(dxhat * xhat, axis=-1, keepdims=True))
        gx_ref[...] = DEEPNORM_ALPHA * dh
        _acc_rows(dgate_ref, _colsum(dh * mixed))
        dmix = (gate_v * dh).astype(BF16)
        dmix_ref[...] = dmix
        dcat = _nt(dmix, w)
        da = dcat[:, :MLA_WIDTH]
        dos_ref[...] = dcat[:, MLA_WIDTH:]
        dov = da * sz
        do_ref[...] = dov.T.astype(BF16)
        dz_ref[...] = da * ov * _silu_grad(zv)
        prod = dov * ov
        for h in range(MLA_HEADS):
            dsum = jnp.sum(prod[:, h * V_DIM:(h + 1) * V_DIM], axis=1, keepdims=True)
            dl_ref[h] = jnp.broadcast_to(dsum, (tm, LANE)).T[0:1, :]

    vec = _whole((1, D_MODEL))
    return pl.pallas_call(
        body, name="outln", grid=(s // tm,),
        in_specs=[_rows(tm, MLA_WIDTH), _rows(tm, MLA_WIDTH), _rows(tm, SSM_WIDTH), _whole((MIX_WIDTH, D_MODEL)),
                  _rows(tm, D_MODEL), vec, vec, vec, _rows(tm, D_MODEL)],
        out_specs=[_rows(tm, MIX_WIDTH), _rows(tm, D_MODEL), _rows(tm, D_MODEL),
                   pl.BlockSpec((MLA_WIDTH, tm), lambda i: (0, i)),
                   _rows(tm, MLA_WIDTH), pl.BlockSpec((MLA_HEADS, 1, tm), lambda i: (0, 0, i)), _rows(tm, SSM_WIDTH),
                   _whole((1, LANE)), vec, vec, vec],
        out_shape=[jax.ShapeDtypeStruct((s, MIX_WIDTH), BF16), jax.ShapeDtypeStruct((s, D_MODEL), BF16),
                   jax.ShapeDtypeStruct((s, D_MODEL), F32), jax.ShapeDtypeStruct((MLA_WIDTH, s), BF16),
                   jax.ShapeDtypeStruct((s, MLA_WIDTH), F32), jax.ShapeDtypeStruct((MLA_HEADS, 1, s), F32),
                   jax.ShapeDtypeStruct((s, SSM_WIDTH), F32), jax.ShapeDtypeStruct((1, LANE), F32),
                   jax.ShapeDtypeStruct((1, D_MODEL), F32), jax.ShapeDtypeStruct((1, D_MODEL), F32),
                   jax.ShapeDtypeStruct((1, D_MODEL), F32)],
        compiler_params=_cparams("arbitrary"),
    )(o, z_attn, o_ssm, w_out, x, gate, ln_g, ln_b, tgt)


def _ssd_bwd(dos, y, z, xbc, dt, hin, a_neg, dskip_x, g_x, tril, triu, expand):
    s = xbc.shape[0]
    tm = min(SSD_ROWS, s)
    cpb = tm // CHUNK
    nb = s // tm
    gw = SSM_WIDTH // SSM_GROUPS
    hpg = SSM_HEADS // SSM_GROUPS

    def body(dos_ref, y_ref, z_ref, xbc_ref, dt_ref, hin_ref, a_ref, dsk_ref, g_ref, tril_ref, triu_ref, exp_ref,
             dxbc_ref, ddt_ref, dz_ref, dg_ref, ddsk_ref, da_ref, dh_sc, dy_sc):
        @pl.when(pl.program_id(0) == 0)
        def _():
            dh_sc[...] = jnp.zeros(dh_sc.shape, F32)

        yv, zv, dov = y_ref[...], z_ref[...], dos_ref[...]
        sz = _silu(zv)
        hf = yv * sz
        gv = g_ref[...]
        dgs, dhfs = [], []
        for grp in range(SSM_GROUPS):
            sl = slice(grp * gw, (grp + 1) * gw)
            n, rstd = _rms(hf[:, sl])
            dgs.append(_colsum(dov[:, sl] * n))
            dhfs.append(_rms_bwd(dov[:, sl] * gv[:, sl], n, rstd))
        dhf = jnp.concatenate(dhfs, axis=1)
        _acc_rows(dg_ref, jnp.concatenate(dgs, axis=1))
        dy_sc[...] = dhf * sz
        dz_ref[...] = dhf * yv * _silu_grad(zv)

        tril, triu, expand = tril_ref[...], triu_ref[...], exp_ref[...]
        ltri = tril > 0
        utri = triu > 0
        lane = _lane_iota((CHUNK, LANE))
        lane1 = _lane_iota((1, LANE))
        lane_lo = lane < SSM_P
        row_last = lax.broadcasted_iota(jnp.int32, (CHUNK, LANE), 0) == CHUNK - 1
        a_neg_v = a_ref[...]

        def chunk(ci, carry):
            dsk_acc, da_acc = carry
            cl = cpb - 1 - ci
            r0 = pl.multiple_of(cl * CHUNK, CHUNK)
            rows = pl.ds(r0, CHUNK)
            dtc = dt_ref[rows, :]
            lam_c, lam_r, lam_last = _chunk_common(dtc, a_neg_v, tril, triu)
            e_c = jnp.exp(lam_c)
            f_c = jnp.exp(lam_last - lam_c)
            cd = jnp.exp(lam_last)
            dt_x, e_x, f_x = _tri_right(dtc, expand), _tri_right(e_c, expand), _tri_right(f_c, expand)
            cd_x = _tri_right(jnp.broadcast_to(cd, (HALO, LANE)), expand)[0:1, :]
            dlam = jnp.zeros((CHUNK, LANE), F32)
            dlast = jnp.zeros((1, LANE), F32)
            ddt_x = jnp.zeros((CHUNK, LANE), F32)
            dsk_parts = []
            for grp in range(SSM_GROUPS):
                bo = SSM_WIDTH + grp * SSM_N
                co = SSM_WIDTH + SSM_GROUPS * SSM_N + grp * SSM_N
                bm = xbc_ref[rows, bo:bo + SSM_N]
                cm = xbc_ref[rows, co:co + SSM_N]
                bm_b, cm_b = bm.astype(BF16), cm.astype(BF16)
                gmat = _nt(cm_b, bm_b)
                gmat_t = _nt(bm_b, cm_b)
                ct_b = cm.T.astype(BF16)
                acc_dg = jnp.zeros((CHUNK, CHUNK), F32)
                acc_dgt = jnp.zeros((CHUNK, CHUNK), F32)
                d_b = jnp.zeros((CHUNK, SSM_N), F32)
                d_c = jnp.zeros((CHUNK, SSM_N), F32)
                for pj in range(hpg // 2):
                    ha = grp * hpg + 2 * pj
                    hb_ = ha + 1
                    lo = ha * SSM_P
                    blk = slice(lo, lo + LANE)
                    xs = xbc_ref[rows, blk]
                    dt2, e2, f2, cd2 = dt_x[:, blk], e_x[:, blk], f_x[:, blk], cd_x[:, blk]
                    x2 = xs * dt2
                    x2b = x2.astype(BF16)
                    dy2 = dy_sc[rows, blk]
                    dy2b = dy2.astype(BF16)
                    hp = hin_ref[cl, :, blk]
                    hp_b = hp.astype(BF16)
                    dhn = dh_sc[:, blk]
                    dhn_b = dhn.astype(BF16)
                    yo = e2 * _nn(cm_b, hp_b)
                    dzz_b = (e2 * dy2).astype(BF16)
                    d_c = d_c + _nt(dzz_b, hp_b)
                    dh_sc[:, blk] = _nn(ct_b, dzz_b) + cd2 * dhn
                    dxs2 = f2 * _nn(bm_b, dhn_b)
                    d_b = d_b + _nt((f2 * x2).astype(BF16), dhn_b)
                    xd = x2 * dxs2
                    t_lam = dy2 * yo - xd
                    t_last = cd2 * (dhn * hp) + xd
                    dxd2 = jnp.zeros((CHUNK, LANE), F32)
                    heads = ((ha, lane_lo), (hb_, jnp.logical_not(lane_lo)))
                    for hh, msk in heads:
                        x2h_b = jnp.where(msk, x2, 0.0).astype(BF16)
                        dy2h_b = jnp.where(msk, dy2, 0.0).astype(BF16)
                        seg = lam_c[:, hh:hh + 1] - lam_r[hh:hh + 1, :]
                        dec = jnp.exp(jnp.where(ltri, seg, -jnp.inf))
                        dect = jnp.exp(jnp.where(utri, -seg, -jnp.inf))
                        dmd = _nt(dy2h_b, x2b) * dec
                        dmtd = _nt(x2h_b, dy2b) * dect
                        acc_dg = acc_dg + dmd
                        acc_dgt = acc_dgt + dmtd
                        dlam_h = jnp.sum(dmd * gmat - dmtd * gmat_t + jnp.where(msk, t_lam, 0.0), axis=1, keepdims=True)
                        last_h = jnp.sum(jnp.sum(jnp.where(msk, t_last, 0.0), axis=0, keepdims=True), axis=1, keepdims=True)
                        dlam = jnp.where(lane == hh, dlam_h, dlam)
                        dlast = jnp.where(lane1 == hh, last_h, dlast)
                        dxd2 = jnp.where(msk, _nn((gmat_t * dect).astype(BF16), dy2b), dxd2)
                    dx2 = dxd2 + dxs2
                    dxbc_ref[rows, blk] = dx2 * dt2 + dy2 * dsk_ref[:, blk]
                    prod = dx2 * xs
                    for hh, msk in heads:
                        col = jnp.sum(jnp.where(msk, prod, 0.0), axis=1, keepdims=True)
                        ddt_x = jnp.where(lane == hh, col, ddt_x)
                    dsk_parts.append(_colsum(dy2 * xs))
                d_c = d_c + _nn(acc_dg.astype(BF16), bm_b)
                d_b = d_b + _nn(acc_dgt.astype(BF16), cm_b)
                dxbc_ref[rows, bo:bo + SSM_N] = d_b
                dxbc_ref[rows, co:co + SSM_N] = d_c
            dlam = dlam + jnp.where(row_last, dlast, 0.0)
            da = _tri_left(triu, dlam)
            ddt_ref[rows, :] = da * a_neg_v + ddt_x
            return dsk_acc + jnp.concatenate(dsk_parts, axis=1), da_acc + _colsum(da * dtc)

        dsk_tot, da_tot = lax.fori_loop(
            0, cpb, chunk, (jnp.zeros((1, SSM_WIDTH), F32), jnp.zeros((1, LANE), F32)))
        _acc_rows(ddsk_ref, dsk_tot)
        _acc_rows(da_ref, da_tot)

    rev = lambda i: (nb - 1 - i, 0)
    rrows = lambda w: pl.BlockSpec((tm, w), rev)
    return pl.pallas_call(
        body, name="ssd_bwd", grid=(nb,),
        in_specs=[rrows(SSM_WIDTH), rrows(SSM_WIDTH), rrows(SSM_WIDTH), rrows(CONV_CH), rrows(LANE),
                  pl.BlockSpec((cpb, SSM_N, SSM_WIDTH), lambda i: (nb - 1 - i, 0, 0)),
                  _whole((1, LANE)), _whole((1, SSM_WIDTH)), _whole((1, SSM_WIDTH)),
                  _whole((CHUNK, CHUNK)), _whole((CHUNK, CHUNK)), _whole((LANE, SSM_WIDTH))],
        out_specs=[rrows(CONV_CH), rrows(LANE), rrows(SSM_WIDTH),
                   _whole((1, SSM_WIDTH)), _whole((1, SSM_WIDTH)), _whole((1, LANE))],
        out_shape=[jax.ShapeDtypeStruct((s, CONV_CH), F32), jax.ShapeDtypeStruct((s, LANE), F32),
                   jax.ShapeDtypeStruct((s, SSM_WIDTH), F32), jax.ShapeDtypeStruct((1, SSM_WIDTH), F32),
                   jax.ShapeDtypeStruct((1, SSM_WIDTH), F32), jax.ShapeDtypeStruct((1, LANE), F32)],
        scratch_shapes=[pltpu.VMEM((SSM_N, SSM_WIDTH), F32), pltpu.VMEM((tm, SSM_WIDTH), F32)],
        compiler_params=_cparams("arbitrary"),
    )(dos, y, z, xbc, dt, hin, a_neg, dskip_x, g_x, tril, triu, expand)


def _ssd_post_bwd(xbc_raw, dxa, ddt, dt_raw, conv_w, conv_b, dt_bias_p):
    s = xbc_raw.shape[0]
    tm = ROW_TILE
    hb = tm // HALO
    nt = s // tm
    ext = tm + HALO

    def body(x_ref, prev_ref, next_ref, d_ref, dnext_ref, ddt_ref, dtr_ref, w_ref, b_ref, db_ref,
             dx_ref, ddtr_ref, dw_ref, dcb_ref, ddb_ref, xe_sc, de_sc):
        i = pl.program_id(0)
        w = w_ref[...]
        xe_sc[pl.ds(0, HALO), :] = jnp.where(i > 0, prev_ref[...], 0.0)
        xe_sc[pl.ds(HALO, tm), :] = x_ref[...]
        xe_sc[pl.ds(HALO + tm, HALO), :] = next_ref[...]
        pre = _conv_taps(xe_sc, w, ext, 0) + b_ref[...]
        sg = _silu_grad(pre)
        de_sc[pl.ds(0, tm), :] = d_ref[...] * sg[:tm]
        de_sc[pl.ds(tm, HALO), :] = jnp.where(i < nt - 1, dnext_ref[...] * sg[tm:], 0.0)
        dconv = de_sc[pl.ds(0, tm), :]
        acc = None
        dws = []
        for k in range(CONV_K):
            term = de_sc[pl.ds(CONV_K - 1 - k, tm), :] * w[k:k + 1, :]
            acc = term if acc is None else acc + term
            dws.append(_colsum(dconv * xe_sc[pl.ds(HALO - (CONV_K - 1) + k, tm), :]))
        dx_ref[...] = acc
        _acc_rows(dw_ref, jnp.concatenate(dws, axis=0))
        _acc_rows(dcb_ref, _colsum(dconv))
        ddtr = ddt_ref[...] * _sigmoid(dtr_ref[...] + db_ref[...])
        ddtr_ref[...] = ddtr
        _acc_rows(ddb_ref, _colsum(ddtr))

    halo_prev = pl.BlockSpec((HALO, CONV_CH), lambda i: (jnp.maximum(i * hb - 1, 0), 0))
    halo_next = pl.BlockSpec((HALO, CONV_CH), lambda i: (jnp.minimum((i + 1) * hb, s // HALO - 1), 0))
    return pl.pallas_call(
        body, name="ssd_post_bwd", grid=(nt,),
        in_specs=[_rows(tm, CONV_CH), halo_prev, halo_next, _rows(tm, CONV_CH), halo_next, _rows(tm, LANE),
                  _rows(tm, LANE), _whole((CONV_K, CONV_CH)), _whole((1, CONV_CH)), _whole((1, LANE))],
        out_specs=[_rows(tm, CONV_CH), _rows(tm, LANE), _whole((CONV_K, CONV_CH)), _whole((1, CONV_CH)),
                   _whole((1, LANE))],
        out_shape=[jax.ShapeDtypeStruct((s, CONV_CH), F32), jax.ShapeDtypeStruct((s, LANE), F32),
                   jax.ShapeDtypeStruct((CONV_K, CONV_CH), F32), jax.ShapeDtypeStruct((1, CONV_CH), F32),
                   jax.ShapeDtypeStruct((1, LANE), F32)],
        scratch_shapes=[pltpu.VMEM((tm + 2 * HALO, CONV_CH), F32), pltpu.VMEM((ext, CONV_CH), F32)],
        compiler_params=_cparams("arbitrary"),
    )(xbc_raw, xbc_raw, xbc_raw, dxa, dxa, ddt, dt_raw, conv_w, conv_b, dt_bias_p)


def _qbwd(dq_att, q_lat, g_q, w_qb_p, cos, sin):
    s = q_lat.shape[0]
    tm = ROW_TILE
    wq = MLA_HEADS * HEAD_PAD

    def body(dq_ref, ql_ref, g_ref, w_ref, cos_ref, sin_ref, dql_ref, draw_ref, dg_ref):
        c, sn = cos_ref[...], sin_ref[...]
        for h in range(MLA_HEADS):
            o = h * HEAD_PAD
            dqh = dq_ref[h].astype(F32) * ATTN_SCALE
            draw_ref[:, o:o + QK_NOPE] = dqh[:, :QK_NOPE].astype(BF16)
            draw_ref[:, o + QK_NOPE:o + HEAD_PAD] = _rope_transposed(dqh[:, QK_NOPE:], c, sn).astype(BF16)
        dn = _nt(draw_ref[...], w_ref[...])
        xhat, rstd = _rms(ql_ref[...])
        _acc_rows(dg_ref, _colsum(dn * xhat))
        dql_ref[...] = _rms_bwd(dn * g_ref[...], xhat, rstd)

    return pl.pallas_call(
        body, name="qbwd", grid=(s // tm,),
        in_specs=[pl.BlockSpec((MLA_HEADS, tm, HEAD_PAD), lambda i: (0, i, 0)), _rows(tm, Q_RANK), _whole((1, Q_RANK)),
                  _whole((Q_RANK, wq)), _rows(tm, LANE), _rows(tm, LANE)],
        out_specs=[_rows(tm, Q_RANK), _rows(tm, wq), _whole((1, Q_RANK))],
        out_shape=[jax.ShapeDtypeStruct((s, Q_RANK), F32), jax.ShapeDtypeStruct((s, wq), BF16),
                   jax.ShapeDtypeStruct((1, Q_RANK), F32)],
        compiler_params=_cparams("arbitrary"),
    )(dq_att, q_lat, g_q, w_qb_p, cos, sin)


def _kvbwd(dk_att, dv, kv_lat, g_kv, w_kvb_p, cos, sin):
    s = kv_lat.shape[0]
    tm = ROW_TILE
    wk = MLA_HEADS * HEAD_PAD
    wr = MLA_HEADS * (QK_NOPE + V_DIM)

    def body(dk_ref, dv_ref, kl_ref, g_ref, w_ref, cos_ref, sin_ref, dkl_ref, draw_ref, dg_ref):
        dkr = None
        for h in range(MLA_HEADS):
            o = h * HEAD_PAD
            draw_ref[:, h * QK_NOPE:(h + 1) * QK_NOPE] = dk_ref[:, o:o + QK_NOPE].astype(BF16)
            part = dk_ref[:, o + QK_NOPE:o + HEAD_PAD].astype(F32)
            dkr = part if dkr is None else dkr + part
        draw_ref[:, MLA_HEADS * QK_NOPE:] = dv_ref[...].astype(BF16)
        dn = _nt(draw_ref[...], w_ref[...])
        xhat, rstd = _rms(kl_ref[:, :KV_RANK])
        _acc_rows(dg_ref, _colsum(dn * xhat))
        dkl_ref[:, :KV_RANK] = _rms_bwd(dn * g_ref[...], xhat, rstd)
        dkl_ref[:, KV_RANK:] = _rope_transposed(dkr, cos_ref[...], sin_ref[...])

    return pl.pallas_call(
        body, name="kvbwd", grid=(s // tm,),
        in_specs=[_rows(tm, wk), _rows(tm, MLA_WIDTH), _rows(tm, KV_LAT_PAD), _whole((1, KV_RANK)),
                  _whole((KV_RANK, wr)), _rows(tm, LANE), _rows(tm, LANE)],
        out_specs=[_rows(tm, KV_LAT_PAD), _rows(tm, wr), _whole((1, KV_RANK))],
        out_shape=[jax.ShapeDtypeStruct((s, KV_LAT_PAD), F32), jax.ShapeDtypeStruct((s, wr), BF16),
                   jax.ShapeDtypeStruct((1, KV_RANK), F32)],
        compiler_params=_cparams("arbitrary"),
    )(dk_att, dv, kv_lat, g_kv, w_kvb_p, cos, sin)


def _inproj_bwd(pieces, w_in_pt, x, scale1p, gx1):
    s = x.shape[0]
    tm = min(ROW_TILE, s)

    def body(*refs):
        p_refs = refs[:len(IN_PAD)]
        w_ref, x_ref, sc_ref, gx1_ref, gx_ref, dp_ref, dsc_ref, dsh_ref = refs[len(IN_PAD):]
        off = 0
        for ref, w in zip(p_refs, IN_PAD):
            dp_ref[:, off:off + w] = ref[...].astype(BF16)
            off += w
        du = _nn(dp_ref[...], w_ref[...])
        gx_ref[...] = gx1_ref[...] + du * sc_ref[...]
        _acc_rows(dsc_ref, _colsum(du * x_ref[...]))
        _acc_rows(dsh_ref, _colsum(du))

    vec = _whole((1, D_MODEL))
    return pl.pallas_call(
        body, name="inproj_bwd", grid=(s // tm,),
        in_specs=[_rows(tm, w) for w in IN_PAD] + [_whole_once((IN_PAD_WIDTH, D_MODEL)), _rows(tm, D_MODEL), vec,
                                                    _rows(tm, D_MODEL)],
        out_specs=[_rows(tm, D_MODEL), _rows(tm, IN_PAD_WIDTH), vec, vec],
        out_shape=[jax.ShapeDtypeStruct((s, D_MODEL), F32), jax.ShapeDtypeStruct((s, IN_PAD_WIDTH), BF16),
                   jax.ShapeDtypeStruct((1, D_MODEL), F32), jax.ShapeDtypeStruct((1, D_MODEL), F32)],
        compiler_params=_cparams("arbitrary"),
    )(*pieces, w_in_pt, x, scale1p, gx1)


def _matmul_tn_rows(name, a, b, tk):
    s, k = a.shape
    n = b.shape[1]
    tm = min(GRAD_ROWS, s)

    def body(a_ref, b_ref, o_ref):
        @pl.when(pl.program_id(1) == 0)
        def _():
            o_ref[...] = jnp.zeros_like(o_ref)
        o_ref[...] += _tn(a_ref[...], b_ref[...])

    return pl.pallas_call(
        body, name=name, grid=(k // tk, s // tm),
        in_specs=[pl.BlockSpec((tm, tk), lambda j, i: (i, j)), pl.BlockSpec((tm, n), lambda j, i: (i, 0))],
        out_specs=pl.BlockSpec((tk, n), lambda j, i: (j, 0)),
        out_shape=jax.ShapeDtypeStruct((k, n), F32),
        compiler_params=_cparams("parallel", "arbitrary"),
    )(a, b)


def _matmul_tn(name, a, b, tn):
    s, k = a.shape
    n = b.shape[1]
    tm = min(GRAD_ROWS, s)

    def body(a_ref, b_ref, o_ref):
        @pl.when(pl.program_id(1) == 0)
        def _():
            o_ref[...] = jnp.zeros_like(o_ref)
        o_ref[...] += _tn(a_ref[...], b_ref[...])

    return pl.pallas_call(
        body, name=name, grid=(n // tn, s // tm),
        in_specs=[pl.BlockSpec((tm, k), lambda j, i: (i, 0)), pl.BlockSpec((tm, tn), lambda j, i: (i, j))],
        out_specs=pl.BlockSpec((k, tn), lambda j, i: (0, j)),
        out_shape=jax.ShapeDtypeStruct((k, n), F32),
        compiler_params=_cparams("parallel", "arbitrary"),
    )(a, b)


def _pack_w_in_t(w_in_t):
    parts, off = [], 0
    for w, wp in zip(IN_SPLITS, IN_PAD):
        parts.append(jnp.pad(w_in_t[off:off + w], ((0, wp - w), (0, 0))))
        off += w
    return jnp.concatenate(parts, axis=0)


def _unpack_w_in_t(g):
    parts, off = [], 0
    for w, wp in zip(IN_SPLITS, IN_PAD):
        parts.append(g[off:off + w])
        off += wp
    return jnp.concatenate(parts, axis=0)


def _rope_tables(positions):
    inv_freq = 1.0 / (ROPE_THETA ** (jnp.arange(ROPE_HALF, dtype=F32) / ROPE_HALF))
    ang = positions.astype(F32)[:, None] * inv_freq
    cos, sin = jnp.cos(ang), jnp.sin(ang)
    zeros = jnp.zeros((positions.shape[0], LANE - QK_ROPE), F32)
    return jnp.concatenate([cos, cos, zeros], axis=1), jnp.concatenate([-sin, sin, zeros], axis=1)


def _local_step(x, tgt, positions, mod, w_in_t, q_norm_g, w_qb_p, kv_norm_g, w_kvb_p, conv_w, conv_b, dt_bias,
                a_log, d_skip, ssm_norm_g, w_out_b, ln_g, ln_b):
    row = lambda v: v.reshape(1, -1)
    shift, scale, gate = mod[:D_MODEL], mod[D_MODEL:2 * D_MODEL], mod[2 * D_MODEL:]
    scale1p = row(1.0 + scale)
    w_in_p = _pack_w_in_t(w_in_t)
    cos, sin = _rope_tables(positions)
    a_neg = row(jnp.pad(-jnp.exp(a_log), (0, LANE - SSM_HEADS)))
    dskip_x = row(jnp.repeat(d_skip, SSM_P))
    dt_bias_p = row(jnp.pad(dt_bias, (0, LANE - SSM_HEADS)))
    tri = jnp.tril(jnp.ones((CHUNK, CHUNK), F32))
    tril, triu = tri.astype(BF16), tri.T.astype(BF16)

    u_bf, q_lat, kv_lat, z_attn, xbc_raw, dt_raw, z_ssm = _inproj(x, scale1p, row(shift), w_in_p)
    nq_bf, q_att = _qpath(q_lat, row(q_norm_g), w_qb_p, cos, sin)
    nkv_bf, k_att, v_att, vt_att = _kvpath(kv_lat, row(kv_norm_g), w_kvb_p, cos, sin)
    o, lse_rows = _attn_fwd(q_att, k_att, vt_att)
    xbc, dt = _ssd_pre(xbc_raw, dt_raw, conv_w, row(conv_b), dt_bias_p)
    expand = jnp.repeat(jnp.eye(LANE, SSM_HEADS, dtype=BF16), SSM_P, axis=1)
    y, o_ssm, hin = _ssd_fwd(xbc, dt, z_ssm, a_neg, dskip_x, row(ssm_norm_g), tril, triu)
    (cat_bf, dmix_bf, gx1, do_t, dz_attn, delta_rows, dos, loss, d_ln_g, d_ln_b, d_gate) = _outln(
        o, z_attn, o_ssm, w_out_b, x, row(gate), row(ln_g), row(ln_b), tgt)

    g_w_out = _matmul_tn("gw_out", cat_bf, dmix_bf, 512)
    dk_att, dv, dq_att = _attn_bwd(q_att, k_att, v_att, do_t, lse_rows, delta_rows)
    dq_lat, dqraw_bf, d_q_norm_g = _qbwd(dq_att, q_lat, row(q_norm_g), w_qb_p, cos, sin)
    dkv_lat, dkvraw_bf, d_kv_norm_g = _kvbwd(dk_att, dv, kv_lat, row(kv_norm_g), w_kvb_p, cos, sin)
    g_w_qb = _matmul_tn("gw_qb", nq_bf, dqraw_bf, MLA_HEADS * HEAD_PAD)
    g_w_kvb = _matmul_tn("gw_kvb", nkv_bf, dkvraw_bf, MLA_HEADS * (QK_NOPE + V_DIM))
    dxa, ddt, dz_ssm, d_ssm_g, ddsk_x, d_a = _ssd_bwd(dos, y, z_ssm, xbc, dt, hin, a_neg, dskip_x, row(ssm_norm_g),
                                                       tril, triu, expand)
    dxbc_raw, ddt_raw, d_conv_w, d_conv_b, d_dt_bias = _ssd_post_bwd(xbc_raw, dxa, ddt, dt_raw, conv_w, row(conv_b),
                                                                     dt_bias_p)
    grad_x, dproj_bf, d_scale, d_shift = _inproj_bwd((dq_lat, dkv_lat, dz_attn, dxbc_raw, ddt_raw, dz_ssm),
                                                     w_in_p, x, scale1p, gx1)
    g_w_in_t = _unpack_w_in_t(_matmul_tn_rows("gw_in", dproj_bf, u_bf, 896))
    return dict(
        loss=loss[0, 0], grad_x=grad_x,
        dmod=jnp.concatenate([d_shift[0], d_scale[0], d_gate[0]]),
        w_in_t=g_w_in_t, q_norm_g=d_q_norm_g[0], w_qb=g_w_qb, kv_norm_g=d_kv_norm_g[0], w_kvb=g_w_kvb,
        conv_w=d_conv_w, conv_b=d_conv_b[0], dt_bias=d_dt_bias[0, :SSM_HEADS],
        a_log=d_a[0, :SSM_HEADS] * a_neg[0, :SSM_HEADS],
        d_skip=ddsk_x.reshape(SSM_HEADS, SSM_P).sum(axis=1), ssm_norm_g=d_ssm_g[0], w_out=g_w_out,
        ln_g=d_ln_g[0], ln_b=d_ln_b[0])


ADAM_ROWS = 512


def _my_index():
    return 4 * lax.axis_index("x") + 2 * lax.axis_index("y") + lax.axis_index("c")


def _exchange(name, sends, gather):
    n = len(sends)
    peers = N_DEV - 1

    def body(*refs):
        send_refs, recv_refs = refs[:n], refs[n:2 * n]
        send_sems, recv_sems, local_sems = refs[2 * n:]
        x, y, c = lax.axis_index("x"), lax.axis_index("y"), lax.axis_index("c")
        me = 4 * x + 2 * y + c

        def src(a, idx):
            return send_refs[a] if gather else send_refs[a].at[idx]

        owns = [pltpu.make_async_copy(src(a, me), recv_refs[a].at[me], local_sems.at[a]) for a in range(n)]
        for cp in owns:
            cp.start()
        copies = []
        for k in range(1, N_DEV):
            px, py, pc = x ^ ((k >> 2) & 1), y ^ ((k >> 1) & 1), c ^ (k & 1)
            peer = 4 * px + 2 * py + pc
            for a in range(n):
                copies.append(pltpu.make_async_remote_copy(
                    src_ref=src(a, peer), dst_ref=recv_refs[a].at[me],
                    send_sem=send_sems.at[a * peers + k - 1], recv_sem=recv_sems.at[a * peers + k - 1],
                    device_id=(px, py, pc), device_id_type=pl.DeviceIdType.MESH))
        for cp in copies:
            cp.start()
        for cp in copies:
            cp.wait()
        for cp in owns:
            cp.wait()

    block_shape = lambda a: a.shape if gather else a.shape[1:]
    return pl.pallas_call(
        body, name=name,
        in_specs=[pl.BlockSpec(memory_space=pl.ANY)] * n, out_specs=[pl.BlockSpec(memory_space=pl.ANY)] * n,
        out_shape=[jax.ShapeDtypeStruct((N_DEV, *block_shape(a)), a.dtype) for a in sends],
        scratch_shapes=[pltpu.SemaphoreType.DMA((n * peers,)), pltpu.SemaphoreType.DMA((n * peers,)),
                        pltpu.SemaphoreType.DMA((n,))],
    )(*sends)


def _gather_two_level(name, sends):
    n = len(sends)
    per = N_DEV - 1

    def body(*refs):
        send_refs, recv_refs = refs[:n], refs[n:2 * n]
        send_sems, recv_sems, local_sems = refs[2 * n:]
        x, y, c = lax.axis_index("x"), lax.axis_index("y"), lax.axis_index("c")
        sibling = (x, y, 1 - c)
        chips = [(1 - x, y), (x, 1 - y), (1 - x, 1 - y)]

        def idx(px, py, pc):
            return 4 * px + 2 * py + pc

        def copy(a, k, block, to, src=None):
            slot = recv_refs[a].at[idx(*block)]
            return pltpu.make_async_remote_copy(
                src_ref=slot if src is None else src, dst_ref=slot,
                send_sem=send_sems.at[a * per + k], recv_sem=recv_sems.at[a * per + k],
                device_id=to, device_id_type=pl.DeviceIdType.MESH)

        me = (x, y, c)
        owns = [pltpu.make_async_copy(send_refs[a], recv_refs[a].at[idx(*me)], local_sems.at[a]) for a in range(n)]
        for cp in owns:
            cp.start()
        first = [copy(a, 0, me, sibling, src=send_refs[a]) for a in range(n)]
        first += [copy(a, 1 + j, me, (*chip, c), src=send_refs[a]) for j, chip in enumerate(chips) for a in range(n)]
        for cp in first:
            cp.start()
        passed = []
        for j, chip in enumerate(chips):
            for a in range(n):
                copy(a, 1 + j, (*chip, c), me).wait_recv()
                fwd = copy(a, 4 + j, (*chip, c), sibling)
                fwd.start()
                passed.append(fwd)
        for a in range(n):
            copy(a, 0, sibling, me).wait_recv()
            for j, chip in enumerate(chips):
                copy(a, 4 + j, (*chip, 1 - c), me).wait_recv()
        for cp in first + passed:
            cp.wait_send()
        for cp in owns:
            cp.wait()

    return pl.pallas_call(
        body, name=name,
        in_specs=[pl.BlockSpec(memory_space=pl.ANY)] * n, out_specs=[pl.BlockSpec(memory_space=pl.ANY)] * n,
        out_shape=[jax.ShapeDtypeStruct((N_DEV, *a.shape), a.dtype) for a in sends],
        scratch_shapes=[pltpu.SemaphoreType.DMA((n * per,)), pltpu.SemaphoreType.DMA((n * per,)),
                        pltpu.SemaphoreType.DMA((n,))],
    )(*sends)


def _flat_rows(parts, row_multiple):
    flat = jnp.concatenate([p.reshape(-1) for p in parts])
    chunk = row_multiple * LANE
    total = -(-flat.shape[0] // chunk) * chunk
    return jnp.pad(flat, (0, total - flat.shape[0])).reshape(-1, LANE)


def _unflat(flat, shapes):
    flat = flat.reshape(-1)
    out, off = [], 0
    for shp in shapes:
        n = math.prod(shp)
        out.append(flat[off:off + n].reshape(shp))
        off += n
    return out


def _adam_update(g, w, m, v):
    m2 = ADAM_B1 * m + (1.0 - ADAM_B1) * g
    v2 = ADAM_B2 * v + (1.0 - ADAM_B2) * (g * g)
    m_hat = m2 / (1.0 - ADAM_B1 ** ADAM_STEP)
    v_hat = v2 / (1.0 - ADAM_B2 ** ADAM_STEP)
    delta = -ADAM_LR * (m_hat / (jnp.sqrt(v_hat) + ADAM_EPS) + ADAM_WD * w)
    return delta, m2, v2


def _adamw_summed(name, parts, w, m, v):
    r = w.shape[0]
    tr = min(ADAM_ROWS, r)

    def body(p_ref, w_ref, m_ref, v_ref, g_ref, d_ref, m2_ref, v2_ref):
        g = p_ref[0]
        for j in range(1, N_DEV):
            g = g + p_ref[j]
        g_ref[...] = g
        d_ref[...], m2_ref[...], v2_ref[...] = _adam_update(g, w_ref[...], m_ref[...], v_ref[...])

    rows = _rows(tr, LANE)
    return pl.pallas_call(
        body, name=name, grid=(r // tr,),
        in_specs=[pl.BlockSpec((N_DEV, tr, LANE), lambda i: (0, i, 0)), rows, rows, rows],
        out_specs=[rows] * 4, out_shape=[jax.ShapeDtypeStruct((r, LANE), F32)] * 4,
        compiler_params=_cparams("parallel"),
    )(parts, w, m, v)


def _modpart(c_all, w_ada, b_cols):
    def body(c_ref, w_ref, b_ref, o_ref):
        o_ref[...] = _nn(c_ref[...].astype(BF16), w_ref[...].astype(BF16)) + b_ref[...]

    return pl.pallas_call(
        body, name="modpart", out_shape=jax.ShapeDtypeStruct((N_DEV, w_ada.shape[1]), F32),
    )(c_all, w_ada, b_cols)


def _adamw_w_ada(c_all_t, dmod_cols, w, m, v):
    def body(c_ref, d_ref, w_ref, m_ref, v_ref, g_ref, dl_ref, m2_ref, v2_ref):
        g = c_ref[:, 0:1] * d_ref[0:1, :]
        for b in range(1, N_DEV):
            g = g + c_ref[:, b:b + 1] * d_ref[b:b + 1, :]
        g_ref[...] = g
        dl_ref[...], m2_ref[...], v2_ref[...] = _adam_update(g, w_ref[...], m_ref[...], v_ref[...])

    return pl.pallas_call(
        body, name="adamw_w_ada", out_shape=[jax.ShapeDtypeStruct(w.shape, F32)] * 4,
        compiler_params=pltpu.CompilerParams(vmem_limit_bytes=VMEM_LIMIT),
    )(c_all_t, dmod_cols, w, m, v)


W_IN_SHARD = IN_WIDTH // N_DEV
W_IN_SHARD_LANES = -(-W_IN_SHARD // LANE) * LANE
BF16_ROWS = 16
W_IN_SEND_ROWS = -(-W_IN_SHARD // BF16_ROWS) * BF16_ROWS


def _transpose_cast(w_pad):
    def body(w_ref, o_ref):
        o_ref[...] = w_ref[...].T.astype(BF16)

    return pl.pallas_call(
        body, name="w_in_transpose", out_shape=jax.ShapeDtypeStruct(w_pad.shape[::-1], BF16),
        compiler_params=pltpu.CompilerParams(vmem_limit_bytes=VMEM_LIMIT),
    )(w_pad)


def _adamw_w_in(parts, w, m, v):
    rows_t = parts.shape[1]
    d, cols = w.shape
    tb = ROW_TILE

    def body(p_ref, w_ref, m_ref, v_ref, g_ref, d_ref, m2_ref, v2_ref):
        gt = p_ref[0].astype(F32)
        for j in range(1, N_DEV):
            gt = gt + p_ref[j].astype(F32)
        gt = jnp.concatenate([gt, jnp.zeros((W_IN_SHARD_LANES - rows_t, tb), F32)], axis=0)
        g = gt.T[:, :cols]
        g_ref[...] = g
        d_ref[...], m2_ref[...], v2_ref[...] = _adam_update(g, w_ref[...], m_ref[...], v_ref[...])

    blk = _rows(tb, cols)
    return pl.pallas_call(
        body, name="adamw_w_in", grid=(d // tb,),
        in_specs=[pl.BlockSpec((N_DEV, rows_t, tb), lambda i: (0, 0, i)), blk, blk, blk],
        out_specs=[blk] * 4, out_shape=[jax.ShapeDtypeStruct(w.shape, F32)] * 4,
        compiler_params=_cparams("parallel"),
    )(parts, w, m, v)


SHARDED = ("w_qb", "w_kvb", "w_out")
REPLICATED = ("b_ada", "q_norm_g", "kv_norm_g", "conv_b", "dt_bias", "a_log", "d_skip", "ssm_norm_g", "ln_g", "ln_b")
WEIGHTS = ("w_ada", "b_ada", "w_in", "q_norm_g", "w_qb", "kv_norm_g", "w_kvb", "conv_w", "conv_b", "dt_bias",
           "a_log", "d_skip", "ssm_norm_g", "w_out", "ln_g", "ln_b")
HEAD_COLS = QK_NOPE + V_DIM


def _adamw_blocks(name, parts, w, m, v):
    r, c = w.shape
    tr = ROW_TILE if r % ROW_TILE == 0 else r

    def body(p_ref, w_ref, m_ref, v_ref, g_ref, d_ref, m2_ref, v2_ref):
        g = p_ref[0].astype(F32)
        for j in range(1, N_DEV):
            g = g + p_ref[j].astype(F32)
        g_ref[...] = g
        d_ref[...], m2_ref[...], v2_ref[...] = _adam_update(g, w_ref[...], m_ref[...], v_ref[...])

    blk = _rows(tr, c)
    return pl.pallas_call(
        body, name=name, grid=(r // tr,),
        in_specs=[pl.BlockSpec((N_DEV, tr, c), lambda i: (0, i, 0)), blk, blk, blk],
        out_specs=[blk] * 4, out_shape=[jax.ShapeDtypeStruct(w.shape, F32)] * 4,
        compiler_params=_cparams("parallel"),
    )(parts, w, m, v)


def kernel(x, c, positions, w_ada, b_ada, w_in, q_norm_g, w_qb, kv_norm_g, w_kvb, conv_w, conv_b, dt_bias, a_log, d_skip, ssm_norm_g, w_out, ln_g, ln_b, loss_target, m_w_ada, m_b_ada, m_w_in, m_q_norm_g, m_w_qb, m_kv_norm_g, m_w_kvb, m_conv_w, m_conv_b, m_dt_bias, m_a_log, m_d_skip, m_ssm_norm_g, m_w_out, m_ln_g, m_ln_b, v_w_ada, v_b_ada, v_w_in, v_q_norm_g, v_w_qb, v_kv_norm_g, v_w_kvb, v_conv_w, v_conv_b, v_dt_bias, v_a_log, v_d_skip, v_ssm_norm_g, v_w_out, v_ln_g, v_ln_b):
    given = dict(w_ada=w_ada, b_ada=b_ada, w_in=w_in, q_norm_g=q_norm_g, w_qb=w_qb, kv_norm_g=kv_norm_g, w_kvb=w_kvb,
                 conv_w=conv_w, conv_b=conv_b, dt_bias=dt_bias, a_log=a_log, d_skip=d_skip, ssm_norm_g=ssm_norm_g,
                 w_out=w_out, ln_g=ln_g, ln_b=ln_b)
    mom = dict(w_ada=m_w_ada, b_ada=m_b_ada, w_in=m_w_in, q_norm_g=m_q_norm_g, w_qb=m_w_qb, kv_norm_g=m_kv_norm_g,
               w_kvb=m_w_kvb, conv_w=m_conv_w, conv_b=m_conv_b, dt_bias=m_dt_bias, a_log=m_a_log, d_skip=m_d_skip,
               ssm_norm_g=m_ssm_norm_g, w_out=m_w_out, ln_g=m_ln_g, ln_b=m_ln_b)
    var = dict(w_ada=v_w_ada, b_ada=v_b_ada, w_in=v_w_in, q_norm_g=v_q_norm_g, w_qb=v_w_qb, kv_norm_g=v_kv_norm_g,
               w_kvb=v_w_kvb, conv_w=v_conv_w, conv_b=v_conv_b, dt_bias=v_dt_bias, a_log=v_a_log, d_skip=v_d_skip,
               ssm_norm_g=v_ssm_norm_g, w_out=v_w_out, ln_g=v_ln_g, ln_b=v_ln_b)
    w0 = {k: a[0] for k, a in given.items()}
    m0 = {k: a[0] for k, a in mom.items()}
    v0 = {k: a[0] for k, a in var.items()}
    me = _my_index()

    w_in_rows = _transpose_cast(jnp.pad(w0["w_in"], ((0, 0), (0, W_IN_SHARD_LANES - W_IN_SHARD))))
    g_w_in, g_w_qb, g_w_kvb, g_w_out, g_conv_w, c_all = _gather_two_level(
        "gather_weights", [w_in_rows] + [w0[k].astype(BF16) for k in SHARDED] + [w0["conv_w"], c])
    c_all = c_all.reshape(N_DEV, D_MODEL)
    w_in_t = g_w_in[:, :W_IN_SHARD, :].reshape(IN_WIDTH, D_MODEL)
    w_qb_p = jnp.pad(g_w_qb, ((0, 0), (0, 0), (0, HEAD_PAD - QK_HEAD))).transpose(1, 0, 2).reshape(Q_RANK, -1)
    w_kvb_p = g_w_kvb.reshape(N_DEV, KV_RANK, 2, QK_NOPE).transpose(1, 2, 0, 3).reshape(KV_RANK, -1)
    w_out_b = g_w_out.reshape(MIX_WIDTH, D_MODEL)
    conv_w_full = g_conv_w.transpose(1, 0, 2).reshape(CONV_K, CONV_CH)

    ada_cols = w0["w_ada"].shape[1]
    b_cols = lax.dynamic_slice(w0["b_ada"], (me * ada_cols,), (ada_cols,)).reshape(1, ada_cols)
    mod_all, = _exchange("gather_mod", [_modpart(c_all, w0["w_ada"], b_cols)], gather=True)
    mod = lax.dynamic_index_in_dim(mod_all, me, axis=1, keepdims=False).reshape(-1)

    loc = _local_step(x[0], loss_target[0], positions[0], mod, w_in_t, w0["q_norm_g"], w_qb_p,
                      w0["kv_norm_g"], w_kvb_p, conv_w_full, w0["conv_b"], w0["dt_bias"], w0["a_log"],
                      w0["d_skip"], w0["ssm_norm_g"], w_out_b, w0["ln_g"], w0["ln_b"])

    rep_shapes = [w0[k].shape for k in REPLICATED] + [(1,)]
    rep_local = [loc["dmod"]] + [loc[k] for k in REPLICATED[1:]] + [loc["loss"].reshape(1)]
    rep_parts, conv_parts = _exchange("gather_small", [_flat_rows(rep_local, HALO), loc["conv_w"]], gather=True)
    conv_cols = w0["conv_w"].shape[1]
    conv_mine = lax.dynamic_slice(conv_parts, (0, 0, me * conv_cols), (N_DEV, CONV_K, conv_cols))
    outs = {"conv_w": _adamw_blocks("adamw_conv_w", conv_mine, w0["conv_w"], m0["conv_w"], v0["conv_w"])}
    zero1 = jnp.zeros((1,), F32)
    rep = _adamw_summed("adamw_replicated", rep_parts,
                        _flat_rows([w0[k] for k in REPLICATED] + [zero1], HALO),
                        _flat_rows([m0[k] for k in REPLICATED] + [zero1], HALO),
                        _flat_rows([v0[k] for k in REPLICATED] + [zero1], HALO))
    rep_g, rep_d, rep_m, rep_v = [_unflat(a, rep_shapes) for a in rep]
    loss = rep_g[-1][0]

    dmod_all = rep_parts.reshape(N_DEV, -1)[:, :3 * D_MODEL]
    dmod_cols = lax.dynamic_slice(dmod_all, (0, me * ada_cols), (N_DEV, ada_cols))
    outs["w_ada"] = _adamw_w_ada(c_all.T, dmod_cols, w0["w_ada"], m0["w_ada"], v0["w_ada"])

    send_w_in = loc["w_in_t"].astype(BF16).reshape(N_DEV, W_IN_SHARD, D_MODEL)
    send_w_in = jnp.pad(send_w_in, ((0, 0), (0, W_IN_SEND_ROWS - W_IN_SHARD), (0, 0)))
    send_w_qb = loc["w_qb"].astype(BF16).reshape(Q_RANK, N_DEV, HEAD_PAD)[:, :, :QK_HEAD].transpose(1, 0, 2)
    send_w_kvb = loc["w_kvb"].astype(BF16).reshape(KV_RANK, 2, N_DEV, QK_NOPE).transpose(2, 0, 1, 3)
    send_w_kvb = send_w_kvb.reshape(N_DEV, KV_RANK, HEAD_COLS)
    send_w_out = loc["w_out"].astype(BF16).reshape(N_DEV, MIX_WIDTH // N_DEV, D_MODEL)
    r_w_in, r_w_qb, r_w_kvb, r_w_out = _exchange(
        "scatter_grads", [send_w_in, send_w_qb, send_w_kvb, send_w_out], gather=False)
    outs["w_in"] = _adamw_w_in(r_w_in, w0["w_in"], m0["w_in"], v0["w_in"])
    for k, parts in zip(SHARDED, (r_w_qb, r_w_kvb, r_w_out)):
        outs[k] = _adamw_blocks("adamw_" + k, parts, w0[k], m0[k], v0[k])

    def collect(idx):
        out = {k: o[idx] for k, o in outs.items()}
        out.update({k: (rep_g, rep_d, rep_m, rep_v)[idx][i] for i, k in enumerate(REPLICATED)})
        return [out[k][None] for k in WEIGHTS]

    return (loss, loc["grad_x"][None], *collect(0), *collect(1), *collect(2), *collect(3))
```

```python
import math

import jax
import jax.numpy as jnp
from jax import lax
from jax.experimental import pallas as pl
from jax.experimental.pallas import tpu as pltpu

F32 = jnp.float32
BF16 = jnp.bfloat16

N_DEV = 8
D_MODEL = 1024
MLA_HEADS = 8
QK_NOPE = 128
QK_ROPE = 64
V_DIM = 128
Q_RANK = 384
KV_RANK = 256
QK_HEAD = QK_NOPE + QK_ROPE
HEAD_PAD = 256
ROPE_HALF = QK_ROPE // 2
ROPE_THETA = 10000.0
MLA_WIDTH = MLA_HEADS * V_DIM
SSM_HEADS = 16
SSM_P = 64
SSM_WIDTH = SSM_HEADS * SSM_P
SSM_GROUPS = 2
SSM_N = 128
CONV_K = 4
CHUNK = 128
CONV_CH = SSM_WIDTH + 2 * SSM_GROUPS * SSM_N
MIX_WIDTH = MLA_WIDTH + SSM_WIDTH
IN_SPLITS = (Q_RANK, KV_RANK + QK_ROPE, MLA_WIDTH, CONV_CH, SSM_HEADS, SSM_WIDTH)
IN_WIDTH = sum(IN_SPLITS)
LANE = 128
KV_LAT_PAD = KV_RANK + LANE
IN_PAD = (Q_RANK, KV_LAT_PAD, MLA_WIDTH, CONV_CH, LANE, SSM_WIDTH)
IN_PAD_WIDTH = sum(IN_PAD)
DEEPNORM_ALPHA = 2.0 ** 0.25
RMS_EPS = 1e-6
LN_EPS = 1e-5
ATTN_SCALE = QK_HEAD ** -0.5
LOG2E = math.log2(math.e)
LN2 = math.log(2.0)
Q_PRESCALE = ATTN_SCALE * LOG2E
ADAM_LR, ADAM_B1, ADAM_B2, ADAM_EPS, ADAM_WD, ADAM_STEP = 0.001, 0.9, 0.999, 1e-08, 0.01, 10

ROW_TILE = 512
ROW_TILE_WIDE = 256
ATTN_TILE = 512
ATTN_UNROLLS = (16, 8, 4, 2)
SSD_ROWS = 512
GRAD_ROWS = 2048
VMEM_LIMIT = 56 * 1024 * 1024


def _nn(a, b):
    return jnp.dot(a, b, preferred_element_type=F32)


def _nt(a, b):
    return lax.dot_general(a, b, (((1,), (1,)), ((), ())), preferred_element_type=F32)


def _tn(a, b):
    return lax.dot_general(a, b, (((0,), (0,)), ((), ())), preferred_element_type=F32)


def _cparams(*sem):
    return pltpu.CompilerParams(dimension_semantics=sem, vmem_limit_bytes=VMEM_LIMIT)


def _rows(tm, w):
    return pl.BlockSpec((tm, w), lambda i: (i, 0))


def _whole(shape):
    return pl.BlockSpec(shape, lambda i: (0,) * len(shape))


def _whole_once(shape):
    return pl.BlockSpec(shape, lambda i: (0,) * len(shape), pipeline_mode=pl.Buffered(1))


def _sigmoid(z):
    return 1.0 / (1.0 + jnp.exp(-z))


def _lane_iota(shape):
    return lax.broadcasted_iota(jnp.int32, shape, len(shape) - 1)


def _swap_halves(r):
    lane = _lane_iota(r.shape)
    return jnp.where(lane < ROPE_HALF, pltpu.roll(r, LANE - ROPE_HALF, 1),
                     jnp.where(lane < QK_ROPE, pltpu.roll(r, ROPE_HALF, 1), 0.0))


def _rope(r, cos, sin):
    return r * cos + _swap_halves(r) * sin


def _rope_transposed(d, cos, sin):
    return d * cos + _swap_halves(d * sin)


def _rms(x):
    rstd = lax.rsqrt(jnp.mean(x * x, axis=-1, keepdims=True) + RMS_EPS)
    return x * rstd, rstd


def _rms_bwd(dxhat, xhat, rstd):
    return rstd * (dxhat - xhat * jnp.mean(dxhat * xhat, axis=-1, keepdims=True))


def _acc_rows(ref, val):
    @pl.when(pl.program_id(0) == 0)
    def _():
        ref[...] = jnp.zeros_like(ref)
    ref[...] += val


def _colsum(v):
    return jnp.sum(v, axis=0, keepdims=True)


def _inproj(x, scale1p, shift, w_in_pt):
    s = x.shape[0]
    tm = ROW_TILE

    def body(x_ref, sc_ref, sh_ref, w_ref, u_ref, *outs):
        u = (x_ref[...] * sc_ref[...] + sh_ref[...]).astype(BF16)
        u_ref[...] = u
        proj = _nt(u, w_ref[...])
        off = 0
        for ref, w in zip(outs, IN_PAD):
            ref[...] = proj[:, off:off + w]
            off += w

    return pl.pallas_call(
        body, name="inproj", grid=(s // tm,),
        in_specs=[_rows(tm, D_MODEL), _whole((1, D_MODEL)), _whole((1, D_MODEL)), _whole((IN_PAD_WIDTH, D_MODEL))],
        out_specs=[_rows(tm, D_MODEL)] + [_rows(tm, w) for w in IN_PAD],
        out_shape=[jax.ShapeDtypeStruct((s, D_MODEL), BF16)] + [jax.ShapeDtypeStruct((s, w), F32) for w in IN_PAD],
        compiler_params=_cparams("parallel"),
    )(x, scale1p, shift, w_in_pt)


def _qpath(q_lat, g_q, w_qb_p, cos, sin):
    s = q_lat.shape[0]
    tm = ROW_TILE

    def body(ql_ref, g_ref, w_ref, cos_ref, sin_ref, nq_ref, q_ref):
        xhat, _ = _rms(ql_ref[...])
        nq = (xhat * g_ref[...]).astype(BF16)
        nq_ref[...] = nq
        raw = _nn(nq, w_ref[...]) * Q_PRESCALE
        c, sn = cos_ref[...], sin_ref[...]
        for h in range(MLA_HEADS):
            o = h * HEAD_PAD
            q_ref[:, o:o + QK_NOPE] = raw[:, o:o + QK_NOPE].astype(BF16)
            q_ref[:, o + QK_NOPE:o + HEAD_PAD] = _rope(raw[:, o + QK_NOPE:o + HEAD_PAD], c, sn).astype(BF16)

    return pl.pallas_call(
        body, name="qpath", grid=(s // tm,),
        in_specs=[_rows(tm, Q_RANK), _whole((1, Q_RANK)), _whole((Q_RANK, MLA_HEADS * HEAD_PAD)),
                  _rows(tm, LANE), _rows(tm, LANE)],
        out_specs=[_rows(tm, Q_RANK), _rows(tm, MLA_HEADS * HEAD_PAD)],
        out_shape=[jax.ShapeDtypeStruct((s, Q_RANK), BF16), jax.ShapeDtypeStruct((s, MLA_HEADS * HEAD_PAD), BF16)],
        compiler_params=_cparams("parallel"),
    )(q_lat, g_q, w_qb_p, cos, sin)


def _kvpath(kv_lat, g_kv, w_kvb_p, cos, sin):
    s = kv_lat.shape[0]
    tm = ROW_TILE

    def body(kl_ref, g_ref, w_ref, cos_ref, sin_ref, nkv_ref, k_ref, v_ref, vt_ref):
        kl = kl_ref[...]
        xhat, _ = _rms(kl[:, :KV_RANK])
        nkv = (xhat * g_ref[...]).astype(BF16)
        nkv_ref[...] = nkv
        raw = _nn(nkv, w_ref[...])
        kr = _rope(kl[:, KV_RANK:], cos_ref[...], sin_ref[...]).astype(BF16)
        for h in range(MLA_HEADS):
            o = h * HEAD_PAD
            k_ref[:, o:o + QK_NOPE] = raw[:, h * QK_NOPE:(h + 1) * QK_NOPE].astype(BF16)
            k_ref[:, o + QK_NOPE:o + HEAD_PAD] = kr
        vals = raw[:, MLA_HEADS * QK_NOPE:]
        v_ref[...] = vals.astype(BF16)
        vt_ref[...] = vals.T.astype(BF16)

    return pl.pallas_call(
        body, name="kvpath", grid=(s // tm,),
        in_specs=[_rows(tm, KV_LAT_PAD), _whole((1, KV_RANK)), _whole((KV_RANK, MLA_HEADS * (QK_NOPE + V_DIM))),
                  _rows(tm, LANE), _rows(tm, LANE)],
        out_specs=[_rows(tm, KV_RANK), _rows(tm, MLA_HEADS * HEAD_PAD), _rows(tm, MLA_WIDTH),
                   pl.BlockSpec((MLA_WIDTH, tm), lambda i: (0, i))],
        out_shape=[jax.ShapeDtypeStruct((s, KV_RANK), BF16), jax.ShapeDtypeStruct((s, MLA_HEADS * HEAD_PAD), BF16),
                   jax.ShapeDtypeStruct((s, MLA_WIDTH), BF16), jax.ShapeDtypeStruct((MLA_WIDTH, s), BF16)],
        compiler_params=_cparams("parallel"),
    )(kv_lat, g_kv, w_kvb_p, cos, sin)


def _causal_mask(t):
    row = lax.broadcasted_iota(jnp.int32, (t, t), 0)
    col = lax.broadcasted_iota(jnp.int32, (t, t), 1)
    return row, col


def _attn_fwd(q, k, vt):
    s = q.shape[0]
    t = min(ATTN_TILE, s)
    nq = s // t

    def body(q_ref, k_ref, vt_ref, o_ref, lse_ref, m_sc, l_sc, acc_sc, sa_sc, sb_sc):
        i = pl.program_id(1)
        qv = q_ref[...]
        m_sc[...] = jnp.full(m_sc.shape, -jnp.inf, F32)
        l_sc[...] = jnp.zeros(l_sc.shape, F32)
        acc_sc[...] = jnp.zeros(acc_sc.shape, F32)

        def scores(j, s_ref):
            s_ref[...] = _nt(k_ref[pl.ds(pl.multiple_of(j * t, t), t), :], qv)

        def update(s_ref, j, masked):
            vt = vt_ref[:, pl.ds(pl.multiple_of(j * t, t), t)]
            sc = s_ref[...]
            if masked:
                row, col = _causal_mask(t)
                sc = jnp.where(row <= col, sc, -jnp.inf)
            m_prev = m_sc[...]
            m_new = jnp.maximum(m_prev, jnp.max(sc, axis=0, keepdims=True))
            alpha = jnp.exp2(m_prev - m_new)
            p = jnp.exp2(sc - m_new)
            l_sc[...] = alpha * l_sc[...] + jnp.sum(p, axis=0, keepdims=True)
            acc_sc[...] = alpha * acc_sc[...] + _nn(vt, p.astype(BF16))
            m_sc[...] = m_new

        def run(j0, count):
            bufs = (sa_sc, sb_sc)
            for u in range(count):
                scores(j0 + u + 1, bufs[(u + 1) % 2])
                update(bufs[u % 2], j0 + u, False)

        scores(0, sa_sc)
        done = 0
        for group in ATTN_UNROLLS:
            def body_(g, carry, base=done, group=group):
                run(base + group * g, group)
                return carry

            n_groups = lax.div(i - done, group)
            lax.fori_loop(0, n_groups, body_, 0)
            done = done + group * n_groups
        odd = lax.rem(i, 2)

        @pl.when(odd == 1)
        def _():
            scores(i, sb_sc)
            update(sa_sc, i - 1, False)
            update(sb_sc, i, True)

        @pl.when(odd == 0)
        def _():
            update(sa_sc, i, True)

        l = l_sc[...]
        o_ref[...] = (acc_sc[...] / l).T
        lse_ref[0] = m_sc[...] + jnp.log2(l)

    return pl.pallas_call(
        body, name="attn_fwd", grid=(MLA_HEADS, nq),
        in_specs=[pl.BlockSpec((t, HEAD_PAD), lambda h, i: (i, h)),
                  pl.BlockSpec((s, HEAD_PAD), lambda h, i: (0, h)),
                  pl.BlockSpec((V_DIM, s), lambda h, i: (h, 0))],
        out_specs=[pl.BlockSpec((t, V_DIM), lambda h, i: (i, h)), pl.BlockSpec((1, 1, t), lambda h, i: (h, 0, i))],
        out_shape=[jax.ShapeDtypeStruct((s, MLA_WIDTH), F32), jax.ShapeDtypeStruct((MLA_HEADS, 1, s), F32)],
        scratch_shapes=[pltpu.VMEM((1, t), F32), pltpu.VMEM((1, t), F32), pltpu.VMEM((V_DIM, t), F32),
                        pltpu.VMEM((t, t), F32), pltpu.VMEM((t, t), F32)],
        compiler_params=_cparams("parallel", "arbitrary"),
    )(q, k, vt)


def _attn_bwd(q, k, v, dot, lse_row, delta_row):
    s = q.shape[0]
    t = min(ATTN_TILE, s)
    nq = s // t

    def body(q_hbm, k_ref, v_ref, do_hbm, lse_ref, dl_ref, dk_ref, dv_ref, dq_hbm,
             dq_sc, dk_sc, dv_sc, sa_sc, sb_sc, pa_sc, pb_sc, sem, stage_sc, q_buf, do_buf, fetch_sems):
        h = pl.program_id(0)
        j = pl.program_id(1)
        kv_ = k_ref[...]
        vv = v_ref[...]
        kt = kv_.astype(F32).T.astype(BF16)
        slot = lax.rem(h, 2)

        def fetch(hh, sl):
            return (pltpu.make_async_copy(q_hbm.at[:, pl.ds(pl.multiple_of(hh * HEAD_PAD, HEAD_PAD), HEAD_PAD)],
                                          q_buf.at[sl], fetch_sems.at[sl, 0]),
                    pltpu.make_async_copy(do_hbm.at[pl.ds(pl.multiple_of(hh * V_DIM, V_DIM), V_DIM), :],
                                          do_buf.at[sl], fetch_sems.at[sl, 1]))

        @pl.when(j == 0)
        def _():
            @pl.when(h == 0)
            def _():
                for cp in fetch(0, 0):
                    cp.start()

            for cp in fetch(h, slot):
                cp.wait()

            @pl.when(h + 1 < MLA_HEADS)
            def _():
                for cp in fetch(h + 1, 1 - slot):
                    cp.start()

            dq_sc[...] = jnp.zeros(dq_sc.shape, F32)

        dk_sc[...] = jnp.zeros(dk_sc.shape, F32)
        dv_sc[...] = jnp.zeros(dv_sc.shape, F32)

        def q_rows(off):
            return q_buf[slot, pl.ds(off, t), :]

        def do_cols(off):
            return do_buf[slot, :, pl.ds(off, t)]

        def scores(i, s_ref, p_ref):
            off = pl.multiple_of(i * t, t)
            s_ref[...] = _nt(kv_, q_rows(off))
            p_ref[...] = _nn(vv, do_cols(off))

        def update(i, s_ref, p_ref, masked):
            off = pl.multiple_of(i * t, t)
            qv = q_rows(off)
            sct = s_ref[...]
            if masked:
                row, col = _causal_mask(t)
                sct = jnp.where(row <= col, sct, -jnp.inf)
            pt = jnp.exp2(sct - lse_ref[0, :, pl.ds(off, t)])
            gt = (pt * (p_ref[...] - dl_ref[0, :, pl.ds(off, t)])).astype(BF16)
            dv_sc[...] += _nt(do_cols(off), pt.astype(BF16))
            dk_sc[...] += _nn(gt, qv)
            dq_sc[:, pl.ds(off, t)] += _nn(kt, gt)

        rest = nq - 1 - j
        scores(j, sa_sc, pa_sc)

        @pl.when(rest >= 1)
        def _():
            scores(j + 1, sb_sc, pb_sc)

        update(j, sa_sc, pa_sc, True)

        def run(i0, count):
            bufs = ((sb_sc, pb_sc), (sa_sc, pa_sc))
            for u in range(count):
                scores(i0 + u + 1, *bufs[(u + 1) % 2])
                update(i0 + u, *bufs[u % 2], False)

        i1, left = j + 1, rest
        for group in ATTN_UNROLLS:
            def body_(g, carry, base=i1, group=group):
                run(base + group * g, group)
                return carry

            n_groups = jnp.where(left >= 1, lax.div(left - 1, group), 0)
            lax.fori_loop(0, n_groups, body_, 0)
            i1 = i1 + group * n_groups
            left = left - group * n_groups

        @pl.when(left == 1)
        def _():
            update(i1, sb_sc, pb_sc, False)

        @pl.when(left == 2)
        def _():
            scores(i1 + 1, sa_sc, pa_sc)
            update(i1, sb_sc, pb_sc, False)
            update(i1 + 1, sa_sc, pa_sc, False)

        dk_ref[...] = (dk_sc[...] * LN2).astype(BF16)
        dv_ref[...] = dv_sc[...].T.astype(BF16)

        def out_copy(jj):
            rows = pl.ds(pl.multiple_of(jj * t, t), t)
            return pltpu.make_async_copy(stage_sc, dq_hbm.at[h, rows, :], sem)

        @pl.when(j > 0)
        def _():
            out_copy(j - 1).wait()

        stage_sc[...] = dq_sc[:, pl.ds(pl.multiple_of(j * t, t), t)].T.astype(BF16)
        out_copy(j).start()

        @pl.when(j == nq - 1)
        def _():
            out_copy(j).wait()

    return pl.pallas_call(
        body, name="attn_bwd", grid=(MLA_HEADS, nq),
        in_specs=[pl.BlockSpec(memory_space=pl.ANY),
                  pl.BlockSpec((t, HEAD_PAD), lambda h, j: (j, h)),
                  pl.BlockSpec((t, V_DIM), lambda h, j: (j, h)),
                  pl.BlockSpec(memory_space=pl.ANY),
                  pl.BlockSpec((1, 1, s), lambda h, j: (h, 0, 0)),
                  pl.BlockSpec((1, 1, s), lambda h, j: (h, 0, 0))],
        out_specs=[pl.BlockSpec((t, HEAD_PAD), lambda h, j: (j, h)), pl.BlockSpec((t, V_DIM), lambda h, j: (j, h)),
                   pl.BlockSpec(memory_space=pl.ANY)],
        out_shape=[jax.ShapeDtypeStruct((s, MLA_HEADS * HEAD_PAD), BF16), jax.ShapeDtypeStruct((s, MLA_WIDTH), BF16),
                   jax.ShapeDtypeStruct((MLA_HEADS, s, HEAD_PAD), BF16)],
        scratch_shapes=[pltpu.VMEM((HEAD_PAD, s), F32), pltpu.VMEM((t, HEAD_PAD), F32), pltpu.VMEM((V_DIM, t), F32),
                        pltpu.VMEM((t, t), F32), pltpu.VMEM((t, t), F32), pltpu.VMEM((t, t), F32),
                        pltpu.VMEM((t, t), F32), pltpu.SemaphoreType.DMA, pltpu.VMEM((t, HEAD_PAD), BF16),
                        pltpu.VMEM((2, s, HEAD_PAD), BF16), pltpu.VMEM((2, V_DIM, s), BF16),
                        pltpu.SemaphoreType.DMA((2, 2))],
        compiler_params=_cparams("arbitrary", "arbitrary"),
    )(q, k, v, dot, lse_row, delta_row)


HALO = 8


def _silu(z):
    return z * _sigmoid(z)


def _silu_grad(z):
    sg = _sigmoid(z)
    return sg * (1.0 + z * (1.0 - sg))


def _softplus(x):
    e = jnp.exp(-jnp.abs(x))
    small = e * (1.0 - e * (0.5 - e * (1.0 / 3.0)))
    return jnp.maximum(x, 0.0) + jnp.where(e < 1e-3, small, jnp.log(1.0 + e))


def _conv_taps(xe_ref, w, tm, first):
    acc = None
    for k in range(CONV_K):
        term = xe_ref[pl.ds(HALO + first - (CONV_K - 1) + k, tm), :] * w[k:k + 1, :]
        acc = term if acc is None else acc + term
    return acc


def _ssd_pre(xbc_raw, dt_raw, conv_w, conv_b, dt_bias_p):
    s = xbc_raw.shape[0]
    tm = ROW_TILE
    hb = tm // HALO

    def body(x_ref, prev_ref, dtr_ref, w_ref, b_ref, db_ref, act_ref, dt_ref, xe_sc):
        i = pl.program_id(0)
        xe_sc[pl.ds(0, HALO), :] = jnp.where(i > 0, prev_ref[...], 0.0)
        xe_sc[pl.ds(HALO, tm), :] = x_ref[...]
        pre = _conv_taps(xe_sc, w_ref[...], tm, 0) + b_ref[...]
        act_ref[...] = _silu(pre)
        dt_ref[...] = _softplus(dtr_ref[...] + db_ref[...])

    return pl.pallas_call(
        body, name="ssd_pre", grid=(s // tm,),
        in_specs=[_rows(tm, CONV_CH), pl.BlockSpec((HALO, CONV_CH), lambda i: (jnp.maximum(i * hb - 1, 0), 0)),
                  _rows(tm, LANE), _whole((CONV_K, CONV_CH)), _whole((1, CONV_CH)), _whole((1, LANE))],
        out_specs=[_rows(tm, CONV_CH), _rows(tm, LANE)],
        out_shape=[jax.ShapeDtypeStruct((s, CONV_CH), F32), jax.ShapeDtypeStruct((s, LANE), F32)],
        scratch_shapes=[pltpu.VMEM((tm + HALO, CONV_CH), F32)],
        compiler_params=_cparams("parallel"),
    )(xbc_raw, xbc_raw, dt_raw, conv_w, conv_b, dt_bias_p)


def _split3(a):
    a1 = a.astype(BF16)
    r1 = a - a1.astype(F32)
    a2 = r1.astype(BF16)
    a3 = (r1 - a2.astype(F32)).astype(BF16)
    return a1, a2, a3


def _tri_left(tri, a):
    a1, a2, a3 = _split3(a)
    return _nn(tri, a1) + _nn(tri, a2) + _nn(tri, a3)


def _tri_right(a, tri):
    a1, a2, a3 = _split3(a)
    return _nn(a1, tri) + _nn(a2, tri) + _nn(a3, tri)


def _pair_sel(lane_lo, col_a, col_b):
    return jnp.where(lane_lo, col_a, col_b)


def _chunk_common(dt, a_neg, tril, triu):
    a = dt * a_neg
    lam_c = _tri_left(tril, a)
    lam_r = _tri_right(a.T, triu)
    lam_last = lam_c[CHUNK - 1:CHUNK, :]
    return lam_c, lam_r, lam_last


def _gated_norm_fwd(y, z, g):
    hf = y * _silu(z)
    outs = []
    for grp in range(SSM_GROUPS):
        w = SSM_WIDTH // SSM_GROUPS
        n, _ = _rms(hf[:, grp * w:(grp + 1) * w])
        outs.append(n)
    return jnp.concatenate(outs, axis=1) * g


def _ssd_fwd(xbc, dt, z, a_neg, dskip_x, g_x, tril, triu):
    s = xbc.shape[0]
    tm = min(SSD_ROWS, s)
    cpb = tm // CHUNK
    nc = s // CHUNK

    def body(xbc_ref, dt_ref, z_ref, a_ref, dsk_ref, g_ref, tril_ref, triu_ref, y_ref, o_ref, hin_ref, h_sc):
        @pl.when(pl.program_id(0) == 0)
        def _():
            h_sc[...] = jnp.zeros(h_sc.shape, F32)

        tril, triu = tril_ref[...], triu_ref[...]
        ltri = tril > 0
        lane_lo = _lane_iota((CHUNK, LANE)) < SSM_P

        def chunk(c, carry):
            r0 = pl.multiple_of(c * CHUNK, CHUNK)
            dtc = dt_ref[pl.ds(r0, CHUNK), :]
            lam_c, lam_r, lam_last = _chunk_common(dtc, a_ref[...], tril, triu)
            e_c = jnp.exp(lam_c)
            f_r = jnp.exp(lam_r[:, CHUNK - 1:CHUNK] - lam_r)
            cd = jnp.exp(lam_last)
            for grp in range(SSM_GROUPS):
                bo = SSM_WIDTH + grp * SSM_N
                co = SSM_WIDTH + SSM_GROUPS * SSM_N + grp * SSM_N
                bm = xbc_ref[pl.ds(r0, CHUNK), bo:bo + SSM_N]
                cm = xbc_ref[pl.ds(r0, CHUNK), co:co + SSM_N]
                cm_b = cm.astype(BF16)
                gmat = _nt(cm_b, bm.astype(BF16))
                bt = bm.T
                for pj in range(SSM_HEADS // SSM_GROUPS // 2):
                    ha = grp * (SSM_HEADS // SSM_GROUPS) + 2 * pj
                    hb_ = ha + 1
                    lo = ha * SSM_P
                    xs = xbc_ref[pl.ds(r0, CHUNK), lo:lo + LANE]
                    x2 = xs * _pair_sel(lane_lo, dtc[:, ha:ha + 1], dtc[:, hb_:hb_ + 1])
                    x2b = x2.astype(BF16)
                    ys, sts = [], []
                    for hh in (ha, hb_):
                        seg = lam_c[:, hh:hh + 1] - lam_r[hh:hh + 1, :]
                        dec = jnp.exp(jnp.where(ltri, seg, -jnp.inf))
                        ys.append(_nn((gmat * dec).astype(BF16), x2b))
                        sts.append(_nn((bt * f_r[hh:hh + 1, :]).astype(BF16), x2b))
                    hp = h_sc[:, lo:lo + LANE]
                    hin_ref[c, :, lo:lo + LANE] = hp
                    zz = _nn(cm_b, hp.astype(BF16))
                    e2 = _pair_sel(lane_lo, e_c[:, ha:ha + 1], e_c[:, hb_:hb_ + 1])
                    yv = jnp.where(lane_lo, ys[0], ys[1]) + e2 * zz
                    y_ref[pl.ds(r0, CHUNK), lo:lo + LANE] = yv + xs * dsk_ref[:, lo:lo + LANE]
                    cd2 = _pair_sel(lane_lo, cd[:, ha:ha + 1], cd[:, hb_:hb_ + 1])
                    h_sc[:, lo:lo + LANE] = hp * cd2 + jnp.where(lane_lo, sts[0], sts[1])
            return carry

        lax.fori_loop(0, cpb, chunk, 0)
        o_ref[...] = _gated_norm_fwd(y_ref[...], z_ref[...], g_ref[...])

    return pl.pallas_call(
        body, name="ssd_fwd", grid=(s // tm,),
        in_specs=[_rows(tm, CONV_CH), _rows(tm, LANE), _rows(tm, SSM_WIDTH), _whole((1, LANE)),
                  _whole((1, SSM_WIDTH)), _whole((1, SSM_WIDTH)), _whole((CHUNK, CHUNK)), _whole((CHUNK, CHUNK))],
        out_specs=[_rows(tm, SSM_WIDTH), _rows(tm, SSM_WIDTH),
                   pl.BlockSpec((cpb, SSM_N, SSM_WIDTH), lambda i: (i, 0, 0))],
        out_shape=[jax.ShapeDtypeStruct((s, SSM_WIDTH), F32), jax.ShapeDtypeStruct((s, SSM_WIDTH), F32),
                   jax.ShapeDtypeStruct((nc, SSM_N, SSM_WIDTH), F32)],
        scratch_shapes=[pltpu.VMEM((SSM_N, SSM_WIDTH), F32)],
        compiler_params=_cparams("arbitrary"),
    )(xbc, dt, z, a_neg, dskip_x, g_x, tril, triu)


def _outln(o, z_attn, o_ssm, w_out, x, gate, ln_g, ln_b, tgt):
    s = x.shape[0]
    tm = min(ROW_TILE_WIDE, s)

    def body(o_ref, z_ref, os_ref, w_ref, x_ref, gate_ref, g_ref, b_ref, t_ref,
             cat_ref, dmix_ref, gx_ref, do_ref, dz_ref, dl_ref, dos_ref, loss_ref, dg_ref, db_ref, dgate_ref):
        ov, zv = o_ref[...], z_ref[...]
        sz = _silu(zv)
        cat_ref[:, :MLA_WIDTH] = (ov * sz).astype(BF16)
        cat_ref[:, MLA_WIDTH:] = os_ref[...].astype(BF16)
        w = w_ref[...]
        mixed = _nn(cat_ref[...], w)
        gate_v = gate_ref[...]
        hv = DEEPNORM_ALPHA * x_ref[...] + gate_v * mixed
        mu = jnp.mean(hv, axis=-1, keepdims=True)
        hc = hv - mu
        rstd = lax.rsqrt(jnp.mean(hc * hc, axis=-1, keepdims=True) + LN_EPS)
        xhat = hc * rstd
        g = g_ref[...]
        err = xhat * g + b_ref[...] - t_ref[...]
        _acc_rows(loss_ref, jnp.full((1, LANE), (0.5 / D_MODEL) * jnp.sum(err * err), F32))
        dy = err * (1.0 / D_MODEL)
        _acc_rows(dg_ref, _colsum(dy * xhat))
        _acc_rows(db_ref, _colsum(dy))
        dxhat = dy * g
        dh = rstd * (dxhat - jnp.mean(dxhat, axis=-1, keepdims=True)
                     - xhat * jnp.mean(dxhat * xhat, axis=-1, keepdims=True))
        gx_ref[...] = DEEPNORM_ALPHA * dh
        _acc_rows(dgate_ref, _colsum(dh * mixed))
        dmix = (gate_v * dh).astype(BF16)
        dmix_ref[...] = dmix
        dcat = _nt(dmix, w)
        da = dcat[:, :MLA_WIDTH]
        dos_ref[...] = dcat[:, MLA_WIDTH:]
        dov = da * sz
        do_ref[...] = dov.T.astype(BF16)
        dz_ref[...] = da * ov * _silu_grad(zv)
        prod = dov * ov
        for h in range(MLA_HEADS):
            dsum = jnp.sum(prod[:, h * V_DIM:(h + 1) * V_DIM], axis=1, keepdims=True)
            dl_ref[h] = jnp.broadcast_to(dsum, (tm, LANE)).T[0:1, :]

    vec = _whole((1, D_MODEL))
    return pl.pallas_call(
        body, name="outln", grid=(s // tm,),
        in_specs=[_rows(tm, MLA_WIDTH), _rows(tm, MLA_WIDTH), _rows(tm, SSM_WIDTH), _whole((MIX_WIDTH, D_MODEL)),
                  _rows(tm, D_MODEL), vec, vec, vec, _rows(tm, D_MODEL)],
        out_specs=[_rows(tm, MIX_WIDTH), _rows(tm, D_MODEL), _rows(tm, D_MODEL),
                   pl.BlockSpec((MLA_WIDTH, tm), lambda i: (0, i)),
                   _rows(tm, MLA_WIDTH), pl.BlockSpec((MLA_HEADS, 1, tm), lambda i: (0, 0, i)), _rows(tm, SSM_WIDTH),
                   _whole((1, LANE)), vec, vec, vec],
        out_shape=[jax.ShapeDtypeStruct((s, MIX_WIDTH), BF16), jax.ShapeDtypeStruct((s, D_MODEL), BF16),
                   jax.ShapeDtypeStruct((s, D_MODEL), F32), jax.ShapeDtypeStruct((MLA_WIDTH, s), BF16),
                   jax.ShapeDtypeStruct((s, MLA_WIDTH), F32), jax.ShapeDtypeStruct((MLA_HEADS, 1, s), F32),
                   jax.ShapeDtypeStruct((s, SSM_WIDTH), F32), jax.ShapeDtypeStruct((1, LANE), F32),
                   jax.ShapeDtypeStruct((1, D_MODEL), F32), jax.ShapeDtypeStruct((1, D_MODEL), F32),
                   jax.ShapeDtypeStruct((1, D_MODEL), F32)],
        compiler_params=_cparams("arbitrary"),
    )(o, z_attn, o_ssm, w_out, x, gate, ln_g, ln_b, tgt)


def _ssd_bwd(dos, y, z, xbc, dt, hin, a_neg, dskip_x, g_x, tril, triu, expand):
    s = xbc.shape[0]
    tm = min(SSD_ROWS, s)
    cpb = tm // CHUNK
    nb = s // tm
    gw = SSM_WIDTH // SSM_GROUPS
    hpg = SSM_HEADS // SSM_GROUPS

    def body(dos_ref, y_ref, z_ref, xbc_ref, dt_ref, hin_ref, a_ref, dsk_ref, g_ref, tril_ref, triu_ref, exp_ref,
             dxbc_ref, ddt_ref, dz_ref, dg_ref, ddsk_ref, da_ref, dh_sc, dy_sc):
        @pl.when(pl.program_id(0) == 0)
        def _():
            dh_sc[...] = jnp.zeros(dh_sc.shape, F32)

        yv, zv, dov = y_ref[...], z_ref[...], dos_ref[...]
        sz = _silu(zv)
        hf = yv * sz
        gv = g_ref[...]
        dgs, dhfs = [], []
        for grp in range(SSM_GROUPS):
            sl = slice(grp * gw, (grp + 1) * gw)
            n, rstd = _rms(hf[:, sl])
            dgs.append(_colsum(dov[:, sl] * n))
            dhfs.append(_rms_bwd(dov[:, sl] * gv[:, sl], n, rstd))
        dhf = jnp.concatenate(dhfs, axis=1)
        _acc_rows(dg_ref, jnp.concatenate(dgs, axis=1))
        dy_sc[...] = dhf * sz
        dz_ref[...] = dhf * yv * _silu_grad(zv)

        tril, triu, expand = tril_ref[...], triu_ref[...], exp_ref[...]
        ltri = tril > 0
        utri = triu > 0
        lane = _lane_iota((CHUNK, LANE))
        lane1 = _lane_iota((1, LANE))
        lane_lo = lane < SSM_P
        row_last = lax.broadcasted_iota(jnp.int32, (CHUNK, LANE), 0) == CHUNK - 1
        a_neg_v = a_ref[...]

        def chunk(ci, carry):
            dsk_acc, da_acc = carry
            cl = cpb - 1 - ci
            r0 = pl.multiple_of(cl * CHUNK, CHUNK)
            rows = pl.ds(r0, CHUNK)
            dtc = dt_ref[rows, :]
            lam_c, lam_r, lam_last = _chunk_common(dtc, a_neg_v, tril, triu)
            e_c = jnp.exp(lam_c)
            f_c = jnp.exp(lam_last - lam_c)
            cd = jnp.exp(lam_last)
            dt_x, e_x, f_x = _tri_right(dtc, expand), _tri_right(e_c, expand), _tri_right(f_c, expand)
            cd_x = _tri_right(jnp.broadcast_to(cd, (HALO, LANE)), expand)[0:1, :]
            dlam = jnp.zeros((CHUNK, LANE), F32)
            dlast = jnp.zeros((1, LANE), F32)
            ddt_x = jnp.zeros((CHUNK, LANE), F32)
            dsk_parts = []
            for grp in range(SSM_GROUPS):
                bo = SSM_WIDTH + grp * SSM_N
                co = SSM_WIDTH + SSM_GROUPS * SSM_N + grp * SSM_N
                bm = xbc_ref[rows, bo:bo + SSM_N]
                cm = xbc_ref[rows, co:co + SSM_N]
                bm_b, cm_b = bm.astype(BF16), cm.astype(BF16)
                gmat = _nt(cm_b, bm_b)
                gmat_t = _nt(bm_b, cm_b)
                ct_b = cm.T.astype(BF16)
                acc_dg = jnp.zeros((CHUNK, CHUNK), F32)
                acc_dgt = jnp.zeros((CHUNK, CHUNK), F32)
                d_b = jnp.zeros((CHUNK, SSM_N), F32)
                d_c = jnp.zeros((CHUNK, SSM_N), F32)
                for pj in range(hpg // 2):
                    ha = grp * hpg + 2 * pj
                    hb_ = ha + 1
                    lo = ha * SSM_P
                    blk = slice(lo, lo + LANE)
                    xs = xbc_ref[rows, blk]
                    dt2, e2, f2, cd2 = dt_x[:, blk], e_x[:, blk], f_x[:, blk], cd_x[:, blk]
                    x2 = xs * dt2
                    x2b = x2.astype(BF16)
                    dy2 = dy_sc[rows, blk]
                    dy2b = dy2.astype(BF16)
                    hp = hin_ref[cl, :, blk]
                    hp_b = hp.astype(BF16)
                    dhn = dh_sc[:, blk]
                    dhn_b = dhn.astype(BF16)
                    yo = e2 * _nn(cm_b, hp_b)
                    dzz_b = (e2 * dy2).astype(BF16)
                    d_c = d_c + _nt(dzz_b, hp_b)
                    dh_sc[:, blk] = _nn(ct_b, dzz_b) + cd2 * dhn
                    dxs2 = f2 * _nn(bm_b, dhn_b)
                    d_b = d_b + _nt((f2 * x2).astype(BF16), dhn_b)
                    xd = x2 * dxs2
                    t_lam = dy2 * yo - xd
                    t_last = cd2 * (dhn * hp) + xd
                    dxd2 = jnp.zeros((CHUNK, LANE), F32)
                    heads = ((ha, lane_lo), (hb_, jnp.logical_not(lane_lo)))
                    for hh, msk in heads:
                        x2h_b = jnp.where(msk, x2, 0.0).astype(BF16)
                        dy2h_b = jnp.where(msk, dy2, 0.0).astype(BF16)
                        seg = lam_c[:, hh:hh + 1] - lam_r[hh:hh + 1, :]
                        dec = jnp.exp(jnp.where(ltri, seg, -jnp.inf))
                        dect = jnp.exp(jnp.where(utri, -seg, -jnp.inf))
                        dmd = _nt(dy2h_b, x2b) * dec
                        dmtd = _nt(x2h_b, dy2b) * dect
                        acc_dg = acc_dg + dmd
                        acc_dgt = acc_dgt + dmtd
                        dlam_h = jnp.sum(dmd * gmat - dmtd * gmat_t + jnp.where(msk, t_lam, 0.0), axis=1, keepdims=True)
                        last_h = jnp.sum(jnp.sum(jnp.where(msk, t_last, 0.0), axis=0, keepdims=True), axis=1, keepdims=True)
                        dlam = jnp.where(lane == hh, dlam_h, dlam)
                        dlast = jnp.where(lane1 == hh, last_h, dlast)
                        dxd2 = jnp.where(msk, _nn((gmat_t * dect).astype(BF16), dy2b), dxd2)
                    dx2 = dxd2 + dxs2
                    dxbc_ref[rows, blk] = dx2 * dt2 + dy2 * dsk_ref[:, blk]
                    prod = dx2 * xs
                    for hh, msk in heads:
                        col = jnp.sum(jnp.where(msk, prod, 0.0), axis=1, keepdims=True)
                        ddt_x = jnp.where(lane == hh, col, ddt_x)
                    dsk_parts.append(_colsum(dy2 * xs))
                d_c = d_c + _nn(acc_dg.astype(BF16), bm_b)
                d_b = d_b + _nn(acc_dgt.astype(BF16), cm_b)
                dxbc_ref[rows, bo:bo + SSM_N] = d_b
                dxbc_ref[rows, co:co + SSM_N] = d_c
            dlam = dlam + jnp.where(row_last, dlast, 0.0)
            da = _tri_left(triu, dlam)
            ddt_ref[rows, :] = da * a_neg_v + ddt_x
            return dsk_acc + jnp.concatenate(dsk_parts, axis=1), da_acc + _colsum(da * dtc)

        dsk_tot, da_tot = lax.fori_loop(
            0, cpb, chunk, (jnp.zeros((1, SSM_WIDTH), F32), jnp.zeros((1, LANE), F32)))
        _acc_rows(ddsk_ref, dsk_tot)
        _acc_rows(da_ref, da_tot)

    rev = lambda i: (nb - 1 - i, 0)
    rrows = lambda w: pl.BlockSpec((tm, w), rev)
    return pl.pallas_call(
        body, name="ssd_bwd", grid=(nb,),
        in_specs=[rrows(SSM_WIDTH), rrows(SSM_WIDTH), rrows(SSM_WIDTH), rrows(CONV_CH), rrows(LANE),
                  pl.BlockSpec((cpb, SSM_N, SSM_WIDTH), lambda i: (nb - 1 - i, 0, 0)),
                  _whole((1, LANE)), _whole((1, SSM_WIDTH)), _whole((1, SSM_WIDTH)),
                  _whole((CHUNK, CHUNK)), _whole((CHUNK, CHUNK)), _whole((LANE, SSM_WIDTH))],
        out_specs=[rrows(CONV_CH), rrows(LANE), rrows(SSM_WIDTH),
                   _whole((1, SSM_WIDTH)), _whole((1, SSM_WIDTH)), _whole((1, LANE))],
        out_shape=[jax.ShapeDtypeStruct((s, CONV_CH), F32), jax.ShapeDtypeStruct((s, LANE), F32),
                   jax.ShapeDtypeStruct((s, SSM_WIDTH), F32), jax.ShapeDtypeStruct((1, SSM_WIDTH), F32),
                   jax.ShapeDtypeStruct((1, SSM_WIDTH), F32), jax.ShapeDtypeStruct((1, LANE), F32)],
        scratch_shapes=[pltpu.VMEM((SSM_N, SSM_WIDTH), F32), pltpu.VMEM((tm, SSM_WIDTH), F32)],
        compiler_params=_cparams("arbitrary"),
    )(dos, y, z, xbc, dt, hin, a_neg, dskip_x, g_x, tril, triu, expand)


def _ssd_post_bwd(xbc_raw, dxa, ddt, dt_raw, conv_w, conv_b, dt_bias_p):
    s = xbc_raw.shape[0]
    tm = ROW_TILE
    hb = tm // HALO
    nt = s // tm
    ext = tm + HALO

    def body(x_ref, prev_ref, next_ref, d_ref, dnext_ref, ddt_ref, dtr_ref, w_ref, b_ref, db_ref,
             dx_ref, ddtr_ref, dw_ref, dcb_ref, ddb_ref, xe_sc, de_sc):
        i = pl.program_id(0)
        w = w_ref[...]
        xe_sc[pl.ds(0, HALO), :] = jnp.where(i > 0, prev_ref[...], 0.0)
        xe_sc[pl.ds(HALO, tm), :] = x_ref[...]
        xe_sc[pl.ds(HALO + tm, HALO), :] = next_ref[...]
        pre = _conv_taps(xe_sc, w, ext, 0) + b_ref[...]
        sg = _silu_grad(pre)
        de_sc[pl.ds(0, tm), :] = d_ref[...] * sg[:tm]
        de_sc[pl.ds(tm, HALO), :] = jnp.where(i < nt - 1, dnext_ref[...] * sg[tm:], 0.0)
        dconv = de_sc[pl.ds(0, tm), :]
        acc = None
        dws = []
        for k in range(CONV_K):
            term = de_sc[pl.ds(CONV_K - 1 - k, tm), :] * w[k:k + 1, :]
            acc = term if acc is None else acc + term
            dws.append(_colsum(dconv * xe_sc[pl.ds(HALO - (CONV_K - 1) + k, tm), :]))
        dx_ref[...] = acc
        _acc_rows(dw_ref, jnp.concatenate(dws, axis=0))
        _acc_rows(dcb_ref, _colsum(dconv))
        ddtr = ddt_ref[...] * _sigmoid(dtr_ref[...] + db_ref[...])
        ddtr_ref[...] = ddtr
        _acc_rows(ddb_ref, _colsum(ddtr))

    halo_prev = pl.BlockSpec((HALO, CONV_CH), lambda i: (jnp.maximum(i * hb - 1, 0), 0))
    halo_next = pl.BlockSpec((HALO, CONV_CH), lambda i: (jnp.minimum((i + 1) * hb, s // HALO - 1), 0))
    return pl.pallas_call(
        body, name="ssd_post_bwd", grid=(nt,),
        in_specs=[_rows(tm, CONV_CH), halo_prev, halo_next, _rows(tm, CONV_CH), halo_next, _rows(tm, LANE),
                  _rows(tm, LANE), _whole((CONV_K, CONV_CH)), _whole((1, CONV_CH)), _whole((1, LANE))],
        out_specs=[_rows(tm, CONV_CH), _rows(tm, LANE), _whole((CONV_K, CONV_CH)), _whole((1, CONV_CH)),
                   _whole((1, LANE))],
        out_shape=[jax.ShapeDtypeStruct((s, CONV_CH), F32), jax.ShapeDtypeStruct((s, LANE), F32),
                   jax.ShapeDtypeStruct((CONV_K, CONV_CH), F32), jax.ShapeDtypeStruct((1, CONV_CH), F32),
                   jax.ShapeDtypeStruct((1, LANE), F32)],
        scratch_shapes=[pltpu.VMEM((tm + 2 * HALO, CONV_CH), F32), pltpu.VMEM((ext, CONV_CH), F32)],
        compiler_params=_cparams("arbitrary"),
    )(xbc_raw, xbc_raw, xbc_raw, dxa, dxa, ddt, dt_raw, conv_w, conv_b, dt_bias_p)


def _qbwd(dq_att, q_lat, g_q, w_qb_p, cos, sin):
    s = q_lat.shape[0]
    tm = ROW_TILE
    wq = MLA_HEADS * HEAD_PAD

    def body(dq_ref, ql_ref, g_ref, w_ref, cos_ref, sin_ref, dql_ref, draw_ref, dg_ref):
        c, sn = cos_ref[...], sin_ref[...]
        for h in range(MLA_HEADS):
            o = h * HEAD_PAD
            dqh = dq_ref[h].astype(F32) * ATTN_SCALE
            draw_ref[:, o:o + QK_NOPE] = dqh[:, :QK_NOPE].astype(BF16)
            draw_ref[:, o + QK_NOPE:o + HEAD_PAD] = _rope_transposed(dqh[:, QK_NOPE:], c, sn).astype(BF16)
        dn = _nt(draw_ref[...], w_ref[...])
        xhat, rstd = _rms(ql_ref[...])
        _acc_rows(dg_ref, _colsum(dn * xhat))
        dql_ref[...] = _rms_bwd(dn * g_ref[...], xhat, rstd)

    return pl.pallas_call(
        body, name="qbwd", grid=(s // tm,),
        in_specs=[pl.BlockSpec((MLA_HEADS, tm, HEAD_PAD), lambda i: (0, i, 0)), _rows(tm, Q_RANK), _whole((1, Q_RANK)),
                  _whole((Q_RANK, wq)), _rows(tm, LANE), _rows(tm, LANE)],
        out_specs=[_rows(tm, Q_RANK), _rows(tm, wq), _whole((1, Q_RANK))],
        out_shape=[jax.ShapeDtypeStruct((s, Q_RANK), F32), jax.ShapeDtypeStruct((s, wq), BF16),
                   jax.ShapeDtypeStruct((1, Q_RANK), F32)],
        compiler_params=_cparams("arbitrary"),
    )(dq_att, q_lat, g_q, w_qb_p, cos, sin)


def _kvbwd(dk_att, dv, kv_lat, g_kv, w_kvb_p, cos, sin):
    s = kv_lat.shape[0]
    tm = ROW_TILE
    wk = MLA_HEADS * HEAD_PAD
    wr = MLA_HEADS * (QK_NOPE + V_DIM)

    def body(dk_ref, dv_ref, kl_ref, g_ref, w_ref, cos_ref, sin_ref, dkl_ref, draw_ref, dg_ref):
        dkr = None
        for h in range(MLA_HEADS):
            o = h * HEAD_PAD
            draw_ref[:, h * QK_NOPE:(h + 1) * QK_NOPE] = dk_ref[:, o:o + QK_NOPE].astype(BF16)
            part = dk_ref[:, o + QK_NOPE:o + HEAD_PAD].astype(F32)
            dkr = part if dkr is None else dkr + part
        draw_ref[:, MLA_HEADS * QK_NOPE:] = dv_ref[...].astype(BF16)
        dn = _nt(draw_ref[...], w_ref[...])
        xhat, rstd = _rms(kl_ref[:, :KV_RANK])
        _acc_rows(dg_ref, _colsum(dn * xhat))
        dkl_ref[:, :KV_RANK] = _rms_bwd(dn * g_ref[...], xhat, rstd)
        dkl_ref[:, KV_RANK:] = _rope_transposed(dkr, cos_ref[...], sin_ref[...])

    return pl.pallas_call(
        body, name="kvbwd", grid=(s // tm,),
        in_specs=[_rows(tm, wk), _rows(tm, MLA_WIDTH), _rows(tm, KV_LAT_PAD), _whole((1, KV_RANK)),
                  _whole((KV_RANK, wr)), _rows(tm, LANE), _rows(tm, LANE)],
        out_specs=[_rows(tm, KV_LAT_PAD), _rows(tm, wr), _whole((1, KV_RANK))],
        out_shape=[jax.ShapeDtypeStruct((s, KV_LAT_PAD), F32), jax.ShapeDtypeStruct((s, wr), BF16),
                   jax.ShapeDtypeStruct((1, KV_RANK), F32)],
        compiler_params=_cparams("arbitrary"),
    )(dk_att, dv, kv_lat, g_kv, w_kvb_p, cos, sin)


def _inproj_bwd(pieces, w_in_pt, x, scale1p, gx1):
    s = x.shape[0]
    tm = min(ROW_TILE, s)

    def body(*refs):
        p_refs = refs[:len(IN_PAD)]
        w_ref, x_ref, sc_ref, gx1_ref, gx_ref, dp_ref, dsc_ref, dsh_ref = refs[len(IN_PAD):]
        off = 0
        for ref, w in zip(p_refs, IN_PAD):
            dp_ref[:, off:off + w] = ref[...].astype(BF16)
            off += w
        du = _nn(dp_ref[...], w_ref[...])
        gx_ref[...] = gx1_ref[...] + du * sc_ref[...]
        _acc_rows(dsc_ref, _colsum(du * x_ref[...]))
        _acc_rows(dsh_ref, _colsum(du))

    vec = _whole((1, D_MODEL))
    return pl.pallas_call(
        body, name="inproj_bwd", grid=(s // tm,),
        in_specs=[_rows(tm, w) for w in IN_PAD] + [_whole_once((IN_PAD_WIDTH, D_MODEL)), _rows(tm, D_MODEL), vec,
                                                    _rows(tm, D_MODEL)],
        out_specs=[_rows(tm, D_MODEL), _rows(tm, IN_PAD_WIDTH), vec, vec],
        out_shape=[jax.ShapeDtypeStruct((s, D_MODEL), F32), jax.ShapeDtypeStruct((s, IN_PAD_WIDTH), BF16),
                   jax.ShapeDtypeStruct((1, D_MODEL), F32), jax.ShapeDtypeStruct((1, D_MODEL), F32)],
        compiler_params=_cparams("arbitrary"),
    )(*pieces, w_in_pt, x, scale1p, gx1)


def _matmul_tn_rows(name, a, b, tk):
    s, k = a.shape
    n = b.shape[1]
    tm = min(GRAD_ROWS, s)

    def body(a_ref, b_ref, o_ref):
        @pl.when(pl.program_id(1) == 0)
        def _():
            o_ref[...] = jnp.zeros_like(o_ref)
        o_ref[...] += _tn(a_ref[...], b_ref[...])

    return pl.pallas_call(
        body, name=name, grid=(k // tk, s // tm),
        in_specs=[pl.BlockSpec((tm, tk), lambda j, i: (i, j)), pl.BlockSpec((tm, n), lambda j, i: (i, 0))],
        out_specs=pl.BlockSpec((tk, n), lambda j, i: (j, 0)),
        out_shape=jax.ShapeDtypeStruct((k, n), F32),
        compiler_params=_cparams("parallel", "arbitrary"),
    )(a, b)


def _matmul_tn(name, a, b, tn):
    s, k = a.shape
    n = b.shape[1]
    tm = min(GRAD_ROWS, s)

    def body(a_ref, b_ref, o_ref):
        @pl.when(pl.program_id(1) == 0)
        def _():
            o_ref[...] = jnp.zeros_like(o_ref)
        o_ref[...] += _tn(a_ref[...], b_ref[...])

    return pl.pallas_call(
        body, name=name, grid=(n // tn, s // tm),
        in_specs=[pl.BlockSpec((tm, k), lambda j, i: (i, 0)), pl.BlockSpec((tm, tn), lambda j, i: (i, j))],
        out_specs=pl.BlockSpec((k, tn), lambda j, i: (0, j)),
        out_shape=jax.ShapeDtypeStruct((k, n), F32),
        compiler_params=_cparams("parallel", "arbitrary"),
    )(a, b)


def _pack_w_in_t(w_in_t):
    parts, off = [], 0
    for w, wp in zip(IN_SPLITS, IN_PAD):
        parts.append(jnp.pad(w_in_t[off:off + w], ((0, wp - w), (0, 0))))
        off += w
    return jnp.concatenate(parts, axis=0)


def _unpack_w_in_t(g):
    parts, off = [], 0
    for w, wp in zip(IN_SPLITS, IN_PAD):
        parts.append(g[off:off + w])
        off += wp
    return jnp.concatenate(parts, axis=0)


def _rope_tables(positions):
    inv_freq = 1.0 / (ROPE_THETA ** (jnp.arange(ROPE_HALF, dtype=F32) / ROPE_HALF))
    ang = positions.astype(F32)[:, None] * inv_freq
    cos, sin = jnp.cos(ang), jnp.sin(ang)
    zeros = jnp.zeros((positions.shape[0], LANE - QK_ROPE), F32)
    return jnp.concatenate([cos, cos, zeros], axis=1), jnp.concatenate([-sin, sin, zeros], axis=1)


def _local_step(x, tgt, positions, mod, w_in_t, q_norm_g, w_qb_p, kv_norm_g, w_kvb_p, conv_w, conv_b, dt_bias,
                a_log, d_skip, ssm_norm_g, w_out_b, ln_g, ln_b):
    row = lambda v: v.reshape(1, -1)
    shift, scale, gate = mod[:D_MODEL], mod[D_MODEL:2 * D_MODEL], mod[2 * D_MODEL:]
    scale1p = row(1.0 + scale)
    w_in_p = _pack_w_in_t(w_in_t)
    cos, sin = _rope_tables(positions)
    a_neg = row(jnp.pad(-jnp.exp(a_log), (0, LANE - SSM_HEADS)))
    dskip_x = row(jnp.repeat(d_skip, SSM_P))
    dt_bias_p = row(jnp.pad(dt_bias, (0, LANE - SSM_HEADS)))
    tri = jnp.tril(jnp.ones((CHUNK, CHUNK), F32))
    tril, triu = tri.astype(BF16), tri.T.astype(BF16)

    u_bf, q_lat, kv_lat, z_attn, xbc_raw, dt_raw, z_ssm = _inproj(x, scale1p, row(shift), w_in_p)
    nq_bf, q_att = _qpath(q_lat, row(q_norm_g), w_qb_p, cos, sin)
    nkv_bf, k_att, v_att, vt_att = _kvpath(kv_lat, row(kv_norm_g), w_kvb_p, cos, sin)
    o, lse_rows = _attn_fwd(q_att, k_att, vt_att)
    xbc, dt = _ssd_pre(xbc_raw, dt_raw, conv_w, row(conv_b), dt_bias_p)
    expand = jnp.repeat(jnp.eye(LANE, SSM_HEADS, dtype=BF16), SSM_P, axis=1)
    y, o_ssm, hin = _ssd_fwd(xbc, dt, z_ssm, a_neg, dskip_x, row(ssm_norm_g), tril, triu)
    (cat_bf, dmix_bf, gx1, do_t, dz_attn, delta_rows, dos, loss, d_ln_g, d_ln_b, d_gate) = _outln(
        o, z_attn, o_ssm, w_out_b, x, row(gate), row(ln_g), row(ln_b), tgt)

    g_w_out = _matmul_tn("gw_out", cat_bf, dmix_bf, 512)
    dk_att, dv, dq_att = _attn_bwd(q_att, k_att, v_att, do_t, lse_rows, delta_rows)
    dq_lat, dqraw_bf, d_q_norm_g = _qbwd(dq_att, q_lat, row(q_norm_g), w_qb_p, cos, sin)
    dkv_lat, dkvraw_bf, d_kv_norm_g = _kvbwd(dk_att, dv, kv_lat, row(kv_norm_g), w_kvb_p, cos, sin)
    g_w_qb = _matmul_tn("gw_qb", nq_bf, dqraw_bf, MLA_HEADS * HEAD_PAD)
    g_w_kvb = _matmul_tn("gw_kvb", nkv_bf, dkvraw_bf, MLA_HEADS * (QK_NOPE + V_DIM))
    dxa, ddt, dz_ssm, d_ssm_g, ddsk_x, d_a = _ssd_bwd(dos, y, z_ssm, xbc, dt, hin, a_neg, dskip_x, row(ssm_norm_g),
                                                       tril, triu, expand)
    dxbc_raw, ddt_raw, d_conv_w, d_conv_b, d_dt_bias = _ssd_post_bwd(xbc_raw, dxa, ddt, dt_raw, conv_w, row(conv_b),
                                                                     dt_bias_p)
    grad_x, dproj_bf, d_scale, d_shift = _inproj_bwd((dq_lat, dkv_lat, dz_attn, dxbc_raw, ddt_raw, dz_ssm),
                                                     w_in_p, x, scale1p, gx1)
    g_w_in_t = _unpack_w_in_t(_matmul_tn_rows("gw_in", dproj_bf, u_bf, 896))
    return dict(
        loss=loss[0, 0], grad_x=grad_x,
        dmod=jnp.concatenate([d_shift[0], d_scale[0], d_gate[0]]),
        w_in_t=g_w_in_t, q_norm_g=d_q_norm_g[0], w_qb=g_w_qb, kv_norm_g=d_kv_norm_g[0], w_kvb=g_w_kvb,
        conv_w=d_conv_w, conv_b=d_conv_b[0], dt_bias=d_dt_bias[0, :SSM_HEADS],
        a_log=d_a[0, :SSM_HEADS] * a_neg[0, :SSM_HEADS],
        d_skip=ddsk_x.reshape(SSM_HEADS, SSM_P).sum(axis=1), ssm_norm_g=d_ssm_g[0], w_out=g_w_out,
        ln_g=d_ln_g[0], ln_b=d_ln_b[0])


ADAM_ROWS = 512


def _my_index():
    return 4 * lax.axis_index("x") + 2 * lax.axis_index("y") + lax.axis_index("c")


def _exchange(name, sends, gather):
    n = len(sends)
    peers = N_DEV - 1

    def body(*refs):
        send_refs, recv_refs = refs[:n], refs[n:2 * n]
        send_sems, recv_sems, local_sems = refs[2 * n:]
        x, y, c = lax.axis_index("x"), lax.axis_index("y"), lax.axis_index("c")
        me = 4 * x + 2 * y + c

        def src(a, idx):
            return send_refs[a] if gather else send_refs[a].at[idx]

        owns = [pltpu.make_async_copy(src(a, me), recv_refs[a].at[me], local_sems.at[a]) for a in range(n)]
        for cp in owns:
            cp.start()
        copies = []
        for k in range(1, N_DEV):
            px, py, pc = x ^ ((k >> 2) & 1), y ^ ((k >> 1) & 1), c ^ (k & 1)
            peer = 4 * px + 2 * py + pc
            for a in range(n):
                copies.append(pltpu.make_async_remote_copy(
                    src_ref=src(a, peer), dst_ref=recv_refs[a].at[me],
                    send_sem=send_sems.at[a * peers + k - 1], recv_sem=recv_sems.at[a * peers + k - 1],
                    device_id=(px, py, pc), device_id_type=pl.DeviceIdType.MESH))
        for cp in copies:
            cp.start()
        for cp in copies:
            cp.wait()
        for cp in owns:
            cp.wait()

    block_shape = lambda a: a.shape if gather else a.shape[1:]
    return pl.pallas_call(
        body, name=name,
        in_specs=[pl.BlockSpec(memory_space=pl.ANY)] * n, out_specs=[pl.BlockSpec(memory_space=pl.ANY)] * n,
        out_shape=[jax.ShapeDtypeStruct((N_DEV, *block_shape(a)), a.dtype) for a in sends],
        scratch_shapes=[pltpu.SemaphoreType.DMA((n * peers,)), pltpu.SemaphoreType.DMA((n * peers,)),
                        pltpu.SemaphoreType.DMA((n,))],
    )(*sends)


def _gather_two_level(name, sends):
    n = len(sends)
    per = N_DEV - 1

    def body(*refs):
        send_refs, recv_refs = refs[:n], refs[n:2 * n]
        send_sems, recv_sems, local_sems = refs[2 * n:]
        x, y, c = lax.axis_index("x"), lax.axis_index("y"), lax.axis_index("c")
        sibling = (x, y, 1 - c)
        chips = [(1 - x, y), (x, 1 - y), (1 - x, 1 - y)]

        def idx(px, py, pc):
            return 4 * px + 2 * py + pc

        def copy(a, k, block, to, src=None):
            slot = recv_refs[a].at[idx(*block)]
            return pltpu.make_async_remote_copy(
                src_ref=slot if src is None else src, dst_ref=slot,
                send_sem=send_sems.at[a * per + k], recv_sem=recv_sems.at[a * per + k],
                device_id=to, device_id_type=pl.DeviceIdType.MESH)

        me = (x, y, c)
        owns = [pltpu.make_async_copy(send_refs[a], recv_refs[a].at[idx(*me)], local_sems.at[a]) for a in range(n)]
        for cp in owns:
            cp.start()
        first = [copy(a, 0, me, sibling, src=send_refs[a]) for a in range(n)]
        first += [copy(a, 1 + j, me, (*chip, c), src=send_refs[a]) for j, chip in enumerate(chips) for a in range(n)]
        for cp in first:
            cp.start()
        passed = []
        for j, chip in enumerate(chips):
            for a in range(n):
                copy(a, 1 + j, (*chip, c), me).wait_recv()
                fwd = copy(a, 4 + j, (*chip, c), sibling)
                fwd.start()
                passed.append(fwd)
        for a in range(n):
            copy(a, 0, sibling, me).wait_recv()
            for j, chip in enumerate(chips):
                copy(a, 4 + j, (*chip, 1 - c), me).wait_recv()
        for cp in first + passed:
            cp.wait_send()
        for cp in owns:
            cp.wait()

    return pl.pallas_call(
        body, name=name,
        in_specs=[pl.BlockSpec(memory_space=pl.ANY)] * n, out_specs=[pl.BlockSpec(memory_space=pl.ANY)] * n,
        out_shape=[jax.ShapeDtypeStruct((N_DEV, *a.shape), a.dtype) for a in sends],
        scratch_shapes=[pltpu.SemaphoreType.DMA((n * per,)), pltpu.SemaphoreType.DMA((n * per,)),
                        pltpu.SemaphoreType.DMA((n,))],
    )(*sends)


def _flat_rows(parts, row_multiple):
    flat = jnp.concatenate([p.reshape(-1) for p in parts])
    chunk = row_multiple * LANE
    total = -(-flat.shape[0] // chunk) * chunk
    return jnp.pad(flat, (0, total - flat.shape[0])).reshape(-1, LANE)


def _unflat(flat, shapes):
    flat = flat.reshape(-1)
    out, off = [], 0
    for shp in shapes:
        n = math.prod(shp)
        out.append(flat[off:off + n].reshape(shp))
        off += n
    return out


def _adam_update(g, w, m, v):
    m2 = ADAM_B1 * m + (1.0 - ADAM_B1) * g
    v2 = ADAM_B2 * v + (1.0 - ADAM_B2) * (g * g)
    m_hat = m2 / (1.0 - ADAM_B1 ** ADAM_STEP)
    v_hat = v2 / (1.0 - ADAM_B2 ** ADAM_STEP)
    delta = -ADAM_LR * (m_hat / (jnp.sqrt(v_hat) + ADAM_EPS) + ADAM_WD * w)
    return delta, m2, v2


def _adamw_summed(name, parts, w, m, v):
    r = w.shape[0]
    tr = min(ADAM_ROWS, r)

    def body(p_ref, w_ref, m_ref, v_ref, g_ref, d_ref, m2_ref, v2_ref):
        g = p_ref[0]
        for j in range(1, N_DEV):
            g = g + p_ref[j]
        g_ref[...] = g
        d_ref[...], m2_ref[...], v2_ref[...] = _adam_update(g, w_ref[...], m_ref[...], v_ref[...])

    rows = _rows(tr, LANE)
    return pl.pallas_call(
        body, name=name, grid=(r // tr,),
        in_specs=[pl.BlockSpec((N_DEV, tr, LANE), lambda i: (0, i, 0)), rows, rows, rows],
        out_specs=[rows] * 4, out_shape=[jax.ShapeDtypeStruct((r, LANE), F32)] * 4,
        compiler_params=_cparams("parallel"),
    )(parts, w, m, v)


def _modpart(c_all, w_ada, b_cols):
    def body(c_ref, w_ref, b_ref, o_ref):
        o_ref[...] = _nn(c_ref[...].astype(BF16), w_ref[...].astype(BF16)) + b_ref[...]

    return pl.pallas_call(
        body, name="modpart", out_shape=jax.ShapeDtypeStruct((N_DEV, w_ada.shape[1]), F32),
    )(c_all, w_ada, b_cols)


def _adamw_w_ada(c_all_t, dmod_cols, w, m, v):
    def body(c_ref, d_ref, w_ref, m_ref, v_ref, g_ref, dl_ref, m2_ref, v2_ref):
        g = c_ref[:, 0:1] * d_ref[0:1, :]
        for b in range(1, N_DEV):
            g = g + c_ref[:, b:b + 1] * d_ref[b:b + 1, :]
        g_ref[...] = g
        dl_ref[...], m2_ref[...], v2_ref[...] = _adam_update(g, w_ref[...], m_ref[...], v_ref[...])

    return pl.pallas_call(
        body, name="adamw_w_ada", out_shape=[jax.ShapeDtypeStruct(w.shape, F32)] * 4,
        compiler_params=pltpu.CompilerParams(vmem_limit_bytes=VMEM_LIMIT),
    )(c_all_t, dmod_cols, w, m, v)


W_IN_SHARD = IN_WIDTH // N_DEV
W_IN_SHARD_LANES = -(-W_IN_SHARD // LANE) * LANE
BF16_ROWS = 16
W_IN_SEND_ROWS = -(-W_IN_SHARD // BF16_ROWS) * BF16_ROWS


def _transpose_cast(w_pad):
    def body(w_ref, o_ref):
        o_ref[...] = w_ref[...].T.astype(BF16)

    return pl.pallas_call(
        body, name="w_in_transpose", out_shape=jax.ShapeDtypeStruct(w_pad.shape[::-1], BF16),
        compiler_params=pltpu.CompilerParams(vmem_limit_bytes=VMEM_LIMIT),
    )(w_pad)


def _adamw_w_in(parts, w, m, v):
    rows_t = parts.shape[1]
    d, cols = w.shape
    tb = ROW_TILE

    def body(p_ref, w_ref, m_ref, v_ref, g_ref, d_ref, m2_ref, v2_ref):
        gt = p_ref[0].astype(F32)
        for j in range(1, N_DEV):
            gt = gt + p_ref[j].astype(F32)
        gt = jnp.concatenate([gt, jnp.zeros((W_IN_SHARD_LANES - rows_t, tb), F32)], axis=0)
        g = gt.T[:, :cols]
        g_ref[...] = g
        d_ref[...], m2_ref[...], v2_ref[...] = _adam_update(g, w_ref[...], m_ref[...], v_ref[...])

    blk = _rows(tb, cols)
    return pl.pallas_call(
        body, name="adamw_w_in", grid=(d // tb,),
        in_specs=[pl.BlockSpec((N_DEV, rows_t, tb), lambda i: (0, 0, i)), blk, blk, blk],
        out_specs=[blk] * 4, out_shape=[jax.ShapeDtypeStruct(w.shape, F32)] * 4,
        compiler_params=_cparams("parallel"),
    )(parts, w, m, v)


SHARDED = ("w_qb", "w_kvb", "w_out")
REPLICATED = ("b_ada", "q_norm_g", "kv_norm_g", "conv_b", "dt_bias", "a_log", "d_skip", "ssm_norm_g", "ln_g", "ln_b")
WEIGHTS = ("w_ada", "b_ada", "w_in", "q_norm_g", "w_qb", "kv_norm_g", "w_kvb", "conv_w", "conv_b", "dt_bias",
           "a_log", "d_skip", "ssm_norm_g", "w_out", "ln_g", "ln_b")
HEAD_COLS = QK_NOPE + V_DIM


def _adamw_blocks(name, parts, w, m, v):
    r, c = w.shape
    tr = ROW_TILE if r % ROW_TILE == 0 else r

    def body(p_ref, w_ref, m_ref, v_ref, g_ref, d_ref, m2_ref, v2_ref):
        g = p_ref[0].astype(F32)
        for j in range(1, N_DEV):
            g = g + p_ref[j].astype(F32)
        g_ref[...] = g
        d_ref[...], m2_ref[...], v2_ref[...] = _adam_update(g, w_ref[...], m_ref[...], v_ref[...])

    blk = _rows(tr, c)
    return pl.pallas_call(
        body, name=name, grid=(r // tr,),
        in_specs=[pl.BlockSpec((N_DEV, tr, c), lambda i: (0, i, 0)), blk, blk, blk],
        out_specs=[blk] * 4, out_shape=[jax.ShapeDtypeStruct(w.shape, F32)] * 4,
        compiler_params=_cparams("parallel"),
    )(parts, w, m, v)


def kernel(x, c, positions, w_ada, b_ada, w_in, q_norm_g, w_qb, kv_norm_g, w_kvb, conv_w, conv_b, dt_bias, a_log, d_skip, ssm_norm_g, w_out, ln_g, ln_b, loss_target, m_w_ada, m_b_ada, m_w_in, m_q_norm_g, m_w_qb, m_kv_norm_g, m_w_kvb, m_conv_w, m_conv_b, m_dt_bias, m_a_log, m_d_skip, m_ssm_norm_g, m_w_out, m_ln_g, m_ln_b, v_w_ada, v_b_ada, v_w_in, v_q_norm_g, v_w_qb, v_kv_norm_g, v_w_kvb, v_conv_w, v_conv_b, v_dt_bias, v_a_log, v_d_skip, v_ssm_norm_g, v_w_out, v_ln_g, v_ln_b):
    given = dict(w_ada=w_ada, b_ada=b_ada, w_in=w_in, q_norm_g=q_norm_g, w_qb=w_qb, kv_norm_g=kv_norm_g, w_kvb=w_kvb,
                 conv_w=conv_w, conv_b=conv_b, dt_bias=dt_bias, a_log=a_log, d_skip=d_skip, ssm_norm_g=ssm_norm_g,
                 w_out=w_out, ln_g=ln_g, ln_b=ln_b)
    mom = dict(w_ada=m_w_ada, b_ada=m_b_ada, w_in=m_w_in, q_norm_g=m_q_norm_g, w_qb=m_w_qb, kv_norm_g=m_kv_norm_g,
               w_kvb=m_w_kvb, conv_w=m_conv_w, conv_b=m_conv_b, dt_bias=m_dt_bias, a_log=m_a_log, d_skip=m_d_skip,
               ssm_norm_g=m_ssm_norm_g, w_out=m_w_out, ln_g=m_ln_g, ln_b=m_ln_b)
    var = dict(w_ada=v_w_ada, b_ada=v_b_ada, w_in=v_w_in, q_norm_g=v_q_norm_g, w_qb=v_w_qb, kv_norm_g=v_kv_norm_g,
               w_kvb=v_w_kvb, conv_w=v_conv_w, conv_b=v_conv_b, dt_bias=v_dt_bias, a_log=v_a_log, d_skip=v_d_skip,
               ssm_norm_g=v_ssm_norm_g, w_out=v_w_out, ln_g=v_ln_g, ln_b=v_ln_b)
    w0 = {k: a[0] for k, a in given.items()}
    m0 = {k: a[0] for k, a in mom.items()}
    v0 = {k: a[0] for k, a in var.items()}
    me = _my_index()

    w_in_rows = _transpose_cast(jnp.pad(w0["w_in"], ((0, 0), (0, W_IN_SHARD_LANES - W_IN_SHARD))))
    g_w_in, g_w_qb, g_w_kvb, g_w_out, g_conv_w, c_all = _gather_two_level(
        "gather_weights", [w_in_rows] + [w0[k].astype(BF16) for k in SHARDED] + [w0["conv_w"], c])
    c_all = c_all.reshape(N_DEV, D_MODEL)
    w_in_t = g_w_in[:, :W_IN_SHARD, :].reshape(IN_WIDTH, D_MODEL)
    w_qb_p = jnp.pad(g_w_qb, ((0, 0), (0, 0), (0, HEAD_PAD - QK_HEAD))).transpose(1, 0, 2).reshape(Q_RANK, -1)
    w_kvb_p = g_w_kvb.reshape(N_DEV, KV_RANK, 2, QK_NOPE).transpose(1, 2, 0, 3).reshape(KV_RANK, -1)
    w_out_b = g_w_out.reshape(MIX_WIDTH, D_MODEL)
    conv_w_full = g_conv_w.transpose(1, 0, 2).reshape(CONV_K, CONV_CH)

    ada_cols = w0["w_ada"].shape[1]
    b_cols = lax.dynamic_slice(w0["b_ada"], (me * ada_cols,), (ada_cols,)).reshape(1, ada_cols)
    mod_all, = _exchange("gather_mod", [_modpart(c_all, w0["w_ada"], b_cols)], gather=True)
    mod = lax.dynamic_index_in_dim(mod_all, me, axis=1, keepdims=False).reshape(-1)

    loc = _local_step(x[0], loss_target[0], positions[0], mod, w_in_t, w0["q_norm_g"], w_qb_p,
                      w0["kv_norm_g"], w_kvb_p, conv_w_full, w0["conv_b"], w0["dt_bias"], w0["a_log"],
                      w0["d_skip"], w0["ssm_norm_g"], w_out_b, w0["ln_g"], w0["ln_b"])

    rep_shapes = [w0[k].shape for k in REPLICATED] + [(1,)]
    rep_local = [loc["dmod"]] + [loc[k] for k in REPLICATED[1:]] + [loc["loss"].reshape(1)]
    rep_parts, conv_parts = _exchange("gather_small", [_flat_rows(rep_local, HALO), loc["conv_w"]], gather=True)
    conv_cols = w0["conv_w"].shape[1]
    conv_mine = lax.dynamic_slice(conv_parts, (0, 0, me * conv_cols), (N_DEV, CONV_K, conv_cols))
    outs = {"conv_w": _adamw_blocks("adamw_conv_w", conv_mine, w0["conv_w"], m0["conv_w"], v0["conv_w"])}
    zero1 = jnp.zeros((1,), F32)
    rep = _adamw_summed("adamw_replicated", rep_parts,
                        _flat_rows([w0[k] for k in REPLICATED] + [zero1], HALO),
                        _flat_rows([m0[k] for k in REPLICATED] + [zero1], HALO),
                        _flat_rows([v0[k] for k in REPLICATED] + [zero1], HALO))
    rep_g, rep_d, rep_m, rep_v = [_unflat(a, rep_shapes) for a in rep]
    loss = rep_g[-1][0]

    dmod_all = rep_parts.reshape(N_DEV, -1)[:, :3 * D_MODEL]
    dmod_cols = lax.dynamic_slice(dmod_all, (0, me * ada_cols), (N_DEV, ada_cols))
    outs["w_ada"] = _adamw_w_ada(c_all.T, dmod_cols, w0["w_ada"], m0["w_ada"], v0["w_ada"])

    send_w_in = loc["w_in_t"].astype(BF16).reshape(N_DEV, W_IN_SHARD, D_MODEL)
    send_w_in = jnp.pad(send_w_in, ((0, 0), (0, W_IN_SEND_ROWS - W_IN_SHARD), (0, 0)))
    send_w_qb = loc["w_qb"].astype(BF16).reshape(Q_RANK, N_DEV, HEAD_PAD)[:, :, :QK_HEAD].transpose(1, 0, 2)
    send_w_kvb = loc["w_kvb"].astype(BF16).reshape(KV_RANK, 2, N_DEV, QK_NOPE).transpose(2, 0, 1, 3)
    send_w_kvb = send_w_kvb.reshape(N_DEV, KV_RANK, HEAD_COLS)
    send_w_out = loc["w_out"].astype(BF16).reshape(N_DEV, MIX_WIDTH // N_DEV, D_MODEL)
    r_w_in, r_w_qb, r_w_kvb, r_w_out = _exchange(
        "scatter_grads", [send_w_in, send_w_qb, send_w_kvb, send_w_out], gather=False)
    outs["w_in"] = _adamw_w_in(r_w_in, w0["w_in"], m0["w_in"], v0["w_in"])
    for k, parts in zip(SHARDED, (r_w_qb, r_w_kvb, r_w_out)):
        outs[k] = _adamw_blocks("adamw_" + k, parts, w0[k], m0[k], v0[k])

    def collect(idx):
        out = {k: o[idx] for k, o in outs.items()}
        out.update({k: (rep_g, rep_d, rep_m, rep_v)[idx][i] for i, k in enumerate(REPLICATED)})
        return [out[k][None] for k in WEIGHTS]

    return (loss, loc["grad_x"][None], *collect(0), *collect(1), *collect(2), *collect(3))
```
